```python
import math
import jax
import jax.numpy as jnp
from jax import lax
import numpy as np

D_MODEL = 1024
BATCH = 4
SEQ = 8192
DEPTH = 2

HEAD_DIM = 64
BAND = 128
MASK_VALUE = -1e30

A_HEADS = 6
A_PATTERNS = ((128, 1), (512, 4), (2048, 16))

B_HEADS = 4
B_KEY_DIM = 128
B_VAL_DIM = 96
B_CHUNK = 64

C_HEADS = 4
C_KV_HEADS = 2
C_WINDOW = 128

A_WIDTH = A_HEADS * HEAD_DIM
B_KEY_WIDTH = B_HEADS * B_KEY_DIM
B_WIDTH = B_HEADS * B_VAL_DIM
C_WIDTH = C_HEADS * HEAD_DIM
C_KV_WIDTH = C_KV_HEADS * HEAD_DIM
MIX_WIDTH = A_WIDTH + B_WIDTH + C_WIDTH
IN_SPLITS = (A_WIDTH, A_WIDTH, A_WIDTH, B_KEY_WIDTH, B_KEY_WIDTH, B_WIDTH, B_WIDTH, C_WIDTH, C_KV_WIDTH, C_KV_WIDTH)
IN_WIDTH = sum(IN_SPLITS)

REL_BUCKETS = 32
REL_MAX_DIST = 2048
REL_HEADS = A_HEADS + C_HEADS

N_EXPERTS = 256
TOP_K = 8
N_GROUPS = 8
TOPK_GROUPS = 4
EXPERT_FF = 256
SHARED_FF = 256
ROUTED_SCALE = 2.5
MOE_BLOCK = 128

DEEPNORM_ALPHA = (2 * DEPTH) ** 0.25
DEEPNORM_BETA = (8 * DEPTH) ** -0.25
LN_EPS = 1e-5

kernel_name = "hybrid_dilated_hgrn2_sinkswa_moe_deepnorm"


def layer_norm(x, g, b):
    x = x.astype(jnp.float32)
    mu = jnp.mean(x, axis=-1, keepdims=True)
    xc = x - mu
    var = jnp.mean(xc * xc, axis=-1, keepdims=True)
    return xc * lax.rsqrt(var + LN_EPS) * g.astype(jnp.float32) + b.astype(jnp.float32)


def rel_bucket(dist):
    max_exact = REL_BUCKETS // 2
    d = jnp.maximum(dist, 0)
    log_ratio = jnp.log(jnp.maximum(d, max_exact).astype(jnp.float32) / max_exact) / math.log(REL_MAX_DIST / max_exact)
    large = jnp.minimum(max_exact + (log_ratio * (REL_BUCKETS - max_exact)).astype(jnp.int32), REL_BUCKETS - 1)
    return jnp.where(d < max_exact, d, large)


def band_offsets():
    return jnp.arange(BAND)[:, None] + BAND - jnp.arange(2 * BAND)[None, :]


def band_keys(t):
    n, h, length, d = t.shape
    tb = t.reshape(n, h, length // BAND, BAND, d)
    prev = jnp.pad(tb[:, :, :-1], ((0, 0), (0, 0), (1, 0), (0, 0), (0, 0)))
    return jnp.concatenate([prev, tb], axis=3)


def band_logits(q, k, bias, max_dist):
    n, h, length, d = q.shape
    nb = length // BAND
    qb = q.reshape(n, h, nb, BAND, d)
    s = jnp.einsum("nhbqd,nhbkd->nhbqk", qb, band_keys(k)) * (d ** -0.5) + bias[None, :, None]
    dist = band_offsets()
    in_band = (dist >= 0) & (dist <= max_dist)
    has_prev = (jnp.arange(nb)[:, None, None] > 0) | (jnp.arange(2 * BAND) >= BAND)[None, None, :]
    mask = in_band[None] & has_prev
    return jnp.where(mask[None, None], s, MASK_VALUE)


def to_residues(t, r, lp):
    b, h, s, d = t.shape
    t = t.reshape(b, h, s // r, r, d).transpose(0, 3, 1, 2, 4).reshape(b * r, h, s // r, d)
    return jnp.pad(t, ((0, 0), (0, 0), (0, lp - s // r), (0, 0)))


def from_residues(t, b, r, length):
    _, h, _, d = t.shape
    t = t[:, :, :length].reshape(b, r, h, length, d).transpose(0, 2, 3, 1, 4)
    return t.reshape(b, h, length * r, d)


def dilated_attention(q, k, v, rel_table):
    b, s, _ = q.shape
    heads = lambda t: t.reshape(b, s, A_HEADS, HEAD_DIM).transpose(0, 2, 1, 3)
    q, k, v = heads(q), heads(k), heads(v)
    dist_units = band_offsets()
    outs, lses = [], []
    for window, r in A_PATTERNS:
        length = s // r
        lp = -(-length // BAND) * BAND
        qr, kr, vr = to_residues(q, r, lp), to_residues(k, r, lp), to_residues(v, r, lp)
        bias = rel_table[rel_bucket(dist_units * r)][..., :A_HEADS].transpose(2, 0, 1)
        logits = band_logits(qr, kr, bias, window // r)
        m = jnp.max(logits, axis=-1, keepdims=True)
        p = jnp.exp(logits - m)
        den = jnp.sum(p, axis=-1, keepdims=True)
        o = jnp.einsum("nhbqk,nhbkd->nhbqd", p, band_keys(vr)) / den
        lse = m + jnp.log(den)
        outs.append(from_residues(o.reshape(b * r, A_HEADS, lp, HEAD_DIM), b, r, length))
        lses.append(from_residues(lse.reshape(b * r, A_HEADS, lp, 1), b, r, length))
    weights = jax.nn.softmax(jnp.stack(lses), axis=0)
    o = jnp.sum(weights * jnp.stack(outs), axis=0)
    return o.transpose(0, 2, 1, 3).reshape(b, s, A_WIDTH)


def hgrn2(q, f_logit, inp, gate, lower_bound, norm_g):
    b, s, _ = q.shape
    nc = s // B_CHUNK
    forget = lower_bound + (1.0 - lower_bound) * jax.nn.sigmoid(f_logit)
    log_f = jnp.log(forget)
    key = (1.0 - lower_bound) * jax.nn.sigmoid(-f_logit)
    chunks = lambda t, dim: t.reshape(b, nc, B_CHUNK, B_HEADS, dim).transpose(1, 0, 3, 2, 4)
    qc = chunks(jax.nn.silu(q), B_KEY_DIM)
    kc = chunks(key, B_KEY_DIM)
    gc = chunks(log_f, B_KEY_DIM)
    ic = chunks(inp, B_VAL_DIM)
    causal = jnp.tril(jnp.ones((B_CHUNK, B_CHUNK), dtype=bool))[:, :, None]

    def step(state, xs):
        qq, kk, ii, gg = xs
        cum = jnp.cumsum(gg, axis=2)
        inter = jnp.einsum("bhtk,bhkv->bhtv", qq * jnp.exp(cum), state)
        diff = cum[:, :, :, None, :] - cum[:, :, None, :, :]
        decay = jnp.where(causal, jnp.exp(jnp.where(causal, diff, 0.0)), 0.0)
        scores = jnp.einsum("bhtk,bhsk,bhtsk->bhts", qq, kk, decay)
        intra = jnp.einsum("bhts,bhsv->bhtv", scores, ii)
        last = cum[:, :, -1]
        k_dec = kk * jnp.exp(last[:, :, None] - cum)
        new_state = jnp.exp(last)[..., None] * state + jnp.einsum("bhsk,bhsv->bhkv", k_dec, ii)
        return new_state, inter + intra

    state0 = jnp.zeros((b, B_HEADS, B_KEY_DIM, B_VAL_DIM), jnp.float32)
    _, o = lax.scan(step, state0, (qc, kc, ic, gc))
    o = o.transpose(1, 0, 3, 2, 4).reshape(b, s, B_HEADS, B_VAL_DIM)
    o = o * lax.rsqrt(jnp.mean(o * o, axis=-1, keepdims=True) + 1e-6)
    return o.reshape(b, s, B_WIDTH) * norm_g * jax.nn.silu(gate)


def sink_window_attention(q, k, v, rel_table, sinks):
    b, s, _ = q.shape
    rep = C_HEADS // C_KV_HEADS
    q = q.reshape(b, s, C_HEADS, HEAD_DIM).transpose(0, 2, 1, 3)
    k = jnp.repeat(k.reshape(b, s, C_KV_HEADS, HEAD_DIM).transpose(0, 2, 1, 3), rep, axis=1)
    v = jnp.repeat(v.reshape(b, s, C_KV_HEADS, HEAD_DIM).transpose(0, 2, 1, 3), rep, axis=1)
    bias = rel_table[rel_bucket(band_offsets())][..., A_HEADS:].transpose(2, 0, 1)
    logits = band_logits(q, k, bias, C_WINDOW - 1)
    sink = jnp.broadcast_to(sinks[None, :, None, None, None], logits.shape[:-1] + (1,))
    p = jax.nn.softmax(jnp.concatenate([logits, sink], axis=-1), axis=-1)[..., :-1]
    o = jnp.einsum("nhbqk,nhbkd->nhbqd", p, band_keys(v))
    return o.reshape(b, C_HEADS, s, HEAD_DIM).transpose(0, 2, 1, 3).reshape(b, s, C_WIDTH)


def mixer_sublayer(x, w_in, w_out, rel_table, lower_bound, b_norm_g, c_sinks):
    h = jnp.einsum("bsd,de->bse", x, w_in).astype(jnp.float32)
    split_at = [int(i) for i in np.cumsum(IN_SPLITS)[:-1]]
    aq, ak, av, bq, bf, bi, bg, cq, ck, cv = jnp.split(h, split_at, axis=-1)
    oa = dilated_attention(aq, ak, av, rel_table)
    ob = hgrn2(bq, bf, bi, bg, lower_bound, b_norm_g.astype(jnp.float32))
    oc = sink_window_attention(cq, ck, cv, rel_table, c_sinks.astype(jnp.float32))
    o = jnp.concatenate([oa, ob, oc], axis=-1)
    return jnp.einsum("bse,ed->bsd", o, w_out.astype(jnp.float32))


def moe_sublayer(x, router_w, router_bias, w_gate, w_up, w_down, sh_gate, sh_up, sh_down):
    b, s, d = x.shape
    t = b * s
    xt = x.reshape(t, d)
    scores = jax.nn.sigmoid(jnp.einsum("td,de->te", xt, router_w).astype(jnp.float32))
    choice = scores + router_bias.astype(jnp.float32)
    per_group = N_EXPERTS // N_GROUPS
    group_score = jnp.sum(lax.top_k(choice.reshape(t, N_GROUPS, per_group), 2)[0], axis=-1)
    top_groups = lax.top_k(group_score, TOPK_GROUPS)[1]
    group_ok = jnp.any(top_groups[:, :, None] == jnp.arange(N_GROUPS), axis=1)
    choice = jnp.where(jnp.repeat(group_ok, per_group, axis=1), choice, MASK_VALUE)
    top_e = lax.top_k(choice, TOP_K)[1]
    gate = jnp.take_along_axis(scores, top_e, axis=1)
    gate = gate / jnp.sum(gate, axis=-1, keepdims=True) * ROUTED_SCALE

    n = t * TOP_K
    flat_e = top_e.reshape(n)
    order = jnp.argsort(flat_e)
    sorted_e = flat_e[order]
    counts = jnp.bincount(flat_e, length=N_EXPERTS)
    padded = (counts + MOE_BLOCK - 1) // MOE_BLOCK * MOE_BLOCK
    pad_end = jnp.cumsum(padded)
    rank = jnp.arange(n) - (jnp.cumsum(counts) - counts)[sorted_e]
    dest = pad_end[sorted_e] - padded[sorted_e] + rank
    n_blocks = -(-(n + N_EXPERTS * (MOE_BLOCK - 1)) // MOE_BLOCK)
    rows = n_blocks * MOE_BLOCK
    row_tok = jnp.zeros((rows,), jnp.int32).at[dest].set((order // TOP_K).astype(jnp.int32))
    row_gate = jnp.zeros((rows,), jnp.float32).at[dest].set(gate.reshape(n)[order])
    block_e = jnp.minimum(jnp.searchsorted(pad_end, jnp.arange(n_blocks) * MOE_BLOCK, side="right"), N_EXPERTS - 1)

    def expert_block(args):
        tok, g, e = args
        xb = xt[tok]
        hb = jax.nn.silu(xb @ w_gate[e]) * (xb @ w_up[e])
        return (hb @ w_down[e]).astype(jnp.float32) * g[:, None]

    y = lax.map(expert_block, (row_tok.reshape(n_blocks, MOE_BLOCK), row_gate.reshape(n_blocks, MOE_BLOCK), block_e))
    routed = jax.ops.segment_sum(y.reshape(rows, d), row_tok, num_segments=t)
    shared = (jax.nn.silu(xt @ sh_gate) * (xt @ sh_up)) @ sh_down
    return (routed + shared.astype(jnp.float32)).reshape(b, s, d)


def setup_inputs(seed: int = 0) -> dict:
    key = jax.random.key(seed)
    ks = jax.random.split(key, 20)
    nrm = lambda k, shape, scale: jax.random.normal(k, shape, jnp.float32) * scale
    L = DEPTH
    return {
        "x": nrm(ks[0], (BATCH, SEQ, D_MODEL), 1.0),
        "w_in": nrm(ks[1], (L, D_MODEL, IN_WIDTH), D_MODEL ** -0.5),
        "w_out": nrm(ks[2], (L, MIX_WIDTH, D_MODEL), MIX_WIDTH ** -0.5 * DEEPNORM_BETA),
        "rel_bias_table": nrm(ks[3], (REL_BUCKETS, REL_HEADS), 0.5),
        "lower_bound_logits": nrm(ks[4], (L, B_KEY_WIDTH), 1.0),
        "hgrn_norm_g": 1.0 + nrm(ks[5], (L, B_WIDTH), 0.02),
        "attn_sinks": nrm(ks[6], (L, C_HEADS), 0.5),
        "ln1_g": 1.0 + nrm(ks[7], (L, D_MODEL), 0.02),
        "ln1_b": nrm(ks[8], (L, D_MODEL), 0.02),
        "router_w": nrm(ks[9], (L, D_MODEL, N_EXPERTS), D_MODEL ** -0.5),
        "router_bias": nrm(ks[10], (L, N_EXPERTS), 0.01),
        "expert_w_gate": nrm(ks[11], (L, N_EXPERTS, D_MODEL, EXPERT_FF), D_MODEL ** -0.5),
        "expert_w_up": nrm(ks[12], (L, N_EXPERTS, D_MODEL, EXPERT_FF), D_MODEL ** -0.5),
        "expert_w_down": nrm(ks[13], (L, N_EXPERTS, EXPERT_FF, D_MODEL), EXPERT_FF ** -0.5 * DEEPNORM_BETA),
        "shared_w_gate": nrm(ks[14], (L, D_MODEL, SHARED_FF), D_MODEL ** -0.5),
        "shared_w_up": nrm(ks[15], (L, D_MODEL, SHARED_FF), D_MODEL ** -0.5),
        "shared_w_down": nrm(ks[16], (L, SHARED_FF, D_MODEL), SHARED_FF ** -0.5 * DEEPNORM_BETA),
        "ln2_g": 1.0 + nrm(ks[17], (L, D_MODEL), 0.02),
        "ln2_b": nrm(ks[18], (L, D_MODEL), 0.02),
    }


def reference(x, w_in, w_out, rel_bias_table, lower_bound_logits, hgrn_norm_g, attn_sinks, ln1_g, ln1_b,
              router_w, router_bias, expert_w_gate, expert_w_up, expert_w_down,
              shared_w_gate, shared_w_up, shared_w_down, ln2_g, ln2_b):
    lb_probs = jax.nn.softmax(lower_bound_logits.astype(jnp.float32), axis=0)
    lower_bounds = jnp.cumsum(lb_probs, axis=0) - lb_probs[0]
    rel_table = rel_bias_table.astype(jnp.float32)
    h = x.astype(jnp.float32)
    for l in range(DEPTH):
        mix = mixer_sublayer(h, w_in[l], w_out[l], rel_table, lower_bounds[l], hgrn_norm_g[l], attn_sinks[l])
        h = layer_norm(DEEPNORM_ALPHA * h + mix, ln1_g[l], ln1_b[l])
        ffn = moe_sublayer(h, router_w[l], router_bias[l], expert_w_gate[l], expert_w_up[l], expert_w_down[l],
                           shared_w_gate[l], shared_w_up[l], shared_w_down[l])
        h = layer_norm(DEEPNORM_ALPHA * h + ffn, ln2_g[l], ln2_b[l])
    return h.astype(x.dtype)
```

```python
import functools
import math

import numpy as np
import jax
import jax.numpy as jnp
from jax import lax
from jax.experimental import pallas as pl
from jax.experimental.pallas import tpu as pltpu

F32 = jnp.float32
BF16 = jnp.bfloat16
I32 = jnp.int32

LANES = 128
SUBLANES = 8
VMEM_LIMIT = 56 * 1024 * 1024

D_MODEL = 1024
DEPTH = 2
HEAD_DIM = 64
BAND = 128
MASK_VALUE = -1e30

A_HEADS = 6
A_PATTERNS = ((128, 1), (512, 4), (2048, 16))
B_HEADS = 4
B_KEY_DIM = 128
B_VAL_DIM = 96
B_VAL_PAD = 128
HG_CHUNK = 64
C_HEADS = 4
C_KV_HEADS = 2
C_WINDOW = 128

A_WIDTH = A_HEADS * HEAD_DIM
B_KEY_WIDTH = B_HEADS * B_KEY_DIM
B_WIDTH = B_HEADS * B_VAL_DIM
B_PAD_WIDTH = B_HEADS * B_VAL_PAD
C_WIDTH = C_HEADS * HEAD_DIM
C_KV_WIDTH = C_KV_HEADS * HEAD_DIM
IN_SPLITS = (A_WIDTH, A_WIDTH, A_WIDTH, B_KEY_WIDTH, B_KEY_WIDTH, B_WIDTH, B_WIDTH, C_WIDTH, C_KV_WIDTH, C_KV_WIDTH)

A_COLS = 3 * A_WIDTH
B_COLS = 4 * B_KEY_WIDTH
C_COLS = 3 * C_WIDTH
MIX_PAD_WIDTH = A_WIDTH + B_PAD_WIDTH + C_WIDTH

REL_BUCKETS = 32
REL_MAX_DIST = 2048

N_EXPERTS = 256
TOP_K = 8
N_GROUPS = 8
TOPK_GROUPS = 4
EXPERT_FF = 256
SHARED_FF = 256
ROUTED_SCALE = 2.5
EXPERT_BLOCK = 256

DEEPNORM_ALPHA = (2 * DEPTH) ** 0.25
LN_EPS = 1e-5


def _cparams(*sem):
    return pltpu.CompilerParams(dimension_semantics=sem, vmem_limit_bytes=VMEM_LIMIT)


def _prep_w_in(w_in):
    d = w_in.shape[0]
    split_at = [int(i) for i in np.cumsum(IN_SPLITS)[:-1]]
    aq, ak, av, bq, bf, bi, bg, cq, ck, cv = jnp.split(w_in, split_at, axis=-1)
    pad_v = lambda w: jnp.pad(w.reshape(d, B_HEADS, B_VAL_DIM), ((0, 0), (0, 0), (0, B_VAL_PAD - B_VAL_DIM))).reshape(d, B_PAD_WIDTH)
    rep = lambda w: jnp.repeat(w.reshape(d, C_KV_HEADS, HEAD_DIM), C_HEADS // C_KV_HEADS, axis=1).reshape(d, C_WIDTH)
    cols = [aq, ak, av, bq, bf, pad_v(bi), pad_v(bg), cq, rep(ck), rep(cv)]
    return jnp.concatenate(cols, axis=-1).astype(BF16)


def _prep_w_out(w_out):
    d = w_out.shape[1]
    wa = w_out[:A_WIDTH]
    wb = w_out[A_WIDTH:A_WIDTH + B_WIDTH].reshape(B_HEADS, B_VAL_DIM, d)
    wb = jnp.pad(wb, ((0, 0), (0, B_VAL_PAD - B_VAL_DIM), (0, 0))).reshape(B_PAD_WIDTH, d)
    wc = w_out[A_WIDTH + B_WIDTH:]
    return jnp.concatenate([wa, wb, wc], axis=0).astype(BF16)


def _pad_heads_vec(v):
    return jnp.pad(v.reshape(B_HEADS, B_VAL_DIM), ((0, 0), (0, B_VAL_PAD - B_VAL_DIM))).reshape(1, B_PAD_WIDTH)


def _rel_bucket(dist):
    max_exact = REL_BUCKETS // 2
    d = jnp.maximum(dist, 0)
    log_ratio = jnp.log(jnp.maximum(d, max_exact).astype(F32) / max_exact) / math.log(REL_MAX_DIST / max_exact)
    large = jnp.minimum(max_exact + (log_ratio * (REL_BUCKETS - max_exact)).astype(I32), REL_BUCKETS - 1)
    return jnp.where(d < max_exact, d, large)


def _band_bias(rel_table, r, head_lo, head_hi):
    dist = jnp.arange(BAND)[:, None] + BAND - jnp.arange(2 * BAND)[None, :]
    return rel_table[_rel_bucket(dist * r)][..., head_lo:head_hi].transpose(2, 0, 1)


def _hgrn_sum_matrix():
    c = HG_CHUNK
    t = np.arange(c)[:, None]
    u = np.arange(c)[None, :]
    blocks = [(u <= t), (u > t)]
    m = c
    while m >= 2:
        mid = (t // m) * m + m // 2
        second = t >= mid
        blocks.append(np.where(second, (u >= mid) & (u <= t), (u > t) & (u < mid)))
        m //= 2
    return np.concatenate(blocks, axis=0).astype(np.float32)


HG_LEVELS = int(math.log2(HG_CHUNK))


def _in_proj_body(x_ref, w_ref, a_ref, b_ref, c_ref):
    xb = x_ref[...].astype(BF16)
    a_ref[...] = jnp.dot(xb, w_ref[:, :A_COLS], preferred_element_type=F32).astype(BF16)
    for j in range(B_COLS // B_KEY_WIDTH):
        lo = A_COLS + j * B_KEY_WIDTH
        b_ref[:, j * B_KEY_WIDTH:(j + 1) * B_KEY_WIDTH] = jnp.dot(
            xb, w_ref[:, lo:lo + B_KEY_WIDTH], preferred_element_type=F32)
    c_ref[...] = jnp.dot(xb, w_ref[:, A_COLS + B_COLS:], preferred_element_type=F32).astype(BF16)


def _in_proj(x2d, w_p):
    t, d = x2d.shape
    tm = 512
    n = w_p.shape[1]
    return pl.pallas_call(
        _in_proj_body,
        grid=(t // tm,),
        in_specs=[pl.BlockSpec((tm, d), lambda i: (i, 0)),
                  pl.BlockSpec((d, n), lambda i: (0, 0))],
        out_specs=[pl.BlockSpec((tm, A_COLS), lambda i: (i, 0)),
                   pl.BlockSpec((tm, B_COLS), lambda i: (i, 0)),
                   pl.BlockSpec((tm, C_COLS), lambda i: (i, 0))],
        out_shape=[jax.ShapeDtypeStruct((t, A_COLS), BF16),
                   jax.ShapeDtypeStruct((t, B_COLS), F32),
                   jax.ShapeDtypeStruct((t, C_COLS), BF16)],
        compiler_params=_cparams("arbitrary"),
        name="in_proj",
    )(x2d, w_p)


def _band_attn_body(*refs, width, max_dist, has_sink, want_lse):
    q_ref, kp_ref, kc_ref, vp_ref, vc_ref, bias_ref = refs[:6]
    rest = refs[6:]
    if has_sink:
        sink_ref, rest = rest[0], rest[1:]
    o_ref = rest[0]
    lse_ref = rest[1] if want_lse else None

    blk = pl.program_id(2)
    row = lax.broadcasted_iota(I32, (BAND, 2 * BAND), 0)
    col = lax.broadcasted_iota(I32, (BAND, 2 * BAND), 1)
    dist = row + BAND - col
    mask = (dist >= 0) & (dist <= max_dist) & ((col >= BAND) | (blk > 0))
    low_half = lax.broadcasted_iota(I32, (BAND, LANES), 1) < HEAD_DIM
    scale = HEAD_DIM ** -0.5

    for tile in range(width // LANES):
        sl = slice(tile * LANES, (tile + 1) * LANES)
        q2 = q_ref[0, :, sl]
        k2 = jnp.concatenate([kp_ref[0, :, sl], kc_ref[0, :, sl]], axis=0)
        v2 = jnp.concatenate([vp_ref[0, :, sl], vc_ref[0, :, sl]], axis=0)
        outs, lses = [], []
        for half in range(2):
            h = 2 * tile + half
            qm = jnp.where(low_half if half == 0 else jnp.logical_not(low_half), q2, jnp.zeros_like(q2))
            s = lax.dot_general(qm, k2, (((1,), (1,)), ((), ())), preferred_element_type=F32)
            s = s * scale + bias_ref[h]
            s = jnp.where(mask, s, MASK_VALUE)
            m = jnp.max(s, axis=-1, keepdims=True)
            if has_sink:
                sink = sink_ref[h]
                m = jnp.maximum(m, sink)
            p = jnp.exp(s - m)
            den = jnp.sum(p, axis=-1, keepdims=True)
            if has_sink:
                den = den + jnp.exp(sink - m)
            pv = jnp.dot(p.astype(BF16), v2, preferred_element_type=F32)
            outs.append(pv / den)
            if want_lse:
                lses.append(jnp.broadcast_to(m + jnp.log(den), (BAND, LANES)))
        o_ref[0, :, sl] = jnp.where(low_half, outs[0], outs[1])
        if want_lse:
            lse_ref[0, :, sl] = jnp.where(low_half, lses[0], lses[1])


def _band_attn(src, bias, *, r, width, src_blocks, max_dist, sinks=None, want_lse):
    nb, length, _ = src.shape
    nblk = length // BAND
    heads = width // HEAD_DIM
    has_sink = sinks is not None

    def cur(off):
        return pl.BlockSpec((1, BAND, width), lambda b, p, i: (b, i, p * src_blocks + off))

    def prev(off):
        return pl.BlockSpec((1, BAND, width), lambda b, p, i: (b, jnp.maximum(i - 1, 0), p * src_blocks + off))

    in_specs = [cur(0), prev(1), cur(1), prev(2), cur(2),
                pl.BlockSpec((heads, BAND, 2 * BAND), lambda b, p, i: (0, 0, 0))]
    args = [src, src, src, src, src, bias]
    if has_sink:
        in_specs.append(pl.BlockSpec(memory_space=pltpu.SMEM))
        args.append(sinks)
    out_spec = pl.BlockSpec((1, BAND, width), lambda b, p, i: (b, i, p))
    out_sds = jax.ShapeDtypeStruct((nb, length, r * width), F32)
    body = functools.partial(_band_attn_body, width=width, max_dist=max_dist, has_sink=has_sink, want_lse=want_lse)
    return pl.pallas_call(
        body,
        grid=(nb, r, nblk),
        in_specs=in_specs,
        out_specs=[out_spec, out_spec] if want_lse else out_spec,
        out_shape=[out_sds, out_sds] if want_lse else out_sds,
        compiler_params=_cparams("arbitrary", "arbitrary", "arbitrary"),
        name="band_attn_r%d_w%d" % (r, width),
    )(*args)


def _hgrn_body(b_ref, lb_ref, ng_ref, w_ref, o_ref, state_ref):
    c = HG_CHUNK
    kd = B_KEY_DIM

    @pl.when(pl.program_id(1) == 0)
    def _():
        state_ref[...] = jnp.zeros_like(state_ref)

    trow = lax.broadcasted_iota(I32, (c, 1), 0)
    ti = lax.broadcasted_iota(I32, (c, c), 0)
    si = lax.broadcasted_iota(I32, (c, c), 1)
    nt = (((1,), (1,)), ((), ()))
    w_all = w_ref[...]

    for h in range(B_HEADS):
        ks = slice(h * kd, (h + 1) * kd)
        q = b_ref[0, :, ks]
        f = b_ref[0, :, B_KEY_WIDTH + h * kd:B_KEY_WIDTH + (h + 1) * kd]
        inp = b_ref[0, :, 2 * B_KEY_WIDTH + h * kd:2 * B_KEY_WIDTH + (h + 1) * kd]
        gate = b_ref[0, :, 3 * B_KEY_WIDTH + h * kd:3 * B_KEY_WIDTH + (h + 1) * kd]
        lb = lb_ref[:, ks]

        log_f = jnp.log(lb + (1.0 - lb) * jax.nn.sigmoid(f))
        key = (1.0 - lb) * jax.nn.sigmoid(-f)
        qs = q * jax.nn.sigmoid(q)

        g_hi = log_f.astype(BF16)
        g_lo = (log_f - g_hi.astype(F32)).astype(BF16)
        e2 = jnp.dot(w_all, jnp.concatenate([g_hi, g_lo], axis=1), preferred_element_type=F32)
        e = e2[:, :kd] + e2[:, kd:]

        cum = e[0:c]
        q_dec = (qs * jnp.exp(cum)).astype(BF16)
        k_dec = (key * jnp.exp(e[c:2 * c])).astype(BF16)
        inp_b = inp.astype(BF16)

        st = state_ref[h]
        inter = lax.dot_general(q_dec, st.astype(BF16), nt, preferred_element_type=F32)

        scores = jnp.where(ti == si,
                           lax.dot_general(qs.astype(BF16), key.astype(BF16), nt, preferred_element_type=F32), 0.0)
        m = c
        for lvl in range(HG_LEVELS):
            el = jnp.exp(e[(2 + lvl) * c:(3 + lvl) * c])
            second = (trow % m) >= (m // 2)
            ql = jnp.where(second, qs * el, 0.0).astype(BF16)
            kl = jnp.where(second, 0.0, key * el).astype(BF16)
            sl = lax.dot_general(ql, kl, nt, preferred_element_type=F32)
            if m < c:
                sl = jnp.where((ti // m) == (si // m), sl, 0.0)
            scores = scores + sl
            m //= 2
        intra = jnp.dot(scores.astype(BF16), inp_b, preferred_element_type=F32)

        new_st = st * jnp.exp(cum[c - 1:c]) + lax.dot_general(
            inp_b, k_dec, (((0,), (0,)), ((), ())), preferred_element_type=F32)
        state_ref[h] = new_st

        o = inter + intra
        ms = jnp.sum(o * o, axis=-1, keepdims=True) * (1.0 / B_VAL_DIM)
        o = o * lax.rsqrt(ms + 1e-6)
        o_ref[0, :, ks] = o * ng_ref[:, ks] * (gate * jax.nn.sigmoid(gate))


def _hgrn(b_all, lower_bound, norm_g_pad, w_sum):
    nb, s, _ = b_all.shape
    c = HG_CHUNK
    return pl.pallas_call(
        _hgrn_body,
        grid=(nb, s // c),
        in_specs=[pl.BlockSpec((1, c, B_COLS), lambda b, i: (b, i, 0)),
                  pl.BlockSpec((1, B_KEY_WIDTH), lambda b, i: (0, 0)),
                  pl.BlockSpec((1, B_PAD_WIDTH), lambda b, i: (0, 0)),
                  pl.BlockSpec(w_sum.shape, lambda b, i: (0, 0))],
        out_specs=pl.BlockSpec((1, c, B_PAD_WIDTH), lambda b, i: (b, i, 0)),
        out_shape=jax.ShapeDtypeStruct((nb, s, B_PAD_WIDTH), F32),
        scratch_shapes=[pltpu.VMEM((B_HEADS, B_VAL_PAD, B_KEY_DIM), F32)],
        compiler_params=_cparams("arbitrary", "arbitrary"),
        name="hgrn2",
    )(b_all, lower_bound, norm_g_pad, w_sum)


def _layer_norm_rows(z, g, b):
    mu = jnp.mean(z, axis=-1, keepdims=True)
    zc = z - mu
    var = jnp.mean(zc * zc, axis=-1, keepdims=True)
    return zc * lax.rsqrt(var + LN_EPS) * g + b


def _mix_out_body(o1, o2, o3, l1, l2, l3, ob, oc, x_ref, w_ref, g_ref, b_ref, out_ref):
    la, lb_, lc = l1[...], l2[...], l3[...]
    m = jnp.maximum(jnp.maximum(la, lb_), lc)
    wa, wb, wc = jnp.exp(la - m), jnp.exp(lb_ - m), jnp.exp(lc - m)
    oa = (wa * o1[...] + wb * o2[...] + wc * o3[...]) / (wa + wb + wc)
    cat = jnp.concatenate([oa, ob[...], oc[...]], axis=1).astype(BF16)
    y = jnp.dot(cat, w_ref[...], preferred_element_type=F32)
    z = DEEPNORM_ALPHA * x_ref[...] + y
    out_ref[...] = _layer_norm_rows(z, g_ref[...], b_ref[...])


def _mix_out(o_list, l_list, ob, oc, x2d, w_out_p, ln_g, ln_b):
    t, d = x2d.shape
    tm = 256
    row = lambda w: pl.BlockSpec((tm, w), lambda i: (i, 0))
    full = lambda a: pl.BlockSpec(a.shape, lambda i: (0, 0))
    return pl.pallas_call(
        _mix_out_body,
        grid=(t // tm,),
        in_specs=[row(A_WIDTH)] * 6 + [row(B_PAD_WIDTH), row(C_WIDTH), row(d), full(w_out_p), full(ln_g), full(ln_b)],
        out_specs=row(d),
        out_shape=jax.ShapeDtypeStruct((t, d), F32),
        compiler_params=_cparams("arbitrary"),
        name="mix_out_ln",
    )(*o_list, *l_list, ob, oc, x2d, w_out_p, ln_g, ln_b)


def _router_body(h_ref, rw_ref, bias_ref, tri_ref, e_ref, rank_ref, gate_ref, cnt_ref, carry_ref):
    tn = h_ref.shape[0]
    per_group = N_EXPERTS // N_GROUPS
    neg_inf = -jnp.inf

    @pl.when(pl.program_id(0) == 0)
    def _():
        carry_ref[...] = jnp.zeros_like(carry_ref)

    logits = lax.dot_general(rw_ref[...], h_ref[...], (((1,), (1,)), ((), ())),
                             precision=lax.Precision.HIGHEST, preferred_element_type=F32)
    scores = jax.nn.sigmoid(logits)
    choice = scores + bias_ref[...]

    def first_max(vals, idx, sentinel):
        top = jnp.max(vals, axis=0, keepdims=True)
        return top, jnp.min(jnp.where(vals == top, idx, sentinel), axis=0, keepdims=True)

    li = lax.broadcasted_iota(I32, (per_group, tn), 0).astype(F32)
    group_rows = []
    for g in range(N_GROUPS):
        cg = choice[g * per_group:(g + 1) * per_group]
        m1, first = first_max(cg, li, float(per_group))
        m2 = jnp.max(jnp.where(li == first, neg_inf, cg), axis=0, keepdims=True)
        group_rows.append(m1 + m2)
    group_score = jnp.concatenate(group_rows, axis=0)

    gi = lax.broadcasted_iota(I32, (N_GROUPS, tn), 0).astype(F32)
    group_ok = jnp.zeros((N_GROUPS, tn), F32)
    cur = group_score
    for _ in range(TOPK_GROUPS):
        _, first = first_max(cur, gi, float(N_GROUPS))
        pick = gi == first
        group_ok = jnp.where(pick, 1.0, group_ok)
        cur = jnp.where(pick, neg_inf, cur)

    cur = jnp.concatenate(
        [jnp.where(group_ok[g:g + 1] > 0.0, choice[g * per_group:(g + 1) * per_group], MASK_VALUE)
         for g in range(N_GROUPS)], axis=0)
    ei = lax.broadcasted_iota(I32, (N_EXPERTS, tn), 0).astype(F32)
    chosen = jnp.zeros((N_EXPERTS, tn), F32)
    picks, gates = [], []
    for _ in range(TOP_K):
        _, idx = first_max(cur, ei, float(N_EXPERTS))
        pick = ei == idx
        picks.append(idx)
        gates.append(jnp.sum(jnp.where(pick, scores, 0.0), axis=0, keepdims=True))
        chosen = jnp.where(pick, 1.0, chosen)
        cur = jnp.where(pick, neg_inf, cur)

    gate = jnp.concatenate(gates, axis=0)
    gate_ref[...] = gate / jnp.sum(gate, axis=0, keepdims=True) * ROUTED_SCALE
    e_ref[...] = jnp.concatenate(picks, axis=0).astype(I32)

    before = jnp.dot(chosen.astype(BF16), tri_ref[...], preferred_element_type=F32) + carry_ref[...]
    ranks = [jnp.sum(jnp.where(ei == idx, before, 0.0), axis=0, keepdims=True) for idx in picks]
    rank_ref[...] = jnp.concatenate(ranks, axis=0).astype(I32)
    carry = carry_ref[...] + jnp.sum(chosen, axis=1, keepdims=True)
    carry_ref[...] = carry
    cnt_ref[...] = carry.astype(I32)


def _router(h2d, rw_t, bias_col):
    t, d = h2d.shape
    tn = 256
    tri = jnp.asarray(np.triu(np.ones((tn, tn), np.float32), k=1), dtype=BF16)
    tok = lambda: pl.BlockSpec((TOP_K, tn), lambda i: (0, i))
    return pl.pallas_call(
        _router_body,
        grid=(t // tn,),
        in_specs=[pl.BlockSpec((tn, d), lambda i: (i, 0)),
                  pl.BlockSpec((N_EXPERTS, d), lambda i: (0, 0)),
                  pl.BlockSpec((N_EXPERTS, 1), lambda i: (0, 0)),
                  pl.BlockSpec((tn, tn), lambda i: (0, 0))],
        out_specs=[tok(), tok(), tok(), pl.BlockSpec((N_EXPERTS, 1), lambda i: (0, 0))],
        out_shape=[jax.ShapeDtypeStruct((TOP_K, t), I32),
                   jax.ShapeDtypeStruct((TOP_K, t), I32),
                   jax.ShapeDtypeStruct((TOP_K, t), F32),
                   jax.ShapeDtypeStruct((N_EXPERTS, 1), I32)],
        scratch_shapes=[pltpu.VMEM((N_EXPERTS, 1), F32)],
        compiler_params=_cparams("arbitrary"),
        name="moe_router",
    )(h2d, rw_t, bias_col, tri)


def _dispatch_body(pad_end_ref, dest_ref, h_hbm, xs_hbm, zero_buf, sem):
    step = pl.program_id(0)
    tt = dest_ref.shape[1]
    bm = zero_buf.shape[0]

    def tail_copy(e):
        end = pl.multiple_of(pad_end_ref[e], bm)
        return pltpu.make_async_copy(zero_buf, xs_hbm.at[pl.ds(end - bm, bm)], sem)

    def has_rows(e):
        prev = jnp.where(e > 0, pad_end_ref[jnp.maximum(e - 1, 0)], 0)
        return pad_end_ref[e] > prev

    @pl.when(step == 0)
    def _():
        zero_buf[...] = jnp.zeros_like(zero_buf)

        def start(e, carry):
            @pl.when(has_rows(e))
            def _():
                tail_copy(e).start()
            return carry

        def wait(e, carry):
            @pl.when(has_rows(e))
            def _():
                tail_copy(e).wait()
            return carry

        lax.fori_loop(0, N_EXPERTS, start, 0)
        lax.fori_loop(0, N_EXPERTS, wait, 0)

    def row_copy(t, k):
        return pltpu.make_async_copy(h_hbm.at[pl.ds(step * tt + t, 1)], xs_hbm.at[pl.ds(dest_ref[k, t], 1)], sem)

    def start_rows(t, carry):
        for k in range(TOP_K):
            row_copy(t, k).start()
        return carry

    def wait_rows(t, carry):
        for k in range(TOP_K):
            row_copy(t, k).wait()
        return carry

    lax.fori_loop(0, tt, start_rows, 0)
    lax.fori_loop(0, tt, wait_rows, 0)


def _dispatch(h2d, dest_t, pad_end, n_rows):
    t, d = h2d.shape
    tt = 512
    grid_spec = pltpu.PrefetchScalarGridSpec(
        num_scalar_prefetch=1,
        grid=(t // tt,),
        in_specs=[pl.BlockSpec((TOP_K, tt), lambda i, pe: (0, i), memory_space=pltpu.SMEM),
                  pl.BlockSpec(memory_space=pl.ANY)],
        out_specs=pl.BlockSpec(memory_space=pl.ANY),
        scratch_shapes=[pltpu.VMEM((EXPERT_BLOCK, d), F32), pltpu.SemaphoreType.DMA(())],
    )
    return pl.pallas_call(
        _dispatch_body,
        grid_spec=grid_spec,
        out_shape=jax.ShapeDtypeStruct((n_rows, d), F32),
        compiler_params=_cparams("arbitrary"),
        name="moe_dispatch",
    )(pad_end, dest_t, h2d)


def _expert_body(block_e_ref, blk_ref, used_ref, x_ref, wg_ref, wu_ref, wd_ref, y_ref):
    @pl.when(pl.program_id(0) < used_ref[0])
    def _():
        xb = x_ref[...].astype(BF16)
        g = jnp.dot(xb, wg_ref[0].astype(BF16), preferred_element_type=F32)
        u = jnp.dot(xb, wu_ref[0].astype(BF16), preferred_element_type=F32)
        hidden = (g * jax.nn.sigmoid(g) * u).astype(BF16)
        y_ref[...] = jnp.dot(hidden, wd_ref[0].astype(BF16), preferred_element_type=F32)


def _experts(xs, block_e, blk_idx, used, w_gate, w_up, w_down):
    n_rows, d = xs.shape
    bm = EXPERT_BLOCK
    ff = w_gate.shape[-1]
    grid_spec = pltpu.PrefetchScalarGridSpec(
        num_scalar_prefetch=3,
        grid=(n_rows // bm,),
        in_specs=[pl.BlockSpec((bm, d), lambda i, be, bi, u: (bi[i], 0)),
                  pl.BlockSpec((1, d, ff), lambda i, be, bi, u: (be[i], 0, 0)),
                  pl.BlockSpec((1, d, ff), lambda i, be, bi, u: (be[i], 0, 0)),
                  pl.BlockSpec((1, ff, d), lambda i, be, bi, u: (be[i], 0, 0))],
        out_specs=pl.BlockSpec((bm, d), lambda i, be, bi, u: (bi[i], 0)),
    )
    return pl.pallas_call(
        _expert_body,
        grid_spec=grid_spec,
        out_shape=jax.ShapeDtypeStruct((n_rows, d), F32),
        compiler_params=_cparams("arbitrary"),
        name="moe_experts",
    )(block_e, blk_idx, used, xs, w_gate, w_up, w_down)


def _combine_body(dest_ref, gate_ref, h_ref, ys_hbm, sg_ref, su_ref, sd_ref, g_ref, b_ref, out_ref, buf, sem):
    tt = h_ref.shape[0]

    def row_copy(t, k):
        return pltpu.make_async_copy(ys_hbm.at[pl.ds(dest_ref[k, t], 1)], buf.at[k, pl.ds(t, 1)], sem)

    def start_rows(t, carry):
        for k in range(TOP_K):
            row_copy(t, k).start()
        return carry

    def wait_rows(t, carry):
        for k in range(TOP_K):
            row_copy(t, k).wait()
        return carry

    lax.fori_loop(0, tt, start_rows, 0)

    h = h_ref[...]
    hb = h.astype(BF16)
    sg = jnp.dot(hb, sg_ref[...], preferred_element_type=F32)
    su = jnp.dot(hb, su_ref[...], preferred_element_type=F32)
    shared = jnp.dot((sg * jax.nn.sigmoid(sg) * su).astype(BF16), sd_ref[...], preferred_element_type=F32)

    lax.fori_loop(0, tt, wait_rows, 0)

    z = DEEPNORM_ALPHA * h + shared
    for k in range(TOP_K):
        z = z + gate_ref[:, k:k + 1] * buf[k]
    out_ref[...] = _layer_norm_rows(z, g_ref[...], b_ref[...])


def _combine(dest_t, gate_tk, h2d, ys, sh_gate, sh_up, sh_down, ln_g, ln_b):
    t, d = h2d.shape
    tt = 128
    full = lambda a: pl.BlockSpec(a.shape, lambda i: (0, 0))
    return pl.pallas_call(
        _combine_body,
        grid=(t // tt,),
        in_specs=[pl.BlockSpec((TOP_K, tt), lambda i: (0, i), memory_space=pltpu.SMEM),
                  pl.BlockSpec((tt, TOP_K), lambda i: (i, 0)),
                  pl.BlockSpec((tt, d), lambda i: (i, 0)),
                  pl.BlockSpec(memory_space=pl.ANY),
                  full(sh_gate), full(sh_up), full(sh_down), full(ln_g), full(ln_b)],
        out_specs=pl.BlockSpec((tt, d), lambda i: (i, 0)),
        out_shape=jax.ShapeDtypeStruct((t, d), F32),
        scratch_shapes=[pltpu.VMEM((TOP_K, tt, d), F32), pltpu.SemaphoreType.DMA(())],
        compiler_params=_cparams("arbitrary"),
        name="moe_combine_ln",
    )(dest_t, gate_tk, h2d, ys, sh_gate, sh_up, sh_down, ln_g, ln_b)


def _mixer_sublayer(h2d, nb, s, w_in_p, w_out_p, a_biases, c_bias, lower_bound, norm_g_pad, sinks, w_sum, ln_g, ln_b):
    a_qkv, b_all, c_qkv = _in_proj(h2d, w_in_p)
    o_list, l_list = [], []
    for (window, r), bias in zip(A_PATTERNS, a_biases):
        src = a_qkv.reshape(nb, s // r, r * A_COLS)
        o, lse = _band_attn(src, bias, r=r, width=A_WIDTH, src_blocks=3, max_dist=window // r, want_lse=True)
        o_list.append(o.reshape(nb * s, A_WIDTH))
        l_list.append(lse.reshape(nb * s, A_WIDTH))
    oc = _band_attn(c_qkv.reshape(nb, s, C_COLS), c_bias, r=1, width=C_WIDTH, src_blocks=3,
                    max_dist=C_WINDOW - 1, sinks=sinks, want_lse=False).reshape(nb * s, C_WIDTH)
    ob = _hgrn(b_all.reshape(nb, s, B_COLS), lower_bound, norm_g_pad, w_sum).reshape(nb * s, B_PAD_WIDTH)
    return _mix_out(o_list, l_list, ob, oc, h2d, w_out_p, ln_g, ln_b)


def _moe_sublayer(h2d, router_w, router_bias, w_gate, w_up, w_down, sh_gate, sh_up, sh_down, ln_g, ln_b):
    t, d = h2d.shape
    bm = EXPERT_BLOCK
    e_t, rank_t, gate_t, counts = _router(h2d, router_w.T.astype(F32), router_bias.astype(F32).reshape(N_EXPERTS, 1))

    counts = counts.reshape(N_EXPERTS)
    padded = (counts + bm - 1) // bm * bm
    pad_end = jnp.cumsum(padded).astype(I32)
    offsets = pad_end - padded
    n_blocks = -(-(t * TOP_K + N_EXPERTS * (bm - 1)) // bm)
    used = (pad_end[-1] // bm).astype(I32)
    blk_idx = jnp.minimum(jnp.arange(n_blocks, dtype=I32), used - 1)
    block_e = jnp.minimum(jnp.searchsorted(pad_end, blk_idx * bm, side="right"), N_EXPERTS - 1).astype(I32)
    dest_t = offsets[e_t] + rank_t

    xs = _dispatch(h2d, dest_t, pad_end, n_blocks * bm)
    ys = _experts(xs, block_e, blk_idx, used.reshape(1), w_gate, w_up, w_down)
    return _combine(dest_t, gate_t.T, h2d, ys, sh_gate.astype(BF16), sh_up.astype(BF16), sh_down.astype(BF16), ln_g, ln_b)


def kernel(x, w_in, w_out, rel_bias_table, lower_bound_logits, hgrn_norm_g, attn_sinks, ln1_g, ln1_b, router_w, router_bias, expert_w_gate, expert_w_up, expert_w_down, shared_w_gate, shared_w_up, shared_w_down, ln2_g, ln2_b):
    nb, s, d = x.shape
    depth = w_in.shape[0]
    lb_probs = jax.nn.softmax(lower_bound_logits.astype(F32), axis=0)
    lower_bounds = jnp.cumsum(lb_probs, axis=0) - lb_probs[0]
    rel_table = rel_bias_table.astype(F32)
    a_biases = [_band_bias(rel_table, r, 0, A_HEADS) for _, r in A_PATTERNS]
    c_bias = _band_bias(rel_table, 1, A_HEADS, A_HEADS + C_HEADS)
    w_sum = jnp.asarray(_hgrn_sum_matrix(), dtype=BF16)
    row = lambda v: v.astype(F32).reshape(1, -1)

    h = x.astype(F32).reshape(nb * s, d)
    for l in range(depth):
        h = _mixer_sublayer(h, nb, s, _prep_w_in(w_in[l]), _prep_w_out(w_out[l].astype(F32)), a_biases, c_bias,
                            lower_bounds[l].reshape(1, B_KEY_WIDTH), _pad_heads_vec(hgrn_norm_g[l].astype(F32)),
                            attn_sinks[l].astype(F32), w_sum, row(ln1_g[l]), row(ln1_b[l]))
        h = _moe_sublayer(h, router_w[l], router_bias[l], expert_w_gate[l], expert_w_up[l], expert_w_down[l],
                          shared_w_gate[l], shared_w_up[l], shared_w_down[l], row(ln2_g[l]), row(ln2_b[l]))
    return h.reshape(nb, s, d).astype(x.dtype)
```

```python
import functools
import math

import numpy as np
import jax
import jax.numpy as jnp
from jax import lax
from jax.experimental import pallas as pl
from jax.experimental.pallas import tpu as pltpu

F32 = jnp.float32
BF16 = jnp.bfloat16
I32 = jnp.int32

LANES = 128
SUBLANES = 8
VMEM_LIMIT = 56 * 1024 * 1024

D_MODEL = 1024
DEPTH = 2
HEAD_DIM = 64
BAND = 128
MASK_VALUE = -1e30

A_HEADS = 6
A_PATTERNS = ((128, 1), (512, 4), (2048, 16))
B_HEADS = 4
B_KEY_DIM = 128
B_VAL_DIM = 96
B_VAL_PAD = 128
HG_CHUNK = 64
C_HEADS = 4
C_KV_HEADS = 2
C_WINDOW = 128

A_WIDTH = A_HEADS * HEAD_DIM
B_KEY_WIDTH = B_HEADS * B_KEY_DIM
B_WIDTH = B_HEADS * B_VAL_DIM
B_PAD_WIDTH = B_HEADS * B_VAL_PAD
C_WIDTH = C_HEADS * HEAD_DIM
C_KV_WIDTH = C_KV_HEADS * HEAD_DIM
IN_SPLITS = (A_WIDTH, A_WIDTH, A_WIDTH, B_KEY_WIDTH, B_KEY_WIDTH, B_WIDTH, B_WIDTH, C_WIDTH, C_KV_WIDTH, C_KV_WIDTH)

A_COLS = 3 * A_WIDTH
B_COLS = 4 * B_KEY_WIDTH
C_COLS = 3 * C_WIDTH
MIX_PAD_WIDTH = A_WIDTH + B_PAD_WIDTH + C_WIDTH

REL_BUCKETS = 32
REL_MAX_DIST = 2048

N_EXPERTS = 256
TOP_K = 8
N_GROUPS = 8
TOPK_GROUPS = 4
EXPERT_FF = 256
SHARED_FF = 256
ROUTED_SCALE = 2.5
EXPERT_BLOCK = 256

DEEPNORM_ALPHA = (2 * DEPTH) ** 0.25
LN_EPS = 1e-5


def _cparams(*sem):
    return pltpu.CompilerParams(dimension_semantics=sem, vmem_limit_bytes=VMEM_LIMIT)


def _prep_w_in(w_in):
    d = w_in.shape[0]
    split_at = [int(i) for i in np.cumsum(IN_SPLITS)[:-1]]
    aq, ak, av, bq, bf, bi, bg, cq, ck, cv = jnp.split(w_in, split_at, axis=-1)
    pad_v = lambda w: jnp.pad(w.reshape(d, B_HEADS, B_VAL_DIM), ((0, 0), (0, 0), (0, B_VAL_PAD - B_VAL_DIM))).reshape(d, B_PAD_WIDTH)
    rep = lambda w: jnp.repeat(w.reshape(d, C_KV_HEADS, HEAD_DIM), C_HEADS // C_KV_HEADS, axis=1).reshape(d, C_WIDTH)
    cols = [aq, ak, av, bq, bf, pad_v(bi), pad_v(bg), cq, rep(ck), rep(cv)]
    return jnp.concatenate(cols, axis=-1).astype(BF16)


def _prep_w_out(w_out):
    d = w_out.shape[1]
    wa = w_out[:A_WIDTH]
    wb = w_out[A_WIDTH:A_WIDTH + B_WIDTH].reshape(B_HEADS, B_VAL_DIM, d)
    wb = jnp.pad(wb, ((0, 0), (0, B_VAL_PAD - B_VAL_DIM), (0, 0))).reshape(B_PAD_WIDTH, d)
    wc = w_out[A_WIDTH + B_WIDTH:]
    return jnp.concatenate([wa, wb, wc], axis=0).astype(BF16)


def _pad_heads_vec(v):
    return jnp.pad(v.reshape(B_HEADS, B_VAL_DIM), ((0, 0), (0, B_VAL_PAD - B_VAL_DIM))).reshape(1, B_PAD_WIDTH)


def _rel_bucket(dist):
    max_exact = REL_BUCKETS // 2
    d = jnp.maximum(dist, 0)
    log_ratio = jnp.log(jnp.maximum(d, max_exact).astype(F32) / max_exact) / math.log(REL_MAX_DIST / max_exact)
    large = jnp.minimum(max_exact + (log_ratio * (REL_BUCKETS - max_exact)).astype(I32), REL_BUCKETS - 1)
    return jnp.where(d < max_exact, d, large)


def _band_bias(rel_table, r, head_lo, head_hi):
    dist = jnp.arange(BAND)[:, None] + BAND - jnp.arange(2 * BAND)[None, :]
    return rel_table[_rel_bucket(dist * r)][..., head_lo:head_hi].transpose(2, 0, 1)


def _hgrn_sum_matrix():
    c = HG_CHUNK
    t = np.arange(c)[:, None]
    u = np.arange(c)[None, :]
    blocks = [(u <= t), (u > t)]
    m = c
    while m >= 2:
        mid = (t // m) * m + m // 2
        second = t >= mid
        blocks.append(np.where(second, (u >= mid) & (u <= t), (u > t) & (u < mid)))
        m //= 2
    return np.concatenate(blocks, axis=0).astype(np.float32)


HG_LEVELS = int(math.log2(HG_CHUNK))


def _in_proj_body(x_ref, w_ref, a_ref, b_ref, c_ref):
    xb = x_ref[...].astype(BF16)
    a_ref[...] = jnp.dot(xb, w_ref[:, :A_COLS], preferred_element_type=F32).astype(BF16)
    for j in range(B_COLS // B_KEY_WIDTH):
        lo = A_COLS + j * B_KEY_WIDTH
        b_ref[:, j * B_KEY_WIDTH:(j + 1) * B_KEY_WIDTH] = jnp.dot(
            xb, w_ref[:, lo:lo + B_KEY_WIDTH], preferred_element_type=F32)
    c_ref[...] = jnp.dot(xb, w_ref[:, A_COLS + B_COLS:], preferred_element_type=F32).astype(BF16)


def _in_proj(x2d, w_p):
    t, d = x2d.shape
    tm = 512
    n = w_p.shape[1]
    return pl.pallas_call(
        _in_proj_body,
        grid=(t // tm,),
        in_specs=[pl.BlockSpec((tm, d), lambda i: (i, 0)),
                  pl.BlockSpec((d, n), lambda i: (0, 0))],
        out_specs=[pl.BlockSpec((tm, A_COLS), lambda i: (i, 0)),
                   pl.BlockSpec((tm, B_COLS), lambda i: (i, 0)),
                   pl.BlockSpec((tm, C_COLS), lambda i: (i, 0))],
        out_shape=[jax.ShapeDtypeStruct((t, A_COLS), BF16),
                   jax.ShapeDtypeStruct((t, B_COLS), F32),
                   jax.ShapeDtypeStruct((t, C_COLS), BF16)],
        compiler_params=_cparams("arbitrary"),
        name="in_proj",
    )(x2d, w_p)


def _band_attn_body(*refs, width, max_dist, has_sink, want_lse):
    q_ref, kp_ref, kc_ref, vp_ref, vc_ref, bias_ref = refs[:6]
    rest = refs[6:]
    if has_sink:
        sink_ref, rest = rest[0], rest[1:]
    o_ref = rest[0]
    lse_ref = rest[1] if want_lse else None

    blk = pl.program_id(2)
    row = lax.broadcasted_iota(I32, (BAND, 2 * BAND), 0)
    col = lax.broadcasted_iota(I32, (BAND, 2 * BAND), 1)
    dist = row + BAND - col
    mask = (dist >= 0) & (dist <= max_dist) & ((col >= BAND) | (blk > 0))
    low_half = lax.broadcasted_iota(I32, (BAND, LANES), 1) < HEAD_DIM
    scale = HEAD_DIM ** -0.5

    for tile in range(width // LANES):
        sl = slice(tile * LANES, (tile + 1) * LANES)
        q2 = q_ref[0, :, sl]
        k2 = jnp.concatenate([kp_ref[0, :, sl], kc_ref[0, :, sl]], axis=0)
        v2 = jnp.concatenate([vp_ref[0, :, sl], vc_ref[0, :, sl]], axis=0)
        outs, lses = [], []
        for half in range(2):
            h = 2 * tile + half
            qm = jnp.where(low_half if half == 0 else jnp.logical_not(low_half), q2, jnp.zeros_like(q2))
            s = lax.dot_general(qm, k2, (((1,), (1,)), ((), ())), preferred_element_type=F32)
            s = s * scale + bias_ref[h]
            s = jnp.where(mask, s, MASK_VALUE)
            m = jnp.max(s, axis=-1, keepdims=True)
            if has_sink:
                sink = sink_ref[h]
                m = jnp.maximum(m, sink)
            p = jnp.exp(s - m)
            den = jnp.sum(p, axis=-1, keepdims=True)
            if has_sink:
                den = den + jnp.exp(sink - m)
            pv = jnp.dot(p.astype(BF16), v2, preferred_element_type=F32)
            outs.append(pv / den)
            if want_lse:
                lses.append(jnp.broadcast_to(m + jnp.log(den), (BAND, LANES)))
        o_ref[0, :, sl] = jnp.where(low_half, outs[0], outs[1])
        if want_lse:
            lse_ref[0, :, sl] = jnp.where(low_half, lses[0], lses[1])


def _band_attn(src, bias, *, r, width, src_blocks, max_dist, sinks=None, want_lse):
    nb, length, _ = src.shape
    nblk = length // BAND
    heads = width // HEAD_DIM
    has_sink = sinks is not None

    def cur(off):
        return pl.BlockSpec((1, BAND, width), lambda b, p, i: (b, i, p * src_blocks + off))

    def prev(off):
        return pl.BlockSpec((1, BAND, width), lambda b, p, i: (b, jnp.maximum(i - 1, 0), p * src_blocks + off))

    in_specs = [cur(0), prev(1), cur(1), prev(2), cur(2),
                pl.BlockSpec((heads, BAND, 2 * BAND), lambda b, p, i: (0, 0, 0))]
    args = [src, src, src, src, src, bias]
    if has_sink:
        in_specs.append(pl.BlockSpec(memory_space=pltpu.SMEM))
        args.append(sinks)
    out_spec = pl.BlockSpec((1, BAND, width), lambda b, p, i: (b, i, p))
    out_sds = jax.ShapeDtypeStruct((nb, length, r * width), F32)
    body = functools.partial(_band_attn_body, width=width, max_dist=max_dist, has_sink=has_sink, want_lse=want_lse)
    return pl.pallas_call(
        body,
        grid=(nb, r, nblk),
        in_specs=in_specs,
        out_specs=[out_spec, out_spec] if want_lse else out_spec,
        out_shape=[out_sds, out_sds] if want_lse else out_sds,
        compiler_params=_cparams("arbitrary", "arbitrary", "arbitrary"),
        name="band_attn_r%d_w%d" % (r, width),
    )(*args)


def _hgrn_body(b_ref, lb_ref, ng_ref, w_ref, o_ref, state_ref):
    c = HG_CHUNK
    kd = B_KEY_DIM

    @pl.when(pl.program_id(1) == 0)
    def _():
        state_ref[...] = jnp.zeros_like(state_ref)

    trow = lax.broadcasted_iota(I32, (c, 1), 0)
    ti = lax.broadcasted_iota(I32, (c, c), 0)
    si = lax.broadcasted_iota(I32, (c, c), 1)
    nt = (((1,), (1,)), ((), ()))
    w_all = w_ref[...]

    for h in range(B_HEADS):
        ks = slice(h * kd, (h + 1) * kd)
        q = b_ref[0, :, ks]
        f = b_ref[0, :, B_KEY_WIDTH + h * kd:B_KEY_WIDTH + (h + 1) * kd]
        inp = b_ref[0, :, 2 * B_KEY_WIDTH + h * kd:2 * B_KEY_WIDTH + (h + 1) * kd]
        gate = b_ref[0, :, 3 * B_KEY_WIDTH + h * kd:3 * B_KEY_WIDTH + (h + 1) * kd]
        lb = lb_ref[:, ks]

        log_f = jnp.log(lb + (1.0 - lb) * jax.nn.sigmoid(f))
        key = (1.0 - lb) * jax.nn.sigmoid(-f)
        qs = q * jax.nn.sigmoid(q)

        g_hi = log_f.astype(BF16)
        g_lo = (log_f - g_hi.astype(F32)).astype(BF16)
        e2 = jnp.dot(w_all, jnp.concatenate([g_hi, g_lo], axis=1), preferred_element_type=F32)
        e = e2[:, :kd] + e2[:, kd:]

        cum = e[0:c]
        q_dec = (qs * jnp.exp(cum)).astype(BF16)
        k_dec = (key * jnp.exp(e[c:2 * c])).astype(BF16)
        inp_b = inp.astype(BF16)

        st = state_ref[h]
        inter = lax.dot_general(q_dec, st.astype(BF16), nt, preferred_element_type=F32)

        scores = jnp.where(ti == si,
                           lax.dot_general(qs.astype(BF16), key.astype(BF16), nt, preferred_element_type=F32), 0.0)
        m = c
        for lvl in range(HG_LEVELS):
            el = jnp.exp(e[(2 + lvl) * c:(3 + lvl) * c])
            second = (trow % m) >= (m // 2)
            ql = jnp.where(second, qs * el, 0.0).astype(BF16)
            kl = jnp.where(second, 0.0, key * el).astype(BF16)
            sl = lax.dot_general(ql, kl, nt, preferred_element_type=F32)
            if m < c:
                sl = jnp.where((ti // m) == (si // m), sl, 0.0)
            scores = scores + sl
            m //= 2
        intra = jnp.dot(scores.astype(BF16), inp_b, preferred_element_type=F32)

        new_st = st * jnp.exp(cum[c - 1:c]) + lax.dot_general(
            inp_b, k_dec, (((0,), (0,)), ((), ())), preferred_element_type=F32)
        state_ref[h] = new_st

        o = inter + intra
        ms = jnp.sum(o * o, axis=-1, keepdims=True) * (1.0 / B_VAL_DIM)
        o = o * lax.rsqrt(ms + 1e-6)
        o_ref[0, :, ks] = o * ng_ref[:, ks] * (gate * jax.nn.sigmoid(gate))


def _hgrn(b_all, lower_bound, norm_g_pad, w_sum):
    nb, s, _ = b_all.shape
    c = HG_CHUNK
    return pl.pallas_call(
        _hgrn_body,
        grid=(nb, s // c),
        in_specs=[pl.BlockSpec((1, c, B_COLS), lambda b, i: (b, i, 0)),
                  pl.BlockSpec((1, B_KEY_WIDTH), lambda b, i: (0, 0)),
                  pl.BlockSpec((1, B_PAD_WIDTH), lambda b, i: (0, 0)),
                  pl.BlockSpec(w_sum.shape, lambda b, i: (0, 0))],
        out_specs=pl.BlockSpec((1, c, B_PAD_WIDTH), lambda b, i: (b, i, 0)),
        out_shape=jax.ShapeDtypeStruct((nb, s, B_PAD_WIDTH), F32),
        scratch_shapes=[pltpu.VMEM((B_HEADS, B_VAL_PAD, B_KEY_DIM), F32)],
        compiler_params=_cparams("arbitrary", "arbitrary"),
        name="hgrn2",
    )(b_all, lower_bound, norm_g_pad, w_sum)


def _layer_norm_rows(z, g, b):
    mu = jnp.mean(z, axis=-1, keepdims=True)
    zc = z - mu
    var = jnp.mean(zc * zc, axis=-1, keepdims=True)
    return zc * lax.rsqrt(var + LN_EPS) * g + b


def _mix_out_body(o1, o2, o3, l1, l2, l3, ob, oc, x_ref, w_ref, g_ref, b_ref, out_ref):
    la, lb_, lc = l1[...], l2[...], l3[...]
    m = jnp.maximum(jnp.maximum(la, lb_), lc)
    wa, wb, wc = jnp.exp(la - m), jnp.exp(lb_ - m), jnp.exp(lc - m)
    oa = (wa * o1[...] + wb * o2[...] + wc * o3[...]) / (wa + wb + wc)
    cat = jnp.concatenate([oa, ob[...], oc[...]], axis=1).astype(BF16)
    y = jnp.dot(cat, w_ref[...], preferred_element_type=F32)
    z = DEEPNORM_ALPHA * x_ref[...] + y
    out_ref[...] = _layer_norm_rows(z, g_ref[...], b_ref[...])


def _mix_out(o_list, l_list, ob, oc, x2d, w_out_p, ln_g, ln_b):
    t, d = x2d.shape
    tm = 256
    row = lambda w: pl.BlockSpec((tm, w), lambda i: (i, 0))
    full = lambda a: pl.BlockSpec(a.shape, lambda i: (0, 0))
    return pl.pallas_call(
        _mix_out_body,
        grid=(t // tm,),
        in_specs=[row(A_WIDTH)] * 6 + [row(B_PAD_WIDTH), row(C_WIDTH), row(d), full(w_out_p), full(ln_g), full(ln_b)],
        out_specs=row(d),
        out_shape=jax.ShapeDtypeStruct((t, d), F32),
        compiler_params=_cparams("arbitrary"),
        name="mix_out_ln",
    )(*o_list, *l_list, ob, oc, x2d, w_out_p, ln_g, ln_b)


def _router_body(h_ref, rw_ref, bias_ref, tri_ref, e_ref, rank_ref, gate_ref, cnt_ref, carry_ref):
    tn = h_ref.shape[0]
    per_group = N_EXPERTS // N_GROUPS
    neg_inf = -jnp.inf

    @pl.when(pl.program_id(0) == 0)
    def _():
        carry_ref[...] = jnp.zeros_like(carry_ref)

    logits = lax.dot_general(rw_ref[...], h_ref[...], (((1,), (1,)), ((), ())),
                             precision=lax.Precision.HIGHEST, preferred_element_type=F32)
    scores = jax.nn.sigmoid(logits)
    choice = scores + bias_ref[...]

    def first_max(vals, idx, sentinel):
        top = jnp.max(vals, axis=0, keepdims=True)
        return top, jnp.min(jnp.where(vals == top, idx, sentinel), axis=0, keepdims=True)

    li = lax.broadcasted_iota(I32, (per_group, tn), 0).astype(F32)
    group_rows = []
    for g in range(N_GROUPS):
        cg = choice[g * per_group:(g + 1) * per_group]
        m1, first = first_max(cg, li, float(per_group))
        m2 = jnp.max(jnp.where(li == first, neg_inf, cg), axis=0, keepdims=True)
        group_rows.append(m1 + m2)
    group_score = jnp.concatenate(group_rows, axis=0)

    gi = lax.broadcasted_iota(I32, (N_GROUPS, tn), 0).astype(F32)
    group_ok = jnp.zeros((N_GROUPS, tn), F32)
    cur = group_score
    for _ in range(TOPK_GROUPS):
        _, first = first_max(cur, gi, float(N_GROUPS))
        pick = gi == first
        group_ok = jnp.where(pick, 1.0, group_ok)
        cur = jnp.where(pick, neg_inf, cur)

    cur = jnp.concatenate(
        [jnp.where(group_ok[g:g + 1] > 0.0, choice[g * per_group:(g + 1) * per_group], MASK_VALUE)
         for g in range(N_GROUPS)], axis=0)
    ei = lax.broadcasted_iota(I32, (N_EXPERTS, tn), 0).astype(F32)
    chosen = jnp.zeros((N_EXPERTS, tn), F32)
    picks, gates = [], []
    for _ in range(TOP_K):
        _, idx = first_max(cur, ei, float(N_EXPERTS))
        pick = ei == idx
        picks.append(idx)
        gates.append(jnp.sum(jnp.where(pick, scores, 0.0), axis=0, keepdims=True))
        chosen = jnp.where(pick, 1.0, chosen)
        cur = jnp.where(pick, neg_inf, cur)

    gate = jnp.concatenate(gates, axis=0)
    gate_ref[...] = gate / jnp.sum(gate, axis=0, keepdims=True) * ROUTED_SCALE
    e_ref[...] = jnp.concatenate(picks, axis=0).astype(I32)

    before = jnp.dot(chosen.astype(BF16), tri_ref[...], preferred_element_type=F32) + carry_ref[...]
    ranks = [jnp.sum(jnp.where(ei == idx, before, 0.0), axis=0, keepdims=True) for idx in picks]
    rank_ref[...] = jnp.concatenate(ranks, axis=0).astype(I32)
    carry = carry_ref[...] + jnp.sum(chosen, axis=1, keepdims=True)
    carry_ref[...] = carry
    cnt_ref[...] = carry.astype(I32)


def _router(h2d, rw_t, bias_col):
    t, d = h2d.shape
    tn = 256
    tri = jnp.asarray(np.triu(np.ones((tn, tn), np.float32), k=1), dtype=BF16)
    tok = lambda: pl.BlockSpec((TOP_K, tn), lambda i: (0, i))
    return pl.pallas_call(
        _router_body,
        grid=(t // tn,),
        in_specs=[pl.BlockSpec((tn, d), lambda i: (i, 0)),
                  pl.BlockSpec((N_EXPERTS, d), lambda i: (0, 0)),
                  pl.BlockSpec((N_EXPERTS, 1), lambda i: (0, 0)),
                  pl.BlockSpec((tn, tn), lambda i: (0, 0))],
        out_specs=[tok(), tok(), tok(), pl.BlockSpec((N_EXPERTS, 1), lambda i: (0, 0))],
        out_shape=[jax.ShapeDtypeStruct((TOP_K, t), I32),
                   jax.ShapeDtypeStruct((TOP_K, t), I32),
                   jax.ShapeDtypeStruct((TOP_K, t), F32),
                   jax.ShapeDtypeStruct((N_EXPERTS, 1), I32)],
        scratch_shapes=[pltpu.VMEM((N_EXPERTS, 1), F32)],
        compiler_params=_cparams("arbitrary"),
        name="moe_router",
    )(h2d, rw_t, bias_col, tri)


def _dispatch_body(pad_end_ref, offs_ref, e_ref, rank_ref, h_ref, xs_hbm, zero_buf, sem):
    step = pl.program_id(0)
    tt = h_ref.shape[0]
    bm = zero_buf.shape[0]

    def tail_copy(e):
        end = pl.multiple_of(pad_end_ref[e], bm)
        return pltpu.make_async_copy(zero_buf, xs_hbm.at[pl.ds(end - bm, bm)], sem)

    def has_rows(e):
        prev = jnp.where(e > 0, pad_end_ref[jnp.maximum(e - 1, 0)], 0)
        return pad_end_ref[e] > prev

    @pl.when(step == 0)
    def _():
        zero_buf[...] = jnp.zeros_like(zero_buf)

        def start(e, carry):
            @pl.when(has_rows(e))
            def _():
                tail_copy(e).start()
            return carry

        def wait(e, carry):
            @pl.when(has_rows(e))
            def _():
                tail_copy(e).wait()
            return carry

        lax.fori_loop(0, N_EXPERTS, start, 0)
        lax.fori_loop(0, N_EXPERTS, wait, 0)

    def start_rows(t, carry):
        for k in range(TOP_K):
            dest = offs_ref[e_ref[k, t]] + rank_ref[k, t]
            pltpu.make_async_copy(h_ref.at[pl.ds(t, 1)], xs_hbm.at[pl.ds(dest, 1)], sem).start()
        return carry

    lax.fori_loop(0, tt, start_rows, 0)
    for k in range(TOP_K):
        pltpu.make_async_copy(h_ref, xs_hbm.at[pl.ds(0, tt)], sem).wait()


def _dispatch(h2d, e_t, rank_t, offsets, pad_end, n_rows):
    t, d = h2d.shape
    tt = 512
    tok = pl.BlockSpec((TOP_K, tt), lambda i, pe, of: (0, i), memory_space=pltpu.SMEM)
    grid_spec = pltpu.PrefetchScalarGridSpec(
        num_scalar_prefetch=2,
        grid=(t // tt,),
        in_specs=[tok, tok, pl.BlockSpec((tt, d), lambda i, pe, of: (i, 0))],
        out_specs=pl.BlockSpec(memory_space=pl.ANY),
        scratch_shapes=[pltpu.VMEM((EXPERT_BLOCK, d), F32), pltpu.SemaphoreType.DMA(())],
    )
    return pl.pallas_call(
        _dispatch_body,
        grid_spec=grid_spec,
        out_shape=jax.ShapeDtypeStruct((n_rows, d), F32),
        compiler_params=_cparams("arbitrary"),
        name="moe_dispatch",
    )(pad_end, offsets, e_t, rank_t, h2d)


def _expert_body(block_e_ref, blk_ref, used_ref, layer_ref, x_ref, wg_ref, wu_ref, wd_ref, y_ref):
    @pl.when(pl.program_id(0) < used_ref[0])
    def _():
        xb = x_ref[...].astype(BF16)
        g = jnp.dot(xb, wg_ref[0, 0].astype(BF16), preferred_element_type=F32)
        u = jnp.dot(xb, wu_ref[0, 0].astype(BF16), preferred_element_type=F32)
        hidden = (g * jax.nn.sigmoid(g) * u).astype(BF16)
        y_ref[...] = jnp.dot(hidden, wd_ref[0, 0].astype(BF16), preferred_element_type=F32)


def _experts(xs, block_e, blk_idx, used, layer, w_gate, w_up, w_down):
    n_rows, d = xs.shape
    bm = EXPERT_BLOCK
    ff = w_gate.shape[-1]
    grid_spec = pltpu.PrefetchScalarGridSpec(
        num_scalar_prefetch=4,
        grid=(n_rows // bm,),
        in_specs=[pl.BlockSpec((bm, d), lambda i, be, bi, u, ly: (bi[i], 0)),
                  pl.BlockSpec((1, 1, d, ff), lambda i, be, bi, u, ly: (ly[0], be[i], 0, 0)),
                  pl.BlockSpec((1, 1, d, ff), lambda i, be, bi, u, ly: (ly[0], be[i], 0, 0)),
                  pl.BlockSpec((1, 1, ff, d), lambda i, be, bi, u, ly: (ly[0], be[i], 0, 0))],
        out_specs=pl.BlockSpec((bm, d), lambda i, be, bi, u, ly: (bi[i], 0)),
    )
    return pl.pallas_call(
        _expert_body,
        grid_spec=grid_spec,
        out_shape=jax.ShapeDtypeStruct((n_rows, d), F32),
        compiler_params=_cparams("arbitrary"),
        name="moe_experts",
    )(block_e, blk_idx, used, layer, xs, w_gate, w_up, w_down)


def _combine_body(offs_ref, e_ref, rank_ref, gate_ref, h_ref, ys_hbm, sg_ref, su_ref, sd_ref, g_ref, b_ref,
                  out_ref, buf, sem):
    tt = h_ref.shape[0]

    def start_rows(t, carry):
        for k in range(TOP_K):
            src = offs_ref[e_ref[k, t]] + rank_ref[k, t]
            pltpu.make_async_copy(ys_hbm.at[pl.ds(src, 1)], buf.at[k, pl.ds(t, 1)], sem).start()
        return carry

    lax.fori_loop(0, tt, start_rows, 0)

    h = h_ref[...]
    hb = h.astype(BF16)
    sg = jnp.dot(hb, sg_ref[...], preferred_element_type=F32)
    su = jnp.dot(hb, su_ref[...], preferred_element_type=F32)
    shared = jnp.dot((sg * jax.nn.sigmoid(sg) * su).astype(BF16), sd_ref[...], preferred_element_type=F32)

    for k in range(TOP_K):
        pltpu.make_async_copy(ys_hbm.at[pl.ds(0, tt)], buf.at[k], sem).wait()

    z = DEEPNORM_ALPHA * h + shared
    for k in range(TOP_K):
        z = z + gate_ref[:, k:k + 1] * buf[k]
    out_ref[...] = _layer_norm_rows(z, g_ref[...], b_ref[...])


def _combine(e_t, rank_t, offsets, gate_tk, h2d, ys, sh_gate, sh_up, sh_down, ln_g, ln_b):
    t, d = h2d.shape
    tt = 128
    full = lambda a: pl.BlockSpec(a.shape, lambda i, of: (0, 0))
    tok = pl.BlockSpec((TOP_K, tt), lambda i, of: (0, i), memory_space=pltpu.SMEM)
    grid_spec = pltpu.PrefetchScalarGridSpec(
        num_scalar_prefetch=1,
        grid=(t // tt,),
        in_specs=[tok, tok,
                  pl.BlockSpec((tt, TOP_K), lambda i, of: (i, 0)),
                  pl.BlockSpec((tt, d), lambda i, of: (i, 0)),
                  pl.BlockSpec(memory_space=pl.ANY),
                  full(sh_gate), full(sh_up), full(sh_down), full(ln_g), full(ln_b)],
        out_specs=pl.BlockSpec((tt, d), lambda i, of: (i, 0)),
        scratch_shapes=[pltpu.VMEM((TOP_K, tt, d), F32), pltpu.SemaphoreType.DMA(())],
    )
    return pl.pallas_call(
        _combine_body,
        grid_spec=grid_spec,
        out_shape=jax.ShapeDtypeStruct((t, d), F32),
        compiler_params=_cparams("arbitrary"),
        name="moe_combine_ln",
    )(offsets, e_t, rank_t, gate_tk, h2d, ys, sh_gate, sh_up, sh_down, ln_g, ln_b)


def _mixer_sublayer(h2d, nb, s, w_in_p, w_out_p, a_biases, c_bias, lower_bound, norm_g_pad, sinks, w_sum, ln_g, ln_b):
    a_qkv, b_all, c_qkv = _in_proj(h2d, w_in_p)
    o_list, l_list = [], []
    for (window, r), bias in zip(A_PATTERNS, a_biases):
        src = a_qkv.reshape(nb, s // r, r * A_COLS)
        o, lse = _band_attn(src, bias, r=r, width=A_WIDTH, src_blocks=3, max_dist=window // r, want_lse=True)
        o_list.append(o.reshape(nb * s, A_WIDTH))
        l_list.append(lse.reshape(nb * s, A_WIDTH))
    oc = _band_attn(c_qkv.reshape(nb, s, C_COLS), c_bias, r=1, width=C_WIDTH, src_blocks=3,
                    max_dist=C_WINDOW - 1, sinks=sinks, want_lse=False).reshape(nb * s, C_WIDTH)
    ob = _hgrn(b_all.reshape(nb, s, B_COLS), lower_bound, norm_g_pad, w_sum).reshape(nb * s, B_PAD_WIDTH)
    return _mix_out(o_list, l_list, ob, oc, h2d, w_out_p, ln_g, ln_b)


def _moe_sublayer(h2d, layer, router_w, router_bias, w_gate, w_up, w_down, sh_gate, sh_up, sh_down, ln_g, ln_b):
    t, d = h2d.shape
    bm = EXPERT_BLOCK
    e_t, rank_t, gate_t, counts = _router(h2d, router_w.T.astype(F32), router_bias.astype(F32).reshape(N_EXPERTS, 1))

    counts = counts.reshape(N_EXPERTS)
    padded = (counts + bm - 1) // bm * bm
    pad_end = jnp.cumsum(padded).astype(I32)
    offsets = pad_end - padded
    n_blocks = -(-(t * TOP_K + N_EXPERTS * (bm - 1)) // bm)
    used = (pad_end[-1] // bm).astype(I32)
    blk_idx = jnp.minimum(jnp.arange(n_blocks, dtype=I32), used - 1)
    block_e = jnp.minimum(jnp.searchsorted(pad_end, blk_idx * bm, side="right"), N_EXPERTS - 1).astype(I32)

    xs = _dispatch(h2d, e_t, rank_t, offsets, pad_end, n_blocks * bm)
    ys = _experts(xs, block_e, blk_idx, used.reshape(1), jnp.full((1,), layer, I32), w_gate, w_up, w_down)
    return _combine(e_t, rank_t, offsets, gate_t.T, h2d, ys,
                    sh_gate.astype(BF16), sh_up.astype(BF16), sh_down.astype(BF16), ln_g, ln_b)


def kernel(x, w_in, w_out, rel_bias_table, lower_bound_logits, hgrn_norm_g, attn_sinks, ln1_g, ln1_b, router_w, router_bias, expert_w_gate, expert_w_up, expert_w_down, shared_w_gate, shared_w_up, shared_w_down, ln2_g, ln2_b):
    nb, s, d = x.shape
    depth = w_in.shape[0]
    lb_probs = jax.nn.softmax(lower_bound_logits.astype(F32), axis=0)
    lower_bounds = jnp.cumsum(lb_probs, axis=0) - lb_probs[0]
    rel_table = rel_bias_table.astype(F32)
    a_biases = [_band_bias(rel_table, r, 0, A_HEADS) for _, r in A_PATTERNS]
    c_bias = _band_bias(rel_table, 1, A_HEADS, A_HEADS + C_HEADS)
    w_sum = jnp.asarray(_hgrn_sum_matrix(), dtype=BF16)
    row = lambda v: v.astype(F32).reshape(1, -1)

    h = x.astype(F32).reshape(nb * s, d)
    for l in range(depth):
        h = _mixer_sublayer(h, nb, s, _prep_w_in(w_in[l]), _prep_w_out(w_out[l].astype(F32)), a_biases, c_bias,
                            lower_bounds[l].reshape(1, B_KEY_WIDTH), _pad_heads_vec(hgrn_norm_g[l].astype(F32)),
                            attn_sinks[l].astype(F32), w_sum, row(ln1_g[l]), row(ln1_b[l]))
        h = _moe_sublayer(h, l, router_w[l], router_bias[l], expert_w_gate, expert_w_up, expert_w_down,
                          shared_w_gate[l], shared_w_up[l], shared_w_down[l], row(ln2_g[l]), row(ln2_b[l]))
    return h.reshape(nb, s, d).astype(x.dtype)
```

```python
import functools
import math

import numpy as np
import jax
import jax.numpy as jnp
from jax import lax
from jax.experimental import pallas as pl
from jax.experimental.pallas import tpu as pltpu

F32 = jnp.float32
BF16 = jnp.bfloat16
I32 = jnp.int32
U32 = jnp.uint32

LANES = 128
SUBLANES = 8
VMEM_LIMIT = 56 * 1024 * 1024

D_MODEL = 1024
DEPTH = 2
HEAD_DIM = 64
BAND = 128
MASK_VALUE = -1e30

A_HEADS = 6
A_PATTERNS = ((128, 1), (512, 4), (2048, 16))
B_HEADS = 4
B_KEY_DIM = 128
B_VAL_DIM = 96
B_VAL_PAD = 128
HG_CHUNK = 64
C_HEADS = 4
C_KV_HEADS = 2
C_WINDOW = 128

A_WIDTH = A_HEADS * HEAD_DIM
B_KEY_WIDTH = B_HEADS * B_KEY_DIM
B_WIDTH = B_HEADS * B_VAL_DIM
B_PAD_WIDTH = B_HEADS * B_VAL_PAD
C_WIDTH = C_HEADS * HEAD_DIM
C_KV_WIDTH = C_KV_HEADS * HEAD_DIM
IN_SPLITS = (A_WIDTH, A_WIDTH, A_WIDTH, B_KEY_WIDTH, B_KEY_WIDTH, B_WIDTH, B_WIDTH, C_WIDTH, C_KV_WIDTH, C_KV_WIDTH)

A_COLS = 3 * A_WIDTH
B_COLS = 4 * B_KEY_WIDTH
C_COLS = 3 * C_WIDTH
MIX_PAD_WIDTH = A_WIDTH + B_PAD_WIDTH + C_WIDTH

REL_BUCKETS = 32
REL_MAX_DIST = 2048

N_EXPERTS = 256
TOP_K = 8
N_GROUPS = 8
TOPK_GROUPS = 4
EXPERT_FF = 256
SHARED_FF = 256
ROUTED_SCALE = 2.5
EXPERT_BLOCK = 256

DEEPNORM_ALPHA = (2 * DEPTH) ** 0.25
LN_EPS = 1e-5


def _cparams(*sem):
    return pltpu.CompilerParams(dimension_semantics=sem, vmem_limit_bytes=VMEM_LIMIT)


def _prep_w_in(w_in):
    d = w_in.shape[0]
    split_at = [int(i) for i in np.cumsum(IN_SPLITS)[:-1]]
    aq, ak, av, bq, bf, bi, bg, cq, ck, cv = jnp.split(w_in, split_at, axis=-1)
    pad_v = lambda w: jnp.pad(w.reshape(d, B_HEADS, B_VAL_DIM), ((0, 0), (0, 0), (0, B_VAL_PAD - B_VAL_DIM))).reshape(d, B_PAD_WIDTH)
    rep = lambda w: jnp.repeat(w.reshape(d, C_KV_HEADS, HEAD_DIM), C_HEADS // C_KV_HEADS, axis=1).reshape(d, C_WIDTH)
    cols = [aq, ak, av, bq, bf, pad_v(bi), pad_v(bg), cq, rep(ck), rep(cv)]
    return jnp.concatenate(cols, axis=-1).astype(BF16)


def _prep_w_out(w_out):
    d = w_out.shape[1]
    wa = w_out[:A_WIDTH]
    wb = w_out[A_WIDTH:A_WIDTH + B_WIDTH].reshape(B_HEADS, B_VAL_DIM, d)
    wb = jnp.pad(wb, ((0, 0), (0, B_VAL_PAD - B_VAL_DIM), (0, 0))).reshape(B_PAD_WIDTH, d)
    wc = w_out[A_WIDTH + B_WIDTH:]
    return jnp.concatenate([wa, wb, wc], axis=0).astype(BF16)


def _pad_heads_vec(v):
    return jnp.pad(v.reshape(B_HEADS, B_VAL_DIM), ((0, 0), (0, B_VAL_PAD - B_VAL_DIM))).reshape(1, B_PAD_WIDTH)


def _rel_bucket(dist):
    max_exact = REL_BUCKETS // 2
    d = jnp.maximum(dist, 0)
    log_ratio = jnp.log(jnp.maximum(d, max_exact).astype(F32) / max_exact) / math.log(REL_MAX_DIST / max_exact)
    large = jnp.minimum(max_exact + (log_ratio * (REL_BUCKETS - max_exact)).astype(I32), REL_BUCKETS - 1)
    return jnp.where(d < max_exact, d, large)


def _band_bias(rel_table, r, head_lo, head_hi):
    dist = jnp.arange(BAND)[:, None] + BAND - jnp.arange(2 * BAND)[None, :]
    onehot = jax.nn.one_hot(_rel_bucket(dist * r), REL_BUCKETS, dtype=F32)
    return jnp.einsum("qkb,bh->hqk", onehot, rel_table[:, head_lo:head_hi], precision=lax.Precision.HIGHEST)


def _hgrn_sum_matrix():
    c = HG_CHUNK
    t = np.arange(c)[:, None]
    u = np.arange(c)[None, :]
    blocks = [(u <= t), (u > t)]
    m = c
    while m >= 2:
        mid = (t // m) * m + m // 2
        second = t >= mid
        blocks.append(np.where(second, (u >= mid) & (u <= t), (u > t) & (u < mid)))
        m //= 2
    return np.concatenate(blocks, axis=0).astype(np.float32)


HG_LEVELS = int(math.log2(HG_CHUNK))


def _in_proj_body(x_ref, w_ref, a_ref, b_ref, c_ref):
    xb = x_ref[...].astype(BF16)
    a_ref[...] = jnp.dot(xb, w_ref[:, :A_COLS], preferred_element_type=F32).astype(BF16)
    for j in range(B_COLS // B_KEY_WIDTH):
        lo = A_COLS + j * B_KEY_WIDTH
        b_ref[:, j * B_KEY_WIDTH:(j + 1) * B_KEY_WIDTH] = jnp.dot(
            xb, w_ref[:, lo:lo + B_KEY_WIDTH], preferred_element_type=F32)
    c_ref[...] = jnp.dot(xb, w_ref[:, A_COLS + B_COLS:], preferred_element_type=F32).astype(BF16)


def _in_proj(x2d, w_p):
    t, d = x2d.shape
    tm = 512
    n = w_p.shape[1]
    return pl.pallas_call(
        _in_proj_body,
        grid=(t // tm,),
        in_specs=[pl.BlockSpec((tm, d), lambda i: (i, 0)),
                  pl.BlockSpec((d, n), lambda i: (0, 0))],
        out_specs=[pl.BlockSpec((tm, A_COLS), lambda i: (i, 0)),
                   pl.BlockSpec((tm, B_COLS), lambda i: (i, 0)),
                   pl.BlockSpec((tm, C_COLS), lambda i: (i, 0))],
        out_shape=[jax.ShapeDtypeStruct((t, A_COLS), BF16),
                   jax.ShapeDtypeStruct((t, B_COLS), F32),
                   jax.ShapeDtypeStruct((t, C_COLS), BF16)],
        compiler_params=_cparams("arbitrary"),
        name="in_proj",
    )(x2d, w_p)


def _band_attn_body(*refs, width, max_dist, has_sink, want_lse):
    q_ref, kp_ref, kc_ref, vp_ref, vc_ref, bias_ref = refs[:6]
    rest = refs[6:]
    if has_sink:
        sink_ref, rest = rest[0], rest[1:]
    o_ref = rest[0]
    lse_ref = rest[1] if want_lse else None

    blk = pl.program_id(2)
    row = lax.broadcasted_iota(I32, (BAND, 2 * BAND), 0)
    col = lax.broadcasted_iota(I32, (BAND, 2 * BAND), 1)
    dist = row + BAND - col
    mask = (dist >= 0) & (dist <= max_dist) & ((col >= BAND) | (blk > 0))
    low_half = lax.broadcasted_iota(I32, (BAND, LANES), 1) < HEAD_DIM
    scale = HEAD_DIM ** -0.5

    for tile in range(width // LANES):
        sl = slice(tile * LANES, (tile + 1) * LANES)
        q2 = q_ref[0, :, sl]
        k2 = jnp.concatenate([kp_ref[0, :, sl], kc_ref[0, :, sl]], axis=0)
        v2 = jnp.concatenate([vp_ref[0, :, sl], vc_ref[0, :, sl]], axis=0)
        outs, lses = [], []
        for half in range(2):
            h = 2 * tile + half
            qm = jnp.where(low_half if half == 0 else jnp.logical_not(low_half), q2, jnp.zeros_like(q2))
            s = lax.dot_general(qm, k2, (((1,), (1,)), ((), ())), preferred_element_type=F32)
            s = s * scale + bias_ref[h]
            s = jnp.where(mask, s, MASK_VALUE)
            m = jnp.max(s, axis=-1, keepdims=True)
            if has_sink:
                sink = sink_ref[h]
                m = jnp.maximum(m, sink)
            p = jnp.exp(s - m)
            den = jnp.sum(p, axis=-1, keepdims=True)
            if has_sink:
                den = den + jnp.exp(sink - m)
            pv = jnp.dot(p.astype(BF16), v2, preferred_element_type=F32)
            outs.append(pv / den)
            if want_lse:
                lses.append(jnp.broadcast_to(m + jnp.log(den), (BAND, LANES)))
        o_ref[0, :, sl] = jnp.where(low_half, outs[0], outs[1])
        if want_lse:
            lse_ref[0, :, sl] = jnp.where(low_half, lses[0], lses[1])


def _band_attn(src, bias, *, r, width, src_blocks, max_dist, sinks=None, want_lse):
    nb, length, _ = src.shape
    nblk = length // BAND
    heads = width // HEAD_DIM
    has_sink = sinks is not None

    def cur(off):
        return pl.BlockSpec((1, BAND, width), lambda b, p, i: (b, i, p * src_blocks + off))

    def prev(off):
        return pl.BlockSpec((1, BAND, width), lambda b, p, i: (b, jnp.maximum(i - 1, 0), p * src_blocks + off))

    in_specs = [cur(0), prev(1), cur(1), prev(2), cur(2),
                pl.BlockSpec((heads, BAND, 2 * BAND), lambda b, p, i: (0, 0, 0))]
    args = [src, src, src, src, src, bias]
    if has_sink:
        in_specs.append(pl.BlockSpec(memory_space=pltpu.SMEM))
        args.append(sinks)
    out_spec = pl.BlockSpec((1, BAND, width), lambda b, p, i: (b, i, p))
    out_sds = jax.ShapeDtypeStruct((nb, length, r * width), F32)
    body = functools.partial(_band_attn_body, width=width, max_dist=max_dist, has_sink=has_sink, want_lse=want_lse)
    return pl.pallas_call(
        body,
        grid=(nb, r, nblk),
        in_specs=in_specs,
        out_specs=[out_spec, out_spec] if want_lse else out_spec,
        out_shape=[out_sds, out_sds] if want_lse else out_sds,
        compiler_params=_cparams("arbitrary", "arbitrary", "arbitrary"),
        name="band_attn_r%d_w%d" % (r, width),
    )(*args)


def _hgrn_body(b_ref, lb_ref, ng_ref, w_ref, o_ref, state_ref):
    c = HG_CHUNK
    kd = B_KEY_DIM

    @pl.when(pl.program_id(1) == 0)
    def _():
        state_ref[...] = jnp.zeros_like(state_ref)

    trow = lax.broadcasted_iota(I32, (c, 1), 0)
    ti = lax.broadcasted_iota(I32, (c, c), 0)
    si = lax.broadcasted_iota(I32, (c, c), 1)
    nt = (((1,), (1,)), ((), ()))
    w_all = w_ref[...]

    for h in range(B_HEADS):
        ks = slice(h * kd, (h + 1) * kd)
        q = b_ref[0, :, ks]
        f = b_ref[0, :, B_KEY_WIDTH + h * kd:B_KEY_WIDTH + (h + 1) * kd]
        inp = b_ref[0, :, 2 * B_KEY_WIDTH + h * kd:2 * B_KEY_WIDTH + (h + 1) * kd]
        gate = b_ref[0, :, 3 * B_KEY_WIDTH + h * kd:3 * B_KEY_WIDTH + (h + 1) * kd]
        lb = lb_ref[:, ks]

        log_f = jnp.log(lb + (1.0 - lb) * jax.nn.sigmoid(f))
        key = (1.0 - lb) * jax.nn.sigmoid(-f)
        qs = q * jax.nn.sigmoid(q)

        g_hi = log_f.astype(BF16)
        g_lo = (log_f - g_hi.astype(F32)).astype(BF16)
        e2 = jnp.dot(w_all, jnp.concatenate([g_hi, g_lo], axis=1), preferred_element_type=F32)
        e = e2[:, :kd] + e2[:, kd:]

        cum = e[0:c]
        q_dec = (qs * jnp.exp(cum)).astype(BF16)
        k_dec = (key * jnp.exp(e[c:2 * c])).astype(BF16)
        inp_b = inp.astype(BF16)

        st = state_ref[h]
        inter = lax.dot_general(q_dec, st.astype(BF16), nt, preferred_element_type=F32)

        scores = jnp.where(ti == si,
                           lax.dot_general(qs.astype(BF16), key.astype(BF16), nt, preferred_element_type=F32), 0.0)
        m = c
        for lvl in range(HG_LEVELS):
            el = jnp.exp(e[(2 + lvl) * c:(3 + lvl) * c])
            second = (trow % m) >= (m // 2)
            ql = jnp.where(second, qs * el, 0.0).astype(BF16)
            kl = jnp.where(second, 0.0, key * el).astype(BF16)
            sl = lax.dot_general(ql, kl, nt, preferred_element_type=F32)
            if m < c:
                sl = jnp.where((ti // m) == (si // m), sl, 0.0)
            scores = scores + sl
            m //= 2
        intra = jnp.dot(scores.astype(BF16), inp_b, preferred_element_type=F32)

        new_st = st * jnp.exp(cum[c - 1:c]) + lax.dot_general(
            inp_b, k_dec, (((0,), (0,)), ((), ())), preferred_element_type=F32)
        state_ref[h] = new_st

        o = inter + intra
        ms = jnp.sum(o * o, axis=-1, keepdims=True) * (1.0 / B_VAL_DIM)
        o = o * lax.rsqrt(ms + 1e-6)
        o_ref[0, :, ks] = o * ng_ref[:, ks] * (gate * jax.nn.sigmoid(gate))


def _hgrn(b_all, lower_bound, norm_g_pad, w_sum):
    nb, s, _ = b_all.shape
    c = HG_CHUNK
    return pl.pallas_call(
        _hgrn_body,
        grid=(nb, s // c),
        in_specs=[pl.BlockSpec((1, c, B_COLS), lambda b, i: (b, i, 0)),
                  pl.BlockSpec((1, B_KEY_WIDTH), lambda b, i: (0, 0)),
                  pl.BlockSpec((1, B_PAD_WIDTH), lambda b, i: (0, 0)),
                  pl.BlockSpec(w_sum.shape, lambda b, i: (0, 0))],
        out_specs=pl.BlockSpec((1, c, B_PAD_WIDTH), lambda b, i: (b, i, 0)),
        out_shape=jax.ShapeDtypeStruct((nb, s, B_PAD_WIDTH), F32),
        scratch_shapes=[pltpu.VMEM((B_HEADS, B_VAL_PAD, B_KEY_DIM), F32)],
        compiler_params=_cparams("arbitrary", "arbitrary"),
        name="hgrn2",
    )(b_all, lower_bound, norm_g_pad, w_sum)


def _layer_norm_rows(z, g, b):
    mu = jnp.mean(z, axis=-1, keepdims=True)
    zc = z - mu
    var = jnp.mean(zc * zc, axis=-1, keepdims=True)
    return zc * lax.rsqrt(var + LN_EPS) * g + b


def _mix_out_body(o1, o2, o3, l1, l2, l3, ob, oc, x_ref, w_ref, g_ref, b_ref, out_ref):
    la, lb_, lc = l1[...], l2[...], l3[...]
    m = jnp.maximum(jnp.maximum(la, lb_), lc)
    wa, wb, wc = jnp.exp(la - m), jnp.exp(lb_ - m), jnp.exp(lc - m)
    oa = (wa * o1[...] + wb * o2[...] + wc * o3[...]) / (wa + wb + wc)
    cat = jnp.concatenate([oa, ob[...], oc[...]], axis=1).astype(BF16)
    y = jnp.dot(cat, w_ref[...], preferred_element_type=F32)
    z = DEEPNORM_ALPHA * x_ref[...] + y
    out_ref[...] = _layer_norm_rows(z, g_ref[...], b_ref[...])


def _mix_out(o_list, l_list, ob, oc, x2d, w_out_p, ln_g, ln_b):
    t, d = x2d.shape
    tm = 256
    row = lambda w: pl.BlockSpec((tm, w), lambda i: (i, 0))
    full = lambda a: pl.BlockSpec(a.shape, lambda i: (0, 0))
    return pl.pallas_call(
        _mix_out_body,
        grid=(t // tm,),
        in_specs=[row(A_WIDTH)] * 6 + [row(B_PAD_WIDTH), row(C_WIDTH), row(d), full(w_out_p), full(ln_g), full(ln_b)],
        out_specs=row(d),
        out_shape=jax.ShapeDtypeStruct((t, d), F32),
        compiler_params=_cparams("arbitrary"),
        name="mix_out_ln",
    )(*o_list, *l_list, ob, oc, x2d, w_out_p, ln_g, ln_b)


def _router_body(h_ref, rw_ref, bias_ref, tri_ref, e_ref, rank_ref, gate_ref, cnt_ref, carry_ref):
    tn = h_ref.shape[0]
    per_group = N_EXPERTS // N_GROUPS
    neg_inf = -jnp.inf

    @pl.when(pl.program_id(0) == 0)
    def _():
        carry_ref[...] = jnp.zeros_like(carry_ref)

    logits = lax.dot_general(rw_ref[...], h_ref[...], (((1,), (1,)), ((), ())),
                             precision=lax.Precision.HIGHEST, preferred_element_type=F32)
    scores = jax.nn.sigmoid(logits)
    choice = scores + bias_ref[...]

    def first_max(vals, idx, sentinel):
        top = jnp.max(vals, axis=0, keepdims=True)
        return top, jnp.min(jnp.where(vals == top, idx, sentinel), axis=0, keepdims=True)

    li = lax.broadcasted_iota(I32, (per_group, tn), 0).astype(F32)
    group_rows = []
    for g in range(N_GROUPS):
        cg = choice[g * per_group:(g + 1) * per_group]
        m1, first = first_max(cg, li, float(per_group))
        m2 = jnp.max(jnp.where(li == first, neg_inf, cg), axis=0, keepdims=True)
        group_rows.append(m1 + m2)
    group_score = jnp.concatenate(group_rows, axis=0)

    gi = lax.broadcasted_iota(I32, (N_GROUPS, tn), 0).astype(F32)
    group_ok = jnp.zeros((N_GROUPS, tn), F32)
    cur = group_score
    for _ in range(TOPK_GROUPS):
        _, first = first_max(cur, gi, float(N_GROUPS))
        pick = gi == first
        group_ok = jnp.where(pick, 1.0, group_ok)
        cur = jnp.where(pick, neg_inf, cur)

    cur = jnp.concatenate(
        [jnp.where(group_ok[g:g + 1] > 0.0, choice[g * per_group:(g + 1) * per_group], MASK_VALUE)
         for g in range(N_GROUPS)], axis=0)
    ei = lax.broadcasted_iota(I32, (N_EXPERTS, tn), 0).astype(F32)
    chosen = jnp.zeros((N_EXPERTS, tn), F32)
    picks, gates = [], []
    for _ in range(TOP_K):
        _, idx = first_max(cur, ei, float(N_EXPERTS))
        pick = ei == idx
        picks.append(idx)
        gates.append(jnp.sum(jnp.where(pick, scores, 0.0), axis=0, keepdims=True))
        chosen = jnp.where(pick, 1.0, chosen)
        cur = jnp.where(pick, neg_inf, cur)

    gate = jnp.concatenate(gates, axis=0)
    gate_ref[...] = gate / jnp.sum(gate, axis=0, keepdims=True) * ROUTED_SCALE
    e_ref[...] = jnp.concatenate(picks, axis=0).astype(I32)

    before = jnp.dot(chosen.astype(BF16), tri_ref[...], preferred_element_type=F32) + carry_ref[...]
    ranks = [jnp.sum(jnp.where(ei == idx, before, 0.0), axis=0, keepdims=True) for idx in picks]
    rank_ref[...] = jnp.concatenate(ranks, axis=0).astype(I32)
    carry = carry_ref[...] + jnp.sum(chosen, axis=1, keepdims=True)
    carry_ref[...] = carry
    cnt_ref[...] = carry.astype(I32)


def _router(h2d, rw_t, bias_col):
    t, d = h2d.shape
    tn = 256
    tri = jnp.asarray(np.triu(np.ones((tn, tn), np.float32), k=1), dtype=BF16)
    tok = lambda: pl.BlockSpec((TOP_K, tn), lambda i: (0, i))
    return pl.pallas_call(
        _router_body,
        grid=(t // tn,),
        in_specs=[pl.BlockSpec((tn, d), lambda i: (i, 0)),
                  pl.BlockSpec((N_EXPERTS, d), lambda i: (0, 0)),
                  pl.BlockSpec((N_EXPERTS, 1), lambda i: (0, 0)),
                  pl.BlockSpec((tn, tn), lambda i: (0, 0))],
        out_specs=[tok(), tok(), tok(), pl.BlockSpec((N_EXPERTS, 1), lambda i: (0, 0))],
        out_shape=[jax.ShapeDtypeStruct((TOP_K, t), I32),
                   jax.ShapeDtypeStruct((TOP_K, t), I32),
                   jax.ShapeDtypeStruct((TOP_K, t), F32),
                   jax.ShapeDtypeStruct((N_EXPERTS, 1), I32)],
        scratch_shapes=[pltpu.VMEM((N_EXPERTS, 1), F32)],
        compiler_params=_cparams("arbitrary"),
        name="moe_router",
    )(h2d, rw_t, bias_col, tri)


def _pack_bf16_pairs(x):
    w = x.shape[1] // 2
    hi = lax.bitcast_convert_type(x[:, :w].astype(BF16).astype(F32), U32)
    lo = lax.bitcast_convert_type(x[:, w:].astype(BF16).astype(F32), U32)
    return hi | (lo >> 16)


def _unpack_bf16_pairs(p):
    hi = lax.bitcast_convert_type(p & jnp.uint32(0xFFFF0000), F32)
    lo = lax.bitcast_convert_type(p << 16, F32)
    return hi, lo


def _dest_body(e_ref, rank_ref, offs_ref, dest_ref):
    tn = e_ref.shape[1]
    ei = lax.broadcasted_iota(I32, (N_EXPERTS, tn), 0)
    offs = offs_ref[...]
    rows = [jnp.sum(jnp.where(ei == e_ref[k:k + 1, :], offs, 0.0), axis=0, keepdims=True) for k in range(TOP_K)]
    dest_ref[...] = jnp.concatenate(rows, axis=0).astype(I32) + rank_ref[...]


def _dest_rows(e_t, rank_t, offsets):
    t = e_t.shape[1]
    tn = 512
    tok = pl.BlockSpec((TOP_K, tn), lambda i: (0, i))
    return pl.pallas_call(
        _dest_body,
        grid=(t // tn,),
        in_specs=[tok, tok, pl.BlockSpec((N_EXPERTS, 1), lambda i: (0, 0))],
        out_specs=tok,
        out_shape=jax.ShapeDtypeStruct((TOP_K, t), I32),
        compiler_params=_cparams("arbitrary"),
        name="moe_dest",
    )(e_t, rank_t, offsets.astype(F32).reshape(N_EXPERTS, 1))


def _dispatch_body(pad_end_ref, dest_ref, h_ref, xs_hbm, packed, zero_buf, sem):
    step = pl.program_id(0)
    tt = h_ref.shape[0]
    bm = zero_buf.shape[0]

    def tail_copy(e):
        end = pl.multiple_of(pad_end_ref[e], bm)
        return pltpu.make_async_copy(zero_buf, xs_hbm.at[pl.ds(end - bm, bm)], sem)

    def has_rows(e):
        prev = jnp.where(e > 0, pad_end_ref[jnp.maximum(e - 1, 0)], 0)
        return pad_end_ref[e] > prev

    @pl.when(step == 0)
    def _():
        zero_buf[...] = jnp.zeros_like(zero_buf)

        def start(e, carry):
            @pl.when(has_rows(e))
            def _():
                tail_copy(e).start()
            return carry

        def wait(e, carry):
            @pl.when(has_rows(e))
            def _():
                tail_copy(e).wait()
            return carry

        lax.fori_loop(0, N_EXPERTS, start, 0)
        lax.fori_loop(0, N_EXPERTS, wait, 0)

    packed[...] = _pack_bf16_pairs(h_ref[...])

    def start_rows(t, carry):
        src = packed.at[pl.ds(t, 1)]
        for k in range(TOP_K):
            pltpu.make_async_copy(src, xs_hbm.at[pl.ds(dest_ref[t * TOP_K + k], 1)], sem).start()
        return carry

    lax.fori_loop(0, tt, start_rows, 0)
    for k in range(TOP_K):
        pltpu.make_async_copy(packed, xs_hbm.at[pl.ds(0, tt)], sem).wait()


def _dispatch(h2d, dest_flat, pad_end, n_rows):
    t, d = h2d.shape
    tt = 512
    grid_spec = pltpu.PrefetchScalarGridSpec(
        num_scalar_prefetch=1,
        grid=(t // tt,),
        in_specs=[pl.BlockSpec((tt * TOP_K,), lambda i, pe: (i,), memory_space=pltpu.SMEM),
                  pl.BlockSpec((tt, d), lambda i, pe: (i, 0))],
        out_specs=pl.BlockSpec(memory_space=pl.ANY),
        scratch_shapes=[pltpu.VMEM((tt, d // 2), U32), pltpu.VMEM((EXPERT_BLOCK, d // 2), U32),
                        pltpu.SemaphoreType.DMA(())],
    )
    return pl.pallas_call(
        _dispatch_body,
        grid_spec=grid_spec,
        out_shape=jax.ShapeDtypeStruct((n_rows, d // 2), U32),
        compiler_params=_cparams("arbitrary"),
        name="moe_dispatch",
    )(pad_end, dest_flat, h2d)


def _expert_body(block_e_ref, blk_ref, used_ref, layer_ref, x_ref, wg_ref, wu_ref, wd_ref, y_ref):
    @pl.when(pl.program_id(0) < used_ref[0])
    def _():
        hi, lo = _unpack_bf16_pairs(x_ref[...])
        xb = jnp.concatenate([hi.astype(BF16), lo.astype(BF16)], axis=1)
        g = jnp.dot(xb, wg_ref[0, 0].astype(BF16), preferred_element_type=F32)
        u = jnp.dot(xb, wu_ref[0, 0].astype(BF16), preferred_element_type=F32)
        hidden = (g * jax.nn.sigmoid(g) * u).astype(BF16)
        y_ref[...] = _pack_bf16_pairs(jnp.dot(hidden, wd_ref[0, 0].astype(BF16), preferred_element_type=F32))


def _experts(xs, block_e, blk_idx, used, layer, w_gate, w_up, w_down):
    n_rows, dp = xs.shape
    bm = EXPERT_BLOCK
    d, ff = w_gate.shape[-2:]
    grid_spec = pltpu.PrefetchScalarGridSpec(
        num_scalar_prefetch=4,
        grid=(n_rows // bm,),
        in_specs=[pl.BlockSpec((bm, dp), lambda i, be, bi, u, ly: (bi[i], 0)),
                  pl.BlockSpec((1, 1, d, ff), lambda i, be, bi, u, ly: (ly[0], be[i], 0, 0)),
                  pl.BlockSpec((1, 1, d, ff), lambda i, be, bi, u, ly: (ly[0], be[i], 0, 0)),
                  pl.BlockSpec((1, 1, ff, d), lambda i, be, bi, u, ly: (ly[0], be[i], 0, 0))],
        out_specs=pl.BlockSpec((bm, dp), lambda i, be, bi, u, ly: (bi[i], 0)),
    )
    return pl.pallas_call(
        _expert_body,
        grid_spec=grid_spec,
        out_shape=jax.ShapeDtypeStruct((n_rows, dp), U32),
        compiler_params=_cparams("arbitrary"),
        name="moe_experts",
    )(block_e, blk_idx, used, layer, xs, w_gate, w_up, w_down)


def _combine_body(dest_ref, gate_ref, h_ref, ys_hbm, sg_ref, su_ref, sd_ref, g_ref, b_ref, out_ref, buf, sem):
    tt = h_ref.shape[0]
    half = h_ref.shape[1] // 2

    def start_rows(t, carry):
        for k in range(TOP_K):
            pltpu.make_async_copy(ys_hbm.at[pl.ds(dest_ref[t * TOP_K + k], 1)], buf.at[k, pl.ds(t, 1)], sem).start()
        return carry

    lax.fori_loop(0, tt, start_rows, 0)

    h = h_ref[...]
    hb = h.astype(BF16)
    sg = jnp.dot(hb, sg_ref[...], preferred_element_type=F32)
    su = jnp.dot(hb, su_ref[...], preferred_element_type=F32)
    shared = jnp.dot((sg * jax.nn.sigmoid(sg) * su).astype(BF16), sd_ref[...], preferred_element_type=F32)

    for k in range(TOP_K):
        pltpu.make_async_copy(ys_hbm.at[pl.ds(0, tt)], buf.at[k], sem).wait()

    z = DEEPNORM_ALPHA * h + shared
    z_hi, z_lo = z[:, :half], z[:, half:]
    for k in range(TOP_K):
        y_hi, y_lo = _unpack_bf16_pairs(buf[k])
        gate = gate_ref[:, k:k + 1]
        z_hi = z_hi + gate * y_hi
        z_lo = z_lo + gate * y_lo
    out_ref[...] = _layer_norm_rows(jnp.concatenate([z_hi, z_lo], axis=1), g_ref[...], b_ref[...])


def _combine(dest_flat, gate_tk, h2d, ys, sh_gate, sh_up, sh_down, ln_g, ln_b):
    t, d = h2d.shape
    tt = 128
    full = lambda a: pl.BlockSpec(a.shape, lambda i: (0, 0))
    return pl.pallas_call(
        _combine_body,
        grid=(t // tt,),
        in_specs=[pl.BlockSpec((tt * TOP_K,), lambda i: (i,), memory_space=pltpu.SMEM),
                  pl.BlockSpec((tt, TOP_K), lambda i: (i, 0)),
                  pl.BlockSpec((tt, d), lambda i: (i, 0)),
                  pl.BlockSpec(memory_space=pl.ANY),
                  full(sh_gate), full(sh_up), full(sh_down), full(ln_g), full(ln_b)],
        out_specs=pl.BlockSpec((tt, d), lambda i: (i, 0)),
        out_shape=jax.ShapeDtypeStruct((t, d), F32),
        scratch_shapes=[pltpu.VMEM((TOP_K, tt, d // 2), U32), pltpu.SemaphoreType.DMA(())],
        compiler_params=_cparams("arbitrary"),
        name="moe_combine_ln",
    )(dest_flat, gate_tk, h2d, ys, sh_gate, sh_up, sh_down, ln_g, ln_b)


def _mixer_sublayer(h2d, nb, s, w_in_p, w_out_p, a_biases, c_bias, lower_bound, norm_g_pad, sinks, w_sum, ln_g, ln_b):
    a_qkv, b_all, c_qkv = _in_proj(h2d, w_in_p)
    o_list, l_list = [], []
    for (window, r), bias in zip(A_PATTERNS, a_biases):
        src = a_qkv.reshape(nb, s // r, r * A_COLS)
        o, lse = _band_attn(src, bias, r=r, width=A_WIDTH, src_blocks=3, max_dist=window // r, want_lse=True)
        o_list.append(o.reshape(nb * s, A_WIDTH))
        l_list.append(lse.reshape(nb * s, A_WIDTH))
    oc = _band_attn(c_qkv.reshape(nb, s, C_COLS), c_bias, r=1, width=C_WIDTH, src_blocks=3,
                    max_dist=C_WINDOW - 1, sinks=sinks, want_lse=False).reshape(nb * s, C_WIDTH)
    ob = _hgrn(b_all.reshape(nb, s, B_COLS), lower_bound, norm_g_pad, w_sum).reshape(nb * s, B_PAD_WIDTH)
    return _mix_out(o_list, l_list, ob, oc, h2d, w_out_p, ln_g, ln_b)


def _moe_sublayer(h2d, layer, router_w, router_bias, w_gate, w_up, w_down, sh_gate, sh_up, sh_down, ln_g, ln_b):
    t, d = h2d.shape
    bm = EXPERT_BLOCK
    e_t, rank_t, gate_t, counts = _router(h2d, router_w.T.astype(F32), router_bias.astype(F32).reshape(N_EXPERTS, 1))

    counts = counts.reshape(N_EXPERTS)
    padded = (counts + bm - 1) // bm * bm
    pad_end = jnp.cumsum(padded).astype(I32)
    offsets = pad_end - padded
    n_blocks = -(-(t * TOP_K + N_EXPERTS * (bm - 1)) // bm)
    used = (pad_end[-1] // bm).astype(I32)
    blk_idx = jnp.minimum(jnp.arange(n_blocks, dtype=I32), used - 1)
    block_e = jnp.minimum(jnp.sum((pad_end[None, :] <= (blk_idx * bm)[:, None]).astype(I32), axis=1), N_EXPERTS - 1)

    dest_flat = _dest_rows(e_t, rank_t, offsets).T.reshape(t * TOP_K)
    xs = _dispatch(h2d, dest_flat, pad_end, n_blocks * bm)
    ys = _experts(xs, block_e, blk_idx, used.reshape(1), jnp.full((1,), layer, I32), w_gate, w_up, w_down)
    return _combine(dest_flat, gate_t.T, h2d, ys,
                    sh_gate.astype(BF16), sh_up.astype(BF16), sh_down.astype(BF16), ln_g, ln_b)


def kernel(x, w_in, w_out, rel_bias_table, lower_bound_logits, hgrn_norm_g, attn_sinks, ln1_g, ln1_b, router_w, router_bias, expert_w_gate, expert_w_up, expert_w_down, shared_w_gate, shared_w_up, shared_w_down, ln2_g, ln2_b):
    nb, s, d = x.shape
    depth = w_in.shape[0]
    lb_probs = jax.nn.softmax(lower_bound_logits.astype(F32), axis=0)
    lower_bounds = jnp.cumsum(lb_probs, axis=0) - lb_probs[0]
    rel_table = rel_bias_table.astype(F32)
    a_biases = [_band_bias(rel_table, r, 0, A_HEADS) for _, r in A_PATTERNS]
    c_bias = _band_bias(rel_table, 1, A_HEADS, A_HEADS + C_HEADS)
    w_sum = jnp.asarray(_hgrn_sum_matrix(), dtype=BF16)
    row = lambda v: v.astype(F32).reshape(1, -1)

    h = x.astype(F32).reshape(nb * s, d)
    for l in range(depth):
        h = _mixer_sublayer(h, nb, s, _prep_w_in(w_in[l]), _prep_w_out(w_out[l].astype(F32)), a_biases, c_bias,
                            lower_bounds[l].reshape(1, B_KEY_WIDTH), _pad_heads_vec(hgrn_norm_g[l].astype(F32)),
                            attn_sinks[l].astype(F32), w_sum, row(ln1_g[l]), row(ln1_b[l]))
        h = _moe_sublayer(h, l, router_w[l], router_bias[l], expert_w_gate, expert_w_up, expert_w_down,
                          shared_w_gate[l], shared_w_up[l], shared_w_down[l], row(ln2_g[l]), row(ln2_b[l]))
    return h.reshape(nb, s, d).astype(x.dtype)
```

```python
import functools
import math

import numpy as np
import jax
import jax.numpy as jnp
from jax import lax
from jax.experimental import pallas as pl
from jax.experimental.pallas import tpu as pltpu

F32 = jnp.float32
BF16 = jnp.bfloat16
I32 = jnp.int32
U32 = jnp.uint32

LANES = 128
SUBLANES = 8
VMEM_LIMIT = 56 * 1024 * 1024

D_MODEL = 1024
DEPTH = 2
HEAD_DIM = 64
BAND = 128
MASK_VALUE = -1e30

A_HEADS = 6
A_PATTERNS = ((128, 1), (512, 4), (2048, 16))
B_HEADS = 4
B_KEY_DIM = 128
B_VAL_DIM = 96
B_VAL_PAD = 128
HG_CHUNK = 64
C_HEADS = 4
C_KV_HEADS = 2
C_WINDOW = 128

A_WIDTH = A_HEADS * HEAD_DIM
B_KEY_WIDTH = B_HEADS * B_KEY_DIM
B_WIDTH = B_HEADS * B_VAL_DIM
B_PAD_WIDTH = B_HEADS * B_VAL_PAD
C_WIDTH = C_HEADS * HEAD_DIM
C_KV_WIDTH = C_KV_HEADS * HEAD_DIM
IN_SPLITS = (A_WIDTH, A_WIDTH, A_WIDTH, B_KEY_WIDTH, B_KEY_WIDTH, B_WIDTH, B_WIDTH, C_WIDTH, C_KV_WIDTH, C_KV_WIDTH)

A_COLS = 3 * A_WIDTH
B_COLS = 4 * B_KEY_WIDTH
C_COLS = 3 * C_WIDTH
MIX_PAD_WIDTH = A_WIDTH + B_PAD_WIDTH + C_WIDTH

REL_BUCKETS = 32
REL_MAX_DIST = 2048

N_EXPERTS = 256
TOP_K = 8
N_GROUPS = 8
TOPK_GROUPS = 4
EXPERT_FF = 256
SHARED_FF = 256
ROUTED_SCALE = 2.5
EXPERT_BLOCK = 256

DEEPNORM_ALPHA = (2 * DEPTH) ** 0.25
LN_EPS = 1e-5


def _cparams(*sem):
    return pltpu.CompilerParams(dimension_semantics=sem, vmem_limit_bytes=VMEM_LIMIT)


def _prep_w_in(w_in):
    d = w_in.shape[0]
    split_at = [int(i) for i in np.cumsum(IN_SPLITS)[:-1]]
    aq, ak, av, bq, bf, bi, bg, cq, ck, cv = jnp.split(w_in, split_at, axis=-1)
    pad_v = lambda w: jnp.pad(w.reshape(d, B_HEADS, B_VAL_DIM), ((0, 0), (0, 0), (0, B_VAL_PAD - B_VAL_DIM))).reshape(d, B_PAD_WIDTH)
    rep = lambda w: jnp.repeat(w.reshape(d, C_KV_HEADS, HEAD_DIM), C_HEADS // C_KV_HEADS, axis=1).reshape(d, C_WIDTH)
    cols = [aq, ak, av, bq, bf, pad_v(bi), pad_v(bg), cq, rep(ck), rep(cv)]
    return jnp.concatenate(cols, axis=-1).astype(BF16)


def _prep_w_out(w_out):
    d = w_out.shape[1]
    wa = w_out[:A_WIDTH]
    wb = w_out[A_WIDTH:A_WIDTH + B_WIDTH].reshape(B_HEADS, B_VAL_DIM, d)
    wb = jnp.pad(wb, ((0, 0), (0, B_VAL_PAD - B_VAL_DIM), (0, 0))).reshape(B_PAD_WIDTH, d)
    wc = w_out[A_WIDTH + B_WIDTH:]
    return jnp.concatenate([wa, wb, wc], axis=0).astype(BF16)


def _pad_heads_vec(v):
    return jnp.pad(v.reshape(B_HEADS, B_VAL_DIM), ((0, 0), (0, B_VAL_PAD - B_VAL_DIM))).reshape(1, B_PAD_WIDTH)


def _rel_bucket(dist):
    max_exact = REL_BUCKETS // 2
    d = jnp.maximum(dist, 0)
    log_ratio = jnp.log(jnp.maximum(d, max_exact).astype(F32) / max_exact) / math.log(REL_MAX_DIST / max_exact)
    large = jnp.minimum(max_exact + (log_ratio * (REL_BUCKETS - max_exact)).astype(I32), REL_BUCKETS - 1)
    return jnp.where(d < max_exact, d, large)


def _band_bias(rel_table, r, head_lo, head_hi):
    dist = jnp.arange(BAND)[:, None] + BAND - jnp.arange(2 * BAND)[None, :]
    onehot = jax.nn.one_hot(_rel_bucket(dist * r), REL_BUCKETS, dtype=F32)
    return jnp.einsum("qkb,bh->hqk", onehot, rel_table[:, head_lo:head_hi], precision=lax.Precision.HIGHEST)


def _hgrn_sum_matrix():
    c = HG_CHUNK
    t = np.arange(c)[:, None]
    u = np.arange(c)[None, :]
    blocks = [(u <= t), (u > t)]
    m = c
    while m >= 2:
        mid = (t // m) * m + m // 2
        second = t >= mid
        blocks.append(np.where(second, (u >= mid) & (u <= t), (u > t) & (u < mid)))
        m //= 2
    return np.concatenate(blocks, axis=0).astype(np.float32)


HG_LEVELS = int(math.log2(HG_CHUNK))


def _in_proj_body(x_ref, w_ref, a_ref, b_ref, c_ref):
    xb = x_ref[...].astype(BF16)
    a_ref[...] = jnp.dot(xb, w_ref[:, :A_COLS], preferred_element_type=F32).astype(BF16)
    for j in range(B_COLS // B_KEY_WIDTH):
        lo = A_COLS + j * B_KEY_WIDTH
        b_ref[:, j * B_KEY_WIDTH:(j + 1) * B_KEY_WIDTH] = jnp.dot(
            xb, w_ref[:, lo:lo + B_KEY_WIDTH], preferred_element_type=F32)
    c_ref[...] = jnp.dot(xb, w_ref[:, A_COLS + B_COLS:], preferred_element_type=F32).astype(BF16)


def _in_proj(x2d, w_p):
    t, d = x2d.shape
    tm = 512
    n = w_p.shape[1]
    return pl.pallas_call(
        _in_proj_body,
        grid=(t // tm,),
        in_specs=[pl.BlockSpec((tm, d), lambda i: (i, 0)),
                  pl.BlockSpec((d, n), lambda i: (0, 0))],
        out_specs=[pl.BlockSpec((tm, A_COLS), lambda i: (i, 0)),
                   pl.BlockSpec((tm, B_COLS), lambda i: (i, 0)),
                   pl.BlockSpec((tm, C_COLS), lambda i: (i, 0))],
        out_shape=[jax.ShapeDtypeStruct((t, A_COLS), BF16),
                   jax.ShapeDtypeStruct((t, B_COLS), F32),
                   jax.ShapeDtypeStruct((t, C_COLS), BF16)],
        compiler_params=_cparams("arbitrary"),
        name="in_proj",
    )(x2d, w_p)


def _band_attn_body(*refs, width, max_dist, has_sink, want_lse):
    q_ref, kp_ref, kc_ref, vp_ref, vc_ref, bias_ref = refs[:6]
    rest = refs[6:]
    if has_sink:
        sink_ref, rest = rest[0], rest[1:]
    o_ref = rest[0]
    lse_ref = rest[1] if want_lse else None

    blk = pl.program_id(2)
    row = lax.broadcasted_iota(I32, (BAND, 2 * BAND), 0)
    col = lax.broadcasted_iota(I32, (BAND, 2 * BAND), 1)
    dist = row + BAND - col
    mask = (dist >= 0) & (dist <= max_dist) & ((col >= BAND) | (blk > 0))
    low_half = lax.broadcasted_iota(I32, (BAND, LANES), 1) < HEAD_DIM
    scale = HEAD_DIM ** -0.5

    for tile in range(width // LANES):
        sl = slice(tile * LANES, (tile + 1) * LANES)
        q2 = q_ref[0, :, sl]
        k2 = jnp.concatenate([kp_ref[0, :, sl], kc_ref[0, :, sl]], axis=0)
        v2 = jnp.concatenate([vp_ref[0, :, sl], vc_ref[0, :, sl]], axis=0)
        outs, lses = [], []
        for half in range(2):
            h = 2 * tile + half
            qm = jnp.where(low_half if half == 0 else jnp.logical_not(low_half), q2, jnp.zeros_like(q2))
            s = lax.dot_general(qm, k2, (((1,), (1,)), ((), ())), preferred_element_type=F32)
            s = s * scale + bias_ref[h]
            s = jnp.where(mask, s, MASK_VALUE)
            m = jnp.max(s, axis=-1, keepdims=True)
            if has_sink:
                sink = sink_ref[h]
                m = jnp.maximum(m, sink)
            p = jnp.exp(s - m)
            den = jnp.sum(p, axis=-1, keepdims=True)
            if has_sink:
                den = den + jnp.exp(sink - m)
            pv = jnp.dot(p.astype(BF16), v2, preferred_element_type=F32)
            outs.append(pv / den)
            if want_lse:
                lses.append(jnp.broadcast_to(m + jnp.log(den), (BAND, LANES)))
        o_ref[0, :, sl] = jnp.where(low_half, outs[0], outs[1])
        if want_lse:
            lse_ref[0, :, sl] = jnp.where(low_half, lses[0], lses[1])


def _band_attn(src, bias, *, r, width, src_blocks, max_dist, sinks=None, want_lse):
    nb, length, _ = src.shape
    nblk = length // BAND
    heads = width // HEAD_DIM
    has_sink = sinks is not None

    def cur(off):
        return pl.BlockSpec((1, BAND, width), lambda b, p, i: (b, i, p * src_blocks + off))

    def prev(off):
        return pl.BlockSpec((1, BAND, width), lambda b, p, i: (b, jnp.maximum(i - 1, 0), p * src_blocks + off))

    in_specs = [cur(0), prev(1), cur(1), prev(2), cur(2),
                pl.BlockSpec((heads, BAND, 2 * BAND), lambda b, p, i: (0, 0, 0))]
    args = [src, src, src, src, src, bias]
    if has_sink:
        in_specs.append(pl.BlockSpec(memory_space=pltpu.SMEM))
        args.append(sinks)
    out_spec = pl.BlockSpec((1, BAND, width), lambda b, p, i: (b, i, p))
    out_sds = jax.ShapeDtypeStruct((nb, length, r * width), F32)
    body = functools.partial(_band_attn_body, width=width, max_dist=max_dist, has_sink=has_sink, want_lse=want_lse)
    return pl.pallas_call(
        body,
        grid=(nb, r, nblk),
        in_specs=in_specs,
        out_specs=[out_spec, out_spec] if want_lse else out_spec,
        out_shape=[out_sds, out_sds] if want_lse else out_sds,
        compiler_params=_cparams("arbitrary", "arbitrary", "arbitrary"),
        name="band_attn_r%d_w%d" % (r, width),
    )(*args)


def _hgrn_body(b_ref, lb_ref, ng_ref, w_ref, o_ref, state_ref):
    c = HG_CHUNK
    kd = B_KEY_DIM

    @pl.when(pl.program_id(1) == 0)
    def _():
        state_ref[...] = jnp.zeros_like(state_ref)

    trow = lax.broadcasted_iota(I32, (c, 1), 0)
    ti = lax.broadcasted_iota(I32, (c, c), 0)
    si = lax.broadcasted_iota(I32, (c, c), 1)
    nt = (((1,), (1,)), ((), ()))
    w_all = w_ref[...]

    for h in range(B_HEADS):
        ks = slice(h * kd, (h + 1) * kd)
        q = b_ref[0, :, ks]
        f = b_ref[0, :, B_KEY_WIDTH + h * kd:B_KEY_WIDTH + (h + 1) * kd]
        inp = b_ref[0, :, 2 * B_KEY_WIDTH + h * kd:2 * B_KEY_WIDTH + (h + 1) * kd]
        gate = b_ref[0, :, 3 * B_KEY_WIDTH + h * kd:3 * B_KEY_WIDTH + (h + 1) * kd]
        lb = lb_ref[:, ks]

        log_f = jnp.log(lb + (1.0 - lb) * jax.nn.sigmoid(f))
        key = (1.0 - lb) * jax.nn.sigmoid(-f)
        qs = q * jax.nn.sigmoid(q)

        g_hi = log_f.astype(BF16)
        g_lo = (log_f - g_hi.astype(F32)).astype(BF16)
        e2 = jnp.dot(w_all, jnp.concatenate([g_hi, g_lo], axis=1), preferred_element_type=F32)
        e = e2[:, :kd] + e2[:, kd:]

        cum = e[0:c]
        q_dec = (qs * jnp.exp(cum)).astype(BF16)
        k_dec = (key * jnp.exp(e[c:2 * c])).astype(BF16)
        inp_b = inp.astype(BF16)

        st = state_ref[h]
        inter = lax.dot_general(q_dec, st.astype(BF16), nt, preferred_element_type=F32)

        scores = jnp.where(ti == si,
                           lax.dot_general(qs.astype(BF16), key.astype(BF16), nt, preferred_element_type=F32), 0.0)
        m = c
        for lvl in range(HG_LEVELS):
            el = jnp.exp(e[(2 + lvl) * c:(3 + lvl) * c])
            second = (trow % m) >= (m // 2)
            ql = jnp.where(second, qs * el, 0.0).astype(BF16)
            kl = jnp.where(second, 0.0, key * el).astype(BF16)
            sl = lax.dot_general(ql, kl, nt, preferred_element_type=F32)
            if m < c:
                sl = jnp.where((ti // m) == (si // m), sl, 0.0)
            scores = scores + sl
            m //= 2
        intra = jnp.dot(scores.astype(BF16), inp_b, preferred_element_type=F32)

        new_st = st * jnp.exp(cum[c - 1:c]) + lax.dot_general(
            inp_b, k_dec, (((0,), (0,)), ((), ())), preferred_element_type=F32)
        state_ref[h] = new_st

        o = inter + intra
        ms = jnp.sum(o * o, axis=-1, keepdims=True) * (1.0 / B_VAL_DIM)
        o = o * lax.rsqrt(ms + 1e-6)
        o_ref[0, :, ks] = o * ng_ref[:, ks] * (gate * jax.nn.sigmoid(gate))


def _hgrn(b_all, lower_bound, norm_g_pad, w_sum):
    nb, s, _ = b_all.shape
    c = HG_CHUNK
    return pl.pallas_call(
        _hgrn_body,
        grid=(nb, s // c),
        in_specs=[pl.BlockSpec((1, c, B_COLS), lambda b, i: (b, i, 0)),
                  pl.BlockSpec((1, B_KEY_WIDTH), lambda b, i: (0, 0)),
                  pl.BlockSpec((1, B_PAD_WIDTH), lambda b, i: (0, 0)),
                  pl.BlockSpec(w_sum.shape, lambda b, i: (0, 0))],
        out_specs=pl.BlockSpec((1, c, B_PAD_WIDTH), lambda b, i: (b, i, 0)),
        out_shape=jax.ShapeDtypeStruct((nb, s, B_PAD_WIDTH), F32),
        scratch_shapes=[pltpu.VMEM((B_HEADS, B_VAL_PAD, B_KEY_DIM), F32)],
        compiler_params=_cparams("arbitrary", "arbitrary"),
        name="hgrn2",
    )(b_all, lower_bound, norm_g_pad, w_sum)


def _layer_norm_rows(z, g, b):
    mu = jnp.mean(z, axis=-1, keepdims=True)
    zc = z - mu
    var = jnp.mean(zc * zc, axis=-1, keepdims=True)
    return zc * lax.rsqrt(var + LN_EPS) * g + b


def _mix_out_body(o1, o2, o3, l1, l2, l3, ob, oc, x_ref, w_ref, g_ref, b_ref, out_ref):
    la, lb_, lc = l1[...], l2[...], l3[...]
    m = jnp.maximum(jnp.maximum(la, lb_), lc)
    wa, wb, wc = jnp.exp(la - m), jnp.exp(lb_ - m), jnp.exp(lc - m)
    oa = (wa * o1[...] + wb * o2[...] + wc * o3[...]) / (wa + wb + wc)
    cat = jnp.concatenate([oa, ob[...], oc[...]], axis=1).astype(BF16)
    y = jnp.dot(cat, w_ref[...], preferred_element_type=F32)
    z = DEEPNORM_ALPHA * x_ref[...] + y
    out_ref[...] = _layer_norm_rows(z, g_ref[...], b_ref[...])


def _mix_out(o_list, l_list, ob, oc, x2d, w_out_p, ln_g, ln_b):
    t, d = x2d.shape
    tm = 256
    row = lambda w: pl.BlockSpec((tm, w), lambda i: (i, 0))
    full = lambda a: pl.BlockSpec(a.shape, lambda i: (0, 0))
    return pl.pallas_call(
        _mix_out_body,
        grid=(t // tm,),
        in_specs=[row(A_WIDTH)] * 6 + [row(B_PAD_WIDTH), row(C_WIDTH), row(d), full(w_out_p), full(ln_g), full(ln_b)],
        out_specs=row(d),
        out_shape=jax.ShapeDtypeStruct((t, d), F32),
        compiler_params=_cparams("arbitrary"),
        name="mix_out_ln",
    )(*o_list, *l_list, ob, oc, x2d, w_out_p, ln_g, ln_b)


def _router_body(h_ref, rw_ref, bias_ref, tri_ref, e_ref, rank_ref, gate_ref, cnt_ref, carry_ref):
    tn = h_ref.shape[0]
    per_group = N_EXPERTS // N_GROUPS
    neg_inf = -jnp.inf

    @pl.when(pl.program_id(0) == 0)
    def _():
        carry_ref[...] = jnp.zeros_like(carry_ref)

    logits = lax.dot_general(rw_ref[...], h_ref[...], (((1,), (1,)), ((), ())),
                             precision=lax.Precision.HIGHEST, preferred_element_type=F32)
    scores = jax.nn.sigmoid(logits)
    choice = scores + bias_ref[...]

    def first_max(vals, idx, sentinel):
        top = jnp.max(vals, axis=0, keepdims=True)
        return top, jnp.min(jnp.where(vals == top, idx, sentinel), axis=0, keepdims=True)

    li = lax.broadcasted_iota(I32, (per_group, tn), 0).astype(F32)
    group_rows = []
    for g in range(N_GROUPS):
        cg = choice[g * per_group:(g + 1) * per_group]
        m1, first = first_max(cg, li, float(per_group))
        m2 = jnp.max(jnp.where(li == first, neg_inf, cg), axis=0, keepdims=True)
        group_rows.append(m1 + m2)
    group_score = jnp.concatenate(group_rows, axis=0)

    gi = lax.broadcasted_iota(I32, (N_GROUPS, tn), 0).astype(F32)
    group_ok = jnp.zeros((N_GROUPS, tn), F32)
    cur = group_score
    for _ in range(TOPK_GROUPS):
        _, first = first_max(cur, gi, float(N_GROUPS))
        pick = gi == first
        group_ok = jnp.where(pick, 1.0, group_ok)
        cur = jnp.where(pick, neg_inf, cur)

    cur = jnp.concatenate(
        [jnp.where(group_ok[g:g + 1] > 0.0, choice[g * per_group:(g + 1) * per_group], MASK_VALUE)
         for g in range(N_GROUPS)], axis=0)
    ei = lax.broadcasted_iota(I32, (N_EXPERTS, tn), 0).astype(F32)
    chosen = jnp.zeros((N_EXPERTS, tn), F32)
    picks, gates = [], []
    for _ in range(TOP_K):
        _, idx = first_max(cur, ei, float(N_EXPERTS))
        pick = ei == idx
        picks.append(idx)
        gates.append(jnp.sum(jnp.where(pick, scores, 0.0), axis=0, keepdims=True))
        chosen = jnp.where(pick, 1.0, chosen)
        cur = jnp.where(pick, neg_inf, cur)

    gate = jnp.concatenate(gates, axis=0)
    gate_ref[...] = gate / jnp.sum(gate, axis=0, keepdims=True) * ROUTED_SCALE
    e_ref[...] = jnp.concatenate(picks, axis=0).astype(I32)

    before = jnp.dot(chosen.astype(BF16), tri_ref[...], preferred_element_type=F32) + carry_ref[...]
    ranks = [jnp.sum(jnp.where(ei == idx, before, 0.0), axis=0, keepdims=True) for idx in picks]
    rank_ref[...] = jnp.concatenate(ranks, axis=0).astype(I32)
    carry = carry_ref[...] + jnp.sum(chosen, axis=1, keepdims=True)
    carry_ref[...] = carry
    cnt_ref[...] = carry.astype(I32)


def _router(h2d, rw_t, bias_col):
    t, d = h2d.shape
    tn = 256
    tri = jnp.asarray(np.triu(np.ones((tn, tn), np.float32), k=1), dtype=BF16)
    tok = lambda: pl.BlockSpec((TOP_K, tn), lambda i: (0, i))
    return pl.pallas_call(
        _router_body,
        grid=(t // tn,),
        in_specs=[pl.BlockSpec((tn, d), lambda i: (i, 0)),
                  pl.BlockSpec((N_EXPERTS, d), lambda i: (0, 0)),
                  pl.BlockSpec((N_EXPERTS, 1), lambda i: (0, 0)),
                  pl.BlockSpec((tn, tn), lambda i: (0, 0))],
        out_specs=[tok(), tok(), tok(), pl.BlockSpec((N_EXPERTS, 1), lambda i: (0, 0))],
        out_shape=[jax.ShapeDtypeStruct((TOP_K, t), I32),
                   jax.ShapeDtypeStruct((TOP_K, t), I32),
                   jax.ShapeDtypeStruct((TOP_K, t), F32),
                   jax.ShapeDtypeStruct((N_EXPERTS, 1), I32)],
        scratch_shapes=[pltpu.VMEM((N_EXPERTS, 1), F32)],
        compiler_params=_cparams("arbitrary"),
        name="moe_router",
    )(h2d, rw_t, bias_col, tri)


def _pack_bf16_pairs(x):
    w = x.shape[1] // 2
    hi = lax.bitcast_convert_type(x[:, :w].astype(BF16).astype(F32), U32)
    lo = lax.bitcast_convert_type(x[:, w:].astype(BF16).astype(F32), U32)
    return hi | (lo >> 16)


def _unpack_bf16_pairs(p):
    hi = lax.bitcast_convert_type(p & jnp.uint32(0xFFFF0000), F32)
    lo = lax.bitcast_convert_type(p << 16, F32)
    return hi, lo


def _dest_body(e_ref, rank_ref, offs_ref, dest_ref):
    tn = e_ref.shape[1]
    ei = lax.broadcasted_iota(I32, (N_EXPERTS, tn), 0)
    offs = offs_ref[...]
    rows = [jnp.sum(jnp.where(ei == e_ref[k:k + 1, :], offs, 0.0), axis=0, keepdims=True) for k in range(TOP_K)]
    dest_ref[...] = jnp.concatenate(rows, axis=0).astype(I32) + rank_ref[...]


def _dest_rows(e_t, rank_t, offsets):
    t = e_t.shape[1]
    tn = 512
    tok = pl.BlockSpec((TOP_K, tn), lambda i: (0, i))
    return pl.pallas_call(
        _dest_body,
        grid=(t // tn,),
        in_specs=[tok, tok, pl.BlockSpec((N_EXPERTS, 1), lambda i: (0, 0))],
        out_specs=tok,
        out_shape=jax.ShapeDtypeStruct((TOP_K, t), I32),
        compiler_params=_cparams("arbitrary"),
        name="moe_dest",
    )(e_t, rank_t, offsets.astype(F32).reshape(N_EXPERTS, 1))


def _dispatch_body(pad_end_ref, dest_ref, h_ref, xs_hbm, packed, zero_buf, sem):
    step = pl.program_id(0)
    tt = h_ref.shape[0]
    bm = zero_buf.shape[0]

    def tail_copy(e):
        end = pl.multiple_of(pad_end_ref[e], bm)
        return pltpu.make_async_copy(zero_buf, xs_hbm.at[pl.ds(end - bm, bm)], sem)

    def has_rows(e):
        prev = jnp.where(e > 0, pad_end_ref[jnp.maximum(e - 1, 0)], 0)
        return pad_end_ref[e] > prev

    @pl.when(step == 0)
    def _():
        zero_buf[...] = jnp.zeros_like(zero_buf)

        def start(e, carry):
            @pl.when(has_rows(e))
            def _():
                tail_copy(e).start()
            return carry

        def wait(e, carry):
            @pl.when(has_rows(e))
            def _():
                tail_copy(e).wait()
            return carry

        lax.fori_loop(0, N_EXPERTS, start, 0)
        lax.fori_loop(0, N_EXPERTS, wait, 0)

    packed[...] = _pack_bf16_pairs(h_ref[...])

    def start_rows(t, carry):
        src = packed.at[pl.ds(t, 1)]
        for k in range(TOP_K):
            pltpu.make_async_copy(src, xs_hbm.at[pl.ds(dest_ref[t * TOP_K + k], 1)], sem).start()
        return carry

    lax.fori_loop(0, tt, start_rows, 0)
    for k in range(TOP_K):
        pltpu.make_async_copy(packed, xs_hbm.at[pl.ds(0, tt)], sem).wait()


def _dispatch(h2d, dest_flat, pad_end, n_rows):
    t, d = h2d.shape
    tt = 512
    grid_spec = pltpu.PrefetchScalarGridSpec(
        num_scalar_prefetch=1,
        grid=(t // tt,),
        in_specs=[pl.BlockSpec((tt * TOP_K,), lambda i, pe: (i,), memory_space=pltpu.SMEM),
                  pl.BlockSpec((tt, d), lambda i, pe: (i, 0))],
        out_specs=pl.BlockSpec(memory_space=pl.ANY),
        scratch_shapes=[pltpu.VMEM((tt, d // 2), U32), pltpu.VMEM((EXPERT_BLOCK, d // 2), U32),
                        pltpu.SemaphoreType.DMA(())],
    )
    return pl.pallas_call(
        _dispatch_body,
        grid_spec=grid_spec,
        out_shape=jax.ShapeDtypeStruct((n_rows, d // 2), U32),
        compiler_params=_cparams("arbitrary"),
        name="moe_dispatch",
    )(pad_end, dest_flat, h2d)


def _expert_body(offs_ref, nblk_ref, layer_ref, xs_hbm, wg_ref, wu_ref, wd_ref, ys_hbm,
                 xbuf, ybuf, wg_b, wu_b, wd_b, sem_in, sem_out):
    e = pl.program_id(0)
    bm = xbuf.shape[1]
    n = nblk_ref[e]
    base = offs_ref[e]

    def rows(j):
        return pl.ds(pl.multiple_of(base + j * bm, bm), bm)

    def x_copy(j, slot):
        return pltpu.make_async_copy(xs_hbm.at[rows(j)], xbuf.at[slot], sem_in.at[slot])

    def y_copy(j, slot):
        return pltpu.make_async_copy(ybuf.at[slot], ys_hbm.at[rows(j)], sem_out.at[slot])

    @pl.when(n > 0)
    def _():
        x_copy(0, 0).start()

    wg_b[...] = wg_ref[0, 0].astype(BF16)
    wu_b[...] = wu_ref[0, 0].astype(BF16)
    wd_b[...] = wd_ref[0, 0].astype(BF16)

    def block(j, carry):
        slot = j % 2
        x_copy(j, slot).wait()

        @pl.when(j + 1 < n)
        def _():
            x_copy(j + 1, 1 - slot).start()

        @pl.when(j >= 2)
        def _():
            y_copy(j - 2, slot).wait()

        hi, lo = _unpack_bf16_pairs(xbuf[slot])
        xb = jnp.concatenate([hi.astype(BF16), lo.astype(BF16)], axis=1)
        g = jnp.dot(xb, wg_b[...], preferred_element_type=F32)
        u = jnp.dot(xb, wu_b[...], preferred_element_type=F32)
        hidden = (g * jax.nn.sigmoid(g) * u).astype(BF16)
        ybuf[slot] = _pack_bf16_pairs(jnp.dot(hidden, wd_b[...], preferred_element_type=F32))
        y_copy(j, slot).start()
        return carry

    lax.fori_loop(0, n, block, 0)

    @pl.when(n >= 2)
    def _():
        y_copy(n - 2, n % 2).wait()

    @pl.when(n >= 1)
    def _():
        y_copy(n - 1, (n - 1) % 2).wait()


def _experts(xs, offsets, n_blk, layer, w_gate, w_up, w_down):
    n_rows, dp = xs.shape
    bm = EXPERT_BLOCK
    d, ff = w_gate.shape[-2:]
    w_spec = lambda a, b: pl.BlockSpec((1, 1, a, b), lambda e, of, nb, ly: (ly[0], e, 0, 0))
    grid_spec = pltpu.PrefetchScalarGridSpec(
        num_scalar_prefetch=3,
        grid=(N_EXPERTS,),
        in_specs=[pl.BlockSpec(memory_space=pl.ANY), w_spec(d, ff), w_spec(d, ff), w_spec(ff, d)],
        out_specs=pl.BlockSpec(memory_space=pl.ANY),
        scratch_shapes=[pltpu.VMEM((2, bm, dp), U32), pltpu.VMEM((2, bm, dp), U32),
                        pltpu.VMEM((d, ff), BF16), pltpu.VMEM((d, ff), BF16), pltpu.VMEM((ff, d), BF16),
                        pltpu.SemaphoreType.DMA((2,)), pltpu.SemaphoreType.DMA((2,))],
    )
    return pl.pallas_call(
        _expert_body,
        grid_spec=grid_spec,
        out_shape=jax.ShapeDtypeStruct((n_rows, dp), U32),
        compiler_params=_cparams("arbitrary"),
        name="moe_experts",
    )(offsets, n_blk, layer, xs, w_gate, w_up, w_down)


def _combine_body(dest_ref, dest_next_ref, gate_ref, h_ref, ys_hbm, sg_ref, su_ref, sd_ref, g_ref, b_ref, out_ref,
                  buf_a, buf_b, sem_a, sem_b):
    step = pl.program_id(0)
    tt = buf_a.shape[1]
    half = h_ref.shape[1] // 2

    def issue(dref, first, buf, sem):
        for t in range(tt):
            for k in range(TOP_K):
                row = dref[(first + t) * TOP_K + k]
                pltpu.make_async_copy(ys_hbm.at[pl.ds(row, 1)], buf.at[k, pl.ds(t, 1)], sem).start()

    def drain(buf, sem):
        for k in range(TOP_K):
            pltpu.make_async_copy(ys_hbm.at[pl.ds(0, tt)], buf.at[k], sem).wait()

    def finish(first, buf):
        rows = pl.ds(first, tt)
        h = h_ref[rows, :]
        hb = h.astype(BF16)
        sg = jnp.dot(hb, sg_ref[...], preferred_element_type=F32)
        su = jnp.dot(hb, su_ref[...], preferred_element_type=F32)
        shared = jnp.dot((sg * jax.nn.sigmoid(sg) * su).astype(BF16), sd_ref[...], preferred_element_type=F32)
        z = DEEPNORM_ALPHA * h + shared
        z_hi, z_lo = z[:, :half], z[:, half:]
        for k in range(TOP_K):
            y_hi, y_lo = _unpack_bf16_pairs(buf[k])
            gate = gate_ref[rows, k:k + 1]
            z_hi = z_hi + gate * y_hi
            z_lo = z_lo + gate * y_lo
        out_ref[rows, :] = _layer_norm_rows(jnp.concatenate([z_hi, z_lo], axis=1), g_ref[...], b_ref[...])

    @pl.when(step == 0)
    def _():
        def start_rows(t, carry):
            for k in range(TOP_K):
                pltpu.make_async_copy(ys_hbm.at[pl.ds(dest_ref[t * TOP_K + k], 1)], buf_a.at[k, pl.ds(t, 1)],
                                      sem_a).start()
            return carry
        lax.fori_loop(0, tt, start_rows, 0)

    drain(buf_a, sem_a)
    issue(dest_ref, tt, buf_b, sem_b)
    finish(0, buf_a)
    issue(dest_next_ref, 0, buf_a, sem_a)
    drain(buf_b, sem_b)
    finish(tt, buf_b)

    @pl.when(step == pl.num_programs(0) - 1)
    def _():
        drain(buf_a, sem_a)


def _combine(dest_flat, gate_tk, h2d, ys, sh_gate, sh_up, sh_down, ln_g, ln_b):
    t, d = h2d.shape
    tt = 128
    steps = t // (2 * tt)
    full = lambda a: pl.BlockSpec(a.shape, lambda i: (0, 0))
    return pl.pallas_call(
        _combine_body,
        grid=(steps,),
        in_specs=[pl.BlockSpec((2 * tt * TOP_K,), lambda i: (i,), memory_space=pltpu.SMEM),
                  pl.BlockSpec((2 * tt * TOP_K,), lambda i: (jnp.minimum(i + 1, steps - 1),), memory_space=pltpu.SMEM),
                  pl.BlockSpec((2 * tt, TOP_K), lambda i: (i, 0)),
                  pl.BlockSpec((2 * tt, d), lambda i: (i, 0)),
                  pl.BlockSpec(memory_space=pl.ANY),
                  full(sh_gate), full(sh_up), full(sh_down), full(ln_g), full(ln_b)],
        out_specs=pl.BlockSpec((2 * tt, d), lambda i: (i, 0)),
        out_shape=jax.ShapeDtypeStruct((t, d), F32),
        scratch_shapes=[pltpu.VMEM((TOP_K, tt, d // 2), U32), pltpu.VMEM((TOP_K, tt, d // 2), U32),
                        pltpu.SemaphoreType.DMA(()), pltpu.SemaphoreType.DMA(())],
        compiler_params=_cparams("arbitrary"),
        name="moe_combine_ln",
    )(dest_flat, dest_flat, gate_tk, h2d, ys, sh_gate, sh_up, sh_down, ln_g, ln_b)


def _mixer_sublayer(h2d, nb, s, w_in_p, w_out_p, a_biases, c_bias, lower_bound, norm_g_pad, sinks, w_sum, ln_g, ln_b):
    a_qkv, b_all, c_qkv = _in_proj(h2d, w_in_p)
    o_list, l_list = [], []
    for (window, r), bias in zip(A_PATTERNS, a_biases):
        src = a_qkv.reshape(nb, s // r, r * A_COLS)
        o, lse = _band_attn(src, bias, r=r, width=A_WIDTH, src_blocks=3, max_dist=window // r, want_lse=True)
        o_list.append(o.reshape(nb * s, A_WIDTH))
        l_list.append(lse.reshape(nb * s, A_WIDTH))
    oc = _band_attn(c_qkv.reshape(nb, s, C_COLS), c_bias, r=1, width=C_WIDTH, src_blocks=3,
                    max_dist=C_WINDOW - 1, sinks=sinks, want_lse=False).reshape(nb * s, C_WIDTH)
    ob = _hgrn(b_all.reshape(nb, s, B_COLS), lower_bound, norm_g_pad, w_sum).reshape(nb * s, B_PAD_WIDTH)
    return _mix_out(o_list, l_list, ob, oc, h2d, w_out_p, ln_g, ln_b)


def _moe_sublayer(h2d, layer, router_w, router_bias, w_gate, w_up, w_down, sh_gate, sh_up, sh_down, ln_g, ln_b):
    t, d = h2d.shape
    bm = EXPERT_BLOCK
    e_t, rank_t, gate_t, counts = _router(h2d, router_w.T.astype(F32), router_bias.astype(F32).reshape(N_EXPERTS, 1))

    counts = counts.reshape(N_EXPERTS)
    padded = (counts + bm - 1) // bm * bm
    pad_end = jnp.cumsum(padded).astype(I32)
    offsets = pad_end - padded
    n_blocks = -(-(t * TOP_K + N_EXPERTS * (bm - 1)) // bm)

    dest_flat = _dest_rows(e_t, rank_t, offsets).T.reshape(t * TOP_K)
    xs = _dispatch(h2d, dest_flat, pad_end, n_blocks * bm)
    ys = _experts(xs, offsets, (padded // bm).astype(I32), jnp.full((1,), layer, I32), w_gate, w_up, w_down)
    return _combine(dest_flat, gate_t.T, h2d, ys,
                    sh_gate.astype(BF16), sh_up.astype(BF16), sh_down.astype(BF16), ln_g, ln_b)


def kernel(x, w_in, w_out, rel_bias_table, lower_bound_logits, hgrn_norm_g, attn_sinks, ln1_g, ln1_b, router_w, router_bias, expert_w_gate, expert_w_up, expert_w_down, shared_w_gate, shared_w_up, shared_w_down, ln2_g, ln2_b):
    nb, s, d = x.shape
    depth = w_in.shape[0]
    lb_probs = jax.nn.softmax(lower_bound_logits.astype(F32), axis=0)
    lower_bounds = jnp.cumsum(lb_probs, axis=0) - lb_probs[0]
    rel_table = rel_bias_table.astype(F32)
    a_biases = [_band_bias(rel_table, r, 0, A_HEADS) for _, r in A_PATTERNS]
    c_bias = _band_bias(rel_table, 1, A_HEADS, A_HEADS + C_HEADS)
    w_sum = jnp.asarray(_hgrn_sum_matrix(), dtype=BF16)
    row = lambda v: v.astype(F32).reshape(1, -1)

    h = x.astype(F32).reshape(nb * s, d)
    for l in range(depth):
        h = _mixer_sublayer(h, nb, s, _prep_w_in(w_in[l]), _prep_w_out(w_out[l].astype(F32)), a_biases, c_bias,
                            lower_bounds[l].reshape(1, B_KEY_WIDTH), _pad_heads_vec(hgrn_norm_g[l].astype(F32)),
                            attn_sinks[l].astype(F32), w_sum, row(ln1_g[l]), row(ln1_b[l]))
        h = _moe_sublayer(h, l, router_w[l], router_bias[l], expert_w_gate, expert_w_up, expert_w_down,
                          shared_w_gate[l], shared_w_up[l], shared_w_down[l], row(ln2_g[l]), row(ln2_b[l]))
    return h.reshape(nb, s, d).astype(x.dtype)
```

```python
import functools
import math

import numpy as np
import jax
import jax.numpy as jnp
from jax import lax
from jax.experimental import pallas as pl
from jax.experimental.pallas import tpu as pltpu

F32 = jnp.float32
BF16 = jnp.bfloat16
I32 = jnp.int32
U32 = jnp.uint32

LANES = 128
SUBLANES = 8
VMEM_LIMIT = 56 * 1024 * 1024

D_MODEL = 1024
DEPTH = 2
HEAD_DIM = 64
BAND = 128
MASK_VALUE = -1e30

A_HEADS = 6
A_PATTERNS = ((128, 1), (512, 4), (2048, 16))
B_HEADS = 4
B_KEY_DIM = 128
B_VAL_DIM = 96
B_VAL_PAD = 128
HG_CHUNK = 64
C_HEADS = 4
C_KV_HEADS = 2
C_WINDOW = 128

A_WIDTH = A_HEADS * HEAD_DIM
B_KEY_WIDTH = B_HEADS * B_KEY_DIM
B_WIDTH = B_HEADS * B_VAL_DIM
B_PAD_WIDTH = B_HEADS * B_VAL_PAD
C_WIDTH = C_HEADS * HEAD_DIM
C_KV_WIDTH = C_KV_HEADS * HEAD_DIM
IN_SPLITS = (A_WIDTH, A_WIDTH, A_WIDTH, B_KEY_WIDTH, B_KEY_WIDTH, B_WIDTH, B_WIDTH, C_WIDTH, C_KV_WIDTH, C_KV_WIDTH)

A_COLS = 3 * A_WIDTH
B_COLS = 4 * B_KEY_WIDTH
C_COLS = 3 * C_WIDTH
MIX_PAD_WIDTH = A_WIDTH + B_PAD_WIDTH + C_WIDTH

REL_BUCKETS = 32
REL_MAX_DIST = 2048

N_EXPERTS = 256
TOP_K = 8
N_GROUPS = 8
TOPK_GROUPS = 4
EXPERT_FF = 256
SHARED_FF = 256
ROUTED_SCALE = 2.5
EXPERT_BLOCK = 256

DEEPNORM_ALPHA = (2 * DEPTH) ** 0.25
LN_EPS = 1e-5


def _cparams(*sem):
    return pltpu.CompilerParams(dimension_semantics=sem, vmem_limit_bytes=VMEM_LIMIT)


def _prep_w_in(w_in):
    d = w_in.shape[0]
    split_at = [int(i) for i in np.cumsum(IN_SPLITS)[:-1]]
    aq, ak, av, bq, bf, bi, bg, cq, ck, cv = jnp.split(w_in, split_at, axis=-1)
    pad_v = lambda w: jnp.pad(w.reshape(d, B_HEADS, B_VAL_DIM), ((0, 0), (0, 0), (0, B_VAL_PAD - B_VAL_DIM))).reshape(d, B_PAD_WIDTH)
    rep = lambda w: jnp.repeat(w.reshape(d, C_KV_HEADS, HEAD_DIM), C_HEADS // C_KV_HEADS, axis=1).reshape(d, C_WIDTH)
    cols = [aq, ak, av, bq, bf, pad_v(bi), pad_v(bg), cq, rep(ck), rep(cv)]
    return jnp.concatenate(cols, axis=-1).astype(BF16)


def _prep_w_out(w_out):
    d = w_out.shape[1]
    wa = w_out[:A_WIDTH]
    wb = w_out[A_WIDTH:A_WIDTH + B_WIDTH].reshape(B_HEADS, B_VAL_DIM, d)
    wb = jnp.pad(wb, ((0, 0), (0, B_VAL_PAD - B_VAL_DIM), (0, 0))).reshape(B_PAD_WIDTH, d)
    wc = w_out[A_WIDTH + B_WIDTH:]
    return jnp.concatenate([wa, wb, wc], axis=0).astype(BF16)


def _pad_heads_vec(v):
    return jnp.pad(v.reshape(B_HEADS, B_VAL_DIM), ((0, 0), (0, B_VAL_PAD - B_VAL_DIM))).reshape(1, B_PAD_WIDTH)


def _rel_bucket(dist):
    max_exact = REL_BUCKETS // 2
    d = jnp.maximum(dist, 0)
    log_ratio = jnp.log(jnp.maximum(d, max_exact).astype(F32) / max_exact) / math.log(REL_MAX_DIST / max_exact)
    large = jnp.minimum(max_exact + (log_ratio * (REL_BUCKETS - max_exact)).astype(I32), REL_BUCKETS - 1)
    return jnp.where(d < max_exact, d, large)


def _band_bias(rel_table, r, head_lo, head_hi):
    dist = jnp.arange(BAND)[:, None] + BAND - jnp.arange(2 * BAND)[None, :]
    onehot = jax.nn.one_hot(_rel_bucket(dist * r), REL_BUCKETS, dtype=F32)
    return jnp.einsum("qkb,bh->hqk", onehot, rel_table[:, head_lo:head_hi], precision=lax.Precision.HIGHEST)


def _hgrn_sum_matrix():
    c = HG_CHUNK
    t = np.arange(c)[:, None]
    u = np.arange(c)[None, :]
    blocks = [(u <= t), (u > t)]
    m = c
    while m >= 2:
        mid = (t // m) * m + m // 2
        second = t >= mid
        blocks.append(np.where(second, (u >= mid) & (u <= t), (u > t) & (u < mid)))
        m //= 2
    return np.concatenate(blocks, axis=0).astype(np.float32)


HG_LEVELS = int(math.log2(HG_CHUNK))


def _in_proj_body(x_ref, w_ref, a_ref, b_ref, c_ref):
    xb = x_ref[...].astype(BF16)
    a_ref[...] = jnp.dot(xb, w_ref[:, :A_COLS], preferred_element_type=F32).astype(BF16)
    for j in range(B_COLS // B_KEY_WIDTH):
        lo = A_COLS + j * B_KEY_WIDTH
        b_ref[:, j * B_KEY_WIDTH:(j + 1) * B_KEY_WIDTH] = jnp.dot(
            xb, w_ref[:, lo:lo + B_KEY_WIDTH], preferred_element_type=F32)
    c_ref[...] = jnp.dot(xb, w_ref[:, A_COLS + B_COLS:], preferred_element_type=F32).astype(BF16)


def _in_proj(x2d, w_p):
    t, d = x2d.shape
    tm = 512
    n = w_p.shape[1]
    return pl.pallas_call(
        _in_proj_body,
        grid=(t // tm,),
        in_specs=[pl.BlockSpec((tm, d), lambda i: (i, 0)),
                  pl.BlockSpec((d, n), lambda i: (0, 0))],
        out_specs=[pl.BlockSpec((tm, A_COLS), lambda i: (i, 0)),
                   pl.BlockSpec((tm, B_COLS), lambda i: (i, 0)),
                   pl.BlockSpec((tm, C_COLS), lambda i: (i, 0))],
        out_shape=[jax.ShapeDtypeStruct((t, A_COLS), BF16),
                   jax.ShapeDtypeStruct((t, B_COLS), F32),
                   jax.ShapeDtypeStruct((t, C_COLS), BF16)],
        compiler_params=_cparams("arbitrary"),
        name="in_proj",
    )(x2d, w_p)


def _band_attn_body(*refs, width, max_dist, has_sink, want_lse):
    q_ref, kp_ref, kc_ref, vp_ref, vc_ref, bias_ref = refs[:6]
    rest = refs[6:]
    if has_sink:
        sink_ref, rest = rest[0], rest[1:]
    o_ref = rest[0]
    lse_ref = rest[1] if want_lse else None

    blk = pl.program_id(2)
    row = lax.broadcasted_iota(I32, (BAND, 2 * BAND), 0)
    col = lax.broadcasted_iota(I32, (BAND, 2 * BAND), 1)
    dist = row + BAND - col
    mask = (dist >= 0) & (dist <= max_dist) & ((col >= BAND) | (blk > 0))
    low_half = lax.broadcasted_iota(I32, (BAND, LANES), 1) < HEAD_DIM
    scale = HEAD_DIM ** -0.5

    for tile in range(width // LANES):
        sl = slice(tile * LANES, (tile + 1) * LANES)
        q2 = q_ref[0, :, sl]
        k2 = jnp.concatenate([kp_ref[0, :, sl], kc_ref[0, :, sl]], axis=0)
        v2 = jnp.concatenate([vp_ref[0, :, sl], vc_ref[0, :, sl]], axis=0)
        outs, lses = [], []
        for half in range(2):
            h = 2 * tile + half
            qm = jnp.where(low_half if half == 0 else jnp.logical_not(low_half), q2, jnp.zeros_like(q2))
            s = lax.dot_general(qm, k2, (((1,), (1,)), ((), ())), preferred_element_type=F32)
            s = s * scale + bias_ref[h]
            s = jnp.where(mask, s, MASK_VALUE)
            m = jnp.max(s, axis=-1, keepdims=True)
            if has_sink:
                sink = sink_ref[h]
                m = jnp.maximum(m, sink)
            p = jnp.exp(s - m)
            den = jnp.sum(p, axis=-1, keepdims=True)
            if has_sink:
                den = den + jnp.exp(sink - m)
            pv = jnp.dot(p.astype(BF16), v2, preferred_element_type=F32)
            outs.append(pv / den)
            if want_lse:
                lses.append(jnp.broadcast_to(m + jnp.log(den), (BAND, LANES)))
        o_ref[0, :, sl] = jnp.where(low_half, outs[0], outs[1])
        if want_lse:
            lse_ref[0, :, sl] = jnp.where(low_half, lses[0], lses[1])


def _band_attn(src, bias, *, r, width, src_blocks, max_dist, sinks=None, want_lse):
    nb, length, _ = src.shape
    nblk = length // BAND
    heads = width // HEAD_DIM
    has_sink = sinks is not None

    def cur(off):
        return pl.BlockSpec((1, BAND, width), lambda b, p, i: (b, i, p * src_blocks + off))

    def prev(off):
        return pl.BlockSpec((1, BAND, width), lambda b, p, i: (b, jnp.maximum(i - 1, 0), p * src_blocks + off))

    in_specs = [cur(0), prev(1), cur(1), prev(2), cur(2),
                pl.BlockSpec((heads, BAND, 2 * BAND), lambda b, p, i: (0, 0, 0))]
    args = [src, src, src, src, src, bias]
    if has_sink:
        in_specs.append(pl.BlockSpec(memory_space=pltpu.SMEM))
        args.append(sinks)
    out_spec = pl.BlockSpec((1, BAND, width), lambda b, p, i: (b, i, p))
    out_sds = jax.ShapeDtypeStruct((nb, length, r * width), F32)
    body = functools.partial(_band_attn_body, width=width, max_dist=max_dist, has_sink=has_sink, want_lse=want_lse)
    return pl.pallas_call(
        body,
        grid=(nb, r, nblk),
        in_specs=in_specs,
        out_specs=[out_spec, out_spec] if want_lse else out_spec,
        out_shape=[out_sds, out_sds] if want_lse else out_sds,
        compiler_params=_cparams("arbitrary", "arbitrary", "arbitrary"),
        name="band_attn_r%d_w%d" % (r, width),
    )(*args)


def _hgrn_body(b_ref, lb_ref, ng_ref, w_ref, o_ref, state_ref):
    c = HG_CHUNK
    kd = B_KEY_DIM

    @pl.when(pl.program_id(1) == 0)
    def _():
        state_ref[...] = jnp.zeros_like(state_ref)

    trow = lax.broadcasted_iota(I32, (c, 1), 0)
    ti = lax.broadcasted_iota(I32, (c, c), 0)
    si = lax.broadcasted_iota(I32, (c, c), 1)
    nt = (((1,), (1,)), ((), ()))
    w_all = w_ref[...]

    for h in range(B_HEADS):
        ks = slice(h * kd, (h + 1) * kd)
        q = b_ref[0, :, ks]
        f = b_ref[0, :, B_KEY_WIDTH + h * kd:B_KEY_WIDTH + (h + 1) * kd]
        inp = b_ref[0, :, 2 * B_KEY_WIDTH + h * kd:2 * B_KEY_WIDTH + (h + 1) * kd]
        gate = b_ref[0, :, 3 * B_KEY_WIDTH + h * kd:3 * B_KEY_WIDTH + (h + 1) * kd]
        lb = lb_ref[:, ks]

        log_f = jnp.log(lb + (1.0 - lb) * jax.nn.sigmoid(f))
        key = (1.0 - lb) * jax.nn.sigmoid(-f)
        qs = q * jax.nn.sigmoid(q)

        g_hi = log_f.astype(BF16)
        g_lo = (log_f - g_hi.astype(F32)).astype(BF16)
        e2 = jnp.dot(w_all, jnp.concatenate([g_hi, g_lo], axis=1), preferred_element_type=F32)
        e = e2[:, :kd] + e2[:, kd:]

        cum = e[0:c]
        q_dec = (qs * jnp.exp(cum)).astype(BF16)
        k_dec = (key * jnp.exp(e[c:2 * c])).astype(BF16)
        inp_b = inp.astype(BF16)

        st = state_ref[h]
        inter = lax.dot_general(q_dec, st.astype(BF16), nt, preferred_element_type=F32)

        scores = jnp.where(ti == si,
                           lax.dot_general(qs.astype(BF16), key.astype(BF16), nt, preferred_element_type=F32), 0.0)
        m = c
        for lvl in range(HG_LEVELS):
            el = jnp.exp(e[(2 + lvl) * c:(3 + lvl) * c])
            second = (trow % m) >= (m // 2)
            ql = jnp.where(second, qs * el, 0.0).astype(BF16)
            kl = jnp.where(second, 0.0, key * el).astype(BF16)
            sl = lax.dot_general(ql, kl, nt, preferred_element_type=F32)
            if m < c:
                sl = jnp.where((ti // m) == (si // m), sl, 0.0)
            scores = scores + sl
            m //= 2
        intra = jnp.dot(scores.astype(BF16), inp_b, preferred_element_type=F32)

        new_st = st * jnp.exp(cum[c - 1:c]) + lax.dot_general(
            inp_b, k_dec, (((0,), (0,)), ((), ())), preferred_element_type=F32)
        state_ref[h] = new_st

        o = inter + intra
        ms = jnp.sum(o * o, axis=-1, keepdims=True) * (1.0 / B_VAL_DIM)
        o = o * lax.rsqrt(ms + 1e-6)
        o_ref[0, :, ks] = o * ng_ref[:, ks] * (gate * jax.nn.sigmoid(gate))


def _hgrn(b_all, lower_bound, norm_g_pad, w_sum):
    nb, s, _ = b_all.shape
    c = HG_CHUNK
    return pl.pallas_call(
        _hgrn_body,
        grid=(nb, s // c),
        in_specs=[pl.BlockSpec((1, c, B_COLS), lambda b, i: (b, i, 0)),
                  pl.BlockSpec((1, B_KEY_WIDTH), lambda b, i: (0, 0)),
                  pl.BlockSpec((1, B_PAD_WIDTH), lambda b, i: (0, 0)),
                  pl.BlockSpec(w_sum.shape, lambda b, i: (0, 0))],
        out_specs=pl.BlockSpec((1, c, B_PAD_WIDTH), lambda b, i: (b, i, 0)),
        out_shape=jax.ShapeDtypeStruct((nb, s, B_PAD_WIDTH), F32),
        scratch_shapes=[pltpu.VMEM((B_HEADS, B_VAL_PAD, B_KEY_DIM), F32)],
        compiler_params=_cparams("arbitrary", "arbitrary"),
        name="hgrn2",
    )(b_all, lower_bound, norm_g_pad, w_sum)


def _layer_norm_rows(z, g, b):
    mu = jnp.mean(z, axis=-1, keepdims=True)
    zc = z - mu
    var = jnp.mean(zc * zc, axis=-1, keepdims=True)
    return zc * lax.rsqrt(var + LN_EPS) * g + b


def _mix_out_body(o1, o2, o3, l1, l2, l3, ob, oc, x_ref, w_ref, g_ref, b_ref, out_ref):
    la, lb_, lc = l1[...], l2[...], l3[...]
    m = jnp.maximum(jnp.maximum(la, lb_), lc)
    wa, wb, wc = jnp.exp(la - m), jnp.exp(lb_ - m), jnp.exp(lc - m)
    oa = (wa * o1[...] + wb * o2[...] + wc * o3[...]) / (wa + wb + wc)
    cat = jnp.concatenate([oa, ob[...], oc[...]], axis=1).astype(BF16)
    y = jnp.dot(cat, w_ref[...], preferred_element_type=F32)
    z = DEEPNORM_ALPHA * x_ref[...] + y
    out_ref[...] = _layer_norm_rows(z, g_ref[...], b_ref[...])


def _mix_out(o_list, l_list, ob, oc, x2d, w_out_p, ln_g, ln_b):
    t, d = x2d.shape
    tm = 256
    row = lambda w: pl.BlockSpec((tm, w), lambda i: (i, 0))
    full = lambda a: pl.BlockSpec(a.shape, lambda i: (0, 0))
    return pl.pallas_call(
        _mix_out_body,
        grid=(t // tm,),
        in_specs=[row(A_WIDTH)] * 6 + [row(B_PAD_WIDTH), row(C_WIDTH), row(d), full(w_out_p), full(ln_g), full(ln_b)],
        out_specs=row(d),
        out_shape=jax.ShapeDtypeStruct((t, d), F32),
        compiler_params=_cparams("arbitrary"),
        name="mix_out_ln",
    )(*o_list, *l_list, ob, oc, x2d, w_out_p, ln_g, ln_b)


def _router_body(h_ref, rw_ref, bias_ref, tri_ref, e_ref, rank_ref, gate_ref, cnt_ref, carry_ref):
    tn = h_ref.shape[0]
    per_group = N_EXPERTS // N_GROUPS
    neg_inf = -jnp.inf

    @pl.when(pl.program_id(0) == 0)
    def _():
        carry_ref[...] = jnp.zeros_like(carry_ref)

    logits = lax.dot_general(rw_ref[...], h_ref[...], (((1,), (1,)), ((), ())),
                             precision=lax.Precision.HIGHEST, preferred_element_type=F32)
    scores = jax.nn.sigmoid(logits)
    choice = scores + bias_ref[...]

    def first_max(vals, idx, sentinel):
        top = jnp.max(vals, axis=0, keepdims=True)
        return top, jnp.min(jnp.where(vals == top, idx, sentinel), axis=0, keepdims=True)

    li = lax.broadcasted_iota(I32, (per_group, tn), 0).astype(F32)
    group_rows = []
    for g in range(N_GROUPS):
        cg = choice[g * per_group:(g + 1) * per_group]
        m1, first = first_max(cg, li, float(per_group))
        m2 = jnp.max(jnp.where(li == first, neg_inf, cg), axis=0, keepdims=True)
        group_rows.append(m1 + m2)
    group_score = jnp.concatenate(group_rows, axis=0)

    gi = lax.broadcasted_iota(I32, (N_GROUPS, tn), 0).astype(F32)
    group_ok = jnp.zeros((N_GROUPS, tn), F32)
    cur = group_score
    for _ in range(TOPK_GROUPS):
        _, first = first_max(cur, gi, float(N_GROUPS))
        pick = gi == first
        group_ok = jnp.where(pick, 1.0, group_ok)
        cur = jnp.where(pick, neg_inf, cur)

    cur = jnp.concatenate(
        [jnp.where(group_ok[g:g + 1] > 0.0, choice[g * per_group:(g + 1) * per_group], MASK_VALUE)
         for g in range(N_GROUPS)], axis=0)
    ei = lax.broadcasted_iota(I32, (N_EXPERTS, tn), 0).astype(F32)
    chosen = jnp.zeros((N_EXPERTS, tn), F32)
    picks, gates = [], []
    for _ in range(TOP_K):
        _, idx = first_max(cur, ei, float(N_EXPERTS))
        pick = ei == idx
        picks.append(idx)
        gates.append(jnp.sum(jnp.where(pick, scores, 0.0), axis=0, keepdims=True))
        chosen = jnp.where(pick, 1.0, chosen)
        cur = jnp.where(pick, neg_inf, cur)

    gate = jnp.concatenate(gates, axis=0)
    gate_ref[...] = gate / jnp.sum(gate, axis=0, keepdims=True) * ROUTED_SCALE
    e_ref[...] = jnp.concatenate(picks, axis=0).astype(I32)

    before = jnp.dot(chosen.astype(BF16), tri_ref[...], preferred_element_type=F32) + carry_ref[...]
    ranks = [jnp.sum(jnp.where(ei == idx, before, 0.0), axis=0, keepdims=True) for idx in picks]
    rank_ref[...] = jnp.concatenate(ranks, axis=0).astype(I32)
    carry = carry_ref[...] + jnp.sum(chosen, axis=1, keepdims=True)
    carry_ref[...] = carry
    cnt_ref[...] = carry.astype(I32)


def _router(h2d, rw_t, bias_col):
    t, d = h2d.shape
    tn = 256
    tri = jnp.asarray(np.triu(np.ones((tn, tn), np.float32), k=1), dtype=BF16)
    tok = lambda: pl.BlockSpec((TOP_K, tn), lambda i: (0, i))
    return pl.pallas_call(
        _router_body,
        grid=(t // tn,),
        in_specs=[pl.BlockSpec((tn, d), lambda i: (i, 0)),
                  pl.BlockSpec((N_EXPERTS, d), lambda i: (0, 0)),
                  pl.BlockSpec((N_EXPERTS, 1), lambda i: (0, 0)),
                  pl.BlockSpec((tn, tn), lambda i: (0, 0))],
        out_specs=[tok(), tok(), tok(), pl.BlockSpec((N_EXPERTS, 1), lambda i: (0, 0))],
        out_shape=[jax.ShapeDtypeStruct((TOP_K, t), I32),
                   jax.ShapeDtypeStruct((TOP_K, t), I32),
                   jax.ShapeDtypeStruct((TOP_K, t), F32),
                   jax.ShapeDtypeStruct((N_EXPERTS, 1), I32)],
        scratch_shapes=[pltpu.VMEM((N_EXPERTS, 1), F32)],
        compiler_params=_cparams("arbitrary"),
        name="moe_router",
    )(h2d, rw_t, bias_col, tri)


def _pack_bf16_pairs(x):
    w = x.shape[1] // 2
    hi = lax.bitcast_convert_type(x[:, :w].astype(BF16).astype(F32), U32)
    lo = lax.bitcast_convert_type(x[:, w:].astype(BF16).astype(F32), U32)
    return hi | (lo >> 16)


def _unpack_bf16_pairs(p):
    hi = lax.bitcast_convert_type(p & jnp.uint32(0xFFFF0000), F32)
    lo = lax.bitcast_convert_type(p << 16, F32)
    return hi, lo


def _dest_body(e_ref, rank_ref, offs_ref, dest_ref):
    tn = e_ref.shape[1]
    ei = lax.broadcasted_iota(I32, (N_EXPERTS, tn), 0)
    offs = offs_ref[...]
    rows = [jnp.sum(jnp.where(ei == e_ref[k:k + 1, :], offs, 0.0), axis=0, keepdims=True) for k in range(TOP_K)]
    dest_ref[...] = jnp.concatenate(rows, axis=0).astype(I32) + rank_ref[...]


def _dest_rows(e_t, rank_t, offsets):
    t = e_t.shape[1]
    tn = 512
    tok = pl.BlockSpec((TOP_K, tn), lambda i: (0, i))
    return pl.pallas_call(
        _dest_body,
        grid=(t // tn,),
        in_specs=[tok, tok, pl.BlockSpec((N_EXPERTS, 1), lambda i: (0, 0))],
        out_specs=tok,
        out_shape=jax.ShapeDtypeStruct((TOP_K, t), I32),
        compiler_params=_cparams("arbitrary"),
        name="moe_dest",
    )(e_t, rank_t, offsets.astype(F32).reshape(N_EXPERTS, 1))


def _dispatch_body(pad_end_ref, dest_ref, h_ref, xs_hbm, packed, zero_buf, sem):
    step = pl.program_id(0)
    tt = h_ref.shape[0]
    bm = zero_buf.shape[0]

    def tail_copy(e):
        end = pl.multiple_of(pad_end_ref[e], bm)
        return pltpu.make_async_copy(zero_buf, xs_hbm.at[pl.ds(end - bm, bm)], sem)

    def has_rows(e):
        prev = jnp.where(e > 0, pad_end_ref[jnp.maximum(e - 1, 0)], 0)
        return pad_end_ref[e] > prev

    @pl.when(step == 0)
    def _():
        zero_buf[...] = jnp.zeros_like(zero_buf)

        def start(e, carry):
            @pl.when(has_rows(e))
            def _():
                tail_copy(e).start()
            return carry

        def wait(e, carry):
            @pl.when(has_rows(e))
            def _():
                tail_copy(e).wait()
            return carry

        lax.fori_loop(0, N_EXPERTS, start, 0)
        lax.fori_loop(0, N_EXPERTS, wait, 0)

    packed[...] = _pack_bf16_pairs(h_ref[...])

    def start_rows(t, carry):
        src = packed.at[pl.ds(t, 1)]
        for k in range(TOP_K):
            pltpu.make_async_copy(src, xs_hbm.at[pl.ds(dest_ref[t * TOP_K + k], 1)], sem).start()
        return carry

    lax.fori_loop(0, tt, start_rows, 0)
    for k in range(TOP_K):
        pltpu.make_async_copy(packed, xs_hbm.at[pl.ds(0, tt)], sem).wait()


def _dispatch(h2d, dest_flat, pad_end, n_rows):
    t, d = h2d.shape
    tt = 512
    grid_spec = pltpu.PrefetchScalarGridSpec(
        num_scalar_prefetch=1,
        grid=(t // tt,),
        in_specs=[pl.BlockSpec((tt * TOP_K,), lambda i, pe: (i,), memory_space=pltpu.SMEM),
                  pl.BlockSpec((tt, d), lambda i, pe: (i, 0))],
        out_specs=pl.BlockSpec(memory_space=pl.ANY),
        scratch_shapes=[pltpu.VMEM((tt, d // 2), U32), pltpu.VMEM((EXPERT_BLOCK, d // 2), U32),
                        pltpu.SemaphoreType.DMA(())],
    )
    return pl.pallas_call(
        _dispatch_body,
        grid_spec=grid_spec,
        out_shape=jax.ShapeDtypeStruct((n_rows, d // 2), U32),
        compiler_params=_cparams("arbitrary"),
        name="moe_dispatch",
    )(pad_end, dest_flat, h2d)


def _expert_body(offs_ref, nblk_ref, layer_ref, xs_hbm, wg_ref, wu_ref, wd_ref, ys_hbm,
                 xbuf, ybuf, wg_b, wu_b, wd_b, sem_in, sem_out):
    e = pl.program_id(0)
    bm = xbuf.shape[1]
    n = nblk_ref[e]
    first = offs_ref[e] // bm
    total = (offs_ref[N_EXPERTS - 1] // bm) + nblk_ref[N_EXPERTS - 1]

    def rows(g):
        return pl.ds(pl.multiple_of(g * bm, bm), bm)

    def x_copy(g, slot):
        return pltpu.make_async_copy(xs_hbm.at[rows(g)], xbuf.at[slot], sem_in.at[slot])

    def y_copy(g, slot):
        return pltpu.make_async_copy(ybuf.at[slot], ys_hbm.at[rows(g)], sem_out.at[slot])

    @pl.when(e == 0)
    def _():
        x_copy(0, 0).start()

    wg_b[...] = wg_ref[0, 0].astype(BF16)
    wu_b[...] = wu_ref[0, 0].astype(BF16)
    wd_b[...] = wd_ref[0, 0].astype(BF16)

    def block(j, carry):
        g = first + j
        slot = g % 2
        x_copy(g, slot).wait()

        @pl.when(g + 1 < total)
        def _():
            x_copy(g + 1, 1 - slot).start()

        @pl.when(g >= 2)
        def _():
            y_copy(g - 2, slot).wait()

        hi, lo = _unpack_bf16_pairs(xbuf[slot])
        xb = jnp.concatenate([hi.astype(BF16), lo.astype(BF16)], axis=1)
        gate = jnp.dot(xb, wg_b[...], preferred_element_type=F32)
        up = jnp.dot(xb, wu_b[...], preferred_element_type=F32)
        hidden = (gate * jax.nn.sigmoid(gate) * up).astype(BF16)
        ybuf[slot] = _pack_bf16_pairs(jnp.dot(hidden, wd_b[...], preferred_element_type=F32))
        y_copy(g, slot).start()
        return carry

    lax.fori_loop(0, n, block, 0)

    @pl.when(e == N_EXPERTS - 1)
    def _():
        @pl.when(total >= 2)
        def _():
            y_copy(total - 2, total % 2).wait()

        y_copy(total - 1, (total - 1) % 2).wait()


def _experts(xs, offsets, n_blk, layer, w_gate, w_up, w_down):
    n_rows, dp = xs.shape
    bm = EXPERT_BLOCK
    d, ff = w_gate.shape[-2:]
    w_spec = lambda a, b: pl.BlockSpec((1, 1, a, b), lambda e, of, nb, ly: (ly[0], e, 0, 0))
    grid_spec = pltpu.PrefetchScalarGridSpec(
        num_scalar_prefetch=3,
        grid=(N_EXPERTS,),
        in_specs=[pl.BlockSpec(memory_space=pl.ANY), w_spec(d, ff), w_spec(d, ff), w_spec(ff, d)],
        out_specs=pl.BlockSpec(memory_space=pl.ANY),
        scratch_shapes=[pltpu.VMEM((2, bm, dp), U32), pltpu.VMEM((2, bm, dp), U32),
                        pltpu.VMEM((d, ff), BF16), pltpu.VMEM((d, ff), BF16), pltpu.VMEM((ff, d), BF16),
                        pltpu.SemaphoreType.DMA((2,)), pltpu.SemaphoreType.DMA((2,))],
    )
    return pl.pallas_call(
        _expert_body,
        grid_spec=grid_spec,
        out_shape=jax.ShapeDtypeStruct((n_rows, dp), U32),
        compiler_params=_cparams("arbitrary"),
        name="moe_experts",
    )(offsets, n_blk, layer, xs, w_gate, w_up, w_down)


def _combine_body(dest_ref, dest_next_ref, gate_ref, h_ref, ys_hbm, sg_ref, su_ref, sd_ref, g_ref, b_ref, out_ref,
                  buf_a, buf_b, sem_a, sem_b):
    step = pl.program_id(0)
    tt = buf_a.shape[1]
    half = h_ref.shape[1] // 2

    def issue(dref, first, buf, sem):
        for t in range(tt):
            for k in range(TOP_K):
                row = dref[(first + t) * TOP_K + k]
                pltpu.make_async_copy(ys_hbm.at[pl.ds(row, 1)], buf.at[k, pl.ds(t, 1)], sem).start()

    def drain(buf, sem):
        for k in range(TOP_K):
            pltpu.make_async_copy(ys_hbm.at[pl.ds(0, tt)], buf.at[k], sem).wait()

    def finish(first, buf):
        rows = pl.ds(first, tt)
        h = h_ref[rows, :]
        hb = h.astype(BF16)
        sg = jnp.dot(hb, sg_ref[...], preferred_element_type=F32)
        su = jnp.dot(hb, su_ref[...], preferred_element_type=F32)
        shared = jnp.dot((sg * jax.nn.sigmoid(sg) * su).astype(BF16), sd_ref[...], preferred_element_type=F32)
        z = DEEPNORM_ALPHA * h + shared
        z_hi, z_lo = z[:, :half], z[:, half:]
        for k in range(TOP_K):
            y_hi, y_lo = _unpack_bf16_pairs(buf[k])
            gate = gate_ref[rows, k:k + 1]
            z_hi = z_hi + gate * y_hi
            z_lo = z_lo + gate * y_lo
        out_ref[rows, :] = _layer_norm_rows(jnp.concatenate([z_hi, z_lo], axis=1), g_ref[...], b_ref[...])

    @pl.when(step == 0)
    def _():
        def start_rows(t, carry):
            for k in range(TOP_K):
                pltpu.make_async_copy(ys_hbm.at[pl.ds(dest_ref[t * TOP_K + k], 1)], buf_a.at[k, pl.ds(t, 1)],
                                      sem_a).start()
            return carry
        lax.fori_loop(0, tt, start_rows, 0)

    drain(buf_a, sem_a)
    issue(dest_ref, tt, buf_b, sem_b)
    finish(0, buf_a)
    issue(dest_next_ref, 0, buf_a, sem_a)
    drain(buf_b, sem_b)
    finish(tt, buf_b)

    @pl.when(step == pl.num_programs(0) - 1)
    def _():
        drain(buf_a, sem_a)


def _combine(dest_flat, gate_tk, h2d, ys, sh_gate, sh_up, sh_down, ln_g, ln_b):
    t, d = h2d.shape
    tt = 128
    steps = t // (2 * tt)
    full = lambda a: pl.BlockSpec(a.shape, lambda i: (0, 0))
    return pl.pallas_call(
        _combine_body,
        grid=(steps,),
        in_specs=[pl.BlockSpec((2 * tt * TOP_K,), lambda i: (i,), memory_space=pltpu.SMEM),
                  pl.BlockSpec((2 * tt * TOP_K,), lambda i: (jnp.minimum(i + 1, steps - 1),), memory_space=pltpu.SMEM),
                  pl.BlockSpec((2 * tt, TOP_K), lambda i: (i, 0)),
                  pl.BlockSpec((2 * tt, d), lambda i: (i, 0)),
                  pl.BlockSpec(memory_space=pl.ANY),
                  full(sh_gate), full(sh_up), full(sh_down), full(ln_g), full(ln_b)],
        out_specs=pl.BlockSpec((2 * tt, d), lambda i: (i, 0)),
        out_shape=jax.ShapeDtypeStruct((t, d), F32),
        scratch_shapes=[pltpu.VMEM((TOP_K, tt, d // 2), U32), pltpu.VMEM((TOP_K, tt, d // 2), U32),
                        pltpu.SemaphoreType.DMA(()), pltpu.SemaphoreType.DMA(())],
        compiler_params=_cparams("arbitrary"),
        name="moe_combine_ln",
    )(dest_flat, dest_flat, gate_tk, h2d, ys, sh_gate, sh_up, sh_down, ln_g, ln_b)


def _mixer_sublayer(h2d, nb, s, w_in_p, w_out_p, a_biases, c_bias, lower_bound, norm_g_pad, sinks, w_sum, ln_g, ln_b):
    a_qkv, b_all, c_qkv = _in_proj(h2d, w_in_p)
    o_list, l_list = [], []
    for (window, r), bias in zip(A_PATTERNS, a_biases):
        src = a_qkv.reshape(nb, s // r, r * A_COLS)
        o, lse = _band_attn(src, bias, r=r, width=A_WIDTH, src_blocks=3, max_dist=window // r, want_lse=True)
        o_list.append(o.reshape(nb * s, A_WIDTH))
        l_list.append(lse.reshape(nb * s, A_WIDTH))
    oc = _band_attn(c_qkv.reshape(nb, s, C_COLS), c_bias, r=1, width=C_WIDTH, src_blocks=3,
                    max_dist=C_WINDOW - 1, sinks=sinks, want_lse=False).reshape(nb * s, C_WIDTH)
    ob = _hgrn(b_all.reshape(nb, s, B_COLS), lower_bound, norm_g_pad, w_sum).reshape(nb * s, B_PAD_WIDTH)
    return _mix_out(o_list, l_list, ob, oc, h2d, w_out_p, ln_g, ln_b)


def _moe_sublayer(h2d, layer, router_w, router_bias, w_gate, w_up, w_down, sh_gate, sh_up, sh_down, ln_g, ln_b):
    t, d = h2d.shape
    bm = EXPERT_BLOCK
    e_t, rank_t, gate_t, counts = _router(h2d, router_w.T.astype(F32), router_bias.astype(F32).reshape(N_EXPERTS, 1))

    counts = counts.reshape(N_EXPERTS)
    padded = (counts + bm - 1) // bm * bm
    pad_end = jnp.cumsum(padded).astype(I32)
    offsets = pad_end - padded
    n_blocks = -(-(t * TOP_K + N_EXPERTS * (bm - 1)) // bm)

    dest_flat = _dest_rows(e_t, rank_t, offsets).T.reshape(t * TOP_K)
    xs = _dispatch(h2d, dest_flat, pad_end, n_blocks * bm)
    ys = _experts(xs, offsets, (padded // bm).astype(I32), jnp.full((1,), layer, I32), w_gate, w_up, w_down)
    return _combine(dest_flat, gate_t.T, h2d, ys,
                    sh_gate.astype(BF16), sh_up.astype(BF16), sh_down.astype(BF16), ln_g, ln_b)


def kernel(x, w_in, w_out, rel_bias_table, lower_bound_logits, hgrn_norm_g, attn_sinks, ln1_g, ln1_b, router_w, router_bias, expert_w_gate, expert_w_up, expert_w_down, shared_w_gate, shared_w_up, shared_w_down, ln2_g, ln2_b):
    nb, s, d = x.shape
    depth = w_in.shape[0]
    lb_probs = jax.nn.softmax(lower_bound_logits.astype(F32), axis=0)
    lower_bounds = jnp.cumsum(lb_probs, axis=0) - lb_probs[0]
    rel_table = rel_bias_table.astype(F32)
    a_biases = [_band_bias(rel_table, r, 0, A_HEADS) for _, r in A_PATTERNS]
    c_bias = _band_bias(rel_table, 1, A_HEADS, A_HEADS + C_HEADS)
    w_sum = jnp.asarray(_hgrn_sum_matrix(), dtype=BF16)
    row = lambda v: v.astype(F32).reshape(1, -1)

    h = x.astype(F32).reshape(nb * s, d)
    for l in range(depth):
        h = _mixer_sublayer(h, nb, s, _prep_w_in(w_in[l]), _prep_w_out(w_out[l].astype(F32)), a_biases, c_bias,
                            lower_bounds[l].reshape(1, B_KEY_WIDTH), _pad_heads_vec(hgrn_norm_g[l].astype(F32)),
                            attn_sinks[l].astype(F32), w_sum, row(ln1_g[l]), row(ln1_b[l]))
        h = _moe_sublayer(h, l, router_w[l], router_bias[l], expert_w_gate, expert_w_up, expert_w_down,
                          shared_w_gate[l], shared_w_up[l], shared_w_down[l], row(ln2_g[l]), row(ln2_b[l]))
    return h.reshape(nb, s, d).astype(x.dtype)
```

```python
import functools
import math

import numpy as np
import jax
import jax.numpy as jnp
from jax import lax
from jax.experimental import pallas as pl
from jax.experimental.pallas import tpu as pltpu

F32 = jnp.float32
BF16 = jnp.bfloat16
I32 = jnp.int32
U32 = jnp.uint32

LANES = 128
SUBLANES = 8
VMEM_LIMIT = 56 * 1024 * 1024

D_MODEL = 1024
DEPTH = 2
HEAD_DIM = 64
BAND = 128
BAND_Q_TILE = 512
MASK_VALUE = -1e30

A_HEADS = 6
A_PATTERNS = ((128, 1), (512, 4), (2048, 16))
B_HEADS = 4
B_KEY_DIM = 128
B_VAL_DIM = 96
B_VAL_PAD = 128
HG_CHUNK = 64
HG_STEP_ROWS = 128
C_HEADS = 4
C_KV_HEADS = 2
C_WINDOW = 128

A_WIDTH = A_HEADS * HEAD_DIM
B_KEY_WIDTH = B_HEADS * B_KEY_DIM
B_WIDTH = B_HEADS * B_VAL_DIM
B_PAD_WIDTH = B_HEADS * B_VAL_PAD
C_WIDTH = C_HEADS * HEAD_DIM
C_KV_WIDTH = C_KV_HEADS * HEAD_DIM
IN_SPLITS = (A_WIDTH, A_WIDTH, A_WIDTH, B_KEY_WIDTH, B_KEY_WIDTH, B_WIDTH, B_WIDTH, C_WIDTH, C_KV_WIDTH, C_KV_WIDTH)

A_COLS = 3 * A_WIDTH
B_COLS = 4 * B_KEY_WIDTH
C_COLS = 3 * C_WIDTH
MIX_PAD_WIDTH = A_WIDTH + B_PAD_WIDTH + C_WIDTH

REL_BUCKETS = 32
REL_MAX_DIST = 2048

N_EXPERTS = 256
TOP_K = 8
N_GROUPS = 8
TOPK_GROUPS = 4
EXPERT_FF = 256
SHARED_FF = 256
ROUTED_SCALE = 2.5
EXPERT_BLOCK = 256
EXPERT_SLOTS = 4

DEEPNORM_ALPHA = (2 * DEPTH) ** 0.25
LN_EPS = 1e-5


def _cparams(*sem):
    return pltpu.CompilerParams(dimension_semantics=sem, vmem_limit_bytes=VMEM_LIMIT)


def _prep_w_in(w_in):
    d = w_in.shape[0]
    split_at = [int(i) for i in np.cumsum(IN_SPLITS)[:-1]]
    aq, ak, av, bq, bf, bi, bg, cq, ck, cv = jnp.split(w_in, split_at, axis=-1)
    pad_v = lambda w: jnp.pad(w.reshape(d, B_HEADS, B_VAL_DIM), ((0, 0), (0, 0), (0, B_VAL_PAD - B_VAL_DIM))).reshape(d, B_PAD_WIDTH)
    rep = lambda w: jnp.repeat(w.reshape(d, C_KV_HEADS, HEAD_DIM), C_HEADS // C_KV_HEADS, axis=1).reshape(d, C_WIDTH)
    cols = [aq, ak, av, bq, bf, pad_v(bi), pad_v(bg), cq, rep(ck), rep(cv)]
    return jnp.concatenate(cols, axis=-1).astype(BF16)


def _prep_w_out(w_out):
    d = w_out.shape[1]
    wa = w_out[:A_WIDTH]
    wb = w_out[A_WIDTH:A_WIDTH + B_WIDTH].reshape(B_HEADS, B_VAL_DIM, d)
    wb = jnp.pad(wb, ((0, 0), (0, B_VAL_PAD - B_VAL_DIM), (0, 0))).reshape(B_PAD_WIDTH, d)
    wc = w_out[A_WIDTH + B_WIDTH:]
    return jnp.concatenate([wa, wb, wc], axis=0).astype(BF16)


def _pad_heads_vec(v):
    return jnp.pad(v.reshape(B_HEADS, B_VAL_DIM), ((0, 0), (0, B_VAL_PAD - B_VAL_DIM))).reshape(1, B_PAD_WIDTH)


def _rel_bucket(dist):
    max_exact = REL_BUCKETS // 2
    d = jnp.maximum(dist, 0)
    log_ratio = jnp.log(jnp.maximum(d, max_exact).astype(F32) / max_exact) / math.log(REL_MAX_DIST / max_exact)
    large = jnp.minimum(max_exact + (log_ratio * (REL_BUCKETS - max_exact)).astype(I32), REL_BUCKETS - 1)
    return jnp.where(d < max_exact, d, large)


def _band_bias(rel_table, r, head_lo, head_hi):
    dist = jnp.arange(BAND)[:, None] + BAND - jnp.arange(2 * BAND)[None, :]
    onehot = jax.nn.one_hot(_rel_bucket(dist * r), REL_BUCKETS, dtype=F32)
    return jnp.einsum("qkb,bh->hqk", onehot, rel_table[:, head_lo:head_hi], precision=lax.Precision.HIGHEST)


def _hgrn_sum_matrix():
    c = HG_CHUNK
    t = np.arange(c)[:, None]
    u = np.arange(c)[None, :]
    blocks = [(u <= t), (u > t)]
    m = c
    while m >= 2:
        mid = (t // m) * m + m // 2
        second = t >= mid
        blocks.append(np.where(second, (u >= mid) & (u <= t), (u > t) & (u < mid)))
        m //= 2
    return np.concatenate(blocks, axis=0).astype(np.float32)


HG_LEVELS = int(math.log2(HG_CHUNK))


def _in_proj_body(x_ref, w_ref, a_ref, b_ref, c_ref):
    xb = x_ref[...].astype(BF16)
    a_ref[...] = jnp.dot(xb, w_ref[:, :A_COLS], preferred_element_type=F32).astype(BF16)
    for j in range(B_COLS // B_KEY_WIDTH):
        lo = A_COLS + j * B_KEY_WIDTH
        b_ref[:, j * B_KEY_WIDTH:(j + 1) * B_KEY_WIDTH] = jnp.dot(
            xb, w_ref[:, lo:lo + B_KEY_WIDTH], preferred_element_type=F32)
    c_ref[...] = jnp.dot(xb, w_ref[:, A_COLS + B_COLS:], preferred_element_type=F32).astype(BF16)


def _in_proj(x2d, w_p):
    t, d = x2d.shape
    tm = 512
    n = w_p.shape[1]
    return pl.pallas_call(
        _in_proj_body,
        grid=(t // tm,),
        in_specs=[pl.BlockSpec((tm, d), lambda i: (i, 0)),
                  pl.BlockSpec((d, n), lambda i: (0, 0))],
        out_specs=[pl.BlockSpec((tm, A_COLS), lambda i: (i, 0)),
                   pl.BlockSpec((tm, B_COLS), lambda i: (i, 0)),
                   pl.BlockSpec((tm, C_COLS), lambda i: (i, 0))],
        out_shape=[jax.ShapeDtypeStruct((t, A_COLS), BF16),
                   jax.ShapeDtypeStruct((t, B_COLS), F32),
                   jax.ShapeDtypeStruct((t, C_COLS), BF16)],
        compiler_params=_cparams("arbitrary"),
        name="in_proj",
    )(x2d, w_p)


def _band_attn_body(*refs, width, max_dist, has_sink, want_lse):
    q_ref, kp_ref, kc_ref, vp_ref, vc_ref, bias_ref = refs[:6]
    rest = refs[6:]
    if has_sink:
        sink_ref, rest = rest[0], rest[1:]
    o_ref = rest[0]
    lse_ref = rest[1] if want_lse else None

    first_tile = pl.program_id(2) == 0
    row = lax.broadcasted_iota(I32, (BAND, 2 * BAND), 0)
    col = lax.broadcasted_iota(I32, (BAND, 2 * BAND), 1)
    dist = row + BAND - col
    in_band = (dist >= 0) & (dist <= max_dist)
    first_mask = in_band & ((col >= BAND) | jnp.logical_not(first_tile))
    low_half = lax.broadcasted_iota(I32, (BAND, LANES), 1) < HEAD_DIM
    scale = HEAD_DIM ** -0.5

    for qb in range(q_ref.shape[1] // BAND):
        rows = slice(qb * BAND, (qb + 1) * BAND)
        mask = first_mask if qb == 0 else in_band
        for tile in range(width // LANES):
            sl = slice(tile * LANES, (tile + 1) * LANES)
            q2 = q_ref[0, rows, sl]
            if qb == 0:
                k2 = jnp.concatenate([kp_ref[0, :, sl], kc_ref[0, :BAND, sl]], axis=0)
                v2 = jnp.concatenate([vp_ref[0, :, sl], vc_ref[0, :BAND, sl]], axis=0)
            else:
                k2 = kc_ref[0, (qb - 1) * BAND:(qb + 1) * BAND, sl]
                v2 = vc_ref[0, (qb - 1) * BAND:(qb + 1) * BAND, sl]
            outs, lses = [], []
            for half in range(2):
                h = 2 * tile + half
                qm = jnp.where(low_half if half == 0 else jnp.logical_not(low_half), q2, jnp.zeros_like(q2))
                s = lax.dot_general(qm, k2, (((1,), (1,)), ((), ())), preferred_element_type=F32)
                s = s * scale + bias_ref[h]
                s = jnp.where(mask, s, MASK_VALUE)
                m = jnp.max(s, axis=-1, keepdims=True)
                if has_sink:
                    sink = sink_ref[h]
                    m = jnp.maximum(m, sink)
                p = jnp.exp(s - m)
                den = jnp.sum(p, axis=-1, keepdims=True)
                if has_sink:
                    den = den + jnp.exp(sink - m)
                pv = jnp.dot(p.astype(BF16), v2, preferred_element_type=F32)
                outs.append(pv / den)
                if want_lse:
                    lses.append(jnp.broadcast_to(m + jnp.log(den), (BAND, LANES)))
            o_ref[0, rows, sl] = jnp.where(low_half, outs[0], outs[1])
            if want_lse:
                lse_ref[0, rows, sl] = jnp.where(low_half, lses[0], lses[1])


def _band_attn(src, bias, *, r, width, src_blocks, max_dist, sinks=None, want_lse):
    nb, length, _ = src.shape
    qt = min(BAND_Q_TILE, length)
    bands = qt // BAND
    heads = width // HEAD_DIM
    has_sink = sinks is not None

    def cur(off):
        return pl.BlockSpec((1, qt, width), lambda b, p, i: (b, i, p * src_blocks + off))

    def prev(off):
        return pl.BlockSpec((1, BAND, width), lambda b, p, i: (b, jnp.maximum(i * bands - 1, 0), p * src_blocks + off))

    in_specs = [cur(0), prev(1), cur(1), prev(2), cur(2),
                pl.BlockSpec((heads, BAND, 2 * BAND), lambda b, p, i: (0, 0, 0))]
    args = [src, src, src, src, src, bias]
    if has_sink:
        in_specs.append(pl.BlockSpec(memory_space=pltpu.SMEM))
        args.append(sinks)
    out_spec = pl.BlockSpec((1, qt, width), lambda b, p, i: (b, i, p))
    out_sds = jax.ShapeDtypeStruct((nb, length, r * width), F32)
    body = functools.partial(_band_attn_body, width=width, max_dist=max_dist, has_sink=has_sink, want_lse=want_lse)
    return pl.pallas_call(
        body,
        grid=(nb, r, length // qt),
        in_specs=in_specs,
        out_specs=[out_spec, out_spec] if want_lse else out_spec,
        out_shape=[out_sds, out_sds] if want_lse else out_sds,
        compiler_params=_cparams("arbitrary", "arbitrary", "arbitrary"),
        name="band_attn_r%d_w%d" % (r, width),
    )(*args)


def _hgrn_body(b_ref, lb_ref, ng_ref, w_ref, o_ref, state_ref):
    c = HG_CHUNK
    kd = B_KEY_DIM

    @pl.when(pl.program_id(1) == 0)
    def _():
        state_ref[...] = jnp.zeros_like(state_ref)

    trow = lax.broadcasted_iota(I32, (c, 1), 0)
    ti = lax.broadcasted_iota(I32, (c, c), 0)
    si = lax.broadcasted_iota(I32, (c, c), 1)
    nt = (((1,), (1,)), ((), ()))
    w_all = w_ref[...]

    states = [state_ref[h] for h in range(B_HEADS)]
    for chunk, h in [(ci, hi) for ci in range(b_ref.shape[1] // c) for hi in range(B_HEADS)]:
        rows = slice(chunk * c, (chunk + 1) * c)
        ks = slice(h * kd, (h + 1) * kd)
        q = b_ref[0, rows, ks]
        f = b_ref[0, rows, B_KEY_WIDTH + h * kd:B_KEY_WIDTH + (h + 1) * kd]
        inp = b_ref[0, rows, 2 * B_KEY_WIDTH + h * kd:2 * B_KEY_WIDTH + (h + 1) * kd]
        gate = b_ref[0, rows, 3 * B_KEY_WIDTH + h * kd:3 * B_KEY_WIDTH + (h + 1) * kd]
        lb = lb_ref[:, ks]

        log_f = jnp.log(lb + (1.0 - lb) * jax.nn.sigmoid(f))
        key = (1.0 - lb) * jax.nn.sigmoid(-f)
        qs = q * jax.nn.sigmoid(q)

        g_hi = log_f.astype(BF16)
        g_lo = (log_f - g_hi.astype(F32)).astype(BF16)
        e2 = jnp.dot(w_all, jnp.concatenate([g_hi, g_lo], axis=1), preferred_element_type=F32)
        e = e2[:, :kd] + e2[:, kd:]

        cum = e[0:c]
        q_dec = (qs * jnp.exp(cum)).astype(BF16)
        k_dec = (key * jnp.exp(e[c:2 * c])).astype(BF16)
        inp_b = inp.astype(BF16)

        st = states[h]
        inter = lax.dot_general(q_dec, st.astype(BF16), nt, preferred_element_type=F32)

        scores = jnp.where(ti == si,
                           lax.dot_general(qs.astype(BF16), key.astype(BF16), nt, preferred_element_type=F32), 0.0)
        m = c
        for lvl in range(HG_LEVELS):
            el = jnp.exp(e[(2 + lvl) * c:(3 + lvl) * c])
            second = (trow % m) >= (m // 2)
            ql = jnp.where(second, qs * el, 0.0).astype(BF16)
            kl = jnp.where(second, 0.0, key * el).astype(BF16)
            sl = lax.dot_general(ql, kl, nt, preferred_element_type=F32)
            if m < c:
                sl = jnp.where((ti // m) == (si // m), sl, 0.0)
            scores = scores + sl
            m //= 2
        intra = jnp.dot(scores.astype(BF16), inp_b, preferred_element_type=F32)

        new_st = st * jnp.exp(cum[c - 1:c]) + lax.dot_general(
            inp_b, k_dec, (((0,), (0,)), ((), ())), preferred_element_type=F32)
        states[h] = new_st

        o = inter + intra
        ms = jnp.sum(o * o, axis=-1, keepdims=True) * (1.0 / B_VAL_DIM)
        o = o * lax.rsqrt(ms + 1e-6)
        o_ref[0, rows, ks] = o * ng_ref[:, ks] * (gate * jax.nn.sigmoid(gate))

    for h in range(B_HEADS):
        state_ref[h] = states[h]


def _hgrn(b_all, lower_bound, norm_g_pad, w_sum):
    nb, s, _ = b_all.shape
    c = HG_STEP_ROWS
    return pl.pallas_call(
        _hgrn_body,
        grid=(nb, s // c),
        in_specs=[pl.BlockSpec((1, c, B_COLS), lambda b, i: (b, i, 0)),
                  pl.BlockSpec((1, B_KEY_WIDTH), lambda b, i: (0, 0)),
                  pl.BlockSpec((1, B_PAD_WIDTH), lambda b, i: (0, 0)),
                  pl.BlockSpec(w_sum.shape, lambda b, i: (0, 0))],
        out_specs=pl.BlockSpec((1, c, B_PAD_WIDTH), lambda b, i: (b, i, 0)),
        out_shape=jax.ShapeDtypeStruct((nb, s, B_PAD_WIDTH), F32),
        scratch_shapes=[pltpu.VMEM((B_HEADS, B_VAL_PAD, B_KEY_DIM), F32)],
        compiler_params=_cparams("arbitrary", "arbitrary"),
        name="hgrn2",
    )(b_all, lower_bound, norm_g_pad, w_sum)


def _layer_norm_rows(z, g, b):
    mu = jnp.mean(z, axis=-1, keepdims=True)
    zc = z - mu
    var = jnp.mean(zc * zc, axis=-1, keepdims=True)
    return zc * lax.rsqrt(var + LN_EPS) * g + b


def _mix_out_body(o1, o2, o3, l1, l2, l3, ob, oc, x_ref, w_ref, g_ref, b_ref, out_ref):
    la, lb_, lc = l1[...], l2[...], l3[...]
    m = jnp.maximum(jnp.maximum(la, lb_), lc)
    wa, wb, wc = jnp.exp(la - m), jnp.exp(lb_ - m), jnp.exp(lc - m)
    oa = (wa * o1[...] + wb * o2[...] + wc * o3[...]) / (wa + wb + wc)
    cat = jnp.concatenate([oa, ob[...], oc[...]], axis=1).astype(BF16)
    y = jnp.dot(cat, w_ref[...], preferred_element_type=F32)
    z = DEEPNORM_ALPHA * x_ref[...] + y
    out_ref[...] = _layer_norm_rows(z, g_ref[...], b_ref[...])


def _mix_out(o_list, l_list, ob, oc, x2d, w_out_p, ln_g, ln_b):
    t, d = x2d.shape
    tm = 256
    row = lambda w: pl.BlockSpec((tm, w), lambda i: (i, 0))
    full = lambda a: pl.BlockSpec(a.shape, lambda i: (0, 0))
    return pl.pallas_call(
        _mix_out_body,
        grid=(t // tm,),
        in_specs=[row(A_WIDTH)] * 6 + [row(B_PAD_WIDTH), row(C_WIDTH), row(d), full(w_out_p), full(ln_g), full(ln_b)],
        out_specs=row(d),
        out_shape=jax.ShapeDtypeStruct((t, d), F32),
        compiler_params=_cparams("arbitrary"),
        name="mix_out_ln",
    )(*o_list, *l_list, ob, oc, x2d, w_out_p, ln_g, ln_b)


def _router_body(h_ref, rw_ref, bias_ref, tri_ref, e_ref, rank_ref, gate_ref, cnt_ref, carry_ref):
    tn = h_ref.shape[0]
    per_group = N_EXPERTS // N_GROUPS
    neg_inf = -jnp.inf

    @pl.when(pl.program_id(0) == 0)
    def _():
        carry_ref[...] = jnp.zeros_like(carry_ref)

    logits = lax.dot_general(rw_ref[...], h_ref[...], (((1,), (1,)), ((), ())),
                             precision=lax.Precision.HIGHEST, preferred_element_type=F32)
    scores = jax.nn.sigmoid(logits)
    choice = scores + bias_ref[...]

    def first_max(vals, idx, sentinel):
        top = jnp.max(vals, axis=0, keepdims=True)
        return top, jnp.min(jnp.where(vals == top, idx, sentinel), axis=0, keepdims=True)

    li = lax.broadcasted_iota(I32, (per_group, tn), 0).astype(F32)
    group_rows = []
    for g in range(N_GROUPS):
        cg = choice[g * per_group:(g + 1) * per_group]
        m1, first = first_max(cg, li, float(per_group))
        m2 = jnp.max(jnp.where(li == first, neg_inf, cg), axis=0, keepdims=True)
        group_rows.append(m1 + m2)
    group_score = jnp.concatenate(group_rows, axis=0)

    gi = lax.broadcasted_iota(I32, (N_GROUPS, tn), 0).astype(F32)
    group_ok = jnp.zeros((N_GROUPS, tn), F32)
    cur = group_score
    for _ in range(TOPK_GROUPS):
        _, first = first_max(cur, gi, float(N_GROUPS))
        pick = gi == first
        group_ok = jnp.where(pick, 1.0, group_ok)
        cur = jnp.where(pick, neg_inf, cur)

    cur = jnp.concatenate(
        [jnp.where(group_ok[g:g + 1] > 0.0, choice[g * per_group:(g + 1) * per_group], MASK_VALUE)
         for g in range(N_GROUPS)], axis=0)
    ei = lax.broadcasted_iota(I32, (N_EXPERTS, tn), 0).astype(F32)
    chosen = jnp.zeros((N_EXPERTS, tn), F32)
    picks, gates = [], []
    for _ in range(TOP_K):
        _, idx = first_max(cur, ei, float(N_EXPERTS))
        pick = ei == idx
        picks.append(idx)
        gates.append(jnp.sum(jnp.where(pick, scores, 0.0), axis=0, keepdims=True))
        chosen = jnp.where(pick, 1.0, chosen)
        cur = jnp.where(pick, neg_inf, cur)

    gate = jnp.concatenate(gates, axis=0)
    gate_ref[...] = gate / jnp.sum(gate, axis=0, keepdims=True) * ROUTED_SCALE
    e_ref[...] = jnp.concatenate(picks, axis=0).astype(I32)

    before = jnp.dot(chosen.astype(BF16), tri_ref[...], preferred_element_type=F32) + carry_ref[...]
    ranks = [jnp.sum(jnp.where(ei == idx, before, 0.0), axis=0, keepdims=True) for idx in picks]
    rank_ref[...] = jnp.concatenate(ranks, axis=0).astype(I32)
    carry = carry_ref[...] + jnp.sum(chosen, axis=1, keepdims=True)
    carry_ref[...] = carry
    cnt_ref[...] = carry.astype(I32)


def _router(h2d, rw_t, bias_col):
    t, d = h2d.shape
    tn = 256
    tri = jnp.asarray(np.triu(np.ones((tn, tn), np.float32), k=1), dtype=BF16)
    tok = lambda: pl.BlockSpec((TOP_K, tn), lambda i: (0, i))
    return pl.pallas_call(
        _router_body,
        grid=(t // tn,),
        in_specs=[pl.BlockSpec((tn, d), lambda i: (i, 0)),
                  pl.BlockSpec((N_EXPERTS, d), lambda i: (0, 0)),
                  pl.BlockSpec((N_EXPERTS, 1), lambda i: (0, 0)),
                  pl.BlockSpec((tn, tn), lambda i: (0, 0))],
        out_specs=[tok(), tok(), tok(), pl.BlockSpec((N_EXPERTS, 1), lambda i: (0, 0))],
        out_shape=[jax.ShapeDtypeStruct((TOP_K, t), I32),
                   jax.ShapeDtypeStruct((TOP_K, t), I32),
                   jax.ShapeDtypeStruct((TOP_K, t), F32),
                   jax.ShapeDtypeStruct((N_EXPERTS, 1), I32)],
        scratch_shapes=[pltpu.VMEM((N_EXPERTS, 1), F32)],
        compiler_params=_cparams("arbitrary"),
        name="moe_router",
    )(h2d, rw_t, bias_col, tri)


def _pack_bf16_pairs(x):
    w = x.shape[1] // 2
    hi = lax.bitcast_convert_type(x[:, :w].astype(BF16).astype(F32), U32)
    lo = lax.bitcast_convert_type(x[:, w:].astype(BF16).astype(F32), U32)
    return hi | (lo >> 16)


def _unpack_bf16_pairs(p):
    hi = lax.bitcast_convert_type(p & jnp.uint32(0xFFFF0000), F32)
    lo = lax.bitcast_convert_type(p << 16, F32)
    return hi, lo


def _dest_body(e_ref, rank_ref, offs_ref, dest_ref):
    tn = e_ref.shape[1]
    ei = lax.broadcasted_iota(I32, (N_EXPERTS, tn), 0)
    offs = offs_ref[...]
    rows = [jnp.sum(jnp.where(ei == e_ref[k:k + 1, :], offs, 0.0), axis=0, keepdims=True) for k in range(TOP_K)]
    dest_ref[...] = jnp.concatenate(rows, axis=0).astype(I32) + rank_ref[...]


def _dest_rows(e_t, rank_t, offsets):
    t = e_t.shape[1]
    tn = 512
    tok = pl.BlockSpec((TOP_K, tn), lambda i: (0, i))
    return pl.pallas_call(
        _dest_body,
        grid=(t // tn,),
        in_specs=[tok, tok, pl.BlockSpec((N_EXPERTS, 1), lambda i: (0, 0))],
        out_specs=tok,
        out_shape=jax.ShapeDtypeStruct((TOP_K, t), I32),
        compiler_params=_cparams("arbitrary"),
        name="moe_dest",
    )(e_t, rank_t, offsets.astype(F32).reshape(N_EXPERTS, 1))


def _dispatch_body(pad_end_ref, dest_ref, h_ref, xs_hbm, packed, zero_buf, sem):
    step = pl.program_id(0)
    tt = h_ref.shape[0]
    bm = zero_buf.shape[0]

    def tail_copy(e):
        end = pl.multiple_of(pad_end_ref[e], bm)
        return pltpu.make_async_copy(zero_buf, xs_hbm.at[pl.ds(end - bm, bm)], sem)

    def has_rows(e):
        prev = jnp.where(e > 0, pad_end_ref[jnp.maximum(e - 1, 0)], 0)
        return pad_end_ref[e] > prev

    @pl.when(step == 0)
    def _():
        zero_buf[...] = jnp.zeros_like(zero_buf)

        def start(e, carry):
            @pl.when(has_rows(e))
            def _():
                tail_copy(e).start()
            return carry

        def wait(e, carry):
            @pl.when(has_rows(e))
            def _():
                tail_copy(e).wait()
            return carry

        lax.fori_loop(0, N_EXPERTS, start, 0)
        lax.fori_loop(0, N_EXPERTS, wait, 0)

    packed[...] = _pack_bf16_pairs(h_ref[...])

    def start_rows(t, carry):
        src = packed.at[pl.ds(t, 1)]
        for k in range(TOP_K):
            pltpu.make_async_copy(src, xs_hbm.at[pl.ds(dest_ref[t * TOP_K + k], 1)], sem).start()
        return carry

    lax.fori_loop(0, tt, start_rows, 0)
    for k in range(TOP_K):
        pltpu.make_async_copy(packed, xs_hbm.at[pl.ds(0, tt)], sem).wait()


def _dispatch(h2d, dest_flat, pad_end, n_rows):
    t, d = h2d.shape
    tt = 512
    grid_spec = pltpu.PrefetchScalarGridSpec(
        num_scalar_prefetch=1,
        grid=(t // tt,),
        in_specs=[pl.BlockSpec((tt * TOP_K,), lambda i, pe: (i,), memory_space=pltpu.SMEM),
                  pl.BlockSpec((tt, d), lambda i, pe: (i, 0))],
        out_specs=pl.BlockSpec(memory_space=pl.ANY),
        scratch_shapes=[pltpu.VMEM((tt, d // 2), U32), pltpu.VMEM((EXPERT_BLOCK, d // 2), U32),
                        pltpu.SemaphoreType.DMA(())],
    )
    return pl.pallas_call(
        _dispatch_body,
        grid_spec=grid_spec,
        out_shape=jax.ShapeDtypeStruct((n_rows, d // 2), U32),
        compiler_params=_cparams("arbitrary"),
        name="moe_dispatch",
    )(pad_end, dest_flat, h2d)


def _expert_body(offs_ref, nblk_ref, layer_ref, xs_hbm, wg_ref, wu_ref, wd_ref, ys_hbm,
                 xbuf, ybuf, wg_b, wu_b, wd_b, sem_in, sem_out):
    e = pl.program_id(0)
    slots, bm = xbuf.shape[:2]
    ahead = slots - 1
    n = nblk_ref[e]
    first = offs_ref[e] // bm
    total = (offs_ref[N_EXPERTS - 1] // bm) + nblk_ref[N_EXPERTS - 1]

    def rows(g):
        return pl.ds(pl.multiple_of(g * bm, bm), bm)

    def x_copy(g, slot):
        return pltpu.make_async_copy(xs_hbm.at[rows(g)], xbuf.at[slot], sem_in.at[slot])

    def y_copy(g, slot):
        return pltpu.make_async_copy(ybuf.at[slot], ys_hbm.at[rows(g)], sem_out.at[slot])

    @pl.when(e == 0)
    def _():
        for g0 in range(ahead):
            @pl.when(g0 < total)
            def _():
                x_copy(g0, g0).start()

    wg_b[...] = wg_ref[0, 0].astype(BF16)
    wu_b[...] = wu_ref[0, 0].astype(BF16)
    wd_b[...] = wd_ref[0, 0].astype(BF16)

    def block(j, carry):
        g = first + j
        slot = g % slots
        x_copy(g, slot).wait()

        @pl.when(g + ahead < total)
        def _():
            x_copy(g + ahead, (g + ahead) % slots).start()

        @pl.when(g >= slots)
        def _():
            y_copy(g - slots, slot).wait()

        hi, lo = _unpack_bf16_pairs(xbuf[slot])
        xb = jnp.concatenate([hi.astype(BF16), lo.astype(BF16)], axis=1)
        gate = jnp.dot(xb, wg_b[...], preferred_element_type=F32)
        up = jnp.dot(xb, wu_b[...], preferred_element_type=F32)
        hidden = (gate * jax.nn.sigmoid(gate) * up).astype(BF16)
        ybuf[slot] = _pack_bf16_pairs(jnp.dot(hidden, wd_b[...], preferred_element_type=F32))
        y_copy(g, slot).start()
        return carry

    lax.fori_loop(0, n, block, 0)

    @pl.when(e == N_EXPERTS - 1)
    def _():
        for back in range(1, slots + 1):
            @pl.when(total >= back)
            def _():
                y_copy(total - back, (total - back) % slots).wait()


def _experts(xs, offsets, n_blk, layer, w_gate, w_up, w_down):
    n_rows, dp = xs.shape
    bm = EXPERT_BLOCK
    d, ff = w_gate.shape[-2:]
    w_spec = lambda a, b: pl.BlockSpec((1, 1, a, b), lambda e, of, nb, ly: (ly[0], e, 0, 0))
    grid_spec = pltpu.PrefetchScalarGridSpec(
        num_scalar_prefetch=3,
        grid=(N_EXPERTS,),
        in_specs=[pl.BlockSpec(memory_space=pl.ANY), w_spec(d, ff), w_spec(d, ff), w_spec(ff, d)],
        out_specs=pl.BlockSpec(memory_space=pl.ANY),
        scratch_shapes=[pltpu.VMEM((EXPERT_SLOTS, bm, dp), U32), pltpu.VMEM((EXPERT_SLOTS, bm, dp), U32),
                        pltpu.VMEM((d, ff), BF16), pltpu.VMEM((d, ff), BF16), pltpu.VMEM((ff, d), BF16),
                        pltpu.SemaphoreType.DMA((EXPERT_SLOTS,)), pltpu.SemaphoreType.DMA((EXPERT_SLOTS,))],
    )
    return pl.pallas_call(
        _expert_body,
        grid_spec=grid_spec,
        out_shape=jax.ShapeDtypeStruct((n_rows, dp), U32),
        compiler_params=_cparams("arbitrary"),
        name="moe_experts",
    )(offsets, n_blk, layer, xs, w_gate, w_up, w_down)


def _combine_body(dest_ref, dest_next_ref, gate_ref, h_ref, ys_hbm, sg_ref, su_ref, sd_ref, g_ref, b_ref, out_ref,
                  buf_a, buf_b, sem_a, sem_b):
    step = pl.program_id(0)
    tt = buf_a.shape[1]
    half = h_ref.shape[1] // 2

    def issue(dref, first, buf, sem):
        for t in range(tt):
            for k in range(TOP_K):
                row = dref[(first + t) * TOP_K + k]
                pltpu.make_async_copy(ys_hbm.at[pl.ds(row, 1)], buf.at[k, pl.ds(t, 1)], sem).start()

    def drain(buf, sem):
        for k in range(TOP_K):
            pltpu.make_async_copy(ys_hbm.at[pl.ds(0, tt)], buf.at[k], sem).wait()

    def finish(first, buf):
        rows = pl.ds(first, tt)
        h = h_ref[rows, :]
        hb = h.astype(BF16)
        sg = jnp.dot(hb, sg_ref[...], preferred_element_type=F32)
        su = jnp.dot(hb, su_ref[...], preferred_element_type=F32)
        shared = jnp.dot((sg * jax.nn.sigmoid(sg) * su).astype(BF16), sd_ref[...], preferred_element_type=F32)
        z = DEEPNORM_ALPHA * h + shared
        z_hi, z_lo = z[:, :half], z[:, half:]
        for k in range(TOP_K):
            y_hi, y_lo = _unpack_bf16_pairs(buf[k])
            gate = gate_ref[rows, k:k + 1]
            z_hi = z_hi + gate * y_hi
            z_lo = z_lo + gate * y_lo
        out_ref[rows, :] = _layer_norm_rows(jnp.concatenate([z_hi, z_lo], axis=1), g_ref[...], b_ref[...])

    @pl.when(step == 0)
    def _():
        def start_rows(t, carry):
            for k in range(TOP_K):
                pltpu.make_async_copy(ys_hbm.at[pl.ds(dest_ref[t * TOP_K + k], 1)], buf_a.at[k, pl.ds(t, 1)],
                                      sem_a).start()
            return carry
        lax.fori_loop(0, tt, start_rows, 0)

    drain(buf_a, sem_a)
    issue(dest_ref, tt, buf_b, sem_b)
    finish(0, buf_a)
    issue(dest_next_ref, 0, buf_a, sem_a)
    drain(buf_b, sem_b)
    finish(tt, buf_b)

    @pl.when(step == pl.num_programs(0) - 1)
    def _():
        drain(buf_a, sem_a)


def _combine(dest_flat, gate_tk, h2d, ys, sh_gate, sh_up, sh_down, ln_g, ln_b):
    t, d = h2d.shape
    tt = 128
    steps = t // (2 * tt)
    full = lambda a: pl.BlockSpec(a.shape, lambda i: (0, 0))
    return pl.pallas_call(
        _combine_body,
        grid=(steps,),
        in_specs=[pl.BlockSpec((2 * tt * TOP_K,), lambda i: (i,), memory_space=pltpu.SMEM),
                  pl.BlockSpec((2 * tt * TOP_K,), lambda i: (jnp.minimum(i + 1, steps - 1),), memory_space=pltpu.SMEM),
                  pl.BlockSpec((2 * tt, TOP_K), lambda i: (i, 0)),
                  pl.BlockSpec((2 * tt, d), lambda i: (i, 0)),
                  pl.BlockSpec(memory_space=pl.ANY),
                  full(sh_gate), full(sh_up), full(sh_down), full(ln_g), full(ln_b)],
        out_specs=pl.BlockSpec((2 * tt, d), lambda i: (i, 0)),
        out_shape=jax.ShapeDtypeStruct((t, d), F32),
        scratch_shapes=[pltpu.VMEM((TOP_K, tt, d // 2), U32), pltpu.VMEM((TOP_K, tt, d // 2), U32),
                        pltpu.SemaphoreType.DMA(()), pltpu.SemaphoreType.DMA(())],
        compiler_params=_cparams("arbitrary"),
        name="moe_combine_ln",
    )(dest_flat, dest_flat, gate_tk, h2d, ys, sh_gate, sh_up, sh_down, ln_g, ln_b)


def _mixer_sublayer(h2d, nb, s, w_in_p, w_out_p, a_biases, c_bias, lower_bound, norm_g_pad, sinks, w_sum, ln_g, ln_b):
    a_qkv, b_all, c_qkv = _in_proj(h2d, w_in_p)
    o_list, l_list = [], []
    for (window, r), bias in zip(A_PATTERNS, a_biases):
        src = a_qkv.reshape(nb, s // r, r * A_COLS)
        o, lse = _band_attn(src, bias, r=r, width=A_WIDTH, src_blocks=3, max_dist=window // r, want_lse=True)
        o_list.append(o.reshape(nb * s, A_WIDTH))
        l_list.append(lse.reshape(nb * s, A_WIDTH))
    oc = _band_attn(c_qkv.reshape(nb, s, C_COLS), c_bias, r=1, width=C_WIDTH, src_blocks=3,
                    max_dist=C_WINDOW - 1, sinks=sinks, want_lse=False).reshape(nb * s, C_WIDTH)
    ob = _hgrn(b_all.reshape(nb, s, B_COLS), lower_bound, norm_g_pad, w_sum).reshape(nb * s, B_PAD_WIDTH)
    return _mix_out(o_list, l_list, ob, oc, h2d, w_out_p, ln_g, ln_b)


def _moe_sublayer(h2d, layer, router_w, router_bias, w_gate, w_up, w_down, sh_gate, sh_up, sh_down, ln_g, ln_b):
    t, d = h2d.shape
    bm = EXPERT_BLOCK
    e_t, rank_t, gate_t, counts = _router(h2d, router_w.T.astype(F32), router_bias.astype(F32).reshape(N_EXPERTS, 1))

    counts = counts.reshape(N_EXPERTS)
    padded = (counts + bm - 1) // bm * bm
    pad_end = jnp.cumsum(padded).astype(I32)
    offsets = pad_end - padded
    n_blocks = -(-(t * TOP_K + N_EXPERTS * (bm - 1)) // bm)

    dest_flat = _dest_rows(e_t, rank_t, offsets).T.reshape(t * TOP_K)
    xs = _dispatch(h2d, dest_flat, pad_end, n_blocks * bm)
    ys = _experts(xs, offsets, (padded // bm).astype(I32), jnp.full((1,), layer, I32), w_gate, w_up, w_down)
    return _combine(dest_flat, gate_t.T, h2d, ys,
                    sh_gate.astype(BF16), sh_up.astype(BF16), sh_down.astype(BF16), ln_g, ln_b)


def kernel(x, w_in, w_out, rel_bias_table, lower_bound_logits, hgrn_norm_g, attn_sinks, ln1_g, ln1_b, router_w, router_bias, expert_w_gate, expert_w_up, expert_w_down, shared_w_gate, shared_w_up, shared_w_down, ln2_g, ln2_b):
    nb, s, d = x.shape
    depth = w_in.shape[0]
    lb_probs = jax.nn.softmax(lower_bound_logits.astype(F32), axis=0)
    lower_bounds = jnp.cumsum(lb_probs, axis=0) - lb_probs[0]
    rel_table = rel_bias_table.astype(F32)
    a_biases = [_band_bias(rel_table, r, 0, A_HEADS) for _, r in A_PATTERNS]
    c_bias = _band_bias(rel_table, 1, A_HEADS, A_HEADS + C_HEADS)
    w_sum = jnp.asarray(_hgrn_sum_matrix(), dtype=BF16)
    row = lambda v: v.astype(F32).reshape(1, -1)

    h = x.astype(F32).reshape(nb * s, d)
    for l in range(depth):
        h = _mixer_sublayer(h, nb, s, _prep_w_in(w_in[l]), _prep_w_out(w_out[l].astype(F32)), a_biases, c_bias,
                            lower_bounds[l].reshape(1, B_KEY_WIDTH), _pad_heads_vec(hgrn_norm_g[l].astype(F32)),
                            attn_sinks[l].astype(F32), w_sum, row(ln1_g[l]), row(ln1_b[l]))
        h = _moe_sublayer(h, l, router_w[l], router_bias[l], expert_w_gate, expert_w_up, expert_w_down,
                          shared_w_gate[l], shared_w_up[l], shared_w_down[l], row(ln2_g[l]), row(ln2_b[l]))
    return h.reshape(nb, s, d).astype(x.dtype)
```

```python
import functools
import math

import numpy as np
import jax
import jax.numpy as jnp
from jax import lax
from jax.experimental import pallas as pl
from jax.experimental.pallas import tpu as pltpu
from jax.experimental.pallas import tpu_sc as plsc

F32 = jnp.float32
BF16 = jnp.bfloat16
I32 = jnp.int32
U32 = jnp.uint32

LANES = 128
SUBLANES = 8
VMEM_LIMIT = 56 * 1024 * 1024

D_MODEL = 1024
DEPTH = 2
HEAD_DIM = 64
BAND = 128
BAND_Q_TILE = 512
MASK_VALUE = -1e30

A_HEADS = 6
A_PATTERNS = ((128, 1), (512, 4), (2048, 16))
B_HEADS = 4
B_KEY_DIM = 128
B_VAL_DIM = 96
B_VAL_PAD = 128
HG_CHUNK = 64
HG_STEP_ROWS = 128
C_HEADS = 4
C_KV_HEADS = 2
C_WINDOW = 128

A_WIDTH = A_HEADS * HEAD_DIM
B_KEY_WIDTH = B_HEADS * B_KEY_DIM
B_WIDTH = B_HEADS * B_VAL_DIM
B_PAD_WIDTH = B_HEADS * B_VAL_PAD
C_WIDTH = C_HEADS * HEAD_DIM
C_KV_WIDTH = C_KV_HEADS * HEAD_DIM
IN_SPLITS = (A_WIDTH, A_WIDTH, A_WIDTH, B_KEY_WIDTH, B_KEY_WIDTH, B_WIDTH, B_WIDTH, C_WIDTH, C_KV_WIDTH, C_KV_WIDTH)

A_COLS = 3 * A_WIDTH
B_COLS = 4 * B_KEY_WIDTH
C_COLS = 3 * C_WIDTH
MIX_PAD_WIDTH = A_WIDTH + B_PAD_WIDTH + C_WIDTH

REL_BUCKETS = 32
REL_MAX_DIST = 2048

N_EXPERTS = 256
TOP_K = 8
N_GROUPS = 8
TOPK_GROUPS = 4
EXPERT_FF = 256
SHARED_FF = 256
ROUTED_SCALE = 2.5
EXPERT_BLOCK = 256
EXPERT_SLOTS = 4

DEEPNORM_ALPHA = (2 * DEPTH) ** 0.25
LN_EPS = 1e-5


def _cparams(*sem):
    return pltpu.CompilerParams(dimension_semantics=sem, vmem_limit_bytes=VMEM_LIMIT)


def _prep_w_in(w_in):
    d = w_in.shape[0]
    split_at = [int(i) for i in np.cumsum(IN_SPLITS)[:-1]]
    aq, ak, av, bq, bf, bi, bg, cq, ck, cv = jnp.split(w_in, split_at, axis=-1)
    pad_v = lambda w: jnp.pad(w.reshape(d, B_HEADS, B_VAL_DIM), ((0, 0), (0, 0), (0, B_VAL_PAD - B_VAL_DIM))).reshape(d, B_PAD_WIDTH)
    rep = lambda w: jnp.repeat(w.reshape(d, C_KV_HEADS, HEAD_DIM), C_HEADS // C_KV_HEADS, axis=1).reshape(d, C_WIDTH)
    cols = [aq, ak, av, bq, bf, pad_v(bi), pad_v(bg), cq, rep(ck), rep(cv)]
    return jnp.concatenate(cols, axis=-1).astype(BF16)


def _prep_w_out(w_out):
    d = w_out.shape[1]
    wa = w_out[:A_WIDTH]
    wb = w_out[A_WIDTH:A_WIDTH + B_WIDTH].reshape(B_HEADS, B_VAL_DIM, d)
    wb = jnp.pad(wb, ((0, 0), (0, B_VAL_PAD - B_VAL_DIM), (0, 0))).reshape(B_PAD_WIDTH, d)
    wc = w_out[A_WIDTH + B_WIDTH:]
    return jnp.concatenate([wa, wb, wc], axis=0).astype(BF16)


def _pad_heads_vec(v):
    return jnp.pad(v.reshape(B_HEADS, B_VAL_DIM), ((0, 0), (0, B_VAL_PAD - B_VAL_DIM))).reshape(1, B_PAD_WIDTH)


def _rel_bucket(dist):
    max_exact = REL_BUCKETS // 2
    d = jnp.maximum(dist, 0)
    log_ratio = jnp.log(jnp.maximum(d, max_exact).astype(F32) / max_exact) / math.log(REL_MAX_DIST / max_exact)
    large = jnp.minimum(max_exact + (log_ratio * (REL_BUCKETS - max_exact)).astype(I32), REL_BUCKETS - 1)
    return jnp.where(d < max_exact, d, large)


def _band_bias(rel_table, r, head_lo, head_hi):
    dist = jnp.arange(BAND)[:, None] + BAND - jnp.arange(2 * BAND)[None, :]
    onehot = jax.nn.one_hot(_rel_bucket(dist * r), REL_BUCKETS, dtype=F32)
    return jnp.einsum("qkb,bh->hqk", onehot, rel_table[:, head_lo:head_hi], precision=lax.Precision.HIGHEST)


def _hgrn_sum_matrix():
    c = HG_CHUNK
    t = np.arange(c)[:, None]
    u = np.arange(c)[None, :]
    blocks = [(u <= t), (u > t)]
    m = c
    while m >= 2:
        mid = (t // m) * m + m // 2
        second = t >= mid
        blocks.append(np.where(second, (u >= mid) & (u <= t), (u > t) & (u < mid)))
        m //= 2
    return np.concatenate(blocks, axis=0).astype(np.float32)


HG_LEVELS = int(math.log2(HG_CHUNK))


def _in_proj_body(x_ref, w_ref, a_ref, b_ref, c_ref):
    xb = x_ref[...].astype(BF16)
    a_ref[...] = jnp.dot(xb, w_ref[:, :A_COLS], preferred_element_type=F32).astype(BF16)
    for j in range(B_COLS // B_KEY_WIDTH):
        lo = A_COLS + j * B_KEY_WIDTH
        b_ref[:, j * B_KEY_WIDTH:(j + 1) * B_KEY_WIDTH] = jnp.dot(
            xb, w_ref[:, lo:lo + B_KEY_WIDTH], preferred_element_type=F32)
    c_ref[...] = jnp.dot(xb, w_ref[:, A_COLS + B_COLS:], preferred_element_type=F32).astype(BF16)


def _in_proj(x2d, w_p):
    t, d = x2d.shape
    tm = 512
    n = w_p.shape[1]
    return pl.pallas_call(
        _in_proj_body,
        grid=(t // tm,),
        in_specs=[pl.BlockSpec((tm, d), lambda i: (i, 0)),
                  pl.BlockSpec((d, n), lambda i: (0, 0))],
        out_specs=[pl.BlockSpec((tm, A_COLS), lambda i: (i, 0)),
                   pl.BlockSpec((tm, B_COLS), lambda i: (i, 0)),
                   pl.BlockSpec((tm, C_COLS), lambda i: (i, 0))],
        out_shape=[jax.ShapeDtypeStruct((t, A_COLS), BF16),
                   jax.ShapeDtypeStruct((t, B_COLS), F32),
                   jax.ShapeDtypeStruct((t, C_COLS), BF16)],
        compiler_params=_cparams("arbitrary"),
        name="in_proj",
    )(x2d, w_p)


def _band_attn_body(*refs, width, max_dist, has_sink, want_lse):
    q_ref, kp_ref, kc_ref, vp_ref, vc_ref, bias_ref = refs[:6]
    rest = refs[6:]
    if has_sink:
        sink_ref, rest = rest[0], rest[1:]
    o_ref = rest[0]
    lse_ref = rest[1] if want_lse else None

    first_tile = pl.program_id(2) == 0
    row = lax.broadcasted_iota(I32, (BAND, 2 * BAND), 0)
    col = lax.broadcasted_iota(I32, (BAND, 2 * BAND), 1)
    dist = row + BAND - col
    in_band = (dist >= 0) & (dist <= max_dist)
    first_mask = in_band & ((col >= BAND) | jnp.logical_not(first_tile))
    low_half = lax.broadcasted_iota(I32, (BAND, LANES), 1) < HEAD_DIM
    scale = HEAD_DIM ** -0.5

    for qb in range(q_ref.shape[1] // BAND):
        rows = slice(qb * BAND, (qb + 1) * BAND)
        mask = first_mask if qb == 0 else in_band
        for tile in range(width // LANES):
            sl = slice(tile * LANES, (tile + 1) * LANES)
            q2 = q_ref[0, rows, sl]
            if qb == 0:
                k2 = jnp.concatenate([kp_ref[0, :, sl], kc_ref[0, :BAND, sl]], axis=0)
                v2 = jnp.concatenate([vp_ref[0, :, sl], vc_ref[0, :BAND, sl]], axis=0)
            else:
                k2 = kc_ref[0, (qb - 1) * BAND:(qb + 1) * BAND, sl]
                v2 = vc_ref[0, (qb - 1) * BAND:(qb + 1) * BAND, sl]
            outs, lses = [], []
            for half in range(2):
                h = 2 * tile + half
                qm = jnp.where(low_half if half == 0 else jnp.logical_not(low_half), q2, jnp.zeros_like(q2))
                s = lax.dot_general(qm, k2, (((1,), (1,)), ((), ())), preferred_element_type=F32)
                s = s * scale + bias_ref[h]
                s = jnp.where(mask, s, MASK_VALUE)
                m = jnp.max(s, axis=-1, keepdims=True)
                if has_sink:
                    sink = sink_ref[h]
                    m = jnp.maximum(m, sink)
                p = jnp.exp(s - m)
                den = jnp.sum(p, axis=-1, keepdims=True)
                if has_sink:
                    den = den + jnp.exp(sink - m)
                pv = jnp.dot(p.astype(BF16), v2, preferred_element_type=F32)
                outs.append(pv / den)
                if want_lse:
                    lses.append(jnp.broadcast_to(m + jnp.log(den), (BAND, LANES)))
            o_ref[0, rows, sl] = jnp.where(low_half, outs[0], outs[1])
            if want_lse:
                lse_ref[0, rows, sl] = jnp.where(low_half, lses[0], lses[1])


def _band_attn(src, bias, *, r, width, src_blocks, max_dist, sinks=None, want_lse):
    nb, length, _ = src.shape
    qt = min(BAND_Q_TILE, length)
    bands = qt // BAND
    heads = width // HEAD_DIM
    has_sink = sinks is not None

    def cur(off):
        return pl.BlockSpec((1, qt, width), lambda b, p, i: (b, i, p * src_blocks + off))

    def prev(off):
        return pl.BlockSpec((1, BAND, width), lambda b, p, i: (b, jnp.maximum(i * bands - 1, 0), p * src_blocks + off))

    in_specs = [cur(0), prev(1), cur(1), prev(2), cur(2),
                pl.BlockSpec((heads, BAND, 2 * BAND), lambda b, p, i: (0, 0, 0))]
    args = [src, src, src, src, src, bias]
    if has_sink:
        in_specs.append(pl.BlockSpec(memory_space=pltpu.SMEM))
        args.append(sinks)
    out_spec = pl.BlockSpec((1, qt, width), lambda b, p, i: (b, i, p))
    out_sds = jax.ShapeDtypeStruct((nb, length, r * width), F32)
    body = functools.partial(_band_attn_body, width=width, max_dist=max_dist, has_sink=has_sink, want_lse=want_lse)
    return pl.pallas_call(
        body,
        grid=(nb, r, length // qt),
        in_specs=in_specs,
        out_specs=[out_spec, out_spec] if want_lse else out_spec,
        out_shape=[out_sds, out_sds] if want_lse else out_sds,
        compiler_params=_cparams("arbitrary", "arbitrary", "arbitrary"),
        name="band_attn_r%d_w%d" % (r, width),
    )(*args)


def _hgrn_body(b_ref, lb_ref, ng_ref, w_ref, o_ref, state_ref):
    c = HG_CHUNK
    kd = B_KEY_DIM

    @pl.when(pl.program_id(1) == 0)
    def _():
        state_ref[...] = jnp.zeros_like(state_ref)

    trow = lax.broadcasted_iota(I32, (c, 1), 0)
    ti = lax.broadcasted_iota(I32, (c, c), 0)
    si = lax.broadcasted_iota(I32, (c, c), 1)
    nt = (((1,), (1,)), ((), ()))
    w_all = w_ref[...]

    states = [state_ref[h] for h in range(B_HEADS)]
    for chunk, h in [(ci, hi) for ci in range(b_ref.shape[1] // c) for hi in range(B_HEADS)]:
        rows = slice(chunk * c, (chunk + 1) * c)
        ks = slice(h * kd, (h + 1) * kd)
        q = b_ref[0, rows, ks]
        f = b_ref[0, rows, B_KEY_WIDTH + h * kd:B_KEY_WIDTH + (h + 1) * kd]
        inp = b_ref[0, rows, 2 * B_KEY_WIDTH + h * kd:2 * B_KEY_WIDTH + (h + 1) * kd]
        gate = b_ref[0, rows, 3 * B_KEY_WIDTH + h * kd:3 * B_KEY_WIDTH + (h + 1) * kd]
        lb = lb_ref[:, ks]

        log_f = jnp.log(lb + (1.0 - lb) * jax.nn.sigmoid(f))
        key = (1.0 - lb) * jax.nn.sigmoid(-f)
        qs = q * jax.nn.sigmoid(q)

        g_hi = log_f.astype(BF16)
        g_lo = (log_f - g_hi.astype(F32)).astype(BF16)
        e2 = jnp.dot(w_all, jnp.concatenate([g_hi, g_lo], axis=1), preferred_element_type=F32)
        e = e2[:, :kd] + e2[:, kd:]

        cum = e[0:c]
        q_dec = (qs * jnp.exp(cum)).astype(BF16)
        k_dec = (key * jnp.exp(e[c:2 * c])).astype(BF16)
        inp_b = inp.astype(BF16)

        st = states[h]
        inter = lax.dot_general(q_dec, st.astype(BF16), nt, preferred_element_type=F32)

        scores = jnp.where(ti == si,
                           lax.dot_general(qs.astype(BF16), key.astype(BF16), nt, preferred_element_type=F32), 0.0)
        m = c
        for lvl in range(HG_LEVELS):
            el = jnp.exp(e[(2 + lvl) * c:(3 + lvl) * c])
            second = (trow % m) >= (m // 2)
            ql = jnp.where(second, qs * el, 0.0).astype(BF16)
            kl = jnp.where(second, 0.0, key * el).astype(BF16)
            sl = lax.dot_general(ql, kl, nt, preferred_element_type=F32)
            if m < c:
                sl = jnp.where((ti // m) == (si // m), sl, 0.0)
            scores = scores + sl
            m //= 2
        intra = jnp.dot(scores.astype(BF16), inp_b, preferred_element_type=F32)

        new_st = st * jnp.exp(cum[c - 1:c]) + lax.dot_general(
            inp_b, k_dec, (((0,), (0,)), ((), ())), preferred_element_type=F32)
        states[h] = new_st

        o = inter + intra
        ms = jnp.sum(o * o, axis=-1, keepdims=True) * (1.0 / B_VAL_DIM)
        o = o * lax.rsqrt(ms + 1e-6)
        o_ref[0, rows, ks] = o * ng_ref[:, ks] * (gate * jax.nn.sigmoid(gate))

    for h in range(B_HEADS):
        state_ref[h] = states[h]


def _hgrn(b_all, lower_bound, norm_g_pad, w_sum):
    nb, s, _ = b_all.shape
    c = HG_STEP_ROWS
    return pl.pallas_call(
        _hgrn_body,
        grid=(nb, s // c),
        in_specs=[pl.BlockSpec((1, c, B_COLS), lambda b, i: (b, i, 0)),
                  pl.BlockSpec((1, B_KEY_WIDTH), lambda b, i: (0, 0)),
                  pl.BlockSpec((1, B_PAD_WIDTH), lambda b, i: (0, 0)),
                  pl.BlockSpec(w_sum.shape, lambda b, i: (0, 0))],
        out_specs=pl.BlockSpec((1, c, B_PAD_WIDTH), lambda b, i: (b, i, 0)),
        out_shape=jax.ShapeDtypeStruct((nb, s, B_PAD_WIDTH), F32),
        scratch_shapes=[pltpu.VMEM((B_HEADS, B_VAL_PAD, B_KEY_DIM), F32)],
        compiler_params=_cparams("arbitrary", "arbitrary"),
        name="hgrn2",
    )(b_all, lower_bound, norm_g_pad, w_sum)


def _layer_norm_rows(z, g, b):
    mu = jnp.mean(z, axis=-1, keepdims=True)
    zc = z - mu
    var = jnp.mean(zc * zc, axis=-1, keepdims=True)
    return zc * lax.rsqrt(var + LN_EPS) * g + b


def _mix_out_body(o1, o2, o3, l1, l2, l3, ob, oc, x_ref, w_ref, g_ref, b_ref, out_ref):
    la, lb_, lc = l1[...], l2[...], l3[...]
    m = jnp.maximum(jnp.maximum(la, lb_), lc)
    wa, wb, wc = jnp.exp(la - m), jnp.exp(lb_ - m), jnp.exp(lc - m)
    oa = (wa * o1[...] + wb * o2[...] + wc * o3[...]) / (wa + wb + wc)
    cat = jnp.concatenate([oa, ob[...], oc[...]], axis=1).astype(BF16)
    y = jnp.dot(cat, w_ref[...], preferred_element_type=F32)
    z = DEEPNORM_ALPHA * x_ref[...] + y
    out_ref[...] = _layer_norm_rows(z, g_ref[...], b_ref[...])


def _mix_out(o_list, l_list, ob, oc, x2d, w_out_p, ln_g, ln_b):
    t, d = x2d.shape
    tm = 256
    row = lambda w: pl.BlockSpec((tm, w), lambda i: (i, 0))
    full = lambda a: pl.BlockSpec(a.shape, lambda i: (0, 0))
    return pl.pallas_call(
        _mix_out_body,
        grid=(t // tm,),
        in_specs=[row(A_WIDTH)] * 6 + [row(B_PAD_WIDTH), row(C_WIDTH), row(d), full(w_out_p), full(ln_g), full(ln_b)],
        out_specs=row(d),
        out_shape=jax.ShapeDtypeStruct((t, d), F32),
        compiler_params=_cparams("arbitrary"),
        name="mix_out_ln",
    )(*o_list, *l_list, ob, oc, x2d, w_out_p, ln_g, ln_b)


def _router_body(h_ref, rw_ref, bias_ref, tri_ref, e_ref, rank_ref, gate_ref, cnt_ref, carry_ref):
    tn = h_ref.shape[0]
    per_group = N_EXPERTS // N_GROUPS
    neg_inf = -jnp.inf

    @pl.when(pl.program_id(0) == 0)
    def _():
        carry_ref[...] = jnp.zeros_like(carry_ref)

    logits = lax.dot_general(rw_ref[...], h_ref[...], (((1,), (1,)), ((), ())),
                             precision=lax.Precision.HIGHEST, preferred_element_type=F32)
    scores = jax.nn.sigmoid(logits)
    choice = scores + bias_ref[...]

    def first_max(vals, idx, sentinel):
        top = jnp.max(vals, axis=0, keepdims=True)
        return top, jnp.min(jnp.where(vals == top, idx, sentinel), axis=0, keepdims=True)

    li = lax.broadcasted_iota(I32, (per_group, tn), 0).astype(F32)
    group_rows = []
    for g in range(N_GROUPS):
        cg = choice[g * per_group:(g + 1) * per_group]
        m1, first = first_max(cg, li, float(per_group))
        m2 = jnp.max(jnp.where(li == first, neg_inf, cg), axis=0, keepdims=True)
        group_rows.append(m1 + m2)
    group_score = jnp.concatenate(group_rows, axis=0)

    gi = lax.broadcasted_iota(I32, (N_GROUPS, tn), 0).astype(F32)
    group_ok = jnp.zeros((N_GROUPS, tn), F32)
    cur = group_score
    for _ in range(TOPK_GROUPS):
        _, first = first_max(cur, gi, float(N_GROUPS))
        pick = gi == first
        group_ok = jnp.where(pick, 1.0, group_ok)
        cur = jnp.where(pick, neg_inf, cur)

    cur = jnp.concatenate(
        [jnp.where(group_ok[g:g + 1] > 0.0, choice[g * per_group:(g + 1) * per_group], MASK_VALUE)
         for g in range(N_GROUPS)], axis=0)
    ei = lax.broadcasted_iota(I32, (N_EXPERTS, tn), 0).astype(F32)
    chosen = jnp.zeros((N_EXPERTS, tn), F32)
    picks, gates = [], []
    for _ in range(TOP_K):
        _, idx = first_max(cur, ei, float(N_EXPERTS))
        pick = ei == idx
        picks.append(idx)
        gates.append(jnp.sum(jnp.where(pick, scores, 0.0), axis=0, keepdims=True))
        chosen = jnp.where(pick, 1.0, chosen)
        cur = jnp.where(pick, neg_inf, cur)

    gate = jnp.concatenate(gates, axis=0)
    gate_ref[...] = gate / jnp.sum(gate, axis=0, keepdims=True) * ROUTED_SCALE
    e_ref[...] = jnp.concatenate(picks, axis=0).astype(I32)

    before = jnp.dot(chosen.astype(BF16), tri_ref[...], preferred_element_type=F32) + carry_ref[...]
    ranks = [jnp.sum(jnp.where(ei == idx, before, 0.0), axis=0, keepdims=True) for idx in picks]
    rank_ref[...] = jnp.concatenate(ranks, axis=0).astype(I32)
    carry = carry_ref[...] + jnp.sum(chosen, axis=1, keepdims=True)
    carry_ref[...] = carry
    cnt_ref[...] = carry.astype(I32)


def _router(h2d, rw_t, bias_col):
    t, d = h2d.shape
    tn = 256
    tri = jnp.asarray(np.triu(np.ones((tn, tn), np.float32), k=1), dtype=BF16)
    tok = lambda: pl.BlockSpec((TOP_K, tn), lambda i: (0, i))
    return pl.pallas_call(
        _router_body,
        grid=(t // tn,),
        in_specs=[pl.BlockSpec((tn, d), lambda i: (i, 0)),
                  pl.BlockSpec((N_EXPERTS, d), lambda i: (0, 0)),
                  pl.BlockSpec((N_EXPERTS, 1), lambda i: (0, 0)),
                  pl.BlockSpec((tn, tn), lambda i: (0, 0))],
        out_specs=[tok(), tok(), tok(), pl.BlockSpec((N_EXPERTS, 1), lambda i: (0, 0))],
        out_shape=[jax.ShapeDtypeStruct((TOP_K, t), I32),
                   jax.ShapeDtypeStruct((TOP_K, t), I32),
                   jax.ShapeDtypeStruct((TOP_K, t), F32),
                   jax.ShapeDtypeStruct((N_EXPERTS, 1), I32)],
        scratch_shapes=[pltpu.VMEM((N_EXPERTS, 1), F32)],
        compiler_params=_cparams("arbitrary"),
        name="moe_router",
    )(h2d, rw_t, bias_col, tri)


def _pack_bf16_pairs(x):
    w = x.shape[1] // 2
    hi = lax.bitcast_convert_type(x[:, :w].astype(BF16).astype(F32), U32)
    lo = lax.bitcast_convert_type(x[:, w:].astype(BF16).astype(F32), U32)
    return hi | (lo >> 16)


def _unpack_bf16_pairs(p):
    hi = lax.bitcast_convert_type(p & jnp.uint32(0xFFFF0000), F32)
    lo = lax.bitcast_convert_type(p << 16, F32)
    return hi, lo


def _dest_body(e_ref, rank_ref, offs_ref, dest_ref):
    tn = e_ref.shape[1]
    ei = lax.broadcasted_iota(I32, (N_EXPERTS, tn), 0)
    offs = offs_ref[...]
    rows = [jnp.sum(jnp.where(ei == e_ref[k:k + 1, :], offs, 0.0), axis=0, keepdims=True) for k in range(TOP_K)]
    dest_ref[...] = jnp.concatenate(rows, axis=0).astype(I32) + rank_ref[...]


def _dest_rows(e_t, rank_t, offsets):
    t = e_t.shape[1]
    tn = 512
    tok = pl.BlockSpec((TOP_K, tn), lambda i: (0, i))
    return pl.pallas_call(
        _dest_body,
        grid=(t // tn,),
        in_specs=[tok, tok, pl.BlockSpec((N_EXPERTS, 1), lambda i: (0, 0))],
        out_specs=tok,
        out_shape=jax.ShapeDtypeStruct((TOP_K, t), I32),
        compiler_params=_cparams("arbitrary"),
        name="moe_dest",
    )(e_t, rank_t, offsets.astype(F32).reshape(N_EXPERTS, 1))


def _dispatch_body(pad_end_ref, dest_ref, h_ref, xs_hbm, packed, zero_buf, sem):
    step = pl.program_id(0)
    tt = h_ref.shape[0]
    bm = zero_buf.shape[0]

    def tail_copy(e):
        end = pl.multiple_of(pad_end_ref[e], bm)
        return pltpu.make_async_copy(zero_buf, xs_hbm.at[pl.ds(end - bm, bm)], sem)

    def has_rows(e):
        prev = jnp.where(e > 0, pad_end_ref[jnp.maximum(e - 1, 0)], 0)
        return pad_end_ref[e] > prev

    @pl.when(step == 0)
    def _():
        zero_buf[...] = jnp.zeros_like(zero_buf)

        def start(e, carry):
            @pl.when(has_rows(e))
            def _():
                tail_copy(e).start()
            return carry

        def wait(e, carry):
            @pl.when(has_rows(e))
            def _():
                tail_copy(e).wait()
            return carry

        lax.fori_loop(0, N_EXPERTS, start, 0)
        lax.fori_loop(0, N_EXPERTS, wait, 0)

    packed[...] = _pack_bf16_pairs(h_ref[...])

    def start_rows(t, carry):
        src = packed.at[pl.ds(t, 1)]
        for k in range(TOP_K):
            pltpu.make_async_copy(src, xs_hbm.at[pl.ds(dest_ref[t * TOP_K + k], 1)], sem).start()
        return carry

    lax.fori_loop(0, tt, start_rows, 0)
    for k in range(TOP_K):
        pltpu.make_async_copy(packed, xs_hbm.at[pl.ds(0, tt)], sem).wait()


def _dispatch(h2d, dest_flat, pad_end, n_rows):
    t, d = h2d.shape
    tt = 512
    grid_spec = pltpu.PrefetchScalarGridSpec(
        num_scalar_prefetch=1,
        grid=(t // tt,),
        in_specs=[pl.BlockSpec((tt * TOP_K,), lambda i, pe: (i,), memory_space=pltpu.SMEM),
                  pl.BlockSpec((tt, d), lambda i, pe: (i, 0))],
        out_specs=pl.BlockSpec(memory_space=pl.ANY),
        scratch_shapes=[pltpu.VMEM((tt, d // 2), U32), pltpu.VMEM((EXPERT_BLOCK, d // 2), U32),
                        pltpu.SemaphoreType.DMA(())],
    )
    return pl.pallas_call(
        _dispatch_body,
        grid_spec=grid_spec,
        out_shape=jax.ShapeDtypeStruct((n_rows, d // 2), U32),
        compiler_params=_cparams("arbitrary"),
        name="moe_dispatch",
    )(pad_end, dest_flat, h2d)


def _expert_body(offs_ref, nblk_ref, layer_ref, xs_hbm, wg_ref, wu_ref, wd_ref, ys_hbm,
                 xbuf, ybuf, wg_b, wu_b, wd_b, sem_in, sem_out):
    e = pl.program_id(0)
    slots, bm = xbuf.shape[:2]
    ahead = slots - 1
    n = nblk_ref[e]
    first = offs_ref[e] // bm
    total = (offs_ref[N_EXPERTS - 1] // bm) + nblk_ref[N_EXPERTS - 1]

    def rows(g):
        return pl.ds(pl.multiple_of(g * bm, bm), bm)

    def x_copy(g, slot):
        return pltpu.make_async_copy(xs_hbm.at[rows(g)], xbuf.at[slot], sem_in.at[slot])

    def y_copy(g, slot):
        return pltpu.make_async_copy(ybuf.at[slot], ys_hbm.at[rows(g)], sem_out.at[slot])

    @pl.when(e == 0)
    def _():
        for g0 in range(ahead):
            @pl.when(g0 < total)
            def _():
                x_copy(g0, g0).start()

    wg_b[...] = wg_ref[0, 0].astype(BF16)
    wu_b[...] = wu_ref[0, 0].astype(BF16)
    wd_b[...] = wd_ref[0, 0].astype(BF16)

    def block(j, carry):
        g = first + j
        slot = g % slots
        x_copy(g, slot).wait()

        @pl.when(g + ahead < total)
        def _():
            x_copy(g + ahead, (g + ahead) % slots).start()

        @pl.when(g >= slots)
        def _():
            y_copy(g - slots, slot).wait()

        hi, lo = _unpack_bf16_pairs(xbuf[slot])
        xb = jnp.concatenate([hi.astype(BF16), lo.astype(BF16)], axis=1)
        gate = jnp.dot(xb, wg_b[...], preferred_element_type=F32)
        up = jnp.dot(xb, wu_b[...], preferred_element_type=F32)
        hidden = (gate * jax.nn.sigmoid(gate) * up).astype(BF16)
        ybuf[slot] = _pack_bf16_pairs(jnp.dot(hidden, wd_b[...], preferred_element_type=F32))
        y_copy(g, slot).start()
        return carry

    lax.fori_loop(0, n, block, 0)

    @pl.when(e == N_EXPERTS - 1)
    def _():
        for back in range(1, slots + 1):
            @pl.when(total >= back)
            def _():
                y_copy(total - back, (total - back) % slots).wait()


def _experts(xs, offsets, n_blk, layer, w_gate, w_up, w_down):
    n_rows, dp = xs.shape
    bm = EXPERT_BLOCK
    d, ff = w_gate.shape[-2:]
    w_spec = lambda a, b: pl.BlockSpec((1, 1, a, b), lambda e, of, nb, ly: (ly[0], e, 0, 0))
    grid_spec = pltpu.PrefetchScalarGridSpec(
        num_scalar_prefetch=3,
        grid=(N_EXPERTS,),
        in_specs=[pl.BlockSpec(memory_space=pl.ANY), w_spec(d, ff), w_spec(d, ff), w_spec(ff, d)],
        out_specs=pl.BlockSpec(memory_space=pl.ANY),
        scratch_shapes=[pltpu.VMEM((EXPERT_SLOTS, bm, dp), U32), pltpu.VMEM((EXPERT_SLOTS, bm, dp), U32),
                        pltpu.VMEM((d, ff), BF16), pltpu.VMEM((d, ff), BF16), pltpu.VMEM((ff, d), BF16),
                        pltpu.SemaphoreType.DMA((EXPERT_SLOTS,)), pltpu.SemaphoreType.DMA((EXPERT_SLOTS,))],
    )
    return pl.pallas_call(
        _expert_body,
        grid_spec=grid_spec,
        out_shape=jax.ShapeDtypeStruct((n_rows, dp), U32),
        compiler_params=_cparams("arbitrary"),
        name="moe_experts",
    )(offsets, n_blk, layer, xs, w_gate, w_up, w_down)


def _combine_body(dest_ref, dest_next_ref, gate_ref, h_ref, ys_hbm, sg_ref, su_ref, sd_ref, g_ref, b_ref, out_ref,
                  buf_a, buf_b, sem_a, sem_b):
    step = pl.program_id(0)
    tt = buf_a.shape[1]
    half = h_ref.shape[1] // 2

    def issue(dref, first, buf, sem):
        for t in range(tt):
            for k in range(TOP_K):
                row = dref[(first + t) * TOP_K + k]
                pltpu.make_async_copy(ys_hbm.at[pl.ds(row, 1)], buf.at[k, pl.ds(t, 1)], sem).start()

    def drain(buf, sem):
        for k in range(TOP_K):
            pltpu.make_async_copy(ys_hbm.at[pl.ds(0, tt)], buf.at[k], sem).wait()

    def finish(first, buf):
        rows = pl.ds(first, tt)
        h = h_ref[rows, :]
        hb = h.astype(BF16)
        sg = jnp.dot(hb, sg_ref[...], preferred_element_type=F32)
        su = jnp.dot(hb, su_ref[...], preferred_element_type=F32)
        shared = jnp.dot((sg * jax.nn.sigmoid(sg) * su).astype(BF16), sd_ref[...], preferred_element_type=F32)
        z = DEEPNORM_ALPHA * h + shared
        z_hi, z_lo = z[:, :half], z[:, half:]
        for k in range(TOP_K):
            y_hi, y_lo = _unpack_bf16_pairs(buf[k])
            gate = gate_ref[rows, k:k + 1]
            z_hi = z_hi + gate * y_hi
            z_lo = z_lo + gate * y_lo
        out_ref[rows, :] = _layer_norm_rows(jnp.concatenate([z_hi, z_lo], axis=1), g_ref[...], b_ref[...])

    @pl.when(step == 0)
    def _():
        def start_rows(t, carry):
            for k in range(TOP_K):
                pltpu.make_async_copy(ys_hbm.at[pl.ds(dest_ref[t * TOP_K + k], 1)], buf_a.at[k, pl.ds(t, 1)],
                                      sem_a).start()
            return carry
        lax.fori_loop(0, tt, start_rows, 0)

    drain(buf_a, sem_a)
    issue(dest_ref, tt, buf_b, sem_b)
    finish(0, buf_a)
    issue(dest_next_ref, 0, buf_a, sem_a)
    drain(buf_b, sem_b)
    finish(tt, buf_b)

    @pl.when(step == pl.num_programs(0) - 1)
    def _():
        drain(buf_a, sem_a)


def _combine(dest_flat, gate_tk, h2d, ys, sh_gate, sh_up, sh_down, ln_g, ln_b):
    t, d = h2d.shape
    tt = 128
    steps = t // (2 * tt)
    full = lambda a: pl.BlockSpec(a.shape, lambda i: (0, 0))
    return pl.pallas_call(
        _combine_body,
        grid=(steps,),
        in_specs=[pl.BlockSpec((2 * tt * TOP_K,), lambda i: (i,), memory_space=pltpu.SMEM),
                  pl.BlockSpec((2 * tt * TOP_K,), lambda i: (jnp.minimum(i + 1, steps - 1),), memory_space=pltpu.SMEM),
                  pl.BlockSpec((2 * tt, TOP_K), lambda i: (i, 0)),
                  pl.BlockSpec((2 * tt, d), lambda i: (i, 0)),
                  pl.BlockSpec(memory_space=pl.ANY),
                  full(sh_gate), full(sh_up), full(sh_down), full(ln_g), full(ln_b)],
        out_specs=pl.BlockSpec((2 * tt, d), lambda i: (i, 0)),
        out_shape=jax.ShapeDtypeStruct((t, d), F32),
        scratch_shapes=[pltpu.VMEM((TOP_K, tt, d // 2), U32), pltpu.VMEM((TOP_K, tt, d // 2), U32),
                        pltpu.SemaphoreType.DMA(()), pltpu.SemaphoreType.DMA(())],
        compiler_params=_cparams("arbitrary"),
        name="moe_combine_ln",
    )(dest_flat, dest_flat, gate_tk, h2d, ys, sh_gate, sh_up, sh_down, ln_g, ln_b)


SC_GATHER_ROWS = 64


def _sc_gather_rows(table, idx):
    info = plsc.get_sparse_core_info()
    nc, ns = info.num_cores, info.num_subcores
    workers = nc * ns
    n = idx.shape[0]
    w = table.shape[1]
    ch = SC_GATHER_ROWS
    per_worker = n // workers
    steps = per_worker // ch
    assert per_worker * workers == n and steps * ch == per_worker and steps % 2 == 0
    mesh = plsc.VectorSubcoreMesh(core_axis_name="c", subcore_axis_name="s")

    @functools.partial(
        pl.kernel, mesh=mesh,
        out_type=jax.ShapeDtypeStruct((n, w), table.dtype),
        scratch_types=[pltpu.VMEM((steps, ch), I32), pltpu.VMEM((2, ch, w), table.dtype),
                       pltpu.SemaphoreType.DMA((2,)), pltpu.SemaphoreType.DMA((2,))],
    )
    def gather_kernel(table_hbm, idx_hbm, out_hbm, idx_v, rows_v, gsem, wsem):
        wid = lax.axis_index("s") * nc + lax.axis_index("c")
        base = wid * per_worker
        pltpu.sync_copy(idx_hbm.at[wid], idx_v)

        def gather(i, slot):
            return pltpu.make_async_copy(table_hbm.at[idx_v.at[i]], rows_v.at[slot], gsem.at[slot])

        def write(i, slot):
            off = pl.multiple_of(base + i * ch, ch)
            return pltpu.make_async_copy(rows_v.at[slot], out_hbm.at[pl.ds(off, ch)], wsem.at[slot])

        gather(0, 0).start()

        @pl.loop(0, steps, step=2)
        def _(i):
            gather(i + 1, 1).start()
            gather(i, 0).wait()
            write(i, 0).start()
            write(i, 0).wait()

            @pl.when(i + 2 < steps)
            def _():
                gather(i + 2, 0).start()

            gather(i + 1, 1).wait()
            write(i + 1, 1).start()
            write(i + 1, 1).wait()

    return gather_kernel(table, idx.reshape(workers, steps, ch))


def _combine_streamed_body(y_ref, gate_ref, h_ref, sg_ref, su_ref, sd_ref, g_ref, b_ref, out_ref):
    half = h_ref.shape[1] // 2
    h = h_ref[...]
    hb = h.astype(BF16)
    sg = jnp.dot(hb, sg_ref[...], preferred_element_type=F32)
    su = jnp.dot(hb, su_ref[...], preferred_element_type=F32)
    shared = jnp.dot((sg * jax.nn.sigmoid(sg) * su).astype(BF16), sd_ref[...], preferred_element_type=F32)
    z = DEEPNORM_ALPHA * h + shared
    z_hi, z_lo = z[:, :half], z[:, half:]
    for k in range(TOP_K):
        y_hi, y_lo = _unpack_bf16_pairs(y_ref[k])
        gate = gate_ref[:, k:k + 1]
        z_hi = z_hi + gate * y_hi
        z_lo = z_lo + gate * y_lo
    out_ref[...] = _layer_norm_rows(jnp.concatenate([z_hi, z_lo], axis=1), g_ref[...], b_ref[...])


def _combine_streamed(y_gathered, gate_tk, h2d, sh_gate, sh_up, sh_down, ln_g, ln_b):
    t, d = h2d.shape
    tt = 256
    full = lambda a: pl.BlockSpec(a.shape, lambda i: (0, 0))
    return pl.pallas_call(
        _combine_streamed_body,
        grid=(t // tt,),
        in_specs=[pl.BlockSpec((TOP_K, tt, d // 2), lambda i: (0, i, 0)),
                  pl.BlockSpec((tt, TOP_K), lambda i: (i, 0)),
                  pl.BlockSpec((tt, d), lambda i: (i, 0)),
                  full(sh_gate), full(sh_up), full(sh_down), full(ln_g), full(ln_b)],
        out_specs=pl.BlockSpec((tt, d), lambda i: (i, 0)),
        out_shape=jax.ShapeDtypeStruct((t, d), F32),
        compiler_params=_cparams("arbitrary"),
        name="moe_combine_streamed_ln",
    )(y_gathered, gate_tk, h2d, sh_gate, sh_up, sh_down, ln_g, ln_b)


def _mixer_sublayer(h2d, nb, s, w_in_p, w_out_p, a_biases, c_bias, lower_bound, norm_g_pad, sinks, w_sum, ln_g, ln_b):
    a_qkv, b_all, c_qkv = _in_proj(h2d, w_in_p)
    o_list, l_list = [], []
    for (window, r), bias in zip(A_PATTERNS, a_biases):
        src = a_qkv.reshape(nb, s // r, r * A_COLS)
        o, lse = _band_attn(src, bias, r=r, width=A_WIDTH, src_blocks=3, max_dist=window // r, want_lse=True)
        o_list.append(o.reshape(nb * s, A_WIDTH))
        l_list.append(lse.reshape(nb * s, A_WIDTH))
    oc = _band_attn(c_qkv.reshape(nb, s, C_COLS), c_bias, r=1, width=C_WIDTH, src_blocks=3,
                    max_dist=C_WINDOW - 1, sinks=sinks, want_lse=False).reshape(nb * s, C_WIDTH)
    ob = _hgrn(b_all.reshape(nb, s, B_COLS), lower_bound, norm_g_pad, w_sum).reshape(nb * s, B_PAD_WIDTH)
    return _mix_out(o_list, l_list, ob, oc, h2d, w_out_p, ln_g, ln_b)


def _moe_sublayer(h2d, layer, router_w, router_bias, w_gate, w_up, w_down, sh_gate, sh_up, sh_down, ln_g, ln_b):
    t, d = h2d.shape
    bm = EXPERT_BLOCK
    e_t, rank_t, gate_t, counts = _router(h2d, router_w.T.astype(F32), router_bias.astype(F32).reshape(N_EXPERTS, 1))

    counts = counts.reshape(N_EXPERTS)
    padded = (counts + bm - 1) // bm * bm
    pad_end = jnp.cumsum(padded).astype(I32)
    offsets = pad_end - padded
    n_blocks = -(-(t * TOP_K + N_EXPERTS * (bm - 1)) // bm)

    dest_t = _dest_rows(e_t, rank_t, offsets)
    xs = _dispatch(h2d, dest_t.T.reshape(t * TOP_K), pad_end, n_blocks * bm)
    ys = _experts(xs, offsets, (padded // bm).astype(I32), jnp.full((1,), layer, I32), w_gate, w_up, w_down)
    y_gathered = _sc_gather_rows(ys, dest_t.reshape(TOP_K * t)).reshape(TOP_K, t, d // 2)
    return _combine_streamed(y_gathered, gate_t.T, h2d,
                             sh_gate.astype(BF16), sh_up.astype(BF16), sh_down.astype(BF16), ln_g, ln_b)


def kernel(x, w_in, w_out, rel_bias_table, lower_bound_logits, hgrn_norm_g, attn_sinks, ln1_g, ln1_b, router_w, router_bias, expert_w_gate, expert_w_up, expert_w_down, shared_w_gate, shared_w_up, shared_w_down, ln2_g, ln2_b):
    nb, s, d = x.shape
    depth = w_in.shape[0]
    lb_probs = jax.nn.softmax(lower_bound_logits.astype(F32), axis=0)
    lower_bounds = jnp.cumsum(lb_probs, axis=0) - lb_probs[0]
    rel_table = rel_bias_table.astype(F32)
    a_biases = [_band_bias(rel_table, r, 0, A_HEADS) for _, r in A_PATTERNS]
    c_bias = _band_bias(rel_table, 1, A_HEADS, A_HEADS + C_HEADS)
    w_sum = jnp.asarray(_hgrn_sum_matrix(), dtype=BF16)
    row = lambda v: v.astype(F32).reshape(1, -1)

    h = x.astype(F32).reshape(nb * s, d)
    for l in range(depth):
        h = _mixer_sublayer(h, nb, s, _prep_w_in(w_in[l]), _prep_w_out(w_out[l].astype(F32)), a_biases, c_bias,
                            lower_bounds[l].reshape(1, B_KEY_WIDTH), _pad_heads_vec(hgrn_norm_g[l].astype(F32)),
                            attn_sinks[l].astype(F32), w_sum, row(ln1_g[l]), row(ln1_b[l]))
        h = _moe_sublayer(h, l, router_w[l], router_bias[l], expert_w_gate, expert_w_up, expert_w_down,
                          shared_w_gate[l], shared_w_up[l], shared_w_down[l], row(ln2_g[l]), row(ln2_b[l]))
    return h.reshape(nb, s, d).astype(x.dtype)
```

```python
import functools
import math

import numpy as np
import jax
import jax.numpy as jnp
from jax import lax
from jax.experimental import pallas as pl
from jax.experimental.pallas import tpu as pltpu
from jax.experimental.pallas import tpu_sc as plsc

F32 = jnp.float32
BF16 = jnp.bfloat16
I32 = jnp.int32
U32 = jnp.uint32

LANES = 128
SUBLANES = 8
VMEM_LIMIT = 56 * 1024 * 1024

D_MODEL = 1024
DEPTH = 2
HEAD_DIM = 64
BAND = 128
BAND_Q_TILE = 512
MASK_VALUE = -1e30

A_HEADS = 6
A_PATTERNS = ((128, 1), (512, 4), (2048, 16))
B_HEADS = 4
B_KEY_DIM = 128
B_VAL_DIM = 96
B_VAL_PAD = 128
HG_CHUNK = 64
HG_STEP_ROWS = 128
C_HEADS = 4
C_KV_HEADS = 2
C_WINDOW = 128

A_WIDTH = A_HEADS * HEAD_DIM
B_KEY_WIDTH = B_HEADS * B_KEY_DIM
B_WIDTH = B_HEADS * B_VAL_DIM
B_PAD_WIDTH = B_HEADS * B_VAL_PAD
C_WIDTH = C_HEADS * HEAD_DIM
C_KV_WIDTH = C_KV_HEADS * HEAD_DIM
IN_SPLITS = (A_WIDTH, A_WIDTH, A_WIDTH, B_KEY_WIDTH, B_KEY_WIDTH, B_WIDTH, B_WIDTH, C_WIDTH, C_KV_WIDTH, C_KV_WIDTH)

A_COLS = 3 * A_WIDTH
B_COLS = 4 * B_KEY_WIDTH
C_COLS = 3 * C_WIDTH
MIX_PAD_WIDTH = A_WIDTH + B_PAD_WIDTH + C_WIDTH

REL_BUCKETS = 32
REL_MAX_DIST = 2048

N_EXPERTS = 256
TOP_K = 8
N_GROUPS = 8
TOPK_GROUPS = 4
EXPERT_FF = 256
SHARED_FF = 256
ROUTED_SCALE = 2.5
EXPERT_BLOCK = 256
EXPERT_SLOTS = 4

DEEPNORM_ALPHA = (2 * DEPTH) ** 0.25
LN_EPS = 1e-5


def _cparams(*sem):
    return pltpu.CompilerParams(dimension_semantics=sem, vmem_limit_bytes=VMEM_LIMIT)


def _prep_w_in(w_in):
    d = w_in.shape[0]
    split_at = [int(i) for i in np.cumsum(IN_SPLITS)[:-1]]
    aq, ak, av, bq, bf, bi, bg, cq, ck, cv = jnp.split(w_in, split_at, axis=-1)
    pad_v = lambda w: jnp.pad(w.reshape(d, B_HEADS, B_VAL_DIM), ((0, 0), (0, 0), (0, B_VAL_PAD - B_VAL_DIM))).reshape(d, B_PAD_WIDTH)
    rep = lambda w: jnp.repeat(w.reshape(d, C_KV_HEADS, HEAD_DIM), C_HEADS // C_KV_HEADS, axis=1).reshape(d, C_WIDTH)
    cols = [aq, ak, av, bq, bf, pad_v(bi), pad_v(bg), cq, rep(ck), rep(cv)]
    return jnp.concatenate(cols, axis=-1).astype(BF16)


def _prep_w_out(w_out):
    d = w_out.shape[1]
    wa = w_out[:A_WIDTH]
    wb = w_out[A_WIDTH:A_WIDTH + B_WIDTH].reshape(B_HEADS, B_VAL_DIM, d)
    wb = jnp.pad(wb, ((0, 0), (0, B_VAL_PAD - B_VAL_DIM), (0, 0))).reshape(B_PAD_WIDTH, d)
    wc = w_out[A_WIDTH + B_WIDTH:]
    return jnp.concatenate([wa, wb, wc], axis=0).astype(BF16)


def _pad_heads_vec(v):
    return jnp.pad(v.reshape(B_HEADS, B_VAL_DIM), ((0, 0), (0, B_VAL_PAD - B_VAL_DIM))).reshape(1, B_PAD_WIDTH)


def _rel_bucket(dist):
    max_exact = REL_BUCKETS // 2
    d = jnp.maximum(dist, 0)
    log_ratio = jnp.log(jnp.maximum(d, max_exact).astype(F32) / max_exact) / math.log(REL_MAX_DIST / max_exact)
    large = jnp.minimum(max_exact + (log_ratio * (REL_BUCKETS - max_exact)).astype(I32), REL_BUCKETS - 1)
    return jnp.where(d < max_exact, d, large)


def _band_bias(rel_table, r, head_lo, head_hi):
    dist = jnp.arange(BAND)[:, None] + BAND - jnp.arange(2 * BAND)[None, :]
    onehot = jax.nn.one_hot(_rel_bucket(dist * r), REL_BUCKETS, dtype=F32)
    return jnp.einsum("qkb,bh->hqk", onehot, rel_table[:, head_lo:head_hi], precision=lax.Precision.HIGHEST)


def _hgrn_sum_matrix():
    c = HG_CHUNK
    t = np.arange(c)[:, None]
    u = np.arange(c)[None, :]
    blocks = [(u <= t), (u > t)]
    m = c
    while m >= 2:
        mid = (t // m) * m + m // 2
        second = t >= mid
        blocks.append(np.where(second, (u >= mid) & (u <= t), (u > t) & (u < mid)))
        m //= 2
    return np.concatenate(blocks, axis=0).astype(np.float32)


HG_LEVELS = int(math.log2(HG_CHUNK))


def _in_proj_body(x_ref, w_ref, a_ref, b_ref, c_ref):
    xb = x_ref[...].astype(BF16)
    a_ref[...] = jnp.dot(xb, w_ref[:, :A_COLS], preferred_element_type=F32).astype(BF16)
    for j in range(B_COLS // B_KEY_WIDTH):
        lo = A_COLS + j * B_KEY_WIDTH
        b_ref[:, j * B_KEY_WIDTH:(j + 1) * B_KEY_WIDTH] = jnp.dot(
            xb, w_ref[:, lo:lo + B_KEY_WIDTH], preferred_element_type=F32)
    c_ref[...] = jnp.dot(xb, w_ref[:, A_COLS + B_COLS:], preferred_element_type=F32).astype(BF16)


def _in_proj(x2d, w_p):
    t, d = x2d.shape
    tm = 512
    n = w_p.shape[1]
    return pl.pallas_call(
        _in_proj_body,
        grid=(t // tm,),
        in_specs=[pl.BlockSpec((tm, d), lambda i: (i, 0)),
                  pl.BlockSpec((d, n), lambda i: (0, 0))],
        out_specs=[pl.BlockSpec((tm, A_COLS), lambda i: (i, 0)),
                   pl.BlockSpec((tm, B_COLS), lambda i: (i, 0)),
                   pl.BlockSpec((tm, C_COLS), lambda i: (i, 0))],
        out_shape=[jax.ShapeDtypeStruct((t, A_COLS), BF16),
                   jax.ShapeDtypeStruct((t, B_COLS), F32),
                   jax.ShapeDtypeStruct((t, C_COLS), BF16)],
        compiler_params=_cparams("arbitrary"),
        name="in_proj",
    )(x2d, w_p)


def _band_attn_body(*refs, width, max_dist, has_sink, want_lse):
    q_ref, kp_ref, kc_ref, vp_ref, vc_ref, bias_ref = refs[:6]
    rest = refs[6:]
    if has_sink:
        sink_ref, rest = rest[0], rest[1:]
    o_ref = rest[0]
    lse_ref = rest[1] if want_lse else None

    first_tile = pl.program_id(2) == 0
    row = lax.broadcasted_iota(I32, (BAND, 2 * BAND), 0)
    col = lax.broadcasted_iota(I32, (BAND, 2 * BAND), 1)
    dist = row + BAND - col
    in_band = (dist >= 0) & (dist <= max_dist)
    first_mask = in_band & ((col >= BAND) | jnp.logical_not(first_tile))
    low_half = lax.broadcasted_iota(I32, (BAND, LANES), 1) < HEAD_DIM
    scale = HEAD_DIM ** -0.5

    for qb in range(q_ref.shape[1] // BAND):
        rows = slice(qb * BAND, (qb + 1) * BAND)
        mask = first_mask if qb == 0 else in_band
        for tile in range(width // LANES):
            sl = slice(tile * LANES, (tile + 1) * LANES)
            q2 = q_ref[0, rows, sl]
            if qb == 0:
                k2 = jnp.concatenate([kp_ref[0, :, sl], kc_ref[0, :BAND, sl]], axis=0)
                v2 = jnp.concatenate([vp_ref[0, :, sl], vc_ref[0, :BAND, sl]], axis=0)
            else:
                k2 = kc_ref[0, (qb - 1) * BAND:(qb + 1) * BAND, sl]
                v2 = vc_ref[0, (qb - 1) * BAND:(qb + 1) * BAND, sl]
            outs, lses = [], []
            for half in range(2):
                h = 2 * tile + half
                qm = jnp.where(low_half if half == 0 else jnp.logical_not(low_half), q2, jnp.zeros_like(q2))
                s = lax.dot_general(qm, k2, (((1,), (1,)), ((), ())), preferred_element_type=F32)
                s = s * scale + bias_ref[h]
                s = jnp.where(mask, s, MASK_VALUE)
                m = jnp.max(s, axis=-1, keepdims=True)
                if has_sink:
                    sink = sink_ref[h]
                    m = jnp.maximum(m, sink)
                p = jnp.exp(s - m)
                den = jnp.sum(p, axis=-1, keepdims=True)
                if has_sink:
                    den = den + jnp.exp(sink - m)
                pv = jnp.dot(p.astype(BF16), v2, preferred_element_type=F32)
                outs.append(pv / den)
                if want_lse:
                    lses.append(jnp.broadcast_to(m + jnp.log(den), (BAND, LANES)))
            o_ref[0, rows, sl] = jnp.where(low_half, outs[0], outs[1])
            if want_lse:
                lse_ref[0, rows, sl] = jnp.where(low_half, lses[0], lses[1])


def _band_attn(src, bias, *, r, width, src_blocks, max_dist, sinks=None, want_lse):
    nb, length, _ = src.shape
    qt = min(BAND_Q_TILE, length)
    bands = qt // BAND
    heads = width // HEAD_DIM
    has_sink = sinks is not None

    def cur(off):
        return pl.BlockSpec((1, qt, width), lambda b, p, i: (b, i, p * src_blocks + off))

    def prev(off):
        return pl.BlockSpec((1, BAND, width), lambda b, p, i: (b, jnp.maximum(i * bands - 1, 0), p * src_blocks + off))

    in_specs = [cur(0), prev(1), cur(1), prev(2), cur(2),
                pl.BlockSpec((heads, BAND, 2 * BAND), lambda b, p, i: (0, 0, 0))]
    args = [src, src, src, src, src, bias]
    if has_sink:
        in_specs.append(pl.BlockSpec(memory_space=pltpu.SMEM))
        args.append(sinks)
    out_spec = pl.BlockSpec((1, qt, width), lambda b, p, i: (b, i, p))
    out_sds = jax.ShapeDtypeStruct((nb, length, r * width), F32)
    body = functools.partial(_band_attn_body, width=width, max_dist=max_dist, has_sink=has_sink, want_lse=want_lse)
    return pl.pallas_call(
        body,
        grid=(nb, r, length // qt),
        in_specs=in_specs,
        out_specs=[out_spec, out_spec] if want_lse else out_spec,
        out_shape=[out_sds, out_sds] if want_lse else out_sds,
        compiler_params=_cparams("arbitrary", "arbitrary", "arbitrary"),
        name="band_attn_r%d_w%d" % (r, width),
    )(*args)


def _hgrn_body(b_ref, lb_ref, ng_ref, w_ref, o_ref, state_ref):
    c = HG_CHUNK
    kd = B_KEY_DIM

    @pl.when(pl.program_id(1) == 0)
    def _():
        state_ref[...] = jnp.zeros_like(state_ref)

    trow = lax.broadcasted_iota(I32, (c, 1), 0)
    ti = lax.broadcasted_iota(I32, (c, c), 0)
    si = lax.broadcasted_iota(I32, (c, c), 1)
    nt = (((1,), (1,)), ((), ()))
    w_all = w_ref[...]

    states = [state_ref[h] for h in range(B_HEADS)]
    for chunk, h in [(ci, hi) for ci in range(b_ref.shape[1] // c) for hi in range(B_HEADS)]:
        rows = slice(chunk * c, (chunk + 1) * c)
        ks = slice(h * kd, (h + 1) * kd)
        q = b_ref[0, rows, ks]
        f = b_ref[0, rows, B_KEY_WIDTH + h * kd:B_KEY_WIDTH + (h + 1) * kd]
        inp = b_ref[0, rows, 2 * B_KEY_WIDTH + h * kd:2 * B_KEY_WIDTH + (h + 1) * kd]
        gate = b_ref[0, rows, 3 * B_KEY_WIDTH + h * kd:3 * B_KEY_WIDTH + (h + 1) * kd]
        lb = lb_ref[:, ks]

        log_f = jnp.log(lb + (1.0 - lb) * jax.nn.sigmoid(f))
        key = (1.0 - lb) * jax.nn.sigmoid(-f)
        qs = q * jax.nn.sigmoid(q)

        g_hi = log_f.astype(BF16)
        g_lo = (log_f - g_hi.astype(F32)).astype(BF16)
        e2 = jnp.dot(w_all, jnp.concatenate([g_hi, g_lo], axis=1), preferred_element_type=F32)
        e = e2[:, :kd] + e2[:, kd:]

        cum = e[0:c]
        q_dec = (qs * jnp.exp(cum)).astype(BF16)
        k_dec = (key * jnp.exp(e[c:2 * c])).astype(BF16)
        inp_b = inp.astype(BF16)

        st = states[h]
        inter = lax.dot_general(q_dec, st.astype(BF16), nt, preferred_element_type=F32)

        scores = jnp.where(ti == si,
                           lax.dot_general(qs.astype(BF16), key.astype(BF16), nt, preferred_element_type=F32), 0.0)
        m = c
        for lvl in range(HG_LEVELS):
            el = jnp.exp(e[(2 + lvl) * c:(3 + lvl) * c])
            second = (trow % m) >= (m // 2)
            ql = jnp.where(second, qs * el, 0.0).astype(BF16)
            kl = jnp.where(second, 0.0, key * el).astype(BF16)
            sl = lax.dot_general(ql, kl, nt, preferred_element_type=F32)
            if m < c:
                sl = jnp.where((ti // m) == (si // m), sl, 0.0)
            scores = scores + sl
            m //= 2
        intra = jnp.dot(scores.astype(BF16), inp_b, preferred_element_type=F32)

        new_st = st * jnp.exp(cum[c - 1:c]) + lax.dot_general(
            inp_b, k_dec, (((0,), (0,)), ((), ())), preferred_element_type=F32)
        states[h] = new_st

        o = inter + intra
        ms = jnp.sum(o * o, axis=-1, keepdims=True) * (1.0 / B_VAL_DIM)
        o = o * lax.rsqrt(ms + 1e-6)
        o_ref[0, rows, ks] = o * ng_ref[:, ks] * (gate * jax.nn.sigmoid(gate))

    for h in range(B_HEADS):
        state_ref[h] = states[h]


def _hgrn(b_all, lower_bound, norm_g_pad, w_sum):
    nb, s, _ = b_all.shape
    c = HG_STEP_ROWS
    return pl.pallas_call(
        _hgrn_body,
        grid=(nb, s // c),
        in_specs=[pl.BlockSpec((1, c, B_COLS), lambda b, i: (b, i, 0)),
                  pl.BlockSpec((1, B_KEY_WIDTH), lambda b, i: (0, 0)),
                  pl.BlockSpec((1, B_PAD_WIDTH), lambda b, i: (0, 0)),
                  pl.BlockSpec(w_sum.shape, lambda b, i: (0, 0))],
        out_specs=pl.BlockSpec((1, c, B_PAD_WIDTH), lambda b, i: (b, i, 0)),
        out_shape=jax.ShapeDtypeStruct((nb, s, B_PAD_WIDTH), F32),
        scratch_shapes=[pltpu.VMEM((B_HEADS, B_VAL_PAD, B_KEY_DIM), F32)],
        compiler_params=_cparams("arbitrary", "arbitrary"),
        name="hgrn2",
    )(b_all, lower_bound, norm_g_pad, w_sum)


def _layer_norm_rows(z, g, b):
    mu = jnp.mean(z, axis=-1, keepdims=True)
    zc = z - mu
    var = jnp.mean(zc * zc, axis=-1, keepdims=True)
    return zc * lax.rsqrt(var + LN_EPS) * g + b


def _mix_out_body(o1, o2, o3, l1, l2, l3, ob, oc, x_ref, w_ref, g_ref, b_ref, out_ref):
    la, lb_, lc = l1[...], l2[...], l3[...]
    m = jnp.maximum(jnp.maximum(la, lb_), lc)
    wa, wb, wc = jnp.exp(la - m), jnp.exp(lb_ - m), jnp.exp(lc - m)
    oa = (wa * o1[...] + wb * o2[...] + wc * o3[...]) / (wa + wb + wc)
    cat = jnp.concatenate([oa, ob[...], oc[...]], axis=1).astype(BF16)
    y = jnp.dot(cat, w_ref[...], preferred_element_type=F32)
    z = DEEPNORM_ALPHA * x_ref[...] + y
    out_ref[...] = _layer_norm_rows(z, g_ref[...], b_ref[...])


def _mix_out(o_list, l_list, ob, oc, x2d, w_out_p, ln_g, ln_b):
    t, d = x2d.shape
    tm = 256
    row = lambda w: pl.BlockSpec((tm, w), lambda i: (i, 0))
    full = lambda a: pl.BlockSpec(a.shape, lambda i: (0, 0))
    return pl.pallas_call(
        _mix_out_body,
        grid=(t // tm,),
        in_specs=[row(A_WIDTH)] * 6 + [row(B_PAD_WIDTH), row(C_WIDTH), row(d), full(w_out_p), full(ln_g), full(ln_b)],
        out_specs=row(d),
        out_shape=jax.ShapeDtypeStruct((t, d), F32),
        compiler_params=_cparams("arbitrary"),
        name="mix_out_ln",
    )(*o_list, *l_list, ob, oc, x2d, w_out_p, ln_g, ln_b)


def _router_body(h_ref, rw_ref, bias_ref, tri_ref, e_ref, rank_ref, gate_ref, cnt_ref, hp_ref, carry_ref):
    tn = h_ref.shape[0]
    per_group = N_EXPERTS // N_GROUPS
    neg_inf = -jnp.inf

    @pl.when(pl.program_id(0) == 0)
    def _():
        carry_ref[...] = jnp.zeros_like(carry_ref)

    hp_ref[...] = _pack_bf16_pairs(h_ref[...])

    logits = lax.dot_general(rw_ref[...], h_ref[...], (((1,), (1,)), ((), ())),
                             precision=lax.Precision.HIGHEST, preferred_element_type=F32)
    scores = jax.nn.sigmoid(logits)
    choice = scores + bias_ref[...]

    def first_max(vals, idx, sentinel):
        top = jnp.max(vals, axis=0, keepdims=True)
        return top, jnp.min(jnp.where(vals == top, idx, sentinel), axis=0, keepdims=True)

    li = lax.broadcasted_iota(I32, (per_group, tn), 0).astype(F32)
    group_rows = []
    for g in range(N_GROUPS):
        cg = choice[g * per_group:(g + 1) * per_group]
        m1, first = first_max(cg, li, float(per_group))
        m2 = jnp.max(jnp.where(li == first, neg_inf, cg), axis=0, keepdims=True)
        group_rows.append(m1 + m2)
    group_score = jnp.concatenate(group_rows, axis=0)

    gi = lax.broadcasted_iota(I32, (N_GROUPS, tn), 0).astype(F32)
    group_ok = jnp.zeros((N_GROUPS, tn), F32)
    cur = group_score
    for _ in range(TOPK_GROUPS):
        _, first = first_max(cur, gi, float(N_GROUPS))
        pick = gi == first
        group_ok = jnp.where(pick, 1.0, group_ok)
        cur = jnp.where(pick, neg_inf, cur)

    cur = jnp.concatenate(
        [jnp.where(group_ok[g:g + 1] > 0.0, choice[g * per_group:(g + 1) * per_group], MASK_VALUE)
         for g in range(N_GROUPS)], axis=0)
    ei = lax.broadcasted_iota(I32, (N_EXPERTS, tn), 0).astype(F32)
    chosen = jnp.zeros((N_EXPERTS, tn), F32)
    picks, gates = [], []
    for _ in range(TOP_K):
        _, idx = first_max(cur, ei, float(N_EXPERTS))
        pick = ei == idx
        picks.append(idx)
        gates.append(jnp.sum(jnp.where(pick, scores, 0.0), axis=0, keepdims=True))
        chosen = jnp.where(pick, 1.0, chosen)
        cur = jnp.where(pick, neg_inf, cur)

    gate = jnp.concatenate(gates, axis=0)
    gate_ref[...] = gate / jnp.sum(gate, axis=0, keepdims=True) * ROUTED_SCALE
    e_ref[...] = jnp.concatenate(picks, axis=0).astype(I32)

    before = jnp.dot(chosen.astype(BF16), tri_ref[...], preferred_element_type=F32) + carry_ref[...]
    ranks = [jnp.sum(jnp.where(ei == idx, before, 0.0), axis=0, keepdims=True) for idx in picks]
    rank_ref[...] = jnp.concatenate(ranks, axis=0).astype(I32)
    carry = carry_ref[...] + jnp.sum(chosen, axis=1, keepdims=True)
    carry_ref[...] = carry
    cnt_ref[...] = carry.astype(I32)


def _router(h2d, rw_t, bias_col):
    t, d = h2d.shape
    tn = 256
    tri = jnp.asarray(np.triu(np.ones((tn, tn), np.float32), k=1), dtype=BF16)
    tok = lambda: pl.BlockSpec((TOP_K, tn), lambda i: (0, i))
    return pl.pallas_call(
        _router_body,
        grid=(t // tn,),
        in_specs=[pl.BlockSpec((tn, d), lambda i: (i, 0)),
                  pl.BlockSpec((N_EXPERTS, d), lambda i: (0, 0)),
                  pl.BlockSpec((N_EXPERTS, 1), lambda i: (0, 0)),
                  pl.BlockSpec((tn, tn), lambda i: (0, 0))],
        out_specs=[tok(), tok(), tok(), pl.BlockSpec((N_EXPERTS, 1), lambda i: (0, 0)),
                   pl.BlockSpec((tn, d // 2), lambda i: (i, 0))],
        out_shape=[jax.ShapeDtypeStruct((TOP_K, t), I32),
                   jax.ShapeDtypeStruct((TOP_K, t), I32),
                   jax.ShapeDtypeStruct((TOP_K, t), F32),
                   jax.ShapeDtypeStruct((N_EXPERTS, 1), I32),
                   jax.ShapeDtypeStruct((t, d // 2), U32)],
        scratch_shapes=[pltpu.VMEM((N_EXPERTS, 1), F32)],
        compiler_params=_cparams("arbitrary"),
        name="moe_router",
    )(h2d, rw_t, bias_col, tri)


def _pack_bf16_pairs(x):
    w = x.shape[1] // 2
    hi = lax.bitcast_convert_type(x[:, :w].astype(BF16).astype(F32), U32)
    lo = lax.bitcast_convert_type(x[:, w:].astype(BF16).astype(F32), U32)
    return hi | (lo >> 16)


def _unpack_bf16_pairs(p):
    hi = lax.bitcast_convert_type(p & jnp.uint32(0xFFFF0000), F32)
    lo = lax.bitcast_convert_type(p << 16, F32)
    return hi, lo


def _dest_body(e_ref, rank_ref, offs_ref, dest_ref):
    tn = e_ref.shape[1]
    ei = lax.broadcasted_iota(I32, (N_EXPERTS, tn), 0)
    offs = offs_ref[...]
    rows = [jnp.sum(jnp.where(ei == e_ref[k:k + 1, :], offs, 0.0), axis=0, keepdims=True) for k in range(TOP_K)]
    dest_ref[...] = jnp.concatenate(rows, axis=0).astype(I32) + rank_ref[...]


def _dest_rows(e_t, rank_t, offsets):
    t = e_t.shape[1]
    tn = 512
    tok = pl.BlockSpec((TOP_K, tn), lambda i: (0, i))
    return pl.pallas_call(
        _dest_body,
        grid=(t // tn,),
        in_specs=[tok, tok, pl.BlockSpec((N_EXPERTS, 1), lambda i: (0, 0))],
        out_specs=tok,
        out_shape=jax.ShapeDtypeStruct((TOP_K, t), I32),
        compiler_params=_cparams("arbitrary"),
        name="moe_dest",
    )(e_t, rank_t, offsets.astype(F32).reshape(N_EXPERTS, 1))


def _dispatch_body(pad_end_ref, dest_ref, h_ref, xs_hbm, packed, zero_buf, sem):
    step = pl.program_id(0)
    tt = h_ref.shape[0]
    bm = zero_buf.shape[0]

    def tail_copy(e):
        end = pl.multiple_of(pad_end_ref[e], bm)
        return pltpu.make_async_copy(zero_buf, xs_hbm.at[pl.ds(end - bm, bm)], sem)

    def has_rows(e):
        prev = jnp.where(e > 0, pad_end_ref[jnp.maximum(e - 1, 0)], 0)
        return pad_end_ref[e] > prev

    @pl.when(step == 0)
    def _():
        zero_buf[...] = jnp.zeros_like(zero_buf)

        def start(e, carry):
            @pl.when(has_rows(e))
            def _():
                tail_copy(e).start()
            return carry

        def wait(e, carry):
            @pl.when(has_rows(e))
            def _():
                tail_copy(e).wait()
            return carry

        lax.fori_loop(0, N_EXPERTS, start, 0)
        lax.fori_loop(0, N_EXPERTS, wait, 0)

    packed[...] = _pack_bf16_pairs(h_ref[...])

    def start_rows(t, carry):
        src = packed.at[pl.ds(t, 1)]
        for k in range(TOP_K):
            pltpu.make_async_copy(src, xs_hbm.at[pl.ds(dest_ref[t * TOP_K + k], 1)], sem).start()
        return carry

    lax.fori_loop(0, tt, start_rows, 0)
    for k in range(TOP_K):
        pltpu.make_async_copy(packed, xs_hbm.at[pl.ds(0, tt)], sem).wait()


def _dispatch(h2d, dest_flat, pad_end, n_rows):
    t, d = h2d.shape
    tt = 512
    grid_spec = pltpu.PrefetchScalarGridSpec(
        num_scalar_prefetch=1,
        grid=(t // tt,),
        in_specs=[pl.BlockSpec((tt * TOP_K,), lambda i, pe: (i,), memory_space=pltpu.SMEM),
                  pl.BlockSpec((tt, d), lambda i, pe: (i, 0))],
        out_specs=pl.BlockSpec(memory_space=pl.ANY),
        scratch_shapes=[pltpu.VMEM((tt, d // 2), U32), pltpu.VMEM((EXPERT_BLOCK, d // 2), U32),
                        pltpu.SemaphoreType.DMA(())],
    )
    return pl.pallas_call(
        _dispatch_body,
        grid_spec=grid_spec,
        out_shape=jax.ShapeDtypeStruct((n_rows, d // 2), U32),
        compiler_params=_cparams("arbitrary"),
        name="moe_dispatch",
    )(pad_end, dest_flat, h2d)


def _expert_body(offs_ref, nblk_ref, cnt_ref, layer_ref, xs_hbm, wg_ref, wu_ref, wd_ref, ys_hbm,
                 xbuf, ybuf, wg_b, wu_b, wd_b, sem_in, sem_out):
    e = pl.program_id(0)
    slots, bm = xbuf.shape[:2]
    ahead = slots - 1
    n = nblk_ref[e]
    first = offs_ref[e] // bm
    total = (offs_ref[N_EXPERTS - 1] // bm) + nblk_ref[N_EXPERTS - 1]

    def rows(g):
        return pl.ds(pl.multiple_of(g * bm, bm), bm)

    def x_copy(g, slot):
        return pltpu.make_async_copy(xs_hbm.at[rows(g)], xbuf.at[slot], sem_in.at[slot])

    def y_copy(g, slot):
        return pltpu.make_async_copy(ybuf.at[slot], ys_hbm.at[rows(g)], sem_out.at[slot])

    @pl.when(e == 0)
    def _():
        for g0 in range(ahead):
            @pl.when(g0 < total)
            def _():
                x_copy(g0, g0).start()

    wg_b[...] = wg_ref[0, 0].astype(BF16)
    wu_b[...] = wu_ref[0, 0].astype(BF16)
    wd_b[...] = wd_ref[0, 0].astype(BF16)

    def block(j, carry):
        g = first + j
        slot = g % slots
        x_copy(g, slot).wait()

        @pl.when(g + ahead < total)
        def _():
            x_copy(g + ahead, (g + ahead) % slots).start()

        @pl.when(g >= slots)
        def _():
            y_copy(g - slots, slot).wait()

        live = lax.broadcasted_iota(I32, (bm, 1), 0) < (cnt_ref[e] - j * bm)
        hi, lo = _unpack_bf16_pairs(jnp.where(live, xbuf[slot], jnp.uint32(0)))
        xb = jnp.concatenate([hi.astype(BF16), lo.astype(BF16)], axis=1)
        gate = jnp.dot(xb, wg_b[...], preferred_element_type=F32)
        up = jnp.dot(xb, wu_b[...], preferred_element_type=F32)
        hidden = (gate * jax.nn.sigmoid(gate) * up).astype(BF16)
        ybuf[slot] = _pack_bf16_pairs(jnp.dot(hidden, wd_b[...], preferred_element_type=F32))
        y_copy(g, slot).start()
        return carry

    lax.fori_loop(0, n, block, 0)

    @pl.when(e == N_EXPERTS - 1)
    def _():
        for back in range(1, slots + 1):
            @pl.when(total >= back)
            def _():
                y_copy(total - back, (total - back) % slots).wait()


def _experts(xs, offsets, n_blk, counts, layer, w_gate, w_up, w_down):
    n_rows, dp = xs.shape
    bm = EXPERT_BLOCK
    d, ff = w_gate.shape[-2:]
    w_spec = lambda a, b: pl.BlockSpec((1, 1, a, b), lambda e, of, nb, ct, ly: (ly[0], e, 0, 0))
    grid_spec = pltpu.PrefetchScalarGridSpec(
        num_scalar_prefetch=4,
        grid=(N_EXPERTS,),
        in_specs=[pl.BlockSpec(memory_space=pl.ANY), w_spec(d, ff), w_spec(d, ff), w_spec(ff, d)],
        out_specs=pl.BlockSpec(memory_space=pl.ANY),
        scratch_shapes=[pltpu.VMEM((EXPERT_SLOTS, bm, dp), U32), pltpu.VMEM((EXPERT_SLOTS, bm, dp), U32),
                        pltpu.VMEM((d, ff), BF16), pltpu.VMEM((d, ff), BF16), pltpu.VMEM((ff, d), BF16),
                        pltpu.SemaphoreType.DMA((EXPERT_SLOTS,)), pltpu.SemaphoreType.DMA((EXPERT_SLOTS,))],
    )
    return pl.pallas_call(
        _expert_body,
        grid_spec=grid_spec,
        out_shape=jax.ShapeDtypeStruct((n_rows, dp), U32),
        compiler_params=_cparams("arbitrary"),
        name="moe_experts",
    )(offsets, n_blk, counts, layer, xs, w_gate, w_up, w_down)


def _combine_body(dest_ref, dest_next_ref, gate_ref, h_ref, ys_hbm, sg_ref, su_ref, sd_ref, g_ref, b_ref, out_ref,
                  buf_a, buf_b, sem_a, sem_b):
    step = pl.program_id(0)
    tt = buf_a.shape[1]
    half = h_ref.shape[1] // 2

    def issue(dref, first, buf, sem):
        for t in range(tt):
            for k in range(TOP_K):
                row = dref[(first + t) * TOP_K + k]
                pltpu.make_async_copy(ys_hbm.at[pl.ds(row, 1)], buf.at[k, pl.ds(t, 1)], sem).start()

    def drain(buf, sem):
        for k in range(TOP_K):
            pltpu.make_async_copy(ys_hbm.at[pl.ds(0, tt)], buf.at[k], sem).wait()

    def finish(first, buf):
        rows = pl.ds(first, tt)
        h = h_ref[rows, :]
        hb = h.astype(BF16)
        sg = jnp.dot(hb, sg_ref[...], preferred_element_type=F32)
        su = jnp.dot(hb, su_ref[...], preferred_element_type=F32)
        shared = jnp.dot((sg * jax.nn.sigmoid(sg) * su).astype(BF16), sd_ref[...], preferred_element_type=F32)
        z = DEEPNORM_ALPHA * h + shared
        z_hi, z_lo = z[:, :half], z[:, half:]
        for k in range(TOP_K):
            y_hi, y_lo = _unpack_bf16_pairs(buf[k])
            gate = gate_ref[rows, k:k + 1]
            z_hi = z_hi + gate * y_hi
            z_lo = z_lo + gate * y_lo
        out_ref[rows, :] = _layer_norm_rows(jnp.concatenate([z_hi, z_lo], axis=1), g_ref[...], b_ref[...])

    @pl.when(step == 0)
    def _():
        def start_rows(t, carry):
            for k in range(TOP_K):
                pltpu.make_async_copy(ys_hbm.at[pl.ds(dest_ref[t * TOP_K + k], 1)], buf_a.at[k, pl.ds(t, 1)],
                                      sem_a).start()
            return carry
        lax.fori_loop(0, tt, start_rows, 0)

    drain(buf_a, sem_a)
    issue(dest_ref, tt, buf_b, sem_b)
    finish(0, buf_a)
    issue(dest_next_ref, 0, buf_a, sem_a)
    drain(buf_b, sem_b)
    finish(tt, buf_b)

    @pl.when(step == pl.num_programs(0) - 1)
    def _():
        drain(buf_a, sem_a)


def _combine(dest_flat, gate_tk, h2d, ys, sh_gate, sh_up, sh_down, ln_g, ln_b):
    t, d = h2d.shape
    tt = 128
    steps = t // (2 * tt)
    full = lambda a: pl.BlockSpec(a.shape, lambda i: (0, 0))
    return pl.pallas_call(
        _combine_body,
        grid=(steps,),
        in_specs=[pl.BlockSpec((2 * tt * TOP_K,), lambda i: (i,), memory_space=pltpu.SMEM),
                  pl.BlockSpec((2 * tt * TOP_K,), lambda i: (jnp.minimum(i + 1, steps - 1),), memory_space=pltpu.SMEM),
                  pl.BlockSpec((2 * tt, TOP_K), lambda i: (i, 0)),
                  pl.BlockSpec((2 * tt, d), lambda i: (i, 0)),
                  pl.BlockSpec(memory_space=pl.ANY),
                  full(sh_gate), full(sh_up), full(sh_down), full(ln_g), full(ln_b)],
        out_specs=pl.BlockSpec((2 * tt, d), lambda i: (i, 0)),
        out_shape=jax.ShapeDtypeStruct((t, d), F32),
        scratch_shapes=[pltpu.VMEM((TOP_K, tt, d // 2), U32), pltpu.VMEM((TOP_K, tt, d // 2), U32),
                        pltpu.SemaphoreType.DMA(()), pltpu.SemaphoreType.DMA(())],
        compiler_params=_cparams("arbitrary"),
        name="moe_combine_ln",
    )(dest_flat, dest_flat, gate_tk, h2d, ys, sh_gate, sh_up, sh_down, ln_g, ln_b)


SC_GATHER_ROWS = 64


def _sc_gather_rows(table, idx):
    info = plsc.get_sparse_core_info()
    nc, ns = info.num_cores, info.num_subcores
    workers = nc * ns
    n = idx.shape[0]
    w = table.shape[1]
    ch = SC_GATHER_ROWS
    per_worker = n // workers
    steps = per_worker // ch
    assert per_worker * workers == n and steps * ch == per_worker and steps % 2 == 0
    mesh = plsc.VectorSubcoreMesh(core_axis_name="c", subcore_axis_name="s")

    @functools.partial(
        pl.kernel, mesh=mesh,
        out_type=jax.ShapeDtypeStruct((n, w), table.dtype),
        scratch_types=[pltpu.VMEM((steps, ch), I32), pltpu.VMEM((2, ch, w), table.dtype),
                       pltpu.SemaphoreType.DMA((2,)), pltpu.SemaphoreType.DMA((2,))],
    )
    def gather_kernel(table_hbm, idx_hbm, out_hbm, idx_v, rows_v, gsem, wsem):
        wid = lax.axis_index("s") * nc + lax.axis_index("c")
        base = wid * per_worker
        pltpu.sync_copy(idx_hbm.at[wid], idx_v)

        def gather(i, slot):
            return pltpu.make_async_copy(table_hbm.at[idx_v.at[i]], rows_v.at[slot], gsem.at[slot])

        def write(i, slot):
            off = pl.multiple_of(base + i * ch, ch)
            return pltpu.make_async_copy(rows_v.at[slot], out_hbm.at[pl.ds(off, ch)], wsem.at[slot])

        gather(0, 0).start()

        @pl.loop(0, steps, step=2)
        def _(i):
            gather(i + 1, 1).start()
            gather(i, 0).wait()
            write(i, 0).start()
            write(i, 0).wait()

            @pl.when(i + 2 < steps)
            def _():
                gather(i + 2, 0).start()

            gather(i + 1, 1).wait()
            write(i + 1, 1).start()
            write(i + 1, 1).wait()

    return gather_kernel(table, idx.reshape(workers, steps, ch))


def _sc_scatter_rows(rows, idx, n_out):
    info = plsc.get_sparse_core_info()
    nc, ns = info.num_cores, info.num_subcores
    workers = nc * ns
    kk, t = idx.shape
    w = rows.shape[1]
    ch = SC_GATHER_ROWS
    per_worker = t // workers
    steps = per_worker // ch
    assert per_worker * workers == t and steps * ch == per_worker and steps % 2 == 0
    idx_w = idx.reshape(kk, workers, steps, ch).transpose(1, 0, 2, 3).reshape(workers, kk * steps, ch)
    mesh = plsc.VectorSubcoreMesh(core_axis_name="c", subcore_axis_name="s")

    @functools.partial(
        pl.kernel, mesh=mesh,
        out_type=jax.ShapeDtypeStruct((n_out, w), rows.dtype),
        scratch_types=[pltpu.VMEM((kk * steps, ch), I32), pltpu.VMEM((2, ch, w), rows.dtype),
                       pltpu.SemaphoreType.DMA((2,)), pltpu.SemaphoreType.DMA((2,))],
    )
    def scatter_kernel(rows_hbm, idx_hbm, out_hbm, idx_v, rows_v, rsem, ssem):
        wid = lax.axis_index("s") * nc + lax.axis_index("c")
        base = wid * per_worker
        pltpu.sync_copy(idx_hbm.at[wid], idx_v)

        def read(i, slot):
            off = pl.multiple_of(base + i * ch, ch)
            return pltpu.make_async_copy(rows_hbm.at[pl.ds(off, ch)], rows_v.at[slot], rsem.at[slot])

        def scatter(i, k, slot):
            return pltpu.make_async_copy(rows_v.at[slot], out_hbm.at[idx_v.at[k * steps + i]], ssem.at[slot])

        def scatter_all(i, slot):
            for k in range(kk):
                scatter(i, k, slot).start()
            for k in range(kk):
                scatter(i, k, slot).wait()

        read(0, 0).start()

        @pl.loop(0, steps, step=2)
        def _(i):
            read(i + 1, 1).start()
            read(i, 0).wait()
            scatter_all(i, 0)

            @pl.when(i + 2 < steps)
            def _():
                read(i + 2, 0).start()

            read(i + 1, 1).wait()
            scatter_all(i + 1, 1)

    return scatter_kernel(rows, idx_w)


def _combine_streamed_body(y_ref, gate_ref, h_ref, sg_ref, su_ref, sd_ref, g_ref, b_ref, out_ref):
    half = h_ref.shape[1] // 2
    h = h_ref[...]
    hb = h.astype(BF16)
    sg = jnp.dot(hb, sg_ref[...], preferred_element_type=F32)
    su = jnp.dot(hb, su_ref[...], preferred_element_type=F32)
    shared = jnp.dot((sg * jax.nn.sigmoid(sg) * su).astype(BF16), sd_ref[...], preferred_element_type=F32)
    z = DEEPNORM_ALPHA * h + shared
    z_hi, z_lo = z[:, :half], z[:, half:]
    for k in range(TOP_K):
        y_hi, y_lo = _unpack_bf16_pairs(y_ref[k])
        gate = gate_ref[:, k:k + 1]
        z_hi = z_hi + gate * y_hi
        z_lo = z_lo + gate * y_lo
    out_ref[...] = _layer_norm_rows(jnp.concatenate([z_hi, z_lo], axis=1), g_ref[...], b_ref[...])


def _combine_streamed(y_gathered, gate_tk, h2d, sh_gate, sh_up, sh_down, ln_g, ln_b):
    t, d = h2d.shape
    tt = 256
    full = lambda a: pl.BlockSpec(a.shape, lambda i: (0, 0))
    return pl.pallas_call(
        _combine_streamed_body,
        grid=(t // tt,),
        in_specs=[pl.BlockSpec((TOP_K, tt, d // 2), lambda i: (0, i, 0)),
                  pl.BlockSpec((tt, TOP_K), lambda i: (i, 0)),
                  pl.BlockSpec((tt, d), lambda i: (i, 0)),
                  full(sh_gate), full(sh_up), full(sh_down), full(ln_g), full(ln_b)],
        out_specs=pl.BlockSpec((tt, d), lambda i: (i, 0)),
        out_shape=jax.ShapeDtypeStruct((t, d), F32),
        compiler_params=_cparams("arbitrary"),
        name="moe_combine_streamed_ln",
    )(y_gathered, gate_tk, h2d, sh_gate, sh_up, sh_down, ln_g, ln_b)


def _mixer_sublayer(h2d, nb, s, w_in_p, w_out_p, a_biases, c_bias, lower_bound, norm_g_pad, sinks, w_sum, ln_g, ln_b):
    a_qkv, b_all, c_qkv = _in_proj(h2d, w_in_p)
    o_list, l_list = [], []
    for (window, r), bias in zip(A_PATTERNS, a_biases):
        src = a_qkv.reshape(nb, s // r, r * A_COLS)
        o, lse = _band_attn(src, bias, r=r, width=A_WIDTH, src_blocks=3, max_dist=window // r, want_lse=True)
        o_list.append(o.reshape(nb * s, A_WIDTH))
        l_list.append(lse.reshape(nb * s, A_WIDTH))
    oc = _band_attn(c_qkv.reshape(nb, s, C_COLS), c_bias, r=1, width=C_WIDTH, src_blocks=3,
                    max_dist=C_WINDOW - 1, sinks=sinks, want_lse=False).reshape(nb * s, C_WIDTH)
    ob = _hgrn(b_all.reshape(nb, s, B_COLS), lower_bound, norm_g_pad, w_sum).reshape(nb * s, B_PAD_WIDTH)
    return _mix_out(o_list, l_list, ob, oc, h2d, w_out_p, ln_g, ln_b)


def _moe_sublayer(h2d, layer, router_w, router_bias, w_gate, w_up, w_down, sh_gate, sh_up, sh_down, ln_g, ln_b):
    t, d = h2d.shape
    bm = EXPERT_BLOCK
    e_t, rank_t, gate_t, counts, h_packed = _router(h2d, router_w.T.astype(F32),
                                                    router_bias.astype(F32).reshape(N_EXPERTS, 1))

    counts = counts.reshape(N_EXPERTS)
    padded = (counts + bm - 1) // bm * bm
    pad_end = jnp.cumsum(padded).astype(I32)
    offsets = pad_end - padded
    n_blocks = -(-(t * TOP_K + N_EXPERTS * (bm - 1)) // bm)

    dest_t = _dest_rows(e_t, rank_t, offsets)
    xs = _sc_scatter_rows(h_packed, dest_t, n_blocks * bm)
    ys = _experts(xs, offsets, (padded // bm).astype(I32), counts, jnp.full((1,), layer, I32), w_gate, w_up, w_down)
    y_gathered = _sc_gather_rows(ys, dest_t.reshape(TOP_K * t)).reshape(TOP_K, t, d // 2)
    return _combine_streamed(y_gathered, gate_t.T, h2d,
                             sh_gate.astype(BF16), sh_up.astype(BF16), sh_down.astype(BF16), ln_g, ln_b)


def kernel(x, w_in, w_out, rel_bias_table, lower_bound_logits, hgrn_norm_g, attn_sinks, ln1_g, ln1_b, router_w, router_bias, expert_w_gate, expert_w_up, expert_w_down, shared_w_gate, shared_w_up, shared_w_down, ln2_g, ln2_b):
    nb, s, d = x.shape
    depth = w_in.shape[0]
    lb_probs = jax.nn.softmax(lower_bound_logits.astype(F32), axis=0)
    lower_bounds = jnp.cumsum(lb_probs, axis=0) - lb_probs[0]
    rel_table = rel_bias_table.astype(F32)
    a_biases = [_band_bias(rel_table, r, 0, A_HEADS) for _, r in A_PATTERNS]
    c_bias = _band_bias(rel_table, 1, A_HEADS, A_HEADS + C_HEADS)
    w_sum = jnp.asarray(_hgrn_sum_matrix(), dtype=BF16)
    row = lambda v: v.astype(F32).reshape(1, -1)

    h = x.astype(F32).reshape(nb * s, d)
    for l in range(depth):
        h = _mixer_sublayer(h, nb, s, _prep_w_in(w_in[l]), _prep_w_out(w_out[l].astype(F32)), a_biases, c_bias,
                            lower_bounds[l].reshape(1, B_KEY_WIDTH), _pad_heads_vec(hgrn_norm_g[l].astype(F32)),
                            attn_sinks[l].astype(F32), w_sum, row(ln1_g[l]), row(ln1_b[l]))
        h = _moe_sublayer(h, l, router_w[l], router_bias[l], expert_w_gate, expert_w_up, expert_w_down,
                          shared_w_gate[l], shared_w_up[l], shared_w_down[l], row(ln2_g[l]), row(ln2_b[l]))
    return h.reshape(nb, s, d).astype(x.dtype)
```

```python
import functools
import math

import numpy as np
import jax
import jax.numpy as jnp
from jax import lax
from jax.experimental import pallas as pl
from jax.experimental.pallas import tpu as pltpu
from jax.experimental.pallas import tpu_sc as plsc

F32 = jnp.float32
BF16 = jnp.bfloat16
I32 = jnp.int32
U32 = jnp.uint32

LANES = 128
SUBLANES = 8
VMEM_LIMIT = 56 * 1024 * 1024

D_MODEL = 1024
DEPTH = 2
HEAD_DIM = 64
BAND = 128
BAND_Q_TILE = 512
MASK_VALUE = -1e30

A_HEADS = 6
A_PATTERNS = ((128, 1), (512, 4), (2048, 16))
IN_PROJ_DILATIONS = tuple(r for _, r in A_PATTERNS if r > 1)
IN_PROJ_ROWS = 512
B_HEADS = 4
B_KEY_DIM = 128
B_VAL_DIM = 96
B_VAL_PAD = 128
HG_CHUNK = 64
HG_STEP_ROWS = 128
C_HEADS = 4
C_KV_HEADS = 2
C_WINDOW = 128

A_WIDTH = A_HEADS * HEAD_DIM
B_KEY_WIDTH = B_HEADS * B_KEY_DIM
B_WIDTH = B_HEADS * B_VAL_DIM
B_PAD_WIDTH = B_HEADS * B_VAL_PAD
C_WIDTH = C_HEADS * HEAD_DIM
C_KV_WIDTH = C_KV_HEADS * HEAD_DIM
IN_SPLITS = (A_WIDTH, A_WIDTH, A_WIDTH, B_KEY_WIDTH, B_KEY_WIDTH, B_WIDTH, B_WIDTH, C_WIDTH, C_KV_WIDTH, C_KV_WIDTH)

A_COLS = 3 * A_WIDTH
B_COLS = 4 * B_KEY_WIDTH
C_COLS = 3 * C_WIDTH
MIX_PAD_WIDTH = A_WIDTH + B_PAD_WIDTH + C_WIDTH

REL_BUCKETS = 32
REL_MAX_DIST = 2048

N_EXPERTS = 256
TOP_K = 8
N_GROUPS = 8
TOPK_GROUPS = 4
EXPERT_FF = 256
SHARED_FF = 256
ROUTED_SCALE = 2.5
EXPERT_BLOCK = 512
EXPERT_SLOTS = 3

DEEPNORM_ALPHA = (2 * DEPTH) ** 0.25
LN_EPS = 1e-5


def _cparams(*sem):
    return pltpu.CompilerParams(dimension_semantics=sem, vmem_limit_bytes=VMEM_LIMIT)


def _prep_w_in(w_in):
    d = w_in.shape[0]
    split_at = [int(i) for i in np.cumsum(IN_SPLITS)[:-1]]
    aq, ak, av, bq, bf, bi, bg, cq, ck, cv = jnp.split(w_in, split_at, axis=-1)
    pad_v = lambda w: jnp.pad(w.reshape(d, B_HEADS, B_VAL_DIM), ((0, 0), (0, 0), (0, B_VAL_PAD - B_VAL_DIM))).reshape(d, B_PAD_WIDTH)
    rep = lambda w: jnp.repeat(w.reshape(d, C_KV_HEADS, HEAD_DIM), C_HEADS // C_KV_HEADS, axis=1).reshape(d, C_WIDTH)
    cols = [aq, ak, av, bq, bf, pad_v(bi), pad_v(bg), cq, rep(ck), rep(cv)]
    return jnp.concatenate(cols, axis=-1).astype(BF16)


def _prep_w_out(w_out):
    d = w_out.shape[1]
    wa = w_out[:A_WIDTH]
    wb = w_out[A_WIDTH:A_WIDTH + B_WIDTH].reshape(B_HEADS, B_VAL_DIM, d)
    wb = jnp.pad(wb, ((0, 0), (0, B_VAL_PAD - B_VAL_DIM), (0, 0))).reshape(B_PAD_WIDTH, d)
    wc = w_out[A_WIDTH + B_WIDTH:]
    return jnp.concatenate([wa, wb, wc], axis=0).astype(BF16)


def _pad_heads_vec(v):
    return jnp.pad(v.reshape(B_HEADS, B_VAL_DIM), ((0, 0), (0, B_VAL_PAD - B_VAL_DIM))).reshape(1, B_PAD_WIDTH)


def _rel_bucket(dist):
    max_exact = REL_BUCKETS // 2
    d = jnp.maximum(dist, 0)
    log_ratio = jnp.log(jnp.maximum(d, max_exact).astype(F32) / max_exact) / math.log(REL_MAX_DIST / max_exact)
    large = jnp.minimum(max_exact + (log_ratio * (REL_BUCKETS - max_exact)).astype(I32), REL_BUCKETS - 1)
    return jnp.where(d < max_exact, d, large)


def _band_bias(rel_table, r, head_lo, head_hi):
    dist = jnp.arange(BAND)[:, None] + BAND - jnp.arange(2 * BAND)[None, :]
    onehot = jax.nn.one_hot(_rel_bucket(dist * r), REL_BUCKETS, dtype=F32)
    return jnp.einsum("qkb,bh->hqk", onehot, rel_table[:, head_lo:head_hi], precision=lax.Precision.HIGHEST)


def _hgrn_sum_matrix():
    c = HG_CHUNK
    t = np.arange(c)[:, None]
    u = np.arange(c)[None, :]
    blocks = [(u <= t), (u > t)]
    m = c
    while m >= 2:
        mid = (t // m) * m + m // 2
        second = t >= mid
        blocks.append(np.where(second, (u >= mid) & (u <= t), (u > t) & (u < mid)))
        m //= 2
    return np.concatenate(blocks, axis=0).astype(np.float32)


HG_LEVELS = int(math.log2(HG_CHUNK))


def _in_proj_body(x_ref, w_ref, *rest):
    n_res = len(IN_PROJ_DILATIONS)
    a_ref, res_refs, (b_ref, c_ref, slabs) = rest[0], rest[1:1 + n_res], rest[1 + n_res:]
    tm = x_ref.shape[0]
    xb = x_ref[...].astype(BF16)
    a = jnp.dot(xb, w_ref[:, :A_COLS], preferred_element_type=F32)
    a_ref[...] = a.astype(BF16)
    for j in range(A_COLS // LANES):
        slabs[j] = a[:, j * LANES:(j + 1) * LANES]
    for r, ref in zip(IN_PROJ_DILATIONS, res_refs):
        for p in range(r):
            for j in range(A_COLS // LANES):
                ref[0, p, :, j * LANES:(j + 1) * LANES] = slabs[j, pl.ds(p, tm // r, stride=r), :].astype(BF16)
    for j in range(B_COLS // B_KEY_WIDTH):
        lo = A_COLS + j * B_KEY_WIDTH
        b_ref[:, j * B_KEY_WIDTH:(j + 1) * B_KEY_WIDTH] = jnp.dot(
            xb, w_ref[:, lo:lo + B_KEY_WIDTH], preferred_element_type=F32)
    c_ref[...] = jnp.dot(xb, w_ref[:, A_COLS + B_COLS:], preferred_element_type=F32).astype(BF16)


def _in_proj(x2d, w_p, nb, s):
    t, d = x2d.shape
    tm = IN_PROJ_ROWS
    n = w_p.shape[1]
    tiles = s // tm
    res_specs = [pl.BlockSpec((1, r, tm // r, A_COLS), lambda i: (i // tiles, 0, i % tiles, 0))
                 for r in IN_PROJ_DILATIONS]
    res_shapes = [jax.ShapeDtypeStruct((nb, r, s // r, A_COLS), BF16) for r in IN_PROJ_DILATIONS]
    outs = pl.pallas_call(
        _in_proj_body,
        grid=(t // tm,),
        in_specs=[pl.BlockSpec((tm, d), lambda i: (i, 0)),
                  pl.BlockSpec((d, n), lambda i: (0, 0))],
        out_specs=[pl.BlockSpec((tm, A_COLS), lambda i: (i, 0))] + res_specs + [
                   pl.BlockSpec((tm, B_COLS), lambda i: (i, 0)),
                   pl.BlockSpec((tm, C_COLS), lambda i: (i, 0))],
        out_shape=[jax.ShapeDtypeStruct((t, A_COLS), BF16)] + res_shapes + [
                   jax.ShapeDtypeStruct((t, B_COLS), F32),
                   jax.ShapeDtypeStruct((t, C_COLS), BF16)],
        scratch_shapes=[pltpu.VMEM((A_COLS // LANES, tm, LANES), F32)],
        compiler_params=_cparams("arbitrary"),
        name="in_proj",
    )(x2d, w_p)
    return outs[0], list(outs[1:-2]), outs[-2], outs[-1]


def _band_attn_body(*refs, width, max_dist, has_sink, want_lse):
    q_ref, kp_ref, kc_ref, vp_ref, vc_ref, bias_ref = refs[:6]
    rest = refs[6:]
    if has_sink:
        sink_ref, rest = rest[0], rest[1:]
    o_ref = rest[0]
    lse_ref = rest[1] if want_lse else None

    first_tile = pl.program_id(2) == 0
    row = lax.broadcasted_iota(I32, (BAND, 2 * BAND), 0)
    col = lax.broadcasted_iota(I32, (BAND, 2 * BAND), 1)
    dist = row + BAND - col
    in_band = (dist >= 0) & (dist <= max_dist)
    first_mask = in_band & ((col >= BAND) | jnp.logical_not(first_tile))
    low_half = lax.broadcasted_iota(I32, (BAND, LANES), 1) < HEAD_DIM
    scale = HEAD_DIM ** -0.5

    for qb in range(q_ref.shape[0] // BAND):
        rows = slice(qb * BAND, (qb + 1) * BAND)
        mask = first_mask if qb == 0 else in_band
        for tile in range(width // LANES):
            sl = slice(tile * LANES, (tile + 1) * LANES)
            q2 = q_ref[rows, sl]
            if qb == 0:
                k2 = jnp.concatenate([kp_ref[:, sl], kc_ref[:BAND, sl]], axis=0)
                v2 = jnp.concatenate([vp_ref[:, sl], vc_ref[:BAND, sl]], axis=0)
            else:
                k2 = kc_ref[(qb - 1) * BAND:(qb + 1) * BAND, sl]
                v2 = vc_ref[(qb - 1) * BAND:(qb + 1) * BAND, sl]
            outs, lses = [], []
            for half in range(2):
                h = 2 * tile + half
                qm = jnp.where(low_half if half == 0 else jnp.logical_not(low_half), q2, jnp.zeros_like(q2))
                s = lax.dot_general(qm, k2, (((1,), (1,)), ((), ())), preferred_element_type=F32)
                s = s * scale + bias_ref[h]
                s = jnp.where(mask, s, MASK_VALUE)
                m = jnp.max(s, axis=-1, keepdims=True)
                if has_sink:
                    sink = sink_ref[h]
                    m = jnp.maximum(m, sink)
                p = jnp.exp(s - m)
                den = jnp.sum(p, axis=-1, keepdims=True)
                if has_sink:
                    den = den + jnp.exp(sink - m)
                pv = jnp.dot(p.astype(BF16), v2, preferred_element_type=F32)
                outs.append(pv / den)
                if want_lse:
                    lses.append(jnp.broadcast_to(m + jnp.log(den), (BAND, LANES)))
            o_ref[rows, sl] = jnp.where(low_half, outs[0], outs[1])
            if want_lse:
                lse_ref[rows, sl] = jnp.where(low_half, lses[0], lses[1])


def _band_attn(src, bias, *, width, max_dist, sinks=None, want_lse):
    nb, r, length, _ = src.shape
    qt = min(BAND_Q_TILE, length)
    bands = qt // BAND
    heads = width // HEAD_DIM
    has_sink = sinks is not None

    def cur(off):
        return pl.BlockSpec((None, None, qt, width), lambda b, p, i: (b, p, i, off))

    def prev(off):
        return pl.BlockSpec((None, None, BAND, width), lambda b, p, i: (b, p, jnp.maximum(i * bands - 1, 0), off))

    in_specs = [cur(0), prev(1), cur(1), prev(2), cur(2),
                pl.BlockSpec((heads, BAND, 2 * BAND), lambda b, p, i: (0, 0, 0))]
    args = [src, src, src, src, src, bias]
    if has_sink:
        in_specs.append(pl.BlockSpec(memory_space=pltpu.SMEM))
        args.append(sinks)
    out_spec = pl.BlockSpec((None, qt, width), lambda b, p, i: (b, i, p))
    out_sds = jax.ShapeDtypeStruct((nb, length, r * width), F32)
    body = functools.partial(_band_attn_body, width=width, max_dist=max_dist, has_sink=has_sink, want_lse=want_lse)
    return pl.pallas_call(
        body,
        grid=(nb, r, length // qt),
        in_specs=in_specs,
        out_specs=[out_spec, out_spec] if want_lse else out_spec,
        out_shape=[out_sds, out_sds] if want_lse else out_sds,
        compiler_params=_cparams("arbitrary", "arbitrary", "arbitrary"),
        name="band_attn_r%d_w%d" % (r, width),
    )(*args)


def _hgrn_body(b_ref, lb_ref, ng_ref, w_ref, o_ref, state_ref):
    c = HG_CHUNK
    kd = B_KEY_DIM

    @pl.when(pl.program_id(1) == 0)
    def _():
        state_ref[...] = jnp.zeros_like(state_ref)

    trow = lax.broadcasted_iota(I32, (c, 1), 0)
    ti = lax.broadcasted_iota(I32, (c, c), 0)
    si = lax.broadcasted_iota(I32, (c, c), 1)
    nt = (((1,), (1,)), ((), ()))
    w_all = w_ref[...]

    states = [state_ref[h] for h in range(B_HEADS)]
    for chunk, h in [(ci, hi) for ci in range(b_ref.shape[1] // c) for hi in range(B_HEADS)]:
        rows = slice(chunk * c, (chunk + 1) * c)
        ks = slice(h * kd, (h + 1) * kd)
        q = b_ref[0, rows, ks]
        f = b_ref[0, rows, B_KEY_WIDTH + h * kd:B_KEY_WIDTH + (h + 1) * kd]
        inp = b_ref[0, rows, 2 * B_KEY_WIDTH + h * kd:2 * B_KEY_WIDTH + (h + 1) * kd]
        gate = b_ref[0, rows, 3 * B_KEY_WIDTH + h * kd:3 * B_KEY_WIDTH + (h + 1) * kd]
        lb = lb_ref[:, ks]

        log_f = jnp.log(lb + (1.0 - lb) * jax.nn.sigmoid(f))
        key = (1.0 - lb) * jax.nn.sigmoid(-f)
        qs = q * jax.nn.sigmoid(q)

        g_hi = log_f.astype(BF16)
        g_lo = (log_f - g_hi.astype(F32)).astype(BF16)
        e2 = jnp.dot(w_all, jnp.concatenate([g_hi, g_lo], axis=1), preferred_element_type=F32)
        e = e2[:, :kd] + e2[:, kd:]

        cum = e[0:c]
        q_dec = (qs * jnp.exp(cum)).astype(BF16)
        k_dec = (key * jnp.exp(e[c:2 * c])).astype(BF16)
        inp_b = inp.astype(BF16)

        st = states[h]
        inter = lax.dot_general(q_dec, st.astype(BF16), nt, preferred_element_type=F32)

        scores = jnp.where(ti == si,
                           lax.dot_general(qs.astype(BF16), key.astype(BF16), nt, preferred_element_type=F32), 0.0)
        m = c
        for lvl in range(HG_LEVELS):
            el = jnp.exp(e[(2 + lvl) * c:(3 + lvl) * c])
            second = (trow % m) >= (m // 2)
            ql = jnp.where(second, qs * el, 0.0).astype(BF16)
            kl = jnp.where(second, 0.0, key * el).astype(BF16)
            sl = lax.dot_general(ql, kl, nt, preferred_element_type=F32)
            if m < c:
                sl = jnp.where((ti // m) == (si // m), sl, 0.0)
            scores = scores + sl
            m //= 2
        intra = jnp.dot(scores.astype(BF16), inp_b, preferred_element_type=F32)

        new_st = st * jnp.exp(cum[c - 1:c]) + lax.dot_general(
            inp_b, k_dec, (((0,), (0,)), ((), ())), preferred_element_type=F32)
        states[h] = new_st

        o = inter + intra
        ms = jnp.sum(o * o, axis=-1, keepdims=True) * (1.0 / B_VAL_DIM)
        o = o * lax.rsqrt(ms + 1e-6)
        o_ref[0, rows, ks] = o * ng_ref[:, ks] * (gate * jax.nn.sigmoid(gate))

    for h in range(B_HEADS):
        state_ref[h] = states[h]


def _hgrn(b_all, lower_bound, norm_g_pad, w_sum):
    nb, s, _ = b_all.shape
    c = HG_STEP_ROWS
    return pl.pallas_call(
        _hgrn_body,
        grid=(nb, s // c),
        in_specs=[pl.BlockSpec((1, c, B_COLS), lambda b, i: (b, i, 0)),
                  pl.BlockSpec((1, B_KEY_WIDTH), lambda b, i: (0, 0)),
                  pl.BlockSpec((1, B_PAD_WIDTH), lambda b, i: (0, 0)),
                  pl.BlockSpec(w_sum.shape, lambda b, i: (0, 0))],
        out_specs=pl.BlockSpec((1, c, B_PAD_WIDTH), lambda b, i: (b, i, 0)),
        out_shape=jax.ShapeDtypeStruct((nb, s, B_PAD_WIDTH), F32),
        scratch_shapes=[pltpu.VMEM((B_HEADS, B_VAL_PAD, B_KEY_DIM), F32)],
        compiler_params=_cparams("arbitrary", "arbitrary"),
        name="hgrn2",
    )(b_all, lower_bound, norm_g_pad, w_sum)


def _layer_norm_rows(z, g, b):
    mu = jnp.mean(z, axis=-1, keepdims=True)
    zc = z - mu
    var = jnp.mean(zc * zc, axis=-1, keepdims=True)
    return zc * lax.rsqrt(var + LN_EPS) * g + b


def _mix_out_body(o1, o2, o3, l1, l2, l3, ob, oc, x_ref, w_ref, g_ref, b_ref, out_ref):
    la, lb_, lc = l1[...], l2[...], l3[...]
    m = jnp.maximum(jnp.maximum(la, lb_), lc)
    wa, wb, wc = jnp.exp(la - m), jnp.exp(lb_ - m), jnp.exp(lc - m)
    oa = (wa * o1[...] + wb * o2[...] + wc * o3[...]) / (wa + wb + wc)
    cat = jnp.concatenate([oa, ob[...], oc[...]], axis=1).astype(BF16)
    y = jnp.dot(cat, w_ref[...], preferred_element_type=F32)
    z = DEEPNORM_ALPHA * x_ref[...] + y
    out_ref[...] = _layer_norm_rows(z, g_ref[...], b_ref[...])


def _mix_out(o_list, l_list, ob, oc, x2d, w_out_p, ln_g, ln_b):
    t, d = x2d.shape
    tm = 256
    row = lambda w: pl.BlockSpec((tm, w), lambda i: (i, 0))
    full = lambda a: pl.BlockSpec(a.shape, lambda i: (0, 0))
    return pl.pallas_call(
        _mix_out_body,
        grid=(t // tm,),
        in_specs=[row(A_WIDTH)] * 6 + [row(B_PAD_WIDTH), row(C_WIDTH), row(d), full(w_out_p), full(ln_g), full(ln_b)],
        out_specs=row(d),
        out_shape=jax.ShapeDtypeStruct((t, d), F32),
        compiler_params=_cparams("arbitrary"),
        name="mix_out_ln",
    )(*o_list, *l_list, ob, oc, x2d, w_out_p, ln_g, ln_b)


def _router_body(h_ref, rw_ref, bias_ref, tri_ref, e_ref, rank_ref, gate_ref, cnt_ref, hp_ref, carry_ref):
    tn = h_ref.shape[0]
    per_group = N_EXPERTS // N_GROUPS
    neg_inf = -jnp.inf

    @pl.when(pl.program_id(0) == 0)
    def _():
        carry_ref[...] = jnp.zeros_like(carry_ref)

    hp_ref[...] = _pack_bf16_pairs(h_ref[...])

    logits = lax.dot_general(rw_ref[...], h_ref[...], (((1,), (1,)), ((), ())),
                             precision=lax.Precision.HIGHEST, preferred_element_type=F32)
    scores = jax.nn.sigmoid(logits)
    choice = scores + bias_ref[...]

    def first_max(vals, idx, sentinel):
        top = jnp.max(vals, axis=0, keepdims=True)
        return top, jnp.min(jnp.where(vals == top, idx, sentinel), axis=0, keepdims=True)

    li = lax.broadcasted_iota(I32, (per_group, tn), 0).astype(F32)
    group_rows = []
    for g in range(N_GROUPS):
        cg = choice[g * per_group:(g + 1) * per_group]
        m1, first = first_max(cg, li, float(per_group))
        m2 = jnp.max(jnp.where(li == first, neg_inf, cg), axis=0, keepdims=True)
        group_rows.append(m1 + m2)
    group_score = jnp.concatenate(group_rows, axis=0)

    gi = lax.broadcasted_iota(I32, (N_GROUPS, tn), 0).astype(F32)
    group_ok = jnp.zeros((N_GROUPS, tn), F32)
    cur = group_score
    for _ in range(TOPK_GROUPS):
        _, first = first_max(cur, gi, float(N_GROUPS))
        pick = gi == first
        group_ok = jnp.where(pick, 1.0, group_ok)
        cur = jnp.where(pick, neg_inf, cur)

    cur = jnp.concatenate(
        [jnp.where(group_ok[g:g + 1] > 0.0, choice[g * per_group:(g + 1) * per_group], MASK_VALUE)
         for g in range(N_GROUPS)], axis=0)
    ei = lax.broadcasted_iota(I32, (N_EXPERTS, tn), 0).astype(F32)
    chosen = jnp.zeros((N_EXPERTS, tn), F32)
    picks, gates = [], []
    for _ in range(TOP_K):
        _, idx = first_max(cur, ei, float(N_EXPERTS))
        pick = ei == idx
        picks.append(idx)
        gates.append(jnp.sum(jnp.where(pick, scores, 0.0), axis=0, keepdims=True))
        chosen = jnp.where(pick, 1.0, chosen)
        cur = jnp.where(pick, neg_inf, cur)

    gate = jnp.concatenate(gates, axis=0)
    gate_ref[...] = gate / jnp.sum(gate, axis=0, keepdims=True) * ROUTED_SCALE
    e_ref[...] = jnp.concatenate(picks, axis=0).astype(I32)

    before = jnp.dot(chosen.astype(BF16), tri_ref[...], preferred_element_type=F32) + carry_ref[...]
    ranks = [jnp.sum(jnp.where(ei == idx, before, 0.0), axis=0, keepdims=True) for idx in picks]
    rank_ref[...] = jnp.concatenate(ranks, axis=0).astype(I32)
    carry = carry_ref[...] + jnp.sum(chosen, axis=1, keepdims=True)
    carry_ref[...] = carry
    cnt_ref[...] = carry.astype(I32)


def _router(h2d, rw_t, bias_col):
    t, d = h2d.shape
    tn = 256
    tri = jnp.asarray(np.triu(np.ones((tn, tn), np.float32), k=1), dtype=BF16)
    tok = lambda: pl.BlockSpec((TOP_K, tn), lambda i: (0, i))
    return pl.pallas_call(
        _router_body,
        grid=(t // tn,),
        in_specs=[pl.BlockSpec((tn, d), lambda i: (i, 0)),
                  pl.BlockSpec((N_EXPERTS, d), lambda i: (0, 0)),
                  pl.BlockSpec((N_EXPERTS, 1), lambda i: (0, 0)),
                  pl.BlockSpec((tn, tn), lambda i: (0, 0))],
        out_specs=[tok(), tok(), tok(), pl.BlockSpec((N_EXPERTS, 1), lambda i: (0, 0)),
                   pl.BlockSpec((tn, d // 2), lambda i: (i, 0))],
        out_shape=[jax.ShapeDtypeStruct((TOP_K, t), I32),
                   jax.ShapeDtypeStruct((TOP_K, t), I32),
                   jax.ShapeDtypeStruct((TOP_K, t), F32),
                   jax.ShapeDtypeStruct((N_EXPERTS, 1), I32),
                   jax.ShapeDtypeStruct((t, d // 2), U32)],
        scratch_shapes=[pltpu.VMEM((N_EXPERTS, 1), F32)],
        compiler_params=_cparams("arbitrary"),
        name="moe_router",
    )(h2d, rw_t, bias_col, tri)


def _pack_bf16_pairs(x):
    w = x.shape[1] // 2
    hi = lax.bitcast_convert_type(x[:, :w].astype(BF16).astype(F32), U32)
    lo = lax.bitcast_convert_type(x[:, w:].astype(BF16).astype(F32), U32)
    return hi | (lo >> 16)


def _unpack_bf16_pairs(p):
    hi = lax.bitcast_convert_type(p & jnp.uint32(0xFFFF0000), F32)
    lo = lax.bitcast_convert_type(p << 16, F32)
    return hi, lo


def _dest_body(e_ref, rank_ref, offs_ref, dest_ref):
    tn = e_ref.shape[1]
    ei = lax.broadcasted_iota(I32, (N_EXPERTS, tn), 0)
    offs = offs_ref[...]
    rows = [jnp.sum(jnp.where(ei == e_ref[k:k + 1, :], offs, 0.0), axis=0, keepdims=True) for k in range(TOP_K)]
    dest_ref[...] = jnp.concatenate(rows, axis=0).astype(I32) + rank_ref[...]


def _dest_rows(e_t, rank_t, offsets):
    t = e_t.shape[1]
    tn = 512
    tok = pl.BlockSpec((TOP_K, tn), lambda i: (0, i))
    return pl.pallas_call(
        _dest_body,
        grid=(t // tn,),
        in_specs=[tok, tok, pl.BlockSpec((N_EXPERTS, 1), lambda i: (0, 0))],
        out_specs=tok,
        out_shape=jax.ShapeDtypeStruct((TOP_K, t), I32),
        compiler_params=_cparams("arbitrary"),
        name="moe_dest",
    )(e_t, rank_t, offsets.astype(F32).reshape(N_EXPERTS, 1))


def _expert_body(offs_ref, nblk_ref, cnt_ref, layer_ref, xs_hbm, wg_ref, wu_ref, wd_ref, ys_hbm,
                 xbuf, ybuf, wg_b, wu_b, wd_b, sem_in, sem_out):
    e = pl.program_id(0)
    slots, bm = xbuf.shape[:2]
    ahead = slots - 1
    n = nblk_ref[e]
    first = offs_ref[e] // bm
    total = (offs_ref[N_EXPERTS - 1] // bm) + nblk_ref[N_EXPERTS - 1]

    def rows(g):
        return pl.ds(pl.multiple_of(g * bm, bm), bm)

    def x_copy(g, slot):
        return pltpu.make_async_copy(xs_hbm.at[rows(g)], xbuf.at[slot], sem_in.at[slot])

    def y_copy(g, slot):
        return pltpu.make_async_copy(ybuf.at[slot], ys_hbm.at[rows(g)], sem_out.at[slot])

    @pl.when(e == 0)
    def _():
        for g0 in range(ahead):
            @pl.when(g0 < total)
            def _():
                x_copy(g0, g0).start()

    wg_b[...] = wg_ref[0, 0].astype(BF16)
    wu_b[...] = wu_ref[0, 0].astype(BF16)
    wd_b[...] = wd_ref[0, 0].astype(BF16)

    def block(j, carry):
        g = first + j
        slot = g % slots
        x_copy(g, slot).wait()

        @pl.when(g + ahead < total)
        def _():
            x_copy(g + ahead, (g + ahead) % slots).start()

        @pl.when(g >= slots)
        def _():
            y_copy(g - slots, slot).wait()

        live = lax.broadcasted_iota(I32, (bm, 1), 0) < (cnt_ref[e] - j * bm)
        hi, lo = _unpack_bf16_pairs(jnp.where(live, xbuf[slot], jnp.uint32(0)))
        xb = jnp.concatenate([hi.astype(BF16), lo.astype(BF16)], axis=1)
        gate = jnp.dot(xb, wg_b[...], preferred_element_type=F32)
        up = jnp.dot(xb, wu_b[...], preferred_element_type=F32)
        hidden = (gate * jax.nn.sigmoid(gate) * up).astype(BF16)
        ybuf[slot] = _pack_bf16_pairs(jnp.dot(hidden, wd_b[...], preferred_element_type=F32))
        y_copy(g, slot).start()
        return carry

    lax.fori_loop(0, n, block, 0)

    @pl.when(e == N_EXPERTS - 1)
    def _():
        for back in range(1, slots + 1):
            @pl.when(total >= back)
            def _():
                y_copy(total - back, (total - back) % slots).wait()


def _experts(xs, offsets, n_blk, counts, layer, w_gate, w_up, w_down):
    n_rows, dp = xs.shape
    bm = EXPERT_BLOCK
    d, ff = w_gate.shape[-2:]
    w_spec = lambda a, b: pl.BlockSpec((1, 1, a, b), lambda e, of, nb, ct, ly: (ly[0], e, 0, 0))
    grid_spec = pltpu.PrefetchScalarGridSpec(
        num_scalar_prefetch=4,
        grid=(N_EXPERTS,),
        in_specs=[pl.BlockSpec(memory_space=pl.ANY), w_spec(d, ff), w_spec(d, ff), w_spec(ff, d)],
        out_specs=pl.BlockSpec(memory_space=pl.ANY),
        scratch_shapes=[pltpu.VMEM((EXPERT_SLOTS, bm, dp), U32), pltpu.VMEM((EXPERT_SLOTS, bm, dp), U32),
                        pltpu.VMEM((d, ff), BF16), pltpu.VMEM((d, ff), BF16), pltpu.VMEM((ff, d), BF16),
                        pltpu.SemaphoreType.DMA((EXPERT_SLOTS,)), pltpu.SemaphoreType.DMA((EXPERT_SLOTS,))],
    )
    return pl.pallas_call(
        _expert_body,
        grid_spec=grid_spec,
        out_shape=jax.ShapeDtypeStruct((n_rows, dp), U32),
        compiler_params=_cparams("arbitrary"),
        name="moe_experts",
    )(offsets, n_blk, counts, layer, xs, w_gate, w_up, w_down)


SC_GATHER_ROWS = 64


def _sc_gather_rows(table, idx):
    info = plsc.get_sparse_core_info()
    nc, ns = info.num_cores, info.num_subcores
    workers = nc * ns
    n = idx.shape[0]
    w = table.shape[1]
    ch = SC_GATHER_ROWS
    per_worker = n // workers
    steps = per_worker // ch
    assert per_worker * workers == n and steps * ch == per_worker and steps % 2 == 0
    mesh = plsc.VectorSubcoreMesh(core_axis_name="c", subcore_axis_name="s")

    @functools.partial(
        pl.kernel, mesh=mesh,
        out_type=jax.ShapeDtypeStruct((n, w), table.dtype),
        scratch_types=[pltpu.VMEM((steps, ch), I32), pltpu.VMEM((2, ch, w), table.dtype),
                       pltpu.SemaphoreType.DMA((2,)), pltpu.SemaphoreType.DMA((2,))],
    )
    def gather_kernel(table_hbm, idx_hbm, out_hbm, idx_v, rows_v, gsem, wsem):
        wid = lax.axis_index("s") * nc + lax.axis_index("c")
        base = wid * per_worker
        pltpu.sync_copy(idx_hbm.at[wid], idx_v)

        def gather(i, slot):
            return pltpu.make_async_copy(table_hbm.at[idx_v.at[i]], rows_v.at[slot], gsem.at[slot])

        def write(i, slot):
            off = pl.multiple_of(base + i * ch, ch)
            return pltpu.make_async_copy(rows_v.at[slot], out_hbm.at[pl.ds(off, ch)], wsem.at[slot])

        gather(0, 0).start()

        @pl.loop(0, steps, step=2)
        def _(i):
            gather(i + 1, 1).start()
            gather(i, 0).wait()
            write(i, 0).start()
            write(i, 0).wait()

            @pl.when(i + 2 < steps)
            def _():
                gather(i + 2, 0).start()

            gather(i + 1, 1).wait()
            write(i + 1, 1).start()
            write(i + 1, 1).wait()

    return gather_kernel(table, idx.reshape(workers, steps, ch))


def _sc_scatter_rows(rows, idx, n_out):
    info = plsc.get_sparse_core_info()
    nc, ns = info.num_cores, info.num_subcores
    workers = nc * ns
    kk, t = idx.shape
    w = rows.shape[1]
    ch = SC_GATHER_ROWS
    per_worker = t // workers
    steps = per_worker // ch
    assert per_worker * workers == t and steps * ch == per_worker and steps % 2 == 0
    idx_w = idx.reshape(kk, workers, steps, ch).transpose(1, 0, 2, 3).reshape(workers, kk * steps, ch)
    mesh = plsc.VectorSubcoreMesh(core_axis_name="c", subcore_axis_name="s")

    @functools.partial(
        pl.kernel, mesh=mesh,
        out_type=jax.ShapeDtypeStruct((n_out, w), rows.dtype),
        scratch_types=[pltpu.VMEM((kk * steps, ch), I32), pltpu.VMEM((2, ch, w), rows.dtype),
                       pltpu.SemaphoreType.DMA((2,)), pltpu.SemaphoreType.DMA((2,))],
    )
    def scatter_kernel(rows_hbm, idx_hbm, out_hbm, idx_v, rows_v, rsem, ssem):
        wid = lax.axis_index("s") * nc + lax.axis_index("c")
        base = wid * per_worker
        pltpu.sync_copy(idx_hbm.at[wid], idx_v)

        def read(i, slot):
            off = pl.multiple_of(base + i * ch, ch)
            return pltpu.make_async_copy(rows_hbm.at[pl.ds(off, ch)], rows_v.at[slot], rsem.at[slot])

        def scatter(i, k, slot):
            return pltpu.make_async_copy(rows_v.at[slot], out_hbm.at[idx_v.at[k * steps + i]], ssem.at[slot])

        def scatter_all(i, slot):
            for k in range(kk):
                scatter(i, k, slot).start()
            for k in range(kk):
                scatter(i, k, slot).wait()

        read(0, 0).start()

        @pl.loop(0, steps, step=2)
        def _(i):
            read(i + 1, 1).start()
            read(i, 0).wait()
            scatter_all(i, 0)

            @pl.when(i + 2 < steps)
            def _():
                read(i + 2, 0).start()

            read(i + 1, 1).wait()
            scatter_all(i + 1, 1)

    return scatter_kernel(rows, idx_w)


def _combine_streamed_body(y_ref, gate_ref, h_ref, sg_ref, su_ref, sd_ref, g_ref, b_ref, out_ref):
    half = h_ref.shape[1] // 2
    h = h_ref[...]
    hb = h.astype(BF16)
    sg = jnp.dot(hb, sg_ref[...], preferred_element_type=F32)
    su = jnp.dot(hb, su_ref[...], preferred_element_type=F32)
    shared = jnp.dot((sg * jax.nn.sigmoid(sg) * su).astype(BF16), sd_ref[...], preferred_element_type=F32)
    z = DEEPNORM_ALPHA * h + shared
    z_hi, z_lo = z[:, :half], z[:, half:]
    for k in range(TOP_K):
        y_hi, y_lo = _unpack_bf16_pairs(y_ref[k])
        gate = gate_ref[:, k:k + 1]
        z_hi = z_hi + gate * y_hi
        z_lo = z_lo + gate * y_lo
    out_ref[...] = _layer_norm_rows(jnp.concatenate([z_hi, z_lo], axis=1), g_ref[...], b_ref[...])


def _combine_streamed(y_gathered, gate_tk, h2d, sh_gate, sh_up, sh_down, ln_g, ln_b):
    t, d = h2d.shape
    tt = 256
    full = lambda a: pl.BlockSpec(a.shape, lambda i: (0, 0))
    return pl.pallas_call(
        _combine_streamed_body,
        grid=(t // tt,),
        in_specs=[pl.BlockSpec((TOP_K, tt, d // 2), lambda i: (0, i, 0)),
                  pl.BlockSpec((tt, TOP_K), lambda i: (i, 0)),
                  pl.BlockSpec((tt, d), lambda i: (i, 0)),
                  full(sh_gate), full(sh_up), full(sh_down), full(ln_g), full(ln_b)],
        out_specs=pl.BlockSpec((tt, d), lambda i: (i, 0)),
        out_shape=jax.ShapeDtypeStruct((t, d), F32),
        compiler_params=_cparams("arbitrary"),
        name="moe_combine_streamed_ln",
    )(y_gathered, gate_tk, h2d, sh_gate, sh_up, sh_down, ln_g, ln_b)


def _mixer_sublayer(h2d, nb, s, w_in_p, w_out_p, a_biases, c_bias, lower_bound, norm_g_pad, sinks, w_sum, ln_g, ln_b):
    a_qkv, a_residue, b_all, c_qkv = _in_proj(h2d, w_in_p, nb, s)
    a_by_dilation = {1: a_qkv.reshape(nb, 1, s, A_COLS), **dict(zip(IN_PROJ_DILATIONS, a_residue))}
    o_list, l_list = [], []
    for (window, r), bias in zip(A_PATTERNS, a_biases):
        o, lse = _band_attn(a_by_dilation[r], bias, width=A_WIDTH, max_dist=window // r, want_lse=True)
        o_list.append(o.reshape(nb * s, A_WIDTH))
        l_list.append(lse.reshape(nb * s, A_WIDTH))
    oc = _band_attn(c_qkv.reshape(nb, 1, s, C_COLS), c_bias, width=C_WIDTH,
                    max_dist=C_WINDOW - 1, sinks=sinks, want_lse=False).reshape(nb * s, C_WIDTH)
    ob = _hgrn(b_all.reshape(nb, s, B_COLS), lower_bound, norm_g_pad, w_sum).reshape(nb * s, B_PAD_WIDTH)
    return _mix_out(o_list, l_list, ob, oc, h2d, w_out_p, ln_g, ln_b)


def _moe_sublayer(h2d, layer, router_w, router_bias, w_gate, w_up, w_down, sh_gate, sh_up, sh_down, ln_g, ln_b):
    t, d = h2d.shape
    bm = EXPERT_BLOCK
    e_t, rank_t, gate_t, counts, h_packed = _router(h2d, router_w.T.astype(F32),
                                                    router_bias.astype(F32).reshape(N_EXPERTS, 1))

    counts = counts.reshape(N_EXPERTS)
    padded = (counts + bm - 1) // bm * bm
    pad_end = jnp.cumsum(padded).astype(I32)
    offsets = pad_end - padded
    n_blocks = -(-(t * TOP_K + N_EXPERTS * (bm - 1)) // bm)

    dest_t = _dest_rows(e_t, rank_t, offsets)
    xs = _sc_scatter_rows(h_packed, dest_t, n_blocks * bm)
    ys = _experts(xs, offsets, (padded // bm).astype(I32), counts, jnp.full((1,), layer, I32), w_gate, w_up, w_down)
    y_gathered = _sc_gather_rows(ys, dest_t.reshape(TOP_K * t)).reshape(TOP_K, t, d // 2)
    return _combine_streamed(y_gathered, gate_t.T, h2d,
                             sh_gate.astype(BF16), sh_up.astype(BF16), sh_down.astype(BF16), ln_g, ln_b)


def kernel(x, w_in, w_out, rel_bias_table, lower_bound_logits, hgrn_norm_g, attn_sinks, ln1_g, ln1_b, router_w, router_bias, expert_w_gate, expert_w_up, expert_w_down, shared_w_gate, shared_w_up, shared_w_down, ln2_g, ln2_b):
    nb, s, d = x.shape
    depth = w_in.shape[0]
    lb_probs = jax.nn.softmax(lower_bound_logits.astype(F32), axis=0)
    lower_bounds = jnp.cumsum(lb_probs, axis=0) - lb_probs[0]
    rel_table = rel_bias_table.astype(F32)
    a_biases = [_band_bias(rel_table, r, 0, A_HEADS) for _, r in A_PATTERNS]
    c_bias = _band_bias(rel_table, 1, A_HEADS, A_HEADS + C_HEADS)
    w_sum = jnp.asarray(_hgrn_sum_matrix(), dtype=BF16)
    row = lambda v: v.astype(F32).reshape(1, -1)

    h = x.astype(F32).reshape(nb * s, d)
    for l in range(depth):
        h = _mixer_sublayer(h, nb, s, _prep_w_in(w_in[l]), _prep_w_out(w_out[l].astype(F32)), a_biases, c_bias,
                            lower_bounds[l].reshape(1, B_KEY_WIDTH), _pad_heads_vec(hgrn_norm_g[l].astype(F32)),
                            attn_sinks[l].astype(F32), w_sum, row(ln1_g[l]), row(ln1_b[l]))
        h = _moe_sublayer(h, l, router_w[l], router_bias[l], expert_w_gate, expert_w_up, expert_w_down,
                          shared_w_gate[l], shared_w_up[l], shared_w_down[l], row(ln2_g[l]), row(ln2_b[l]))
    return h.reshape(nb, s, d).astype(x.dtype)
```

```python
import functools
import math

import numpy as np
import jax
import jax.numpy as jnp
from jax import lax
from jax.experimental import pallas as pl
from jax.experimental.pallas import tpu as pltpu
from jax.experimental.pallas import tpu_sc as plsc

F32 = jnp.float32
BF16 = jnp.bfloat16
I32 = jnp.int32
U32 = jnp.uint32

LANES = 128
SUBLANES = 8
VMEM_LIMIT = 56 * 1024 * 1024

D_MODEL = 1024
DEPTH = 2
HEAD_DIM = 64
BAND = 128
BAND_Q_TILE = 512
MASK_VALUE = -1e30

A_HEADS = 6
A_PATTERNS = ((128, 1), (512, 4), (2048, 16))
IN_PROJ_DILATIONS = tuple(r for _, r in A_PATTERNS if r > 1)
IN_PROJ_ROWS = 512
MIX_OUT_ROWS = 256
B_HEADS = 4
B_KEY_DIM = 128
B_VAL_DIM = 96
B_VAL_PAD = 128
HG_CHUNK = 64
HG_STEP_ROWS = 256
C_HEADS = 4
C_KV_HEADS = 2
C_WINDOW = 128

A_WIDTH = A_HEADS * HEAD_DIM
B_KEY_WIDTH = B_HEADS * B_KEY_DIM
B_WIDTH = B_HEADS * B_VAL_DIM
B_PAD_WIDTH = B_HEADS * B_VAL_PAD
C_WIDTH = C_HEADS * HEAD_DIM
C_KV_WIDTH = C_KV_HEADS * HEAD_DIM
IN_SPLITS = (A_WIDTH, A_WIDTH, A_WIDTH, B_KEY_WIDTH, B_KEY_WIDTH, B_WIDTH, B_WIDTH, C_WIDTH, C_KV_WIDTH, C_KV_WIDTH)

A_COLS = 3 * A_WIDTH
B_COLS = 4 * B_KEY_WIDTH
C_COLS = 3 * C_WIDTH
MIX_PAD_WIDTH = A_WIDTH + B_PAD_WIDTH + C_WIDTH

REL_BUCKETS = 32
REL_MAX_DIST = 2048

N_EXPERTS = 256
TOP_K = 8
N_GROUPS = 8
TOPK_GROUPS = 4
EXPERT_FF = 256
SHARED_FF = 256
ROUTED_SCALE = 2.5
EXPERT_BLOCK = 512
EXPERT_SLOTS = 3

DEEPNORM_ALPHA = (2 * DEPTH) ** 0.25
LN_EPS = 1e-5


def _cparams(*sem):
    return pltpu.CompilerParams(dimension_semantics=sem, vmem_limit_bytes=VMEM_LIMIT)


def _prep_w_in(w_in):
    d = w_in.shape[0]
    split_at = [int(i) for i in np.cumsum(IN_SPLITS)[:-1]]
    aq, ak, av, bq, bf, bi, bg, cq, ck, cv = jnp.split(w_in, split_at, axis=-1)
    pad_v = lambda w: jnp.pad(w.reshape(d, B_HEADS, B_VAL_DIM), ((0, 0), (0, 0), (0, B_VAL_PAD - B_VAL_DIM))).reshape(d, B_PAD_WIDTH)
    rep = lambda w: jnp.repeat(w.reshape(d, C_KV_HEADS, HEAD_DIM), C_HEADS // C_KV_HEADS, axis=1).reshape(d, C_WIDTH)
    cols = [aq, ak, av, bq, bf, pad_v(bi), pad_v(bg), cq, rep(ck), rep(cv)]
    return jnp.concatenate(cols, axis=-1).astype(BF16)


def _prep_w_out(w_out):
    d = w_out.shape[1]
    wa = w_out[:A_WIDTH]
    wb = w_out[A_WIDTH:A_WIDTH + B_WIDTH].reshape(B_HEADS, B_VAL_DIM, d)
    wb = jnp.pad(wb, ((0, 0), (0, B_VAL_PAD - B_VAL_DIM), (0, 0))).reshape(B_PAD_WIDTH, d)
    wc = w_out[A_WIDTH + B_WIDTH:]
    return jnp.concatenate([wa, wb, wc], axis=0).astype(BF16)


def _pad_heads_vec(v):
    return jnp.pad(v.reshape(B_HEADS, B_VAL_DIM), ((0, 0), (0, B_VAL_PAD - B_VAL_DIM))).reshape(1, B_PAD_WIDTH)


def _rel_bucket(dist):
    max_exact = REL_BUCKETS // 2
    d = jnp.maximum(dist, 0)
    log_ratio = jnp.log(jnp.maximum(d, max_exact).astype(F32) / max_exact) / math.log(REL_MAX_DIST / max_exact)
    large = jnp.minimum(max_exact + (log_ratio * (REL_BUCKETS - max_exact)).astype(I32), REL_BUCKETS - 1)
    return jnp.where(d < max_exact, d, large)


def _band_bias(rel_table, r, head_lo, head_hi):
    dist = jnp.arange(BAND)[:, None] + BAND - jnp.arange(2 * BAND)[None, :]
    onehot = jax.nn.one_hot(_rel_bucket(dist * r), REL_BUCKETS, dtype=F32)
    return jnp.einsum("qkb,bh->hqk", onehot, rel_table[:, head_lo:head_hi], precision=lax.Precision.HIGHEST)


def _hgrn_sum_matrix():
    c = HG_CHUNK
    t = np.arange(c)[:, None]
    u = np.arange(c)[None, :]
    blocks = [(u <= t), (u > t)]
    m = c
    while m >= 2:
        mid = (t // m) * m + m // 2
        second = t >= mid
        blocks.append(np.where(second, (u >= mid) & (u <= t), (u > t) & (u < mid)))
        m //= 2
    return np.concatenate(blocks, axis=0).astype(np.float32)


HG_LEVELS = int(math.log2(HG_CHUNK))


def _in_proj_body(x_ref, w_ref, *rest):
    n_res = len(IN_PROJ_DILATIONS)
    a_ref, res_refs, (b_ref, c_ref, slabs) = rest[0], rest[1:1 + n_res], rest[1 + n_res:]
    tm = x_ref.shape[0]
    xb = x_ref[...].astype(BF16)
    a = jnp.dot(xb, w_ref[:, :A_COLS], preferred_element_type=F32)
    a_ref[...] = a.astype(BF16)
    for j in range(A_COLS // LANES):
        slabs[j] = a[:, j * LANES:(j + 1) * LANES]
    for r, ref in zip(IN_PROJ_DILATIONS, res_refs):
        for p in range(r):
            for j in range(A_COLS // LANES):
                ref[0, p, :, j * LANES:(j + 1) * LANES] = slabs[j, pl.ds(p, tm // r, stride=r), :].astype(BF16)
    for j in range(B_COLS // B_KEY_WIDTH):
        lo = A_COLS + j * B_KEY_WIDTH
        b_ref[:, j * B_KEY_WIDTH:(j + 1) * B_KEY_WIDTH] = jnp.dot(
            xb, w_ref[:, lo:lo + B_KEY_WIDTH], preferred_element_type=F32)
    c_ref[...] = jnp.dot(xb, w_ref[:, A_COLS + B_COLS:], preferred_element_type=F32).astype(BF16)


def _in_proj(x2d, w_p, nb, s):
    t, d = x2d.shape
    tm = IN_PROJ_ROWS
    n = w_p.shape[1]
    tiles = s // tm
    res_specs = [pl.BlockSpec((1, r, tm // r, A_COLS), lambda i: (i // tiles, 0, i % tiles, 0))
                 for r in IN_PROJ_DILATIONS]
    res_shapes = [jax.ShapeDtypeStruct((nb, r, s // r, A_COLS), BF16) for r in IN_PROJ_DILATIONS]
    outs = pl.pallas_call(
        _in_proj_body,
        grid=(t // tm,),
        in_specs=[pl.BlockSpec((tm, d), lambda i: (i, 0)),
                  pl.BlockSpec((d, n), lambda i: (0, 0))],
        out_specs=[pl.BlockSpec((tm, A_COLS), lambda i: (i, 0))] + res_specs + [
                   pl.BlockSpec((tm, B_COLS), lambda i: (i, 0)),
                   pl.BlockSpec((tm, C_COLS), lambda i: (i, 0))],
        out_shape=[jax.ShapeDtypeStruct((t, A_COLS), BF16)] + res_shapes + [
                   jax.ShapeDtypeStruct((t, B_COLS), F32),
                   jax.ShapeDtypeStruct((t, C_COLS), BF16)],
        scratch_shapes=[pltpu.VMEM((A_COLS // LANES, tm, LANES), F32)],
        compiler_params=_cparams("arbitrary"),
        name="in_proj",
    )(x2d, w_p)
    return outs[0], list(outs[1:-2]), outs[-2], outs[-1]


def _band_attn_body(*refs, width, max_dist, has_sink, want_lse):
    q_ref, kp_ref, kc_ref, vp_ref, vc_ref, bias_ref = refs[:6]
    rest = refs[6:]
    if has_sink:
        sink_ref, rest = rest[0], rest[1:]
    o_ref = rest[0]
    lse_ref = rest[1] if want_lse else None

    first_tile = pl.program_id(2) == 0
    row = lax.broadcasted_iota(I32, (BAND, 2 * BAND), 0)
    col = lax.broadcasted_iota(I32, (BAND, 2 * BAND), 1)
    dist = row + BAND - col
    in_band = (dist >= 0) & (dist <= max_dist)
    first_mask = in_band & ((col >= BAND) | jnp.logical_not(first_tile))
    low_half = lax.broadcasted_iota(I32, (BAND, LANES), 1) < HEAD_DIM
    scale = HEAD_DIM ** -0.5

    for qb in range(q_ref.shape[0] // BAND):
        rows = slice(qb * BAND, (qb + 1) * BAND)
        mask = first_mask if qb == 0 else in_band
        for tile in range(width // LANES):
            sl = slice(tile * LANES, (tile + 1) * LANES)
            q2 = q_ref[rows, sl]
            if qb == 0:
                k2 = jnp.concatenate([kp_ref[:, sl], kc_ref[:BAND, sl]], axis=0)
                v2 = jnp.concatenate([vp_ref[:, sl], vc_ref[:BAND, sl]], axis=0)
            else:
                k2 = kc_ref[(qb - 1) * BAND:(qb + 1) * BAND, sl]
                v2 = vc_ref[(qb - 1) * BAND:(qb + 1) * BAND, sl]
            outs, lses = [], []
            for half in range(2):
                h = 2 * tile + half
                qm = jnp.where(low_half if half == 0 else jnp.logical_not(low_half), q2, jnp.zeros_like(q2))
                s = lax.dot_general(qm, k2, (((1,), (1,)), ((), ())), preferred_element_type=F32)
                s = s * scale + bias_ref[h]
                s = jnp.where(mask, s, MASK_VALUE)
                m = jnp.max(s, axis=-1, keepdims=True)
                if has_sink:
                    sink = sink_ref[h]
                    m = jnp.maximum(m, sink)
                p = jnp.exp(s - m)
                den = jnp.sum(p, axis=-1, keepdims=True)
                if has_sink:
                    den = den + jnp.exp(sink - m)
                pv = jnp.dot(p.astype(BF16), v2, preferred_element_type=F32)
                outs.append(pv / den)
                if want_lse:
                    lses.append(jnp.broadcast_to(m + jnp.log(den), (BAND, LANES)))
            o_ref[rows, sl] = jnp.where(low_half, outs[0], outs[1])
            if want_lse:
                lse_ref[rows, sl] = jnp.where(low_half, lses[0], lses[1])


def _band_attn(src, bias, *, width, max_dist, sinks=None, want_lse):
    nb, r, length, _ = src.shape
    qt = min(BAND_Q_TILE, length)
    bands = qt // BAND
    heads = width // HEAD_DIM
    has_sink = sinks is not None

    def cur(off):
        return pl.BlockSpec((None, None, qt, width), lambda b, p, i: (b, p, i, off))

    def prev(off):
        return pl.BlockSpec((None, None, BAND, width), lambda b, p, i: (b, p, jnp.maximum(i * bands - 1, 0), off))

    in_specs = [cur(0), prev(1), cur(1), prev(2), cur(2),
                pl.BlockSpec((heads, BAND, 2 * BAND), lambda b, p, i: (0, 0, 0))]
    args = [src, src, src, src, src, bias]
    if has_sink:
        in_specs.append(pl.BlockSpec(memory_space=pltpu.SMEM))
        args.append(sinks)
    out_spec = pl.BlockSpec((None, None, qt, width), lambda b, p, i: (b, p, i, 0))
    out_sds = jax.ShapeDtypeStruct((nb, r, length, width), F32)
    body = functools.partial(_band_attn_body, width=width, max_dist=max_dist, has_sink=has_sink, want_lse=want_lse)
    return pl.pallas_call(
        body,
        grid=(nb, r, length // qt),
        in_specs=in_specs,
        out_specs=[out_spec, out_spec] if want_lse else out_spec,
        out_shape=[out_sds, out_sds] if want_lse else out_sds,
        compiler_params=_cparams("arbitrary", "arbitrary", "arbitrary"),
        name="band_attn_r%d_w%d" % (r, width),
    )(*args)


def _hgrn_body(b_ref, lb_ref, ng_ref, w_ref, o_ref, state_ref):
    c = HG_CHUNK
    kd = B_KEY_DIM

    @pl.when(pl.program_id(1) == 0)
    def _():
        state_ref[...] = jnp.zeros_like(state_ref)

    trow = lax.broadcasted_iota(I32, (c, 1), 0)
    ti = lax.broadcasted_iota(I32, (c, c), 0)
    si = lax.broadcasted_iota(I32, (c, c), 1)
    nt = (((1,), (1,)), ((), ()))
    w_all = w_ref[...]

    states = [state_ref[h] for h in range(B_HEADS)]
    for chunk, h in [(ci, hi) for ci in range(b_ref.shape[1] // c) for hi in range(B_HEADS)]:
        rows = slice(chunk * c, (chunk + 1) * c)
        ks = slice(h * kd, (h + 1) * kd)
        q = b_ref[0, rows, ks]
        f = b_ref[0, rows, B_KEY_WIDTH + h * kd:B_KEY_WIDTH + (h + 1) * kd]
        inp = b_ref[0, rows, 2 * B_KEY_WIDTH + h * kd:2 * B_KEY_WIDTH + (h + 1) * kd]
        gate = b_ref[0, rows, 3 * B_KEY_WIDTH + h * kd:3 * B_KEY_WIDTH + (h + 1) * kd]
        lb = lb_ref[:, ks]

        log_f = jnp.log(lb + (1.0 - lb) * jax.nn.sigmoid(f))
        key = (1.0 - lb) * jax.nn.sigmoid(-f)
        qs = q * jax.nn.sigmoid(q)

        g_hi = log_f.astype(BF16)
        g_lo = (log_f - g_hi.astype(F32)).astype(BF16)
        e2 = jnp.dot(w_all, jnp.concatenate([g_hi, g_lo], axis=1), preferred_element_type=F32)
        e = e2[:, :kd] + e2[:, kd:]

        cum = e[0:c]
        q_dec = (qs * jnp.exp(cum)).astype(BF16)
        k_dec = (key * jnp.exp(e[c:2 * c])).astype(BF16)
        inp_b = inp.astype(BF16)

        st = states[h]
        inter = lax.dot_general(q_dec, st.astype(BF16), nt, preferred_element_type=F32)

        scores = jnp.where(ti == si,
                           lax.dot_general(qs.astype(BF16), key.astype(BF16), nt, preferred_element_type=F32), 0.0)
        m = c
        for lvl in range(HG_LEVELS):
            el = jnp.exp(e[(2 + lvl) * c:(3 + lvl) * c])
            second = (trow % m) >= (m // 2)
            ql = jnp.where(second, qs * el, 0.0).astype(BF16)
            kl = jnp.where(second, 0.0, key * el).astype(BF16)
            sl = lax.dot_general(ql, kl, nt, preferred_element_type=F32)
            if m < c:
                sl = jnp.where((ti // m) == (si // m), sl, 0.0)
            scores = scores + sl
            m //= 2
        intra = jnp.dot(scores.astype(BF16), inp_b, preferred_element_type=F32)

        new_st = st * jnp.exp(cum[c - 1:c]) + lax.dot_general(
            inp_b, k_dec, (((0,), (0,)), ((), ())), preferred_element_type=F32)
        states[h] = new_st

        o = inter + intra
        ms = jnp.sum(o * o, axis=-1, keepdims=True) * (1.0 / B_VAL_DIM)
        o = o * lax.rsqrt(ms + 1e-6)
        o_ref[0, rows, ks] = o * ng_ref[:, ks] * (gate * jax.nn.sigmoid(gate))

    for h in range(B_HEADS):
        state_ref[h] = states[h]


def _hgrn(b_all, lower_bound, norm_g_pad, w_sum):
    nb, s, _ = b_all.shape
    c = HG_STEP_ROWS
    return pl.pallas_call(
        _hgrn_body,
        grid=(nb, s // c),
        in_specs=[pl.BlockSpec((1, c, B_COLS), lambda b, i: (b, i, 0)),
                  pl.BlockSpec((1, B_KEY_WIDTH), lambda b, i: (0, 0)),
                  pl.BlockSpec((1, B_PAD_WIDTH), lambda b, i: (0, 0)),
                  pl.BlockSpec(w_sum.shape, lambda b, i: (0, 0))],
        out_specs=pl.BlockSpec((1, c, B_PAD_WIDTH), lambda b, i: (b, i, 0)),
        out_shape=jax.ShapeDtypeStruct((nb, s, B_PAD_WIDTH), F32),
        scratch_shapes=[pltpu.VMEM((B_HEADS, B_VAL_PAD, B_KEY_DIM), F32)],
        compiler_params=_cparams("arbitrary", "arbitrary"),
        name="hgrn2",
    )(b_all, lower_bound, norm_g_pad, w_sum)


def _layer_norm_rows(z, g, b):
    mu = jnp.mean(z, axis=-1, keepdims=True)
    zc = z - mu
    var = jnp.mean(zc * zc, axis=-1, keepdims=True)
    return zc * lax.rsqrt(var + LN_EPS) * g + b


def _mix_out_body(*refs):
    n_pat = len(A_PATTERNS)
    o_refs, l_refs = refs[:n_pat], refs[n_pat:2 * n_pat]
    ob, oc, x_ref, w_ref, g_ref, b_ref, out_ref = refs[2 * n_pat:2 * n_pat + 7]
    scratch = refs[2 * n_pat + 7:]
    tm = x_ref.shape[0]
    n_slabs = A_WIDTH // LANES

    def token_order(ref, r, scr):
        if r == 1:
            return [ref[0, 0, :, j * LANES:(j + 1) * LANES] for j in range(n_slabs)]
        for p in range(r):
            for j in range(n_slabs):
                scr[j, pl.ds(p, tm // r, stride=r), :] = ref[0, p, :, j * LANES:(j + 1) * LANES]
        return [scr[j] for j in range(n_slabs)]

    scr_iter = iter(scratch)
    dil = [r for _, r in A_PATTERNS]
    o_slabs = [token_order(ref, r, None if r == 1 else next(scr_iter)) for ref, r in zip(o_refs, dil)]
    l_slabs = [token_order(ref, r, None if r == 1 else next(scr_iter)) for ref, r in zip(l_refs, dil)]

    merged = []
    for j in range(n_slabs):
        ls = [l[j] for l in l_slabs]
        m = functools.reduce(jnp.maximum, ls)
        ws = [jnp.exp(l - m) for l in ls]
        num = functools.reduce(lambda a, b: a + b, [w * o[j] for w, o in zip(ws, o_slabs)])
        merged.append(num / functools.reduce(lambda a, b: a + b, ws))
    cat = jnp.concatenate(merged + [ob[...], oc[...]], axis=1).astype(BF16)
    y = jnp.dot(cat, w_ref[...], preferred_element_type=F32)
    z = DEEPNORM_ALPHA * x_ref[...] + y
    out_ref[...] = _layer_norm_rows(z, g_ref[...], b_ref[...])


def _mix_out(o_list, l_list, ob, oc, x2d, w_out_p, ln_g, ln_b):
    t, d = x2d.shape
    tm = MIX_OUT_ROWS
    tiles = o_list[0].shape[1] * o_list[0].shape[2] // tm
    row = lambda w: pl.BlockSpec((tm, w), lambda i: (i, 0))
    full = lambda a: pl.BlockSpec(a.shape, lambda i: (0, 0))
    res = lambda a: pl.BlockSpec((1, a.shape[1], tm // a.shape[1], A_WIDTH), lambda i: (i // tiles, 0, i % tiles, 0))
    n_scratch = 2 * sum(1 for _, r in A_PATTERNS if r > 1)
    return pl.pallas_call(
        _mix_out_body,
        grid=(t // tm,),
        in_specs=[res(a) for a in o_list + l_list] + [row(B_PAD_WIDTH), row(C_WIDTH), row(d), full(w_out_p),
                                                      full(ln_g), full(ln_b)],
        out_specs=row(d),
        out_shape=jax.ShapeDtypeStruct((t, d), F32),
        scratch_shapes=[pltpu.VMEM((A_WIDTH // LANES, tm, LANES), F32)] * n_scratch,
        compiler_params=_cparams("arbitrary"),
        name="mix_out_ln",
    )(*o_list, *l_list, ob, oc, x2d, w_out_p, ln_g, ln_b)


def _router_body(h_ref, rw_ref, bias_ref, tri_ref, e_ref, rank_ref, gate_ref, cnt_ref, hp_ref, carry_ref):
    tn = h_ref.shape[0]
    per_group = N_EXPERTS // N_GROUPS
    neg_inf = -jnp.inf

    @pl.when(pl.program_id(0) == 0)
    def _():
        carry_ref[...] = jnp.zeros_like(carry_ref)

    hp_ref[...] = _pack_bf16_pairs(h_ref[...])

    logits = lax.dot_general(rw_ref[...], h_ref[...], (((1,), (1,)), ((), ())),
                             precision=lax.Precision.HIGHEST, preferred_element_type=F32)
    scores = jax.nn.sigmoid(logits)
    choice = scores + bias_ref[...]

    def first_max(vals, idx, sentinel):
        top = jnp.max(vals, axis=0, keepdims=True)
        return top, jnp.min(jnp.where(vals == top, idx, sentinel), axis=0, keepdims=True)

    li = lax.broadcasted_iota(I32, (per_group, tn), 0).astype(F32)
    group_rows = []
    for g in range(N_GROUPS):
        cg = choice[g * per_group:(g + 1) * per_group]
        m1, first = first_max(cg, li, float(per_group))
        m2 = jnp.max(jnp.where(li == first, neg_inf, cg), axis=0, keepdims=True)
        group_rows.append(m1 + m2)
    group_score = jnp.concatenate(group_rows, axis=0)

    gi = lax.broadcasted_iota(I32, (N_GROUPS, tn), 0).astype(F32)
    group_ok = jnp.zeros((N_GROUPS, tn), F32)
    cur = group_score
    for _ in range(TOPK_GROUPS):
        _, first = first_max(cur, gi, float(N_GROUPS))
        pick = gi == first
        group_ok = jnp.where(pick, 1.0, group_ok)
        cur = jnp.where(pick, neg_inf, cur)

    cur = jnp.concatenate(
        [jnp.where(group_ok[g:g + 1] > 0.0, choice[g * per_group:(g + 1) * per_group], MASK_VALUE)
         for g in range(N_GROUPS)], axis=0)
    ei = lax.broadcasted_iota(I32, (N_EXPERTS, tn), 0).astype(F32)
    chosen = jnp.zeros((N_EXPERTS, tn), F32)
    picks, gates = [], []
    for _ in range(TOP_K):
        _, idx = first_max(cur, ei, float(N_EXPERTS))
        pick = ei == idx
        picks.append(idx)
        gates.append(jnp.sum(jnp.where(pick, scores, 0.0), axis=0, keepdims=True))
        chosen = jnp.where(pick, 1.0, chosen)
        cur = jnp.where(pick, neg_inf, cur)

    gate = jnp.concatenate(gates, axis=0)
    gate_ref[...] = gate / jnp.sum(gate, axis=0, keepdims=True) * ROUTED_SCALE
    e_ref[...] = jnp.concatenate(picks, axis=0).astype(I32)

    before = jnp.dot(chosen.astype(BF16), tri_ref[...], preferred_element_type=F32) + carry_ref[...]
    ranks = [jnp.sum(jnp.where(ei == idx, before, 0.0), axis=0, keepdims=True) for idx in picks]
    rank_ref[...] = jnp.concatenate(ranks, axis=0).astype(I32)
    carry = carry_ref[...] + jnp.sum(chosen, axis=1, keepdims=True)
    carry_ref[...] = carry
    cnt_ref[...] = carry.astype(I32)


def _router(h2d, rw_t, bias_col):
    t, d = h2d.shape
    tn = 256
    tri = jnp.asarray(np.triu(np.ones((tn, tn), np.float32), k=1), dtype=BF16)
    tok = lambda: pl.BlockSpec((TOP_K, tn), lambda i: (0, i))
    return pl.pallas_call(
        _router_body,
        grid=(t // tn,),
        in_specs=[pl.BlockSpec((tn, d), lambda i: (i, 0)),
                  pl.BlockSpec((N_EXPERTS, d), lambda i: (0, 0)),
                  pl.BlockSpec((N_EXPERTS, 1), lambda i: (0, 0)),
                  pl.BlockSpec((tn, tn), lambda i: (0, 0))],
        out_specs=[tok(), tok(), tok(), pl.BlockSpec((N_EXPERTS, 1), lambda i: (0, 0)),
                   pl.BlockSpec((tn, d // 2), lambda i: (i, 0))],
        out_shape=[jax.ShapeDtypeStruct((TOP_K, t), I32),
                   jax.ShapeDtypeStruct((TOP_K, t), I32),
                   jax.ShapeDtypeStruct((TOP_K, t), F32),
                   jax.ShapeDtypeStruct((N_EXPERTS, 1), I32),
                   jax.ShapeDtypeStruct((t, d // 2), U32)],
        scratch_shapes=[pltpu.VMEM((N_EXPERTS, 1), F32)],
        compiler_params=_cparams("arbitrary"),
        name="moe_router",
    )(h2d, rw_t, bias_col, tri)


def _pack_bf16_pairs(x):
    w = x.shape[1] // 2
    hi = lax.bitcast_convert_type(x[:, :w].astype(BF16).astype(F32), U32)
    lo = lax.bitcast_convert_type(x[:, w:].astype(BF16).astype(F32), U32)
    return hi | (lo >> 16)


def _unpack_bf16_pairs(p):
    hi = lax.bitcast_convert_type(p & jnp.uint32(0xFFFF0000), F32)
    lo = lax.bitcast_convert_type(p << 16, F32)
    return hi, lo


def _dest_body(e_ref, rank_ref, offs_ref, dest_ref):
    tn = e_ref.shape[1]
    ei = lax.broadcasted_iota(I32, (N_EXPERTS, tn), 0)
    offs = offs_ref[...]
    rows = [jnp.sum(jnp.where(ei == e_ref[k:k + 1, :], offs, 0.0), axis=0, keepdims=True) for k in range(TOP_K)]
    dest_ref[...] = jnp.concatenate(rows, axis=0).astype(I32) + rank_ref[...]


def _dest_rows(e_t, rank_t, offsets):
    t = e_t.shape[1]
    tn = 512
    tok = pl.BlockSpec((TOP_K, tn), lambda i: (0, i))
    return pl.pallas_call(
        _dest_body,
        grid=(t // tn,),
        in_specs=[tok, tok, pl.BlockSpec((N_EXPERTS, 1), lambda i: (0, 0))],
        out_specs=tok,
        out_shape=jax.ShapeDtypeStruct((TOP_K, t), I32),
        compiler_params=_cparams("arbitrary"),
        name="moe_dest",
    )(e_t, rank_t, offsets.astype(F32).reshape(N_EXPERTS, 1))


def _expert_body(offs_ref, nblk_ref, cnt_ref, layer_ref, xs_hbm, wg_ref, wu_ref, wd_ref, ys_hbm,
                 xbuf, ybuf, wg_b, wu_b, wd_b, sem_in, sem_out):
    e = pl.program_id(0)
    slots, bm = xbuf.shape[:2]
    ahead = slots - 1
    n = nblk_ref[e]
    first = offs_ref[e] // bm
    total = (offs_ref[N_EXPERTS - 1] // bm) + nblk_ref[N_EXPERTS - 1]

    def rows(g):
        return pl.ds(pl.multiple_of(g * bm, bm), bm)

    def x_copy(g, slot):
        return pltpu.make_async_copy(xs_hbm.at[rows(g)], xbuf.at[slot], sem_in.at[slot])

    def y_copy(g, slot):
        return pltpu.make_async_copy(ybuf.at[slot], ys_hbm.at[rows(g)], sem_out.at[slot])

    @pl.when(e == 0)
    def _():
        for g0 in range(ahead):
            @pl.when(g0 < total)
            def _():
                x_copy(g0, g0).start()

    wg_b[...] = wg_ref[0, 0].astype(BF16)
    wu_b[...] = wu_ref[0, 0].astype(BF16)
    wd_b[...] = wd_ref[0, 0].astype(BF16)

    def block(j, carry):
        g = first + j
        slot = g % slots
        x_copy(g, slot).wait()

        @pl.when(g + ahead < total)
        def _():
            x_copy(g + ahead, (g + ahead) % slots).start()

        @pl.when(g >= slots)
        def _():
            y_copy(g - slots, slot).wait()

        live = lax.broadcasted_iota(I32, (bm, 1), 0) < (cnt_ref[e] - j * bm)
        hi, lo = _unpack_bf16_pairs(jnp.where(live, xbuf[slot], jnp.uint32(0)))
        xb = jnp.concatenate([hi.astype(BF16), lo.astype(BF16)], axis=1)
        gate = jnp.dot(xb, wg_b[...], preferred_element_type=F32)
        up = jnp.dot(xb, wu_b[...], preferred_element_type=F32)
        hidden = (gate * jax.nn.sigmoid(gate) * up).astype(BF16)
        ybuf[slot] = _pack_bf16_pairs(jnp.dot(hidden, wd_b[...], preferred_element_type=F32))
        y_copy(g, slot).start()
        return carry

    lax.fori_loop(0, n, block, 0)

    @pl.when(e == N_EXPERTS - 1)
    def _():
        for back in range(1, slots + 1):
            @pl.when(total >= back)
            def _():
                y_copy(total - back, (total - back) % slots).wait()


def _experts(xs, offsets, n_blk, counts, layer, w_gate, w_up, w_down):
    n_rows, dp = xs.shape
    bm = EXPERT_BLOCK
    d, ff = w_gate.shape[-2:]
    w_spec = lambda a, b: pl.BlockSpec((1, 1, a, b), lambda e, of, nb, ct, ly: (ly[0], e, 0, 0))
    grid_spec = pltpu.PrefetchScalarGridSpec(
        num_scalar_prefetch=4,
        grid=(N_EXPERTS,),
        in_specs=[pl.BlockSpec(memory_space=pl.ANY), w_spec(d, ff), w_spec(d, ff), w_spec(ff, d)],
        out_specs=pl.BlockSpec(memory_space=pl.ANY),
        scratch_shapes=[pltpu.VMEM((EXPERT_SLOTS, bm, dp), U32), pltpu.VMEM((EXPERT_SLOTS, bm, dp), U32),
                        pltpu.VMEM((d, ff), BF16), pltpu.VMEM((d, ff), BF16), pltpu.VMEM((ff, d), BF16),
                        pltpu.SemaphoreType.DMA((EXPERT_SLOTS,)), pltpu.SemaphoreType.DMA((EXPERT_SLOTS,))],
    )
    return pl.pallas_call(
        _expert_body,
        grid_spec=grid_spec,
        out_shape=jax.ShapeDtypeStruct((n_rows, dp), U32),
        compiler_params=_cparams("arbitrary"),
        name="moe_experts",
    )(offsets, n_blk, counts, layer, xs, w_gate, w_up, w_down)


SC_GATHER_ROWS = 64


def _sc_gather_rows(table, idx):
    info = plsc.get_sparse_core_info()
    nc, ns = info.num_cores, info.num_subcores
    workers = nc * ns
    n = idx.shape[0]
    w = table.shape[1]
    ch = SC_GATHER_ROWS
    per_worker = n // workers
    steps = per_worker // ch
    assert per_worker * workers == n and steps * ch == per_worker and steps % 2 == 0
    mesh = plsc.VectorSubcoreMesh(core_axis_name="c", subcore_axis_name="s")

    @functools.partial(
        pl.kernel, mesh=mesh,
        out_type=jax.ShapeDtypeStruct((n, w), table.dtype),
        scratch_types=[pltpu.VMEM((steps, ch), I32), pltpu.VMEM((2, ch, w), table.dtype),
                       pltpu.SemaphoreType.DMA((2,)), pltpu.SemaphoreType.DMA((2,))],
    )
    def gather_kernel(table_hbm, idx_hbm, out_hbm, idx_v, rows_v, gsem, wsem):
        wid = lax.axis_index("s") * nc + lax.axis_index("c")
        base = wid * per_worker
        pltpu.sync_copy(idx_hbm.at[wid], idx_v)

        def gather(i, slot):
            return pltpu.make_async_copy(table_hbm.at[idx_v.at[i]], rows_v.at[slot], gsem.at[slot])

        def write(i, slot):
            off = pl.multiple_of(base + i * ch, ch)
            return pltpu.make_async_copy(rows_v.at[slot], out_hbm.at[pl.ds(off, ch)], wsem.at[slot])

        gather(0, 0).start()

        @pl.loop(0, steps, step=2)
        def _(i):
            gather(i + 1, 1).start()
            gather(i, 0).wait()
            write(i, 0).start()
            write(i, 0).wait()

            @pl.when(i + 2 < steps)
            def _():
                gather(i + 2, 0).start()

            gather(i + 1, 1).wait()
            write(i + 1, 1).start()
            write(i + 1, 1).wait()

    return gather_kernel(table, idx.reshape(workers, steps, ch))


def _sc_scatter_rows(rows, idx, n_out):
    info = plsc.get_sparse_core_info()
    nc, ns = info.num_cores, info.num_subcores
    workers = nc * ns
    kk, t = idx.shape
    w = rows.shape[1]
    ch = SC_GATHER_ROWS
    per_worker = t // workers
    steps = per_worker // ch
    assert per_worker * workers == t and steps * ch == per_worker and steps % 2 == 0
    idx_w = idx.reshape(kk, workers, steps, ch).transpose(1, 0, 2, 3).reshape(workers, kk * steps, ch)
    mesh = plsc.VectorSubcoreMesh(core_axis_name="c", subcore_axis_name="s")

    @functools.partial(
        pl.kernel, mesh=mesh,
        out_type=jax.ShapeDtypeStruct((n_out, w), rows.dtype),
        scratch_types=[pltpu.VMEM((kk * steps, ch), I32), pltpu.VMEM((2, ch, w), rows.dtype),
                       pltpu.SemaphoreType.DMA((2,)), pltpu.SemaphoreType.DMA((2,))],
    )
    def scatter_kernel(rows_hbm, idx_hbm, out_hbm, idx_v, rows_v, rsem, ssem):
        wid = lax.axis_index("s") * nc + lax.axis_index("c")
        base = wid * per_worker
        pltpu.sync_copy(idx_hbm.at[wid], idx_v)

        def read(i, slot):
            off = pl.multiple_of(base + i * ch, ch)
            return pltpu.make_async_copy(rows_hbm.at[pl.ds(off, ch)], rows_v.at[slot], rsem.at[slot])

        def scatter(i, k, slot):
            return pltpu.make_async_copy(rows_v.at[slot], out_hbm.at[idx_v.at[k * steps + i]], ssem.at[slot])

        def scatter_all(i, slot):
            for k in range(kk):
                scatter(i, k, slot).start()
            for k in range(kk):
                scatter(i, k, slot).wait()

        read(0, 0).start()

        @pl.loop(0, steps, step=2)
        def _(i):
            read(i + 1, 1).start()
            read(i, 0).wait()
            scatter_all(i, 0)

            @pl.when(i + 2 < steps)
            def _():
                read(i + 2, 0).start()

            read(i + 1, 1).wait()
            scatter_all(i + 1, 1)

    return scatter_kernel(rows, idx_w)


def _combine_streamed_body(y_ref, gate_ref, h_ref, sg_ref, su_ref, sd_ref, g_ref, b_ref, out_ref):
    half = h_ref.shape[1] // 2
    h = h_ref[...]
    hb = h.astype(BF16)
    sg = jnp.dot(hb, sg_ref[...], preferred_element_type=F32)
    su = jnp.dot(hb, su_ref[...], preferred_element_type=F32)
    shared = jnp.dot((sg * jax.nn.sigmoid(sg) * su).astype(BF16), sd_ref[...], preferred_element_type=F32)
    z = DEEPNORM_ALPHA * h + shared
    z_hi, z_lo = z[:, :half], z[:, half:]
    for k in range(TOP_K):
        y_hi, y_lo = _unpack_bf16_pairs(y_ref[k])
        gate = gate_ref[:, k:k + 1]
        z_hi = z_hi + gate * y_hi
        z_lo = z_lo + gate * y_lo
    out_ref[...] = _layer_norm_rows(jnp.concatenate([z_hi, z_lo], axis=1), g_ref[...], b_ref[...])


def _combine_streamed(y_gathered, gate_tk, h2d, sh_gate, sh_up, sh_down, ln_g, ln_b):
    t, d = h2d.shape
    tt = 256
    full = lambda a: pl.BlockSpec(a.shape, lambda i: (0, 0))
    return pl.pallas_call(
        _combine_streamed_body,
        grid=(t // tt,),
        in_specs=[pl.BlockSpec((TOP_K, tt, d // 2), lambda i: (0, i, 0)),
                  pl.BlockSpec((tt, TOP_K), lambda i: (i, 0)),
                  pl.BlockSpec((tt, d), lambda i: (i, 0)),
                  full(sh_gate), full(sh_up), full(sh_down), full(ln_g), full(ln_b)],
        out_specs=pl.BlockSpec((tt, d), lambda i: (i, 0)),
        out_shape=jax.ShapeDtypeStruct((t, d), F32),
        compiler_params=_cparams("arbitrary"),
        name="moe_combine_streamed_ln",
    )(y_gathered, gate_tk, h2d, sh_gate, sh_up, sh_down, ln_g, ln_b)


def _mixer_sublayer(h2d, nb, s, w_in_p, w_out_p, a_biases, c_bias, lower_bound, norm_g_pad, sinks, w_sum, ln_g, ln_b):
    a_qkv, a_residue, b_all, c_qkv = _in_proj(h2d, w_in_p, nb, s)
    a_by_dilation = {1: a_qkv.reshape(nb, 1, s, A_COLS), **dict(zip(IN_PROJ_DILATIONS, a_residue))}
    o_list, l_list = [], []
    for (window, r), bias in zip(A_PATTERNS, a_biases):
        o, lse = _band_attn(a_by_dilation[r], bias, width=A_WIDTH, max_dist=window // r, want_lse=True)
        o_list.append(o)
        l_list.append(lse)
    oc = _band_attn(c_qkv.reshape(nb, 1, s, C_COLS), c_bias, width=C_WIDTH,
                    max_dist=C_WINDOW - 1, sinks=sinks, want_lse=False).reshape(nb * s, C_WIDTH)
    ob = _hgrn(b_all.reshape(nb, s, B_COLS), lower_bound, norm_g_pad, w_sum).reshape(nb * s, B_PAD_WIDTH)
    return _mix_out(o_list, l_list, ob, oc, h2d, w_out_p, ln_g, ln_b)


def _moe_sublayer(h2d, layer, router_w, router_bias, w_gate, w_up, w_down, sh_gate, sh_up, sh_down, ln_g, ln_b):
    t, d = h2d.shape
    bm = EXPERT_BLOCK
    e_t, rank_t, gate_t, counts, h_packed = _router(h2d, router_w.T.astype(F32),
                                                    router_bias.astype(F32).reshape(N_EXPERTS, 1))

    counts = counts.reshape(N_EXPERTS)
    padded = (counts + bm - 1) // bm * bm
    pad_end = jnp.cumsum(padded).astype(I32)
    offsets = pad_end - padded
    n_blocks = -(-(t * TOP_K + N_EXPERTS * (bm - 1)) // bm)

    dest_t = _dest_rows(e_t, rank_t, offsets)
    xs = _sc_scatter_rows(h_packed, dest_t, n_blocks * bm)
    ys = _experts(xs, offsets, (padded // bm).astype(I32), counts, jnp.full((1,), layer, I32), w_gate, w_up, w_down)
    y_gathered = _sc_gather_rows(ys, dest_t.reshape(TOP_K * t)).reshape(TOP_K, t, d // 2)
    return _combine_streamed(y_gathered, gate_t.T, h2d,
                             sh_gate.astype(BF16), sh_up.astype(BF16), sh_down.astype(BF16), ln_g, ln_b)


def kernel(x, w_in, w_out, rel_bias_table, lower_bound_logits, hgrn_norm_g, attn_sinks, ln1_g, ln1_b, router_w, router_bias, expert_w_gate, expert_w_up, expert_w_down, shared_w_gate, shared_w_up, shared_w_down, ln2_g, ln2_b):
    nb, s, d = x.shape
    depth = w_in.shape[0]
    lb_probs = jax.nn.softmax(lower_bound_logits.astype(F32), axis=0)
    lower_bounds = jnp.cumsum(lb_probs, axis=0) - lb_probs[0]
    rel_table = rel_bias_table.astype(F32)
    a_biases = [_band_bias(rel_table, r, 0, A_HEADS) for _, r in A_PATTERNS]
    c_bias = _band_bias(rel_table, 1, A_HEADS, A_HEADS + C_HEADS)
    w_sum = jnp.asarray(_hgrn_sum_matrix(), dtype=BF16)
    row = lambda v: v.astype(F32).reshape(1, -1)

    h = x.astype(F32).reshape(nb * s, d)
    for l in range(depth):
        h = _mixer_sublayer(h, nb, s, _prep_w_in(w_in[l]), _prep_w_out(w_out[l].astype(F32)), a_biases, c_bias,
                            lower_bounds[l].reshape(1, B_KEY_WIDTH), _pad_heads_vec(hgrn_norm_g[l].astype(F32)),
                            attn_sinks[l].astype(F32), w_sum, row(ln1_g[l]), row(ln1_b[l]))
        h = _moe_sublayer(h, l, router_w[l], router_bias[l], expert_w_gate, expert_w_up, expert_w_down,
                          shared_w_gate[l], shared_w_up[l], shared_w_down[l], row(ln2_g[l]), row(ln2_b[l]))
    return h.reshape(nb, s, d).astype(x.dtype)
```

```python
import functools
import math

import numpy as np
import jax
import jax.numpy as jnp
from jax import lax
from jax.experimental import pallas as pl
from jax.experimental.pallas import tpu as pltpu
from jax.experimental.pallas import tpu_sc as plsc

F32 = jnp.float32
BF16 = jnp.bfloat16
I32 = jnp.int32
U32 = jnp.uint32

LANES = 128
SUBLANES = 8
VMEM_LIMIT = 56 * 1024 * 1024

D_MODEL = 1024
DEPTH = 2
HEAD_DIM = 64
BAND = 128
BAND_Q_TILE = 512
MASK_VALUE = -1e30

A_HEADS = 6
A_PATTERNS = ((128, 1), (512, 4), (2048, 16))
IN_PROJ_DILATIONS = tuple(r for _, r in A_PATTERNS if r > 1)
IN_PROJ_ROWS = 512
MIX_OUT_ROWS = 256
B_HEADS = 4
B_KEY_DIM = 128
B_VAL_DIM = 96
B_VAL_PAD = 128
HG_CHUNK = 64
HG_STEP_ROWS = 256
C_HEADS = 4
C_KV_HEADS = 2
C_WINDOW = 128

A_WIDTH = A_HEADS * HEAD_DIM
B_KEY_WIDTH = B_HEADS * B_KEY_DIM
B_WIDTH = B_HEADS * B_VAL_DIM
B_PAD_WIDTH = B_HEADS * B_VAL_PAD
C_WIDTH = C_HEADS * HEAD_DIM
C_KV_WIDTH = C_KV_HEADS * HEAD_DIM
IN_SPLITS = (A_WIDTH, A_WIDTH, A_WIDTH, B_KEY_WIDTH, B_KEY_WIDTH, B_WIDTH, B_WIDTH, C_WIDTH, C_KV_WIDTH, C_KV_WIDTH)

A_COLS = 3 * A_WIDTH
B_COLS = 4 * B_KEY_WIDTH
C_COLS = 3 * C_WIDTH
MIX_PAD_WIDTH = A_WIDTH + B_PAD_WIDTH + C_WIDTH

REL_BUCKETS = 32
REL_MAX_DIST = 2048

N_EXPERTS = 256
TOP_K = 8
N_GROUPS = 8
TOPK_GROUPS = 4
EXPERT_FF = 256
SHARED_FF = 256
ROUTED_SCALE = 2.5
EXPERT_BLOCK = 512
EXPERT_SLOTS = 3
COMBINE_PARTS = 4
COMBINE_ROWS = 256

DEEPNORM_ALPHA = (2 * DEPTH) ** 0.25
LN_EPS = 1e-5


def _cparams(*sem):
    return pltpu.CompilerParams(dimension_semantics=sem, vmem_limit_bytes=VMEM_LIMIT)


def _prep_w_in(w_in):
    d = w_in.shape[0]
    split_at = [int(i) for i in np.cumsum(IN_SPLITS)[:-1]]
    aq, ak, av, bq, bf, bi, bg, cq, ck, cv = jnp.split(w_in, split_at, axis=-1)
    pad_v = lambda w: jnp.pad(w.reshape(d, B_HEADS, B_VAL_DIM), ((0, 0), (0, 0), (0, B_VAL_PAD - B_VAL_DIM))).reshape(d, B_PAD_WIDTH)
    rep = lambda w: jnp.repeat(w.reshape(d, C_KV_HEADS, HEAD_DIM), C_HEADS // C_KV_HEADS, axis=1).reshape(d, C_WIDTH)
    cols = [aq, ak, av, bq, bf, pad_v(bi), pad_v(bg), cq, rep(ck), rep(cv)]
    return jnp.concatenate(cols, axis=-1).astype(BF16)


def _prep_w_out(w_out):
    d = w_out.shape[1]
    wa = w_out[:A_WIDTH]
    wb = w_out[A_WIDTH:A_WIDTH + B_WIDTH].reshape(B_HEADS, B_VAL_DIM, d)
    wb = jnp.pad(wb, ((0, 0), (0, B_VAL_PAD - B_VAL_DIM), (0, 0))).reshape(B_PAD_WIDTH, d)
    wc = w_out[A_WIDTH + B_WIDTH:]
    return jnp.concatenate([wa, wb, wc], axis=0).astype(BF16)


def _pad_heads_vec(v):
    return jnp.pad(v.reshape(B_HEADS, B_VAL_DIM), ((0, 0), (0, B_VAL_PAD - B_VAL_DIM))).reshape(1, B_PAD_WIDTH)


def _rel_bucket(dist):
    max_exact = REL_BUCKETS // 2
    d = jnp.maximum(dist, 0)
    log_ratio = jnp.log(jnp.maximum(d, max_exact).astype(F32) / max_exact) / math.log(REL_MAX_DIST / max_exact)
    large = jnp.minimum(max_exact + (log_ratio * (REL_BUCKETS - max_exact)).astype(I32), REL_BUCKETS - 1)
    return jnp.where(d < max_exact, d, large)


def _band_bias(rel_table, r, head_lo, head_hi):
    dist = jnp.arange(BAND)[:, None] + BAND - jnp.arange(2 * BAND)[None, :]
    onehot = jax.nn.one_hot(_rel_bucket(dist * r), REL_BUCKETS, dtype=F32)
    return jnp.einsum("qkb,bh->hqk", onehot, rel_table[:, head_lo:head_hi], precision=lax.Precision.HIGHEST)


def _hgrn_sum_matrix():
    c = HG_CHUNK
    t = np.arange(c)[:, None]
    u = np.arange(c)[None, :]
    blocks = [(u <= t), (u > t)]
    m = c
    while m >= 2:
        mid = (t // m) * m + m // 2
        second = t >= mid
        blocks.append(np.where(second, (u >= mid) & (u <= t), (u > t) & (u < mid)))
        m //= 2
    return np.concatenate(blocks, axis=0).astype(np.float32)


HG_LEVELS = int(math.log2(HG_CHUNK))


def _in_proj_body(x_ref, w_ref, *rest):
    n_res = len(IN_PROJ_DILATIONS)
    a_ref, res_refs, (b_ref, c_ref, slabs) = rest[0], rest[1:1 + n_res], rest[1 + n_res:]
    tm = x_ref.shape[0]
    xb = x_ref[...].astype(BF16)
    a = jnp.dot(xb, w_ref[:, :A_COLS], preferred_element_type=F32)
    a_ref[...] = a.astype(BF16)
    for j in range(A_COLS // LANES):
        slabs[j] = a[:, j * LANES:(j + 1) * LANES]
    for r, ref in zip(IN_PROJ_DILATIONS, res_refs):
        for p in range(r):
            for j in range(A_COLS // LANES):
                ref[0, p, :, j * LANES:(j + 1) * LANES] = slabs[j, pl.ds(p, tm // r, stride=r), :].astype(BF16)
    for j in range(B_COLS // B_KEY_WIDTH):
        lo = A_COLS + j * B_KEY_WIDTH
        b_ref[:, j * B_KEY_WIDTH:(j + 1) * B_KEY_WIDTH] = jnp.dot(
            xb, w_ref[:, lo:lo + B_KEY_WIDTH], preferred_element_type=F32)
    c_ref[...] = jnp.dot(xb, w_ref[:, A_COLS + B_COLS:], preferred_element_type=F32).astype(BF16)


def _in_proj(x2d, w_p, nb, s):
    t, d = x2d.shape
    tm = IN_PROJ_ROWS
    n = w_p.shape[1]
    tiles = s // tm
    res_specs = [pl.BlockSpec((1, r, tm // r, A_COLS), lambda i: (i // tiles, 0, i % tiles, 0))
                 for r in IN_PROJ_DILATIONS]
    res_shapes = [jax.ShapeDtypeStruct((nb, r, s // r, A_COLS), BF16) for r in IN_PROJ_DILATIONS]
    outs = pl.pallas_call(
        _in_proj_body,
        grid=(t // tm,),
        in_specs=[pl.BlockSpec((tm, d), lambda i: (i, 0)),
                  pl.BlockSpec((d, n), lambda i: (0, 0))],
        out_specs=[pl.BlockSpec((tm, A_COLS), lambda i: (i, 0))] + res_specs + [
                   pl.BlockSpec((tm, B_COLS), lambda i: (i, 0)),
                   pl.BlockSpec((tm, C_COLS), lambda i: (i, 0))],
        out_shape=[jax.ShapeDtypeStruct((t, A_COLS), BF16)] + res_shapes + [
                   jax.ShapeDtypeStruct((t, B_COLS), F32),
                   jax.ShapeDtypeStruct((t, C_COLS), BF16)],
        scratch_shapes=[pltpu.VMEM((A_COLS // LANES, tm, LANES), F32)],
        compiler_params=_cparams("arbitrary"),
        name="in_proj",
    )(x2d, w_p)
    return outs[0], list(outs[1:-2]), outs[-2], outs[-1]


def _band_attn_body(*refs, width, max_dist, has_sink, want_lse):
    q_ref, kp_ref, kc_ref, vp_ref, vc_ref, bias_ref = refs[:6]
    rest = refs[6:]
    if has_sink:
        sink_ref, rest = rest[0], rest[1:]
    o_ref = rest[0]
    lse_ref = rest[1] if want_lse else None

    first_tile = pl.program_id(2) == 0
    row = lax.broadcasted_iota(I32, (BAND, 2 * BAND), 0)
    col = lax.broadcasted_iota(I32, (BAND, 2 * BAND), 1)
    dist = row + BAND - col
    in_band = (dist >= 0) & (dist <= max_dist)
    first_mask = in_band & ((col >= BAND) | jnp.logical_not(first_tile))
    low_half = lax.broadcasted_iota(I32, (BAND, LANES), 1) < HEAD_DIM
    scale = HEAD_DIM ** -0.5

    for qb in range(q_ref.shape[0] // BAND):
        rows = slice(qb * BAND, (qb + 1) * BAND)
        mask = first_mask if qb == 0 else in_band
        for tile in range(width // LANES):
            sl = slice(tile * LANES, (tile + 1) * LANES)
            q2 = q_ref[rows, sl]
            if qb == 0:
                k2 = jnp.concatenate([kp_ref[:, sl], kc_ref[:BAND, sl]], axis=0)
                v2 = jnp.concatenate([vp_ref[:, sl], vc_ref[:BAND, sl]], axis=0)
            else:
                k2 = kc_ref[(qb - 1) * BAND:(qb + 1) * BAND, sl]
                v2 = vc_ref[(qb - 1) * BAND:(qb + 1) * BAND, sl]
            outs, lses = [], []
            for half in range(2):
                h = 2 * tile + half
                qm = jnp.where(low_half if half == 0 else jnp.logical_not(low_half), q2, jnp.zeros_like(q2))
                s = lax.dot_general(qm, k2, (((1,), (1,)), ((), ())), preferred_element_type=F32)
                s = s * scale + bias_ref[h]
                s = jnp.where(mask, s, MASK_VALUE)
                m = jnp.max(s, axis=-1, keepdims=True)
                if has_sink:
                    sink = sink_ref[h]
                    m = jnp.maximum(m, sink)
                p = jnp.exp(s - m)
                den = jnp.sum(p, axis=-1, keepdims=True)
                if has_sink:
                    den = den + jnp.exp(sink - m)
                pv = jnp.dot(p.astype(BF16), v2, preferred_element_type=F32)
                outs.append(pv / den)
                if want_lse:
                    lses.append(jnp.broadcast_to(m + jnp.log(den), (BAND, LANES)))
            o_ref[rows, sl] = jnp.where(low_half, outs[0], outs[1])
            if want_lse:
                lse_ref[rows, sl] = jnp.where(low_half, lses[0], lses[1])


def _band_attn(src, bias, *, width, max_dist, sinks=None, want_lse):
    nb, r, length, _ = src.shape
    qt = min(BAND_Q_TILE, length)
    bands = qt // BAND
    heads = width // HEAD_DIM
    has_sink = sinks is not None

    def cur(off):
        return pl.BlockSpec((None, None, qt, width), lambda b, p, i: (b, p, i, off))

    def prev(off):
        return pl.BlockSpec((None, None, BAND, width), lambda b, p, i: (b, p, jnp.maximum(i * bands - 1, 0), off))

    in_specs = [cur(0), prev(1), cur(1), prev(2), cur(2),
                pl.BlockSpec((heads, BAND, 2 * BAND), lambda b, p, i: (0, 0, 0))]
    args = [src, src, src, src, src, bias]
    if has_sink:
        in_specs.append(pl.BlockSpec(memory_space=pltpu.SMEM))
        args.append(sinks)
    out_spec = pl.BlockSpec((None, None, qt, width), lambda b, p, i: (b, p, i, 0))
    out_sds = jax.ShapeDtypeStruct((nb, r, length, width), F32)
    body = functools.partial(_band_attn_body, width=width, max_dist=max_dist, has_sink=has_sink, want_lse=want_lse)
    return pl.pallas_call(
        body,
        grid=(nb, r, length // qt),
        in_specs=in_specs,
        out_specs=[out_spec, out_spec] if want_lse else out_spec,
        out_shape=[out_sds, out_sds] if want_lse else out_sds,
        compiler_params=_cparams("arbitrary", "arbitrary", "arbitrary"),
        name="band_attn_r%d_w%d" % (r, width),
    )(*args)


def _hgrn_body(b_ref, lb_ref, ng_ref, w_ref, o_ref, state_ref):
    c = HG_CHUNK
    kd = B_KEY_DIM

    @pl.when(pl.program_id(1) == 0)
    def _():
        state_ref[...] = jnp.zeros_like(state_ref)

    trow = lax.broadcasted_iota(I32, (c, 1), 0)
    ti = lax.broadcasted_iota(I32, (c, c), 0)
    si = lax.broadcasted_iota(I32, (c, c), 1)
    nt = (((1,), (1,)), ((), ()))
    w_all = w_ref[...]

    states = [state_ref[h] for h in range(B_HEADS)]
    for chunk, h in [(ci, hi) for ci in range(b_ref.shape[1] // c) for hi in range(B_HEADS)]:
        rows = slice(chunk * c, (chunk + 1) * c)
        ks = slice(h * kd, (h + 1) * kd)
        q = b_ref[0, rows, ks]
        f = b_ref[0, rows, B_KEY_WIDTH + h * kd:B_KEY_WIDTH + (h + 1) * kd]
        inp = b_ref[0, rows, 2 * B_KEY_WIDTH + h * kd:2 * B_KEY_WIDTH + (h + 1) * kd]
        gate = b_ref[0, rows, 3 * B_KEY_WIDTH + h * kd:3 * B_KEY_WIDTH + (h + 1) * kd]
        lb = lb_ref[:, ks]

        log_f = jnp.log(lb + (1.0 - lb) * jax.nn.sigmoid(f))
        key = (1.0 - lb) * jax.nn.sigmoid(-f)
        qs = q * jax.nn.sigmoid(q)

        g_hi = log_f.astype(BF16)
        g_lo = (log_f - g_hi.astype(F32)).astype(BF16)
        e2 = jnp.dot(w_all, jnp.concatenate([g_hi, g_lo], axis=1), preferred_element_type=F32)
        e = e2[:, :kd] + e2[:, kd:]

        cum = e[0:c]
        q_dec = (qs * jnp.exp(cum)).astype(BF16)
        k_dec = (key * jnp.exp(e[c:2 * c])).astype(BF16)
        inp_b = inp.astype(BF16)

        st = states[h]
        inter = lax.dot_general(q_dec, st.astype(BF16), nt, preferred_element_type=F32)

        scores = jnp.where(ti == si,
                           lax.dot_general(qs.astype(BF16), key.astype(BF16), nt, preferred_element_type=F32), 0.0)
        m = c
        for lvl in range(HG_LEVELS):
            el = jnp.exp(e[(2 + lvl) * c:(3 + lvl) * c])
            second = (trow % m) >= (m // 2)
            ql = jnp.where(second, qs * el, 0.0).astype(BF16)
            kl = jnp.where(second, 0.0, key * el).astype(BF16)
            sl = lax.dot_general(ql, kl, nt, preferred_element_type=F32)
            if m < c:
                sl = jnp.where((ti // m) == (si // m), sl, 0.0)
            scores = scores + sl
            m //= 2
        intra = jnp.dot(scores.astype(BF16), inp_b, preferred_element_type=F32)

        new_st = st * jnp.exp(cum[c - 1:c]) + lax.dot_general(
            inp_b, k_dec, (((0,), (0,)), ((), ())), preferred_element_type=F32)
        states[h] = new_st

        o = inter + intra
        ms = jnp.sum(o * o, axis=-1, keepdims=True) * (1.0 / B_VAL_DIM)
        o = o * lax.rsqrt(ms + 1e-6)
        o_ref[0, rows, ks] = o * ng_ref[:, ks] * (gate * jax.nn.sigmoid(gate))

    for h in range(B_HEADS):
        state_ref[h] = states[h]


def _hgrn(b_all, lower_bound, norm_g_pad, w_sum):
    nb, s, _ = b_all.shape
    c = HG_STEP_ROWS
    return pl.pallas_call(
        _hgrn_body,
        grid=(nb, s // c),
        in_specs=[pl.BlockSpec((1, c, B_COLS), lambda b, i: (b, i, 0)),
                  pl.BlockSpec((1, B_KEY_WIDTH), lambda b, i: (0, 0)),
                  pl.BlockSpec((1, B_PAD_WIDTH), lambda b, i: (0, 0)),
                  pl.BlockSpec(w_sum.shape, lambda b, i: (0, 0))],
        out_specs=pl.BlockSpec((1, c, B_PAD_WIDTH), lambda b, i: (b, i, 0)),
        out_shape=jax.ShapeDtypeStruct((nb, s, B_PAD_WIDTH), F32),
        scratch_shapes=[pltpu.VMEM((B_HEADS, B_VAL_PAD, B_KEY_DIM), F32)],
        compiler_params=_cparams("arbitrary", "arbitrary"),
        name="hgrn2",
    )(b_all, lower_bound, norm_g_pad, w_sum)


def _layer_norm_rows(z, g, b):
    mu = jnp.mean(z, axis=-1, keepdims=True)
    zc = z - mu
    var = jnp.mean(zc * zc, axis=-1, keepdims=True)
    return zc * lax.rsqrt(var + LN_EPS) * g + b


def _mix_out_body(*refs):
    n_pat = len(A_PATTERNS)
    o_refs, l_refs = refs[:n_pat], refs[n_pat:2 * n_pat]
    ob, oc, x_ref, w_ref, g_ref, b_ref, out_ref = refs[2 * n_pat:2 * n_pat + 7]
    scratch = refs[2 * n_pat + 7:]
    tm = x_ref.shape[0]
    n_slabs = A_WIDTH // LANES

    def token_order(ref, r, scr):
        if r == 1:
            return [ref[0, 0, :, j * LANES:(j + 1) * LANES] for j in range(n_slabs)]
        for p in range(r):
            for j in range(n_slabs):
                scr[j, pl.ds(p, tm // r, stride=r), :] = ref[0, p, :, j * LANES:(j + 1) * LANES]
        return [scr[j] for j in range(n_slabs)]

    scr_iter = iter(scratch)
    dil = [r for _, r in A_PATTERNS]
    o_slabs = [token_order(ref, r, None if r == 1 else next(scr_iter)) for ref, r in zip(o_refs, dil)]
    l_slabs = [token_order(ref, r, None if r == 1 else next(scr_iter)) for ref, r in zip(l_refs, dil)]

    merged = []
    for j in range(n_slabs):
        ls = [l[j] for l in l_slabs]
        m = functools.reduce(jnp.maximum, ls)
        ws = [jnp.exp(l - m) for l in ls]
        num = functools.reduce(lambda a, b: a + b, [w * o[j] for w, o in zip(ws, o_slabs)])
        merged.append(num / functools.reduce(lambda a, b: a + b, ws))
    cat = jnp.concatenate(merged + [ob[...], oc[...]], axis=1).astype(BF16)
    y = jnp.dot(cat, w_ref[...], preferred_element_type=F32)
    z = DEEPNORM_ALPHA * x_ref[...] + y
    out_ref[...] = _layer_norm_rows(z, g_ref[...], b_ref[...])


def _mix_out(o_list, l_list, ob, oc, x2d, w_out_p, ln_g, ln_b):
    t, d = x2d.shape
    tm = MIX_OUT_ROWS
    tiles = o_list[0].shape[1] * o_list[0].shape[2] // tm
    row = lambda w: pl.BlockSpec((tm, w), lambda i: (i, 0))
    full = lambda a: pl.BlockSpec(a.shape, lambda i: (0, 0))
    res = lambda a: pl.BlockSpec((1, a.shape[1], tm // a.shape[1], A_WIDTH), lambda i: (i // tiles, 0, i % tiles, 0))
    n_scratch = 2 * sum(1 for _, r in A_PATTERNS if r > 1)
    return pl.pallas_call(
        _mix_out_body,
        grid=(t // tm,),
        in_specs=[res(a) for a in o_list + l_list] + [row(B_PAD_WIDTH), row(C_WIDTH), row(d), full(w_out_p),
                                                      full(ln_g), full(ln_b)],
        out_specs=row(d),
        out_shape=jax.ShapeDtypeStruct((t, d), F32),
        scratch_shapes=[pltpu.VMEM((A_WIDTH // LANES, tm, LANES), F32)] * n_scratch,
        compiler_params=_cparams("arbitrary"),
        name="mix_out_ln",
    )(*o_list, *l_list, ob, oc, x2d, w_out_p, ln_g, ln_b)


def _router_body(h_ref, rw_ref, bias_ref, tri_ref, e_ref, rank_ref, gate_ref, cnt_ref, hp_ref, carry_ref):
    tn = h_ref.shape[0]
    per_group = N_EXPERTS // N_GROUPS
    neg_inf = -jnp.inf

    @pl.when(pl.program_id(0) == 0)
    def _():
        carry_ref[...] = jnp.zeros_like(carry_ref)

    hp_ref[...] = _pack_bf16_pairs(h_ref[...])

    logits = lax.dot_general(rw_ref[...], h_ref[...], (((1,), (1,)), ((), ())),
                             precision=lax.Precision.HIGHEST, preferred_element_type=F32)
    scores = jax.nn.sigmoid(logits)
    choice = scores + bias_ref[...]

    def first_max(vals, idx, sentinel):
        top = jnp.max(vals, axis=0, keepdims=True)
        return top, jnp.min(jnp.where(vals == top, idx, sentinel), axis=0, keepdims=True)

    li = lax.broadcasted_iota(I32, (per_group, tn), 0).astype(F32)
    group_rows = []
    for g in range(N_GROUPS):
        cg = choice[g * per_group:(g + 1) * per_group]
        m1, first = first_max(cg, li, float(per_group))
        m2 = jnp.max(jnp.where(li == first, neg_inf, cg), axis=0, keepdims=True)
        group_rows.append(m1 + m2)
    group_score = jnp.concatenate(group_rows, axis=0)

    gi = lax.broadcasted_iota(I32, (N_GROUPS, tn), 0).astype(F32)
    group_ok = jnp.zeros((N_GROUPS, tn), F32)
    cur = group_score
    for _ in range(TOPK_GROUPS):
        _, first = first_max(cur, gi, float(N_GROUPS))
        pick = gi == first
        group_ok = jnp.where(pick, 1.0, group_ok)
        cur = jnp.where(pick, neg_inf, cur)

    cur = jnp.concatenate(
        [jnp.where(group_ok[g:g + 1] > 0.0, choice[g * per_group:(g + 1) * per_group], MASK_VALUE)
         for g in range(N_GROUPS)], axis=0)
    ei = lax.broadcasted_iota(I32, (N_EXPERTS, tn), 0).astype(F32)
    chosen = jnp.zeros((N_EXPERTS, tn), F32)
    picks, gates = [], []
    for _ in range(TOP_K):
        _, idx = first_max(cur, ei, float(N_EXPERTS))
        pick = ei == idx
        picks.append(idx)
        gates.append(jnp.sum(jnp.where(pick, scores, 0.0), axis=0, keepdims=True))
        chosen = jnp.where(pick, 1.0, chosen)
        cur = jnp.where(pick, neg_inf, cur)

    gate = jnp.concatenate(gates, axis=0)
    gate_ref[...] = gate / jnp.sum(gate, axis=0, keepdims=True) * ROUTED_SCALE
    e_ref[...] = jnp.concatenate(picks, axis=0).astype(I32)

    before = jnp.dot(chosen.astype(BF16), tri_ref[...], preferred_element_type=F32) + carry_ref[...]
    ranks = [jnp.sum(jnp.where(ei == idx, before, 0.0), axis=0, keepdims=True) for idx in picks]
    rank_ref[...] = jnp.concatenate(ranks, axis=0).astype(I32)
    carry = carry_ref[...] + jnp.sum(chosen, axis=1, keepdims=True)
    carry_ref[...] = carry
    cnt_ref[...] = carry.astype(I32)


def _router(h2d, rw_t, bias_col):
    t, d = h2d.shape
    tn = 256
    tri = jnp.asarray(np.triu(np.ones((tn, tn), np.float32), k=1), dtype=BF16)
    tok = lambda: pl.BlockSpec((TOP_K, tn), lambda i: (0, i))
    return pl.pallas_call(
        _router_body,
        grid=(t // tn,),
        in_specs=[pl.BlockSpec((tn, d), lambda i: (i, 0)),
                  pl.BlockSpec((N_EXPERTS, d), lambda i: (0, 0)),
                  pl.BlockSpec((N_EXPERTS, 1), lambda i: (0, 0)),
                  pl.BlockSpec((tn, tn), lambda i: (0, 0))],
        out_specs=[tok(), tok(), tok(), pl.BlockSpec((N_EXPERTS, 1), lambda i: (0, 0)),
                   pl.BlockSpec((tn, d // 2), lambda i: (i, 0))],
        out_shape=[jax.ShapeDtypeStruct((TOP_K, t), I32),
                   jax.ShapeDtypeStruct((TOP_K, t), I32),
                   jax.ShapeDtypeStruct((TOP_K, t), F32),
                   jax.ShapeDtypeStruct((N_EXPERTS, 1), I32),
                   jax.ShapeDtypeStruct((t, d // 2), U32)],
        scratch_shapes=[pltpu.VMEM((N_EXPERTS, 1), F32)],
        compiler_params=_cparams("arbitrary"),
        name="moe_router",
    )(h2d, rw_t, bias_col, tri)


def _pack_bf16_pairs(x):
    w = x.shape[1] // 2
    hi = lax.bitcast_convert_type(x[:, :w].astype(BF16).astype(F32), U32)
    lo = lax.bitcast_convert_type(x[:, w:].astype(BF16).astype(F32), U32)
    return hi | (lo >> 16)


def _unpack_bf16_pairs(p):
    hi = lax.bitcast_convert_type(p & jnp.uint32(0xFFFF0000), F32)
    lo = lax.bitcast_convert_type(p << 16, F32)
    return hi, lo


def _dest_body(e_ref, rank_ref, offs_ref, dest_ref):
    tn = e_ref.shape[1]
    ei = lax.broadcasted_iota(I32, (N_EXPERTS, tn), 0)
    offs = offs_ref[...]
    rows = [jnp.sum(jnp.where(ei == e_ref[k:k + 1, :], offs, 0.0), axis=0, keepdims=True) for k in range(TOP_K)]
    dest_ref[...] = jnp.concatenate(rows, axis=0).astype(I32) + rank_ref[...]


def _dest_rows(e_t, rank_t, offsets):
    t = e_t.shape[1]
    tn = 512
    tok = pl.BlockSpec((TOP_K, tn), lambda i: (0, i))
    return pl.pallas_call(
        _dest_body,
        grid=(t // tn,),
        in_specs=[tok, tok, pl.BlockSpec((N_EXPERTS, 1), lambda i: (0, 0))],
        out_specs=tok,
        out_shape=jax.ShapeDtypeStruct((TOP_K, t), I32),
        compiler_params=_cparams("arbitrary"),
        name="moe_dest",
    )(e_t, rank_t, offsets.astype(F32).reshape(N_EXPERTS, 1))


def _expert_body(offs_ref, nblk_ref, cnt_ref, layer_ref, xs_hbm, wg_ref, wu_ref, wd_ref, ys_hbm,
                 xbuf, ybuf, wg_b, wu_b, wd_b, sem_in, sem_out):
    e = pl.program_id(0)
    slots, bm = xbuf.shape[:2]
    ahead = slots - 1
    n = nblk_ref[e]
    first = offs_ref[e] // bm
    total = (offs_ref[N_EXPERTS - 1] // bm) + nblk_ref[N_EXPERTS - 1]

    def rows(g):
        return pl.ds(pl.multiple_of(g * bm, bm), bm)

    def x_copy(g, slot):
        return pltpu.make_async_copy(xs_hbm.at[rows(g)], xbuf.at[slot], sem_in.at[slot])

    def y_copy(g, slot):
        return pltpu.make_async_copy(ybuf.at[slot], ys_hbm.at[rows(g)], sem_out.at[slot])

    @pl.when(e == 0)
    def _():
        for g0 in range(ahead):
            @pl.when(g0 < total)
            def _():
                x_copy(g0, g0).start()

    wg_b[...] = wg_ref[0, 0].astype(BF16)
    wu_b[...] = wu_ref[0, 0].astype(BF16)
    wd_b[...] = wd_ref[0, 0].astype(BF16)

    def block(j, carry):
        g = first + j
        slot = g % slots
        x_copy(g, slot).wait()

        @pl.when(g + ahead < total)
        def _():
            x_copy(g + ahead, (g + ahead) % slots).start()

        @pl.when(g >= slots)
        def _():
            y_copy(g - slots, slot).wait()

        live = lax.broadcasted_iota(I32, (bm, 1), 0) < (cnt_ref[e] - j * bm)
        hi, lo = _unpack_bf16_pairs(jnp.where(live, xbuf[slot], jnp.uint32(0)))
        xb = jnp.concatenate([hi.astype(BF16), lo.astype(BF16)], axis=1)
        gate = jnp.dot(xb, wg_b[...], preferred_element_type=F32)
        up = jnp.dot(xb, wu_b[...], preferred_element_type=F32)
        hidden = (gate * jax.nn.sigmoid(gate) * up).astype(BF16)
        ybuf[slot] = _pack_bf16_pairs(jnp.dot(hidden, wd_b[...], preferred_element_type=F32))
        y_copy(g, slot).start()
        return carry

    lax.fori_loop(0, n, block, 0)

    @pl.when(e == N_EXPERTS - 1)
    def _():
        for back in range(1, slots + 1):
            @pl.when(total >= back)
            def _():
                y_copy(total - back, (total - back) % slots).wait()


def _experts(xs, offsets, n_blk, counts, layer, w_gate, w_up, w_down):
    n_rows, dp = xs.shape
    bm = EXPERT_BLOCK
    d, ff = w_gate.shape[-2:]
    w_spec = lambda a, b: pl.BlockSpec((1, 1, a, b), lambda e, of, nb, ct, ly: (ly[0], e, 0, 0))
    grid_spec = pltpu.PrefetchScalarGridSpec(
        num_scalar_prefetch=4,
        grid=(N_EXPERTS,),
        in_specs=[pl.BlockSpec(memory_space=pl.ANY), w_spec(d, ff), w_spec(d, ff), w_spec(ff, d)],
        out_specs=pl.BlockSpec(memory_space=pl.ANY),
        scratch_shapes=[pltpu.VMEM((EXPERT_SLOTS, bm, dp), U32), pltpu.VMEM((EXPERT_SLOTS, bm, dp), U32),
                        pltpu.VMEM((d, ff), BF16), pltpu.VMEM((d, ff), BF16), pltpu.VMEM((ff, d), BF16),
                        pltpu.SemaphoreType.DMA((EXPERT_SLOTS,)), pltpu.SemaphoreType.DMA((EXPERT_SLOTS,))],
    )
    return pl.pallas_call(
        _expert_body,
        grid_spec=grid_spec,
        out_shape=jax.ShapeDtypeStruct((n_rows, dp), U32),
        compiler_params=_cparams("arbitrary"),
        name="moe_experts",
    )(offsets, n_blk, counts, layer, xs, w_gate, w_up, w_down)


SC_GATHER_ROWS = 64


def _sc_gather_rows(table, idx):
    info = plsc.get_sparse_core_info()
    nc, ns = info.num_cores, info.num_subcores
    workers = nc * ns
    n = idx.shape[0]
    w = table.shape[1]
    ch = SC_GATHER_ROWS
    per_worker = n // workers
    steps = per_worker // ch
    assert per_worker * workers == n and steps * ch == per_worker and steps % 2 == 0
    mesh = plsc.VectorSubcoreMesh(core_axis_name="c", subcore_axis_name="s")

    @functools.partial(
        pl.kernel, mesh=mesh,
        out_type=jax.ShapeDtypeStruct((n, w), table.dtype),
        scratch_types=[pltpu.VMEM((steps, ch), I32), pltpu.VMEM((2, ch, w), table.dtype),
                       pltpu.SemaphoreType.DMA((2,)), pltpu.SemaphoreType.DMA((2,))],
    )
    def gather_kernel(table_hbm, idx_hbm, out_hbm, idx_v, rows_v, gsem, wsem):
        wid = lax.axis_index("s") * nc + lax.axis_index("c")
        base = wid * per_worker
        pltpu.sync_copy(idx_hbm.at[wid], idx_v)

        def gather(i, slot):
            return pltpu.make_async_copy(table_hbm.at[idx_v.at[i]], rows_v.at[slot], gsem.at[slot])

        def write(i, slot):
            off = pl.multiple_of(base + i * ch, ch)
            return pltpu.make_async_copy(rows_v.at[slot], out_hbm.at[pl.ds(off, ch)], wsem.at[slot])

        gather(0, 0).start()

        @pl.loop(0, steps, step=2)
        def _(i):
            gather(i + 1, 1).start()
            gather(i, 0).wait()
            write(i, 0).start()
            write(i, 0).wait()

            @pl.when(i + 2 < steps)
            def _():
                gather(i + 2, 0).start()

            gather(i + 1, 1).wait()
            write(i + 1, 1).start()
            write(i + 1, 1).wait()

    return gather_kernel(table, idx.reshape(workers, steps, ch))


def _sc_scatter_rows(rows, idx, n_out):
    info = plsc.get_sparse_core_info()
    nc, ns = info.num_cores, info.num_subcores
    workers = nc * ns
    kk, t = idx.shape
    w = rows.shape[1]
    ch = SC_GATHER_ROWS
    per_worker = t // workers
    steps = per_worker // ch
    assert per_worker * workers == t and steps * ch == per_worker and steps % 2 == 0
    idx_w = idx.reshape(kk, workers, steps, ch).transpose(1, 0, 2, 3).reshape(workers, kk * steps, ch)
    mesh = plsc.VectorSubcoreMesh(core_axis_name="c", subcore_axis_name="s")

    @functools.partial(
        pl.kernel, mesh=mesh,
        out_type=jax.ShapeDtypeStruct((n_out, w), rows.dtype),
        scratch_types=[pltpu.VMEM((kk * steps, ch), I32), pltpu.VMEM((2, ch, w), rows.dtype),
                       pltpu.SemaphoreType.DMA((2,)), pltpu.SemaphoreType.DMA((2,))],
    )
    def scatter_kernel(rows_hbm, idx_hbm, out_hbm, idx_v, rows_v, rsem, ssem):
        wid = lax.axis_index("s") * nc + lax.axis_index("c")
        base = wid * per_worker
        pltpu.sync_copy(idx_hbm.at[wid], idx_v)

        def read(i, slot):
            off = pl.multiple_of(base + i * ch, ch)
            return pltpu.make_async_copy(rows_hbm.at[pl.ds(off, ch)], rows_v.at[slot], rsem.at[slot])

        def scatter(i, k, slot):
            return pltpu.make_async_copy(rows_v.at[slot], out_hbm.at[idx_v.at[k * steps + i]], ssem.at[slot])

        def scatter_all(i, slot):
            for k in range(kk):
                scatter(i, k, slot).start()
            for k in range(kk):
                scatter(i, k, slot).wait()

        read(0, 0).start()

        @pl.loop(0, steps, step=2)
        def _(i):
            read(i + 1, 1).start()
            read(i, 0).wait()
            scatter_all(i, 0)

            @pl.when(i + 2 < steps)
            def _():
                read(i + 2, 0).start()

            read(i + 1, 1).wait()
            scatter_all(i + 1, 1)

    return scatter_kernel(rows, idx_w)


def _combine_streamed_body(y_ref, gate_ref, h_ref, sg_ref, su_ref, sd_ref, g_ref, b_ref, *rest):
    out_ref = rest[-1]
    half = h_ref.shape[1] // 2
    h = h_ref[...]
    hb = h.astype(BF16)
    sg = jnp.dot(hb, sg_ref[...], preferred_element_type=F32)
    su = jnp.dot(hb, su_ref[...], preferred_element_type=F32)
    shared = jnp.dot((sg * jax.nn.sigmoid(sg) * su).astype(BF16), sd_ref[...], preferred_element_type=F32)
    z = DEEPNORM_ALPHA * h + shared
    z_hi, z_lo = z[:, :half], z[:, half:]
    for k in range(TOP_K):
        y_hi, y_lo = _unpack_bf16_pairs(y_ref[k])
        gate = gate_ref[:, k:k + 1]
        z_hi = z_hi + gate * y_hi
        z_lo = z_lo + gate * y_lo
    out_ref[...] = _layer_norm_rows(jnp.concatenate([z_hi, z_lo], axis=1), g_ref[...], b_ref[...])


def _combine_streamed(y_part, gate_tk, h2d, sh_gate, sh_up, sh_down, ln_g, ln_b, part, partial_out):
    t, d = h2d.shape
    tt = COMBINE_ROWS
    steps = y_part.shape[1] // tt
    first = part * steps
    full = lambda a: pl.BlockSpec(a.shape, lambda i: (0, 0))
    in_specs = [pl.BlockSpec((TOP_K, tt, d // 2), lambda i: (0, i, 0)),
                pl.BlockSpec((tt, TOP_K), lambda i: (first + i, 0)),
                pl.BlockSpec((tt, d), lambda i: (first + i, 0)),
                full(sh_gate), full(sh_up), full(sh_down), full(ln_g), full(ln_b)]
    args = [y_part, gate_tk, h2d, sh_gate, sh_up, sh_down, ln_g, ln_b]
    aliases = {}
    if partial_out is not None:
        in_specs.append(pl.BlockSpec(memory_space=pl.ANY))
        args.append(partial_out)
        aliases = {len(args) - 1: 0}
    return pl.pallas_call(
        _combine_streamed_body,
        grid=(steps,),
        in_specs=in_specs,
        out_specs=pl.BlockSpec((tt, d), lambda i: (first + i, 0)),
        out_shape=jax.ShapeDtypeStruct((t, d), F32),
        input_output_aliases=aliases,
        compiler_params=_cparams("arbitrary"),
        name="moe_combine_streamed_ln",
    )(*args)


def _mixer_sublayer(h2d, nb, s, w_in_p, w_out_p, a_biases, c_bias, lower_bound, norm_g_pad, sinks, w_sum, ln_g, ln_b):
    a_qkv, a_residue, b_all, c_qkv = _in_proj(h2d, w_in_p, nb, s)
    a_by_dilation = {1: a_qkv.reshape(nb, 1, s, A_COLS), **dict(zip(IN_PROJ_DILATIONS, a_residue))}
    o_list, l_list = [], []
    for (window, r), bias in zip(A_PATTERNS, a_biases):
        o, lse = _band_attn(a_by_dilation[r], bias, width=A_WIDTH, max_dist=window // r, want_lse=True)
        o_list.append(o)
        l_list.append(lse)
    oc = _band_attn(c_qkv.reshape(nb, 1, s, C_COLS), c_bias, width=C_WIDTH,
                    max_dist=C_WINDOW - 1, sinks=sinks, want_lse=False).reshape(nb * s, C_WIDTH)
    ob = _hgrn(b_all.reshape(nb, s, B_COLS), lower_bound, norm_g_pad, w_sum).reshape(nb * s, B_PAD_WIDTH)
    return _mix_out(o_list, l_list, ob, oc, h2d, w_out_p, ln_g, ln_b)


def _moe_sublayer(h2d, layer, router_w, router_bias, w_gate, w_up, w_down, sh_gate, sh_up, sh_down, ln_g, ln_b):
    t, d = h2d.shape
    bm = EXPERT_BLOCK
    e_t, rank_t, gate_t, counts, h_packed = _router(h2d, router_w.T.astype(F32),
                                                    router_bias.astype(F32).reshape(N_EXPERTS, 1))

    counts = counts.reshape(N_EXPERTS)
    padded = (counts + bm - 1) // bm * bm
    pad_end = jnp.cumsum(padded).astype(I32)
    offsets = pad_end - padded
    n_blocks = -(-(t * TOP_K + N_EXPERTS * (bm - 1)) // bm)

    dest_t = _dest_rows(e_t, rank_t, offsets)
    xs = _sc_scatter_rows(h_packed, dest_t, n_blocks * bm)
    ys = _experts(xs, offsets, (padded // bm).astype(I32), counts, jnp.full((1,), layer, I32), w_gate, w_up, w_down)
    tp = t // COMBINE_PARTS
    gate_tk = gate_t.T
    shared_w = (sh_gate.astype(BF16), sh_up.astype(BF16), sh_down.astype(BF16))
    out = None
    for part in range(COMBINE_PARTS):
        idx = dest_t[:, part * tp:(part + 1) * tp].reshape(TOP_K * tp)
        y_part = _sc_gather_rows(ys, idx).reshape(TOP_K, tp, d // 2)
        out = _combine_streamed(y_part, gate_tk, h2d, *shared_w, ln_g, ln_b, part, out)
    return out


def kernel(x, w_in, w_out, rel_bias_table, lower_bound_logits, hgrn_norm_g, attn_sinks, ln1_g, ln1_b, router_w, router_bias, expert_w_gate, expert_w_up, expert_w_down, shared_w_gate, shared_w_up, shared_w_down, ln2_g, ln2_b):
    nb, s, d = x.shape
    depth = w_in.shape[0]
    lb_probs = jax.nn.softmax(lower_bound_logits.astype(F32), axis=0)
    lower_bounds = jnp.cumsum(lb_probs, axis=0) - lb_probs[0]
    rel_table = rel_bias_table.astype(F32)
    a_biases = [_band_bias(rel_table, r, 0, A_HEADS) for _, r in A_PATTERNS]
    c_bias = _band_bias(rel_table, 1, A_HEADS, A_HEADS + C_HEADS)
    w_sum = jnp.asarray(_hgrn_sum_matrix(), dtype=BF16)
    row = lambda v: v.astype(F32).reshape(1, -1)

    h = x.astype(F32).reshape(nb * s, d)
    for l in range(depth):
        h = _mixer_sublayer(h, nb, s, _prep_w_in(w_in[l]), _prep_w_out(w_out[l].astype(F32)), a_biases, c_bias,
                            lower_bounds[l].reshape(1, B_KEY_WIDTH), _pad_heads_vec(hgrn_norm_g[l].astype(F32)),
                            attn_sinks[l].astype(F32), w_sum, row(ln1_g[l]), row(ln1_b[l]))
        h = _moe_sublayer(h, l, router_w[l], router_bias[l], expert_w_gate, expert_w_up, expert_w_down,
                          shared_w_gate[l], shared_w_up[l], shared_w_down[l], row(ln2_g[l]), row(ln2_b[l]))
    return h.reshape(nb, s, d).astype(x.dtype)
```

```python
import functools
import math

import numpy as np
import jax
import jax.numpy as jnp
from jax import lax
from jax.experimental import pallas as pl
from jax.experimental.pallas import tpu as pltpu
from jax.experimental.pallas import tpu_sc as plsc

F32 = jnp.float32
BF16 = jnp.bfloat16
I32 = jnp.int32
U32 = jnp.uint32

LANES = 128
SUBLANES = 8
VMEM_LIMIT = 56 * 1024 * 1024

D_MODEL = 1024
DEPTH = 2
HEAD_DIM = 64
BAND = 128
BAND_Q_TILE = 1024
MASK_VALUE = -1e30

A_HEADS = 6
A_PATTERNS = ((128, 1), (512, 4), (2048, 16))
IN_PROJ_DILATIONS = tuple(r for _, r in A_PATTERNS if r > 1)
IN_PROJ_ROWS = 512
MIX_OUT_ROWS = 512
B_HEADS = 4
B_KEY_DIM = 128
B_VAL_DIM = 96
B_VAL_PAD = 128
HG_CHUNK = 64
HG_STEP_ROWS = 256
C_HEADS = 4
C_KV_HEADS = 2
C_WINDOW = 128

A_WIDTH = A_HEADS * HEAD_DIM
B_KEY_WIDTH = B_HEADS * B_KEY_DIM
B_WIDTH = B_HEADS * B_VAL_DIM
B_PAD_WIDTH = B_HEADS * B_VAL_PAD
C_WIDTH = C_HEADS * HEAD_DIM
C_KV_WIDTH = C_KV_HEADS * HEAD_DIM
IN_SPLITS = (A_WIDTH, A_WIDTH, A_WIDTH, B_KEY_WIDTH, B_KEY_WIDTH, B_WIDTH, B_WIDTH, C_WIDTH, C_KV_WIDTH, C_KV_WIDTH)

A_COLS = 3 * A_WIDTH
B_COLS = 4 * B_KEY_WIDTH
C_COLS = 3 * C_WIDTH
MIX_PAD_WIDTH = A_WIDTH + B_PAD_WIDTH + C_WIDTH

REL_BUCKETS = 32
REL_MAX_DIST = 2048

N_EXPERTS = 256
TOP_K = 8
N_GROUPS = 8
TOPK_GROUPS = 4
EXPERT_FF = 256
SHARED_FF = 256
ROUTED_SCALE = 2.5
EXPERT_BLOCK = 512
EXPERT_SLOTS = 3
COMBINE_PARTS = 4
COMBINE_ROWS = 512

DEEPNORM_ALPHA = (2 * DEPTH) ** 0.25
LN_EPS = 1e-5


def _cparams(*sem):
    return pltpu.CompilerParams(dimension_semantics=sem, vmem_limit_bytes=VMEM_LIMIT)


def _prep_w_in(w_in):
    d = w_in.shape[0]
    split_at = [int(i) for i in np.cumsum(IN_SPLITS)[:-1]]
    aq, ak, av, bq, bf, bi, bg, cq, ck, cv = jnp.split(w_in, split_at, axis=-1)
    pad_v = lambda w: jnp.pad(w.reshape(d, B_HEADS, B_VAL_DIM), ((0, 0), (0, 0), (0, B_VAL_PAD - B_VAL_DIM))).reshape(d, B_PAD_WIDTH)
    rep = lambda w: jnp.repeat(w.reshape(d, C_KV_HEADS, HEAD_DIM), C_HEADS // C_KV_HEADS, axis=1).reshape(d, C_WIDTH)
    cols = [aq, ak, av, bq, bf, pad_v(bi), pad_v(bg), cq, rep(ck), rep(cv)]
    return jnp.concatenate(cols, axis=-1).astype(BF16)


def _prep_w_out(w_out):
    d = w_out.shape[1]
    wa = w_out[:A_WIDTH]
    wb = w_out[A_WIDTH:A_WIDTH + B_WIDTH].reshape(B_HEADS, B_VAL_DIM, d)
    wb = jnp.pad(wb, ((0, 0), (0, B_VAL_PAD - B_VAL_DIM), (0, 0))).reshape(B_PAD_WIDTH, d)
    wc = w_out[A_WIDTH + B_WIDTH:]
    return jnp.concatenate([wa, wb, wc], axis=0).astype(BF16)


def _pad_heads_vec(v):
    return jnp.pad(v.reshape(B_HEADS, B_VAL_DIM), ((0, 0), (0, B_VAL_PAD - B_VAL_DIM))).reshape(1, B_PAD_WIDTH)


def _rel_bucket(dist):
    max_exact = REL_BUCKETS // 2
    d = jnp.maximum(dist, 0)
    log_ratio = jnp.log(jnp.maximum(d, max_exact).astype(F32) / max_exact) / math.log(REL_MAX_DIST / max_exact)
    large = jnp.minimum(max_exact + (log_ratio * (REL_BUCKETS - max_exact)).astype(I32), REL_BUCKETS - 1)
    return jnp.where(d < max_exact, d, large)


def _band_bias(rel_table, r, head_lo, head_hi):
    dist = jnp.arange(BAND)[:, None] + BAND - jnp.arange(2 * BAND)[None, :]
    onehot = jax.nn.one_hot(_rel_bucket(dist * r), REL_BUCKETS, dtype=F32)
    return jnp.einsum("qkb,bh->hqk", onehot, rel_table[:, head_lo:head_hi], precision=lax.Precision.HIGHEST)


def _hgrn_sum_matrix():
    c = HG_CHUNK
    t = np.arange(c)[:, None]
    u = np.arange(c)[None, :]
    blocks = [(u <= t), (u > t)]
    m = c
    while m >= 2:
        mid = (t // m) * m + m // 2
        second = t >= mid
        blocks.append(np.where(second, (u >= mid) & (u <= t), (u > t) & (u < mid)))
        m //= 2
    return np.concatenate(blocks, axis=0).astype(np.float32)


HG_LEVELS = int(math.log2(HG_CHUNK))


def _in_proj_body(x_ref, w_ref, *rest):
    n_res = len(IN_PROJ_DILATIONS)
    a_ref, res_refs, (b_ref, c_ref, slabs) = rest[0], rest[1:1 + n_res], rest[1 + n_res:]
    tm = x_ref.shape[0]
    xb = x_ref[...].astype(BF16)
    a = jnp.dot(xb, w_ref[:, :A_COLS], preferred_element_type=F32)
    a_ref[...] = a.astype(BF16)
    for j in range(A_COLS // LANES):
        slabs[j] = a[:, j * LANES:(j + 1) * LANES]
    for r, ref in zip(IN_PROJ_DILATIONS, res_refs):
        for p in range(r):
            for j in range(A_COLS // LANES):
                ref[0, p, :, j * LANES:(j + 1) * LANES] = slabs[j, pl.ds(p, tm // r, stride=r), :].astype(BF16)
    for j in range(B_COLS // B_KEY_WIDTH):
        lo = A_COLS + j * B_KEY_WIDTH
        b_ref[:, j * B_KEY_WIDTH:(j + 1) * B_KEY_WIDTH] = jnp.dot(
            xb, w_ref[:, lo:lo + B_KEY_WIDTH], preferred_element_type=F32)
    c_ref[...] = jnp.dot(xb, w_ref[:, A_COLS + B_COLS:], preferred_element_type=F32).astype(BF16)


def _in_proj(x2d, w_p, nb, s):
    t, d = x2d.shape
    tm = IN_PROJ_ROWS
    n = w_p.shape[1]
    tiles = s // tm
    res_specs = [pl.BlockSpec((1, r, tm // r, A_COLS), lambda i: (i // tiles, 0, i % tiles, 0))
                 for r in IN_PROJ_DILATIONS]
    res_shapes = [jax.ShapeDtypeStruct((nb, r, s // r, A_COLS), BF16) for r in IN_PROJ_DILATIONS]
    outs = pl.pallas_call(
        _in_proj_body,
        grid=(t // tm,),
        in_specs=[pl.BlockSpec((tm, d), lambda i: (i, 0)),
                  pl.BlockSpec((d, n), lambda i: (0, 0))],
        out_specs=[pl.BlockSpec((tm, A_COLS), lambda i: (i, 0))] + res_specs + [
                   pl.BlockSpec((tm, B_COLS), lambda i: (i, 0)),
                   pl.BlockSpec((tm, C_COLS), lambda i: (i, 0))],
        out_shape=[jax.ShapeDtypeStruct((t, A_COLS), BF16)] + res_shapes + [
                   jax.ShapeDtypeStruct((t, B_COLS), F32),
                   jax.ShapeDtypeStruct((t, C_COLS), BF16)],
        scratch_shapes=[pltpu.VMEM((A_COLS // LANES, tm, LANES), F32)],
        compiler_params=_cparams("arbitrary"),
        name="in_proj",
    )(x2d, w_p)
    return outs[0], list(outs[1:-2]), outs[-2], outs[-1]


def _band_attn_body(*refs, width, max_dist, has_sink, want_lse):
    q_ref, kp_ref, kc_ref, vp_ref, vc_ref, bias_ref = refs[:6]
    rest = refs[6:]
    if has_sink:
        sink_ref, rest = rest[0], rest[1:]
    o_ref = rest[0]
    lse_ref = rest[1] if want_lse else None

    first_tile = pl.program_id(2) == 0
    row = lax.broadcasted_iota(I32, (BAND, 2 * BAND), 0)
    col = lax.broadcasted_iota(I32, (BAND, 2 * BAND), 1)
    dist = row + BAND - col
    in_band = (dist >= 0) & (dist <= max_dist)
    first_mask = in_band & ((col >= BAND) | jnp.logical_not(first_tile))
    low_half = lax.broadcasted_iota(I32, (BAND, LANES), 1) < HEAD_DIM
    scale = HEAD_DIM ** -0.5

    for qb in range(q_ref.shape[0] // BAND):
        rows = slice(qb * BAND, (qb + 1) * BAND)
        mask = first_mask if qb == 0 else in_band
        for tile in range(width // LANES):
            sl = slice(tile * LANES, (tile + 1) * LANES)
            q2 = q_ref[rows, sl]
            if qb == 0:
                k2 = jnp.concatenate([kp_ref[:, sl], kc_ref[:BAND, sl]], axis=0)
                v2 = jnp.concatenate([vp_ref[:, sl], vc_ref[:BAND, sl]], axis=0)
            else:
                k2 = kc_ref[(qb - 1) * BAND:(qb + 1) * BAND, sl]
                v2 = vc_ref[(qb - 1) * BAND:(qb + 1) * BAND, sl]
            outs, lses = [], []
            for half in range(2):
                h = 2 * tile + half
                qm = jnp.where(low_half if half == 0 else jnp.logical_not(low_half), q2, jnp.zeros_like(q2))
                s = lax.dot_general(qm, k2, (((1,), (1,)), ((), ())), preferred_element_type=F32)
                s = s * scale + bias_ref[h]
                s = jnp.where(mask, s, MASK_VALUE)
                m = jnp.max(s, axis=-1, keepdims=True)
                if has_sink:
                    sink = sink_ref[h]
                    m = jnp.maximum(m, sink)
                p = jnp.exp(s - m)
                den = jnp.sum(p, axis=-1, keepdims=True)
                if has_sink:
                    den = den + jnp.exp(sink - m)
                pv = jnp.dot(p.astype(BF16), v2, preferred_element_type=F32)
                outs.append(pv / den)
                if want_lse:
                    lses.append(jnp.broadcast_to(m + jnp.log(den), (BAND, LANES)))
            o_ref[rows, sl] = jnp.where(low_half, outs[0], outs[1])
            if want_lse:
                lse_ref[rows, sl] = jnp.where(low_half, lses[0], lses[1])


def _band_attn(src, bias, *, width, max_dist, sinks=None, want_lse):
    nb, r, length, _ = src.shape
    qt = min(BAND_Q_TILE, length)
    bands = qt // BAND
    heads = width // HEAD_DIM
    has_sink = sinks is not None

    def cur(off):
        return pl.BlockSpec((None, None, qt, width), lambda b, p, i: (b, p, i, off))

    def prev(off):
        return pl.BlockSpec((None, None, BAND, width), lambda b, p, i: (b, p, jnp.maximum(i * bands - 1, 0), off))

    in_specs = [cur(0), prev(1), cur(1), prev(2), cur(2),
                pl.BlockSpec((heads, BAND, 2 * BAND), lambda b, p, i: (0, 0, 0))]
    args = [src, src, src, src, src, bias]
    if has_sink:
        in_specs.append(pl.BlockSpec(memory_space=pltpu.SMEM))
        args.append(sinks)
    out_spec = pl.BlockSpec((None, None, qt, width), lambda b, p, i: (b, p, i, 0))
    out_sds = jax.ShapeDtypeStruct((nb, r, length, width), F32)
    body = functools.partial(_band_attn_body, width=width, max_dist=max_dist, has_sink=has_sink, want_lse=want_lse)
    return pl.pallas_call(
        body,
        grid=(nb, r, length // qt),
        in_specs=in_specs,
        out_specs=[out_spec, out_spec] if want_lse else out_spec,
        out_shape=[out_sds, out_sds] if want_lse else out_sds,
        compiler_params=_cparams("arbitrary", "arbitrary", "arbitrary"),
        name="band_attn_r%d_w%d" % (r, width),
    )(*args)


def _hgrn_body(b_ref, lb_ref, ng_ref, w_ref, o_ref, state_ref):
    c = HG_CHUNK
    kd = B_KEY_DIM

    @pl.when(pl.program_id(1) == 0)
    def _():
        state_ref[...] = jnp.zeros_like(state_ref)

    trow = lax.broadcasted_iota(I32, (c, 1), 0)
    ti = lax.broadcasted_iota(I32, (c, c), 0)
    si = lax.broadcasted_iota(I32, (c, c), 1)
    nt = (((1,), (1,)), ((), ()))
    w_all = w_ref[...]

    states = [state_ref[h] for h in range(B_HEADS)]
    for chunk, h in [(ci, hi) for ci in range(b_ref.shape[1] // c) for hi in range(B_HEADS)]:
        rows = slice(chunk * c, (chunk + 1) * c)
        ks = slice(h * kd, (h + 1) * kd)
        q = b_ref[0, rows, ks]
        f = b_ref[0, rows, B_KEY_WIDTH + h * kd:B_KEY_WIDTH + (h + 1) * kd]
        inp = b_ref[0, rows, 2 * B_KEY_WIDTH + h * kd:2 * B_KEY_WIDTH + (h + 1) * kd]
        gate = b_ref[0, rows, 3 * B_KEY_WIDTH + h * kd:3 * B_KEY_WIDTH + (h + 1) * kd]
        lb = lb_ref[:, ks]

        log_f = jnp.log(lb + (1.0 - lb) * jax.nn.sigmoid(f))
        key = (1.0 - lb) * jax.nn.sigmoid(-f)
        qs = q * jax.nn.sigmoid(q)

        g_hi = log_f.astype(BF16)
        g_lo = (log_f - g_hi.astype(F32)).astype(BF16)
        e2 = jnp.dot(w_all, jnp.concatenate([g_hi, g_lo], axis=1), preferred_element_type=F32)
        e = e2[:, :kd] + e2[:, kd:]

        cum = e[0:c]
        q_dec = (qs * jnp.exp(cum)).astype(BF16)
        k_dec = (key * jnp.exp(e[c:2 * c])).astype(BF16)
        inp_b = inp.astype(BF16)

        st = states[h]
        inter = lax.dot_general(q_dec, st.astype(BF16), nt, preferred_element_type=F32)

        scores = jnp.where(ti == si,
                           lax.dot_general(qs.astype(BF16), key.astype(BF16), nt, preferred_element_type=F32), 0.0)
        m = c
        for lvl in range(HG_LEVELS):
            el = jnp.exp(e[(2 + lvl) * c:(3 + lvl) * c])
            second = (trow % m) >= (m // 2)
            ql = jnp.where(second, qs * el, 0.0).astype(BF16)
            kl = jnp.where(second, 0.0, key * el).astype(BF16)
            sl = lax.dot_general(ql, kl, nt, preferred_element_type=F32)
            if m < c:
                sl = jnp.where((ti // m) == (si // m), sl, 0.0)
            scores = scores + sl
            m //= 2
        intra = jnp.dot(scores.astype(BF16), inp_b, preferred_element_type=F32)

        new_st = st * jnp.exp(cum[c - 1:c]) + lax.dot_general(
            inp_b, k_dec, (((0,), (0,)), ((), ())), preferred_element_type=F32)
        states[h] = new_st

        o = inter + intra
        ms = jnp.sum(o * o, axis=-1, keepdims=True) * (1.0 / B_VAL_DIM)
        o = o * lax.rsqrt(ms + 1e-6)
        o_ref[0, rows, ks] = o * ng_ref[:, ks] * (gate * jax.nn.sigmoid(gate))

    for h in range(B_HEADS):
        state_ref[h] = states[h]


def _hgrn(b_all, lower_bound, norm_g_pad, w_sum):
    nb, s, _ = b_all.shape
    c = HG_STEP_ROWS
    return pl.pallas_call(
        _hgrn_body,
        grid=(nb, s // c),
        in_specs=[pl.BlockSpec((1, c, B_COLS), lambda b, i: (b, i, 0)),
                  pl.BlockSpec((1, B_KEY_WIDTH), lambda b, i: (0, 0)),
                  pl.BlockSpec((1, B_PAD_WIDTH), lambda b, i: (0, 0)),
                  pl.BlockSpec(w_sum.shape, lambda b, i: (0, 0))],
        out_specs=pl.BlockSpec((1, c, B_PAD_WIDTH), lambda b, i: (b, i, 0)),
        out_shape=jax.ShapeDtypeStruct((nb, s, B_PAD_WIDTH), F32),
        scratch_shapes=[pltpu.VMEM((B_HEADS, B_VAL_PAD, B_KEY_DIM), F32)],
        compiler_params=_cparams("arbitrary", "arbitrary"),
        name="hgrn2",
    )(b_all, lower_bound, norm_g_pad, w_sum)


def _layer_norm_rows(z, g, b):
    mu = jnp.mean(z, axis=-1, keepdims=True)
    zc = z - mu
    var = jnp.mean(zc * zc, axis=-1, keepdims=True)
    return zc * lax.rsqrt(var + LN_EPS) * g + b


def _mix_out_body(*refs):
    n_pat = len(A_PATTERNS)
    o_refs, l_refs = refs[:n_pat], refs[n_pat:2 * n_pat]
    ob, oc, x_ref, w_ref, g_ref, b_ref, out_ref = refs[2 * n_pat:2 * n_pat + 7]
    scratch = refs[2 * n_pat + 7:]
    tm = x_ref.shape[0]
    n_slabs = A_WIDTH // LANES

    def token_order(ref, r, scr):
        if r == 1:
            return [ref[0, 0, :, j * LANES:(j + 1) * LANES] for j in range(n_slabs)]
        for p in range(r):
            for j in range(n_slabs):
                scr[j, pl.ds(p, tm // r, stride=r), :] = ref[0, p, :, j * LANES:(j + 1) * LANES]
        return [scr[j] for j in range(n_slabs)]

    scr_iter = iter(scratch)
    dil = [r for _, r in A_PATTERNS]
    o_slabs = [token_order(ref, r, None if r == 1 else next(scr_iter)) for ref, r in zip(o_refs, dil)]
    l_slabs = [token_order(ref, r, None if r == 1 else next(scr_iter)) for ref, r in zip(l_refs, dil)]

    merged = []
    for j in range(n_slabs):
        ls = [l[j] for l in l_slabs]
        m = functools.reduce(jnp.maximum, ls)
        ws = [jnp.exp(l - m) for l in ls]
        num = functools.reduce(lambda a, b: a + b, [w * o[j] for w, o in zip(ws, o_slabs)])
        merged.append(num / functools.reduce(lambda a, b: a + b, ws))
    cat = jnp.concatenate(merged + [ob[...], oc[...]], axis=1).astype(BF16)
    y = jnp.dot(cat, w_ref[...], preferred_element_type=F32)
    z = DEEPNORM_ALPHA * x_ref[...] + y
    out_ref[...] = _layer_norm_rows(z, g_ref[...], b_ref[...])


def _mix_out(o_list, l_list, ob, oc, x2d, w_out_p, ln_g, ln_b):
    t, d = x2d.shape
    tm = MIX_OUT_ROWS
    tiles = o_list[0].shape[1] * o_list[0].shape[2] // tm
    row = lambda w: pl.BlockSpec((tm, w), lambda i: (i, 0))
    full = lambda a: pl.BlockSpec(a.shape, lambda i: (0, 0))
    res = lambda a: pl.BlockSpec((1, a.shape[1], tm // a.shape[1], A_WIDTH), lambda i: (i // tiles, 0, i % tiles, 0))
    n_scratch = 2 * sum(1 for _, r in A_PATTERNS if r > 1)
    return pl.pallas_call(
        _mix_out_body,
        grid=(t // tm,),
        in_specs=[res(a) for a in o_list + l_list] + [row(B_PAD_WIDTH), row(C_WIDTH), row(d), full(w_out_p),
                                                      full(ln_g), full(ln_b)],
        out_specs=row(d),
        out_shape=jax.ShapeDtypeStruct((t, d), F32),
        scratch_shapes=[pltpu.VMEM((A_WIDTH // LANES, tm, LANES), F32)] * n_scratch,
        compiler_params=_cparams("arbitrary"),
        name="mix_out_ln",
    )(*o_list, *l_list, ob, oc, x2d, w_out_p, ln_g, ln_b)


def _router_body(h_ref, rw_ref, bias_ref, tri_ref, e_ref, rank_ref, gate_ref, cnt_ref, hp_ref, carry_ref):
    tn = h_ref.shape[0]
    per_group = N_EXPERTS // N_GROUPS
    neg_inf = -jnp.inf

    @pl.when(pl.program_id(0) == 0)
    def _():
        carry_ref[...] = jnp.zeros_like(carry_ref)

    hp_ref[...] = _pack_bf16_pairs(h_ref[...])

    logits = lax.dot_general(rw_ref[...], h_ref[...], (((1,), (1,)), ((), ())),
                             precision=lax.Precision.HIGHEST, preferred_element_type=F32)
    scores = jax.nn.sigmoid(logits)
    choice = scores + bias_ref[...]

    def first_max(vals, idx, sentinel):
        top = jnp.max(vals, axis=0, keepdims=True)
        return top, jnp.min(jnp.where(vals == top, idx, sentinel), axis=0, keepdims=True)

    li = lax.broadcasted_iota(I32, (per_group, tn), 0).astype(F32)
    group_rows = []
    for g in range(N_GROUPS):
        cg = choice[g * per_group:(g + 1) * per_group]
        m1, first = first_max(cg, li, float(per_group))
        m2 = jnp.max(jnp.where(li == first, neg_inf, cg), axis=0, keepdims=True)
        group_rows.append(m1 + m2)
    group_score = jnp.concatenate(group_rows, axis=0)

    gi = lax.broadcasted_iota(I32, (N_GROUPS, tn), 0).astype(F32)
    group_ok = jnp.zeros((N_GROUPS, tn), F32)
    cur = group_score
    for _ in range(TOPK_GROUPS):
        _, first = first_max(cur, gi, float(N_GROUPS))
        pick = gi == first
        group_ok = jnp.where(pick, 1.0, group_ok)
        cur = jnp.where(pick, neg_inf, cur)

    cur = jnp.concatenate(
        [jnp.where(group_ok[g:g + 1] > 0.0, choice[g * per_group:(g + 1) * per_group], MASK_VALUE)
         for g in range(N_GROUPS)], axis=0)
    ei = lax.broadcasted_iota(I32, (N_EXPERTS, tn), 0).astype(F32)
    chosen = jnp.zeros((N_EXPERTS, tn), F32)
    picks, gates = [], []
    for _ in range(TOP_K):
        _, idx = first_max(cur, ei, float(N_EXPERTS))
        pick = ei == idx
        picks.append(idx)
        gates.append(jnp.sum(jnp.where(pick, scores, 0.0), axis=0, keepdims=True))
        chosen = jnp.where(pick, 1.0, chosen)
        cur = jnp.where(pick, neg_inf, cur)

    gate = jnp.concatenate(gates, axis=0)
    gate_ref[...] = gate / jnp.sum(gate, axis=0, keepdims=True) * ROUTED_SCALE
    e_ref[...] = jnp.concatenate(picks, axis=0).astype(I32)

    before = jnp.dot(chosen.astype(BF16), tri_ref[...], preferred_element_type=F32) + carry_ref[...]
    ranks = [jnp.sum(jnp.where(ei == idx, before, 0.0), axis=0, keepdims=True) for idx in picks]
    rank_ref[...] = jnp.concatenate(ranks, axis=0).astype(I32)
    carry = carry_ref[...] + jnp.sum(chosen, axis=1, keepdims=True)
    carry_ref[...] = carry
    cnt_ref[...] = carry.astype(I32)


def _router(h2d, rw_t, bias_col):
    t, d = h2d.shape
    tn = 256
    tri = jnp.asarray(np.triu(np.ones((tn, tn), np.float32), k=1), dtype=BF16)
    tok = lambda: pl.BlockSpec((TOP_K, tn), lambda i: (0, i))
    return pl.pallas_call(
        _router_body,
        grid=(t // tn,),
        in_specs=[pl.BlockSpec((tn, d), lambda i: (i, 0)),
                  pl.BlockSpec((N_EXPERTS, d), lambda i: (0, 0)),
                  pl.BlockSpec((N_EXPERTS, 1), lambda i: (0, 0)),
                  pl.BlockSpec((tn, tn), lambda i: (0, 0))],
        out_specs=[tok(), tok(), tok(), pl.BlockSpec((N_EXPERTS, 1), lambda i: (0, 0)),
                   pl.BlockSpec((tn, d // 2), lambda i: (i, 0))],
        out_shape=[jax.ShapeDtypeStruct((TOP_K, t), I32),
                   jax.ShapeDtypeStruct((TOP_K, t), I32),
                   jax.ShapeDtypeStruct((TOP_K, t), F32),
                   jax.ShapeDtypeStruct((N_EXPERTS, 1), I32),
                   jax.ShapeDtypeStruct((t, d // 2), U32)],
        scratch_shapes=[pltpu.VMEM((N_EXPERTS, 1), F32)],
        compiler_params=_cparams("arbitrary"),
        name="moe_router",
    )(h2d, rw_t, bias_col, tri)


def _pack_bf16_pairs(x):
    w = x.shape[1] // 2
    hi = lax.bitcast_convert_type(x[:, :w].astype(BF16).astype(F32), U32)
    lo = lax.bitcast_convert_type(x[:, w:].astype(BF16).astype(F32), U32)
    return hi | (lo >> 16)


def _unpack_bf16_pairs(p):
    hi = lax.bitcast_convert_type(p & jnp.uint32(0xFFFF0000), F32)
    lo = lax.bitcast_convert_type(p << 16, F32)
    return hi, lo


def _dest_body(e_ref, rank_ref, offs_ref, dest_ref):
    tn = e_ref.shape[1]
    ei = lax.broadcasted_iota(I32, (N_EXPERTS, tn), 0)
    offs = offs_ref[...]
    rows = [jnp.sum(jnp.where(ei == e_ref[k:k + 1, :], offs, 0.0), axis=0, keepdims=True) for k in range(TOP_K)]
    dest_ref[...] = jnp.concatenate(rows, axis=0).astype(I32) + rank_ref[...]


def _dest_rows(e_t, rank_t, offsets):
    t = e_t.shape[1]
    tn = 512
    tok = pl.BlockSpec((TOP_K, tn), lambda i: (0, i))
    return pl.pallas_call(
        _dest_body,
        grid=(t // tn,),
        in_specs=[tok, tok, pl.BlockSpec((N_EXPERTS, 1), lambda i: (0, 0))],
        out_specs=tok,
        out_shape=jax.ShapeDtypeStruct((TOP_K, t), I32),
        compiler_params=_cparams("arbitrary"),
        name="moe_dest",
    )(e_t, rank_t, offsets.astype(F32).reshape(N_EXPERTS, 1))


def _expert_body(offs_ref, nblk_ref, cnt_ref, layer_ref, xs_hbm, wg_ref, wu_ref, wd_ref, ys_hbm,
                 xbuf, ybuf, wg_b, wu_b, wd_b, sem_in, sem_out):
    e = pl.program_id(0)
    slots, bm = xbuf.shape[:2]
    ahead = slots - 1
    n = nblk_ref[e]
    first = offs_ref[e] // bm
    total = (offs_ref[N_EXPERTS - 1] // bm) + nblk_ref[N_EXPERTS - 1]

    def rows(g):
        return pl.ds(pl.multiple_of(g * bm, bm), bm)

    def x_copy(g, slot):
        return pltpu.make_async_copy(xs_hbm.at[rows(g)], xbuf.at[slot], sem_in.at[slot])

    def y_copy(g, slot):
        return pltpu.make_async_copy(ybuf.at[slot], ys_hbm.at[rows(g)], sem_out.at[slot])

    @pl.when(e == 0)
    def _():
        for g0 in range(ahead):
            @pl.when(g0 < total)
            def _():
                x_copy(g0, g0).start()

    wg_b[...] = wg_ref[0, 0].astype(BF16)
    wu_b[...] = wu_ref[0, 0].astype(BF16)
    wd_b[...] = wd_ref[0, 0].astype(BF16)

    def block(j, carry):
        g = first + j
        slot = g % slots
        x_copy(g, slot).wait()

        @pl.when(g + ahead < total)
        def _():
            x_copy(g + ahead, (g + ahead) % slots).start()

        @pl.when(g >= slots)
        def _():
            y_copy(g - slots, slot).wait()

        live = lax.broadcasted_iota(I32, (bm, 1), 0) < (cnt_ref[e] - j * bm)
        hi, lo = _unpack_bf16_pairs(jnp.where(live, xbuf[slot], jnp.uint32(0)))
        xb = jnp.concatenate([hi.astype(BF16), lo.astype(BF16)], axis=1)
        gate = jnp.dot(xb, wg_b[...], preferred_element_type=F32)
        up = jnp.dot(xb, wu_b[...], preferred_element_type=F32)
        hidden = (gate * jax.nn.sigmoid(gate) * up).astype(BF16)
        ybuf[slot] = _pack_bf16_pairs(jnp.dot(hidden, wd_b[...], preferred_element_type=F32))
        y_copy(g, slot).start()
        return carry

    lax.fori_loop(0, n, block, 0)

    @pl.when(e == N_EXPERTS - 1)
    def _():
        for back in range(1, slots + 1):
            @pl.when(total >= back)
            def _():
                y_copy(total - back, (total - back) % slots).wait()


def _experts(xs, offsets, n_blk, counts, layer, w_gate, w_up, w_down):
    n_rows, dp = xs.shape
    bm = EXPERT_BLOCK
    d, ff = w_gate.shape[-2:]
    w_spec = lambda a, b: pl.BlockSpec((1, 1, a, b), lambda e, of, nb, ct, ly: (ly[0], e, 0, 0))
    grid_spec = pltpu.PrefetchScalarGridSpec(
        num_scalar_prefetch=4,
        grid=(N_EXPERTS,),
        in_specs=[pl.BlockSpec(memory_space=pl.ANY), w_spec(d, ff), w_spec(d, ff), w_spec(ff, d)],
        out_specs=pl.BlockSpec(memory_space=pl.ANY),
        scratch_shapes=[pltpu.VMEM((EXPERT_SLOTS, bm, dp), U32), pltpu.VMEM((EXPERT_SLOTS, bm, dp), U32),
                        pltpu.VMEM((d, ff), BF16), pltpu.VMEM((d, ff), BF16), pltpu.VMEM((ff, d), BF16),
                        pltpu.SemaphoreType.DMA((EXPERT_SLOTS,)), pltpu.SemaphoreType.DMA((EXPERT_SLOTS,))],
    )
    return pl.pallas_call(
        _expert_body,
        grid_spec=grid_spec,
        out_shape=jax.ShapeDtypeStruct((n_rows, dp), U32),
        compiler_params=_cparams("arbitrary"),
        name="moe_experts",
    )(offsets, n_blk, counts, layer, xs, w_gate, w_up, w_down)


SC_GATHER_ROWS = 64


def _sc_gather_rows(table, idx):
    info = plsc.get_sparse_core_info()
    nc, ns = info.num_cores, info.num_subcores
    workers = nc * ns
    n = idx.shape[0]
    w = table.shape[1]
    ch = SC_GATHER_ROWS
    per_worker = n // workers
    steps = per_worker // ch
    assert per_worker * workers == n and steps * ch == per_worker and steps % 2 == 0
    mesh = plsc.VectorSubcoreMesh(core_axis_name="c", subcore_axis_name="s")

    @functools.partial(
        pl.kernel, mesh=mesh,
        out_type=jax.ShapeDtypeStruct((n, w), table.dtype),
        scratch_types=[pltpu.VMEM((steps, ch), I32), pltpu.VMEM((2, ch, w), table.dtype),
                       pltpu.SemaphoreType.DMA((2,)), pltpu.SemaphoreType.DMA((2,))],
    )
    def gather_kernel(table_hbm, idx_hbm, out_hbm, idx_v, rows_v, gsem, wsem):
        wid = lax.axis_index("s") * nc + lax.axis_index("c")
        base = wid * per_worker
        pltpu.sync_copy(idx_hbm.at[wid], idx_v)

        def gather(i, slot):
            return pltpu.make_async_copy(table_hbm.at[idx_v.at[i]], rows_v.at[slot], gsem.at[slot])

        def write(i, slot):
            off = pl.multiple_of(base + i * ch, ch)
            return pltpu.make_async_copy(rows_v.at[slot], out_hbm.at[pl.ds(off, ch)], wsem.at[slot])

        gather(0, 0).start()

        @pl.loop(0, steps, step=2)
        def _(i):
            gather(i + 1, 1).start()
            gather(i, 0).wait()
            write(i, 0).start()
            write(i, 0).wait()

            @pl.when(i + 2 < steps)
            def _():
                gather(i + 2, 0).start()

            gather(i + 1, 1).wait()
            write(i + 1, 1).start()
            write(i + 1, 1).wait()

    return gather_kernel(table, idx.reshape(workers, steps, ch))


def _sc_scatter_rows(rows, idx, n_out):
    info = plsc.get_sparse_core_info()
    nc, ns = info.num_cores, info.num_subcores
    workers = nc * ns
    kk, t = idx.shape
    w = rows.shape[1]
    ch = SC_GATHER_ROWS
    per_worker = t // workers
    steps = per_worker // ch
    assert per_worker * workers == t and steps * ch == per_worker and steps % 2 == 0
    idx_w = idx.reshape(kk, workers, steps, ch).transpose(1, 0, 2, 3).reshape(workers, kk * steps, ch)
    mesh = plsc.VectorSubcoreMesh(core_axis_name="c", subcore_axis_name="s")

    @functools.partial(
        pl.kernel, mesh=mesh,
        out_type=jax.ShapeDtypeStruct((n_out, w), rows.dtype),
        scratch_types=[pltpu.VMEM((kk * steps, ch), I32), pltpu.VMEM((2, ch, w), rows.dtype),
                       pltpu.SemaphoreType.DMA((2,)), pltpu.SemaphoreType.DMA((2,))],
    )
    def scatter_kernel(rows_hbm, idx_hbm, out_hbm, idx_v, rows_v, rsem, ssem):
        wid = lax.axis_index("s") * nc + lax.axis_index("c")
        base = wid * per_worker
        pltpu.sync_copy(idx_hbm.at[wid], idx_v)

        def read(i, slot):
            off = pl.multiple_of(base + i * ch, ch)
            return pltpu.make_async_copy(rows_hbm.at[pl.ds(off, ch)], rows_v.at[slot], rsem.at[slot])

        def scatter(i, k, slot):
            return pltpu.make_async_copy(rows_v.at[slot], out_hbm.at[idx_v.at[k * steps + i]], ssem.at[slot])

        def scatter_all(i, slot):
            for k in range(kk):
                scatter(i, k, slot).start()
            for k in range(kk):
                scatter(i, k, slot).wait()

        read(0, 0).start()

        @pl.loop(0, steps, step=2)
        def _(i):
            read(i + 1, 1).start()
            read(i, 0).wait()
            scatter_all(i, 0)

            @pl.when(i + 2 < steps)
            def _():
                read(i + 2, 0).start()

            read(i + 1, 1).wait()
            scatter_all(i + 1, 1)

    return scatter_kernel(rows, idx_w)


def _combine_streamed_body(y_ref, gate_ref, h_ref, sg_ref, su_ref, sd_ref, g_ref, b_ref, *rest):
    out_ref = rest[-1]
    half = h_ref.shape[1] // 2
    h = h_ref[...]
    hb = h.astype(BF16)
    sg = jnp.dot(hb, sg_ref[...], preferred_element_type=F32)
    su = jnp.dot(hb, su_ref[...], preferred_element_type=F32)
    shared = jnp.dot((sg * jax.nn.sigmoid(sg) * su).astype(BF16), sd_ref[...], preferred_element_type=F32)
    z = DEEPNORM_ALPHA * h + shared
    z_hi, z_lo = z[:, :half], z[:, half:]
    for k in range(TOP_K):
        y_hi, y_lo = _unpack_bf16_pairs(y_ref[k])
        gate = gate_ref[:, k:k + 1]
        z_hi = z_hi + gate * y_hi
        z_lo = z_lo + gate * y_lo
    out_ref[...] = _layer_norm_rows(jnp.concatenate([z_hi, z_lo], axis=1), g_ref[...], b_ref[...])


def _combine_streamed(y_part, gate_tk, h2d, sh_gate, sh_up, sh_down, ln_g, ln_b, part, partial_out):
    t, d = h2d.shape
    tt = COMBINE_ROWS
    steps = y_part.shape[1] // tt
    first = part * steps
    full = lambda a: pl.BlockSpec(a.shape, lambda i: (0, 0))
    in_specs = [pl.BlockSpec((TOP_K, tt, d // 2), lambda i: (0, i, 0)),
                pl.BlockSpec((tt, TOP_K), lambda i: (first + i, 0)),
                pl.BlockSpec((tt, d), lambda i: (first + i, 0)),
                full(sh_gate), full(sh_up), full(sh_down), full(ln_g), full(ln_b)]
    args = [y_part, gate_tk, h2d, sh_gate, sh_up, sh_down, ln_g, ln_b]
    aliases = {}
    if partial_out is not None:
        in_specs.append(pl.BlockSpec(memory_space=pl.ANY))
        args.append(partial_out)
        aliases = {len(args) - 1: 0}
    return pl.pallas_call(
        _combine_streamed_body,
        grid=(steps,),
        in_specs=in_specs,
        out_specs=pl.BlockSpec((tt, d), lambda i: (first + i, 0)),
        out_shape=jax.ShapeDtypeStruct((t, d), F32),
        input_output_aliases=aliases,
        compiler_params=_cparams("arbitrary"),
        name="moe_combine_streamed_ln",
    )(*args)


def _mixer_sublayer(h2d, nb, s, w_in_p, w_out_p, a_biases, c_bias, lower_bound, norm_g_pad, sinks, w_sum, ln_g, ln_b):
    a_qkv, a_residue, b_all, c_qkv = _in_proj(h2d, w_in_p, nb, s)
    a_by_dilation = {1: a_qkv.reshape(nb, 1, s, A_COLS), **dict(zip(IN_PROJ_DILATIONS, a_residue))}
    o_list, l_list = [], []
    for (window, r), bias in zip(A_PATTERNS, a_biases):
        o, lse = _band_attn(a_by_dilation[r], bias, width=A_WIDTH, max_dist=window // r, want_lse=True)
        o_list.append(o)
        l_list.append(lse)
    oc = _band_attn(c_qkv.reshape(nb, 1, s, C_COLS), c_bias, width=C_WIDTH,
                    max_dist=C_WINDOW - 1, sinks=sinks, want_lse=False).reshape(nb * s, C_WIDTH)
    ob = _hgrn(b_all.reshape(nb, s, B_COLS), lower_bound, norm_g_pad, w_sum).reshape(nb * s, B_PAD_WIDTH)
    return _mix_out(o_list, l_list, ob, oc, h2d, w_out_p, ln_g, ln_b)


def _moe_sublayer(h2d, layer, router_w, router_bias, w_gate, w_up, w_down, sh_gate, sh_up, sh_down, ln_g, ln_b):
    t, d = h2d.shape
    bm = EXPERT_BLOCK
    e_t, rank_t, gate_t, counts, h_packed = _router(h2d, router_w.T.astype(F32),
                                                    router_bias.astype(F32).reshape(N_EXPERTS, 1))

    counts = counts.reshape(N_EXPERTS)
    padded = (counts + bm - 1) // bm * bm
    pad_end = jnp.cumsum(padded).astype(I32)
    offsets = pad_end - padded
    n_blocks = -(-(t * TOP_K + N_EXPERTS * (bm - 1)) // bm)

    dest_t = _dest_rows(e_t, rank_t, offsets)
    xs = _sc_scatter_rows(h_packed, dest_t, n_blocks * bm)
    ys = _experts(xs, offsets, (padded // bm).astype(I32), counts, jnp.full((1,), layer, I32), w_gate, w_up, w_down)
    tp = t // COMBINE_PARTS
    gate_tk = gate_t.T
    shared_w = (sh_gate.astype(BF16), sh_up.astype(BF16), sh_down.astype(BF16))
    out = None
    for part in range(COMBINE_PARTS):
        idx = dest_t[:, part * tp:(part + 1) * tp].reshape(TOP_K * tp)
        y_part = _sc_gather_rows(ys, idx).reshape(TOP_K, tp, d // 2)
        out = _combine_streamed(y_part, gate_tk, h2d, *shared_w, ln_g, ln_b, part, out)
    return out


def kernel(x, w_in, w_out, rel_bias_table, lower_bound_logits, hgrn_norm_g, attn_sinks, ln1_g, ln1_b, router_w, router_bias, expert_w_gate, expert_w_up, expert_w_down, shared_w_gate, shared_w_up, shared_w_down, ln2_g, ln2_b):
    nb, s, d = x.shape
    depth = w_in.shape[0]
    lb_probs = jax.nn.softmax(lower_bound_logits.astype(F32), axis=0)
    lower_bounds = jnp.cumsum(lb_probs, axis=0) - lb_probs[0]
    rel_table = rel_bias_table.astype(F32)
    a_biases = [_band_bias(rel_table, r, 0, A_HEADS) for _, r in A_PATTERNS]
    c_bias = _band_bias(rel_table, 1, A_HEADS, A_HEADS + C_HEADS)
    w_sum = jnp.asarray(_hgrn_sum_matrix(), dtype=BF16)
    row = lambda v: v.astype(F32).reshape(1, -1)

    h = x.astype(F32).reshape(nb * s, d)
    for l in range(depth):
        h = _mixer_sublayer(h, nb, s, _prep_w_in(w_in[l]), _prep_w_out(w_out[l].astype(F32)), a_biases, c_bias,
                            lower_bounds[l].reshape(1, B_KEY_WIDTH), _pad_heads_vec(hgrn_norm_g[l].astype(F32)),
                            attn_sinks[l].astype(F32), w_sum, row(ln1_g[l]), row(ln1_b[l]))
        h = _moe_sublayer(h, l, router_w[l], router_bias[l], expert_w_gate, expert_w_up, expert_w_down,
                          shared_w_gate[l], shared_w_up[l], shared_w_down[l], row(ln2_g[l]), row(ln2_b[l]))
    return h.reshape(nb, s, d).astype(x.dtype)
```

```python
import functools
import math

import numpy as np
import jax
import jax.numpy as jnp
from jax import lax
from jax.experimental import pallas as pl
from jax.experimental.pallas import tpu as pltpu
from jax.experimental.pallas import tpu_sc as plsc

F32 = jnp.float32
BF16 = jnp.bfloat16
I32 = jnp.int32
U32 = jnp.uint32

LANES = 128
SUBLANES = 8
VMEM_LIMIT = 56 * 1024 * 1024

D_MODEL = 1024
DEPTH = 2
HEAD_DIM = 64
BAND = 128
BAND_Q_TILE = 1024
MASK_VALUE = -1e30

A_HEADS = 6
A_PATTERNS = ((128, 1), (512, 4), (2048, 16))
IN_PROJ_DILATIONS = tuple(r for _, r in A_PATTERNS if r > 1)
IN_PROJ_ROWS = 512
MIX_OUT_ROWS = 512
B_HEADS = 4
B_KEY_DIM = 128
B_VAL_DIM = 96
B_VAL_PAD = 128
HG_CHUNK = 64
HG_STEP_ROWS = 256
C_HEADS = 4
C_KV_HEADS = 2
C_WINDOW = 128

A_WIDTH = A_HEADS * HEAD_DIM
B_KEY_WIDTH = B_HEADS * B_KEY_DIM
B_WIDTH = B_HEADS * B_VAL_DIM
B_PAD_WIDTH = B_HEADS * B_VAL_PAD
C_WIDTH = C_HEADS * HEAD_DIM
C_KV_WIDTH = C_KV_HEADS * HEAD_DIM
IN_SPLITS = (A_WIDTH, A_WIDTH, A_WIDTH, B_KEY_WIDTH, B_KEY_WIDTH, B_WIDTH, B_WIDTH, C_WIDTH, C_KV_WIDTH, C_KV_WIDTH)

A_COLS = 3 * A_WIDTH
B_COLS = 4 * B_KEY_WIDTH
C_COLS = 3 * C_WIDTH
MIX_PAD_WIDTH = A_WIDTH + B_PAD_WIDTH + C_WIDTH

REL_BUCKETS = 32
REL_MAX_DIST = 2048

N_EXPERTS = 256
TOP_K = 8
N_GROUPS = 8
TOPK_GROUPS = 4
EXPERT_FF = 256
SHARED_FF = 256
ROUTED_SCALE = 2.5
EXPERT_BLOCK = 512
EXPERT_SLOTS = 3
COMBINE_PARTS = 4
COMBINE_ROWS = 512

DEEPNORM_ALPHA = (2 * DEPTH) ** 0.25
LN_EPS = 1e-5


def _cparams(*sem):
    return pltpu.CompilerParams(dimension_semantics=sem, vmem_limit_bytes=VMEM_LIMIT)


def _prep_w_in(w_in):
    d = w_in.shape[0]
    split_at = [int(i) for i in np.cumsum(IN_SPLITS)[:-1]]
    aq, ak, av, bq, bf, bi, bg, cq, ck, cv = jnp.split(w_in, split_at, axis=-1)
    pad_v = lambda w: jnp.pad(w.reshape(d, B_HEADS, B_VAL_DIM), ((0, 0), (0, 0), (0, B_VAL_PAD - B_VAL_DIM))).reshape(d, B_PAD_WIDTH)
    rep = lambda w: jnp.repeat(w.reshape(d, C_KV_HEADS, HEAD_DIM), C_HEADS // C_KV_HEADS, axis=1).reshape(d, C_WIDTH)
    cols = [aq, ak, av, bq, bf, pad_v(bi), pad_v(bg), cq, rep(ck), rep(cv)]
    return jnp.concatenate(cols, axis=-1).astype(BF16)


def _prep_w_out(w_out):
    d = w_out.shape[1]
    wa = w_out[:A_WIDTH]
    wb = w_out[A_WIDTH:A_WIDTH + B_WIDTH].reshape(B_HEADS, B_VAL_DIM, d)
    wb = jnp.pad(wb, ((0, 0), (0, B_VAL_PAD - B_VAL_DIM), (0, 0))).reshape(B_PAD_WIDTH, d)
    wc = w_out[A_WIDTH + B_WIDTH:]
    return jnp.concatenate([wa, wb, wc], axis=0).astype(BF16)


def _pad_heads_vec(v):
    return jnp.pad(v.reshape(B_HEADS, B_VAL_DIM), ((0, 0), (0, B_VAL_PAD - B_VAL_DIM))).reshape(1, B_PAD_WIDTH)


def _rel_bucket(dist):
    max_exact = REL_BUCKETS // 2
    d = jnp.maximum(dist, 0)
    log_ratio = jnp.log(jnp.maximum(d, max_exact).astype(F32) / max_exact) / math.log(REL_MAX_DIST / max_exact)
    large = jnp.minimum(max_exact + (log_ratio * (REL_BUCKETS - max_exact)).astype(I32), REL_BUCKETS - 1)
    return jnp.where(d < max_exact, d, large)


def _band_bias(rel_table, r, head_lo, head_hi):
    dist = jnp.arange(BAND)[:, None] + BAND - jnp.arange(2 * BAND)[None, :]
    onehot = jax.nn.one_hot(_rel_bucket(dist * r), REL_BUCKETS, dtype=F32)
    return jnp.einsum("qkb,bh->hqk", onehot, rel_table[:, head_lo:head_hi], precision=lax.Precision.HIGHEST)


def _hgrn_sum_matrix():
    return np.tril(np.ones((HG_CHUNK, HG_CHUNK), np.float32))


HG_LEVELS = int(math.log2(HG_CHUNK))


def _in_proj_body(x_ref, w_ref, *rest):
    n_res = len(IN_PROJ_DILATIONS)
    a_ref, res_refs, (b_ref, c_ref, slabs) = rest[0], rest[1:1 + n_res], rest[1 + n_res:]
    tm = x_ref.shape[0]
    xb = x_ref[...].astype(BF16)
    a = jnp.dot(xb, w_ref[:, :A_COLS], preferred_element_type=F32)
    a_ref[...] = a.astype(BF16)
    for j in range(A_COLS // LANES):
        slabs[j] = a[:, j * LANES:(j + 1) * LANES]
    for r, ref in zip(IN_PROJ_DILATIONS, res_refs):
        for p in range(r):
            for j in range(A_COLS // LANES):
                ref[0, p, :, j * LANES:(j + 1) * LANES] = slabs[j, pl.ds(p, tm // r, stride=r), :].astype(BF16)
    for j in range(B_COLS // B_KEY_WIDTH):
        lo = A_COLS + j * B_KEY_WIDTH
        b_ref[:, j * B_KEY_WIDTH:(j + 1) * B_KEY_WIDTH] = jnp.dot(
            xb, w_ref[:, lo:lo + B_KEY_WIDTH], preferred_element_type=F32)
    c_ref[...] = jnp.dot(xb, w_ref[:, A_COLS + B_COLS:], preferred_element_type=F32).astype(BF16)


def _in_proj(x2d, w_p, nb, s):
    t, d = x2d.shape
    tm = IN_PROJ_ROWS
    n = w_p.shape[1]
    tiles = s // tm
    res_specs = [pl.BlockSpec((1, r, tm // r, A_COLS), lambda i: (i // tiles, 0, i % tiles, 0))
                 for r in IN_PROJ_DILATIONS]
    res_shapes = [jax.ShapeDtypeStruct((nb, r, s // r, A_COLS), BF16) for r in IN_PROJ_DILATIONS]
    outs = pl.pallas_call(
        _in_proj_body,
        grid=(t // tm,),
        in_specs=[pl.BlockSpec((tm, d), lambda i: (i, 0)),
                  pl.BlockSpec((d, n), lambda i: (0, 0))],
        out_specs=[pl.BlockSpec((tm, A_COLS), lambda i: (i, 0))] + res_specs + [
                   pl.BlockSpec((tm, B_COLS), lambda i: (i, 0)),
                   pl.BlockSpec((tm, C_COLS), lambda i: (i, 0))],
        out_shape=[jax.ShapeDtypeStruct((t, A_COLS), BF16)] + res_shapes + [
                   jax.ShapeDtypeStruct((t, B_COLS), F32),
                   jax.ShapeDtypeStruct((t, C_COLS), BF16)],
        scratch_shapes=[pltpu.VMEM((A_COLS // LANES, tm, LANES), F32)],
        compiler_params=_cparams("arbitrary"),
        name="in_proj",
    )(x2d, w_p)
    return outs[0], list(outs[1:-2]), outs[-2], outs[-1]


def _band_attn_body(*refs, width, max_dist, has_sink, want_lse):
    q_ref, kp_ref, kc_ref, vp_ref, vc_ref, bias_ref = refs[:6]
    rest = refs[6:]
    if has_sink:
        sink_ref, rest = rest[0], rest[1:]
    o_ref = rest[0]
    lse_ref = rest[1] if want_lse else None

    first_tile = pl.program_id(2) == 0
    row = lax.broadcasted_iota(I32, (BAND, 2 * BAND), 0)
    col = lax.broadcasted_iota(I32, (BAND, 2 * BAND), 1)
    dist = row + BAND - col
    in_band = (dist >= 0) & (dist <= max_dist)
    first_mask = in_band & ((col >= BAND) | jnp.logical_not(first_tile))
    low_half = lax.broadcasted_iota(I32, (BAND, LANES), 1) < HEAD_DIM
    scale = HEAD_DIM ** -0.5

    for qb in range(q_ref.shape[0] // BAND):
        rows = slice(qb * BAND, (qb + 1) * BAND)
        mask = first_mask if qb == 0 else in_band
        for tile in range(width // LANES):
            sl = slice(tile * LANES, (tile + 1) * LANES)
            q2 = q_ref[rows, sl]
            if qb == 0:
                k2 = jnp.concatenate([kp_ref[:, sl], kc_ref[:BAND, sl]], axis=0)
                v2 = jnp.concatenate([vp_ref[:, sl], vc_ref[:BAND, sl]], axis=0)
            else:
                k2 = kc_ref[(qb - 1) * BAND:(qb + 1) * BAND, sl]
                v2 = vc_ref[(qb - 1) * BAND:(qb + 1) * BAND, sl]
            outs, lses = [], []
            for half in range(2):
                h = 2 * tile + half
                qm = jnp.where(low_half if half == 0 else jnp.logical_not(low_half), q2, jnp.zeros_like(q2))
                s = lax.dot_general(qm, k2, (((1,), (1,)), ((), ())), preferred_element_type=F32)
                s = s * scale + bias_ref[h]
                s = jnp.where(mask, s, MASK_VALUE)
                m = jnp.max(s, axis=-1, keepdims=True)
                if has_sink:
                    sink = sink_ref[h]
                    m = jnp.maximum(m, sink)
                p = jnp.exp(s - m)
                den = jnp.sum(p, axis=-1, keepdims=True)
                if has_sink:
                    den = den + jnp.exp(sink - m)
                pv = jnp.dot(p.astype(BF16), v2, preferred_element_type=F32)
                outs.append(pv / den)
                if want_lse:
                    lses.append(jnp.broadcast_to(m + jnp.log(den), (BAND, LANES)))
            o_ref[rows, sl] = jnp.where(low_half, outs[0], outs[1])
            if want_lse:
                lse_ref[rows, sl] = jnp.where(low_half, lses[0], lses[1])


def _band_attn(src, bias, *, width, max_dist, sinks=None, want_lse):
    nb, r, length, _ = src.shape
    qt = min(BAND_Q_TILE, length)
    bands = qt // BAND
    heads = width // HEAD_DIM
    has_sink = sinks is not None

    def cur(off):
        return pl.BlockSpec((None, None, qt, width), lambda b, p, i: (b, p, i, off))

    def prev(off):
        return pl.BlockSpec((None, None, BAND, width), lambda b, p, i: (b, p, jnp.maximum(i * bands - 1, 0), off))

    in_specs = [cur(0), prev(1), cur(1), prev(2), cur(2),
                pl.BlockSpec((heads, BAND, 2 * BAND), lambda b, p, i: (0, 0, 0))]
    args = [src, src, src, src, src, bias]
    if has_sink:
        in_specs.append(pl.BlockSpec(memory_space=pltpu.SMEM))
        args.append(sinks)
    out_spec = pl.BlockSpec((None, None, qt, width), lambda b, p, i: (b, p, i, 0))
    out_sds = jax.ShapeDtypeStruct((nb, r, length, width), F32)
    body = functools.partial(_band_attn_body, width=width, max_dist=max_dist, has_sink=has_sink, want_lse=want_lse)
    return pl.pallas_call(
        body,
        grid=(nb, r, length // qt),
        in_specs=in_specs,
        out_specs=[out_spec, out_spec] if want_lse else out_spec,
        out_shape=[out_sds, out_sds] if want_lse else out_sds,
        compiler_params=_cparams("arbitrary", "arbitrary", "arbitrary"),
        name="band_attn_r%d_w%d" % (r, width),
    )(*args)


def _hgrn_body(b_ref, lb_ref, ng_ref, w_ref, o_ref, state_ref):
    c = HG_CHUNK
    kd = B_KEY_DIM

    @pl.when(pl.program_id(1) == 0)
    def _():
        state_ref[...] = jnp.zeros_like(state_ref)

    trow = lax.broadcasted_iota(I32, (c, 1), 0)
    odd_row = (trow % 2) == 1
    low_sub = lax.broadcasted_iota(I32, (SUBLANES, 1), 0) < SUBLANES // 2
    ti = lax.broadcasted_iota(I32, (c, c), 0)
    si = lax.broadcasted_iota(I32, (c, c), 1)
    nt = (((1,), (1,)), ((), ()))
    tri = w_ref[...]

    states = [state_ref[h] for h in range(B_HEADS)]
    for chunk, h in [(ci, hi) for ci in range(b_ref.shape[1] // c) for hi in range(B_HEADS)]:
        rows = slice(chunk * c, (chunk + 1) * c)
        ks = slice(h * kd, (h + 1) * kd)
        q = b_ref[0, rows, ks]
        f = b_ref[0, rows, B_KEY_WIDTH + h * kd:B_KEY_WIDTH + (h + 1) * kd]
        inp = b_ref[0, rows, 2 * B_KEY_WIDTH + h * kd:2 * B_KEY_WIDTH + (h + 1) * kd]
        gate = b_ref[0, rows, 3 * B_KEY_WIDTH + h * kd:3 * B_KEY_WIDTH + (h + 1) * kd]
        lb = lb_ref[:, ks]

        a = jnp.exp(-jnp.abs(f))
        big = 1.0 / (1.0 + a)
        small = a * big
        pos = f >= 0.0
        forget = lb + (1.0 - lb) * jnp.where(pos, big, small)
        log_f = jnp.log(forget)
        key = (1.0 - lb) * jnp.where(pos, small, big)
        qs = q * jax.nn.sigmoid(q)

        g_hi = log_f.astype(BF16)
        g_lo = (log_f - g_hi.astype(F32)).astype(BF16)
        cum2 = jnp.dot(tri, jnp.concatenate([g_hi, g_lo], axis=1), preferred_element_type=F32)
        cum = cum2[:, :kd] + cum2[:, kd:]
        last = cum[c - 1:c]

        q_dec = (qs * jnp.exp(cum)).astype(BF16)
        k_dec = (key * jnp.exp(last - cum)).astype(BF16)
        inp_b = inp.astype(BF16)

        def level_decay(m):
            if m == 2:
                return jnp.where(odd_row, forget, 1.0)
            if m == 4:
                pieces = [jnp.where(low_sub,
                                    jnp.broadcast_to(cum[8 * j + 1:8 * j + 2], (SUBLANES, kd)),
                                    jnp.broadcast_to(cum[8 * j + 5:8 * j + 6], (SUBLANES, kd)))
                          for j in range(c // SUBLANES)]
            else:
                pieces = [jnp.broadcast_to(cum[b * m + m // 2 - 1:b * m + m // 2], (m, kd)) for b in range(c // m)]
            return jnp.exp(-jnp.abs(cum - jnp.concatenate(pieces, axis=0)))

        st = states[h]
        inter = lax.dot_general(q_dec, st.astype(BF16), nt, preferred_element_type=F32)

        scores = jnp.where(ti == si,
                           lax.dot_general(qs.astype(BF16), key.astype(BF16), nt, preferred_element_type=F32), 0.0)
        m = c
        for lvl in range(HG_LEVELS):
            el = level_decay(m)
            sl = lax.dot_general((qs * el).astype(BF16), (key * el).astype(BF16), nt, preferred_element_type=F32)
            in_level = ((ti // m) == (si // m)) & ((ti % m) >= (m // 2)) & ((si % m) < (m // 2))
            scores = jnp.where(in_level, sl, scores)
            m //= 2
        intra = jnp.dot(scores.astype(BF16), inp_b, preferred_element_type=F32)

        new_st = st * jnp.exp(last) + lax.dot_general(
            inp_b, k_dec, (((0,), (0,)), ((), ())), preferred_element_type=F32)
        states[h] = new_st

        o = inter + intra
        ms = jnp.sum(o * o, axis=-1, keepdims=True) * (1.0 / B_VAL_DIM)
        o = o * lax.rsqrt(ms + 1e-6)
        o_ref[0, rows, ks] = o * ng_ref[:, ks] * (gate * jax.nn.sigmoid(gate))

    for h in range(B_HEADS):
        state_ref[h] = states[h]


def _hgrn(b_all, lower_bound, norm_g_pad, w_sum):
    nb, s, _ = b_all.shape
    c = HG_STEP_ROWS
    return pl.pallas_call(
        _hgrn_body,
        grid=(nb, s // c),
        in_specs=[pl.BlockSpec((1, c, B_COLS), lambda b, i: (b, i, 0)),
                  pl.BlockSpec((1, B_KEY_WIDTH), lambda b, i: (0, 0)),
                  pl.BlockSpec((1, B_PAD_WIDTH), lambda b, i: (0, 0)),
                  pl.BlockSpec(w_sum.shape, lambda b, i: (0, 0))],
        out_specs=pl.BlockSpec((1, c, B_PAD_WIDTH), lambda b, i: (b, i, 0)),
        out_shape=jax.ShapeDtypeStruct((nb, s, B_PAD_WIDTH), F32),
        scratch_shapes=[pltpu.VMEM((B_HEADS, B_VAL_PAD, B_KEY_DIM), F32)],
        compiler_params=_cparams("arbitrary", "arbitrary"),
        name="hgrn2",
    )(b_all, lower_bound, norm_g_pad, w_sum)


def _layer_norm_rows(z, g, b):
    mu = jnp.mean(z, axis=-1, keepdims=True)
    zc = z - mu
    var = jnp.mean(zc * zc, axis=-1, keepdims=True)
    return zc * lax.rsqrt(var + LN_EPS) * g + b


def _mix_out_body(*refs):
    n_pat = len(A_PATTERNS)
    o_refs, l_refs = refs[:n_pat], refs[n_pat:2 * n_pat]
    ob, oc, x_ref, w_ref, g_ref, b_ref, out_ref = refs[2 * n_pat:2 * n_pat + 7]
    scratch = refs[2 * n_pat + 7:]
    tm = x_ref.shape[0]
    n_slabs = A_WIDTH // LANES

    def token_order(ref, r, scr):
        if r == 1:
            return [ref[0, 0, :, j * LANES:(j + 1) * LANES] for j in range(n_slabs)]
        for p in range(r):
            for j in range(n_slabs):
                scr[j, pl.ds(p, tm // r, stride=r), :] = ref[0, p, :, j * LANES:(j + 1) * LANES]
        return [scr[j] for j in range(n_slabs)]

    scr_iter = iter(scratch)
    dil = [r for _, r in A_PATTERNS]
    o_slabs = [token_order(ref, r, None if r == 1 else next(scr_iter)) for ref, r in zip(o_refs, dil)]
    l_slabs = [token_order(ref, r, None if r == 1 else next(scr_iter)) for ref, r in zip(l_refs, dil)]

    merged = []
    for j in range(n_slabs):
        ls = [l[j] for l in l_slabs]
        m = functools.reduce(jnp.maximum, ls)
        ws = [jnp.exp(l - m) for l in ls]
        num = functools.reduce(lambda a, b: a + b, [w * o[j] for w, o in zip(ws, o_slabs)])
        merged.append(num / functools.reduce(lambda a, b: a + b, ws))
    cat = jnp.concatenate(merged + [ob[...], oc[...]], axis=1).astype(BF16)
    y = jnp.dot(cat, w_ref[...], preferred_element_type=F32)
    z = DEEPNORM_ALPHA * x_ref[...] + y
    out_ref[...] = _layer_norm_rows(z, g_ref[...], b_ref[...])


def _mix_out(o_list, l_list, ob, oc, x2d, w_out_p, ln_g, ln_b):
    t, d = x2d.shape
    tm = MIX_OUT_ROWS
    tiles = o_list[0].shape[1] * o_list[0].shape[2] // tm
    row = lambda w: pl.BlockSpec((tm, w), lambda i: (i, 0))
    full = lambda a: pl.BlockSpec(a.shape, lambda i: (0, 0))
    res = lambda a: pl.BlockSpec((1, a.shape[1], tm // a.shape[1], A_WIDTH), lambda i: (i // tiles, 0, i % tiles, 0))
    n_scratch = 2 * sum(1 for _, r in A_PATTERNS if r > 1)
    return pl.pallas_call(
        _mix_out_body,
        grid=(t // tm,),
        in_specs=[res(a) for a in o_list + l_list] + [row(B_PAD_WIDTH), row(C_WIDTH), row(d), full(w_out_p),
                                                      full(ln_g), full(ln_b)],
        out_specs=row(d),
        out_shape=jax.ShapeDtypeStruct((t, d), F32),
        scratch_shapes=[pltpu.VMEM((A_WIDTH // LANES, tm, LANES), F32)] * n_scratch,
        compiler_params=_cparams("arbitrary"),
        name="mix_out_ln",
    )(*o_list, *l_list, ob, oc, x2d, w_out_p, ln_g, ln_b)


def _router_body(h_ref, rw_ref, bias_ref, tri_ref, e_ref, rank_ref, gate_ref, cnt_ref, hp_ref, carry_ref):
    tn = h_ref.shape[0]
    per_group = N_EXPERTS // N_GROUPS
    neg_inf = -jnp.inf

    @pl.when(pl.program_id(0) == 0)
    def _():
        carry_ref[...] = jnp.zeros_like(carry_ref)

    hp_ref[...] = _pack_bf16_pairs(h_ref[...])

    logits = lax.dot_general(rw_ref[...], h_ref[...], (((1,), (1,)), ((), ())),
                             precision=lax.Precision.HIGHEST, preferred_element_type=F32)
    scores = jax.nn.sigmoid(logits)
    choice = scores + bias_ref[...]

    def first_max(vals, idx, sentinel):
        top = jnp.max(vals, axis=0, keepdims=True)
        return top, jnp.min(jnp.where(vals == top, idx, sentinel), axis=0, keepdims=True)

    li = lax.broadcasted_iota(I32, (per_group, tn), 0).astype(F32)
    group_rows = []
    for g in range(N_GROUPS):
        cg = choice[g * per_group:(g + 1) * per_group]
        m1, first = first_max(cg, li, float(per_group))
        m2 = jnp.max(jnp.where(li == first, neg_inf, cg), axis=0, keepdims=True)
        group_rows.append(m1 + m2)
    group_score = jnp.concatenate(group_rows, axis=0)

    gi = lax.broadcasted_iota(I32, (N_GROUPS, tn), 0).astype(F32)
    group_ok = jnp.zeros((N_GROUPS, tn), F32)
    cur = group_score
    for _ in range(TOPK_GROUPS):
        _, first = first_max(cur, gi, float(N_GROUPS))
        pick = gi == first
        group_ok = jnp.where(pick, 1.0, group_ok)
        cur = jnp.where(pick, neg_inf, cur)

    cur = jnp.concatenate(
        [jnp.where(group_ok[g:g + 1] > 0.0, choice[g * per_group:(g + 1) * per_group], MASK_VALUE)
         for g in range(N_GROUPS)], axis=0)
    ei = lax.broadcasted_iota(I32, (N_EXPERTS, tn), 0).astype(F32)
    chosen = jnp.zeros((N_EXPERTS, tn), F32)
    picks, gates = [], []
    for _ in range(TOP_K):
        _, idx = first_max(cur, ei, float(N_EXPERTS))
        pick = ei == idx
        picks.append(idx)
        gates.append(jnp.sum(jnp.where(pick, scores, 0.0), axis=0, keepdims=True))
        chosen = jnp.where(pick, 1.0, chosen)
        cur = jnp.where(pick, neg_inf, cur)

    gate = jnp.concatenate(gates, axis=0)
    gate_ref[...] = gate / jnp.sum(gate, axis=0, keepdims=True) * ROUTED_SCALE
    e_ref[...] = jnp.concatenate(picks, axis=0).astype(I32)

    before = jnp.dot(chosen.astype(BF16), tri_ref[...], preferred_element_type=F32) + carry_ref[...]
    ranks = [jnp.sum(jnp.where(ei == idx, before, 0.0), axis=0, keepdims=True) for idx in picks]
    rank_ref[...] = jnp.concatenate(ranks, axis=0).astype(I32)
    carry = carry_ref[...] + jnp.sum(chosen, axis=1, keepdims=True)
    carry_ref[...] = carry
    cnt_ref[...] = carry.astype(I32)


def _router(h2d, rw_t, bias_col):
    t, d = h2d.shape
    tn = 256
    tri = jnp.asarray(np.triu(np.ones((tn, tn), np.float32), k=1), dtype=BF16)
    tok = lambda: pl.BlockSpec((TOP_K, tn), lambda i: (0, i))
    return pl.pallas_call(
        _router_body,
        grid=(t // tn,),
        in_specs=[pl.BlockSpec((tn, d), lambda i: (i, 0)),
                  pl.BlockSpec((N_EXPERTS, d), lambda i: (0, 0)),
                  pl.BlockSpec((N_EXPERTS, 1), lambda i: (0, 0)),
                  pl.BlockSpec((tn, tn), lambda i: (0, 0))],
        out_specs=[tok(), tok(), tok(), pl.BlockSpec((N_EXPERTS, 1), lambda i: (0, 0)),
                   pl.BlockSpec((tn, d // 2), lambda i: (i, 0))],
        out_shape=[jax.ShapeDtypeStruct((TOP_K, t), I32),
                   jax.ShapeDtypeStruct((TOP_K, t), I32),
                   jax.ShapeDtypeStruct((TOP_K, t), F32),
                   jax.ShapeDtypeStruct((N_EXPERTS, 1), I32),
                   jax.ShapeDtypeStruct((t, d // 2), U32)],
        scratch_shapes=[pltpu.VMEM((N_EXPERTS, 1), F32)],
        compiler_params=_cparams("arbitrary"),
        name="moe_router",
    )(h2d, rw_t, bias_col, tri)


def _pack_bf16_pairs(x):
    w = x.shape[1] // 2
    hi = lax.bitcast_convert_type(x[:, :w].astype(BF16).astype(F32), U32)
    lo = lax.bitcast_convert_type(x[:, w:].astype(BF16).astype(F32), U32)
    return hi | (lo >> 16)


def _unpack_bf16_pairs(p):
    hi = lax.bitcast_convert_type(p & jnp.uint32(0xFFFF0000), F32)
    lo = lax.bitcast_convert_type(p << 16, F32)
    return hi, lo


def _dest_body(e_ref, rank_ref, offs_ref, dest_ref):
    tn = e_ref.shape[1]
    ei = lax.broadcasted_iota(I32, (N_EXPERTS, tn), 0)
    offs = offs_ref[...]
    rows = [jnp.sum(jnp.where(ei == e_ref[k:k + 1, :], offs, 0.0), axis=0, keepdims=True) for k in range(TOP_K)]
    dest_ref[...] = jnp.concatenate(rows, axis=0).astype(I32) + rank_ref[...]


def _dest_rows(e_t, rank_t, offsets):
    t = e_t.shape[1]
    tn = 512
    tok = pl.BlockSpec((TOP_K, tn), lambda i: (0, i))
    return pl.pallas_call(
        _dest_body,
        grid=(t // tn,),
        in_specs=[tok, tok, pl.BlockSpec((N_EXPERTS, 1), lambda i: (0, 0))],
        out_specs=tok,
        out_shape=jax.ShapeDtypeStruct((TOP_K, t), I32),
        compiler_params=_cparams("arbitrary"),
        name="moe_dest",
    )(e_t, rank_t, offsets.astype(F32).reshape(N_EXPERTS, 1))


def _expert_body(offs_ref, nblk_ref, cnt_ref, layer_ref, xs_hbm, wg_ref, wu_ref, wd_ref, ys_hbm,
                 xbuf, ybuf, wg_b, wu_b, wd_b, sem_in, sem_out):
    e = pl.program_id(0)
    slots, bm = xbuf.shape[:2]
    ahead = slots - 1
    n = nblk_ref[e]
    first = offs_ref[e] // bm
    total = (offs_ref[N_EXPERTS - 1] // bm) + nblk_ref[N_EXPERTS - 1]

    def rows(g):
        return pl.ds(pl.multiple_of(g * bm, bm), bm)

    def x_copy(g, slot):
        return pltpu.make_async_copy(xs_hbm.at[rows(g)], xbuf.at[slot], sem_in.at[slot])

    def y_copy(g, slot):
        return pltpu.make_async_copy(ybuf.at[slot], ys_hbm.at[rows(g)], sem_out.at[slot])

    @pl.when(e == 0)
    def _():
        for g0 in range(ahead):
            @pl.when(g0 < total)
            def _():
                x_copy(g0, g0).start()

    wg_b[...] = wg_ref[0, 0].astype(BF16)
    wu_b[...] = wu_ref[0, 0].astype(BF16)
    wd_b[...] = wd_ref[0, 0].astype(BF16)

    def block(j, carry):
        g = first + j
        slot = g % slots
        x_copy(g, slot).wait()

        @pl.when(g + ahead < total)
        def _():
            x_copy(g + ahead, (g + ahead) % slots).start()

        @pl.when(g >= slots)
        def _():
            y_copy(g - slots, slot).wait()

        live = lax.broadcasted_iota(I32, (bm, 1), 0) < (cnt_ref[e] - j * bm)
        hi, lo = _unpack_bf16_pairs(jnp.where(live, xbuf[slot], jnp.uint32(0)))
        xb = jnp.concatenate([hi.astype(BF16), lo.astype(BF16)], axis=1)
        gate = jnp.dot(xb, wg_b[...], preferred_element_type=F32)
        up = jnp.dot(xb, wu_b[...], preferred_element_type=F32)
        hidden = (gate * jax.nn.sigmoid(gate) * up).astype(BF16)
        ybuf[slot] = _pack_bf16_pairs(jnp.dot(hidden, wd_b[...], preferred_element_type=F32))
        y_copy(g, slot).start()
        return carry

    lax.fori_loop(0, n, block, 0)

    @pl.when(e == N_EXPERTS - 1)
    def _():
        for back in range(1, slots + 1):
            @pl.when(total >= back)
            def _():
                y_copy(total - back, (total - back) % slots).wait()


def _experts(xs, offsets, n_blk, counts, layer, w_gate, w_up, w_down):
    n_rows, dp = xs.shape
    bm = EXPERT_BLOCK
    d, ff = w_gate.shape[-2:]
    w_spec = lambda a, b: pl.BlockSpec((1, 1, a, b), lambda e, of, nb, ct, ly: (ly[0], e, 0, 0))
    grid_spec = pltpu.PrefetchScalarGridSpec(
        num_scalar_prefetch=4,
        grid=(N_EXPERTS,),
        in_specs=[pl.BlockSpec(memory_space=pl.ANY), w_spec(d, ff), w_spec(d, ff), w_spec(ff, d)],
        out_specs=pl.BlockSpec(memory_space=pl.ANY),
        scratch_shapes=[pltpu.VMEM((EXPERT_SLOTS, bm, dp), U32), pltpu.VMEM((EXPERT_SLOTS, bm, dp), U32),
                        pltpu.VMEM((d, ff), BF16), pltpu.VMEM((d, ff), BF16), pltpu.VMEM((ff, d), BF16),
                        pltpu.SemaphoreType.DMA((EXPERT_SLOTS,)), pltpu.SemaphoreType.DMA((EXPERT_SLOTS,))],
    )
    return pl.pallas_call(
        _expert_body,
        grid_spec=grid_spec,
        out_shape=jax.ShapeDtypeStruct((n_rows, dp), U32),
        compiler_params=_cparams("arbitrary"),
        name="moe_experts",
    )(offsets, n_blk, counts, layer, xs, w_gate, w_up, w_down)


SC_GATHER_ROWS = 64


def _sc_gather_rows(table, idx):
    info = plsc.get_sparse_core_info()
    nc, ns = info.num_cores, info.num_subcores
    workers = nc * ns
    n = idx.shape[0]
    w = table.shape[1]
    ch = SC_GATHER_ROWS
    per_worker = n // workers
    steps = per_worker // ch
    assert per_worker * workers == n and steps * ch == per_worker and steps % 2 == 0
    mesh = plsc.VectorSubcoreMesh(core_axis_name="c", subcore_axis_name="s")

    @functools.partial(
        pl.kernel, mesh=mesh,
        out_type=jax.ShapeDtypeStruct((n, w), table.dtype),
        scratch_types=[pltpu.VMEM((steps, ch), I32), pltpu.VMEM((2, ch, w), table.dtype),
                       pltpu.SemaphoreType.DMA((2,)), pltpu.SemaphoreType.DMA((2,))],
    )
    def gather_kernel(table_hbm, idx_hbm, out_hbm, idx_v, rows_v, gsem, wsem):
        wid = lax.axis_index("s") * nc + lax.axis_index("c")
        base = wid * per_worker
        pltpu.sync_copy(idx_hbm.at[wid], idx_v)

        def gather(i, slot):
            return pltpu.make_async_copy(table_hbm.at[idx_v.at[i]], rows_v.at[slot], gsem.at[slot])

        def write(i, slot):
            off = pl.multiple_of(base + i * ch, ch)
            return pltpu.make_async_copy(rows_v.at[slot], out_hbm.at[pl.ds(off, ch)], wsem.at[slot])

        gather(0, 0).start()

        @pl.loop(0, steps, step=2)
        def _(i):
            gather(i + 1, 1).start()
            gather(i, 0).wait()
            write(i, 0).start()
            write(i, 0).wait()

            @pl.when(i + 2 < steps)
            def _():
                gather(i + 2, 0).start()

            gather(i + 1, 1).wait()
            write(i + 1, 1).start()
            write(i + 1, 1).wait()

    return gather_kernel(table, idx.reshape(workers, steps, ch))


def _sc_scatter_rows(rows, idx, n_out):
    info = plsc.get_sparse_core_info()
    nc, ns = info.num_cores, info.num_subcores
    workers = nc * ns
    kk, t = idx.shape
    w = rows.shape[1]
    ch = SC_GATHER_ROWS
    per_worker = t // workers
    steps = per_worker // ch
    assert per_worker * workers == t and steps * ch == per_worker and steps % 2 == 0
    idx_w = idx.reshape(kk, workers, steps, ch).transpose(1, 0, 2, 3).reshape(workers, kk * steps, ch)
    mesh = plsc.VectorSubcoreMesh(core_axis_name="c", subcore_axis_name="s")

    @functools.partial(
        pl.kernel, mesh=mesh,
        out_type=jax.ShapeDtypeStruct((n_out, w), rows.dtype),
        scratch_types=[pltpu.VMEM((kk * steps, ch), I32), pltpu.VMEM((2, ch, w), rows.dtype),
                       pltpu.SemaphoreType.DMA((2,)), pltpu.SemaphoreType.DMA((2,))],
    )
    def scatter_kernel(rows_hbm, idx_hbm, out_hbm, idx_v, rows_v, rsem, ssem):
        wid = lax.axis_index("s") * nc + lax.axis_index("c")
        base = wid * per_worker
        pltpu.sync_copy(idx_hbm.at[wid], idx_v)

        def read(i, slot):
            off = pl.multiple_of(base + i * ch, ch)
            return pltpu.make_async_copy(rows_hbm.at[pl.ds(off, ch)], rows_v.at[slot], rsem.at[slot])

        def scatter(i, k, slot):
            return pltpu.make_async_copy(rows_v.at[slot], out_hbm.at[idx_v.at[k * steps + i]], ssem.at[slot])

        def scatter_all(i, slot):
            for k in range(kk):
                scatter(i, k, slot).start()
            for k in range(kk):
                scatter(i, k, slot).wait()

        read(0, 0).start()

        @pl.loop(0, steps, step=2)
        def _(i):
            read(i + 1, 1).start()
            read(i, 0).wait()
            scatter_all(i, 0)

            @pl.when(i + 2 < steps)
            def _():
                read(i + 2, 0).start()

            read(i + 1, 1).wait()
            scatter_all(i + 1, 1)

    return scatter_kernel(rows, idx_w)


def _combine_streamed_body(y_ref, gate_ref, h_ref, sg_ref, su_ref, sd_ref, g_ref, b_ref, *rest):
    out_ref = rest[-1]
    half = h_ref.shape[1] // 2
    h = h_ref[...]
    hb = h.astype(BF16)
    sg = jnp.dot(hb, sg_ref[...], preferred_element_type=F32)
    su = jnp.dot(hb, su_ref[...], preferred_element_type=F32)
    shared = jnp.dot((sg * jax.nn.sigmoid(sg) * su).astype(BF16), sd_ref[...], preferred_element_type=F32)
    z = DEEPNORM_ALPHA * h + shared
    z_hi, z_lo = z[:, :half], z[:, half:]
    for k in range(TOP_K):
        y_hi, y_lo = _unpack_bf16_pairs(y_ref[k])
        gate = gate_ref[:, k:k + 1]
        z_hi = z_hi + gate * y_hi
        z_lo = z_lo + gate * y_lo
    out_ref[...] = _layer_norm_rows(jnp.concatenate([z_hi, z_lo], axis=1), g_ref[...], b_ref[...])


def _combine_streamed(y_part, gate_tk, h2d, sh_gate, sh_up, sh_down, ln_g, ln_b, part, partial_out):
    t, d = h2d.shape
    tt = COMBINE_ROWS
    steps = y_part.shape[1] // tt
    first = part * steps
    full = lambda a: pl.BlockSpec(a.shape, lambda i: (0, 0))
    in_specs = [pl.BlockSpec((TOP_K, tt, d // 2), lambda i: (0, i, 0)),
                pl.BlockSpec((tt, TOP_K), lambda i: (first + i, 0)),
                pl.BlockSpec((tt, d), lambda i: (first + i, 0)),
                full(sh_gate), full(sh_up), full(sh_down), full(ln_g), full(ln_b)]
    args = [y_part, gate_tk, h2d, sh_gate, sh_up, sh_down, ln_g, ln_b]
    aliases = {}
    if partial_out is not None:
        in_specs.append(pl.BlockSpec(memory_space=pl.ANY))
        args.append(partial_out)
        aliases = {len(args) - 1: 0}
    return pl.pallas_call(
        _combine_streamed_body,
        grid=(steps,),
        in_specs=in_specs,
        out_specs=pl.BlockSpec((tt, d), lambda i: (first + i, 0)),
        out_shape=jax.ShapeDtypeStruct((t, d), F32),
        input_output_aliases=aliases,
        compiler_params=_cparams("arbitrary"),
        name="moe_combine_streamed_ln",
    )(*args)


def _mixer_sublayer(h2d, nb, s, w_in_p, w_out_p, a_biases, c_bias, lower_bound, norm_g_pad, sinks, w_sum, ln_g, ln_b):
    a_qkv, a_residue, b_all, c_qkv = _in_proj(h2d, w_in_p, nb, s)
    a_by_dilation = {1: a_qkv.reshape(nb, 1, s, A_COLS), **dict(zip(IN_PROJ_DILATIONS, a_residue))}
    o_list, l_list = [], []
    for (window, r), bias in zip(A_PATTERNS, a_biases):
        o, lse = _band_attn(a_by_dilation[r], bias, width=A_WIDTH, max_dist=window // r, want_lse=True)
        o_list.append(o)
        l_list.append(lse)
    oc = _band_attn(c_qkv.reshape(nb, 1, s, C_COLS), c_bias, width=C_WIDTH,
                    max_dist=C_WINDOW - 1, sinks=sinks, want_lse=False).reshape(nb * s, C_WIDTH)
    ob = _hgrn(b_all.reshape(nb, s, B_COLS), lower_bound, norm_g_pad, w_sum).reshape(nb * s, B_PAD_WIDTH)
    return _mix_out(o_list, l_list, ob, oc, h2d, w_out_p, ln_g, ln_b)


def _moe_sublayer(h2d, layer, router_w, router_bias, w_gate, w_up, w_down, sh_gate, sh_up, sh_down, ln_g, ln_b):
    t, d = h2d.shape
    bm = EXPERT_BLOCK
    e_t, rank_t, gate_t, counts, h_packed = _router(h2d, router_w.T.astype(F32),
                                                    router_bias.astype(F32).reshape(N_EXPERTS, 1))

    counts = counts.reshape(N_EXPERTS)
    padded = (counts + bm - 1) // bm * bm
    pad_end = jnp.cumsum(padded).astype(I32)
    offsets = pad_end - padded
    n_blocks = -(-(t * TOP_K + N_EXPERTS * (bm - 1)) // bm)

    dest_t = _dest_rows(e_t, rank_t, offsets)
    xs = _sc_scatter_rows(h_packed, dest_t, n_blocks * bm)
    ys = _experts(xs, offsets, (padded // bm).astype(I32), counts, jnp.full((1,), layer, I32), w_gate, w_up, w_down)
    tp = t // COMBINE_PARTS
    gate_tk = gate_t.T
    shared_w = (sh_gate.astype(BF16), sh_up.astype(BF16), sh_down.astype(BF16))
    out = None
    for part in range(COMBINE_PARTS):
        idx = dest_t[:, part * tp:(part + 1) * tp].reshape(TOP_K * tp)
        y_part = _sc_gather_rows(ys, idx).reshape(TOP_K, tp, d // 2)
        out = _combine_streamed(y_part, gate_tk, h2d, *shared_w, ln_g, ln_b, part, out)
    return out


def kernel(x, w_in, w_out, rel_bias_table, lower_bound_logits, hgrn_norm_g, attn_sinks, ln1_g, ln1_b, router_w, router_bias, expert_w_gate, expert_w_up, expert_w_down, shared_w_gate, shared_w_up, shared_w_down, ln2_g, ln2_b):
    nb, s, d = x.shape
    depth = w_in.shape[0]
    lb_probs = jax.nn.softmax(lower_bound_logits.astype(F32), axis=0)
    lower_bounds = jnp.cumsum(lb_probs, axis=0) - lb_probs[0]
    rel_table = rel_bias_table.astype(F32)
    a_biases = [_band_bias(rel_table, r, 0, A_HEADS) for _, r in A_PATTERNS]
    c_bias = _band_bias(rel_table, 1, A_HEADS, A_HEADS + C_HEADS)
    w_sum = jnp.asarray(_hgrn_sum_matrix(), dtype=BF16)
    row = lambda v: v.astype(F32).reshape(1, -1)

    h = x.astype(F32).reshape(nb * s, d)
    for l in range(depth):
        h = _mixer_sublayer(h, nb, s, _prep_w_in(w_in[l]), _prep_w_out(w_out[l].astype(F32)), a_biases, c_bias,
                            lower_bounds[l].reshape(1, B_KEY_WIDTH), _pad_heads_vec(hgrn_norm_g[l].astype(F32)),
                            attn_sinks[l].astype(F32), w_sum, row(ln1_g[l]), row(ln1_b[l]))
        h = _moe_sublayer(h, l, router_w[l], router_bias[l], expert_w_gate, expert_w_up, expert_w_down,
                          shared_w_gate[l], shared_w_up[l], shared_w_down[l], row(ln2_g[l]), row(ln2_b[l]))
    return h.reshape(nb, s, d).astype(x.dtype)
```

```python
import functools
import math

import numpy as np
import jax
import jax.numpy as jnp
from jax import lax
from jax.experimental import pallas as pl
from jax.experimental.pallas import tpu as pltpu
from jax.experimental.pallas import tpu_sc as plsc

F32 = jnp.float32
BF16 = jnp.bfloat16
I32 = jnp.int32
U32 = jnp.uint32

LANES = 128
SUBLANES = 8
VMEM_LIMIT = 56 * 1024 * 1024

D_MODEL = 1024
DEPTH = 2
HEAD_DIM = 64
BAND = 128
BAND_Q_TILE = 1024
MASK_VALUE = -1e30

A_HEADS = 6
A_PATTERNS = ((128, 1), (512, 4), (2048, 16))
IN_PROJ_DILATIONS = tuple(r for _, r in A_PATTERNS if r > 1)
IN_PROJ_ROWS = 512
MIX_OUT_ROWS = 512
B_HEADS = 4
B_KEY_DIM = 128
B_VAL_DIM = 96
B_VAL_PAD = 128
HG_CHUNK = 64
HG_STEP_ROWS = 256
C_HEADS = 4
C_KV_HEADS = 2
C_WINDOW = 128

A_WIDTH = A_HEADS * HEAD_DIM
B_KEY_WIDTH = B_HEADS * B_KEY_DIM
B_WIDTH = B_HEADS * B_VAL_DIM
B_PAD_WIDTH = B_HEADS * B_VAL_PAD
C_WIDTH = C_HEADS * HEAD_DIM
C_KV_WIDTH = C_KV_HEADS * HEAD_DIM
IN_SPLITS = (A_WIDTH, A_WIDTH, A_WIDTH, B_KEY_WIDTH, B_KEY_WIDTH, B_WIDTH, B_WIDTH, C_WIDTH, C_KV_WIDTH, C_KV_WIDTH)

A_COLS = 3 * A_WIDTH
B_COLS = 4 * B_KEY_WIDTH
C_COLS = 3 * C_WIDTH
MIX_PAD_WIDTH = A_WIDTH + B_PAD_WIDTH + C_WIDTH

REL_BUCKETS = 32
REL_MAX_DIST = 2048

N_EXPERTS = 256
TOP_K = 8
N_GROUPS = 8
TOPK_GROUPS = 4
EXPERT_FF = 256
SHARED_FF = 256
ROUTED_SCALE = 2.5
ROUTER_TOKENS = 256
DEST_TOKENS = 512
EXPERT_BLOCK = 512
EXPERT_SLOTS = 3
COMBINE_PARTS = 4
COMBINE_ROWS = 512

DEEPNORM_ALPHA = (2 * DEPTH) ** 0.25
LN_EPS = 1e-5


def _cparams(*sem):
    return pltpu.CompilerParams(dimension_semantics=sem, vmem_limit_bytes=VMEM_LIMIT)


def _prep_w_in(w_in):
    d = w_in.shape[0]
    split_at = [int(i) for i in np.cumsum(IN_SPLITS)[:-1]]
    aq, ak, av, bq, bf, bi, bg, cq, ck, cv = jnp.split(w_in, split_at, axis=-1)
    pad_v = lambda w: jnp.pad(w.reshape(d, B_HEADS, B_VAL_DIM), ((0, 0), (0, 0), (0, B_VAL_PAD - B_VAL_DIM))).reshape(d, B_PAD_WIDTH)
    rep = lambda w: jnp.repeat(w.reshape(d, C_KV_HEADS, HEAD_DIM), C_HEADS // C_KV_HEADS, axis=1).reshape(d, C_WIDTH)
    cols = [aq, ak, av, bq, bf, pad_v(bi), pad_v(bg), cq, rep(ck), rep(cv)]
    return jnp.concatenate(cols, axis=-1).astype(BF16)


def _prep_w_out(w_out):
    d = w_out.shape[1]
    wa = w_out[:A_WIDTH]
    wb = w_out[A_WIDTH:A_WIDTH + B_WIDTH].reshape(B_HEADS, B_VAL_DIM, d)
    wb = jnp.pad(wb, ((0, 0), (0, B_VAL_PAD - B_VAL_DIM), (0, 0))).reshape(B_PAD_WIDTH, d)
    wc = w_out[A_WIDTH + B_WIDTH:]
    return jnp.concatenate([wa, wb, wc], axis=0).astype(BF16)


def _pad_heads_vec(v):
    return jnp.pad(v.reshape(B_HEADS, B_VAL_DIM), ((0, 0), (0, B_VAL_PAD - B_VAL_DIM))).reshape(1, B_PAD_WIDTH)


def _rel_bucket(dist):
    max_exact = REL_BUCKETS // 2
    d = jnp.maximum(dist, 0)
    log_ratio = jnp.log(jnp.maximum(d, max_exact).astype(F32) / max_exact) / math.log(REL_MAX_DIST / max_exact)
    large = jnp.minimum(max_exact + (log_ratio * (REL_BUCKETS - max_exact)).astype(I32), REL_BUCKETS - 1)
    return jnp.where(d < max_exact, d, large)


def _band_bias(rel_table, r, head_lo, head_hi):
    dist = jnp.arange(BAND)[:, None] + BAND - jnp.arange(2 * BAND)[None, :]
    onehot = jax.nn.one_hot(_rel_bucket(dist * r), REL_BUCKETS, dtype=F32)
    return jnp.einsum("qkb,bh->hqk", onehot, rel_table[:, head_lo:head_hi], precision=lax.Precision.HIGHEST)


def _hgrn_sum_matrix():
    return np.tril(np.ones((HG_CHUNK, HG_CHUNK), np.float32))


HG_LEVELS = int(math.log2(HG_CHUNK))


def _in_proj_body(x_ref, w_ref, *rest):
    n_res = len(IN_PROJ_DILATIONS)
    a_ref, res_refs, (b_ref, c_ref, slabs) = rest[0], rest[1:1 + n_res], rest[1 + n_res:]
    tm = x_ref.shape[0]
    xb = x_ref[...].astype(BF16)
    a = jnp.dot(xb, w_ref[:, :A_COLS], preferred_element_type=F32)
    a_ref[...] = a.astype(BF16)
    for j in range(A_COLS // LANES):
        slabs[j] = a[:, j * LANES:(j + 1) * LANES]
    for r, ref in zip(IN_PROJ_DILATIONS, res_refs):
        for p in range(r):
            for j in range(A_COLS // LANES):
                ref[0, p, :, j * LANES:(j + 1) * LANES] = slabs[j, pl.ds(p, tm // r, stride=r), :].astype(BF16)
    for j in range(B_COLS // B_KEY_WIDTH):
        lo = A_COLS + j * B_KEY_WIDTH
        b_ref[:, j * B_KEY_WIDTH:(j + 1) * B_KEY_WIDTH] = jnp.dot(
            xb, w_ref[:, lo:lo + B_KEY_WIDTH], preferred_element_type=F32)
    c_ref[...] = jnp.dot(xb, w_ref[:, A_COLS + B_COLS:], preferred_element_type=F32).astype(BF16)


def _in_proj(x2d, w_p, nb, s):
    t, d = x2d.shape
    tm = IN_PROJ_ROWS
    n = w_p.shape[1]
    tiles = s // tm
    res_specs = [pl.BlockSpec((1, r, tm // r, A_COLS), lambda i: (i // tiles, 0, i % tiles, 0))
                 for r in IN_PROJ_DILATIONS]
    res_shapes = [jax.ShapeDtypeStruct((nb, r, s // r, A_COLS), BF16) for r in IN_PROJ_DILATIONS]
    outs = pl.pallas_call(
        _in_proj_body,
        grid=(t // tm,),
        in_specs=[pl.BlockSpec((tm, d), lambda i: (i, 0)),
                  pl.BlockSpec((d, n), lambda i: (0, 0))],
        out_specs=[pl.BlockSpec((tm, A_COLS), lambda i: (i, 0))] + res_specs + [
                   pl.BlockSpec((tm, B_COLS), lambda i: (i, 0)),
                   pl.BlockSpec((tm, C_COLS), lambda i: (i, 0))],
        out_shape=[jax.ShapeDtypeStruct((t, A_COLS), BF16)] + res_shapes + [
                   jax.ShapeDtypeStruct((t, B_COLS), F32),
                   jax.ShapeDtypeStruct((t, C_COLS), BF16)],
        scratch_shapes=[pltpu.VMEM((A_COLS // LANES, tm, LANES), F32)],
        compiler_params=_cparams("arbitrary"),
        name="in_proj",
    )(x2d, w_p)
    return outs[0], list(outs[1:-2]), outs[-2], outs[-1]


def _band_attn_body(*refs, width, max_dist, has_sink, want_lse):
    q_ref, kp_ref, kc_ref, vp_ref, vc_ref, bias_ref = refs[:6]
    rest = refs[6:]
    if has_sink:
        sink_ref, rest = rest[0], rest[1:]
    o_ref = rest[0]
    lse_ref = rest[1] if want_lse else None

    first_tile = pl.program_id(2) == 0
    row = lax.broadcasted_iota(I32, (BAND, 2 * BAND), 0)
    col = lax.broadcasted_iota(I32, (BAND, 2 * BAND), 1)
    dist = row + BAND - col
    in_band = (dist >= 0) & (dist <= max_dist)
    first_mask = in_band & ((col >= BAND) | jnp.logical_not(first_tile))
    low_half = lax.broadcasted_iota(I32, (BAND, LANES), 1) < HEAD_DIM
    scale = HEAD_DIM ** -0.5

    for qb in range(q_ref.shape[0] // BAND):
        rows = slice(qb * BAND, (qb + 1) * BAND)
        mask = first_mask if qb == 0 else in_band
        for tile in range(width // LANES):
            sl = slice(tile * LANES, (tile + 1) * LANES)
            q2 = q_ref[rows, sl]
            if qb == 0:
                k2 = jnp.concatenate([kp_ref[:, sl], kc_ref[:BAND, sl]], axis=0)
                v2 = jnp.concatenate([vp_ref[:, sl], vc_ref[:BAND, sl]], axis=0)
            else:
                k2 = kc_ref[(qb - 1) * BAND:(qb + 1) * BAND, sl]
                v2 = vc_ref[(qb - 1) * BAND:(qb + 1) * BAND, sl]
            outs, lses = [], []
            for half in range(2):
                h = 2 * tile + half
                qm = jnp.where(low_half if half == 0 else jnp.logical_not(low_half), q2, jnp.zeros_like(q2))
                s = lax.dot_general(qm, k2, (((1,), (1,)), ((), ())), preferred_element_type=F32)
                s = s * scale + bias_ref[h]
                s = jnp.where(mask, s, MASK_VALUE)
                m = jnp.max(s, axis=-1, keepdims=True)
                if has_sink:
                    sink = sink_ref[h]
                    m = jnp.maximum(m, sink)
                p = jnp.exp(s - m)
                den = jnp.sum(p, axis=-1, keepdims=True)
                if has_sink:
                    den = den + jnp.exp(sink - m)
                pv = jnp.dot(p.astype(BF16), v2, preferred_element_type=F32)
                outs.append(pv / den)
                if want_lse:
                    lses.append(jnp.broadcast_to(m + jnp.log(den), (BAND, LANES)))
            o_ref[rows, sl] = jnp.where(low_half, outs[0], outs[1])
            if want_lse:
                lse_ref[rows, sl] = jnp.where(low_half, lses[0], lses[1])


def _band_attn(src, bias, *, width, max_dist, sinks=None, want_lse):
    nb, r, length, _ = src.shape
    qt = min(BAND_Q_TILE, length)
    bands = qt // BAND
    heads = width // HEAD_DIM
    has_sink = sinks is not None

    def cur(off):
        return pl.BlockSpec((None, None, qt, width), lambda b, p, i: (b, p, i, off))

    def prev(off):
        return pl.BlockSpec((None, None, BAND, width), lambda b, p, i: (b, p, jnp.maximum(i * bands - 1, 0), off))

    in_specs = [cur(0), prev(1), cur(1), prev(2), cur(2),
                pl.BlockSpec((heads, BAND, 2 * BAND), lambda b, p, i: (0, 0, 0))]
    args = [src, src, src, src, src, bias]
    if has_sink:
        in_specs.append(pl.BlockSpec(memory_space=pltpu.SMEM))
        args.append(sinks)
    out_spec = pl.BlockSpec((None, None, qt, width), lambda b, p, i: (b, p, i, 0))
    out_sds = jax.ShapeDtypeStruct((nb, r, length, width), F32)
    body = functools.partial(_band_attn_body, width=width, max_dist=max_dist, has_sink=has_sink, want_lse=want_lse)
    return pl.pallas_call(
        body,
        grid=(nb, r, length // qt),
        in_specs=in_specs,
        out_specs=[out_spec, out_spec] if want_lse else out_spec,
        out_shape=[out_sds, out_sds] if want_lse else out_sds,
        compiler_params=_cparams("arbitrary", "arbitrary", "arbitrary"),
        name="band_attn_r%d_w%d" % (r, width),
    )(*args)


def _hgrn_body(b_ref, lb_ref, ng_ref, w_ref, o_ref, state_ref):
    c = HG_CHUNK
    kd = B_KEY_DIM

    @pl.when(pl.program_id(1) == 0)
    def _():
        state_ref[...] = jnp.zeros_like(state_ref)

    trow = lax.broadcasted_iota(I32, (c, 1), 0)
    odd_row = (trow % 2) == 1
    low_sub = lax.broadcasted_iota(I32, (SUBLANES, 1), 0) < SUBLANES // 2
    ti = lax.broadcasted_iota(I32, (c, c), 0)
    si = lax.broadcasted_iota(I32, (c, c), 1)
    nt = (((1,), (1,)), ((), ()))
    tri = w_ref[...]

    in_level = [((ti // m) == (si // m)) & ((ti % m) >= (m // 2)) & ((si % m) < (m // 2))
                for m in (c >> lvl for lvl in range(HG_LEVELS))]

    states = [state_ref[h] for h in range(B_HEADS)]
    for chunk, h in [(ci, hi) for ci in range(b_ref.shape[1] // c) for hi in range(B_HEADS)]:
        rows = slice(chunk * c, (chunk + 1) * c)
        ks = slice(h * kd, (h + 1) * kd)
        q = b_ref[0, rows, ks]
        f = b_ref[0, rows, B_KEY_WIDTH + h * kd:B_KEY_WIDTH + (h + 1) * kd]
        inp = b_ref[0, rows, 2 * B_KEY_WIDTH + h * kd:2 * B_KEY_WIDTH + (h + 1) * kd]
        gate = b_ref[0, rows, 3 * B_KEY_WIDTH + h * kd:3 * B_KEY_WIDTH + (h + 1) * kd]
        lb = lb_ref[:, ks]

        a = jnp.exp(-jnp.abs(f))
        big = 1.0 / (1.0 + a)
        small = a * big
        pos = f >= 0.0
        forget = lb + (1.0 - lb) * jnp.where(pos, big, small)
        log_f = jnp.log(forget)
        key = (1.0 - lb) * jnp.where(pos, small, big)
        qs = q * jax.nn.sigmoid(q)

        g_hi = log_f.astype(BF16)
        g_lo = (log_f - g_hi.astype(F32)).astype(BF16)
        cum2 = jnp.dot(tri, jnp.concatenate([g_hi, g_lo], axis=1), preferred_element_type=F32)
        cum = cum2[:, :kd] + cum2[:, kd:]
        last = cum[c - 1:c]

        q_dec = (qs * jnp.exp(cum)).astype(BF16)
        k_dec = (key * jnp.exp(last - cum)).astype(BF16)
        inp_b = inp.astype(BF16)

        def level_decay(m):
            if m == 2:
                return jnp.where(odd_row, forget, 1.0)
            if m == SUBLANES // 2:
                row_bcast = lambda i: jnp.broadcast_to(cum[i:i + 1], (SUBLANES, kd))
                pieces = [jnp.where(low_sub, row_bcast(SUBLANES * j + m // 2 - 1), row_bcast(SUBLANES * j + m + m // 2 - 1))
                          for j in range(c // SUBLANES)]
            else:
                pieces = [jnp.broadcast_to(cum[b * m + m // 2 - 1:b * m + m // 2], (m, kd)) for b in range(c // m)]
            return jnp.exp(-jnp.abs(cum - jnp.concatenate(pieces, axis=0)))

        st = states[h]
        inter = lax.dot_general(q_dec, st.astype(BF16), nt, preferred_element_type=F32)

        scores = jnp.where(ti == si,
                           lax.dot_general(qs.astype(BF16), key.astype(BF16), nt, preferred_element_type=F32), 0.0)
        m = c
        for lvl in range(HG_LEVELS):
            el = level_decay(m)
            sl = lax.dot_general((qs * el).astype(BF16), (key * el).astype(BF16), nt, preferred_element_type=F32)
            scores = jnp.where(in_level[lvl], sl, scores)
            m //= 2
        intra = jnp.dot(scores.astype(BF16), inp_b, preferred_element_type=F32)

        new_st = st * jnp.exp(last) + lax.dot_general(
            inp_b, k_dec, (((0,), (0,)), ((), ())), preferred_element_type=F32)
        states[h] = new_st

        o = inter + intra
        ms = jnp.sum(o * o, axis=-1, keepdims=True) * (1.0 / B_VAL_DIM)
        o = o * lax.rsqrt(ms + 1e-6)
        o_ref[0, rows, ks] = o * ng_ref[:, ks] * (gate * jax.nn.sigmoid(gate))

    for h in range(B_HEADS):
        state_ref[h] = states[h]


def _hgrn(b_all, lower_bound, norm_g_pad, w_sum):
    nb, s, _ = b_all.shape
    c = HG_STEP_ROWS
    return pl.pallas_call(
        _hgrn_body,
        grid=(nb, s // c),
        in_specs=[pl.BlockSpec((1, c, B_COLS), lambda b, i: (b, i, 0)),
                  pl.BlockSpec((1, B_KEY_WIDTH), lambda b, i: (0, 0)),
                  pl.BlockSpec((1, B_PAD_WIDTH), lambda b, i: (0, 0)),
                  pl.BlockSpec(w_sum.shape, lambda b, i: (0, 0))],
        out_specs=pl.BlockSpec((1, c, B_PAD_WIDTH), lambda b, i: (b, i, 0)),
        out_shape=jax.ShapeDtypeStruct((nb, s, B_PAD_WIDTH), F32),
        scratch_shapes=[pltpu.VMEM((B_HEADS, B_VAL_PAD, B_KEY_DIM), F32)],
        compiler_params=_cparams("arbitrary", "arbitrary"),
        name="hgrn2",
    )(b_all, lower_bound, norm_g_pad, w_sum)


def _layer_norm_rows(z, g, b):
    mu = jnp.mean(z, axis=-1, keepdims=True)
    zc = z - mu
    var = jnp.mean(zc * zc, axis=-1, keepdims=True)
    return zc * lax.rsqrt(var + LN_EPS) * g + b


def _mix_out_body(*refs):
    n_pat = len(A_PATTERNS)
    o_refs, l_refs = refs[:n_pat], refs[n_pat:2 * n_pat]
    ob, oc, x_ref, w_ref, g_ref, b_ref, out_ref = refs[2 * n_pat:2 * n_pat + 7]
    scratch = refs[2 * n_pat + 7:]
    tm = x_ref.shape[0]
    n_slabs = A_WIDTH // LANES

    def token_order(ref, r, scr):
        if r == 1:
            return [ref[0, 0, :, j * LANES:(j + 1) * LANES] for j in range(n_slabs)]
        for p in range(r):
            for j in range(n_slabs):
                scr[j, pl.ds(p, tm // r, stride=r), :] = ref[0, p, :, j * LANES:(j + 1) * LANES]
        return [scr[j] for j in range(n_slabs)]

    scr_iter = iter(scratch)
    dil = [r for _, r in A_PATTERNS]
    o_slabs = [token_order(ref, r, None if r == 1 else next(scr_iter)) for ref, r in zip(o_refs, dil)]
    l_slabs = [token_order(ref, r, None if r == 1 else next(scr_iter)) for ref, r in zip(l_refs, dil)]

    merged = []
    for j in range(n_slabs):
        ls = [l[j] for l in l_slabs]
        m = functools.reduce(jnp.maximum, ls)
        ws = [jnp.exp(l - m) for l in ls]
        num = functools.reduce(lambda a, b: a + b, [w * o[j] for w, o in zip(ws, o_slabs)])
        merged.append(num / functools.reduce(lambda a, b: a + b, ws))
    cat = jnp.concatenate(merged + [ob[...], oc[...]], axis=1).astype(BF16)
    y = jnp.dot(cat, w_ref[...], preferred_element_type=F32)
    z = DEEPNORM_ALPHA * x_ref[...] + y
    out_ref[...] = _layer_norm_rows(z, g_ref[...], b_ref[...])


def _mix_out(o_list, l_list, ob, oc, x2d, w_out_p, ln_g, ln_b):
    t, d = x2d.shape
    tm = MIX_OUT_ROWS
    tiles = o_list[0].shape[1] * o_list[0].shape[2] // tm
    row = lambda w: pl.BlockSpec((tm, w), lambda i: (i, 0))
    full = lambda a: pl.BlockSpec(a.shape, lambda i: (0, 0))
    res = lambda a: pl.BlockSpec((1, a.shape[1], tm // a.shape[1], A_WIDTH), lambda i: (i // tiles, 0, i % tiles, 0))
    n_scratch = 2 * sum(1 for _, r in A_PATTERNS if r > 1)
    return pl.pallas_call(
        _mix_out_body,
        grid=(t // tm,),
        in_specs=[res(a) for a in o_list + l_list] + [row(B_PAD_WIDTH), row(C_WIDTH), row(d), full(w_out_p),
                                                      full(ln_g), full(ln_b)],
        out_specs=row(d),
        out_shape=jax.ShapeDtypeStruct((t, d), F32),
        scratch_shapes=[pltpu.VMEM((A_WIDTH // LANES, tm, LANES), F32)] * n_scratch,
        compiler_params=_cparams("arbitrary"),
        name="mix_out_ln",
    )(*o_list, *l_list, ob, oc, x2d, w_out_p, ln_g, ln_b)


def _router_body(h_ref, rw_ref, bias_ref, tri_ref, e_ref, rank_ref, gate_ref, cnt_ref, hp_ref, carry_ref):
    tn = h_ref.shape[0]
    per_group = N_EXPERTS // N_GROUPS
    neg_inf = -jnp.inf

    @pl.when(pl.program_id(0) == 0)
    def _():
        carry_ref[...] = jnp.zeros_like(carry_ref)

    hp_ref[...] = _pack_bf16_pairs(h_ref[...])

    h = h_ref[...]
    h_hi = h.astype(BF16)
    h_lo = (h - h_hi.astype(F32)).astype(BF16)
    nt = (((1,), (1,)), ((), ()))
    w_hi, w_lo = rw_ref[0], rw_ref[1]
    logits = (lax.dot_general(w_hi, h_hi, nt, preferred_element_type=F32)
              + lax.dot_general(w_hi, h_lo, nt, preferred_element_type=F32)
              + lax.dot_general(w_lo, h_hi, nt, preferred_element_type=F32))
    scores = jax.nn.sigmoid(logits)
    choice = scores + bias_ref[...]

    def first_max(vals, idx, sentinel):
        top = jnp.max(vals, axis=0, keepdims=True)
        return top, jnp.min(jnp.where(vals == top, idx, sentinel), axis=0, keepdims=True)

    li = lax.broadcasted_iota(I32, (per_group, tn), 0).astype(F32)
    group_rows = []
    for g in range(N_GROUPS):
        cg = choice[g * per_group:(g + 1) * per_group]
        m1, first = first_max(cg, li, float(per_group))
        m2 = jnp.max(jnp.where(li == first, neg_inf, cg), axis=0, keepdims=True)
        group_rows.append(m1 + m2)
    group_score = jnp.concatenate(group_rows, axis=0)

    gi = lax.broadcasted_iota(I32, (N_GROUPS, tn), 0).astype(F32)
    group_ok = jnp.zeros((N_GROUPS, tn), F32)
    cur = group_score
    for _ in range(TOPK_GROUPS):
        _, first = first_max(cur, gi, float(N_GROUPS))
        pick = gi == first
        group_ok = jnp.where(pick, 1.0, group_ok)
        cur = jnp.where(pick, neg_inf, cur)

    cur = jnp.concatenate(
        [jnp.where(group_ok[g:g + 1] > 0.0, choice[g * per_group:(g + 1) * per_group], MASK_VALUE)
         for g in range(N_GROUPS)], axis=0)
    ei = lax.broadcasted_iota(I32, (N_EXPERTS, tn), 0).astype(F32)
    chosen = jnp.zeros((N_EXPERTS, tn), F32)
    picks, gates = [], []
    for _ in range(TOP_K):
        _, idx = first_max(cur, ei, float(N_EXPERTS))
        pick = ei == idx
        picks.append(idx)
        gates.append(jnp.sum(jnp.where(pick, scores, 0.0), axis=0, keepdims=True))
        chosen = jnp.where(pick, 1.0, chosen)
        cur = jnp.where(pick, neg_inf, cur)

    gate = jnp.concatenate(gates, axis=0)
    gate_ref[...] = gate / jnp.sum(gate, axis=0, keepdims=True) * ROUTED_SCALE
    e_ref[...] = jnp.concatenate(picks, axis=0).astype(I32)

    before = jnp.dot(chosen.astype(BF16), tri_ref[...], preferred_element_type=F32) + carry_ref[...]
    ranks = [jnp.sum(jnp.where(ei == idx, before, 0.0), axis=0, keepdims=True) for idx in picks]
    rank_ref[...] = jnp.concatenate(ranks, axis=0).astype(I32)
    carry = carry_ref[...] + jnp.sum(chosen, axis=1, keepdims=True)
    carry_ref[...] = carry
    cnt_ref[...] = carry.astype(I32)


def _router(h2d, rw_t, bias_col):
    t, d = h2d.shape
    tn = ROUTER_TOKENS
    tri = jnp.asarray(np.triu(np.ones((tn, tn), np.float32), k=1), dtype=BF16)
    rw_hi = rw_t.astype(BF16)
    rw_split = jnp.stack([rw_hi, (rw_t - rw_hi.astype(F32)).astype(BF16)])
    tok = lambda: pl.BlockSpec((TOP_K, tn), lambda i: (0, i))
    return pl.pallas_call(
        _router_body,
        grid=(t // tn,),
        in_specs=[pl.BlockSpec((tn, d), lambda i: (i, 0)),
                  pl.BlockSpec((2, N_EXPERTS, d), lambda i: (0, 0, 0)),
                  pl.BlockSpec((N_EXPERTS, 1), lambda i: (0, 0)),
                  pl.BlockSpec((tn, tn), lambda i: (0, 0))],
        out_specs=[tok(), tok(), tok(), pl.BlockSpec((N_EXPERTS, 1), lambda i: (0, 0)),
                   pl.BlockSpec((tn, d // 2), lambda i: (i, 0))],
        out_shape=[jax.ShapeDtypeStruct((TOP_K, t), I32),
                   jax.ShapeDtypeStruct((TOP_K, t), I32),
                   jax.ShapeDtypeStruct((TOP_K, t), F32),
                   jax.ShapeDtypeStruct((N_EXPERTS, 1), I32),
                   jax.ShapeDtypeStruct((t, d // 2), U32)],
        scratch_shapes=[pltpu.VMEM((N_EXPERTS, 1), F32)],
        compiler_params=_cparams("arbitrary"),
        name="moe_router",
    )(h2d, rw_split, bias_col, tri)


def _pack_bf16_pairs(x):
    w = x.shape[1] // 2
    hi = lax.bitcast_convert_type(x[:, :w].astype(BF16).astype(F32), U32)
    lo = lax.bitcast_convert_type(x[:, w:].astype(BF16).astype(F32), U32)
    return hi | (lo >> 16)


def _unpack_bf16_pairs(p):
    hi = lax.bitcast_convert_type(p & jnp.uint32(0xFFFF0000), F32)
    lo = lax.bitcast_convert_type(p << 16, F32)
    return hi, lo


def _dest_body(e_ref, rank_ref, offs_ref, dest_ref):
    tn = e_ref.shape[1]
    ei = lax.broadcasted_iota(I32, (N_EXPERTS, tn), 0)
    offs = offs_ref[...]
    rows = [jnp.sum(jnp.where(ei == e_ref[k:k + 1, :], offs, 0.0), axis=0, keepdims=True) for k in range(TOP_K)]
    dest_ref[...] = jnp.concatenate(rows, axis=0).astype(I32) + rank_ref[...]


def _dest_rows(e_t, rank_t, offsets):
    t = e_t.shape[1]
    tn = DEST_TOKENS
    tok = pl.BlockSpec((TOP_K, tn), lambda i: (0, i))
    return pl.pallas_call(
        _dest_body,
        grid=(t // tn,),
        in_specs=[tok, tok, pl.BlockSpec((N_EXPERTS, 1), lambda i: (0, 0))],
        out_specs=tok,
        out_shape=jax.ShapeDtypeStruct((TOP_K, t), I32),
        compiler_params=_cparams("arbitrary"),
        name="moe_dest",
    )(e_t, rank_t, offsets.astype(F32).reshape(N_EXPERTS, 1))


def _expert_body(offs_ref, nblk_ref, cnt_ref, layer_ref, xs_hbm, wg_ref, wu_ref, wd_ref, ys_hbm,
                 xbuf, ybuf, wg_b, wu_b, wd_b, sem_in, sem_out):
    e = pl.program_id(0)
    slots, bm = xbuf.shape[:2]
    ahead = slots - 1
    n = nblk_ref[e]
    first = offs_ref[e] // bm
    total = (offs_ref[N_EXPERTS - 1] // bm) + nblk_ref[N_EXPERTS - 1]

    def rows(g):
        return pl.ds(pl.multiple_of(g * bm, bm), bm)

    def x_copy(g, slot):
        return pltpu.make_async_copy(xs_hbm.at[rows(g)], xbuf.at[slot], sem_in.at[slot])

    def y_copy(g, slot):
        return pltpu.make_async_copy(ybuf.at[slot], ys_hbm.at[rows(g)], sem_out.at[slot])

    @pl.when(e == 0)
    def _():
        for g0 in range(ahead):
            @pl.when(g0 < total)
            def _():
                x_copy(g0, g0).start()

    wg_b[...] = wg_ref[0, 0].astype(BF16)
    wu_b[...] = wu_ref[0, 0].astype(BF16)
    wd_b[...] = wd_ref[0, 0].astype(BF16)

    def block(j, carry):
        g = first + j
        slot = g % slots
        x_copy(g, slot).wait()

        @pl.when(g + ahead < total)
        def _():
            x_copy(g + ahead, (g + ahead) % slots).start()

        @pl.when(g >= slots)
        def _():
            y_copy(g - slots, slot).wait()

        live = lax.broadcasted_iota(I32, (bm, 1), 0) < (cnt_ref[e] - j * bm)
        hi, lo = _unpack_bf16_pairs(jnp.where(live, xbuf[slot], jnp.uint32(0)))
        xb = jnp.concatenate([hi.astype(BF16), lo.astype(BF16)], axis=1)
        gate = jnp.dot(xb, wg_b[...], preferred_element_type=F32)
        up = jnp.dot(xb, wu_b[...], preferred_element_type=F32)
        hidden = (gate * jax.nn.sigmoid(gate) * up).astype(BF16)
        ybuf[slot] = _pack_bf16_pairs(jnp.dot(hidden, wd_b[...], preferred_element_type=F32))
        y_copy(g, slot).start()
        return carry

    lax.fori_loop(0, n, block, 0)

    @pl.when(e == N_EXPERTS - 1)
    def _():
        for back in range(1, slots + 1):
            @pl.when(total >= back)
            def _():
                y_copy(total - back, (total - back) % slots).wait()


def _experts(xs, offsets, n_blk, counts, layer, w_gate, w_up, w_down):
    n_rows, dp = xs.shape
    bm = EXPERT_BLOCK
    d, ff = w_gate.shape[-2:]
    w_spec = lambda a, b: pl.BlockSpec((1, 1, a, b), lambda e, of, nb, ct, ly: (ly[0], e, 0, 0))
    grid_spec = pltpu.PrefetchScalarGridSpec(
        num_scalar_prefetch=4,
        grid=(N_EXPERTS,),
        in_specs=[pl.BlockSpec(memory_space=pl.ANY), w_spec(d, ff), w_spec(d, ff), w_spec(ff, d)],
        out_specs=pl.BlockSpec(memory_space=pl.ANY),
        scratch_shapes=[pltpu.VMEM((EXPERT_SLOTS, bm, dp), U32), pltpu.VMEM((EXPERT_SLOTS, bm, dp), U32),
                        pltpu.VMEM((d, ff), BF16), pltpu.VMEM((d, ff), BF16), pltpu.VMEM((ff, d), BF16),
                        pltpu.SemaphoreType.DMA((EXPERT_SLOTS,)), pltpu.SemaphoreType.DMA((EXPERT_SLOTS,))],
    )
    return pl.pallas_call(
        _expert_body,
        grid_spec=grid_spec,
        out_shape=jax.ShapeDtypeStruct((n_rows, dp), U32),
        compiler_params=_cparams("arbitrary"),
        name="moe_experts",
    )(offsets, n_blk, counts, layer, xs, w_gate, w_up, w_down)


SC_GATHER_ROWS = 64


def _sc_gather_rows(table, idx):
    info = plsc.get_sparse_core_info()
    nc, ns = info.num_cores, info.num_subcores
    workers = nc * ns
    n = idx.shape[0]
    w = table.shape[1]
    ch = SC_GATHER_ROWS
    per_worker = n // workers
    steps = per_worker // ch
    assert per_worker * workers == n and steps * ch == per_worker and steps % 2 == 0
    mesh = plsc.VectorSubcoreMesh(core_axis_name="c", subcore_axis_name="s")

    @functools.partial(
        pl.kernel, mesh=mesh,
        out_type=jax.ShapeDtypeStruct((n, w), table.dtype),
        scratch_types=[pltpu.VMEM((steps, ch), I32), pltpu.VMEM((2, ch, w), table.dtype),
                       pltpu.SemaphoreType.DMA((2,)), pltpu.SemaphoreType.DMA((2,))],
    )
    def gather_kernel(table_hbm, idx_hbm, out_hbm, idx_v, rows_v, gsem, wsem):
        wid = lax.axis_index("s") * nc + lax.axis_index("c")
        base = wid * per_worker
        pltpu.sync_copy(idx_hbm.at[wid], idx_v)

        def gather(i, slot):
            return pltpu.make_async_copy(table_hbm.at[idx_v.at[i]], rows_v.at[slot], gsem.at[slot])

        def write(i, slot):
            off = pl.multiple_of(base + i * ch, ch)
            return pltpu.make_async_copy(rows_v.at[slot], out_hbm.at[pl.ds(off, ch)], wsem.at[slot])

        gather(0, 0).start()

        @pl.loop(0, steps, step=2)
        def _(i):
            gather(i + 1, 1).start()
            gather(i, 0).wait()
            write(i, 0).start()
            write(i, 0).wait()

            @pl.when(i + 2 < steps)
            def _():
                gather(i + 2, 0).start()

            gather(i + 1, 1).wait()
            write(i + 1, 1).start()
            write(i + 1, 1).wait()

    return gather_kernel(table, idx.reshape(workers, steps, ch))


def _sc_scatter_rows(rows, idx, n_out):
    info = plsc.get_sparse_core_info()
    nc, ns = info.num_cores, info.num_subcores
    workers = nc * ns
    kk, t = idx.shape
    w = rows.shape[1]
    ch = SC_GATHER_ROWS
    per_worker = t // workers
    steps = per_worker // ch
    assert per_worker * workers == t and steps * ch == per_worker and steps % 2 == 0
    idx_w = idx.reshape(kk, workers, steps, ch).transpose(1, 0, 2, 3).reshape(workers, kk * steps, ch)
    mesh = plsc.VectorSubcoreMesh(core_axis_name="c", subcore_axis_name="s")

    @functools.partial(
        pl.kernel, mesh=mesh,
        out_type=jax.ShapeDtypeStruct((n_out, w), rows.dtype),
        scratch_types=[pltpu.VMEM((kk * steps, ch), I32), pltpu.VMEM((2, ch, w), rows.dtype),
                       pltpu.SemaphoreType.DMA((2,)), pltpu.SemaphoreType.DMA((2,))],
    )
    def scatter_kernel(rows_hbm, idx_hbm, out_hbm, idx_v, rows_v, rsem, ssem):
        wid = lax.axis_index("s") * nc + lax.axis_index("c")
        base = wid * per_worker
        pltpu.sync_copy(idx_hbm.at[wid], idx_v)

        def read(i, slot):
            off = pl.multiple_of(base + i * ch, ch)
            return pltpu.make_async_copy(rows_hbm.at[pl.ds(off, ch)], rows_v.at[slot], rsem.at[slot])

        def scatter(i, k, slot):
            return pltpu.make_async_copy(rows_v.at[slot], out_hbm.at[idx_v.at[k * steps + i]], ssem.at[slot])

        def scatter_all(i, slot):
            for k in range(kk):
                scatter(i, k, slot).start()
            for k in range(kk):
                scatter(i, k, slot).wait()

        read(0, 0).start()

        @pl.loop(0, steps, step=2)
        def _(i):
            read(i + 1, 1).start()
            read(i, 0).wait()
            scatter_all(i, 0)

            @pl.when(i + 2 < steps)
            def _():
                read(i + 2, 0).start()

            read(i + 1, 1).wait()
            scatter_all(i + 1, 1)

    return scatter_kernel(rows, idx_w)


def _combine_streamed_body(y_ref, gate_ref, h_ref, sg_ref, su_ref, sd_ref, g_ref, b_ref, *rest):
    out_ref = rest[-1]
    half = h_ref.shape[1] // 2
    h = h_ref[...]
    hb = h.astype(BF16)
    sg = jnp.dot(hb, sg_ref[...], preferred_element_type=F32)
    su = jnp.dot(hb, su_ref[...], preferred_element_type=F32)
    shared = jnp.dot((sg * jax.nn.sigmoid(sg) * su).astype(BF16), sd_ref[...], preferred_element_type=F32)
    z = DEEPNORM_ALPHA * h + shared
    z_hi, z_lo = z[:, :half], z[:, half:]
    for k in range(TOP_K):
        y_hi, y_lo = _unpack_bf16_pairs(y_ref[k])
        gate = gate_ref[:, k:k + 1]
        z_hi = z_hi + gate * y_hi
        z_lo = z_lo + gate * y_lo
    out_ref[...] = _layer_norm_rows(jnp.concatenate([z_hi, z_lo], axis=1), g_ref[...], b_ref[...])


def _combine_streamed(y_part, gate_tk, h2d, sh_gate, sh_up, sh_down, ln_g, ln_b, part, partial_out):
    t, d = h2d.shape
    tt = COMBINE_ROWS
    steps = y_part.shape[1] // tt
    first = part * steps
    full = lambda a: pl.BlockSpec(a.shape, lambda i: (0, 0))
    in_specs = [pl.BlockSpec((TOP_K, tt, d // 2), lambda i: (0, i, 0)),
                pl.BlockSpec((tt, TOP_K), lambda i: (first + i, 0)),
                pl.BlockSpec((tt, d), lambda i: (first + i, 0)),
                full(sh_gate), full(sh_up), full(sh_down), full(ln_g), full(ln_b)]
    args = [y_part, gate_tk, h2d, sh_gate, sh_up, sh_down, ln_g, ln_b]
    aliases = {}
    if partial_out is not None:
        in_specs.append(pl.BlockSpec(memory_space=pl.ANY))
        args.append(partial_out)
        aliases = {len(args) - 1: 0}
    return pl.pallas_call(
        _combine_streamed_body,
        grid=(steps,),
        in_specs=in_specs,
        out_specs=pl.BlockSpec((tt, d), lambda i: (first + i, 0)),
        out_shape=jax.ShapeDtypeStruct((t, d), F32),
        input_output_aliases=aliases,
        compiler_params=_cparams("arbitrary"),
        name="moe_combine_streamed_ln",
    )(*args)


def _mixer_sublayer(h2d, nb, s, w_in_p, w_out_p, a_biases, c_bias, lower_bound, norm_g_pad, sinks, w_sum, ln_g, ln_b):
    a_qkv, a_residue, b_all, c_qkv = _in_proj(h2d, w_in_p, nb, s)
    a_by_dilation = {1: a_qkv.reshape(nb, 1, s, A_COLS), **dict(zip(IN_PROJ_DILATIONS, a_residue))}
    o_list, l_list = [], []
    for (window, r), bias in zip(A_PATTERNS, a_biases):
        o, lse = _band_attn(a_by_dilation[r], bias, width=A_WIDTH, max_dist=window // r, want_lse=True)
        o_list.append(o)
        l_list.append(lse)
    oc = _band_attn(c_qkv.reshape(nb, 1, s, C_COLS), c_bias, width=C_WIDTH,
                    max_dist=C_WINDOW - 1, sinks=sinks, want_lse=False).reshape(nb * s, C_WIDTH)
    ob = _hgrn(b_all.reshape(nb, s, B_COLS), lower_bound, norm_g_pad, w_sum).reshape(nb * s, B_PAD_WIDTH)
    return _mix_out(o_list, l_list, ob, oc, h2d, w_out_p, ln_g, ln_b)


def _moe_sublayer(h2d, layer, router_w, router_bias, w_gate, w_up, w_down, sh_gate, sh_up, sh_down, ln_g, ln_b):
    t, d = h2d.shape
    bm = EXPERT_BLOCK
    e_t, rank_t, gate_t, counts, h_packed = _router(h2d, router_w.T.astype(F32),
                                                    router_bias.astype(F32).reshape(N_EXPERTS, 1))

    counts = counts.reshape(N_EXPERTS)
    padded = (counts + bm - 1) // bm * bm
    pad_end = jnp.cumsum(padded).astype(I32)
    offsets = pad_end - padded
    n_blocks = -(-(t * TOP_K + N_EXPERTS * (bm - 1)) // bm)

    dest_t = _dest_rows(e_t, rank_t, offsets)
    xs = _sc_scatter_rows(h_packed, dest_t, n_blocks * bm)
    ys = _experts(xs, offsets, (padded // bm).astype(I32), counts, jnp.full((1,), layer, I32), w_gate, w_up, w_down)
    tp = t // COMBINE_PARTS
    gate_tk = gate_t.T
    shared_w = (sh_gate.astype(BF16), sh_up.astype(BF16), sh_down.astype(BF16))
    out = None
    for part in range(COMBINE_PARTS):
        idx = dest_t[:, part * tp:(part + 1) * tp].reshape(TOP_K * tp)
        y_part = _sc_gather_rows(ys, idx).reshape(TOP_K, tp, d // 2)
        out = _combine_streamed(y_part, gate_tk, h2d, *shared_w, ln_g, ln_b, part, out)
    return out


def kernel(x, w_in, w_out, rel_bias_table, lower_bound_logits, hgrn_norm_g, attn_sinks, ln1_g, ln1_b, router_w, router_bias, expert_w_gate, expert_w_up, expert_w_down, shared_w_gate, shared_w_up, shared_w_down, ln2_g, ln2_b):
    nb, s, d = x.shape
    depth = w_in.shape[0]
    lb_probs = jax.nn.softmax(lower_bound_logits.astype(F32), axis=0)
    lower_bounds = jnp.cumsum(lb_probs, axis=0) - lb_probs[0]
    rel_table = rel_bias_table.astype(F32)
    a_biases = [_band_bias(rel_table, r, 0, A_HEADS) for _, r in A_PATTERNS]
    c_bias = _band_bias(rel_table, 1, A_HEADS, A_HEADS + C_HEADS)
    w_sum = jnp.asarray(_hgrn_sum_matrix(), dtype=BF16)
    row = lambda v: v.astype(F32).reshape(1, -1)

    h = x.astype(F32).reshape(nb * s, d)
    for l in range(depth):
        h = _mixer_sublayer(h, nb, s, _prep_w_in(w_in[l]), _prep_w_out(w_out[l].astype(F32)), a_biases, c_bias,
                            lower_bounds[l].reshape(1, B_KEY_WIDTH), _pad_heads_vec(hgrn_norm_g[l].astype(F32)),
                            attn_sinks[l].astype(F32), w_sum, row(ln1_g[l]), row(ln1_b[l]))
        h = _moe_sublayer(h, l, router_w[l], router_bias[l], expert_w_gate, expert_w_up, expert_w_down,
                          shared_w_gate[l], shared_w_up[l], shared_w_down[l], row(ln2_g[l]), row(ln2_b[l]))
    return h.reshape(nb, s, d).astype(x.dtype)
```

```python
import functools
import math

import numpy as np
import jax
import jax.numpy as jnp
from jax import lax
from jax.experimental import pallas as pl
from jax.experimental.pallas import tpu as pltpu
from jax.experimental.pallas import tpu_sc as plsc

F32 = jnp.float32
BF16 = jnp.bfloat16
I32 = jnp.int32
U32 = jnp.uint32

LANES = 128
SUBLANES = 8
VMEM_LIMIT = 56 * 1024 * 1024

D_MODEL = 1024
DEPTH = 2
HEAD_DIM = 64
BAND = 128
BAND_Q_TILE = 1024
MASK_VALUE = -1e30

A_HEADS = 6
A_PATTERNS = ((128, 1), (512, 4), (2048, 16))
IN_PROJ_DILATIONS = tuple(r for _, r in A_PATTERNS if r > 1)
IN_PROJ_ROWS = 512
MIX_OUT_ROWS = 512
B_HEADS = 4
B_KEY_DIM = 128
B_VAL_DIM = 96
B_VAL_PAD = 128
HG_CHUNK = 64
HG_STEP_ROWS = 256
C_HEADS = 4
C_KV_HEADS = 2
C_WINDOW = 128

A_WIDTH = A_HEADS * HEAD_DIM
B_KEY_WIDTH = B_HEADS * B_KEY_DIM
B_WIDTH = B_HEADS * B_VAL_DIM
B_PAD_WIDTH = B_HEADS * B_VAL_PAD
C_WIDTH = C_HEADS * HEAD_DIM
C_KV_WIDTH = C_KV_HEADS * HEAD_DIM
IN_SPLITS = (A_WIDTH, A_WIDTH, A_WIDTH, B_KEY_WIDTH, B_KEY_WIDTH, B_WIDTH, B_WIDTH, C_WIDTH, C_KV_WIDTH, C_KV_WIDTH)

A_COLS = 3 * A_WIDTH
B_COLS = 4 * B_KEY_WIDTH
C_COLS = 3 * C_WIDTH
MIX_PAD_WIDTH = A_WIDTH + B_PAD_WIDTH + C_WIDTH

REL_BUCKETS = 32
REL_MAX_DIST = 2048

N_EXPERTS = 256
TOP_K = 8
N_GROUPS = 8
TOPK_GROUPS = 4
EXPERT_FF = 256
SHARED_FF = 256
ROUTED_SCALE = 2.5
ROUTER_TOKENS = 256
DEST_TOKENS = 2048
EXPERT_BLOCK = 512
EXPERT_SLOTS = 3
COMBINE_PARTS = 4
COMBINE_ROWS = 512

DEEPNORM_ALPHA = (2 * DEPTH) ** 0.25
LN_EPS = 1e-5


def _cparams(*sem):
    return pltpu.CompilerParams(dimension_semantics=sem, vmem_limit_bytes=VMEM_LIMIT)


def _prep_w_in(w_in):
    d = w_in.shape[0]
    split_at = [int(i) for i in np.cumsum(IN_SPLITS)[:-1]]
    aq, ak, av, bq, bf, bi, bg, cq, ck, cv = jnp.split(w_in, split_at, axis=-1)
    pad_v = lambda w: jnp.pad(w.reshape(d, B_HEADS, B_VAL_DIM), ((0, 0), (0, 0), (0, B_VAL_PAD - B_VAL_DIM))).reshape(d, B_PAD_WIDTH)
    rep = lambda w: jnp.repeat(w.reshape(d, C_KV_HEADS, HEAD_DIM), C_HEADS // C_KV_HEADS, axis=1).reshape(d, C_WIDTH)
    cols = [aq, ak, av, bq, bf, pad_v(bi), pad_v(bg), cq, rep(ck), rep(cv)]
    return jnp.concatenate(cols, axis=-1).astype(BF16)


def _prep_w_out(w_out):
    d = w_out.shape[1]
    wa = w_out[:A_WIDTH]
    wb = w_out[A_WIDTH:A_WIDTH + B_WIDTH].reshape(B_HEADS, B_VAL_DIM, d)
    wb = jnp.pad(wb, ((0, 0), (0, B_VAL_PAD - B_VAL_DIM), (0, 0))).reshape(B_PAD_WIDTH, d)
    wc = w_out[A_WIDTH + B_WIDTH:]
    return jnp.concatenate([wa, wb, wc], axis=0).astype(BF16)


def _pad_heads_vec(v):
    return jnp.pad(v.reshape(B_HEADS, B_VAL_DIM), ((0, 0), (0, B_VAL_PAD - B_VAL_DIM))).reshape(1, B_PAD_WIDTH)


def _rel_bucket(dist):
    max_exact = REL_BUCKETS // 2
    d = jnp.maximum(dist, 0)
    log_ratio = jnp.log(jnp.maximum(d, max_exact).astype(F32) / max_exact) / math.log(REL_MAX_DIST / max_exact)
    large = jnp.minimum(max_exact + (log_ratio * (REL_BUCKETS - max_exact)).astype(I32), REL_BUCKETS - 1)
    return jnp.where(d < max_exact, d, large)


def _band_bias(rel_table, r, head_lo, head_hi):
    dist = jnp.arange(BAND)[:, None] + BAND - jnp.arange(2 * BAND)[None, :]
    onehot = jax.nn.one_hot(_rel_bucket(dist * r), REL_BUCKETS, dtype=F32)
    return jnp.einsum("qkb,bh->hqk", onehot, rel_table[:, head_lo:head_hi], precision=lax.Precision.HIGHEST)


def _hgrn_sum_matrix():
    return np.tril(np.ones((HG_CHUNK, HG_CHUNK), np.float32))


HG_LEVELS = int(math.log2(HG_CHUNK))


def _in_proj_body(x_ref, w_ref, *rest):
    n_res = len(IN_PROJ_DILATIONS)
    a_ref, res_refs, (b_ref, c_ref, slabs) = rest[0], rest[1:1 + n_res], rest[1 + n_res:]
    tm = x_ref.shape[0]
    xb = x_ref[...].astype(BF16)
    a = jnp.dot(xb, w_ref[:, :A_COLS], preferred_element_type=F32)
    a_ref[...] = a.astype(BF16)
    for j in range(A_COLS // LANES):
        slabs[j] = a[:, j * LANES:(j + 1) * LANES]
    for r, ref in zip(IN_PROJ_DILATIONS, res_refs):
        for p in range(r):
            for j in range(A_COLS // LANES):
                ref[0, p, :, j * LANES:(j + 1) * LANES] = slabs[j, pl.ds(p, tm // r, stride=r), :].astype(BF16)
    for j in range(B_COLS // B_KEY_WIDTH):
        lo = A_COLS + j * B_KEY_WIDTH
        b_ref[:, j * B_KEY_WIDTH:(j + 1) * B_KEY_WIDTH] = jnp.dot(
            xb, w_ref[:, lo:lo + B_KEY_WIDTH], preferred_element_type=F32)
    c_ref[...] = jnp.dot(xb, w_ref[:, A_COLS + B_COLS:], preferred_element_type=F32).astype(BF16)


def _in_proj(x2d, w_p, nb, s):
    t, d = x2d.shape
    tm = IN_PROJ_ROWS
    n = w_p.shape[1]
    tiles = s // tm
    res_specs = [pl.BlockSpec((1, r, tm // r, A_COLS), lambda i: (i // tiles, 0, i % tiles, 0))
                 for r in IN_PROJ_DILATIONS]
    res_shapes = [jax.ShapeDtypeStruct((nb, r, s // r, A_COLS), BF16) for r in IN_PROJ_DILATIONS]
    outs = pl.pallas_call(
        _in_proj_body,
        grid=(t // tm,),
        in_specs=[pl.BlockSpec((tm, d), lambda i: (i, 0)),
                  pl.BlockSpec((d, n), lambda i: (0, 0))],
        out_specs=[pl.BlockSpec((tm, A_COLS), lambda i: (i, 0))] + res_specs + [
                   pl.BlockSpec((tm, B_COLS), lambda i: (i, 0)),
                   pl.BlockSpec((tm, C_COLS), lambda i: (i, 0))],
        out_shape=[jax.ShapeDtypeStruct((t, A_COLS), BF16)] + res_shapes + [
                   jax.ShapeDtypeStruct((t, B_COLS), F32),
                   jax.ShapeDtypeStruct((t, C_COLS), BF16)],
        scratch_shapes=[pltpu.VMEM((A_COLS // LANES, tm, LANES), F32)],
        compiler_params=_cparams("arbitrary"),
        name="in_proj",
    )(x2d, w_p)
    return outs[0], list(outs[1:-2]), outs[-2], outs[-1]


def _band_attn_body(*refs, width, max_dist, has_sink, want_lse):
    q_ref, kp_ref, kc_ref, vp_ref, vc_ref, bias_ref = refs[:6]
    rest = refs[6:]
    if has_sink:
        sink_ref, rest = rest[0], rest[1:]
    o_ref = rest[0]
    lse_ref = rest[1] if want_lse else None

    first_tile = pl.program_id(2) == 0
    row = lax.broadcasted_iota(I32, (BAND, 2 * BAND), 0)
    col = lax.broadcasted_iota(I32, (BAND, 2 * BAND), 1)
    dist = row + BAND - col
    in_band = (dist >= 0) & (dist <= max_dist)
    first_mask = in_band & ((col >= BAND) | jnp.logical_not(first_tile))
    lane = lax.broadcasted_iota(I32, (BAND, LANES), 1)
    low_half = lane < HEAD_DIM
    scale = HEAD_DIM ** -0.5

    for qb in range(q_ref.shape[0] // BAND):
        rows = slice(qb * BAND, (qb + 1) * BAND)
        mask = first_mask if qb == 0 else in_band
        lse_tile = jnp.zeros((BAND, LANES), F32)
        for tile in range(width // LANES):
            sl = slice(tile * LANES, (tile + 1) * LANES)
            q2 = q_ref[rows, sl]
            if qb == 0:
                k2 = jnp.concatenate([kp_ref[:, sl], kc_ref[:BAND, sl]], axis=0)
                v2 = jnp.concatenate([vp_ref[:, sl], vc_ref[:BAND, sl]], axis=0)
            else:
                k2 = kc_ref[(qb - 1) * BAND:(qb + 1) * BAND, sl]
                v2 = vc_ref[(qb - 1) * BAND:(qb + 1) * BAND, sl]
            outs, lses = [], []
            for half in range(2):
                h = 2 * tile + half
                qm = jnp.where(low_half if half == 0 else jnp.logical_not(low_half), q2, jnp.zeros_like(q2))
                s = lax.dot_general(qm, k2, (((1,), (1,)), ((), ())), preferred_element_type=F32)
                s = s * scale + bias_ref[h]
                s = jnp.where(mask, s, MASK_VALUE)
                m = jnp.max(s, axis=-1, keepdims=True)
                if has_sink:
                    sink = sink_ref[h]
                    m = jnp.maximum(m, sink)
                p = jnp.exp(s - m)
                den = jnp.sum(p, axis=-1, keepdims=True)
                if has_sink:
                    den = den + jnp.exp(sink - m)
                pv = jnp.dot(p.astype(BF16), v2, preferred_element_type=F32)
                outs.append(pv / den)
                if want_lse:
                    lse_tile = jnp.where(lane == h, m + jnp.log(den), lse_tile)
            o_ref[rows, sl] = jnp.where(low_half, outs[0], outs[1])
        if want_lse:
            lse_ref[rows, :] = lse_tile


def _band_attn(src, bias, *, width, max_dist, sinks=None, want_lse):
    nb, r, length, _ = src.shape
    qt = min(BAND_Q_TILE, length)
    bands = qt // BAND
    heads = width // HEAD_DIM
    has_sink = sinks is not None

    def cur(off):
        return pl.BlockSpec((None, None, qt, width), lambda b, p, i: (b, p, i, off))

    def prev(off):
        return pl.BlockSpec((None, None, BAND, width), lambda b, p, i: (b, p, jnp.maximum(i * bands - 1, 0), off))

    in_specs = [cur(0), prev(1), cur(1), prev(2), cur(2),
                pl.BlockSpec((heads, BAND, 2 * BAND), lambda b, p, i: (0, 0, 0))]
    args = [src, src, src, src, src, bias]
    if has_sink:
        in_specs.append(pl.BlockSpec(memory_space=pltpu.SMEM))
        args.append(sinks)
    out_spec = pl.BlockSpec((None, None, qt, width), lambda b, p, i: (b, p, i, 0))
    out_sds = jax.ShapeDtypeStruct((nb, r, length, width), F32)
    lse_spec = pl.BlockSpec((None, None, qt, LANES), lambda b, p, i: (b, p, i, 0))
    lse_sds = jax.ShapeDtypeStruct((nb, r, length, LANES), F32)
    body = functools.partial(_band_attn_body, width=width, max_dist=max_dist, has_sink=has_sink, want_lse=want_lse)
    return pl.pallas_call(
        body,
        grid=(nb, r, length // qt),
        in_specs=in_specs,
        out_specs=[out_spec, lse_spec] if want_lse else out_spec,
        out_shape=[out_sds, lse_sds] if want_lse else out_sds,
        compiler_params=_cparams("arbitrary", "arbitrary", "arbitrary"),
        name="band_attn_r%d_w%d" % (r, width),
    )(*args)


def _hgrn_body(b_ref, lb_ref, ng_ref, w_ref, o_ref, state_ref):
    c = HG_CHUNK
    kd = B_KEY_DIM

    @pl.when(pl.program_id(1) == 0)
    def _():
        state_ref[...] = jnp.zeros_like(state_ref)

    trow = lax.broadcasted_iota(I32, (c, 1), 0)
    odd_row = (trow % 2) == 1
    low_sub = lax.broadcasted_iota(I32, (SUBLANES, 1), 0) < SUBLANES // 2
    ti = lax.broadcasted_iota(I32, (c, c), 0)
    si = lax.broadcasted_iota(I32, (c, c), 1)
    nt = (((1,), (1,)), ((), ()))
    tri = w_ref[...]

    in_level = [((ti // m) == (si // m)) & ((ti % m) >= (m // 2)) & ((si % m) < (m // 2))
                for m in (c >> lvl for lvl in range(HG_LEVELS))]

    states = [state_ref[h] for h in range(B_HEADS)]
    for chunk, h in [(ci, hi) for ci in range(b_ref.shape[1] // c) for hi in range(B_HEADS)]:
        rows = slice(chunk * c, (chunk + 1) * c)
        ks = slice(h * kd, (h + 1) * kd)
        q = b_ref[0, rows, ks]
        f = b_ref[0, rows, B_KEY_WIDTH + h * kd:B_KEY_WIDTH + (h + 1) * kd]
        inp = b_ref[0, rows, 2 * B_KEY_WIDTH + h * kd:2 * B_KEY_WIDTH + (h + 1) * kd]
        gate = b_ref[0, rows, 3 * B_KEY_WIDTH + h * kd:3 * B_KEY_WIDTH + (h + 1) * kd]
        lb = lb_ref[:, ks]

        a = jnp.exp(-jnp.abs(f))
        big = 1.0 / (1.0 + a)
        small = a * big
        pos = f >= 0.0
        forget = lb + (1.0 - lb) * jnp.where(pos, big, small)
        log_f = jnp.log(forget)
        key = (1.0 - lb) * jnp.where(pos, small, big)
        qs = q * jax.nn.sigmoid(q)

        g_hi = log_f.astype(BF16)
        g_lo = (log_f - g_hi.astype(F32)).astype(BF16)
        cum2 = jnp.dot(tri, jnp.concatenate([g_hi, g_lo], axis=1), preferred_element_type=F32)
        cum = cum2[:, :kd] + cum2[:, kd:]
        last = cum[c - 1:c]

        q_dec = (qs * jnp.exp(cum)).astype(BF16)
        k_dec = (key * jnp.exp(last - cum)).astype(BF16)
        inp_b = inp.astype(BF16)

        def level_decay(m):
            if m == 2:
                return jnp.where(odd_row, forget, 1.0)
            if m == SUBLANES // 2:
                row_bcast = lambda i: jnp.broadcast_to(cum[i:i + 1], (SUBLANES, kd))
                pieces = [jnp.where(low_sub, row_bcast(SUBLANES * j + m // 2 - 1), row_bcast(SUBLANES * j + m + m // 2 - 1))
                          for j in range(c // SUBLANES)]
            else:
                pieces = [jnp.broadcast_to(cum[b * m + m // 2 - 1:b * m + m // 2], (m, kd)) for b in range(c // m)]
            return jnp.exp(-jnp.abs(cum - jnp.concatenate(pieces, axis=0)))

        st = states[h]
        inter = lax.dot_general(q_dec, st.astype(BF16), nt, preferred_element_type=F32)

        scores = jnp.where(ti == si,
                           lax.dot_general(qs.astype(BF16), key.astype(BF16), nt, preferred_element_type=F32), 0.0)
        m = c
        for lvl in range(HG_LEVELS):
            el = level_decay(m)
            sl = lax.dot_general((qs * el).astype(BF16), (key * el).astype(BF16), nt, preferred_element_type=F32)
            scores = jnp.where(in_level[lvl], sl, scores)
            m //= 2
        intra = jnp.dot(scores.astype(BF16), inp_b, preferred_element_type=F32)

        new_st = st * jnp.exp(last) + lax.dot_general(
            inp_b, k_dec, (((0,), (0,)), ((), ())), preferred_element_type=F32)
        states[h] = new_st

        o = inter + intra
        ms = jnp.sum(o * o, axis=-1, keepdims=True) * (1.0 / B_VAL_DIM)
        o = o * lax.rsqrt(ms + 1e-6)
        o_ref[0, rows, ks] = o * ng_ref[:, ks] * (gate * jax.nn.sigmoid(gate))

    for h in range(B_HEADS):
        state_ref[h] = states[h]


def _hgrn(b_all, lower_bound, norm_g_pad, w_sum):
    nb, s, _ = b_all.shape
    c = HG_STEP_ROWS
    return pl.pallas_call(
        _hgrn_body,
        grid=(nb, s // c),
        in_specs=[pl.BlockSpec((1, c, B_COLS), lambda b, i: (b, i, 0)),
                  pl.BlockSpec((1, B_KEY_WIDTH), lambda b, i: (0, 0)),
                  pl.BlockSpec((1, B_PAD_WIDTH), lambda b, i: (0, 0)),
                  pl.BlockSpec(w_sum.shape, lambda b, i: (0, 0))],
        out_specs=pl.BlockSpec((1, c, B_PAD_WIDTH), lambda b, i: (b, i, 0)),
        out_shape=jax.ShapeDtypeStruct((nb, s, B_PAD_WIDTH), F32),
        scratch_shapes=[pltpu.VMEM((B_HEADS, B_VAL_PAD, B_KEY_DIM), F32)],
        compiler_params=_cparams("arbitrary", "arbitrary"),
        name="hgrn2",
    )(b_all, lower_bound, norm_g_pad, w_sum)


def _layer_norm_rows(z, g, b):
    mu = jnp.mean(z, axis=-1, keepdims=True)
    zc = z - mu
    var = jnp.mean(zc * zc, axis=-1, keepdims=True)
    return zc * lax.rsqrt(var + LN_EPS) * g + b


def _mix_out_body(*refs):
    n_pat = len(A_PATTERNS)
    o_refs, l_refs = refs[:n_pat], refs[n_pat:2 * n_pat]
    ob, oc, x_ref, w_ref, g_ref, b_ref, spread_ref, out_ref = refs[2 * n_pat:2 * n_pat + 8]
    scratch = refs[2 * n_pat + 8:]
    tm = x_ref.shape[0]
    n_slabs = A_WIDTH // LANES

    def token_order(ref, r, scr):
        slabs = ref.shape[-1] // LANES
        if r == 1:
            return [ref[0, 0, :, j * LANES:(j + 1) * LANES] for j in range(slabs)]
        for p in range(r):
            for j in range(slabs):
                scr[j, pl.ds(p, tm // r, stride=r), :] = ref[0, p, :, j * LANES:(j + 1) * LANES]
        return [scr[j] for j in range(slabs)]

    scr_iter = iter(scratch)
    dil = [r for _, r in A_PATTERNS]
    o_slabs = [token_order(ref, r, None if r == 1 else next(scr_iter)) for ref, r in zip(o_refs, dil)]
    ls = [token_order(ref, r, None if r == 1 else next(scr_iter))[0] for ref, r in zip(l_refs, dil)]

    m = functools.reduce(jnp.maximum, ls)
    ws = [jnp.exp(l - m) for l in ls]
    inv = 1.0 / functools.reduce(lambda a, b: a + b, ws)
    spread = spread_ref[...]
    wide = []
    for w in ws:
        wn = w * inv
        hi = wn.astype(BF16)
        lo = (wn - hi.astype(F32)).astype(BF16)
        wide.append(jnp.dot(hi, spread, preferred_element_type=F32) + jnp.dot(lo, spread, preferred_element_type=F32))
    merged = [functools.reduce(lambda a, b: a + b,
                               [w[:, j * LANES:(j + 1) * LANES] * o[j] for w, o in zip(wide, o_slabs)])
              for j in range(n_slabs)]
    cat = jnp.concatenate(merged + [ob[...], oc[...]], axis=1).astype(BF16)
    y = jnp.dot(cat, w_ref[...], preferred_element_type=F32)
    z = DEEPNORM_ALPHA * x_ref[...] + y
    out_ref[...] = _layer_norm_rows(z, g_ref[...], b_ref[...])


def _mix_out(o_list, l_list, ob, oc, x2d, w_out_p, ln_g, ln_b):
    t, d = x2d.shape
    tm = MIX_OUT_ROWS
    tiles = o_list[0].shape[1] * o_list[0].shape[2] // tm
    row = lambda w: pl.BlockSpec((tm, w), lambda i: (i, 0))
    full = lambda a: pl.BlockSpec(a.shape, lambda i: (0, 0))
    res = lambda a: pl.BlockSpec((1, a.shape[1], tm // a.shape[1], a.shape[3]), lambda i: (i // tiles, 0, i % tiles, 0))
    spread = jnp.asarray(np.arange(LANES)[:, None] == np.arange(A_WIDTH)[None, :] // HEAD_DIM, dtype=BF16)
    dilated = [a for a in o_list + l_list if a.shape[1] > 1]
    return pl.pallas_call(
        _mix_out_body,
        grid=(t // tm,),
        in_specs=[res(a) for a in o_list + l_list] + [row(B_PAD_WIDTH), row(C_WIDTH), row(d), full(w_out_p),
                                                      full(ln_g), full(ln_b), full(spread)],
        out_specs=row(d),
        out_shape=jax.ShapeDtypeStruct((t, d), F32),
        scratch_shapes=[pltpu.VMEM((a.shape[3] // LANES, tm, LANES), F32) for a in dilated],
        compiler_params=_cparams("arbitrary"),
        name="mix_out_ln",
    )(*o_list, *l_list, ob, oc, x2d, w_out_p, ln_g, ln_b, spread)


def _router_body(h_ref, rw_ref, bias_ref, tri_ref, e_ref, rank_ref, gate_ref, cnt_ref, hp_ref, carry_ref):
    tn = h_ref.shape[0]
    per_group = N_EXPERTS // N_GROUPS
    neg_inf = -jnp.inf

    @pl.when(pl.program_id(0) == 0)
    def _():
        carry_ref[...] = jnp.zeros_like(carry_ref)

    hp_ref[...] = _pack_bf16_pairs(h_ref[...])

    h = h_ref[...]
    h_hi = h.astype(BF16)
    h_lo = (h - h_hi.astype(F32)).astype(BF16)
    nt = (((1,), (1,)), ((), ()))
    w_hi, w_lo = rw_ref[0], rw_ref[1]
    logits = (lax.dot_general(w_hi, h_hi, nt, preferred_element_type=F32)
              + lax.dot_general(w_hi, h_lo, nt, preferred_element_type=F32)
              + lax.dot_general(w_lo, h_hi, nt, preferred_element_type=F32))
    scores = jax.nn.sigmoid(logits)
    choice = scores + bias_ref[...]

    def first_max(vals, idx, sentinel):
        top = jnp.max(vals, axis=0, keepdims=True)
        return top, jnp.min(jnp.where(vals == top, idx, sentinel), axis=0, keepdims=True)

    li = lax.broadcasted_iota(I32, (per_group, tn), 0).astype(F32)
    group_rows = []
    for g in range(N_GROUPS):
        cg = choice[g * per_group:(g + 1) * per_group]
        m1, first = first_max(cg, li, float(per_group))
        m2 = jnp.max(jnp.where(li == first, neg_inf, cg), axis=0, keepdims=True)
        group_rows.append(m1 + m2)
    group_score = jnp.concatenate(group_rows, axis=0)

    gi = lax.broadcasted_iota(I32, (N_GROUPS, tn), 0).astype(F32)
    group_ok = jnp.zeros((N_GROUPS, tn), F32)
    cur = group_score
    for _ in range(TOPK_GROUPS):
        _, first = first_max(cur, gi, float(N_GROUPS))
        pick = gi == first
        group_ok = jnp.where(pick, 1.0, group_ok)
        cur = jnp.where(pick, neg_inf, cur)

    cur = jnp.concatenate(
        [jnp.where(group_ok[g:g + 1] > 0.0, choice[g * per_group:(g + 1) * per_group], MASK_VALUE)
         for g in range(N_GROUPS)], axis=0)
    ei = lax.broadcasted_iota(I32, (N_EXPERTS, tn), 0).astype(F32)
    chosen = jnp.zeros((N_EXPERTS, tn), F32)
    picks, gates = [], []
    for _ in range(TOP_K):
        _, idx = first_max(cur, ei, float(N_EXPERTS))
        pick = ei == idx
        picks.append(idx)
        gates.append(jnp.sum(jnp.where(pick, scores, 0.0), axis=0, keepdims=True))
        chosen = jnp.where(pick, 1.0, chosen)
        cur = jnp.where(pick, neg_inf, cur)

    gate = jnp.concatenate(gates, axis=0)
    gate_ref[...] = gate / jnp.sum(gate, axis=0, keepdims=True) * ROUTED_SCALE
    e_ref[...] = jnp.concatenate(picks, axis=0).astype(I32)

    before = jnp.dot(chosen.astype(BF16), tri_ref[...], preferred_element_type=F32) + carry_ref[...]
    ranks = [jnp.sum(jnp.where(ei == idx, before, 0.0), axis=0, keepdims=True) for idx in picks]
    rank_ref[...] = jnp.concatenate(ranks, axis=0).astype(I32)
    carry = carry_ref[...] + jnp.sum(chosen, axis=1, keepdims=True)
    carry_ref[...] = carry
    cnt_ref[...] = carry.astype(I32)


def _router(h2d, rw_t, bias_col):
    t, d = h2d.shape
    tn = ROUTER_TOKENS
    tri = jnp.asarray(np.triu(np.ones((tn, tn), np.float32), k=1), dtype=BF16)
    rw_hi = rw_t.astype(BF16)
    rw_split = jnp.stack([rw_hi, (rw_t - rw_hi.astype(F32)).astype(BF16)])
    tok = lambda: pl.BlockSpec((TOP_K, tn), lambda i: (0, i))
    return pl.pallas_call(
        _router_body,
        grid=(t // tn,),
        in_specs=[pl.BlockSpec((tn, d), lambda i: (i, 0)),
                  pl.BlockSpec((2, N_EXPERTS, d), lambda i: (0, 0, 0)),
                  pl.BlockSpec((N_EXPERTS, 1), lambda i: (0, 0)),
                  pl.BlockSpec((tn, tn), lambda i: (0, 0))],
        out_specs=[tok(), tok(), tok(), pl.BlockSpec((N_EXPERTS, 1), lambda i: (0, 0)),
                   pl.BlockSpec((tn, d // 2), lambda i: (i, 0))],
        out_shape=[jax.ShapeDtypeStruct((TOP_K, t), I32),
                   jax.ShapeDtypeStruct((TOP_K, t), I32),
                   jax.ShapeDtypeStruct((TOP_K, t), F32),
                   jax.ShapeDtypeStruct((N_EXPERTS, 1), I32),
                   jax.ShapeDtypeStruct((t, d // 2), U32)],
        scratch_shapes=[pltpu.VMEM((N_EXPERTS, 1), F32)],
        compiler_params=_cparams("arbitrary"),
        name="moe_router",
    )(h2d, rw_split, bias_col, tri)


def _pack_bf16_pairs(x):
    w = x.shape[1] // 2
    hi = lax.bitcast_convert_type(x[:, :w].astype(BF16).astype(F32), U32)
    lo = lax.bitcast_convert_type(x[:, w:].astype(BF16).astype(F32), U32)
    return hi | (lo >> 16)


def _unpack_bf16_pairs(p):
    hi = lax.bitcast_convert_type(p & jnp.uint32(0xFFFF0000), F32)
    lo = lax.bitcast_convert_type(p << 16, F32)
    return hi, lo


def _dest_body(e_ref, rank_ref, offs_ref, dest_ref):
    tn = e_ref.shape[1]
    ei = lax.broadcasted_iota(I32, (N_EXPERTS, tn), 0)
    offs = offs_ref[...]
    rows = [jnp.sum(jnp.where(ei == e_ref[k:k + 1, :], offs, 0.0), axis=0, keepdims=True) for k in range(TOP_K)]
    dest_ref[...] = jnp.concatenate(rows, axis=0).astype(I32) + rank_ref[...]


def _dest_rows(e_t, rank_t, offsets):
    t = e_t.shape[1]
    tn = DEST_TOKENS
    tok = pl.BlockSpec((TOP_K, tn), lambda i: (0, i))
    return pl.pallas_call(
        _dest_body,
        grid=(t // tn,),
        in_specs=[tok, tok, pl.BlockSpec((N_EXPERTS, 1), lambda i: (0, 0))],
        out_specs=tok,
        out_shape=jax.ShapeDtypeStruct((TOP_K, t), I32),
        compiler_params=_cparams("arbitrary"),
        name="moe_dest",
    )(e_t, rank_t, offsets.astype(F32).reshape(N_EXPERTS, 1))


def _expert_body(offs_ref, nblk_ref, cnt_ref, layer_ref, xs_hbm, wg_ref, wu_ref, wd_ref, ys_hbm,
                 xbuf, ybuf, wg_b, wu_b, wd_b, sem_in, sem_out):
    e = pl.program_id(0)
    slots, bm = xbuf.shape[:2]
    ahead = slots - 1
    n = nblk_ref[e]
    first = offs_ref[e] // bm
    total = (offs_ref[N_EXPERTS - 1] // bm) + nblk_ref[N_EXPERTS - 1]

    def rows(g):
        return pl.ds(pl.multiple_of(g * bm, bm), bm)

    def x_copy(g, slot):
        return pltpu.make_async_copy(xs_hbm.at[rows(g)], xbuf.at[slot], sem_in.at[slot])

    def y_copy(g, slot):
        return pltpu.make_async_copy(ybuf.at[slot], ys_hbm.at[rows(g)], sem_out.at[slot])

    @pl.when(e == 0)
    def _():
        for g0 in range(ahead):
            @pl.when(g0 < total)
            def _():
                x_copy(g0, g0).start()

    wg_b[...] = wg_ref[0, 0].astype(BF16)
    wu_b[...] = wu_ref[0, 0].astype(BF16)
    wd_b[...] = wd_ref[0, 0].astype(BF16)

    def block(j, carry):
        g = first + j
        slot = g % slots
        x_copy(g, slot).wait()

        @pl.when(g + ahead < total)
        def _():
            x_copy(g + ahead, (g + ahead) % slots).start()

        @pl.when(g >= slots)
        def _():
            y_copy(g - slots, slot).wait()

        live = lax.broadcasted_iota(I32, (bm, 1), 0) < (cnt_ref[e] - j * bm)
        hi, lo = _unpack_bf16_pairs(jnp.where(live, xbuf[slot], jnp.uint32(0)))
        xb = jnp.concatenate([hi.astype(BF16), lo.astype(BF16)], axis=1)
        gate = jnp.dot(xb, wg_b[...], preferred_element_type=F32)
        up = jnp.dot(xb, wu_b[...], preferred_element_type=F32)
        hidden = (gate * jax.nn.sigmoid(gate) * up).astype(BF16)
        ybuf[slot] = _pack_bf16_pairs(jnp.dot(hidden, wd_b[...], preferred_element_type=F32))
        y_copy(g, slot).start()
        return carry

    lax.fori_loop(0, n, block, 0)

    @pl.when(e == N_EXPERTS - 1)
    def _():
        for back in range(1, slots + 1):
            @pl.when(total >= back)
            def _():
                y_copy(total - back, (total - back) % slots).wait()


def _experts(xs, offsets, n_blk, counts, layer, w_gate, w_up, w_down):
    n_rows, dp = xs.shape
    bm = EXPERT_BLOCK
    d, ff = w_gate.shape[-2:]
    w_spec = lambda a, b: pl.BlockSpec((1, 1, a, b), lambda e, of, nb, ct, ly: (ly[0], e, 0, 0))
    grid_spec = pltpu.PrefetchScalarGridSpec(
        num_scalar_prefetch=4,
        grid=(N_EXPERTS,),
        in_specs=[pl.BlockSpec(memory_space=pl.ANY), w_spec(d, ff), w_spec(d, ff), w_spec(ff, d)],
        out_specs=pl.BlockSpec(memory_space=pl.ANY),
        scratch_shapes=[pltpu.VMEM((EXPERT_SLOTS, bm, dp), U32), pltpu.VMEM((EXPERT_SLOTS, bm, dp), U32),
                        pltpu.VMEM((d, ff), BF16), pltpu.VMEM((d, ff), BF16), pltpu.VMEM((ff, d), BF16),
                        pltpu.SemaphoreType.DMA((EXPERT_SLOTS,)), pltpu.SemaphoreType.DMA((EXPERT_SLOTS,))],
    )
    return pl.pallas_call(
        _expert_body,
        grid_spec=grid_spec,
        out_shape=jax.ShapeDtypeStruct((n_rows, dp), U32),
        compiler_params=_cparams("arbitrary"),
        name="moe_experts",
    )(offsets, n_blk, counts, layer, xs, w_gate, w_up, w_down)


SC_GATHER_ROWS = 64


def _sc_gather_rows(table, idx):
    info = plsc.get_sparse_core_info()
    nc, ns = info.num_cores, info.num_subcores
    workers = nc * ns
    n = idx.shape[0]
    w = table.shape[1]
    ch = SC_GATHER_ROWS
    per_worker = n // workers
    steps = per_worker // ch
    assert per_worker * workers == n and steps * ch == per_worker and steps % 2 == 0
    mesh = plsc.VectorSubcoreMesh(core_axis_name="c", subcore_axis_name="s")

    @functools.partial(
        pl.kernel, mesh=mesh,
        out_type=jax.ShapeDtypeStruct((n, w), table.dtype),
        scratch_types=[pltpu.VMEM((steps, ch), I32), pltpu.VMEM((2, ch, w), table.dtype),
                       pltpu.SemaphoreType.DMA((2,)), pltpu.SemaphoreType.DMA((2,))],
    )
    def gather_kernel(table_hbm, idx_hbm, out_hbm, idx_v, rows_v, gsem, wsem):
        wid = lax.axis_index("s") * nc + lax.axis_index("c")
        base = wid * per_worker
        pltpu.sync_copy(idx_hbm.at[wid], idx_v)

        def gather(i, slot):
            return pltpu.make_async_copy(table_hbm.at[idx_v.at[i]], rows_v.at[slot], gsem.at[slot])

        def write(i, slot):
            off = pl.multiple_of(base + i * ch, ch)
            return pltpu.make_async_copy(rows_v.at[slot], out_hbm.at[pl.ds(off, ch)], wsem.at[slot])

        gather(0, 0).start()

        @pl.loop(0, steps, step=2)
        def _(i):
            gather(i + 1, 1).start()
            gather(i, 0).wait()
            write(i, 0).start()
            write(i, 0).wait()

            @pl.when(i + 2 < steps)
            def _():
                gather(i + 2, 0).start()

            gather(i + 1, 1).wait()
            write(i + 1, 1).start()
            write(i + 1, 1).wait()

    return gather_kernel(table, idx.reshape(workers, steps, ch))


def _sc_scatter_rows(rows, idx, n_out):
    info = plsc.get_sparse_core_info()
    nc, ns = info.num_cores, info.num_subcores
    workers = nc * ns
    kk, t = idx.shape
    w = rows.shape[1]
    ch = SC_GATHER_ROWS
    per_worker = t // workers
    steps = per_worker // ch
    assert per_worker * workers == t and steps * ch == per_worker and steps % 2 == 0
    idx_w = idx.reshape(kk, workers, steps, ch).transpose(1, 0, 2, 3).reshape(workers, kk * steps, ch)
    mesh = plsc.VectorSubcoreMesh(core_axis_name="c", subcore_axis_name="s")

    @functools.partial(
        pl.kernel, mesh=mesh,
        out_type=jax.ShapeDtypeStruct((n_out, w), rows.dtype),
        scratch_types=[pltpu.VMEM((kk * steps, ch), I32), pltpu.VMEM((2, ch, w), rows.dtype),
                       pltpu.SemaphoreType.DMA((2,)), pltpu.SemaphoreType.DMA((2,))],
    )
    def scatter_kernel(rows_hbm, idx_hbm, out_hbm, idx_v, rows_v, rsem, ssem):
        wid = lax.axis_index("s") * nc + lax.axis_index("c")
        base = wid * per_worker
        pltpu.sync_copy(idx_hbm.at[wid], idx_v)

        def read(i, slot):
            off = pl.multiple_of(base + i * ch, ch)
            return pltpu.make_async_copy(rows_hbm.at[pl.ds(off, ch)], rows_v.at[slot], rsem.at[slot])

        def scatter(i, k, slot):
            return pltpu.make_async_copy(rows_v.at[slot], out_hbm.at[idx_v.at[k * steps + i]], ssem.at[slot])

        def scatter_all(i, slot):
            for k in range(kk):
                scatter(i, k, slot).start()
            for k in range(kk):
                scatter(i, k, slot).wait()

        read(0, 0).start()

        @pl.loop(0, steps, step=2)
        def _(i):
            read(i + 1, 1).start()
            read(i, 0).wait()
            scatter_all(i, 0)

            @pl.when(i + 2 < steps)
            def _():
                read(i + 2, 0).start()

            read(i + 1, 1).wait()
            scatter_all(i + 1, 1)

    return scatter_kernel(rows, idx_w)


def _combine_streamed_body(y_ref, gate_ref, h_ref, sg_ref, su_ref, sd_ref, g_ref, b_ref, *rest):
    out_ref = rest[-1]
    half = h_ref.shape[1] // 2
    h = h_ref[...]
    hb = h.astype(BF16)
    sg = jnp.dot(hb, sg_ref[...], preferred_element_type=F32)
    su = jnp.dot(hb, su_ref[...], preferred_element_type=F32)
    shared = jnp.dot((sg * jax.nn.sigmoid(sg) * su).astype(BF16), sd_ref[...], preferred_element_type=F32)
    z = DEEPNORM_ALPHA * h + shared
    z_hi, z_lo = z[:, :half], z[:, half:]
    for k in range(TOP_K):
        y_hi, y_lo = _unpack_bf16_pairs(y_ref[k])
        gate = gate_ref[:, k:k + 1]
        z_hi = z_hi + gate * y_hi
        z_lo = z_lo + gate * y_lo
    out_ref[...] = _layer_norm_rows(jnp.concatenate([z_hi, z_lo], axis=1), g_ref[...], b_ref[...])


def _combine_streamed(y_part, gate_tk, h2d, sh_gate, sh_up, sh_down, ln_g, ln_b, part, partial_out):
    t, d = h2d.shape
    tt = COMBINE_ROWS
    steps = y_part.shape[1] // tt
    first = part * steps
    full = lambda a: pl.BlockSpec(a.shape, lambda i: (0, 0))
    in_specs = [pl.BlockSpec((TOP_K, tt, d // 2), lambda i: (0, i, 0)),
                pl.BlockSpec((tt, TOP_K), lambda i: (first + i, 0)),
                pl.BlockSpec((tt, d), lambda i: (first + i, 0)),
                full(sh_gate), full(sh_up), full(sh_down), full(ln_g), full(ln_b)]
    args = [y_part, gate_tk, h2d, sh_gate, sh_up, sh_down, ln_g, ln_b]
    aliases = {}
    if partial_out is not None:
        in_specs.append(pl.BlockSpec(memory_space=pl.ANY))
        args.append(partial_out)
        aliases = {len(args) - 1: 0}
    return pl.pallas_call(
        _combine_streamed_body,
        grid=(steps,),
        in_specs=in_specs,
        out_specs=pl.BlockSpec((tt, d), lambda i: (first + i, 0)),
        out_shape=jax.ShapeDtypeStruct((t, d), F32),
        input_output_aliases=aliases,
        compiler_params=_cparams("arbitrary"),
        name="moe_combine_streamed_ln",
    )(*args)


def _mixer_sublayer(h2d, nb, s, w_in_p, w_out_p, a_biases, c_bias, lower_bound, norm_g_pad, sinks, w_sum, ln_g, ln_b):
    a_qkv, a_residue, b_all, c_qkv = _in_proj(h2d, w_in_p, nb, s)
    a_by_dilation = {1: a_qkv.reshape(nb, 1, s, A_COLS), **dict(zip(IN_PROJ_DILATIONS, a_residue))}
    o_list, l_list = [], []
    for (window, r), bias in zip(A_PATTERNS, a_biases):
        o, lse = _band_attn(a_by_dilation[r], bias, width=A_WIDTH, max_dist=window // r, want_lse=True)
        o_list.append(o)
        l_list.append(lse)
    oc = _band_attn(c_qkv.reshape(nb, 1, s, C_COLS), c_bias, width=C_WIDTH,
                    max_dist=C_WINDOW - 1, sinks=sinks, want_lse=False).reshape(nb * s, C_WIDTH)
    ob = _hgrn(b_all.reshape(nb, s, B_COLS), lower_bound, norm_g_pad, w_sum).reshape(nb * s, B_PAD_WIDTH)
    return _mix_out(o_list, l_list, ob, oc, h2d, w_out_p, ln_g, ln_b)


def _moe_sublayer(h2d, layer, router_w, router_bias, w_gate, w_up, w_down, sh_gate, sh_up, sh_down, ln_g, ln_b):
    t, d = h2d.shape
    bm = EXPERT_BLOCK
    e_t, rank_t, gate_t, counts, h_packed = _router(h2d, router_w.T.astype(F32),
                                                    router_bias.astype(F32).reshape(N_EXPERTS, 1))

    counts = counts.reshape(N_EXPERTS)
    padded = (counts + bm - 1) // bm * bm
    pad_end = jnp.cumsum(padded).astype(I32)
    offsets = pad_end - padded
    n_blocks = -(-(t * TOP_K + N_EXPERTS * (bm - 1)) // bm)

    dest_t = _dest_rows(e_t, rank_t, offsets)
    xs = _sc_scatter_rows(h_packed, dest_t, n_blocks * bm)
    ys = _experts(xs, offsets, (padded // bm).astype(I32), counts, jnp.full((1,), layer, I32), w_gate, w_up, w_down)
    tp = t // COMBINE_PARTS
    gate_tk = gate_t.T
    shared_w = (sh_gate.astype(BF16), sh_up.astype(BF16), sh_down.astype(BF16))
    out = None
    for part in range(COMBINE_PARTS):
        idx = dest_t[:, part * tp:(part + 1) * tp].reshape(TOP_K * tp)
        y_part = _sc_gather_rows(ys, idx).reshape(TOP_K, tp, d // 2)
        out = _combine_streamed(y_part, gate_tk, h2d, *shared_w, ln_g, ln_b, part, out)
    return out


def kernel(x, w_in, w_out, rel_bias_table, lower_bound_logits, hgrn_norm_g, attn_sinks, ln1_g, ln1_b, router_w, router_bias, expert_w_gate, expert_w_up, expert_w_down, shared_w_gate, shared_w_up, shared_w_down, ln2_g, ln2_b):
    nb, s, d = x.shape
    depth = w_in.shape[0]
    lb_probs = jax.nn.softmax(lower_bound_logits.astype(F32), axis=0)
    lower_bounds = jnp.cumsum(lb_probs, axis=0) - lb_probs[0]
    rel_table = rel_bias_table.astype(F32)
    a_biases = [_band_bias(rel_table, r, 0, A_HEADS) for _, r in A_PATTERNS]
    c_bias = _band_bias(rel_table, 1, A_HEADS, A_HEADS + C_HEADS)
    w_sum = jnp.asarray(_hgrn_sum_matrix(), dtype=BF16)
    row = lambda v: v.astype(F32).reshape(1, -1)

    h = x.astype(F32).reshape(nb * s, d)
    for l in range(depth):
        h = _mixer_sublayer(h, nb, s, _prep_w_in(w_in[l]), _prep_w_out(w_out[l].astype(F32)), a_biases, c_bias,
                            lower_bounds[l].reshape(1, B_KEY_WIDTH), _pad_heads_vec(hgrn_norm_g[l].astype(F32)),
                            attn_sinks[l].astype(F32), w_sum, row(ln1_g[l]), row(ln1_b[l]))
        h = _moe_sublayer(h, l, router_w[l], router_bias[l], expert_w_gate, expert_w_up, expert_w_down,
                          shared_w_gate[l], shared_w_up[l], shared_w_down[l], row(ln2_g[l]), row(ln2_b[l]))
    return h.reshape(nb, s, d).astype(x.dtype)
```

```python
import functools
import math

import numpy as np
import jax
import jax.numpy as jnp
from jax import lax
from jax.experimental import pallas as pl
from jax.experimental.pallas import tpu as pltpu
from jax.experimental.pallas import tpu_sc as plsc

F32 = jnp.float32
BF16 = jnp.bfloat16
I32 = jnp.int32
U32 = jnp.uint32

LANES = 128
SUBLANES = 8
VMEM_LIMIT = 56 * 1024 * 1024

D_MODEL = 1024
DEPTH = 2
HEAD_DIM = 64
BAND = 128
BAND_Q_TILE = 1024
MASK_VALUE = -1e30

A_HEADS = 6
A_PATTERNS = ((128, 1), (512, 4), (2048, 16))
IN_PROJ_DILATIONS = tuple(r for _, r in A_PATTERNS if r > 1)
IN_PROJ_ROWS = 512
MIX_OUT_ROWS = 512
B_HEADS = 4
B_KEY_DIM = 128
B_VAL_DIM = 96
B_VAL_PAD = 128
HG_CHUNK = 64
HG_STEP_ROWS = 512
C_HEADS = 4
C_KV_HEADS = 2
C_WINDOW = 128

A_WIDTH = A_HEADS * HEAD_DIM
B_KEY_WIDTH = B_HEADS * B_KEY_DIM
B_WIDTH = B_HEADS * B_VAL_DIM
B_PAD_WIDTH = B_HEADS * B_VAL_PAD
C_WIDTH = C_HEADS * HEAD_DIM
C_KV_WIDTH = C_KV_HEADS * HEAD_DIM
IN_SPLITS = (A_WIDTH, A_WIDTH, A_WIDTH, B_KEY_WIDTH, B_KEY_WIDTH, B_WIDTH, B_WIDTH, C_WIDTH, C_KV_WIDTH, C_KV_WIDTH)

A_COLS = 3 * A_WIDTH
B_COLS = 4 * B_KEY_WIDTH
C_COLS = 3 * C_WIDTH
MIX_PAD_WIDTH = A_WIDTH + B_PAD_WIDTH + C_WIDTH

REL_BUCKETS = 32
REL_MAX_DIST = 2048

N_EXPERTS = 256
TOP_K = 8
N_GROUPS = 8
TOPK_GROUPS = 4
EXPERT_FF = 256
SHARED_FF = 256
ROUTED_SCALE = 2.5
ROUTER_TOKENS = 256
DEST_TOKENS = 2048
EXPERT_BLOCK = 512
EXPERT_SLOTS = 3
COMBINE_PARTS = 4
COMBINE_ROWS = 512

DEEPNORM_ALPHA = (2 * DEPTH) ** 0.25
LN_EPS = 1e-5


def _cparams(*sem):
    return pltpu.CompilerParams(dimension_semantics=sem, vmem_limit_bytes=VMEM_LIMIT)


def _prep_w_in(w_in):
    d = w_in.shape[0]
    split_at = [int(i) for i in np.cumsum(IN_SPLITS)[:-1]]
    aq, ak, av, bq, bf, bi, bg, cq, ck, cv = jnp.split(w_in, split_at, axis=-1)
    pad_v = lambda w: jnp.pad(w.reshape(d, B_HEADS, B_VAL_DIM), ((0, 0), (0, 0), (0, B_VAL_PAD - B_VAL_DIM))).reshape(d, B_PAD_WIDTH)
    rep = lambda w: jnp.repeat(w.reshape(d, C_KV_HEADS, HEAD_DIM), C_HEADS // C_KV_HEADS, axis=1).reshape(d, C_WIDTH)
    cols = [aq, ak, av, bq, bf, pad_v(bi), pad_v(bg), cq, rep(ck), rep(cv)]
    return jnp.concatenate(cols, axis=-1).astype(BF16)


def _prep_w_out(w_out):
    d = w_out.shape[1]
    wa = w_out[:A_WIDTH]
    wb = w_out[A_WIDTH:A_WIDTH + B_WIDTH].reshape(B_HEADS, B_VAL_DIM, d)
    wb = jnp.pad(wb, ((0, 0), (0, B_VAL_PAD - B_VAL_DIM), (0, 0))).reshape(B_PAD_WIDTH, d)
    wc = w_out[A_WIDTH + B_WIDTH:]
    return jnp.concatenate([wa, wb, wc], axis=0).astype(BF16)


def _pad_heads_vec(v):
    return jnp.pad(v.reshape(B_HEADS, B_VAL_DIM), ((0, 0), (0, B_VAL_PAD - B_VAL_DIM))).reshape(1, B_PAD_WIDTH)


def _rel_bucket(dist):
    max_exact = REL_BUCKETS // 2
    d = jnp.maximum(dist, 0)
    log_ratio = jnp.log(jnp.maximum(d, max_exact).astype(F32) / max_exact) / math.log(REL_MAX_DIST / max_exact)
    large = jnp.minimum(max_exact + (log_ratio * (REL_BUCKETS - max_exact)).astype(I32), REL_BUCKETS - 1)
    return jnp.where(d < max_exact, d, large)


def _band_bias(rel_table, r, head_lo, head_hi):
    dist = jnp.arange(BAND)[:, None] + BAND - jnp.arange(2 * BAND)[None, :]
    onehot = jax.nn.one_hot(_rel_bucket(dist * r), REL_BUCKETS, dtype=F32)
    return jnp.einsum("qkb,bh->hqk", onehot, rel_table[:, head_lo:head_hi], precision=lax.Precision.HIGHEST)


def _hgrn_sum_matrix():
    return np.tril(np.ones((HG_CHUNK, HG_CHUNK), np.float32))


HG_LEVELS = int(math.log2(HG_CHUNK))


def _in_proj_body(x_ref, w_ref, *rest):
    n_res = len(IN_PROJ_DILATIONS)
    a_ref, res_refs, (b_ref, c_ref, slabs) = rest[0], rest[1:1 + n_res], rest[1 + n_res:]
    tm = x_ref.shape[0]
    xb = x_ref[...].astype(BF16)
    a = jnp.dot(xb, w_ref[:, :A_COLS], preferred_element_type=F32)
    a_ref[...] = a.astype(BF16)
    for j in range(A_COLS // LANES):
        slabs[0, j] = a[:, j * LANES:(j + 1) * LANES]
    prev_r = 1
    for level, (r, ref) in enumerate(zip(IN_PROJ_DILATIONS, res_refs)):
        step, n, n_prev = r // prev_r, tm // r, tm // prev_r
        src, dst = slabs.at[level % 2], slabs.at[(level + 1) % 2]
        for p in range(r):
            p_prev, q = p % prev_r, p // prev_r
            for j in range(A_COLS // LANES):
                piece = src[j, pl.ds(p_prev * n_prev + q, n, stride=step), :]
                if level + 1 < len(IN_PROJ_DILATIONS):
                    dst[j, p * n:(p + 1) * n, :] = piece
                ref[0, p, :, j * LANES:(j + 1) * LANES] = piece.astype(BF16)
        prev_r = r
    for j in range(B_COLS // B_KEY_WIDTH):
        lo = A_COLS + j * B_KEY_WIDTH
        b_ref[:, j * B_KEY_WIDTH:(j + 1) * B_KEY_WIDTH] = jnp.dot(
            xb, w_ref[:, lo:lo + B_KEY_WIDTH], preferred_element_type=F32)
    c_ref[...] = jnp.dot(xb, w_ref[:, A_COLS + B_COLS:], preferred_element_type=F32).astype(BF16)


def _in_proj(x2d, w_p, nb, s):
    t, d = x2d.shape
    tm = IN_PROJ_ROWS
    n = w_p.shape[1]
    tiles = s // tm
    res_specs = [pl.BlockSpec((1, r, tm // r, A_COLS), lambda i: (i // tiles, 0, i % tiles, 0))
                 for r in IN_PROJ_DILATIONS]
    res_shapes = [jax.ShapeDtypeStruct((nb, r, s // r, A_COLS), BF16) for r in IN_PROJ_DILATIONS]
    outs = pl.pallas_call(
        _in_proj_body,
        grid=(t // tm,),
        in_specs=[pl.BlockSpec((tm, d), lambda i: (i, 0)),
                  pl.BlockSpec((d, n), lambda i: (0, 0))],
        out_specs=[pl.BlockSpec((tm, A_COLS), lambda i: (i, 0))] + res_specs + [
                   pl.BlockSpec((tm, B_COLS), lambda i: (i, 0)),
                   pl.BlockSpec((tm, C_COLS), lambda i: (i, 0))],
        out_shape=[jax.ShapeDtypeStruct((t, A_COLS), BF16)] + res_shapes + [
                   jax.ShapeDtypeStruct((t, B_COLS), F32),
                   jax.ShapeDtypeStruct((t, C_COLS), BF16)],
        scratch_shapes=[pltpu.VMEM((2, A_COLS // LANES, tm, LANES), F32)],
        compiler_params=_cparams("arbitrary"),
        name="in_proj",
    )(x2d, w_p)
    return outs[0], list(outs[1:-2]), outs[-2], outs[-1]


def _band_attn_body(*refs, width, max_dist, has_sink, want_lse):
    q_ref, kp_ref, kc_ref, vp_ref, vc_ref, bias_ref = refs[:6]
    rest = refs[6:]
    if has_sink:
        sink_ref, rest = rest[0], rest[1:]
    o_ref = rest[0]
    lse_ref = rest[1] if want_lse else None

    first_tile = pl.program_id(2) == 0
    row = lax.broadcasted_iota(I32, (BAND, 2 * BAND), 0)
    col = lax.broadcasted_iota(I32, (BAND, 2 * BAND), 1)
    dist = row + BAND - col
    in_band = (dist >= 0) & (dist <= max_dist)
    first_mask = in_band & ((col >= BAND) | jnp.logical_not(first_tile))
    lane = lax.broadcasted_iota(I32, (BAND, LANES), 1)
    low_half = lane < HEAD_DIM
    scale = HEAD_DIM ** -0.5

    for qb in range(q_ref.shape[0] // BAND):
        rows = slice(qb * BAND, (qb + 1) * BAND)
        mask = first_mask if qb == 0 else in_band
        lse_tile = jnp.zeros((BAND, LANES), F32)
        for tile in range(width // LANES):
            sl = slice(tile * LANES, (tile + 1) * LANES)
            q2 = q_ref[rows, sl]
            if qb == 0:
                k2 = jnp.concatenate([kp_ref[:, sl], kc_ref[:BAND, sl]], axis=0)
                v2 = jnp.concatenate([vp_ref[:, sl], vc_ref[:BAND, sl]], axis=0)
            else:
                k2 = kc_ref[(qb - 1) * BAND:(qb + 1) * BAND, sl]
                v2 = vc_ref[(qb - 1) * BAND:(qb + 1) * BAND, sl]
            outs, lses = [], []
            for half in range(2):
                h = 2 * tile + half
                qm = jnp.where(low_half if half == 0 else jnp.logical_not(low_half), q2, jnp.zeros_like(q2))
                s = lax.dot_general(qm, k2, (((1,), (1,)), ((), ())), preferred_element_type=F32)
                s = s * scale + bias_ref[h]
                s = jnp.where(mask, s, MASK_VALUE)
                m = jnp.max(s, axis=-1, keepdims=True)
                if has_sink:
                    sink = sink_ref[h]
                    m = jnp.maximum(m, sink)
                p = jnp.exp(s - m)
                den = jnp.sum(p, axis=-1, keepdims=True)
                if has_sink:
                    den = den + jnp.exp(sink - m)
                pv = jnp.dot(p.astype(BF16), v2, preferred_element_type=F32)
                outs.append(pv / den)
                if want_lse:
                    lse_tile = jnp.where(lane == h, m + jnp.log(den), lse_tile)
            o_ref[rows, sl] = jnp.where(low_half, outs[0], outs[1])
        if want_lse:
            lse_ref[rows, :] = lse_tile


def _band_attn(src, bias, *, width, max_dist, sinks=None, want_lse):
    nb, r, length, _ = src.shape
    qt = min(BAND_Q_TILE, length)
    bands = qt // BAND
    heads = width // HEAD_DIM
    has_sink = sinks is not None

    def cur(off):
        return pl.BlockSpec((None, None, qt, width), lambda b, p, i: (b, p, i, off))

    def prev(off):
        return pl.BlockSpec((None, None, BAND, width), lambda b, p, i: (b, p, jnp.maximum(i * bands - 1, 0), off))

    in_specs = [cur(0), prev(1), cur(1), prev(2), cur(2),
                pl.BlockSpec((heads, BAND, 2 * BAND), lambda b, p, i: (0, 0, 0))]
    args = [src, src, src, src, src, bias]
    if has_sink:
        in_specs.append(pl.BlockSpec(memory_space=pltpu.SMEM))
        args.append(sinks)
    out_spec = pl.BlockSpec((None, None, qt, width), lambda b, p, i: (b, p, i, 0))
    out_sds = jax.ShapeDtypeStruct((nb, r, length, width), F32)
    lse_spec = pl.BlockSpec((None, None, qt, LANES), lambda b, p, i: (b, p, i, 0))
    lse_sds = jax.ShapeDtypeStruct((nb, r, length, LANES), F32)
    body = functools.partial(_band_attn_body, width=width, max_dist=max_dist, has_sink=has_sink, want_lse=want_lse)
    return pl.pallas_call(
        body,
        grid=(nb, r, length // qt),
        in_specs=in_specs,
        out_specs=[out_spec, lse_spec] if want_lse else out_spec,
        out_shape=[out_sds, lse_sds] if want_lse else out_sds,
        compiler_params=_cparams("arbitrary", "arbitrary", "arbitrary"),
        name="band_attn_r%d_w%d" % (r, width),
    )(*args)


def _hgrn_body(b_ref, lb_ref, ng_ref, w_ref, o_ref, state_ref):
    c = HG_CHUNK
    kd = B_KEY_DIM

    @pl.when(pl.program_id(1) == 0)
    def _():
        state_ref[...] = jnp.zeros_like(state_ref)

    trow = lax.broadcasted_iota(I32, (c, 1), 0)
    odd_row = (trow % 2) == 1
    low_sub = lax.broadcasted_iota(I32, (SUBLANES, 1), 0) < SUBLANES // 2
    ti = lax.broadcasted_iota(I32, (c, c), 0)
    si = lax.broadcasted_iota(I32, (c, c), 1)
    nt = (((1,), (1,)), ((), ()))
    tri = w_ref[...]

    in_level = [((ti // m) == (si // m)) & ((ti % m) >= (m // 2)) & ((si % m) < (m // 2))
                for m in (c >> lvl for lvl in range(HG_LEVELS))]

    states = [state_ref[h] for h in range(B_HEADS)]
    for chunk, h in [(ci, hi) for ci in range(b_ref.shape[1] // c) for hi in range(B_HEADS)]:
        rows = slice(chunk * c, (chunk + 1) * c)
        ks = slice(h * kd, (h + 1) * kd)
        q = b_ref[0, rows, ks]
        f = b_ref[0, rows, B_KEY_WIDTH + h * kd:B_KEY_WIDTH + (h + 1) * kd]
        inp = b_ref[0, rows, 2 * B_KEY_WIDTH + h * kd:2 * B_KEY_WIDTH + (h + 1) * kd]
        gate = b_ref[0, rows, 3 * B_KEY_WIDTH + h * kd:3 * B_KEY_WIDTH + (h + 1) * kd]
        lb = lb_ref[:, ks]

        a = jnp.exp(-jnp.abs(f))
        big = 1.0 / (1.0 + a)
        small = a * big
        pos = f >= 0.0
        forget = lb + (1.0 - lb) * jnp.where(pos, big, small)
        log_f = jnp.log(forget)
        key = (1.0 - lb) * jnp.where(pos, small, big)
        qs = q * jax.nn.sigmoid(q)

        g_hi = log_f.astype(BF16)
        g_lo = (log_f - g_hi.astype(F32)).astype(BF16)
        cum2 = jnp.dot(tri, jnp.concatenate([g_hi, g_lo], axis=1), preferred_element_type=F32)
        cum = cum2[:, :kd] + cum2[:, kd:]
        last = cum[c - 1:c]

        q_dec = (qs * jnp.exp(cum)).astype(BF16)
        k_dec = (key * jnp.exp(last - cum)).astype(BF16)
        inp_b = inp.astype(BF16)

        def level_decay(m):
            if m == 2:
                return jnp.where(odd_row, forget, 1.0)
            if m == SUBLANES // 2:
                row_bcast = lambda i: jnp.broadcast_to(cum[i:i + 1], (SUBLANES, kd))
                pieces = [jnp.where(low_sub, row_bcast(SUBLANES * j + m // 2 - 1), row_bcast(SUBLANES * j + m + m // 2 - 1))
                          for j in range(c // SUBLANES)]
            else:
                pieces = [jnp.broadcast_to(cum[b * m + m // 2 - 1:b * m + m // 2], (m, kd)) for b in range(c // m)]
            return jnp.exp(-jnp.abs(cum - jnp.concatenate(pieces, axis=0)))

        st = states[h]
        inter = lax.dot_general(q_dec, st.astype(BF16), nt, preferred_element_type=F32)

        scores = jnp.where(ti == si,
                           lax.dot_general(qs.astype(BF16), key.astype(BF16), nt, preferred_element_type=F32), 0.0)
        m = c
        for lvl in range(HG_LEVELS):
            el = level_decay(m)
            sl = lax.dot_general((qs * el).astype(BF16), (key * el).astype(BF16), nt, preferred_element_type=F32)
            scores = jnp.where(in_level[lvl], sl, scores)
            m //= 2
        intra = jnp.dot(scores.astype(BF16), inp_b, preferred_element_type=F32)

        new_st = st * jnp.exp(last) + lax.dot_general(
            inp_b, k_dec, (((0,), (0,)), ((), ())), preferred_element_type=F32)
        states[h] = new_st

        o = inter + intra
        ms = jnp.sum(o * o, axis=-1, keepdims=True) * (1.0 / B_VAL_DIM)
        o = o * lax.rsqrt(ms + 1e-6)
        o_ref[0, rows, ks] = o * ng_ref[:, ks] * (gate * jax.nn.sigmoid(gate))

    for h in range(B_HEADS):
        state_ref[h] = states[h]


def _hgrn(b_all, lower_bound, norm_g_pad, w_sum):
    nb, s, _ = b_all.shape
    c = HG_STEP_ROWS
    return pl.pallas_call(
        _hgrn_body,
        grid=(nb, s // c),
        in_specs=[pl.BlockSpec((1, c, B_COLS), lambda b, i: (b, i, 0)),
                  pl.BlockSpec((1, B_KEY_WIDTH), lambda b, i: (0, 0)),
                  pl.BlockSpec((1, B_PAD_WIDTH), lambda b, i: (0, 0)),
                  pl.BlockSpec(w_sum.shape, lambda b, i: (0, 0))],
        out_specs=pl.BlockSpec((1, c, B_PAD_WIDTH), lambda b, i: (b, i, 0)),
        out_shape=jax.ShapeDtypeStruct((nb, s, B_PAD_WIDTH), F32),
        scratch_shapes=[pltpu.VMEM((B_HEADS, B_VAL_PAD, B_KEY_DIM), F32)],
        compiler_params=_cparams("arbitrary", "arbitrary"),
        name="hgrn2",
    )(b_all, lower_bound, norm_g_pad, w_sum)


def _layer_norm_rows(z, g, b):
    mu = jnp.mean(z, axis=-1, keepdims=True)
    zc = z - mu
    var = jnp.mean(zc * zc, axis=-1, keepdims=True)
    return zc * lax.rsqrt(var + LN_EPS) * g + b


def _mix_out_body(*refs):
    n_pat = len(A_PATTERNS)
    o_refs, l_refs = refs[:n_pat], refs[n_pat:2 * n_pat]
    ob, oc, x_ref, w_ref, g_ref, b_ref, spread_ref, out_ref = refs[2 * n_pat:2 * n_pat + 8]
    scratch = refs[2 * n_pat + 8:]
    tm = x_ref.shape[0]
    n_slabs = A_WIDTH // LANES

    def token_order(ref, r, scr):
        slabs = ref.shape[-1] // LANES
        if r == 1:
            return [ref[0, 0, :, j * LANES:(j + 1) * LANES] for j in range(slabs)]
        for p in range(r):
            for j in range(slabs):
                scr[j, pl.ds(p, tm // r, stride=r), :] = ref[0, p, :, j * LANES:(j + 1) * LANES]
        return [scr[j] for j in range(slabs)]

    scr_iter = iter(scratch)
    dil = [r for _, r in A_PATTERNS]
    o_slabs = [token_order(ref, r, None if r == 1 else next(scr_iter)) for ref, r in zip(o_refs, dil)]
    ls = [token_order(ref, r, None if r == 1 else next(scr_iter))[0] for ref, r in zip(l_refs, dil)]

    m = functools.reduce(jnp.maximum, ls)
    ws = [jnp.exp(l - m) for l in ls]
    inv = 1.0 / functools.reduce(lambda a, b: a + b, ws)
    spread = spread_ref[...]
    wide = []
    for w in ws:
        wn = w * inv
        hi = wn.astype(BF16)
        lo = (wn - hi.astype(F32)).astype(BF16)
        wide.append(jnp.dot(hi, spread, preferred_element_type=F32) + jnp.dot(lo, spread, preferred_element_type=F32))
    merged = [functools.reduce(lambda a, b: a + b,
                               [w[:, j * LANES:(j + 1) * LANES] * o[j] for w, o in zip(wide, o_slabs)])
              for j in range(n_slabs)]
    cat = jnp.concatenate(merged + [ob[...], oc[...]], axis=1).astype(BF16)
    y = jnp.dot(cat, w_ref[...], preferred_element_type=F32)
    z = DEEPNORM_ALPHA * x_ref[...] + y
    out_ref[...] = _layer_norm_rows(z, g_ref[...], b_ref[...])


def _mix_out(o_list, l_list, ob, oc, x2d, w_out_p, ln_g, ln_b):
    t, d = x2d.shape
    tm = MIX_OUT_ROWS
    tiles = o_list[0].shape[1] * o_list[0].shape[2] // tm
    row = lambda w: pl.BlockSpec((tm, w), lambda i: (i, 0))
    full = lambda a: pl.BlockSpec(a.shape, lambda i: (0, 0))
    res = lambda a: pl.BlockSpec((1, a.shape[1], tm // a.shape[1], a.shape[3]), lambda i: (i // tiles, 0, i % tiles, 0))
    spread = jnp.asarray(np.arange(LANES)[:, None] == np.arange(A_WIDTH)[None, :] // HEAD_DIM, dtype=BF16)
    dilated = [a for a in o_list + l_list if a.shape[1] > 1]
    return pl.pallas_call(
        _mix_out_body,
        grid=(t // tm,),
        in_specs=[res(a) for a in o_list + l_list] + [row(B_PAD_WIDTH), row(C_WIDTH), row(d), full(w_out_p),
                                                      full(ln_g), full(ln_b), full(spread)],
        out_specs=row(d),
        out_shape=jax.ShapeDtypeStruct((t, d), F32),
        scratch_shapes=[pltpu.VMEM((a.shape[3] // LANES, tm, LANES), F32) for a in dilated],
        compiler_params=_cparams("arbitrary"),
        name="mix_out_ln",
    )(*o_list, *l_list, ob, oc, x2d, w_out_p, ln_g, ln_b, spread)


def _router_body(h_ref, rw_ref, bias_ref, tri_ref, e_ref, rank_ref, gate_ref, cnt_ref, hp_ref, carry_ref):
    tn = h_ref.shape[0]
    per_group = N_EXPERTS // N_GROUPS
    neg_inf = -jnp.inf

    @pl.when(pl.program_id(0) == 0)
    def _():
        carry_ref[...] = jnp.zeros_like(carry_ref)

    hp_ref[...] = _pack_bf16_pairs(h_ref[...])

    h = h_ref[...]
    h_hi = h.astype(BF16)
    h_lo = (h - h_hi.astype(F32)).astype(BF16)
    nt = (((1,), (1,)), ((), ()))
    w_hi, w_lo = rw_ref[0], rw_ref[1]
    logits = (lax.dot_general(w_hi, h_hi, nt, preferred_element_type=F32)
              + lax.dot_general(w_hi, h_lo, nt, preferred_element_type=F32)
              + lax.dot_general(w_lo, h_hi, nt, preferred_element_type=F32))
    scores = jax.nn.sigmoid(logits)
    choice = scores + bias_ref[...]

    def first_max(vals, idx, sentinel):
        top = jnp.max(vals, axis=0, keepdims=True)
        return top, jnp.min(jnp.where(vals == top, idx, sentinel), axis=0, keepdims=True)

    li = lax.broadcasted_iota(I32, (per_group, tn), 0).astype(F32)
    group_rows = []
    for g in range(N_GROUPS):
        cg = choice[g * per_group:(g + 1) * per_group]
        m1, first = first_max(cg, li, float(per_group))
        m2 = jnp.max(jnp.where(li == first, neg_inf, cg), axis=0, keepdims=True)
        group_rows.append(m1 + m2)
    group_score = jnp.concatenate(group_rows, axis=0)

    gi = lax.broadcasted_iota(I32, (N_GROUPS, tn), 0).astype(F32)
    group_ok = jnp.zeros((N_GROUPS, tn), F32)
    cur = group_score
    for _ in range(TOPK_GROUPS):
        _, first = first_max(cur, gi, float(N_GROUPS))
        pick = gi == first
        group_ok = jnp.where(pick, 1.0, group_ok)
        cur = jnp.where(pick, neg_inf, cur)

    cur = jnp.concatenate(
        [jnp.where(group_ok[g:g + 1] > 0.0, choice[g * per_group:(g + 1) * per_group], MASK_VALUE)
         for g in range(N_GROUPS)], axis=0)
    ei = lax.broadcasted_iota(I32, (N_EXPERTS, tn), 0).astype(F32)
    chosen = jnp.zeros((N_EXPERTS, tn), F32)
    picks, gates = [], []
    for _ in range(TOP_K):
        _, idx = first_max(cur, ei, float(N_EXPERTS))
        pick = ei == idx
        picks.append(idx)
        gates.append(jnp.sum(jnp.where(pick, scores, 0.0), axis=0, keepdims=True))
        chosen = jnp.where(pick, 1.0, chosen)
        cur = jnp.where(pick, neg_inf, cur)

    gate = jnp.concatenate(gates, axis=0)
    gate_ref[...] = gate / jnp.sum(gate, axis=0, keepdims=True) * ROUTED_SCALE
    e_ref[...] = jnp.concatenate(picks, axis=0).astype(I32)

    before = jnp.dot(chosen.astype(BF16), tri_ref[...], preferred_element_type=F32) + carry_ref[...]
    ranks = [jnp.sum(jnp.where(ei == idx, before, 0.0), axis=0, keepdims=True) for idx in picks]
    rank_ref[...] = jnp.concatenate(ranks, axis=0).astype(I32)
    carry = carry_ref[...] + jnp.sum(chosen, axis=1, keepdims=True)
    carry_ref[...] = carry
    cnt_ref[...] = carry.astype(I32)


def _router(h2d, rw_t, bias_col):
    t, d = h2d.shape
    tn = ROUTER_TOKENS
    tri = jnp.asarray(np.triu(np.ones((tn, tn), np.float32), k=1), dtype=BF16)
    rw_hi = rw_t.astype(BF16)
    rw_split = jnp.stack([rw_hi, (rw_t - rw_hi.astype(F32)).astype(BF16)])
    tok = lambda: pl.BlockSpec((TOP_K, tn), lambda i: (0, i))
    return pl.pallas_call(
        _router_body,
        grid=(t // tn,),
        in_specs=[pl.BlockSpec((tn, d), lambda i: (i, 0)),
                  pl.BlockSpec((2, N_EXPERTS, d), lambda i: (0, 0, 0)),
                  pl.BlockSpec((N_EXPERTS, 1), lambda i: (0, 0)),
                  pl.BlockSpec((tn, tn), lambda i: (0, 0))],
        out_specs=[tok(), tok(), tok(), pl.BlockSpec((N_EXPERTS, 1), lambda i: (0, 0)),
                   pl.BlockSpec((tn, d // 2), lambda i: (i, 0))],
        out_shape=[jax.ShapeDtypeStruct((TOP_K, t), I32),
                   jax.ShapeDtypeStruct((TOP_K, t), I32),
                   jax.ShapeDtypeStruct((TOP_K, t), F32),
                   jax.ShapeDtypeStruct((N_EXPERTS, 1), I32),
                   jax.ShapeDtypeStruct((t, d // 2), U32)],
        scratch_shapes=[pltpu.VMEM((N_EXPERTS, 1), F32)],
        compiler_params=_cparams("arbitrary"),
        name="moe_router",
    )(h2d, rw_split, bias_col, tri)


def _pack_bf16_pairs(x):
    w = x.shape[1] // 2
    hi = lax.bitcast_convert_type(x[:, :w].astype(BF16).astype(F32), U32)
    lo = lax.bitcast_convert_type(x[:, w:].astype(BF16).astype(F32), U32)
    return hi | (lo >> 16)


def _unpack_bf16_pairs(p):
    hi = lax.bitcast_convert_type(p & jnp.uint32(0xFFFF0000), F32)
    lo = lax.bitcast_convert_type(p << 16, F32)
    return hi, lo


def _dest_body(e_ref, rank_ref, offs_ref, dest_ref):
    tn = e_ref.shape[1]
    ei = lax.broadcasted_iota(I32, (N_EXPERTS, tn), 0)
    offs = offs_ref[...]
    rows = [jnp.sum(jnp.where(ei == e_ref[k:k + 1, :], offs, 0.0), axis=0, keepdims=True) for k in range(TOP_K)]
    dest_ref[...] = jnp.concatenate(rows, axis=0).astype(I32) + rank_ref[...]


def _dest_rows(e_t, rank_t, offsets):
    t = e_t.shape[1]
    tn = DEST_TOKENS
    tok = pl.BlockSpec((TOP_K, tn), lambda i: (0, i))
    return pl.pallas_call(
        _dest_body,
        grid=(t // tn,),
        in_specs=[tok, tok, pl.BlockSpec((N_EXPERTS, 1), lambda i: (0, 0))],
        out_specs=tok,
        out_shape=jax.ShapeDtypeStruct((TOP_K, t), I32),
        compiler_params=_cparams("arbitrary"),
        name="moe_dest",
    )(e_t, rank_t, offsets.astype(F32).reshape(N_EXPERTS, 1))


def _expert_body(offs_ref, nblk_ref, cnt_ref, layer_ref, xs_hbm, wg_ref, wu_ref, wd_ref, ys_hbm,
                 xbuf, ybuf, wg_b, wu_b, wd_b, sem_in, sem_out):
    e = pl.program_id(0)
    slots, bm = xbuf.shape[:2]
    ahead = slots - 1
    n = nblk_ref[e]
    first = offs_ref[e] // bm
    total = (offs_ref[N_EXPERTS - 1] // bm) + nblk_ref[N_EXPERTS - 1]

    def rows(g):
        return pl.ds(pl.multiple_of(g * bm, bm), bm)

    def x_copy(g, slot):
        return pltpu.make_async_copy(xs_hbm.at[rows(g)], xbuf.at[slot], sem_in.at[slot])

    def y_copy(g, slot):
        return pltpu.make_async_copy(ybuf.at[slot], ys_hbm.at[rows(g)], sem_out.at[slot])

    @pl.when(e == 0)
    def _():
        for g0 in range(ahead):
            @pl.when(g0 < total)
            def _():
                x_copy(g0, g0).start()

    wg_b[...] = wg_ref[0, 0].astype(BF16)
    wu_b[...] = wu_ref[0, 0].astype(BF16)
    wd_b[...] = wd_ref[0, 0].astype(BF16)

    def block(j, carry):
        g = first + j
        slot = g % slots
        x_copy(g, slot).wait()

        @pl.when(g + ahead < total)
        def _():
            x_copy(g + ahead, (g + ahead) % slots).start()

        @pl.when(g >= slots)
        def _():
            y_copy(g - slots, slot).wait()

        live = lax.broadcasted_iota(I32, (bm, 1), 0) < (cnt_ref[e] - j * bm)
        hi, lo = _unpack_bf16_pairs(jnp.where(live, xbuf[slot], jnp.uint32(0)))
        xb = jnp.concatenate([hi.astype(BF16), lo.astype(BF16)], axis=1)
        gate = jnp.dot(xb, wg_b[...], preferred_element_type=F32)
        up = jnp.dot(xb, wu_b[...], preferred_element_type=F32)
        hidden = (gate * jax.nn.sigmoid(gate) * up).astype(BF16)
        ybuf[slot] = _pack_bf16_pairs(jnp.dot(hidden, wd_b[...], preferred_element_type=F32))
        y_copy(g, slot).start()
        return carry

    lax.fori_loop(0, n, block, 0)

    @pl.when(e == N_EXPERTS - 1)
    def _():
        for back in range(1, slots + 1):
            @pl.when(total >= back)
            def _():
                y_copy(total - back, (total - back) % slots).wait()


def _experts(xs, offsets, n_blk, counts, layer, w_gate, w_up, w_down):
    n_rows, dp = xs.shape
    bm = EXPERT_BLOCK
    d, ff = w_gate.shape[-2:]
    w_spec = lambda a, b: pl.BlockSpec((1, 1, a, b), lambda e, of, nb, ct, ly: (ly[0], e, 0, 0))
    grid_spec = pltpu.PrefetchScalarGridSpec(
        num_scalar_prefetch=4,
        grid=(N_EXPERTS,),
        in_specs=[pl.BlockSpec(memory_space=pl.ANY), w_spec(d, ff), w_spec(d, ff), w_spec(ff, d)],
        out_specs=pl.BlockSpec(memory_space=pl.ANY),
        scratch_shapes=[pltpu.VMEM((EXPERT_SLOTS, bm, dp), U32), pltpu.VMEM((EXPERT_SLOTS, bm, dp), U32),
                        pltpu.VMEM((d, ff), BF16), pltpu.VMEM((d, ff), BF16), pltpu.VMEM((ff, d), BF16),
                        pltpu.SemaphoreType.DMA((EXPERT_SLOTS,)), pltpu.SemaphoreType.DMA((EXPERT_SLOTS,))],
    )
    return pl.pallas_call(
        _expert_body,
        grid_spec=grid_spec,
        out_shape=jax.ShapeDtypeStruct((n_rows, dp), U32),
        compiler_params=_cparams("arbitrary"),
        name="moe_experts",
    )(offsets, n_blk, counts, layer, xs, w_gate, w_up, w_down)


SC_GATHER_ROWS = 64


def _sc_gather_rows(table, idx):
    info = plsc.get_sparse_core_info()
    nc, ns = info.num_cores, info.num_subcores
    workers = nc * ns
    n = idx.shape[0]
    w = table.shape[1]
    ch = SC_GATHER_ROWS
    per_worker = n // workers
    steps = per_worker // ch
    assert per_worker * workers == n and steps * ch == per_worker and steps % 2 == 0
    mesh = plsc.VectorSubcoreMesh(core_axis_name="c", subcore_axis_name="s")

    @functools.partial(
        pl.kernel, mesh=mesh,
        out_type=jax.ShapeDtypeStruct((n, w), table.dtype),
        scratch_types=[pltpu.VMEM((steps, ch), I32), pltpu.VMEM((2, ch, w), table.dtype),
                       pltpu.SemaphoreType.DMA((2,)), pltpu.SemaphoreType.DMA((2,))],
    )
    def gather_kernel(table_hbm, idx_hbm, out_hbm, idx_v, rows_v, gsem, wsem):
        wid = lax.axis_index("s") * nc + lax.axis_index("c")
        base = wid * per_worker
        pltpu.sync_copy(idx_hbm.at[wid], idx_v)

        def gather(i, slot):
            return pltpu.make_async_copy(table_hbm.at[idx_v.at[i]], rows_v.at[slot], gsem.at[slot])

        def write(i, slot):
            off = pl.multiple_of(base + i * ch, ch)
            return pltpu.make_async_copy(rows_v.at[slot], out_hbm.at[pl.ds(off, ch)], wsem.at[slot])

        gather(0, 0).start()

        @pl.loop(0, steps, step=2)
        def _(i):
            gather(i + 1, 1).start()
            gather(i, 0).wait()
            write(i, 0).start()
            write(i, 0).wait()

            @pl.when(i + 2 < steps)
            def _():
                gather(i + 2, 0).start()

            gather(i + 1, 1).wait()
            write(i + 1, 1).start()
            write(i + 1, 1).wait()

    return gather_kernel(table, idx.reshape(workers, steps, ch))


def _sc_scatter_rows(rows, idx, n_out):
    info = plsc.get_sparse_core_info()
    nc, ns = info.num_cores, info.num_subcores
    workers = nc * ns
    kk, t = idx.shape
    w = rows.shape[1]
    ch = SC_GATHER_ROWS
    per_worker = t // workers
    steps = per_worker // ch
    assert per_worker * workers == t and steps * ch == per_worker and steps % 2 == 0
    idx_w = idx.reshape(kk, workers, steps, ch).transpose(1, 0, 2, 3).reshape(workers, kk * steps, ch)
    mesh = plsc.VectorSubcoreMesh(core_axis_name="c", subcore_axis_name="s")

    @functools.partial(
        pl.kernel, mesh=mesh,
        out_type=jax.ShapeDtypeStruct((n_out, w), rows.dtype),
        scratch_types=[pltpu.VMEM((kk * steps, ch), I32), pltpu.VMEM((2, ch, w), rows.dtype),
                       pltpu.SemaphoreType.DMA((2,)), pltpu.SemaphoreType.DMA((2,))],
    )
    def scatter_kernel(rows_hbm, idx_hbm, out_hbm, idx_v, rows_v, rsem, ssem):
        wid = lax.axis_index("s") * nc + lax.axis_index("c")
        base = wid * per_worker
        pltpu.sync_copy(idx_hbm.at[wid], idx_v)

        def read(i, slot):
            off = pl.multiple_of(base + i * ch, ch)
            return pltpu.make_async_copy(rows_hbm.at[pl.ds(off, ch)], rows_v.at[slot], rsem.at[slot])

        def scatter(i, k, slot):
            return pltpu.make_async_copy(rows_v.at[slot], out_hbm.at[idx_v.at[k * steps + i]], ssem.at[slot])

        def scatter_all(i, slot):
            for k in range(kk):
                scatter(i, k, slot).start()
            for k in range(kk):
                scatter(i, k, slot).wait()

        read(0, 0).start()

        @pl.loop(0, steps, step=2)
        def _(i):
            read(i + 1, 1).start()
            read(i, 0).wait()
            scatter_all(i, 0)

            @pl.when(i + 2 < steps)
            def _():
                read(i + 2, 0).start()

            read(i + 1, 1).wait()
            scatter_all(i + 1, 1)

    return scatter_kernel(rows, idx_w)


def _combine_streamed_body(y_ref, gate_ref, h_ref, sg_ref, su_ref, sd_ref, g_ref, b_ref, *rest):
    out_ref = rest[-1]
    half = h_ref.shape[1] // 2
    h = h_ref[...]
    hb = h.astype(BF16)
    sg = jnp.dot(hb, sg_ref[...], preferred_element_type=F32)
    su = jnp.dot(hb, su_ref[...], preferred_element_type=F32)
    shared = jnp.dot((sg * jax.nn.sigmoid(sg) * su).astype(BF16), sd_ref[...], preferred_element_type=F32)
    z = DEEPNORM_ALPHA * h + shared
    z_hi, z_lo = z[:, :half], z[:, half:]
    for k in range(TOP_K):
        y_hi, y_lo = _unpack_bf16_pairs(y_ref[k])
        gate = gate_ref[:, k:k + 1]
        z_hi = z_hi + gate * y_hi
        z_lo = z_lo + gate * y_lo
    out_ref[...] = _layer_norm_rows(jnp.concatenate([z_hi, z_lo], axis=1), g_ref[...], b_ref[...])


def _combine_streamed(y_part, gate_tk, h2d, sh_gate, sh_up, sh_down, ln_g, ln_b, part, partial_out):
    t, d = h2d.shape
    tt = COMBINE_ROWS
    steps = y_part.shape[1] // tt
    first = part * steps
    full = lambda a: pl.BlockSpec(a.shape, lambda i: (0, 0))
    in_specs = [pl.BlockSpec((TOP_K, tt, d // 2), lambda i: (0, i, 0)),
                pl.BlockSpec((tt, TOP_K), lambda i: (first + i, 0)),
                pl.BlockSpec((tt, d), lambda i: (first + i, 0)),
                full(sh_gate), full(sh_up), full(sh_down), full(ln_g), full(ln_b)]
    args = [y_part, gate_tk, h2d, sh_gate, sh_up, sh_down, ln_g, ln_b]
    aliases = {}
    if partial_out is not None:
        in_specs.append(pl.BlockSpec(memory_space=pl.ANY))
        args.append(partial_out)
        aliases = {len(args) - 1: 0}
    return pl.pallas_call(
        _combine_streamed_body,
        grid=(steps,),
        in_specs=in_specs,
        out_specs=pl.BlockSpec((tt, d), lambda i: (first + i, 0)),
        out_shape=jax.ShapeDtypeStruct((t, d), F32),
        input_output_aliases=aliases,
        compiler_params=_cparams("arbitrary"),
        name="moe_combine_streamed_ln",
    )(*args)


def _mixer_sublayer(h2d, nb, s, w_in_p, w_out_p, a_biases, c_bias, lower_bound, norm_g_pad, sinks, w_sum, ln_g, ln_b):
    a_qkv, a_residue, b_all, c_qkv = _in_proj(h2d, w_in_p, nb, s)
    a_by_dilation = {1: a_qkv.reshape(nb, 1, s, A_COLS), **dict(zip(IN_PROJ_DILATIONS, a_residue))}
    o_list, l_list = [], []
    for (window, r), bias in zip(A_PATTERNS, a_biases):
        o, lse = _band_attn(a_by_dilation[r], bias, width=A_WIDTH, max_dist=window // r, want_lse=True)
        o_list.append(o)
        l_list.append(lse)
    oc = _band_attn(c_qkv.reshape(nb, 1, s, C_COLS), c_bias, width=C_WIDTH,
                    max_dist=C_WINDOW - 1, sinks=sinks, want_lse=False).reshape(nb * s, C_WIDTH)
    ob = _hgrn(b_all.reshape(nb, s, B_COLS), lower_bound, norm_g_pad, w_sum).reshape(nb * s, B_PAD_WIDTH)
    return _mix_out(o_list, l_list, ob, oc, h2d, w_out_p, ln_g, ln_b)


def _moe_sublayer(h2d, layer, router_w, router_bias, w_gate, w_up, w_down, sh_gate, sh_up, sh_down, ln_g, ln_b):
    t, d = h2d.shape
    bm = EXPERT_BLOCK
    e_t, rank_t, gate_t, counts, h_packed = _router(h2d, router_w.T.astype(F32),
                                                    router_bias.astype(F32).reshape(N_EXPERTS, 1))

    counts = counts.reshape(N_EXPERTS)
    padded = (counts + bm - 1) // bm * bm
    pad_end = jnp.cumsum(padded).astype(I32)
    offsets = pad_end - padded
    n_blocks = -(-(t * TOP_K + N_EXPERTS * (bm - 1)) // bm)

    dest_t = _dest_rows(e_t, rank_t, offsets)
    xs = _sc_scatter_rows(h_packed, dest_t, n_blocks * bm)
    ys = _experts(xs, offsets, (padded // bm).astype(I32), counts, jnp.full((1,), layer, I32), w_gate, w_up, w_down)
    tp = t // COMBINE_PARTS
    gate_tk = gate_t.T
    shared_w = (sh_gate.astype(BF16), sh_up.astype(BF16), sh_down.astype(BF16))
    out = None
    for part in range(COMBINE_PARTS):
        idx = dest_t[:, part * tp:(part + 1) * tp].reshape(TOP_K * tp)
        y_part = _sc_gather_rows(ys, idx).reshape(TOP_K, tp, d // 2)
        out = _combine_streamed(y_part, gate_tk, h2d, *shared_w, ln_g, ln_b, part, out)
    return out


def kernel(x, w_in, w_out, rel_bias_table, lower_bound_logits, hgrn_norm_g, attn_sinks, ln1_g, ln1_b, router_w, router_bias, expert_w_gate, expert_w_up, expert_w_down, shared_w_gate, shared_w_up, shared_w_down, ln2_g, ln2_b):
    nb, s, d = x.shape
    depth = w_in.shape[0]
    lb_probs = jax.nn.softmax(lower_bound_logits.astype(F32), axis=0)
    lower_bounds = jnp.cumsum(lb_probs, axis=0) - lb_probs[0]
    rel_table = rel_bias_table.astype(F32)
    a_biases = [_band_bias(rel_table, r, 0, A_HEADS) for _, r in A_PATTERNS]
    c_bias = _band_bias(rel_table, 1, A_HEADS, A_HEADS + C_HEADS)
    w_sum = jnp.asarray(_hgrn_sum_matrix(), dtype=BF16)
    row = lambda v: v.astype(F32).reshape(1, -1)

    h = x.astype(F32).reshape(nb * s, d)
    for l in range(depth):
        h = _mixer_sublayer(h, nb, s, _prep_w_in(w_in[l]), _prep_w_out(w_out[l].astype(F32)), a_biases, c_bias,
                            lower_bounds[l].reshape(1, B_KEY_WIDTH), _pad_heads_vec(hgrn_norm_g[l].astype(F32)),
                            attn_sinks[l].astype(F32), w_sum, row(ln1_g[l]), row(ln1_b[l]))
        h = _moe_sublayer(h, l, router_w[l], router_bias[l], expert_w_gate, expert_w_up, expert_w_down,
                          shared_w_gate[l], shared_w_up[l], shared_w_down[l], row(ln2_g[l]), row(ln2_b[l]))
    return h.reshape(nb, s, d).astype(x.dtype)
```

```python
import functools
import math

import numpy as np
import jax
import jax.numpy as jnp
from jax import lax
from jax.experimental import pallas as pl
from jax.experimental.pallas import tpu as pltpu
from jax.experimental.pallas import tpu_sc as plsc

F32 = jnp.float32
BF16 = jnp.bfloat16
I32 = jnp.int32
U32 = jnp.uint32

LANES = 128
SUBLANES = 8
VMEM_LIMIT = 56 * 1024 * 1024

D_MODEL = 1024
DEPTH = 2
HEAD_DIM = 64
BAND = 128
BAND_Q_TILE = 1024
MASK_VALUE = -1e30

A_HEADS = 6
A_PATTERNS = ((128, 1), (512, 4), (2048, 16))
IN_PROJ_DILATIONS = tuple(r for _, r in A_PATTERNS if r > 1)
IN_PROJ_ROWS = 512
MIX_OUT_ROWS = 512
B_HEADS = 4
B_KEY_DIM = 128
B_VAL_DIM = 96
B_VAL_PAD = 128
HG_CHUNK = 64
HG_STEP_ROWS = 512
C_HEADS = 4
C_KV_HEADS = 2
C_WINDOW = 128

A_WIDTH = A_HEADS * HEAD_DIM
B_KEY_WIDTH = B_HEADS * B_KEY_DIM
B_WIDTH = B_HEADS * B_VAL_DIM
B_PAD_WIDTH = B_HEADS * B_VAL_PAD
C_WIDTH = C_HEADS * HEAD_DIM
C_KV_WIDTH = C_KV_HEADS * HEAD_DIM
IN_SPLITS = (A_WIDTH, A_WIDTH, A_WIDTH, B_KEY_WIDTH, B_KEY_WIDTH, B_WIDTH, B_WIDTH, C_WIDTH, C_KV_WIDTH, C_KV_WIDTH)

A_COLS = 3 * A_WIDTH
B_COLS = 4 * B_KEY_WIDTH
C_COLS = 3 * C_WIDTH
MIX_PAD_WIDTH = A_WIDTH + B_PAD_WIDTH + C_WIDTH

REL_BUCKETS = 32
REL_MAX_DIST = 2048

N_EXPERTS = 256
TOP_K = 8
N_GROUPS = 8
TOPK_GROUPS = 4
EXPERT_FF = 256
SHARED_FF = 256
ROUTED_SCALE = 2.5
ROUTER_TOKENS = 256
DEST_TOKENS = 2048
EXPERT_BLOCK = 512
EXPERT_SLOTS = 3
COMBINE_PARTS = 4
COMBINE_ROWS = 512

DEEPNORM_ALPHA = (2 * DEPTH) ** 0.25
LN_EPS = 1e-5


def _cparams(*sem):
    return pltpu.CompilerParams(dimension_semantics=sem, vmem_limit_bytes=VMEM_LIMIT)


def _prep_w_in(w_in):
    d = w_in.shape[0]
    split_at = [int(i) for i in np.cumsum(IN_SPLITS)[:-1]]
    aq, ak, av, bq, bf, bi, bg, cq, ck, cv = jnp.split(w_in, split_at, axis=-1)
    pad_v = lambda w: jnp.pad(w.reshape(d, B_HEADS, B_VAL_DIM), ((0, 0), (0, 0), (0, B_VAL_PAD - B_VAL_DIM))).reshape(d, B_PAD_WIDTH)
    rep = lambda w: jnp.repeat(w.reshape(d, C_KV_HEADS, HEAD_DIM), C_HEADS // C_KV_HEADS, axis=1).reshape(d, C_WIDTH)
    scale = HEAD_DIM ** -0.5
    cols = [aq * scale, ak, av, bq, bf, pad_v(bi), pad_v(bg), cq * scale, rep(ck), rep(cv)]
    return jnp.concatenate(cols, axis=-1).astype(BF16)


def _prep_w_out(w_out):
    d = w_out.shape[1]
    wa = w_out[:A_WIDTH]
    wb = w_out[A_WIDTH:A_WIDTH + B_WIDTH].reshape(B_HEADS, B_VAL_DIM, d)
    wb = jnp.pad(wb, ((0, 0), (0, B_VAL_PAD - B_VAL_DIM), (0, 0))).reshape(B_PAD_WIDTH, d)
    wc = w_out[A_WIDTH + B_WIDTH:]
    return jnp.concatenate([wa, wb, wc], axis=0).astype(BF16)


def _pad_heads_vec(v):
    return jnp.pad(v.reshape(B_HEADS, B_VAL_DIM), ((0, 0), (0, B_VAL_PAD - B_VAL_DIM))).reshape(1, B_PAD_WIDTH)


def _rel_bucket(dist):
    max_exact = REL_BUCKETS // 2
    d = jnp.maximum(dist, 0)
    log_ratio = jnp.log(jnp.maximum(d, max_exact).astype(F32) / max_exact) / math.log(REL_MAX_DIST / max_exact)
    large = jnp.minimum(max_exact + (log_ratio * (REL_BUCKETS - max_exact)).astype(I32), REL_BUCKETS - 1)
    return jnp.where(d < max_exact, d, large)


def _band_bias(rel_table, r, head_lo, head_hi):
    dist = jnp.arange(BAND)[:, None] + BAND - jnp.arange(2 * BAND)[None, :]
    onehot = jax.nn.one_hot(_rel_bucket(dist * r), REL_BUCKETS, dtype=F32)
    return jnp.einsum("qkb,bh->hqk", onehot, rel_table[:, head_lo:head_hi], precision=lax.Precision.HIGHEST)


def _hgrn_sum_matrix():
    return np.tril(np.ones((HG_CHUNK, HG_CHUNK), np.float32))


HG_LEVELS = int(math.log2(HG_CHUNK))


def _in_proj_body(x_ref, w_ref, *rest):
    n_res = len(IN_PROJ_DILATIONS)
    a_ref, res_refs, (b_ref, c_ref, slabs) = rest[0], rest[1:1 + n_res], rest[1 + n_res:]
    tm = x_ref.shape[0]
    xb = x_ref[...].astype(BF16)
    a = jnp.dot(xb, w_ref[:, :A_COLS], preferred_element_type=F32)
    a_ref[...] = a.astype(BF16)
    for j in range(A_COLS // LANES):
        slabs[0, j] = a[:, j * LANES:(j + 1) * LANES]
    prev_r = 1
    for level, (r, ref) in enumerate(zip(IN_PROJ_DILATIONS, res_refs)):
        step, n, n_prev = r // prev_r, tm // r, tm // prev_r
        src, dst = slabs.at[level % 2], slabs.at[(level + 1) % 2]
        for p in range(r):
            p_prev, q = p % prev_r, p // prev_r
            for j in range(A_COLS // LANES):
                piece = src[j, pl.ds(p_prev * n_prev + q, n, stride=step), :]
                if level + 1 < len(IN_PROJ_DILATIONS):
                    dst[j, p * n:(p + 1) * n, :] = piece
                ref[0, p, :, j * LANES:(j + 1) * LANES] = piece.astype(BF16)
        prev_r = r
    for j in range(B_COLS // B_KEY_WIDTH):
        lo = A_COLS + j * B_KEY_WIDTH
        b_ref[:, j * B_KEY_WIDTH:(j + 1) * B_KEY_WIDTH] = jnp.dot(
            xb, w_ref[:, lo:lo + B_KEY_WIDTH], preferred_element_type=F32)
    c_ref[...] = jnp.dot(xb, w_ref[:, A_COLS + B_COLS:], preferred_element_type=F32).astype(BF16)


def _in_proj(x2d, w_p, nb, s):
    t, d = x2d.shape
    tm = IN_PROJ_ROWS
    n = w_p.shape[1]
    tiles = s // tm
    res_specs = [pl.BlockSpec((1, r, tm // r, A_COLS), lambda i: (i // tiles, 0, i % tiles, 0))
                 for r in IN_PROJ_DILATIONS]
    res_shapes = [jax.ShapeDtypeStruct((nb, r, s // r, A_COLS), BF16) for r in IN_PROJ_DILATIONS]
    outs = pl.pallas_call(
        _in_proj_body,
        grid=(t // tm,),
        in_specs=[pl.BlockSpec((tm, d), lambda i: (i, 0)),
                  pl.BlockSpec((d, n), lambda i: (0, 0))],
        out_specs=[pl.BlockSpec((tm, A_COLS), lambda i: (i, 0))] + res_specs + [
                   pl.BlockSpec((tm, B_COLS), lambda i: (i, 0)),
                   pl.BlockSpec((tm, C_COLS), lambda i: (i, 0))],
        out_shape=[jax.ShapeDtypeStruct((t, A_COLS), BF16)] + res_shapes + [
                   jax.ShapeDtypeStruct((t, B_COLS), F32),
                   jax.ShapeDtypeStruct((t, C_COLS), BF16)],
        scratch_shapes=[pltpu.VMEM((2, A_COLS // LANES, tm, LANES), F32)],
        compiler_params=_cparams("arbitrary"),
        name="in_proj",
    )(x2d, w_p)
    return outs[0], list(outs[1:-2]), outs[-2], outs[-1]


def _band_attn_body(*refs, width, max_dist, has_sink, want_lse):
    q_ref, kp_ref, kc_ref, vp_ref, vc_ref, bias_ref = refs[:6]
    rest = refs[6:]
    if has_sink:
        sink_ref, rest = rest[0], rest[1:]
    o_ref = rest[0]
    lse_ref = rest[1] if want_lse else None

    first_tile = pl.program_id(2) == 0
    row = lax.broadcasted_iota(I32, (BAND, 2 * BAND), 0)
    col = lax.broadcasted_iota(I32, (BAND, 2 * BAND), 1)
    dist = row + BAND - col
    in_band = (dist >= 0) & (dist <= max_dist)
    first_mask = in_band & ((col >= BAND) | jnp.logical_not(first_tile))
    lane = lax.broadcasted_iota(I32, (BAND, LANES), 1)
    low_half = lane < HEAD_DIM

    for qb in range(q_ref.shape[0] // BAND):
        rows = slice(qb * BAND, (qb + 1) * BAND)
        mask = first_mask if qb == 0 else in_band
        lse_tile = jnp.zeros((BAND, LANES), F32)
        for tile in range(width // LANES):
            sl = slice(tile * LANES, (tile + 1) * LANES)
            q2 = q_ref[rows, sl]
            if qb == 0:
                k2 = jnp.concatenate([kp_ref[:, sl], kc_ref[:BAND, sl]], axis=0)
                v2 = jnp.concatenate([vp_ref[:, sl], vc_ref[:BAND, sl]], axis=0)
            else:
                k2 = kc_ref[(qb - 1) * BAND:(qb + 1) * BAND, sl]
                v2 = vc_ref[(qb - 1) * BAND:(qb + 1) * BAND, sl]
            outs, lses = [], []
            for half in range(2):
                h = 2 * tile + half
                qm = jnp.where(low_half if half == 0 else jnp.logical_not(low_half), q2, jnp.zeros_like(q2))
                s = lax.dot_general(qm, k2, (((1,), (1,)), ((), ())), preferred_element_type=F32)
                s = s + bias_ref[h]
                s = jnp.where(mask, s, MASK_VALUE)
                m = jnp.max(s, axis=-1, keepdims=True)
                if has_sink:
                    sink = sink_ref[h]
                    m = jnp.maximum(m, sink)
                p = jnp.exp(s - m)
                den = jnp.sum(p, axis=-1, keepdims=True)
                if has_sink:
                    den = den + jnp.exp(sink - m)
                pv = jnp.dot(p.astype(BF16), v2, preferred_element_type=F32)
                outs.append(pv / den)
                if want_lse:
                    lse_tile = jnp.where(lane == h, m + jnp.log(den), lse_tile)
            o_ref[rows, sl] = jnp.where(low_half, outs[0], outs[1]).astype(o_ref.dtype)
        if want_lse:
            lse_ref[rows, :] = lse_tile


def _band_attn(src, bias, *, width, max_dist, sinks=None, want_lse):
    nb, r, length, _ = src.shape
    qt = min(BAND_Q_TILE, length)
    bands = qt // BAND
    heads = width // HEAD_DIM
    has_sink = sinks is not None

    def cur(off):
        return pl.BlockSpec((None, None, qt, width), lambda b, p, i: (b, p, i, off))

    def prev(off):
        return pl.BlockSpec((None, None, BAND, width), lambda b, p, i: (b, p, jnp.maximum(i * bands - 1, 0), off))

    in_specs = [cur(0), prev(1), cur(1), prev(2), cur(2),
                pl.BlockSpec((heads, BAND, 2 * BAND), lambda b, p, i: (0, 0, 0))]
    args = [src, src, src, src, src, bias]
    if has_sink:
        in_specs.append(pl.BlockSpec(memory_space=pltpu.SMEM))
        args.append(sinks)
    out_spec = pl.BlockSpec((None, None, qt, width), lambda b, p, i: (b, p, i, 0))
    out_sds = jax.ShapeDtypeStruct((nb, r, length, width), F32 if want_lse else BF16)
    lse_spec = pl.BlockSpec((None, None, qt, LANES), lambda b, p, i: (b, p, i, 0))
    lse_sds = jax.ShapeDtypeStruct((nb, r, length, LANES), F32)
    body = functools.partial(_band_attn_body, width=width, max_dist=max_dist, has_sink=has_sink, want_lse=want_lse)
    return pl.pallas_call(
        body,
        grid=(nb, r, length // qt),
        in_specs=in_specs,
        out_specs=[out_spec, lse_spec] if want_lse else out_spec,
        out_shape=[out_sds, lse_sds] if want_lse else out_sds,
        compiler_params=_cparams("arbitrary", "arbitrary", "arbitrary"),
        name="band_attn_r%d_w%d" % (r, width),
    )(*args)


def _hgrn_body(b_ref, lb_ref, ng_ref, w_ref, o_ref, state_ref):
    c = HG_CHUNK
    kd = B_KEY_DIM

    @pl.when(pl.program_id(1) == 0)
    def _():
        state_ref[...] = jnp.zeros_like(state_ref)

    trow = lax.broadcasted_iota(I32, (c, 1), 0)
    odd_row = (trow % 2) == 1
    low_sub = lax.broadcasted_iota(I32, (SUBLANES, 1), 0) < SUBLANES // 2
    ti = lax.broadcasted_iota(I32, (c, c), 0)
    si = lax.broadcasted_iota(I32, (c, c), 1)
    nt = (((1,), (1,)), ((), ()))
    tri = w_ref[...]

    in_level = [((ti // m) == (si // m)) & ((ti % m) >= (m // 2)) & ((si % m) < (m // 2))
                for m in (c >> lvl for lvl in range(HG_LEVELS))]

    states = [state_ref[h] for h in range(B_HEADS)]
    for chunk, h in [(ci, hi) for ci in range(b_ref.shape[1] // c) for hi in range(B_HEADS)]:
        rows = slice(chunk * c, (chunk + 1) * c)
        ks = slice(h * kd, (h + 1) * kd)
        q = b_ref[0, rows, ks]
        f = b_ref[0, rows, B_KEY_WIDTH + h * kd:B_KEY_WIDTH + (h + 1) * kd]
        inp = b_ref[0, rows, 2 * B_KEY_WIDTH + h * kd:2 * B_KEY_WIDTH + (h + 1) * kd]
        gate = b_ref[0, rows, 3 * B_KEY_WIDTH + h * kd:3 * B_KEY_WIDTH + (h + 1) * kd]
        lb = lb_ref[:, ks]

        a = jnp.exp(-jnp.abs(f))
        big = 1.0 / (1.0 + a)
        small = a * big
        pos = f >= 0.0
        forget = lb + (1.0 - lb) * jnp.where(pos, big, small)
        log_f = jnp.log(forget)
        key = (1.0 - lb) * jnp.where(pos, small, big)
        qs = q * jax.nn.sigmoid(q)

        g_hi = log_f.astype(BF16)
        g_lo = (log_f - g_hi.astype(F32)).astype(BF16)
        cum2 = jnp.dot(tri, jnp.concatenate([g_hi, g_lo], axis=1), preferred_element_type=F32)
        cum = cum2[:, :kd] + cum2[:, kd:]
        last = cum[c - 1:c]

        q_dec = (qs * jnp.exp(cum)).astype(BF16)
        k_dec = (key * jnp.exp(last - cum)).astype(BF16)
        inp_b = inp.astype(BF16)

        def level_decay(m):
            if m == 2:
                return jnp.where(odd_row, forget, 1.0)
            if m == SUBLANES // 2:
                row_bcast = lambda i: jnp.broadcast_to(cum[i:i + 1], (SUBLANES, kd))
                pieces = [jnp.where(low_sub, row_bcast(SUBLANES * j + m // 2 - 1), row_bcast(SUBLANES * j + m + m // 2 - 1))
                          for j in range(c // SUBLANES)]
            else:
                pieces = [jnp.broadcast_to(cum[b * m + m // 2 - 1:b * m + m // 2], (m, kd)) for b in range(c // m)]
            return jnp.exp(-jnp.abs(cum - jnp.concatenate(pieces, axis=0)))

        st = states[h]
        inter = lax.dot_general(q_dec, st.astype(BF16), nt, preferred_element_type=F32)

        scores = jnp.where(ti == si,
                           lax.dot_general(qs.astype(BF16), key.astype(BF16), nt, preferred_element_type=F32), 0.0)
        m = c
        for lvl in range(HG_LEVELS):
            el = level_decay(m)
            sl = lax.dot_general((qs * el).astype(BF16), (key * el).astype(BF16), nt, preferred_element_type=F32)
            scores = jnp.where(in_level[lvl], sl, scores)
            m //= 2
        intra = jnp.dot(scores.astype(BF16), inp_b, preferred_element_type=F32)

        new_st = st * jnp.exp(last) + lax.dot_general(
            inp_b, k_dec, (((0,), (0,)), ((), ())), preferred_element_type=F32)
        states[h] = new_st

        o = inter + intra
        ms = jnp.sum(o * o, axis=-1, keepdims=True) * (1.0 / B_VAL_DIM)
        o = o * lax.rsqrt(ms + 1e-6)
        o_ref[0, rows, ks] = (o * ng_ref[:, ks] * (gate * jax.nn.sigmoid(gate))).astype(o_ref.dtype)

    for h in range(B_HEADS):
        state_ref[h] = states[h]


def _hgrn(b_all, lower_bound, norm_g_pad, w_sum):
    nb, s, _ = b_all.shape
    c = HG_STEP_ROWS
    return pl.pallas_call(
        _hgrn_body,
        grid=(nb, s // c),
        in_specs=[pl.BlockSpec((1, c, B_COLS), lambda b, i: (b, i, 0)),
                  pl.BlockSpec((1, B_KEY_WIDTH), lambda b, i: (0, 0)),
                  pl.BlockSpec((1, B_PAD_WIDTH), lambda b, i: (0, 0)),
                  pl.BlockSpec(w_sum.shape, lambda b, i: (0, 0))],
        out_specs=pl.BlockSpec((1, c, B_PAD_WIDTH), lambda b, i: (b, i, 0)),
        out_shape=jax.ShapeDtypeStruct((nb, s, B_PAD_WIDTH), BF16),
        scratch_shapes=[pltpu.VMEM((B_HEADS, B_VAL_PAD, B_KEY_DIM), F32)],
        compiler_params=_cparams("arbitrary", "arbitrary"),
        name="hgrn2",
    )(b_all, lower_bound, norm_g_pad, w_sum)


def _layer_norm_rows(z, g, b):
    mu = jnp.mean(z, axis=-1, keepdims=True)
    zc = z - mu
    var = jnp.mean(zc * zc, axis=-1, keepdims=True)
    return zc * lax.rsqrt(var + LN_EPS) * g + b


def _mix_out_body(*refs):
    n_pat = len(A_PATTERNS)
    o_refs, l_refs = refs[:n_pat], refs[n_pat:2 * n_pat]
    ob, oc, x_ref, w_ref, g_ref, b_ref, spread_ref, out_ref = refs[2 * n_pat:2 * n_pat + 8]
    scratch = refs[2 * n_pat + 8:]
    tm = x_ref.shape[0]
    n_slabs = A_WIDTH // LANES

    def token_order(ref, r, scr):
        slabs = ref.shape[-1] // LANES
        if r == 1:
            return [ref[0, 0, :, j * LANES:(j + 1) * LANES] for j in range(slabs)]
        for p in range(r):
            for j in range(slabs):
                scr[j, pl.ds(p, tm // r, stride=r), :] = ref[0, p, :, j * LANES:(j + 1) * LANES]
        return [scr[j] for j in range(slabs)]

    scr_iter = iter(scratch)
    dil = [r for _, r in A_PATTERNS]
    o_slabs = [token_order(ref, r, None if r == 1 else next(scr_iter)) for ref, r in zip(o_refs, dil)]
    ls = [token_order(ref, r, None if r == 1 else next(scr_iter))[0] for ref, r in zip(l_refs, dil)]

    m = functools.reduce(jnp.maximum, ls)
    ws = [jnp.exp(l - m) for l in ls]
    inv = 1.0 / functools.reduce(lambda a, b: a + b, ws)
    spread = spread_ref[...]
    wide = []
    for w in ws:
        wn = w * inv
        hi = wn.astype(BF16)
        lo = (wn - hi.astype(F32)).astype(BF16)
        wide.append(jnp.dot(hi, spread, preferred_element_type=F32) + jnp.dot(lo, spread, preferred_element_type=F32))
    merged = [functools.reduce(lambda a, b: a + b,
                               [w[:, j * LANES:(j + 1) * LANES] * o[j] for w, o in zip(wide, o_slabs)])
              for j in range(n_slabs)]
    cat = jnp.concatenate([m.astype(BF16) for m in merged] + [ob[...], oc[...]], axis=1)
    y = jnp.dot(cat, w_ref[...], preferred_element_type=F32)
    z = DEEPNORM_ALPHA * x_ref[...] + y
    out_ref[...] = _layer_norm_rows(z, g_ref[...], b_ref[...])


def _mix_out(o_list, l_list, ob, oc, x2d, w_out_p, ln_g, ln_b):
    t, d = x2d.shape
    tm = MIX_OUT_ROWS
    tiles = o_list[0].shape[1] * o_list[0].shape[2] // tm
    row = lambda w: pl.BlockSpec((tm, w), lambda i: (i, 0))
    full = lambda a: pl.BlockSpec(a.shape, lambda i: (0, 0))
    res = lambda a: pl.BlockSpec((1, a.shape[1], tm // a.shape[1], a.shape[3]), lambda i: (i // tiles, 0, i % tiles, 0))
    spread = jnp.asarray(np.arange(LANES)[:, None] == np.arange(A_WIDTH)[None, :] // HEAD_DIM, dtype=BF16)
    dilated = [a for a in o_list + l_list if a.shape[1] > 1]
    return pl.pallas_call(
        _mix_out_body,
        grid=(t // tm,),
        in_specs=[res(a) for a in o_list + l_list] + [row(B_PAD_WIDTH), row(C_WIDTH), row(d), full(w_out_p),
                                                      full(ln_g), full(ln_b), full(spread)],
        out_specs=row(d),
        out_shape=jax.ShapeDtypeStruct((t, d), F32),
        scratch_shapes=[pltpu.VMEM((a.shape[3] // LANES, tm, LANES), F32) for a in dilated],
        compiler_params=_cparams("arbitrary"),
        name="mix_out_ln",
    )(*o_list, *l_list, ob, oc, x2d, w_out_p, ln_g, ln_b, spread)


def _router_body(h_ref, rw_ref, bias_ref, tri_ref, e_ref, rank_ref, gate_ref, cnt_ref, hp_ref, carry_ref):
    tn = h_ref.shape[0]
    per_group = N_EXPERTS // N_GROUPS
    neg_inf = -jnp.inf

    @pl.when(pl.program_id(0) == 0)
    def _():
        carry_ref[...] = jnp.zeros_like(carry_ref)

    hp_ref[...] = _pack_bf16_pairs(h_ref[...])

    h = h_ref[...]
    h_hi = h.astype(BF16)
    h_lo = (h - h_hi.astype(F32)).astype(BF16)
    nt = (((1,), (1,)), ((), ()))
    w_hi, w_lo = rw_ref[0], rw_ref[1]
    logits = (lax.dot_general(w_hi, h_hi, nt, preferred_element_type=F32)
              + lax.dot_general(w_hi, h_lo, nt, preferred_element_type=F32)
              + lax.dot_general(w_lo, h_hi, nt, preferred_element_type=F32))
    scores = jax.nn.sigmoid(logits)
    choice = scores + bias_ref[...]

    def first_max(vals, idx, sentinel):
        top = jnp.max(vals, axis=0, keepdims=True)
        return top, jnp.min(jnp.where(vals == top, idx, sentinel), axis=0, keepdims=True)

    li = lax.broadcasted_iota(I32, (per_group, tn), 0).astype(F32)
    group_rows = []
    for g in range(N_GROUPS):
        cg = choice[g * per_group:(g + 1) * per_group]
        m1, first = first_max(cg, li, float(per_group))
        m2 = jnp.max(jnp.where(li == first, neg_inf, cg), axis=0, keepdims=True)
        group_rows.append(m1 + m2)
    group_score = jnp.concatenate(group_rows, axis=0)

    gi = lax.broadcasted_iota(I32, (N_GROUPS, tn), 0).astype(F32)
    group_ok = jnp.zeros((N_GROUPS, tn), F32)
    cur = group_score
    for _ in range(TOPK_GROUPS):
        _, first = first_max(cur, gi, float(N_GROUPS))
        pick = gi == first
        group_ok = jnp.where(pick, 1.0, group_ok)
        cur = jnp.where(pick, neg_inf, cur)

    cur = jnp.concatenate(
        [jnp.where(group_ok[g:g + 1] > 0.0, choice[g * per_group:(g + 1) * per_group], MASK_VALUE)
         for g in range(N_GROUPS)], axis=0)
    ei = lax.broadcasted_iota(I32, (N_EXPERTS, tn), 0).astype(F32)
    chosen = jnp.zeros((N_EXPERTS, tn), F32)
    picks, gates = [], []
    for _ in range(TOP_K):
        _, idx = first_max(cur, ei, float(N_EXPERTS))
        pick = ei == idx
        picks.append(idx)
        gates.append(jnp.sum(jnp.where(pick, scores, 0.0), axis=0, keepdims=True))
        chosen = jnp.where(pick, 1.0, chosen)
        cur = jnp.where(pick, neg_inf, cur)

    gate = jnp.concatenate(gates, axis=0)
    gate_ref[...] = gate / jnp.sum(gate, axis=0, keepdims=True) * ROUTED_SCALE
    e_ref[...] = jnp.concatenate(picks, axis=0).astype(I32)

    before = jnp.dot(chosen.astype(BF16), tri_ref[...], preferred_element_type=F32) + carry_ref[...]
    ranks = [jnp.sum(jnp.where(ei == idx, before, 0.0), axis=0, keepdims=True) for idx in picks]
    rank_ref[...] = jnp.concatenate(ranks, axis=0).astype(I32)
    carry = carry_ref[...] + jnp.sum(chosen, axis=1, keepdims=True)
    carry_ref[...] = carry
    cnt_ref[...] = carry.astype(I32)


def _router(h2d, rw_t, bias_col):
    t, d = h2d.shape
    tn = ROUTER_TOKENS
    tri = jnp.asarray(np.triu(np.ones((tn, tn), np.float32), k=1), dtype=BF16)
    rw_hi = rw_t.astype(BF16)
    rw_split = jnp.stack([rw_hi, (rw_t - rw_hi.astype(F32)).astype(BF16)])
    tok = lambda: pl.BlockSpec((TOP_K, tn), lambda i: (0, i))
    return pl.pallas_call(
        _router_body,
        grid=(t // tn,),
        in_specs=[pl.BlockSpec((tn, d), lambda i: (i, 0)),
                  pl.BlockSpec((2, N_EXPERTS, d), lambda i: (0, 0, 0)),
                  pl.BlockSpec((N_EXPERTS, 1), lambda i: (0, 0)),
                  pl.BlockSpec((tn, tn), lambda i: (0, 0))],
        out_specs=[tok(), tok(), tok(), pl.BlockSpec((N_EXPERTS, 1), lambda i: (0, 0)),
                   pl.BlockSpec((tn, d // 2), lambda i: (i, 0))],
        out_shape=[jax.ShapeDtypeStruct((TOP_K, t), I32),
                   jax.ShapeDtypeStruct((TOP_K, t), I32),
                   jax.ShapeDtypeStruct((TOP_K, t), F32),
                   jax.ShapeDtypeStruct((N_EXPERTS, 1), I32),
                   jax.ShapeDtypeStruct((t, d // 2), U32)],
        scratch_shapes=[pltpu.VMEM((N_EXPERTS, 1), F32)],
        compiler_params=_cparams("arbitrary"),
        name="moe_router",
    )(h2d, rw_split, bias_col, tri)


def _pack_bf16_pairs(x):
    w = x.shape[1] // 2
    hi = lax.bitcast_convert_type(x[:, :w].astype(BF16).astype(F32), U32)
    lo = lax.bitcast_convert_type(x[:, w:].astype(BF16).astype(F32), U32)
    return hi | (lo >> 16)


def _unpack_bf16_pairs(p):
    hi = lax.bitcast_convert_type(p & jnp.uint32(0xFFFF0000), F32)
    lo = lax.bitcast_convert_type(p << 16, F32)
    return hi, lo


def _dest_body(e_ref, rank_ref, offs_ref, dest_ref):
    tn = e_ref.shape[1]
    ei = lax.broadcasted_iota(I32, (N_EXPERTS, tn), 0)
    offs = offs_ref[...]
    rows = [jnp.sum(jnp.where(ei == e_ref[k:k + 1, :], offs, 0.0), axis=0, keepdims=True) for k in range(TOP_K)]
    dest_ref[...] = jnp.concatenate(rows, axis=0).astype(I32) + rank_ref[...]


def _dest_rows(e_t, rank_t, offsets):
    t = e_t.shape[1]
    tn = DEST_TOKENS
    tok = pl.BlockSpec((TOP_K, tn), lambda i: (0, i))
    return pl.pallas_call(
        _dest_body,
        grid=(t // tn,),
        in_specs=[tok, tok, pl.BlockSpec((N_EXPERTS, 1), lambda i: (0, 0))],
        out_specs=tok,
        out_shape=jax.ShapeDtypeStruct((TOP_K, t), I32),
        compiler_params=_cparams("arbitrary"),
        name="moe_dest",
    )(e_t, rank_t, offsets.astype(F32).reshape(N_EXPERTS, 1))


def _expert_body(offs_ref, nblk_ref, cnt_ref, layer_ref, xs_hbm, wg_ref, wu_ref, wd_ref, ys_hbm,
                 xbuf, ybuf, wg_b, wu_b, wd_b, sem_in, sem_out):
    e = pl.program_id(0)
    slots, bm = xbuf.shape[:2]
    ahead = slots - 1
    n = nblk_ref[e]
    first = offs_ref[e] // bm
    total = (offs_ref[N_EXPERTS - 1] // bm) + nblk_ref[N_EXPERTS - 1]

    def rows(g):
        return pl.ds(pl.multiple_of(g * bm, bm), bm)

    def x_copy(g, slot):
        return pltpu.make_async_copy(xs_hbm.at[rows(g)], xbuf.at[slot], sem_in.at[slot])

    def y_copy(g, slot):
        return pltpu.make_async_copy(ybuf.at[slot], ys_hbm.at[rows(g)], sem_out.at[slot])

    @pl.when(e == 0)
    def _():
        for g0 in range(ahead):
            @pl.when(g0 < total)
            def _():
                x_copy(g0, g0).start()

    wg_b[...] = wg_ref[0, 0].astype(BF16)
    wu_b[...] = wu_ref[0, 0].astype(BF16)
    wd_b[...] = wd_ref[0, 0].astype(BF16)

    def block(j, carry):
        g = first + j
        slot = g % slots
        x_copy(g, slot).wait()

        @pl.when(g + ahead < total)
        def _():
            x_copy(g + ahead, (g + ahead) % slots).start()

        @pl.when(g >= slots)
        def _():
            y_copy(g - slots, slot).wait()

        live = lax.broadcasted_iota(I32, (bm, 1), 0) < (cnt_ref[e] - j * bm)
        hi, lo = _unpack_bf16_pairs(jnp.where(live, xbuf[slot], jnp.uint32(0)))
        xb = jnp.concatenate([hi.astype(BF16), lo.astype(BF16)], axis=1)
        gate = jnp.dot(xb, wg_b[...], preferred_element_type=F32)
        up = jnp.dot(xb, wu_b[...], preferred_element_type=F32)
        hidden = (gate * jax.nn.sigmoid(gate) * up).astype(BF16)
        ybuf[slot] = _pack_bf16_pairs(jnp.dot(hidden, wd_b[...], preferred_element_type=F32))
        y_copy(g, slot).start()
        return carry

    lax.fori_loop(0, n, block, 0)

    @pl.when(e == N_EXPERTS - 1)
    def _():
        for back in range(1, slots + 1):
            @pl.when(total >= back)
            def _():
                y_copy(total - back, (total - back) % slots).wait()


def _experts(xs, offsets, n_blk, counts, layer, w_gate, w_up, w_down):
    n_rows, dp = xs.shape
    bm = EXPERT_BLOCK
    d, ff = w_gate.shape[-2:]
    w_spec = lambda a, b: pl.BlockSpec((1, 1, a, b), lambda e, of, nb, ct, ly: (ly[0], e, 0, 0))
    grid_spec = pltpu.PrefetchScalarGridSpec(
        num_scalar_prefetch=4,
        grid=(N_EXPERTS,),
        in_specs=[pl.BlockSpec(memory_space=pl.ANY), w_spec(d, ff), w_spec(d, ff), w_spec(ff, d)],
        out_specs=pl.BlockSpec(memory_space=pl.ANY),
        scratch_shapes=[pltpu.VMEM((EXPERT_SLOTS, bm, dp), U32), pltpu.VMEM((EXPERT_SLOTS, bm, dp), U32),
                        pltpu.VMEM((d, ff), BF16), pltpu.VMEM((d, ff), BF16), pltpu.VMEM((ff, d), BF16),
                        pltpu.SemaphoreType.DMA((EXPERT_SLOTS,)), pltpu.SemaphoreType.DMA((EXPERT_SLOTS,))],
    )
    return pl.pallas_call(
        _expert_body,
        grid_spec=grid_spec,
        out_shape=jax.ShapeDtypeStruct((n_rows, dp), U32),
        compiler_params=_cparams("arbitrary"),
        name="moe_experts",
    )(offsets, n_blk, counts, layer, xs, w_gate, w_up, w_down)


SC_GATHER_ROWS = 64


def _sc_gather_rows(table, idx):
    info = plsc.get_sparse_core_info()
    nc, ns = info.num_cores, info.num_subcores
    workers = nc * ns
    n = idx.shape[0]
    w = table.shape[1]
    ch = SC_GATHER_ROWS
    per_worker = n // workers
    steps = per_worker // ch
    assert per_worker * workers == n and steps * ch == per_worker and steps % 2 == 0
    mesh = plsc.VectorSubcoreMesh(core_axis_name="c", subcore_axis_name="s")

    @functools.partial(
        pl.kernel, mesh=mesh,
        out_type=jax.ShapeDtypeStruct((n, w), table.dtype),
        scratch_types=[pltpu.VMEM((steps, ch), I32), pltpu.VMEM((2, ch, w), table.dtype),
                       pltpu.SemaphoreType.DMA((2,)), pltpu.SemaphoreType.DMA((2,))],
    )
    def gather_kernel(table_hbm, idx_hbm, out_hbm, idx_v, rows_v, gsem, wsem):
        wid = lax.axis_index("s") * nc + lax.axis_index("c")
        base = wid * per_worker
        pltpu.sync_copy(idx_hbm.at[wid], idx_v)

        def gather(i, slot):
            return pltpu.make_async_copy(table_hbm.at[idx_v.at[i]], rows_v.at[slot], gsem.at[slot])

        def write(i, slot):
            off = pl.multiple_of(base + i * ch, ch)
            return pltpu.make_async_copy(rows_v.at[slot], out_hbm.at[pl.ds(off, ch)], wsem.at[slot])

        gather(0, 0).start()

        @pl.loop(0, steps, step=2)
        def _(i):
            gather(i + 1, 1).start()
            gather(i, 0).wait()
            write(i, 0).start()
            write(i, 0).wait()

            @pl.when(i + 2 < steps)
            def _():
                gather(i + 2, 0).start()

            gather(i + 1, 1).wait()
            write(i + 1, 1).start()
            write(i + 1, 1).wait()

    return gather_kernel(table, idx.reshape(workers, steps, ch))


def _sc_scatter_rows(rows, idx, n_out):
    info = plsc.get_sparse_core_info()
    nc, ns = info.num_cores, info.num_subcores
    workers = nc * ns
    kk, t = idx.shape
    w = rows.shape[1]
    ch = SC_GATHER_ROWS
    per_worker = t // workers
    steps = per_worker // ch
    assert per_worker * workers == t and steps * ch == per_worker and steps % 2 == 0
    idx_w = idx.reshape(kk, workers, steps, ch).transpose(1, 0, 2, 3).reshape(workers, kk * steps, ch)
    mesh = plsc.VectorSubcoreMesh(core_axis_name="c", subcore_axis_name="s")

    @functools.partial(
        pl.kernel, mesh=mesh,
        out_type=jax.ShapeDtypeStruct((n_out, w), rows.dtype),
        scratch_types=[pltpu.VMEM((kk * steps, ch), I32), pltpu.VMEM((2, ch, w), rows.dtype),
                       pltpu.SemaphoreType.DMA((2,)), pltpu.SemaphoreType.DMA((2,))],
    )
    def scatter_kernel(rows_hbm, idx_hbm, out_hbm, idx_v, rows_v, rsem, ssem):
        wid = lax.axis_index("s") * nc + lax.axis_index("c")
        base = wid * per_worker
        pltpu.sync_copy(idx_hbm.at[wid], idx_v)

        def read(i, slot):
            off = pl.multiple_of(base + i * ch, ch)
            return pltpu.make_async_copy(rows_hbm.at[pl.ds(off, ch)], rows_v.at[slot], rsem.at[slot])

        def scatter(i, k, slot):
            return pltpu.make_async_copy(rows_v.at[slot], out_hbm.at[idx_v.at[k * steps + i]], ssem.at[slot])

        def scatter_all(i, slot):
            for k in range(kk):
                scatter(i, k, slot).start()
            for k in range(kk):
                scatter(i, k, slot).wait()

        read(0, 0).start()

        @pl.loop(0, steps, step=2)
        def _(i):
            read(i + 1, 1).start()
            read(i, 0).wait()
            scatter_all(i, 0)

            @pl.when(i + 2 < steps)
            def _():
                read(i + 2, 0).start()

            read(i + 1, 1).wait()
            scatter_all(i + 1, 1)

    return scatter_kernel(rows, idx_w)


def _combine_streamed_body(y_ref, gate_ref, h_ref, sg_ref, su_ref, sd_ref, g_ref, b_ref, *rest):
    out_ref = rest[-1]
    half = h_ref.shape[1] // 2
    h = h_ref[...]
    hb = h.astype(BF16)
    sg = jnp.dot(hb, sg_ref[...], preferred_element_type=F32)
    su = jnp.dot(hb, su_ref[...], preferred_element_type=F32)
    shared = jnp.dot((sg * jax.nn.sigmoid(sg) * su).astype(BF16), sd_ref[...], preferred_element_type=F32)
    z = DEEPNORM_ALPHA * h + shared
    z_hi, z_lo = z[:, :half], z[:, half:]
    for k in range(TOP_K):
        y_hi, y_lo = _unpack_bf16_pairs(y_ref[k])
        gate = gate_ref[:, k:k + 1]
        z_hi = z_hi + gate * y_hi
        z_lo = z_lo + gate * y_lo
    out_ref[...] = _layer_norm_rows(jnp.concatenate([z_hi, z_lo], axis=1), g_ref[...], b_ref[...])


def _combine_streamed(y_part, gate_tk, h2d, sh_gate, sh_up, sh_down, ln_g, ln_b, part, partial_out):
    t, d = h2d.shape
    tt = COMBINE_ROWS
    steps = y_part.shape[1] // tt
    first = part * steps
    full = lambda a: pl.BlockSpec(a.shape, lambda i: (0, 0))
    in_specs = [pl.BlockSpec((TOP_K, tt, d // 2), lambda i: (0, i, 0)),
                pl.BlockSpec((tt, TOP_K), lambda i: (first + i, 0)),
                pl.BlockSpec((tt, d), lambda i: (first + i, 0)),
                full(sh_gate), full(sh_up), full(sh_down), full(ln_g), full(ln_b)]
    args = [y_part, gate_tk, h2d, sh_gate, sh_up, sh_down, ln_g, ln_b]
    aliases = {}
    if partial_out is not None:
        in_specs.append(pl.BlockSpec(memory_space=pl.ANY))
        args.append(partial_out)
        aliases = {len(args) - 1: 0}
    return pl.pallas_call(
        _combine_streamed_body,
        grid=(steps,),
        in_specs=in_specs,
        out_specs=pl.BlockSpec((tt, d), lambda i: (first + i, 0)),
        out_shape=jax.ShapeDtypeStruct((t, d), F32),
        input_output_aliases=aliases,
        compiler_params=_cparams("arbitrary"),
        name="moe_combine_streamed_ln",
    )(*args)


def _mixer_sublayer(h2d, nb, s, w_in_p, w_out_p, a_biases, c_bias, lower_bound, norm_g_pad, sinks, w_sum, ln_g, ln_b):
    a_qkv, a_residue, b_all, c_qkv = _in_proj(h2d, w_in_p, nb, s)
    a_by_dilation = {1: a_qkv.reshape(nb, 1, s, A_COLS), **dict(zip(IN_PROJ_DILATIONS, a_residue))}
    o_list, l_list = [], []
    for (window, r), bias in zip(A_PATTERNS, a_biases):
        o, lse = _band_attn(a_by_dilation[r], bias, width=A_WIDTH, max_dist=window // r, want_lse=True)
        o_list.append(o)
        l_list.append(lse)
    oc = _band_attn(c_qkv.reshape(nb, 1, s, C_COLS), c_bias, width=C_WIDTH,
                    max_dist=C_WINDOW - 1, sinks=sinks, want_lse=False).reshape(nb * s, C_WIDTH)
    ob = _hgrn(b_all.reshape(nb, s, B_COLS), lower_bound, norm_g_pad, w_sum).reshape(nb * s, B_PAD_WIDTH)
    return _mix_out(o_list, l_list, ob, oc, h2d, w_out_p, ln_g, ln_b)


def _moe_sublayer(h2d, layer, router_w, router_bias, w_gate, w_up, w_down, sh_gate, sh_up, sh_down, ln_g, ln_b):
    t, d = h2d.shape
    bm = EXPERT_BLOCK
    e_t, rank_t, gate_t, counts, h_packed = _router(h2d, router_w.T.astype(F32),
                                                    router_bias.astype(F32).reshape(N_EXPERTS, 1))

    counts = counts.reshape(N_EXPERTS)
    padded = (counts + bm - 1) // bm * bm
    pad_end = jnp.cumsum(padded).astype(I32)
    offsets = pad_end - padded
    n_blocks = -(-(t * TOP_K + N_EXPERTS * (bm - 1)) // bm)

    dest_t = _dest_rows(e_t, rank_t, offsets)
    xs = _sc_scatter_rows(h_packed, dest_t, n_blocks * bm)
    ys = _experts(xs, offsets, (padded // bm).astype(I32), counts, jnp.full((1,), layer, I32), w_gate, w_up, w_down)
    tp = t // COMBINE_PARTS
    gate_tk = gate_t.T
    shared_w = (sh_gate.astype(BF16), sh_up.astype(BF16), sh_down.astype(BF16))
    out = None
    for part in range(COMBINE_PARTS):
        idx = dest_t[:, part * tp:(part + 1) * tp].reshape(TOP_K * tp)
        y_part = _sc_gather_rows(ys, idx).reshape(TOP_K, tp, d // 2)
        out = _combine_streamed(y_part, gate_tk, h2d, *shared_w, ln_g, ln_b, part, out)
    return out


def kernel(x, w_in, w_out, rel_bias_table, lower_bound_logits, hgrn_norm_g, attn_sinks, ln1_g, ln1_b, router_w, router_bias, expert_w_gate, expert_w_up, expert_w_down, shared_w_gate, shared_w_up, shared_w_down, ln2_g, ln2_b):
    nb, s, d = x.shape
    depth = w_in.shape[0]
    lb_probs = jax.nn.softmax(lower_bound_logits.astype(F32), axis=0)
    lower_bounds = jnp.cumsum(lb_probs, axis=0) - lb_probs[0]
    rel_table = rel_bias_table.astype(F32)
    a_biases = [_band_bias(rel_table, r, 0, A_HEADS) for _, r in A_PATTERNS]
    c_bias = _band_bias(rel_table, 1, A_HEADS, A_HEADS + C_HEADS)
    w_sum = jnp.asarray(_hgrn_sum_matrix(), dtype=BF16)
    row = lambda v: v.astype(F32).reshape(1, -1)

    h = x.astype(F32).reshape(nb * s, d)
    for l in range(depth):
        h = _mixer_sublayer(h, nb, s, _prep_w_in(w_in[l]), _prep_w_out(w_out[l].astype(F32)), a_biases, c_bias,
                            lower_bounds[l].reshape(1, B_KEY_WIDTH), _pad_heads_vec(hgrn_norm_g[l].astype(F32)),
                            attn_sinks[l].astype(F32), w_sum, row(ln1_g[l]), row(ln1_b[l]))
        h = _moe_sublayer(h, l, router_w[l], router_bias[l], expert_w_gate, expert_w_up, expert_w_down,
                          shared_w_gate[l], shared_w_up[l], shared_w_down[l], row(ln2_g[l]), row(ln2_b[l]))
    return h.reshape(nb, s, d).astype(x.dtype)
```

```python
import functools
import math

import numpy as np
import jax
import jax.numpy as jnp
from jax import lax
from jax.experimental import pallas as pl
from jax.experimental.pallas import tpu as pltpu
from jax.experimental.pallas import tpu_sc as plsc

F32 = jnp.float32
BF16 = jnp.bfloat16
I32 = jnp.int32
U32 = jnp.uint32

LANES = 128
SUBLANES = 8
VMEM_LIMIT = 56 * 1024 * 1024

D_MODEL = 1024
DEPTH = 2
HEAD_DIM = 64
BAND = 128
BAND_Q_TILE = 1024
MASK_VALUE = -1e30

A_HEADS = 6
A_PATTERNS = ((128, 1), (512, 4), (2048, 16))
IN_PROJ_DILATIONS = tuple(r for _, r in A_PATTERNS if r > 1)
IN_PROJ_ROWS = 512
MIX_OUT_ROWS = 512
B_HEADS = 4
B_KEY_DIM = 128
B_VAL_DIM = 96
B_VAL_PAD = 128
HG_CHUNK = 64
HG_STEP_ROWS = 512
C_HEADS = 4
C_KV_HEADS = 2
C_WINDOW = 128

A_WIDTH = A_HEADS * HEAD_DIM
B_KEY_WIDTH = B_HEADS * B_KEY_DIM
B_WIDTH = B_HEADS * B_VAL_DIM
B_PAD_WIDTH = B_HEADS * B_VAL_PAD
C_WIDTH = C_HEADS * HEAD_DIM
C_KV_WIDTH = C_KV_HEADS * HEAD_DIM
IN_SPLITS = (A_WIDTH, A_WIDTH, A_WIDTH, B_KEY_WIDTH, B_KEY_WIDTH, B_WIDTH, B_WIDTH, C_WIDTH, C_KV_WIDTH, C_KV_WIDTH)

A_COLS = 3 * A_WIDTH
B_COLS = 4 * B_KEY_WIDTH
C_COLS = 3 * C_WIDTH
MIX_PAD_WIDTH = A_WIDTH + B_PAD_WIDTH + C_WIDTH

REL_BUCKETS = 32
REL_MAX_DIST = 2048

N_EXPERTS = 256
TOP_K = 8
N_GROUPS = 8
TOPK_GROUPS = 4
EXPERT_FF = 256
SHARED_FF = 256
ROUTED_SCALE = 2.5
ROUTER_TOKENS = 512
DEST_TOKENS = 2048
EXPERT_BLOCK = 512
EXPERT_SLOTS = 3
COMBINE_PARTS = 4
COMBINE_ROWS = 512

DEEPNORM_ALPHA = (2 * DEPTH) ** 0.25
LN_EPS = 1e-5


def _cparams(*sem):
    return pltpu.CompilerParams(dimension_semantics=sem, vmem_limit_bytes=VMEM_LIMIT)


def _prep_w_in(w_in):
    d = w_in.shape[0]
    split_at = [int(i) for i in np.cumsum(IN_SPLITS)[:-1]]
    aq, ak, av, bq, bf, bi, bg, cq, ck, cv = jnp.split(w_in, split_at, axis=-1)
    pad_v = lambda w: jnp.pad(w.reshape(d, B_HEADS, B_VAL_DIM), ((0, 0), (0, 0), (0, B_VAL_PAD - B_VAL_DIM))).reshape(d, B_PAD_WIDTH)
    rep = lambda w: jnp.repeat(w.reshape(d, C_KV_HEADS, HEAD_DIM), C_HEADS // C_KV_HEADS, axis=1).reshape(d, C_WIDTH)
    scale = HEAD_DIM ** -0.5
    cols = [aq * scale, ak, av, bq, bf, pad_v(bi), pad_v(bg), cq * scale, rep(ck), rep(cv)]
    return jnp.concatenate(cols, axis=-1).astype(BF16)


def _prep_w_out(w_out):
    d = w_out.shape[1]
    wa = w_out[:A_WIDTH]
    wb = w_out[A_WIDTH:A_WIDTH + B_WIDTH].reshape(B_HEADS, B_VAL_DIM, d)
    wb = jnp.pad(wb, ((0, 0), (0, B_VAL_PAD - B_VAL_DIM), (0, 0))).reshape(B_PAD_WIDTH, d)
    wc = w_out[A_WIDTH + B_WIDTH:]
    return jnp.concatenate([wa, wb, wc], axis=0).astype(BF16)


def _pad_heads_vec(v):
    return jnp.pad(v.reshape(B_HEADS, B_VAL_DIM), ((0, 0), (0, B_VAL_PAD - B_VAL_DIM))).reshape(1, B_PAD_WIDTH)


def _rel_bucket(dist):
    max_exact = REL_BUCKETS // 2
    d = jnp.maximum(dist, 0)
    log_ratio = jnp.log(jnp.maximum(d, max_exact).astype(F32) / max_exact) / math.log(REL_MAX_DIST / max_exact)
    large = jnp.minimum(max_exact + (log_ratio * (REL_BUCKETS - max_exact)).astype(I32), REL_BUCKETS - 1)
    return jnp.where(d < max_exact, d, large)


def _band_bias(rel_table, r, head_lo, head_hi):
    dist = jnp.arange(BAND)[:, None] + BAND - jnp.arange(2 * BAND)[None, :]
    onehot = jax.nn.one_hot(_rel_bucket(dist * r), REL_BUCKETS, dtype=F32)
    return jnp.einsum("qkb,bh->hqk", onehot, rel_table[:, head_lo:head_hi], precision=lax.Precision.HIGHEST)


def _hgrn_sum_matrix():
    return np.tril(np.ones((HG_CHUNK, HG_CHUNK), np.float32))


HG_LEVELS = int(math.log2(HG_CHUNK))


def _in_proj_body(x_ref, w_ref, *rest):
    n_res = len(IN_PROJ_DILATIONS)
    a_ref, res_refs, (b_ref, c_ref, slabs) = rest[0], rest[1:1 + n_res], rest[1 + n_res:]
    tm = x_ref.shape[0]
    xb = x_ref[...].astype(BF16)
    a = jnp.dot(xb, w_ref[:, :A_COLS], preferred_element_type=F32)
    a_ref[...] = a.astype(BF16)
    for j in range(A_COLS // LANES):
        slabs[0, j] = a[:, j * LANES:(j + 1) * LANES]
    prev_r = 1
    for level, (r, ref) in enumerate(zip(IN_PROJ_DILATIONS, res_refs)):
        step, n, n_prev = r // prev_r, tm // r, tm // prev_r
        src, dst = slabs.at[level % 2], slabs.at[(level + 1) % 2]
        for p in range(r):
            p_prev, q = p % prev_r, p // prev_r
            for j in range(A_COLS // LANES):
                piece = src[j, pl.ds(p_prev * n_prev + q, n, stride=step), :]
                if level + 1 < len(IN_PROJ_DILATIONS):
                    dst[j, p * n:(p + 1) * n, :] = piece
                ref[0, p, :, j * LANES:(j + 1) * LANES] = piece.astype(BF16)
        prev_r = r
    for j in range(B_COLS // B_KEY_WIDTH):
        lo = A_COLS + j * B_KEY_WIDTH
        b_ref[:, j * B_KEY_WIDTH:(j + 1) * B_KEY_WIDTH] = jnp.dot(
            xb, w_ref[:, lo:lo + B_KEY_WIDTH], preferred_element_type=F32)
    c_ref[...] = jnp.dot(xb, w_ref[:, A_COLS + B_COLS:], preferred_element_type=F32).astype(BF16)


def _in_proj(x2d, w_p, nb, s):
    t, d = x2d.shape
    tm = IN_PROJ_ROWS
    n = w_p.shape[1]
    tiles = s // tm
    res_specs = [pl.BlockSpec((1, r, tm // r, A_COLS), lambda i: (i // tiles, 0, i % tiles, 0))
                 for r in IN_PROJ_DILATIONS]
    res_shapes = [jax.ShapeDtypeStruct((nb, r, s // r, A_COLS), BF16) for r in IN_PROJ_DILATIONS]
    outs = pl.pallas_call(
        _in_proj_body,
        grid=(t // tm,),
        in_specs=[pl.BlockSpec((tm, d), lambda i: (i, 0)),
                  pl.BlockSpec((d, n), lambda i: (0, 0))],
        out_specs=[pl.BlockSpec((tm, A_COLS), lambda i: (i, 0))] + res_specs + [
                   pl.BlockSpec((tm, B_COLS), lambda i: (i, 0)),
                   pl.BlockSpec((tm, C_COLS), lambda i: (i, 0))],
        out_shape=[jax.ShapeDtypeStruct((t, A_COLS), BF16)] + res_shapes + [
                   jax.ShapeDtypeStruct((t, B_COLS), F32),
                   jax.ShapeDtypeStruct((t, C_COLS), BF16)],
        scratch_shapes=[pltpu.VMEM((2, A_COLS // LANES, tm, LANES), F32)],
        compiler_params=_cparams("arbitrary"),
        name="in_proj",
    )(x2d, w_p)
    return outs[0], list(outs[1:-2]), outs[-2], outs[-1]


def _band_attn_body(*refs, width, max_dist, has_sink, want_lse):
    q_ref, kp_ref, kc_ref, vp_ref, vc_ref, bias_ref = refs[:6]
    rest = refs[6:]
    if has_sink:
        sink_ref, rest = rest[0], rest[1:]
    o_ref = rest[0]
    lse_ref = rest[1] if want_lse else None

    first_tile = pl.program_id(2) == 0
    row = lax.broadcasted_iota(I32, (BAND, 2 * BAND), 0)
    col = lax.broadcasted_iota(I32, (BAND, 2 * BAND), 1)
    dist = row + BAND - col
    in_band = (dist >= 0) & (dist <= max_dist)
    first_mask = in_band & ((col >= BAND) | jnp.logical_not(first_tile))
    lane = lax.broadcasted_iota(I32, (BAND, LANES), 1)
    low_half = lane < HEAD_DIM

    for qb in range(q_ref.shape[0] // BAND):
        rows = slice(qb * BAND, (qb + 1) * BAND)
        mask = first_mask if qb == 0 else in_band
        lse_tile = jnp.zeros((BAND, LANES), F32)
        for tile in range(width // LANES):
            sl = slice(tile * LANES, (tile + 1) * LANES)
            q2 = q_ref[rows, sl]
            if qb == 0:
                k2 = jnp.concatenate([kp_ref[:, sl], kc_ref[:BAND, sl]], axis=0)
                v2 = jnp.concatenate([vp_ref[:, sl], vc_ref[:BAND, sl]], axis=0)
            else:
                k2 = kc_ref[(qb - 1) * BAND:(qb + 1) * BAND, sl]
                v2 = vc_ref[(qb - 1) * BAND:(qb + 1) * BAND, sl]
            outs, lses = [], []
            for half in range(2):
                h = 2 * tile + half
                qm = jnp.where(low_half if half == 0 else jnp.logical_not(low_half), q2, jnp.zeros_like(q2))
                s = lax.dot_general(qm, k2, (((1,), (1,)), ((), ())), preferred_element_type=F32)
                s = s + bias_ref[h]
                s = jnp.where(mask, s, MASK_VALUE)
                m = jnp.max(s, axis=-1, keepdims=True)
                if has_sink:
                    sink = sink_ref[h]
                    m = jnp.maximum(m, sink)
                p = jnp.exp(s - m)
                den = jnp.sum(p, axis=-1, keepdims=True)
                if has_sink:
                    den = den + jnp.exp(sink - m)
                pv = jnp.dot(p.astype(BF16), v2, preferred_element_type=F32)
                outs.append(pv / den)
                if want_lse:
                    lse_tile = jnp.where(lane == h, m + jnp.log(den), lse_tile)
            o_ref[rows, sl] = jnp.where(low_half, outs[0], outs[1]).astype(o_ref.dtype)
        if want_lse:
            lse_ref[rows, :] = lse_tile


def _band_attn(src, bias, *, width, max_dist, sinks=None, want_lse):
    nb, r, length, _ = src.shape
    qt = min(BAND_Q_TILE, length)
    bands = qt // BAND
    heads = width // HEAD_DIM
    has_sink = sinks is not None

    def cur(off):
        return pl.BlockSpec((None, None, qt, width), lambda b, p, i: (b, p, i, off))

    def prev(off):
        return pl.BlockSpec((None, None, BAND, width), lambda b, p, i: (b, p, jnp.maximum(i * bands - 1, 0), off))

    in_specs = [cur(0), prev(1), cur(1), prev(2), cur(2),
                pl.BlockSpec((heads, BAND, 2 * BAND), lambda b, p, i: (0, 0, 0))]
    args = [src, src, src, src, src, bias]
    if has_sink:
        in_specs.append(pl.BlockSpec(memory_space=pltpu.SMEM))
        args.append(sinks)
    out_spec = pl.BlockSpec((None, None, qt, width), lambda b, p, i: (b, p, i, 0))
    out_sds = jax.ShapeDtypeStruct((nb, r, length, width), F32 if want_lse else BF16)
    lse_spec = pl.BlockSpec((None, None, qt, LANES), lambda b, p, i: (b, p, i, 0))
    lse_sds = jax.ShapeDtypeStruct((nb, r, length, LANES), F32)
    body = functools.partial(_band_attn_body, width=width, max_dist=max_dist, has_sink=has_sink, want_lse=want_lse)
    return pl.pallas_call(
        body,
        grid=(nb, r, length // qt),
        in_specs=in_specs,
        out_specs=[out_spec, lse_spec] if want_lse else out_spec,
        out_shape=[out_sds, lse_sds] if want_lse else out_sds,
        compiler_params=_cparams("arbitrary", "arbitrary", "arbitrary"),
        name="band_attn_r%d_w%d" % (r, width),
    )(*args)


def _hgrn_body(b_ref, lb_ref, ng_ref, w_ref, o_ref, state_ref):
    c = HG_CHUNK
    kd = B_KEY_DIM

    @pl.when(pl.program_id(1) == 0)
    def _():
        state_ref[...] = jnp.zeros_like(state_ref)

    trow = lax.broadcasted_iota(I32, (c, 1), 0)
    odd_row = (trow % 2) == 1
    low_sub = lax.broadcasted_iota(I32, (SUBLANES, 1), 0) < SUBLANES // 2
    ti = lax.broadcasted_iota(I32, (c, c), 0)
    si = lax.broadcasted_iota(I32, (c, c), 1)
    nt = (((1,), (1,)), ((), ()))
    tri = w_ref[...]

    in_level = [((ti // m) == (si // m)) & ((ti % m) >= (m // 2)) & ((si % m) < (m // 2))
                for m in (c >> lvl for lvl in range(HG_LEVELS))]

    states = [state_ref[h] for h in range(B_HEADS)]
    for chunk, h in [(ci, hi) for ci in range(b_ref.shape[1] // c) for hi in range(B_HEADS)]:
        rows = slice(chunk * c, (chunk + 1) * c)
        ks = slice(h * kd, (h + 1) * kd)
        q = b_ref[0, rows, ks]
        f = b_ref[0, rows, B_KEY_WIDTH + h * kd:B_KEY_WIDTH + (h + 1) * kd]
        inp = b_ref[0, rows, 2 * B_KEY_WIDTH + h * kd:2 * B_KEY_WIDTH + (h + 1) * kd]
        gate = b_ref[0, rows, 3 * B_KEY_WIDTH + h * kd:3 * B_KEY_WIDTH + (h + 1) * kd]
        lb = lb_ref[:, ks]

        a = jnp.exp(-jnp.abs(f))
        big = 1.0 / (1.0 + a)
        small = a * big
        pos = f >= 0.0
        forget = lb + (1.0 - lb) * jnp.where(pos, big, small)
        log_f = jnp.log(forget)
        key = (1.0 - lb) * jnp.where(pos, small, big)
        qs = q * jax.nn.sigmoid(q)

        g_hi = log_f.astype(BF16)
        g_lo = (log_f - g_hi.astype(F32)).astype(BF16)
        cum2 = jnp.dot(tri, jnp.concatenate([g_hi, g_lo], axis=1), preferred_element_type=F32)
        cum = cum2[:, :kd] + cum2[:, kd:]
        last = cum[c - 1:c]

        q_dec = (qs * jnp.exp(cum)).astype(BF16)
        k_dec = (key * jnp.exp(last - cum)).astype(BF16)
        inp_b = inp.astype(BF16)

        def level_decay(m):
            if m == 2:
                return jnp.where(odd_row, forget, 1.0)
            if m == SUBLANES // 2:
                row_bcast = lambda i: jnp.broadcast_to(cum[i:i + 1], (SUBLANES, kd))
                pieces = [jnp.where(low_sub, row_bcast(SUBLANES * j + m // 2 - 1), row_bcast(SUBLANES * j + m + m // 2 - 1))
                          for j in range(c // SUBLANES)]
            else:
                pieces = [jnp.broadcast_to(cum[b * m + m // 2 - 1:b * m + m // 2], (m, kd)) for b in range(c // m)]
            return jnp.exp(-jnp.abs(cum - jnp.concatenate(pieces, axis=0)))

        st = states[h]
        inter = lax.dot_general(q_dec, st.astype(BF16), nt, preferred_element_type=F32)

        scores = jnp.where(ti == si,
                           lax.dot_general(qs.astype(BF16), key.astype(BF16), nt, preferred_element_type=F32), 0.0)
        m = c
        for lvl in range(HG_LEVELS):
            el = level_decay(m)
            sl = lax.dot_general((qs * el).astype(BF16), (key * el).astype(BF16), nt, preferred_element_type=F32)
            scores = jnp.where(in_level[lvl], sl, scores)
            m //= 2
        intra = jnp.dot(scores.astype(BF16), inp_b, preferred_element_type=F32)

        new_st = st * jnp.exp(last) + lax.dot_general(
            inp_b, k_dec, (((0,), (0,)), ((), ())), preferred_element_type=F32)
        states[h] = new_st

        o = inter + intra
        ms = jnp.sum(o * o, axis=-1, keepdims=True) * (1.0 / B_VAL_DIM)
        o = o * lax.rsqrt(ms + 1e-6)
        o_ref[0, rows, ks] = (o * ng_ref[:, ks] * (gate * jax.nn.sigmoid(gate))).astype(o_ref.dtype)

    for h in range(B_HEADS):
        state_ref[h] = states[h]


def _hgrn(b_all, lower_bound, norm_g_pad, w_sum):
    nb, s, _ = b_all.shape
    c = HG_STEP_ROWS
    return pl.pallas_call(
        _hgrn_body,
        grid=(nb, s // c),
        in_specs=[pl.BlockSpec((1, c, B_COLS), lambda b, i: (b, i, 0)),
                  pl.BlockSpec((1, B_KEY_WIDTH), lambda b, i: (0, 0)),
                  pl.BlockSpec((1, B_PAD_WIDTH), lambda b, i: (0, 0)),
                  pl.BlockSpec(w_sum.shape, lambda b, i: (0, 0))],
        out_specs=pl.BlockSpec((1, c, B_PAD_WIDTH), lambda b, i: (b, i, 0)),
        out_shape=jax.ShapeDtypeStruct((nb, s, B_PAD_WIDTH), BF16),
        scratch_shapes=[pltpu.VMEM((B_HEADS, B_VAL_PAD, B_KEY_DIM), F32)],
        compiler_params=_cparams("arbitrary", "arbitrary"),
        name="hgrn2",
    )(b_all, lower_bound, norm_g_pad, w_sum)


def _layer_norm_rows(z, g, b):
    mu = jnp.mean(z, axis=-1, keepdims=True)
    zc = z - mu
    var = jnp.mean(zc * zc, axis=-1, keepdims=True)
    return zc * lax.rsqrt(var + LN_EPS) * g + b


def _mix_out_body(*refs):
    n_pat = len(A_PATTERNS)
    o_refs, l_refs = refs[:n_pat], refs[n_pat:2 * n_pat]
    ob, oc, x_ref, w_ref, g_ref, b_ref, spread_ref, out_ref = refs[2 * n_pat:2 * n_pat + 8]
    scratch = refs[2 * n_pat + 8:]
    tm = x_ref.shape[0]
    n_slabs = A_WIDTH // LANES

    def token_order(ref, r, scr):
        slabs = ref.shape[-1] // LANES
        if r == 1:
            return [ref[0, 0, :, j * LANES:(j + 1) * LANES] for j in range(slabs)]
        for p in range(r):
            for j in range(slabs):
                scr[j, pl.ds(p, tm // r, stride=r), :] = ref[0, p, :, j * LANES:(j + 1) * LANES]
        return [scr[j] for j in range(slabs)]

    scr_iter = iter(scratch)
    dil = [r for _, r in A_PATTERNS]
    o_slabs = [token_order(ref, r, None if r == 1 else next(scr_iter)) for ref, r in zip(o_refs, dil)]
    ls = [token_order(ref, r, None if r == 1 else next(scr_iter))[0] for ref, r in zip(l_refs, dil)]

    m = functools.reduce(jnp.maximum, ls)
    ws = [jnp.exp(l - m) for l in ls]
    inv = 1.0 / functools.reduce(lambda a, b: a + b, ws)
    spread = spread_ref[...]
    wide = []
    for w in ws:
        wn = w * inv
        hi = wn.astype(BF16)
        lo = (wn - hi.astype(F32)).astype(BF16)
        wide.append(jnp.dot(hi, spread, preferred_element_type=F32) + jnp.dot(lo, spread, preferred_element_type=F32))
    merged = [functools.reduce(lambda a, b: a + b,
                               [w[:, j * LANES:(j + 1) * LANES] * o[j] for w, o in zip(wide, o_slabs)])
              for j in range(n_slabs)]
    cat = jnp.concatenate([m.astype(BF16) for m in merged] + [ob[...], oc[...]], axis=1)
    y = jnp.dot(cat, w_ref[...], preferred_element_type=F32)
    z = DEEPNORM_ALPHA * x_ref[...] + y
    out_ref[...] = _layer_norm_rows(z, g_ref[...], b_ref[...])


def _mix_out(o_list, l_list, ob, oc, x2d, w_out_p, ln_g, ln_b):
    t, d = x2d.shape
    tm = MIX_OUT_ROWS
    tiles = o_list[0].shape[1] * o_list[0].shape[2] // tm
    row = lambda w: pl.BlockSpec((tm, w), lambda i: (i, 0))
    full = lambda a: pl.BlockSpec(a.shape, lambda i: (0, 0))
    res = lambda a: pl.BlockSpec((1, a.shape[1], tm // a.shape[1], a.shape[3]), lambda i: (i // tiles, 0, i % tiles, 0))
    spread = jnp.asarray(np.arange(LANES)[:, None] == np.arange(A_WIDTH)[None, :] // HEAD_DIM, dtype=BF16)
    dilated = [a for a in o_list + l_list if a.shape[1] > 1]
    return pl.pallas_call(
        _mix_out_body,
        grid=(t // tm,),
        in_specs=[res(a) for a in o_list + l_list] + [row(B_PAD_WIDTH), row(C_WIDTH), row(d), full(w_out_p),
                                                      full(ln_g), full(ln_b), full(spread)],
        out_specs=row(d),
        out_shape=jax.ShapeDtypeStruct((t, d), F32),
        scratch_shapes=[pltpu.VMEM((a.shape[3] // LANES, tm, LANES), F32) for a in dilated],
        compiler_params=_cparams("arbitrary"),
        name="mix_out_ln",
    )(*o_list, *l_list, ob, oc, x2d, w_out_p, ln_g, ln_b, spread)


def _router_body(h_ref, rw_ref, bias_ref, tri_ref, e_ref, rank_ref, gate_ref, cnt_ref, hp_ref, carry_ref):
    tn = h_ref.shape[0]
    per_group = N_EXPERTS // N_GROUPS
    neg_inf = -jnp.inf

    @pl.when(pl.program_id(0) == 0)
    def _():
        carry_ref[...] = jnp.zeros_like(carry_ref)

    hp_ref[...] = _pack_bf16_pairs(h_ref[...])

    h = h_ref[...]
    h_hi = h.astype(BF16)
    h_lo = (h - h_hi.astype(F32)).astype(BF16)
    nt = (((1,), (1,)), ((), ()))
    w_hi, w_lo = rw_ref[0], rw_ref[1]
    logits = (lax.dot_general(w_hi, h_hi, nt, preferred_element_type=F32)
              + lax.dot_general(w_hi, h_lo, nt, preferred_element_type=F32)
              + lax.dot_general(w_lo, h_hi, nt, preferred_element_type=F32))
    scores = jax.nn.sigmoid(logits)
    choice = scores + bias_ref[...]

    def first_max(vals, idx, sentinel):
        top = jnp.max(vals, axis=0, keepdims=True)
        return top, jnp.min(jnp.where(vals == top, idx, sentinel), axis=0, keepdims=True)

    li = lax.broadcasted_iota(I32, (per_group, tn), 0).astype(F32)
    group_rows = []
    for g in range(N_GROUPS):
        cg = choice[g * per_group:(g + 1) * per_group]
        m1, first = first_max(cg, li, float(per_group))
        m2 = jnp.max(jnp.where(li == first, neg_inf, cg), axis=0, keepdims=True)
        group_rows.append(m1 + m2)
    group_score = jnp.concatenate(group_rows, axis=0)

    gi = lax.broadcasted_iota(I32, (N_GROUPS, tn), 0).astype(F32)
    group_ok = jnp.zeros((N_GROUPS, tn), F32)
    cur = group_score
    for _ in range(TOPK_GROUPS):
        _, first = first_max(cur, gi, float(N_GROUPS))
        pick = gi == first
        group_ok = jnp.where(pick, 1.0, group_ok)
        cur = jnp.where(pick, neg_inf, cur)

    cur = jnp.concatenate(
        [jnp.where(group_ok[g:g + 1] > 0.0, choice[g * per_group:(g + 1) * per_group], MASK_VALUE)
         for g in range(N_GROUPS)], axis=0)
    ei = lax.broadcasted_iota(I32, (N_EXPERTS, tn), 0).astype(F32)
    chosen = jnp.zeros((N_EXPERTS, tn), F32)
    picks, gates = [], []
    for _ in range(TOP_K):
        _, idx = first_max(cur, ei, float(N_EXPERTS))
        pick = ei == idx
        picks.append(idx)
        gates.append(jnp.sum(jnp.where(pick, scores, 0.0), axis=0, keepdims=True))
        chosen = jnp.where(pick, 1.0, chosen)
        cur = jnp.where(pick, neg_inf, cur)

    gate = jnp.concatenate(gates, axis=0)
    gate_ref[...] = gate / jnp.sum(gate, axis=0, keepdims=True) * ROUTED_SCALE
    e_ref[...] = jnp.concatenate(picks, axis=0).astype(I32)

    before = jnp.dot(chosen.astype(BF16), tri_ref[...], preferred_element_type=F32) + carry_ref[...]
    ranks = [jnp.sum(jnp.where(ei == idx, before, 0.0), axis=0, keepdims=True) for idx in picks]
    rank_ref[...] = jnp.concatenate(ranks, axis=0).astype(I32)
    carry = carry_ref[...] + jnp.sum(chosen, axis=1, keepdims=True)
    carry_ref[...] = carry
    cnt_ref[...] = carry.astype(I32)


def _router(h2d, rw_t, bias_col):
    t, d = h2d.shape
    tn = ROUTER_TOKENS
    tri = jnp.asarray(np.triu(np.ones((tn, tn), np.float32), k=1), dtype=BF16)
    rw_hi = rw_t.astype(BF16)
    rw_split = jnp.stack([rw_hi, (rw_t - rw_hi.astype(F32)).astype(BF16)])
    tok = lambda: pl.BlockSpec((TOP_K, tn), lambda i: (0, i))
    return pl.pallas_call(
        _router_body,
        grid=(t // tn,),
        in_specs=[pl.BlockSpec((tn, d), lambda i: (i, 0)),
                  pl.BlockSpec((2, N_EXPERTS, d), lambda i: (0, 0, 0)),
                  pl.BlockSpec((N_EXPERTS, 1), lambda i: (0, 0)),
                  pl.BlockSpec((tn, tn), lambda i: (0, 0))],
        out_specs=[tok(), tok(), tok(), pl.BlockSpec((N_EXPERTS, 1), lambda i: (0, 0)),
                   pl.BlockSpec((tn, d // 2), lambda i: (i, 0))],
        out_shape=[jax.ShapeDtypeStruct((TOP_K, t), I32),
                   jax.ShapeDtypeStruct((TOP_K, t), I32),
                   jax.ShapeDtypeStruct((TOP_K, t), F32),
                   jax.ShapeDtypeStruct((N_EXPERTS, 1), I32),
                   jax.ShapeDtypeStruct((t, d // 2), U32)],
        scratch_shapes=[pltpu.VMEM((N_EXPERTS, 1), F32)],
        compiler_params=_cparams("arbitrary"),
        name="moe_router",
    )(h2d, rw_split, bias_col, tri)


def _pack_bf16_pairs(x):
    w = x.shape[1] // 2
    hi = lax.bitcast_convert_type(x[:, :w].astype(BF16).astype(F32), U32)
    lo = lax.bitcast_convert_type(x[:, w:].astype(BF16).astype(F32), U32)
    return hi | (lo >> 16)


def _unpack_bf16_pairs(p):
    hi = lax.bitcast_convert_type(p & jnp.uint32(0xFFFF0000), F32)
    lo = lax.bitcast_convert_type(p << 16, F32)
    return hi, lo


def _dest_body(e_ref, rank_ref, offs_ref, dest_ref):
    tn = e_ref.shape[1]
    ei = lax.broadcasted_iota(I32, (N_EXPERTS, tn), 0)
    offs = offs_ref[...]
    rows = [jnp.sum(jnp.where(ei == e_ref[k:k + 1, :], offs, 0.0), axis=0, keepdims=True) for k in range(TOP_K)]
    dest_ref[...] = jnp.concatenate(rows, axis=0).astype(I32) + rank_ref[...]


def _dest_rows(e_t, rank_t, offsets):
    t = e_t.shape[1]
    tn = DEST_TOKENS
    tok = pl.BlockSpec((TOP_K, tn), lambda i: (0, i))
    return pl.pallas_call(
        _dest_body,
        grid=(t // tn,),
        in_specs=[tok, tok, pl.BlockSpec((N_EXPERTS, 1), lambda i: (0, 0))],
        out_specs=tok,
        out_shape=jax.ShapeDtypeStruct((TOP_K, t), I32),
        compiler_params=_cparams("arbitrary"),
        name="moe_dest",
    )(e_t, rank_t, offsets.astype(F32).reshape(N_EXPERTS, 1))


def _expert_body(offs_ref, nblk_ref, cnt_ref, layer_ref, xs_hbm, wg_ref, wu_ref, wd_ref, ys_hbm,
                 xbuf, ybuf, wg_b, wu_b, wd_b, sem_in, sem_out):
    e = pl.program_id(0)
    slots, bm = xbuf.shape[:2]
    ahead = slots - 1
    n = nblk_ref[e]
    first = offs_ref[e] // bm
    total = (offs_ref[N_EXPERTS - 1] // bm) + nblk_ref[N_EXPERTS - 1]

    def rows(g):
        return pl.ds(pl.multiple_of(g * bm, bm), bm)

    def x_copy(g, slot):
        return pltpu.make_async_copy(xs_hbm.at[rows(g)], xbuf.at[slot], sem_in.at[slot])

    def y_copy(g, slot):
        return pltpu.make_async_copy(ybuf.at[slot], ys_hbm.at[rows(g)], sem_out.at[slot])

    @pl.when(e == 0)
    def _():
        for g0 in range(ahead):
            @pl.when(g0 < total)
            def _():
                x_copy(g0, g0).start()

    wg_b[...] = wg_ref[0, 0].astype(BF16)
    wu_b[...] = wu_ref[0, 0].astype(BF16)
    wd_b[...] = wd_ref[0, 0].astype(BF16)

    def block(j, carry):
        g = first + j
        slot = g % slots
        x_copy(g, slot).wait()

        @pl.when(g + ahead < total)
        def _():
            x_copy(g + ahead, (g + ahead) % slots).start()

        @pl.when(g >= slots)
        def _():
            y_copy(g - slots, slot).wait()

        live = lax.broadcasted_iota(I32, (bm, 1), 0) < (cnt_ref[e] - j * bm)
        hi, lo = _unpack_bf16_pairs(jnp.where(live, xbuf[slot], jnp.uint32(0)))
        xb = jnp.concatenate([hi.astype(BF16), lo.astype(BF16)], axis=1)
        gate = jnp.dot(xb, wg_b[...], preferred_element_type=F32)
        up = jnp.dot(xb, wu_b[...], preferred_element_type=F32)
        hidden = (gate * jax.nn.sigmoid(gate) * up).astype(BF16)
        ybuf[slot] = _pack_bf16_pairs(jnp.dot(hidden, wd_b[...], preferred_element_type=F32))
        y_copy(g, slot).start()
        return carry

    lax.fori_loop(0, n, block, 0)

    @pl.when(e == N_EXPERTS - 1)
    def _():
        for back in range(1, slots + 1):
            @pl.when(total >= back)
            def _():
                y_copy(total - back, (total - back) % slots).wait()


def _experts(xs, offsets, n_blk, counts, layer, w_gate, w_up, w_down):
    n_rows, dp = xs.shape
    bm = EXPERT_BLOCK
    d, ff = w_gate.shape[-2:]
    w_spec = lambda a, b: pl.BlockSpec((1, 1, a, b), lambda e, of, nb, ct, ly: (ly[0], e, 0, 0))
    grid_spec = pltpu.PrefetchScalarGridSpec(
        num_scalar_prefetch=4,
        grid=(N_EXPERTS,),
        in_specs=[pl.BlockSpec(memory_space=pl.ANY), w_spec(d, ff), w_spec(d, ff), w_spec(ff, d)],
        out_specs=pl.BlockSpec(memory_space=pl.ANY),
        scratch_shapes=[pltpu.VMEM((EXPERT_SLOTS, bm, dp), U32), pltpu.VMEM((EXPERT_SLOTS, bm, dp), U32),
                        pltpu.VMEM((d, ff), BF16), pltpu.VMEM((d, ff), BF16), pltpu.VMEM((ff, d), BF16),
                        pltpu.SemaphoreType.DMA((EXPERT_SLOTS,)), pltpu.SemaphoreType.DMA((EXPERT_SLOTS,))],
    )
    return pl.pallas_call(
        _expert_body,
        grid_spec=grid_spec,
        out_shape=jax.ShapeDtypeStruct((n_rows, dp), U32),
        compiler_params=_cparams("arbitrary"),
        name="moe_experts",
    )(offsets, n_blk, counts, layer, xs, w_gate, w_up, w_down)


SC_GATHER_ROWS = 64


def _sc_gather_rows(table, idx):
    info = plsc.get_sparse_core_info()
    nc, ns = info.num_cores, info.num_subcores
    workers = nc * ns
    n = idx.shape[0]
    w = table.shape[1]
    ch = SC_GATHER_ROWS
    per_worker = n // workers
    steps = per_worker // ch
    assert per_worker * workers == n and steps * ch == per_worker and steps % 2 == 0
    mesh = plsc.VectorSubcoreMesh(core_axis_name="c", subcore_axis_name="s")

    @functools.partial(
        pl.kernel, mesh=mesh,
        out_type=jax.ShapeDtypeStruct((n, w), table.dtype),
        scratch_types=[pltpu.VMEM((steps, ch), I32), pltpu.VMEM((2, ch, w), table.dtype),
                       pltpu.SemaphoreType.DMA((2,)), pltpu.SemaphoreType.DMA((2,))],
    )
    def gather_kernel(table_hbm, idx_hbm, out_hbm, idx_v, rows_v, gsem, wsem):
        wid = lax.axis_index("s") * nc + lax.axis_index("c")
        base = wid * per_worker
        pltpu.sync_copy(idx_hbm.at[wid], idx_v)

        def gather(i, slot):
            return pltpu.make_async_copy(table_hbm.at[idx_v.at[i]], rows_v.at[slot], gsem.at[slot])

        def write(i, slot):
            off = pl.multiple_of(base + i * ch, ch)
            return pltpu.make_async_copy(rows_v.at[slot], out_hbm.at[pl.ds(off, ch)], wsem.at[slot])

        gather(0, 0).start()

        @pl.loop(0, steps, step=2)
        def _(i):
            gather(i + 1, 1).start()
            gather(i, 0).wait()
            write(i, 0).start()
            write(i, 0).wait()

            @pl.when(i + 2 < steps)
            def _():
                gather(i + 2, 0).start()

            gather(i + 1, 1).wait()
            write(i + 1, 1).start()
            write(i + 1, 1).wait()

    return gather_kernel(table, idx.reshape(workers, steps, ch))


def _sc_scatter_rows(rows, idx, n_out):
    info = plsc.get_sparse_core_info()
    nc, ns = info.num_cores, info.num_subcores
    workers = nc * ns
    kk, t = idx.shape
    w = rows.shape[1]
    ch = SC_GATHER_ROWS
    per_worker = t // workers
    steps = per_worker // ch
    assert per_worker * workers == t and steps * ch == per_worker and steps % 2 == 0
    idx_w = idx.reshape(kk, workers, steps, ch).transpose(1, 0, 2, 3).reshape(workers, kk * steps, ch)
    mesh = plsc.VectorSubcoreMesh(core_axis_name="c", subcore_axis_name="s")

    @functools.partial(
        pl.kernel, mesh=mesh,
        out_type=jax.ShapeDtypeStruct((n_out, w), rows.dtype),
        scratch_types=[pltpu.VMEM((kk * steps, ch), I32), pltpu.VMEM((2, ch, w), rows.dtype),
                       pltpu.SemaphoreType.DMA((2,)), pltpu.SemaphoreType.DMA((2,))],
    )
    def scatter_kernel(rows_hbm, idx_hbm, out_hbm, idx_v, rows_v, rsem, ssem):
        wid = lax.axis_index("s") * nc + lax.axis_index("c")
        base = wid * per_worker
        pltpu.sync_copy(idx_hbm.at[wid], idx_v)

        def read(i, slot):
            off = pl.multiple_of(base + i * ch, ch)
            return pltpu.make_async_copy(rows_hbm.at[pl.ds(off, ch)], rows_v.at[slot], rsem.at[slot])

        def scatter(i, k, slot):
            return pltpu.make_async_copy(rows_v.at[slot], out_hbm.at[idx_v.at[k * steps + i]], ssem.at[slot])

        def scatter_all(i, slot):
            for k in range(kk):
                scatter(i, k, slot).start()
            for k in range(kk):
                scatter(i, k, slot).wait()

        read(0, 0).start()

        @pl.loop(0, steps, step=2)
        def _(i):
            read(i + 1, 1).start()
            read(i, 0).wait()
            scatter_all(i, 0)

            @pl.when(i + 2 < steps)
            def _():
                read(i + 2, 0).start()

            read(i + 1, 1).wait()
            scatter_all(i + 1, 1)

    return scatter_kernel(rows, idx_w)


def _combine_streamed_body(y_ref, gate_ref, h_ref, sg_ref, su_ref, sd_ref, g_ref, b_ref, *rest):
    out_ref = rest[-1]
    half = h_ref.shape[1] // 2
    h = h_ref[...]
    hb = h.astype(BF16)
    sg = jnp.dot(hb, sg_ref[...], preferred_element_type=F32)
    su = jnp.dot(hb, su_ref[...], preferred_element_type=F32)
    shared = jnp.dot((sg * jax.nn.sigmoid(sg) * su).astype(BF16), sd_ref[...], preferred_element_type=F32)
    z = DEEPNORM_ALPHA * h + shared
    z_hi, z_lo = z[:, :half], z[:, half:]
    for k in range(TOP_K):
        y_hi, y_lo = _unpack_bf16_pairs(y_ref[k])
        gate = gate_ref[:, k:k + 1]
        z_hi = z_hi + gate * y_hi
        z_lo = z_lo + gate * y_lo
    out_ref[...] = _layer_norm_rows(jnp.concatenate([z_hi, z_lo], axis=1), g_ref[...], b_ref[...])


def _combine_streamed(y_part, gate_tk, h2d, sh_gate, sh_up, sh_down, ln_g, ln_b, part, partial_out):
    t, d = h2d.shape
    tt = COMBINE_ROWS
    steps = y_part.shape[1] // tt
    first = part * steps
    full = lambda a: pl.BlockSpec(a.shape, lambda i: (0, 0))
    in_specs = [pl.BlockSpec((TOP_K, tt, d // 2), lambda i: (0, i, 0)),
                pl.BlockSpec((tt, TOP_K), lambda i: (first + i, 0)),
                pl.BlockSpec((tt, d), lambda i: (first + i, 0)),
                full(sh_gate), full(sh_up), full(sh_down), full(ln_g), full(ln_b)]
    args = [y_part, gate_tk, h2d, sh_gate, sh_up, sh_down, ln_g, ln_b]
    aliases = {}
    if partial_out is not None:
        in_specs.append(pl.BlockSpec(memory_space=pl.ANY))
        args.append(partial_out)
        aliases = {len(args) - 1: 0}
    return pl.pallas_call(
        _combine_streamed_body,
        grid=(steps,),
        in_specs=in_specs,
        out_specs=pl.BlockSpec((tt, d), lambda i: (first + i, 0)),
        out_shape=jax.ShapeDtypeStruct((t, d), F32),
        input_output_aliases=aliases,
        compiler_params=_cparams("arbitrary"),
        name="moe_combine_streamed_ln",
    )(*args)


def _mixer_sublayer(h2d, nb, s, w_in_p, w_out_p, a_biases, c_bias, lower_bound, norm_g_pad, sinks, w_sum, ln_g, ln_b):
    a_qkv, a_residue, b_all, c_qkv = _in_proj(h2d, w_in_p, nb, s)
    a_by_dilation = {1: a_qkv.reshape(nb, 1, s, A_COLS), **dict(zip(IN_PROJ_DILATIONS, a_residue))}
    o_list, l_list = [], []
    for (window, r), bias in zip(A_PATTERNS, a_biases):
        o, lse = _band_attn(a_by_dilation[r], bias, width=A_WIDTH, max_dist=window // r, want_lse=True)
        o_list.append(o)
        l_list.append(lse)
    oc = _band_attn(c_qkv.reshape(nb, 1, s, C_COLS), c_bias, width=C_WIDTH,
                    max_dist=C_WINDOW - 1, sinks=sinks, want_lse=False).reshape(nb * s, C_WIDTH)
    ob = _hgrn(b_all.reshape(nb, s, B_COLS), lower_bound, norm_g_pad, w_sum).reshape(nb * s, B_PAD_WIDTH)
    return _mix_out(o_list, l_list, ob, oc, h2d, w_out_p, ln_g, ln_b)


def _moe_sublayer(h2d, layer, router_w, router_bias, w_gate, w_up, w_down, sh_gate, sh_up, sh_down, ln_g, ln_b):
    t, d = h2d.shape
    bm = EXPERT_BLOCK
    e_t, rank_t, gate_t, counts, h_packed = _router(h2d, router_w.T.astype(F32),
                                                    router_bias.astype(F32).reshape(N_EXPERTS, 1))

    counts = counts.reshape(N_EXPERTS)
    padded = (counts + bm - 1) // bm * bm
    pad_end = jnp.cumsum(padded).astype(I32)
    offsets = pad_end - padded
    n_blocks = -(-(t * TOP_K + N_EXPERTS * (bm - 1)) // bm)

    dest_t = _dest_rows(e_t, rank_t, offsets)
    xs = _sc_scatter_rows(h_packed, dest_t, n_blocks * bm)
    ys = _experts(xs, offsets, (padded // bm).astype(I32), counts, jnp.full((1,), layer, I32), w_gate, w_up, w_down)
    tp = t // COMBINE_PARTS
    gate_tk = gate_t.T
    shared_w = (sh_gate.astype(BF16), sh_up.astype(BF16), sh_down.astype(BF16))
    out = None
    for part in range(COMBINE_PARTS):
        idx = dest_t[:, part * tp:(part + 1) * tp].reshape(TOP_K * tp)
        y_part = _sc_gather_rows(ys, idx).reshape(TOP_K, tp, d // 2)
        out = _combine_streamed(y_part, gate_tk, h2d, *shared_w, ln_g, ln_b, part, out)
    return out


def kernel(x, w_in, w_out, rel_bias_table, lower_bound_logits, hgrn_norm_g, attn_sinks, ln1_g, ln1_b, router_w, router_bias, expert_w_gate, expert_w_up, expert_w_down, shared_w_gate, shared_w_up, shared_w_down, ln2_g, ln2_b):
    nb, s, d = x.shape
    depth = w_in.shape[0]
    lb_probs = jax.nn.softmax(lower_bound_logits.astype(F32), axis=0)
    lower_bounds = jnp.cumsum(lb_probs, axis=0) - lb_probs[0]
    rel_table = rel_bias_table.astype(F32)
    a_biases = [_band_bias(rel_table, r, 0, A_HEADS) for _, r in A_PATTERNS]
    c_bias = _band_bias(rel_table, 1, A_HEADS, A_HEADS + C_HEADS)
    w_sum = jnp.asarray(_hgrn_sum_matrix(), dtype=BF16)
    row = lambda v: v.astype(F32).reshape(1, -1)

    h = x.astype(F32).reshape(nb * s, d)
    for l in range(depth):
        h = _mixer_sublayer(h, nb, s, _prep_w_in(w_in[l]), _prep_w_out(w_out[l].astype(F32)), a_biases, c_bias,
                            lower_bounds[l].reshape(1, B_KEY_WIDTH), _pad_heads_vec(hgrn_norm_g[l].astype(F32)),
                            attn_sinks[l].astype(F32), w_sum, row(ln1_g[l]), row(ln1_b[l]))
        h = _moe_sublayer(h, l, router_w[l], router_bias[l], expert_w_gate, expert_w_up, expert_w_down,
                          shared_w_gate[l], shared_w_up[l], shared_w_down[l], row(ln2_g[l]), row(ln2_b[l]))
    return h.reshape(nb, s, d).astype(x.dtype)
```

```python
import functools
import math

import numpy as np
import jax
import jax.numpy as jnp
from jax import lax
from jax.experimental import pallas as pl
from jax.experimental.pallas import tpu as pltpu
from jax.experimental.pallas import tpu_sc as plsc

F32 = jnp.float32
BF16 = jnp.bfloat16
I32 = jnp.int32
U32 = jnp.uint32

LANES = 128
SUBLANES = 8
VMEM_LIMIT = 56 * 1024 * 1024

D_MODEL = 1024
DEPTH = 2
HEAD_DIM = 64
BAND = 128
BAND_Q_TILE = 1024
MASK_VALUE = -1e30

A_HEADS = 6
A_PATTERNS = ((128, 1), (512, 4), (2048, 16))
IN_PROJ_DILATIONS = tuple(r for _, r in A_PATTERNS if r > 1)
IN_PROJ_ROWS = 512
MIX_OUT_ROWS = 512
B_HEADS = 4
B_KEY_DIM = 128
B_VAL_DIM = 96
B_VAL_PAD = 128
HG_CHUNK = 64
HG_STEP_ROWS = 512
C_HEADS = 4
C_KV_HEADS = 2
C_WINDOW = 128

A_WIDTH = A_HEADS * HEAD_DIM
B_KEY_WIDTH = B_HEADS * B_KEY_DIM
B_WIDTH = B_HEADS * B_VAL_DIM
B_PAD_WIDTH = B_HEADS * B_VAL_PAD
C_WIDTH = C_HEADS * HEAD_DIM
C_KV_WIDTH = C_KV_HEADS * HEAD_DIM
IN_SPLITS = (A_WIDTH, A_WIDTH, A_WIDTH, B_KEY_WIDTH, B_KEY_WIDTH, B_WIDTH, B_WIDTH, C_WIDTH, C_KV_WIDTH, C_KV_WIDTH)

A_COLS = 3 * A_WIDTH
B_COLS = 4 * B_KEY_WIDTH
C_COLS = 3 * C_WIDTH
MIX_PAD_WIDTH = A_WIDTH + B_PAD_WIDTH + C_WIDTH

REL_BUCKETS = 32
REL_MAX_DIST = 2048

N_EXPERTS = 256
TOP_K = 8
N_GROUPS = 8
TOPK_GROUPS = 4
EXPERT_FF = 256
SHARED_FF = 256
ROUTED_SCALE = 2.5
ROUTER_TOKENS = 512
DEST_TOKENS = 2048
EXPERT_BLOCK = 512
EXPERT_SLOTS = 4
COMBINE_PARTS = 2
COMBINE_ROWS = 512

DEEPNORM_ALPHA = (2 * DEPTH) ** 0.25
LN_EPS = 1e-5


def _cparams(*sem):
    return pltpu.CompilerParams(dimension_semantics=sem, vmem_limit_bytes=VMEM_LIMIT)


def _prep_w_in(w_in):
    d = w_in.shape[0]
    split_at = [int(i) for i in np.cumsum(IN_SPLITS)[:-1]]
    aq, ak, av, bq, bf, bi, bg, cq, ck, cv = jnp.split(w_in, split_at, axis=-1)
    pad_v = lambda w: jnp.pad(w.reshape(d, B_HEADS, B_VAL_DIM), ((0, 0), (0, 0), (0, B_VAL_PAD - B_VAL_DIM))).reshape(d, B_PAD_WIDTH)
    rep = lambda w: jnp.repeat(w.reshape(d, C_KV_HEADS, HEAD_DIM), C_HEADS // C_KV_HEADS, axis=1).reshape(d, C_WIDTH)
    scale = HEAD_DIM ** -0.5
    cols = [aq * scale, ak, av, bq, bf, pad_v(bi), pad_v(bg), cq * scale, rep(ck), rep(cv)]
    return jnp.concatenate(cols, axis=-1).astype(BF16)


def _prep_w_out(w_out):
    d = w_out.shape[1]
    wa = w_out[:A_WIDTH]
    wb = w_out[A_WIDTH:A_WIDTH + B_WIDTH].reshape(B_HEADS, B_VAL_DIM, d)
    wb = jnp.pad(wb, ((0, 0), (0, B_VAL_PAD - B_VAL_DIM), (0, 0))).reshape(B_PAD_WIDTH, d)
    wc = w_out[A_WIDTH + B_WIDTH:]
    return jnp.concatenate([wa, wb, wc], axis=0).astype(BF16)


def _pad_heads_vec(v):
    return jnp.pad(v.reshape(B_HEADS, B_VAL_DIM), ((0, 0), (0, B_VAL_PAD - B_VAL_DIM))).reshape(1, B_PAD_WIDTH)


def _rel_bucket(dist):
    max_exact = REL_BUCKETS // 2
    d = jnp.maximum(dist, 0)
    log_ratio = jnp.log(jnp.maximum(d, max_exact).astype(F32) / max_exact) / math.log(REL_MAX_DIST / max_exact)
    large = jnp.minimum(max_exact + (log_ratio * (REL_BUCKETS - max_exact)).astype(I32), REL_BUCKETS - 1)
    return jnp.where(d < max_exact, d, large)


def _band_bias(rel_table, r, head_lo, head_hi):
    dist = jnp.arange(BAND)[:, None] + BAND - jnp.arange(2 * BAND)[None, :]
    onehot = jax.nn.one_hot(_rel_bucket(dist * r), REL_BUCKETS, dtype=F32)
    return jnp.einsum("qkb,bh->hqk", onehot, rel_table[:, head_lo:head_hi], precision=lax.Precision.HIGHEST)


def _hgrn_sum_matrix():
    return np.tril(np.ones((HG_CHUNK, HG_CHUNK), np.float32))


HG_LEVELS = int(math.log2(HG_CHUNK))


def _in_proj_body(x_ref, w_ref, *rest):
    n_res = len(IN_PROJ_DILATIONS)
    a_ref, res_refs, (b_ref, c_ref, slabs) = rest[0], rest[1:1 + n_res], rest[1 + n_res:]
    tm = x_ref.shape[0]
    xb = x_ref[...].astype(BF16)
    a = jnp.dot(xb, w_ref[:, :A_COLS], preferred_element_type=F32)
    a_ref[...] = a.astype(BF16)
    for j in range(A_COLS // LANES):
        slabs[0, j] = a[:, j * LANES:(j + 1) * LANES]
    prev_r = 1
    for level, (r, ref) in enumerate(zip(IN_PROJ_DILATIONS, res_refs)):
        step, n, n_prev = r // prev_r, tm // r, tm // prev_r
        src, dst = slabs.at[level % 2], slabs.at[(level + 1) % 2]
        for p in range(r):
            p_prev, q = p % prev_r, p // prev_r
            for j in range(A_COLS // LANES):
                piece = src[j, pl.ds(p_prev * n_prev + q, n, stride=step), :]
                if level + 1 < len(IN_PROJ_DILATIONS):
                    dst[j, p * n:(p + 1) * n, :] = piece
                ref[0, p, :, j * LANES:(j + 1) * LANES] = piece.astype(BF16)
        prev_r = r
    for j in range(B_COLS // B_KEY_WIDTH):
        lo = A_COLS + j * B_KEY_WIDTH
        b_ref[:, j * B_KEY_WIDTH:(j + 1) * B_KEY_WIDTH] = jnp.dot(
            xb, w_ref[:, lo:lo + B_KEY_WIDTH], preferred_element_type=F32)
    c_ref[...] = jnp.dot(xb, w_ref[:, A_COLS + B_COLS:], preferred_element_type=F32).astype(BF16)


def _in_proj(x2d, w_p, nb, s):
    t, d = x2d.shape
    tm = IN_PROJ_ROWS
    n = w_p.shape[1]
    tiles = s // tm
    res_specs = [pl.BlockSpec((1, r, tm // r, A_COLS), lambda i: (i // tiles, 0, i % tiles, 0))
                 for r in IN_PROJ_DILATIONS]
    res_shapes = [jax.ShapeDtypeStruct((nb, r, s // r, A_COLS), BF16) for r in IN_PROJ_DILATIONS]
    outs = pl.pallas_call(
        _in_proj_body,
        grid=(t // tm,),
        in_specs=[pl.BlockSpec((tm, d), lambda i: (i, 0)),
                  pl.BlockSpec((d, n), lambda i: (0, 0))],
        out_specs=[pl.BlockSpec((tm, A_COLS), lambda i: (i, 0))] + res_specs + [
                   pl.BlockSpec((tm, B_COLS), lambda i: (i, 0)),
                   pl.BlockSpec((tm, C_COLS), lambda i: (i, 0))],
        out_shape=[jax.ShapeDtypeStruct((t, A_COLS), BF16)] + res_shapes + [
                   jax.ShapeDtypeStruct((t, B_COLS), F32),
                   jax.ShapeDtypeStruct((t, C_COLS), BF16)],
        scratch_shapes=[pltpu.VMEM((2, A_COLS // LANES, tm, LANES), F32)],
        compiler_params=_cparams("arbitrary"),
        name="in_proj",
    )(x2d, w_p)
    return outs[0], list(outs[1:-2]), outs[-2], outs[-1]


def _band_attn_body(*refs, width, max_dist, has_sink, want_lse):
    q_ref, kp_ref, kc_ref, vp_ref, vc_ref, bias_ref = refs[:6]
    rest = refs[6:]
    if has_sink:
        sink_ref, rest = rest[0], rest[1:]
    o_ref = rest[0]
    lse_ref = rest[1] if want_lse else None

    first_tile = pl.program_id(2) == 0
    row = lax.broadcasted_iota(I32, (BAND, 2 * BAND), 0)
    col = lax.broadcasted_iota(I32, (BAND, 2 * BAND), 1)
    dist = row + BAND - col
    in_band = (dist >= 0) & (dist <= max_dist)
    first_mask = in_band & ((col >= BAND) | jnp.logical_not(first_tile))
    lane = lax.broadcasted_iota(I32, (BAND, LANES), 1)
    low_half = lane < HEAD_DIM

    for qb in range(q_ref.shape[0] // BAND):
        rows = slice(qb * BAND, (qb + 1) * BAND)
        mask = first_mask if qb == 0 else in_band
        lse_tile = jnp.zeros((BAND, LANES), F32)
        for tile in range(width // LANES):
            sl = slice(tile * LANES, (tile + 1) * LANES)
            q2 = q_ref[rows, sl]
            if qb == 0:
                k2 = jnp.concatenate([kp_ref[:, sl], kc_ref[:BAND, sl]], axis=0)
                v2 = jnp.concatenate([vp_ref[:, sl], vc_ref[:BAND, sl]], axis=0)
            else:
                k2 = kc_ref[(qb - 1) * BAND:(qb + 1) * BAND, sl]
                v2 = vc_ref[(qb - 1) * BAND:(qb + 1) * BAND, sl]
            outs, lses = [], []
            for half in range(2):
                h = 2 * tile + half
                qm = jnp.where(low_half if half == 0 else jnp.logical_not(low_half), q2, jnp.zeros_like(q2))
                s = lax.dot_general(qm, k2, (((1,), (1,)), ((), ())), preferred_element_type=F32)
                s = s + bias_ref[h]
                s = jnp.where(mask, s, MASK_VALUE)
                m = jnp.max(s, axis=-1, keepdims=True)
                if has_sink:
                    sink = sink_ref[h]
                    m = jnp.maximum(m, sink)
                p = jnp.exp(s - m)
                den = jnp.sum(p, axis=-1, keepdims=True)
                if has_sink:
                    den = den + jnp.exp(sink - m)
                pv = jnp.dot(p.astype(BF16), v2, preferred_element_type=F32)
                outs.append(pv / den)
                if want_lse:
                    lse_tile = jnp.where(lane == h, m + jnp.log(den), lse_tile)
            o_ref[rows, sl] = jnp.where(low_half, outs[0], outs[1]).astype(o_ref.dtype)
        if want_lse:
            lse_ref[rows, :] = lse_tile


def _band_attn(src, bias, *, width, max_dist, sinks=None, want_lse):
    nb, r, length, _ = src.shape
    qt = min(BAND_Q_TILE, length)
    bands = qt // BAND
    heads = width // HEAD_DIM
    has_sink = sinks is not None

    def cur(off):
        return pl.BlockSpec((None, None, qt, width), lambda b, p, i: (b, p, i, off))

    def prev(off):
        return pl.BlockSpec((None, None, BAND, width), lambda b, p, i: (b, p, jnp.maximum(i * bands - 1, 0), off))

    in_specs = [cur(0), prev(1), cur(1), prev(2), cur(2),
                pl.BlockSpec((heads, BAND, 2 * BAND), lambda b, p, i: (0, 0, 0))]
    args = [src, src, src, src, src, bias]
    if has_sink:
        in_specs.append(pl.BlockSpec(memory_space=pltpu.SMEM))
        args.append(sinks)
    out_spec = pl.BlockSpec((None, None, qt, width), lambda b, p, i: (b, p, i, 0))
    out_sds = jax.ShapeDtypeStruct((nb, r, length, width), F32 if want_lse else BF16)
    lse_spec = pl.BlockSpec((None, None, qt, LANES), lambda b, p, i: (b, p, i, 0))
    lse_sds = jax.ShapeDtypeStruct((nb, r, length, LANES), F32)
    body = functools.partial(_band_attn_body, width=width, max_dist=max_dist, has_sink=has_sink, want_lse=want_lse)
    return pl.pallas_call(
        body,
        grid=(nb, r, length // qt),
        in_specs=in_specs,
        out_specs=[out_spec, lse_spec] if want_lse else out_spec,
        out_shape=[out_sds, lse_sds] if want_lse else out_sds,
        compiler_params=_cparams("arbitrary", "arbitrary", "arbitrary"),
        name="band_attn_r%d_w%d" % (r, width),
    )(*args)


def _hgrn_body(b_ref, lb_ref, ng_ref, w_ref, o_ref, state_ref):
    c = HG_CHUNK
    kd = B_KEY_DIM

    @pl.when(pl.program_id(1) == 0)
    def _():
        state_ref[...] = jnp.zeros_like(state_ref)

    trow = lax.broadcasted_iota(I32, (c, 1), 0)
    odd_row = (trow % 2) == 1
    low_sub = lax.broadcasted_iota(I32, (SUBLANES, 1), 0) < SUBLANES // 2
    ti = lax.broadcasted_iota(I32, (c, c), 0)
    si = lax.broadcasted_iota(I32, (c, c), 1)
    nt = (((1,), (1,)), ((), ()))
    tri = w_ref[...]

    in_level = [((ti // m) == (si // m)) & ((ti % m) >= (m // 2)) & ((si % m) < (m // 2))
                for m in (c >> lvl for lvl in range(HG_LEVELS))]

    states = [state_ref[h] for h in range(B_HEADS)]
    for chunk, h in [(ci, hi) for ci in range(b_ref.shape[1] // c) for hi in range(B_HEADS)]:
        rows = slice(chunk * c, (chunk + 1) * c)
        ks = slice(h * kd, (h + 1) * kd)
        q = b_ref[0, rows, ks]
        f = b_ref[0, rows, B_KEY_WIDTH + h * kd:B_KEY_WIDTH + (h + 1) * kd]
        inp = b_ref[0, rows, 2 * B_KEY_WIDTH + h * kd:2 * B_KEY_WIDTH + (h + 1) * kd]
        gate = b_ref[0, rows, 3 * B_KEY_WIDTH + h * kd:3 * B_KEY_WIDTH + (h + 1) * kd]
        lb = lb_ref[:, ks]

        a = jnp.exp(-jnp.abs(f))
        big = 1.0 / (1.0 + a)
        small = a * big
        pos = f >= 0.0
        forget = lb + (1.0 - lb) * jnp.where(pos, big, small)
        log_f = jnp.log(forget)
        key = (1.0 - lb) * jnp.where(pos, small, big)
        qs = q * jax.nn.sigmoid(q)

        g_hi = log_f.astype(BF16)
        g_lo = (log_f - g_hi.astype(F32)).astype(BF16)
        cum2 = jnp.dot(tri, jnp.concatenate([g_hi, g_lo], axis=1), preferred_element_type=F32)
        cum = cum2[:, :kd] + cum2[:, kd:]
        last = cum[c - 1:c]

        q_dec = (qs * jnp.exp(cum)).astype(BF16)
        k_dec = (key * jnp.exp(last - cum)).astype(BF16)
        inp_b = inp.astype(BF16)

        def level_decay(m):
            if m == 2:
                return jnp.where(odd_row, forget, 1.0)
            if m == SUBLANES // 2:
                row_bcast = lambda i: jnp.broadcast_to(cum[i:i + 1], (SUBLANES, kd))
                pieces = [jnp.where(low_sub, row_bcast(SUBLANES * j + m // 2 - 1), row_bcast(SUBLANES * j + m + m // 2 - 1))
                          for j in range(c // SUBLANES)]
            else:
                pieces = [jnp.broadcast_to(cum[b * m + m // 2 - 1:b * m + m // 2], (m, kd)) for b in range(c // m)]
            return jnp.exp(-jnp.abs(cum - jnp.concatenate(pieces, axis=0)))

        st = states[h]
        inter = lax.dot_general(q_dec, st.astype(BF16), nt, preferred_element_type=F32)

        scores = jnp.where(ti == si,
                           lax.dot_general(qs.astype(BF16), key.astype(BF16), nt, preferred_element_type=F32), 0.0)
        m = c
        for lvl in range(HG_LEVELS):
            el = level_decay(m)
            sl = lax.dot_general((qs * el).astype(BF16), (key * el).astype(BF16), nt, preferred_element_type=F32)
            scores = jnp.where(in_level[lvl], sl, scores)
            m //= 2
        intra = jnp.dot(scores.astype(BF16), inp_b, preferred_element_type=F32)

        new_st = st * jnp.exp(last) + lax.dot_general(
            inp_b, k_dec, (((0,), (0,)), ((), ())), preferred_element_type=F32)
        states[h] = new_st

        o = inter + intra
        ms = jnp.sum(o * o, axis=-1, keepdims=True) * (1.0 / B_VAL_DIM)
        o = o * lax.rsqrt(ms + 1e-6)
        o_ref[0, rows, ks] = (o * ng_ref[:, ks] * (gate * jax.nn.sigmoid(gate))).astype(o_ref.dtype)

    for h in range(B_HEADS):
        state_ref[h] = states[h]


def _hgrn(b_all, lower_bound, norm_g_pad, w_sum):
    nb, s, _ = b_all.shape
    c = HG_STEP_ROWS
    return pl.pallas_call(
        _hgrn_body,
        grid=(nb, s // c),
        in_specs=[pl.BlockSpec((1, c, B_COLS), lambda b, i: (b, i, 0)),
                  pl.BlockSpec((1, B_KEY_WIDTH), lambda b, i: (0, 0)),
                  pl.BlockSpec((1, B_PAD_WIDTH), lambda b, i: (0, 0)),
                  pl.BlockSpec(w_sum.shape, lambda b, i: (0, 0))],
        out_specs=pl.BlockSpec((1, c, B_PAD_WIDTH), lambda b, i: (b, i, 0)),
        out_shape=jax.ShapeDtypeStruct((nb, s, B_PAD_WIDTH), BF16),
        scratch_shapes=[pltpu.VMEM((B_HEADS, B_VAL_PAD, B_KEY_DIM), F32)],
        compiler_params=_cparams("arbitrary", "arbitrary"),
        name="hgrn2",
    )(b_all, lower_bound, norm_g_pad, w_sum)


def _layer_norm_rows(z, g, b):
    mu = jnp.mean(z, axis=-1, keepdims=True)
    zc = z - mu
    var = jnp.mean(zc * zc, axis=-1, keepdims=True)
    return zc * lax.rsqrt(var + LN_EPS) * g + b


def _mix_out_body(*refs):
    n_pat = len(A_PATTERNS)
    o_refs, l_refs = refs[:n_pat], refs[n_pat:2 * n_pat]
    ob, oc, x_ref, w_ref, g_ref, b_ref, spread_ref, out_ref = refs[2 * n_pat:2 * n_pat + 8]
    scratch = refs[2 * n_pat + 8:]
    tm = x_ref.shape[0]
    n_slabs = A_WIDTH // LANES

    def token_order(ref, r, scr):
        slabs = ref.shape[-1] // LANES
        if r == 1:
            return [ref[0, 0, :, j * LANES:(j + 1) * LANES] for j in range(slabs)]
        for p in range(r):
            for j in range(slabs):
                scr[j, pl.ds(p, tm // r, stride=r), :] = ref[0, p, :, j * LANES:(j + 1) * LANES]
        return [scr[j] for j in range(slabs)]

    scr_iter = iter(scratch)
    dil = [r for _, r in A_PATTERNS]
    o_slabs = [token_order(ref, r, None if r == 1 else next(scr_iter)) for ref, r in zip(o_refs, dil)]
    ls = [token_order(ref, r, None if r == 1 else next(scr_iter))[0] for ref, r in zip(l_refs, dil)]

    m = functools.reduce(jnp.maximum, ls)
    ws = [jnp.exp(l - m) for l in ls]
    inv = 1.0 / functools.reduce(lambda a, b: a + b, ws)
    spread = spread_ref[...]
    wide = []
    for w in ws:
        wn = w * inv
        hi = wn.astype(BF16)
        lo = (wn - hi.astype(F32)).astype(BF16)
        wide.append(jnp.dot(hi, spread, preferred_element_type=F32) + jnp.dot(lo, spread, preferred_element_type=F32))
    merged = [functools.reduce(lambda a, b: a + b,
                               [w[:, j * LANES:(j + 1) * LANES] * o[j] for w, o in zip(wide, o_slabs)])
              for j in range(n_slabs)]
    cat = jnp.concatenate([m.astype(BF16) for m in merged] + [ob[...], oc[...]], axis=1)
    y = jnp.dot(cat, w_ref[...], preferred_element_type=F32)
    z = DEEPNORM_ALPHA * x_ref[...] + y
    out_ref[...] = _layer_norm_rows(z, g_ref[...], b_ref[...])


def _mix_out(o_list, l_list, ob, oc, x2d, w_out_p, ln_g, ln_b):
    t, d = x2d.shape
    tm = MIX_OUT_ROWS
    tiles = o_list[0].shape[1] * o_list[0].shape[2] // tm
    row = lambda w: pl.BlockSpec((tm, w), lambda i: (i, 0))
    full = lambda a: pl.BlockSpec(a.shape, lambda i: (0, 0))
    res = lambda a: pl.BlockSpec((1, a.shape[1], tm // a.shape[1], a.shape[3]), lambda i: (i // tiles, 0, i % tiles, 0))
    spread = jnp.asarray(np.arange(LANES)[:, None] == np.arange(A_WIDTH)[None, :] // HEAD_DIM, dtype=BF16)
    dilated = [a for a in o_list + l_list if a.shape[1] > 1]
    return pl.pallas_call(
        _mix_out_body,
        grid=(t // tm,),
        in_specs=[res(a) for a in o_list + l_list] + [row(B_PAD_WIDTH), row(C_WIDTH), row(d), full(w_out_p),
                                                      full(ln_g), full(ln_b), full(spread)],
        out_specs=row(d),
        out_shape=jax.ShapeDtypeStruct((t, d), F32),
        scratch_shapes=[pltpu.VMEM((a.shape[3] // LANES, tm, LANES), F32) for a in dilated],
        compiler_params=_cparams("arbitrary"),
        name="mix_out_ln",
    )(*o_list, *l_list, ob, oc, x2d, w_out_p, ln_g, ln_b, spread)


def _router_body(h_ref, rw_ref, bias_ref, tri_ref, e_ref, rank_ref, gate_ref, cnt_ref, hp_ref, carry_ref):
    tn = h_ref.shape[0]
    per_group = N_EXPERTS // N_GROUPS
    neg_inf = -jnp.inf

    @pl.when(pl.program_id(0) == 0)
    def _():
        carry_ref[...] = jnp.zeros_like(carry_ref)

    hp_ref[...] = _pack_bf16_pairs(h_ref[...])

    h = h_ref[...]
    h_hi = h.astype(BF16)
    h_lo = (h - h_hi.astype(F32)).astype(BF16)
    nt = (((1,), (1,)), ((), ()))
    w_hi, w_lo = rw_ref[0], rw_ref[1]
    logits = (lax.dot_general(w_hi, h_hi, nt, preferred_element_type=F32)
              + lax.dot_general(w_hi, h_lo, nt, preferred_element_type=F32)
              + lax.dot_general(w_lo, h_hi, nt, preferred_element_type=F32))
    scores = jax.nn.sigmoid(logits)
    choice = scores + bias_ref[...]

    def first_max(vals, idx, sentinel):
        top = jnp.max(vals, axis=0, keepdims=True)
        return top, jnp.min(jnp.where(vals == top, idx, sentinel), axis=0, keepdims=True)

    li = lax.broadcasted_iota(I32, (per_group, tn), 0).astype(F32)
    group_rows = []
    for g in range(N_GROUPS):
        cg = choice[g * per_group:(g + 1) * per_group]
        m1, first = first_max(cg, li, float(per_group))
        m2 = jnp.max(jnp.where(li == first, neg_inf, cg), axis=0, keepdims=True)
        group_rows.append(m1 + m2)
    group_score = jnp.concatenate(group_rows, axis=0)

    gi = lax.broadcasted_iota(I32, (N_GROUPS, tn), 0).astype(F32)
    group_ok = jnp.zeros((N_GROUPS, tn), F32)
    cur = group_score
    for _ in range(TOPK_GROUPS):
        _, first = first_max(cur, gi, float(N_GROUPS))
        pick = gi == first
        group_ok = jnp.where(pick, 1.0, group_ok)
        cur = jnp.where(pick, neg_inf, cur)

    cur = jnp.concatenate(
        [jnp.where(group_ok[g:g + 1] > 0.0, choice[g * per_group:(g + 1) * per_group], MASK_VALUE)
         for g in range(N_GROUPS)], axis=0)
    ei = lax.broadcasted_iota(I32, (N_EXPERTS, tn), 0).astype(F32)
    chosen = jnp.zeros((N_EXPERTS, tn), F32)
    picks, gates = [], []
    for _ in range(TOP_K):
        _, idx = first_max(cur, ei, float(N_EXPERTS))
        pick = ei == idx
        picks.append(idx)
        gates.append(jnp.sum(jnp.where(pick, scores, 0.0), axis=0, keepdims=True))
        chosen = jnp.where(pick, 1.0, chosen)
        cur = jnp.where(pick, neg_inf, cur)

    gate = jnp.concatenate(gates, axis=0)
    gate_ref[...] = gate / jnp.sum(gate, axis=0, keepdims=True) * ROUTED_SCALE
    e_ref[...] = jnp.concatenate(picks, axis=0).astype(I32)

    before = jnp.dot(chosen.astype(BF16), tri_ref[...], preferred_element_type=F32) + carry_ref[...]
    ranks = [jnp.sum(jnp.where(ei == idx, before, 0.0), axis=0, keepdims=True) for idx in picks]
    rank_ref[...] = jnp.concatenate(ranks, axis=0).astype(I32)
    carry = carry_ref[...] + jnp.sum(chosen, axis=1, keepdims=True)
    carry_ref[...] = carry
    cnt_ref[...] = carry.astype(I32)


def _router(h2d, rw_t, bias_col):
    t, d = h2d.shape
    tn = ROUTER_TOKENS
    tri = jnp.asarray(np.triu(np.ones((tn, tn), np.float32), k=1), dtype=BF16)
    rw_hi = rw_t.astype(BF16)
    rw_split = jnp.stack([rw_hi, (rw_t - rw_hi.astype(F32)).astype(BF16)])
    tok = lambda: pl.BlockSpec((TOP_K, tn), lambda i: (0, i))
    return pl.pallas_call(
        _router_body,
        grid=(t // tn,),
        in_specs=[pl.BlockSpec((tn, d), lambda i: (i, 0)),
                  pl.BlockSpec((2, N_EXPERTS, d), lambda i: (0, 0, 0)),
                  pl.BlockSpec((N_EXPERTS, 1), lambda i: (0, 0)),
                  pl.BlockSpec((tn, tn), lambda i: (0, 0))],
        out_specs=[tok(), tok(), tok(), pl.BlockSpec((N_EXPERTS, 1), lambda i: (0, 0)),
                   pl.BlockSpec((tn, d // 2), lambda i: (i, 0))],
        out_shape=[jax.ShapeDtypeStruct((TOP_K, t), I32),
                   jax.ShapeDtypeStruct((TOP_K, t), I32),
                   jax.ShapeDtypeStruct((TOP_K, t), F32),
                   jax.ShapeDtypeStruct((N_EXPERTS, 1), I32),
                   jax.ShapeDtypeStruct((t, d // 2), U32)],
        scratch_shapes=[pltpu.VMEM((N_EXPERTS, 1), F32)],
        compiler_params=_cparams("arbitrary"),
        name="moe_router",
    )(h2d, rw_split, bias_col, tri)


def _pack_bf16_pairs(x):
    w = x.shape[1] // 2
    hi = lax.bitcast_convert_type(x[:, :w].astype(BF16).astype(F32), U32)
    lo = lax.bitcast_convert_type(x[:, w:].astype(BF16).astype(F32), U32)
    return hi | (lo >> 16)


def _unpack_bf16_pairs(p):
    hi = lax.bitcast_convert_type(p & jnp.uint32(0xFFFF0000), F32)
    lo = lax.bitcast_convert_type(p << 16, F32)
    return hi, lo


def _dest_body(e_ref, rank_ref, offs_ref, dest_ref):
    tn = e_ref.shape[1]
    ei = lax.broadcasted_iota(I32, (N_EXPERTS, tn), 0)
    offs = offs_ref[...]
    rows = [jnp.sum(jnp.where(ei == e_ref[k:k + 1, :], offs, 0.0), axis=0, keepdims=True) for k in range(TOP_K)]
    dest_ref[...] = jnp.concatenate(rows, axis=0).astype(I32) + rank_ref[...]


def _dest_rows(e_t, rank_t, offsets):
    t = e_t.shape[1]
    tn = DEST_TOKENS
    tok = pl.BlockSpec((TOP_K, tn), lambda i: (0, i))
    return pl.pallas_call(
        _dest_body,
        grid=(t // tn,),
        in_specs=[tok, tok, pl.BlockSpec((N_EXPERTS, 1), lambda i: (0, 0))],
        out_specs=tok,
        out_shape=jax.ShapeDtypeStruct((TOP_K, t), I32),
        compiler_params=_cparams("arbitrary"),
        name="moe_dest",
    )(e_t, rank_t, offsets.astype(F32).reshape(N_EXPERTS, 1))


def _expert_body(offs_ref, nblk_ref, cnt_ref, layer_ref, xs_hbm, wg_ref, wu_ref, wd_ref, ys_hbm,
                 xbuf, ybuf, wg_b, wu_b, wd_b, sem_in, sem_out):
    e = pl.program_id(0)
    slots, bm = xbuf.shape[:2]
    ahead = slots - 1
    n = nblk_ref[e]
    first = offs_ref[e] // bm
    total = (offs_ref[N_EXPERTS - 1] // bm) + nblk_ref[N_EXPERTS - 1]

    def rows(g):
        return pl.ds(pl.multiple_of(g * bm, bm), bm)

    def x_copy(g, slot):
        return pltpu.make_async_copy(xs_hbm.at[rows(g)], xbuf.at[slot], sem_in.at[slot])

    def y_copy(g, slot):
        return pltpu.make_async_copy(ybuf.at[slot], ys_hbm.at[rows(g)], sem_out.at[slot])

    @pl.when(e == 0)
    def _():
        for g0 in range(ahead):
            @pl.when(g0 < total)
            def _():
                x_copy(g0, g0).start()

    wg_b[...] = wg_ref[0, 0].astype(BF16)
    wu_b[...] = wu_ref[0, 0].astype(BF16)
    wd_b[...] = wd_ref[0, 0].astype(BF16)

    def block(j, carry):
        g = first + j
        slot = g % slots
        x_copy(g, slot).wait()

        @pl.when(g + ahead < total)
        def _():
            x_copy(g + ahead, (g + ahead) % slots).start()

        @pl.when(g >= slots)
        def _():
            y_copy(g - slots, slot).wait()

        live = lax.broadcasted_iota(I32, (bm, 1), 0) < (cnt_ref[e] - j * bm)
        hi, lo = _unpack_bf16_pairs(jnp.where(live, xbuf[slot], jnp.uint32(0)))
        xb = jnp.concatenate([hi.astype(BF16), lo.astype(BF16)], axis=1)
        gate = jnp.dot(xb, wg_b[...], preferred_element_type=F32)
        up = jnp.dot(xb, wu_b[...], preferred_element_type=F32)
        hidden = (gate * jax.nn.sigmoid(gate) * up).astype(BF16)
        ybuf[slot] = _pack_bf16_pairs(jnp.dot(hidden, wd_b[...], preferred_element_type=F32))
        y_copy(g, slot).start()
        return carry

    lax.fori_loop(0, n, block, 0)

    @pl.when(e == N_EXPERTS - 1)
    def _():
        for back in range(1, slots + 1):
            @pl.when(total >= back)
            def _():
                y_copy(total - back, (total - back) % slots).wait()


def _experts(xs, offsets, n_blk, counts, layer, w_gate, w_up, w_down):
    n_rows, dp = xs.shape
    bm = EXPERT_BLOCK
    d, ff = w_gate.shape[-2:]
    w_spec = lambda a, b: pl.BlockSpec((1, 1, a, b), lambda e, of, nb, ct, ly: (ly[0], e, 0, 0))
    grid_spec = pltpu.PrefetchScalarGridSpec(
        num_scalar_prefetch=4,
        grid=(N_EXPERTS,),
        in_specs=[pl.BlockSpec(memory_space=pl.ANY), w_spec(d, ff), w_spec(d, ff), w_spec(ff, d)],
        out_specs=pl.BlockSpec(memory_space=pl.ANY),
        scratch_shapes=[pltpu.VMEM((EXPERT_SLOTS, bm, dp), U32), pltpu.VMEM((EXPERT_SLOTS, bm, dp), U32),
                        pltpu.VMEM((d, ff), BF16), pltpu.VMEM((d, ff), BF16), pltpu.VMEM((ff, d), BF16),
                        pltpu.SemaphoreType.DMA((EXPERT_SLOTS,)), pltpu.SemaphoreType.DMA((EXPERT_SLOTS,))],
    )
    return pl.pallas_call(
        _expert_body,
        grid_spec=grid_spec,
        out_shape=jax.ShapeDtypeStruct((n_rows, dp), U32),
        compiler_params=_cparams("arbitrary"),
        name="moe_experts",
    )(offsets, n_blk, counts, layer, xs, w_gate, w_up, w_down)


SC_GATHER_ROWS = 64


def _sc_gather_rows(table, idx):
    info = plsc.get_sparse_core_info()
    nc, ns = info.num_cores, info.num_subcores
    workers = nc * ns
    n = idx.shape[0]
    w = table.shape[1]
    ch = SC_GATHER_ROWS
    per_worker = n // workers
    steps = per_worker // ch
    assert per_worker * workers == n and steps * ch == per_worker and steps % 2 == 0
    mesh = plsc.VectorSubcoreMesh(core_axis_name="c", subcore_axis_name="s")

    @functools.partial(
        pl.kernel, mesh=mesh,
        out_type=jax.ShapeDtypeStruct((n, w), table.dtype),
        scratch_types=[pltpu.VMEM((steps, ch), I32), pltpu.VMEM((2, ch, w), table.dtype),
                       pltpu.SemaphoreType.DMA((2,)), pltpu.SemaphoreType.DMA((2,))],
    )
    def gather_kernel(table_hbm, idx_hbm, out_hbm, idx_v, rows_v, gsem, wsem):
        wid = lax.axis_index("s") * nc + lax.axis_index("c")
        base = wid * per_worker
        pltpu.sync_copy(idx_hbm.at[wid], idx_v)

        def gather(i, slot):
            return pltpu.make_async_copy(table_hbm.at[idx_v.at[i]], rows_v.at[slot], gsem.at[slot])

        def write(i, slot):
            off = pl.multiple_of(base + i * ch, ch)
            return pltpu.make_async_copy(rows_v.at[slot], out_hbm.at[pl.ds(off, ch)], wsem.at[slot])

        gather(0, 0).start()

        @pl.loop(0, steps, step=2)
        def _(i):
            gather(i + 1, 1).start()
            gather(i, 0).wait()
            write(i, 0).start()
            write(i, 0).wait()

            @pl.when(i + 2 < steps)
            def _():
                gather(i + 2, 0).start()

            gather(i + 1, 1).wait()
            write(i + 1, 1).start()
            write(i + 1, 1).wait()

    return gather_kernel(table, idx.reshape(workers, steps, ch))


def _sc_scatter_rows(rows, idx, n_out):
    info = plsc.get_sparse_core_info()
    nc, ns = info.num_cores, info.num_subcores
    workers = nc * ns
    kk, t = idx.shape
    w = rows.shape[1]
    ch = SC_GATHER_ROWS
    per_worker = t // workers
    steps = per_worker // ch
    assert per_worker * workers == t and steps * ch == per_worker and steps % 2 == 0
    idx_w = idx.reshape(kk, workers, steps, ch).transpose(1, 0, 2, 3).reshape(workers, kk * steps, ch)
    mesh = plsc.VectorSubcoreMesh(core_axis_name="c", subcore_axis_name="s")

    @functools.partial(
        pl.kernel, mesh=mesh,
        out_type=jax.ShapeDtypeStruct((n_out, w), rows.dtype),
        scratch_types=[pltpu.VMEM((kk * steps, ch), I32), pltpu.VMEM((2, ch, w), rows.dtype),
                       pltpu.SemaphoreType.DMA((2,)), pltpu.SemaphoreType.DMA((2,))],
    )
    def scatter_kernel(rows_hbm, idx_hbm, out_hbm, idx_v, rows_v, rsem, ssem):
        wid = lax.axis_index("s") * nc + lax.axis_index("c")
        base = wid * per_worker
        pltpu.sync_copy(idx_hbm.at[wid], idx_v)

        def read(i, slot):
            off = pl.multiple_of(base + i * ch, ch)
            return pltpu.make_async_copy(rows_hbm.at[pl.ds(off, ch)], rows_v.at[slot], rsem.at[slot])

        def scatter(i, k, slot):
            return pltpu.make_async_copy(rows_v.at[slot], out_hbm.at[idx_v.at[k * steps + i]], ssem.at[slot])

        def scatter_all(i, slot):
            for k in range(kk):
                scatter(i, k, slot).start()
            for k in range(kk):
                scatter(i, k, slot).wait()

        read(0, 0).start()

        @pl.loop(0, steps, step=2)
        def _(i):
            read(i + 1, 1).start()
            read(i, 0).wait()
            scatter_all(i, 0)

            @pl.when(i + 2 < steps)
            def _():
                read(i + 2, 0).start()

            read(i + 1, 1).wait()
            scatter_all(i + 1, 1)

    return scatter_kernel(rows, idx_w)


def _combine_streamed_body(y_ref, gate_ref, h_ref, sg_ref, su_ref, sd_ref, g_ref, b_ref, *rest):
    out_ref = rest[-1]
    half = h_ref.shape[1] // 2
    h = h_ref[...]
    hb = h.astype(BF16)
    sg = jnp.dot(hb, sg_ref[...], preferred_element_type=F32)
    su = jnp.dot(hb, su_ref[...], preferred_element_type=F32)
    shared = jnp.dot((sg * jax.nn.sigmoid(sg) * su).astype(BF16), sd_ref[...], preferred_element_type=F32)
    z = DEEPNORM_ALPHA * h + shared
    z_hi, z_lo = z[:, :half], z[:, half:]
    for k in range(TOP_K):
        y_hi, y_lo = _unpack_bf16_pairs(y_ref[k])
        gate = gate_ref[:, k:k + 1]
        z_hi = z_hi + gate * y_hi
        z_lo = z_lo + gate * y_lo
    out_ref[...] = _layer_norm_rows(jnp.concatenate([z_hi, z_lo], axis=1), g_ref[...], b_ref[...])


def _combine_streamed(y_part, gate_tk, h2d, sh_gate, sh_up, sh_down, ln_g, ln_b, part, partial_out):
    t, d = h2d.shape
    tt = COMBINE_ROWS
    steps = y_part.shape[1] // tt
    first = part * steps
    full = lambda a: pl.BlockSpec(a.shape, lambda i: (0, 0))
    in_specs = [pl.BlockSpec((TOP_K, tt, d // 2), lambda i: (0, i, 0)),
                pl.BlockSpec((tt, TOP_K), lambda i: (first + i, 0)),
                pl.BlockSpec((tt, d), lambda i: (first + i, 0)),
                full(sh_gate), full(sh_up), full(sh_down), full(ln_g), full(ln_b)]
    args = [y_part, gate_tk, h2d, sh_gate, sh_up, sh_down, ln_g, ln_b]
    aliases = {}
    if partial_out is not None:
        in_specs.append(pl.BlockSpec(memory_space=pl.ANY))
        args.append(partial_out)
        aliases = {len(args) - 1: 0}
    return pl.pallas_call(
        _combine_streamed_body,
        grid=(steps,),
        in_specs=in_specs,
        out_specs=pl.BlockSpec((tt, d), lambda i: (first + i, 0)),
        out_shape=jax.ShapeDtypeStruct((t, d), F32),
        input_output_aliases=aliases,
        compiler_params=_cparams("arbitrary"),
        name="moe_combine_streamed_ln",
    )(*args)


def _mixer_sublayer(h2d, nb, s, w_in_p, w_out_p, a_biases, c_bias, lower_bound, norm_g_pad, sinks, w_sum, ln_g, ln_b):
    a_qkv, a_residue, b_all, c_qkv = _in_proj(h2d, w_in_p, nb, s)
    a_by_dilation = {1: a_qkv.reshape(nb, 1, s, A_COLS), **dict(zip(IN_PROJ_DILATIONS, a_residue))}
    o_list, l_list = [], []
    for (window, r), bias in zip(A_PATTERNS, a_biases):
        o, lse = _band_attn(a_by_dilation[r], bias, width=A_WIDTH, max_dist=window // r, want_lse=True)
        o_list.append(o)
        l_list.append(lse)
    oc = _band_attn(c_qkv.reshape(nb, 1, s, C_COLS), c_bias, width=C_WIDTH,
                    max_dist=C_WINDOW - 1, sinks=sinks, want_lse=False).reshape(nb * s, C_WIDTH)
    ob = _hgrn(b_all.reshape(nb, s, B_COLS), lower_bound, norm_g_pad, w_sum).reshape(nb * s, B_PAD_WIDTH)
    return _mix_out(o_list, l_list, ob, oc, h2d, w_out_p, ln_g, ln_b)


def _moe_sublayer(h2d, layer, router_w, router_bias, w_gate, w_up, w_down, sh_gate, sh_up, sh_down, ln_g, ln_b):
    t, d = h2d.shape
    bm = EXPERT_BLOCK
    e_t, rank_t, gate_t, counts, h_packed = _router(h2d, router_w.T.astype(F32),
                                                    router_bias.astype(F32).reshape(N_EXPERTS, 1))

    counts = counts.reshape(N_EXPERTS)
    padded = (counts + bm - 1) // bm * bm
    pad_end = jnp.cumsum(padded).astype(I32)
    offsets = pad_end - padded
    n_blocks = -(-(t * TOP_K + N_EXPERTS * (bm - 1)) // bm)

    dest_t = _dest_rows(e_t, rank_t, offsets)
    xs = _sc_scatter_rows(h_packed, dest_t, n_blocks * bm)
    ys = _experts(xs, offsets, (padded // bm).astype(I32), counts, jnp.full((1,), layer, I32), w_gate, w_up, w_down)
    tp = t // COMBINE_PARTS
    gate_tk = gate_t.T
    shared_w = (sh_gate.astype(BF16), sh_up.astype(BF16), sh_down.astype(BF16))
    out = None
    for part in range(COMBINE_PARTS):
        idx = dest_t[:, part * tp:(part + 1) * tp].reshape(TOP_K * tp)
        y_part = _sc_gather_rows(ys, idx).reshape(TOP_K, tp, d // 2)
        out = _combine_streamed(y_part, gate_tk, h2d, *shared_w, ln_g, ln_b, part, out)
    return out


def kernel(x, w_in, w_out, rel_bias_table, lower_bound_logits, hgrn_norm_g, attn_sinks, ln1_g, ln1_b, router_w, router_bias, expert_w_gate, expert_w_up, expert_w_down, shared_w_gate, shared_w_up, shared_w_down, ln2_g, ln2_b):
    nb, s, d = x.shape
    depth = w_in.shape[0]
    lb_probs = jax.nn.softmax(lower_bound_logits.astype(F32), axis=0)
    lower_bounds = jnp.cumsum(lb_probs, axis=0) - lb_probs[0]
    rel_table = rel_bias_table.astype(F32)
    a_biases = [_band_bias(rel_table, r, 0, A_HEADS) for _, r in A_PATTERNS]
    c_bias = _band_bias(rel_table, 1, A_HEADS, A_HEADS + C_HEADS)
    w_sum = jnp.asarray(_hgrn_sum_matrix(), dtype=BF16)
    row = lambda v: v.astype(F32).reshape(1, -1)

    h = x.astype(F32).reshape(nb * s, d)
    for l in range(depth):
        h = _mixer_sublayer(h, nb, s, _prep_w_in(w_in[l]), _prep_w_out(w_out[l].astype(F32)), a_biases, c_bias,
                            lower_bounds[l].reshape(1, B_KEY_WIDTH), _pad_heads_vec(hgrn_norm_g[l].astype(F32)),
                            attn_sinks[l].astype(F32), w_sum, row(ln1_g[l]), row(ln1_b[l]))
        h = _moe_sublayer(h, l, router_w[l], router_bias[l], expert_w_gate, expert_w_up, expert_w_down,
                          shared_w_gate[l], shared_w_up[l], shared_w_down[l], row(ln2_g[l]), row(ln2_b[l]))
    return h.reshape(nb, s, d).astype(x.dtype)
```

```python
import functools
import math

import numpy as np
import jax
import jax.numpy as jnp
from jax import lax
from jax.experimental import pallas as pl
from jax.experimental.pallas import tpu as pltpu
from jax.experimental.pallas import tpu_sc as plsc

F32 = jnp.float32
BF16 = jnp.bfloat16
I32 = jnp.int32
U32 = jnp.uint32

LANES = 128
SUBLANES = 8
VMEM_LIMIT = 56 * 1024 * 1024

D_MODEL = 1024
DEPTH = 2
HEAD_DIM = 64
BAND = 128
BAND_Q_TILE = 1024
MASK_VALUE = -1e30

A_HEADS = 6
A_PATTERNS = ((128, 1), (512, 4), (2048, 16))
IN_PROJ_DILATIONS = tuple(r for _, r in A_PATTERNS if r > 1)
IN_PROJ_ROWS = 512
MIX_OUT_ROWS = 512
B_HEADS = 4
B_KEY_DIM = 128
B_VAL_DIM = 96
B_VAL_PAD = 128
HG_CHUNK = 64
HG_STEP_ROWS = 512
C_HEADS = 4
C_KV_HEADS = 2
C_WINDOW = 128

A_WIDTH = A_HEADS * HEAD_DIM
B_KEY_WIDTH = B_HEADS * B_KEY_DIM
B_WIDTH = B_HEADS * B_VAL_DIM
B_PAD_WIDTH = B_HEADS * B_VAL_PAD
C_WIDTH = C_HEADS * HEAD_DIM
C_KV_WIDTH = C_KV_HEADS * HEAD_DIM
IN_SPLITS = (A_WIDTH, A_WIDTH, A_WIDTH, B_KEY_WIDTH, B_KEY_WIDTH, B_WIDTH, B_WIDTH, C_WIDTH, C_KV_WIDTH, C_KV_WIDTH)

A_COLS = 3 * A_WIDTH
B_COLS = 4 * B_KEY_WIDTH
C_COLS = 3 * C_WIDTH
MIX_PAD_WIDTH = A_WIDTH + B_PAD_WIDTH + C_WIDTH

REL_BUCKETS = 32
REL_MAX_DIST = 2048

N_EXPERTS = 256
TOP_K = 8
N_GROUPS = 8
TOPK_GROUPS = 4
EXPERT_FF = 256
SHARED_FF = 256
ROUTED_SCALE = 2.5
ROUTER_TOKENS = 512
DEST_TOKENS = 2048
EXPERT_BLOCK = 512
EXPERT_SLOTS = 6
COMBINE_PARTS = 2
COMBINE_ROWS = 512

DEEPNORM_ALPHA = (2 * DEPTH) ** 0.25
LN_EPS = 1e-5


def _cparams(*sem):
    return pltpu.CompilerParams(dimension_semantics=sem, vmem_limit_bytes=VMEM_LIMIT)


def _prep_w_in(w_in):
    d = w_in.shape[0]
    split_at = [int(i) for i in np.cumsum(IN_SPLITS)[:-1]]
    aq, ak, av, bq, bf, bi, bg, cq, ck, cv = jnp.split(w_in, split_at, axis=-1)
    pad_v = lambda w: jnp.pad(w.reshape(d, B_HEADS, B_VAL_DIM), ((0, 0), (0, 0), (0, B_VAL_PAD - B_VAL_DIM))).reshape(d, B_PAD_WIDTH)
    rep = lambda w: jnp.repeat(w.reshape(d, C_KV_HEADS, HEAD_DIM), C_HEADS // C_KV_HEADS, axis=1).reshape(d, C_WIDTH)
    scale = HEAD_DIM ** -0.5
    cols = [aq * scale, ak, av, bq, bf, pad_v(bi), pad_v(bg), cq * scale, rep(ck), rep(cv)]
    return jnp.concatenate(cols, axis=-1).astype(BF16)


def _prep_w_out(w_out):
    d = w_out.shape[1]
    wa = w_out[:A_WIDTH]
    wb = w_out[A_WIDTH:A_WIDTH + B_WIDTH].reshape(B_HEADS, B_VAL_DIM, d)
    wb = jnp.pad(wb, ((0, 0), (0, B_VAL_PAD - B_VAL_DIM), (0, 0))).reshape(B_PAD_WIDTH, d)
    wc = w_out[A_WIDTH + B_WIDTH:]
    return jnp.concatenate([wa, wb, wc], axis=0).astype(BF16)


def _pad_heads_vec(v):
    return jnp.pad(v.reshape(B_HEADS, B_VAL_DIM), ((0, 0), (0, B_VAL_PAD - B_VAL_DIM))).reshape(1, B_PAD_WIDTH)


def _rel_bucket(dist):
    max_exact = REL_BUCKETS // 2
    d = jnp.maximum(dist, 0)
    log_ratio = jnp.log(jnp.maximum(d, max_exact).astype(F32) / max_exact) / math.log(REL_MAX_DIST / max_exact)
    large = jnp.minimum(max_exact + (log_ratio * (REL_BUCKETS - max_exact)).astype(I32), REL_BUCKETS - 1)
    return jnp.where(d < max_exact, d, large)


def _band_bias(rel_table, r, head_lo, head_hi):
    dist = jnp.arange(BAND)[:, None] + BAND - jnp.arange(2 * BAND)[None, :]
    onehot = jax.nn.one_hot(_rel_bucket(dist * r), REL_BUCKETS, dtype=F32)
    return jnp.einsum("qkb,bh->hqk", onehot, rel_table[:, head_lo:head_hi], precision=lax.Precision.HIGHEST)


def _hgrn_sum_matrix():
    return np.tril(np.ones((HG_CHUNK, HG_CHUNK), np.float32))


HG_LEVELS = int(math.log2(HG_CHUNK))


def _in_proj_body(x_ref, w_ref, *rest):
    n_res = len(IN_PROJ_DILATIONS)
    a_ref, res_refs, (b_ref, c_ref, slabs) = rest[0], rest[1:1 + n_res], rest[1 + n_res:]
    tm = x_ref.shape[0]
    xb = x_ref[...].astype(BF16)
    a = jnp.dot(xb, w_ref[:, :A_COLS], preferred_element_type=F32)
    a_ref[...] = a.astype(BF16)
    for j in range(A_COLS // LANES):
        slabs[0, j] = a[:, j * LANES:(j + 1) * LANES]
    prev_r = 1
    for level, (r, ref) in enumerate(zip(IN_PROJ_DILATIONS, res_refs)):
        step, n, n_prev = r // prev_r, tm // r, tm // prev_r
        src, dst = slabs.at[level % 2], slabs.at[(level + 1) % 2]
        for p in range(r):
            p_prev, q = p % prev_r, p // prev_r
            for j in range(A_COLS // LANES):
                piece = src[j, pl.ds(p_prev * n_prev + q, n, stride=step), :]
                if level + 1 < len(IN_PROJ_DILATIONS):
                    dst[j, p * n:(p + 1) * n, :] = piece
                ref[0, p, :, j * LANES:(j + 1) * LANES] = piece.astype(BF16)
        prev_r = r
    for j in range(B_COLS // B_KEY_WIDTH):
        lo = A_COLS + j * B_KEY_WIDTH
        b_ref[:, j * B_KEY_WIDTH:(j + 1) * B_KEY_WIDTH] = jnp.dot(
            xb, w_ref[:, lo:lo + B_KEY_WIDTH], preferred_element_type=F32)
    c_ref[...] = jnp.dot(xb, w_ref[:, A_COLS + B_COLS:], preferred_element_type=F32).astype(BF16)


def _in_proj(x2d, w_p, nb, s):
    t, d = x2d.shape
    tm = IN_PROJ_ROWS
    n = w_p.shape[1]
    tiles = s // tm
    res_specs = [pl.BlockSpec((1, r, tm // r, A_COLS), lambda i: (i // tiles, 0, i % tiles, 0))
                 for r in IN_PROJ_DILATIONS]
    res_shapes = [jax.ShapeDtypeStruct((nb, r, s // r, A_COLS), BF16) for r in IN_PROJ_DILATIONS]
    outs = pl.pallas_call(
        _in_proj_body,
        grid=(t // tm,),
        in_specs=[pl.BlockSpec((tm, d), lambda i: (i, 0)),
                  pl.BlockSpec((d, n), lambda i: (0, 0))],
        out_specs=[pl.BlockSpec((tm, A_COLS), lambda i: (i, 0))] + res_specs + [
                   pl.BlockSpec((tm, B_COLS), lambda i: (i, 0)),
                   pl.BlockSpec((tm, C_COLS), lambda i: (i, 0))],
        out_shape=[jax.ShapeDtypeStruct((t, A_COLS), BF16)] + res_shapes + [
                   jax.ShapeDtypeStruct((t, B_COLS), F32),
                   jax.ShapeDtypeStruct((t, C_COLS), BF16)],
        scratch_shapes=[pltpu.VMEM((2, A_COLS // LANES, tm, LANES), F32)],
        compiler_params=_cparams("arbitrary"),
        name="in_proj",
    )(x2d, w_p)
    return outs[0], list(outs[1:-2]), outs[-2], outs[-1]


def _band_attn_body(*refs, width, max_dist, has_sink, want_lse):
    q_ref, kp_ref, kc_ref, vp_ref, vc_ref, bias_ref = refs[:6]
    rest = refs[6:]
    if has_sink:
        sink_ref, rest = rest[0], rest[1:]
    o_ref = rest[0]
    lse_ref = rest[1] if want_lse else None

    first_tile = pl.program_id(2) == 0
    row = lax.broadcasted_iota(I32, (BAND, 2 * BAND), 0)
    col = lax.broadcasted_iota(I32, (BAND, 2 * BAND), 1)
    dist = row + BAND - col
    in_band = (dist >= 0) & (dist <= max_dist)
    first_mask = in_band & ((col >= BAND) | jnp.logical_not(first_tile))
    lane = lax.broadcasted_iota(I32, (BAND, LANES), 1)
    low_half = lane < HEAD_DIM

    for qb in range(q_ref.shape[0] // BAND):
        rows = slice(qb * BAND, (qb + 1) * BAND)
        mask = first_mask if qb == 0 else in_band
        lse_tile = jnp.zeros((BAND, LANES), F32)
        for tile in range(width // LANES):
            sl = slice(tile * LANES, (tile + 1) * LANES)
            q2 = q_ref[rows, sl]
            if qb == 0:
                k2 = jnp.concatenate([kp_ref[:, sl], kc_ref[:BAND, sl]], axis=0)
                v2 = jnp.concatenate([vp_ref[:, sl], vc_ref[:BAND, sl]], axis=0)
            else:
                k2 = kc_ref[(qb - 1) * BAND:(qb + 1) * BAND, sl]
                v2 = vc_ref[(qb - 1) * BAND:(qb + 1) * BAND, sl]
            outs, lses = [], []
            for half in range(2):
                h = 2 * tile + half
                qm = jnp.where(low_half if half == 0 else jnp.logical_not(low_half), q2, jnp.zeros_like(q2))
                s = lax.dot_general(qm, k2, (((1,), (1,)), ((), ())), preferred_element_type=F32)
                s = s + bias_ref[h]
                s = jnp.where(mask, s, MASK_VALUE)
                m = jnp.max(s, axis=-1, keepdims=True)
                if has_sink:
                    sink = sink_ref[h]
                    m = jnp.maximum(m, sink)
                p = jnp.exp(s - m)
                den = jnp.sum(p, axis=-1, keepdims=True)
                if has_sink:
                    den = den + jnp.exp(sink - m)
                pv = jnp.dot(p.astype(BF16), v2, preferred_element_type=F32)
                outs.append(pv / den)
                if want_lse:
                    lse_tile = jnp.where(lane == h, m + jnp.log(den), lse_tile)
            o_ref[rows, sl] = jnp.where(low_half, outs[0], outs[1]).astype(o_ref.dtype)
        if want_lse:
            lse_ref[rows, :] = lse_tile


def _band_attn(src, bias, *, width, max_dist, sinks=None, want_lse):
    nb, r, length, _ = src.shape
    qt = min(BAND_Q_TILE, length)
    bands = qt // BAND
    heads = width // HEAD_DIM
    has_sink = sinks is not None

    def cur(off):
        return pl.BlockSpec((None, None, qt, width), lambda b, p, i: (b, p, i, off))

    def prev(off):
        return pl.BlockSpec((None, None, BAND, width), lambda b, p, i: (b, p, jnp.maximum(i * bands - 1, 0), off))

    in_specs = [cur(0), prev(1), cur(1), prev(2), cur(2),
                pl.BlockSpec((heads, BAND, 2 * BAND), lambda b, p, i: (0, 0, 0))]
    args = [src, src, src, src, src, bias]
    if has_sink:
        in_specs.append(pl.BlockSpec(memory_space=pltpu.SMEM))
        args.append(sinks)
    out_spec = pl.BlockSpec((None, None, qt, width), lambda b, p, i: (b, p, i, 0))
    out_sds = jax.ShapeDtypeStruct((nb, r, length, width), F32 if want_lse else BF16)
    lse_spec = pl.BlockSpec((None, None, qt, LANES), lambda b, p, i: (b, p, i, 0))
    lse_sds = jax.ShapeDtypeStruct((nb, r, length, LANES), F32)
    body = functools.partial(_band_attn_body, width=width, max_dist=max_dist, has_sink=has_sink, want_lse=want_lse)
    return pl.pallas_call(
        body,
        grid=(nb, r, length // qt),
        in_specs=in_specs,
        out_specs=[out_spec, lse_spec] if want_lse else out_spec,
        out_shape=[out_sds, lse_sds] if want_lse else out_sds,
        compiler_params=_cparams("arbitrary", "arbitrary", "arbitrary"),
        name="band_attn_r%d_w%d" % (r, width),
    )(*args)


def _hgrn_body(b_ref, lb_ref, ng_ref, w_ref, o_ref, state_ref):
    c = HG_CHUNK
    kd = B_KEY_DIM

    @pl.when(pl.program_id(1) == 0)
    def _():
        state_ref[...] = jnp.zeros_like(state_ref)

    trow = lax.broadcasted_iota(I32, (c, 1), 0)
    odd_row = (trow % 2) == 1
    low_sub = lax.broadcasted_iota(I32, (SUBLANES, 1), 0) < SUBLANES // 2
    ti = lax.broadcasted_iota(I32, (c, c), 0)
    si = lax.broadcasted_iota(I32, (c, c), 1)
    nt = (((1,), (1,)), ((), ()))
    tri = w_ref[...]

    in_level = [((ti // m) == (si // m)) & ((ti % m) >= (m // 2)) & ((si % m) < (m // 2))
                for m in (c >> lvl for lvl in range(HG_LEVELS))]

    states = [state_ref[h] for h in range(B_HEADS)]
    for chunk, h in [(ci, hi) for ci in range(b_ref.shape[1] // c) for hi in range(B_HEADS)]:
        rows = slice(chunk * c, (chunk + 1) * c)
        ks = slice(h * kd, (h + 1) * kd)
        q = b_ref[0, rows, ks]
        f = b_ref[0, rows, B_KEY_WIDTH + h * kd:B_KEY_WIDTH + (h + 1) * kd]
        inp = b_ref[0, rows, 2 * B_KEY_WIDTH + h * kd:2 * B_KEY_WIDTH + (h + 1) * kd]
        gate = b_ref[0, rows, 3 * B_KEY_WIDTH + h * kd:3 * B_KEY_WIDTH + (h + 1) * kd]
        lb = lb_ref[:, ks]

        a = jnp.exp(-jnp.abs(f))
        big = 1.0 / (1.0 + a)
        small = a * big
        pos = f >= 0.0
        forget = lb + (1.0 - lb) * jnp.where(pos, big, small)
        log_f = jnp.log(forget)
        key = (1.0 - lb) * jnp.where(pos, small, big)
        qs = q * jax.nn.sigmoid(q)

        g_hi = log_f.astype(BF16)
        g_lo = (log_f - g_hi.astype(F32)).astype(BF16)
        cum2 = jnp.dot(tri, jnp.concatenate([g_hi, g_lo], axis=1), preferred_element_type=F32)
        cum = cum2[:, :kd] + cum2[:, kd:]
        last = cum[c - 1:c]

        q_dec = (qs * jnp.exp(cum)).astype(BF16)
        k_dec = (key * jnp.exp(last - cum)).astype(BF16)
        inp_b = inp.astype(BF16)

        def level_decay(m):
            if m == 2:
                return jnp.where(odd_row, forget, 1.0)
            if m == SUBLANES // 2:
                row_bcast = lambda i: jnp.broadcast_to(cum[i:i + 1], (SUBLANES, kd))
                pieces = [jnp.where(low_sub, row_bcast(SUBLANES * j + m // 2 - 1), row_bcast(SUBLANES * j + m + m // 2 - 1))
                          for j in range(c // SUBLANES)]
            else:
                pieces = [jnp.broadcast_to(cum[b * m + m // 2 - 1:b * m + m // 2], (m, kd)) for b in range(c // m)]
            return jnp.exp(-jnp.abs(cum - jnp.concatenate(pieces, axis=0)))

        st = states[h]
        inter = lax.dot_general(q_dec, st.astype(BF16), nt, preferred_element_type=F32)

        scores = jnp.where(ti == si,
                           lax.dot_general(qs.astype(BF16), key.astype(BF16), nt, preferred_element_type=F32), 0.0)
        m = c
        for lvl in range(HG_LEVELS):
            el = level_decay(m)
            sl = lax.dot_general((qs * el).astype(BF16), (key * el).astype(BF16), nt, preferred_element_type=F32)
            scores = jnp.where(in_level[lvl], sl, scores)
            m //= 2
        intra = jnp.dot(scores.astype(BF16), inp_b, preferred_element_type=F32)

        new_st = st * jnp.exp(last) + lax.dot_general(
            inp_b, k_dec, (((0,), (0,)), ((), ())), preferred_element_type=F32)
        states[h] = new_st

        o = inter + intra
        ms = jnp.sum(o * o, axis=-1, keepdims=True) * (1.0 / B_VAL_DIM)
        o = o * lax.rsqrt(ms + 1e-6)
        o_ref[0, rows, ks] = (o * ng_ref[:, ks] * (gate * jax.nn.sigmoid(gate))).astype(o_ref.dtype)

    for h in range(B_HEADS):
        state_ref[h] = states[h]


def _hgrn(b_all, lower_bound, norm_g_pad, w_sum):
    nb, s, _ = b_all.shape
    c = HG_STEP_ROWS
    return pl.pallas_call(
        _hgrn_body,
        grid=(nb, s // c),
        in_specs=[pl.BlockSpec((1, c, B_COLS), lambda b, i: (b, i, 0)),
                  pl.BlockSpec((1, B_KEY_WIDTH), lambda b, i: (0, 0)),
                  pl.BlockSpec((1, B_PAD_WIDTH), lambda b, i: (0, 0)),
                  pl.BlockSpec(w_sum.shape, lambda b, i: (0, 0))],
        out_specs=pl.BlockSpec((1, c, B_PAD_WIDTH), lambda b, i: (b, i, 0)),
        out_shape=jax.ShapeDtypeStruct((nb, s, B_PAD_WIDTH), BF16),
        scratch_shapes=[pltpu.VMEM((B_HEADS, B_VAL_PAD, B_KEY_DIM), F32)],
        compiler_params=_cparams("arbitrary", "arbitrary"),
        name="hgrn2",
    )(b_all, lower_bound, norm_g_pad, w_sum)


def _layer_norm_rows(z, g, b):
    mu = jnp.mean(z, axis=-1, keepdims=True)
    zc = z - mu
    var = jnp.mean(zc * zc, axis=-1, keepdims=True)
    return zc * lax.rsqrt(var + LN_EPS) * g + b


def _mix_out_body(*refs):
    n_pat = len(A_PATTERNS)
    o_refs, l_refs = refs[:n_pat], refs[n_pat:2 * n_pat]
    ob, oc, x_ref, w_ref, g_ref, b_ref, spread_ref, out_ref = refs[2 * n_pat:2 * n_pat + 8]
    scratch = refs[2 * n_pat + 8:]
    tm = x_ref.shape[0]
    n_slabs = A_WIDTH // LANES

    def token_order(ref, r, scr):
        slabs = ref.shape[-1] // LANES
        if r == 1:
            return [ref[0, 0, :, j * LANES:(j + 1) * LANES] for j in range(slabs)]
        for p in range(r):
            for j in range(slabs):
                scr[j, pl.ds(p, tm // r, stride=r), :] = ref[0, p, :, j * LANES:(j + 1) * LANES]
        return [scr[j] for j in range(slabs)]

    scr_iter = iter(scratch)
    dil = [r for _, r in A_PATTERNS]
    o_slabs = [token_order(ref, r, None if r == 1 else next(scr_iter)) for ref, r in zip(o_refs, dil)]
    ls = [token_order(ref, r, None if r == 1 else next(scr_iter))[0] for ref, r in zip(l_refs, dil)]

    m = functools.reduce(jnp.maximum, ls)
    ws = [jnp.exp(l - m) for l in ls]
    inv = 1.0 / functools.reduce(lambda a, b: a + b, ws)
    spread = spread_ref[...]
    wide = []
    for w in ws:
        wn = w * inv
        hi = wn.astype(BF16)
        lo = (wn - hi.astype(F32)).astype(BF16)
        wide.append(jnp.dot(hi, spread, preferred_element_type=F32) + jnp.dot(lo, spread, preferred_element_type=F32))
    merged = [functools.reduce(lambda a, b: a + b,
                               [w[:, j * LANES:(j + 1) * LANES] * o[j] for w, o in zip(wide, o_slabs)])
              for j in range(n_slabs)]
    cat = jnp.concatenate([m.astype(BF16) for m in merged] + [ob[...], oc[...]], axis=1)
    y = jnp.dot(cat, w_ref[...], preferred_element_type=F32)
    z = DEEPNORM_ALPHA * x_ref[...] + y
    out_ref[...] = _layer_norm_rows(z, g_ref[...], b_ref[...])


def _mix_out(o_list, l_list, ob, oc, x2d, w_out_p, ln_g, ln_b):
    t, d = x2d.shape
    tm = MIX_OUT_ROWS
    tiles = o_list[0].shape[1] * o_list[0].shape[2] // tm
    row = lambda w: pl.BlockSpec((tm, w), lambda i: (i, 0))
    full = lambda a: pl.BlockSpec(a.shape, lambda i: (0, 0))
    res = lambda a: pl.BlockSpec((1, a.shape[1], tm // a.shape[1], a.shape[3]), lambda i: (i // tiles, 0, i % tiles, 0))
    spread = jnp.asarray(np.arange(LANES)[:, None] == np.arange(A_WIDTH)[None, :] // HEAD_DIM, dtype=BF16)
    dilated = [a for a in o_list + l_list if a.shape[1] > 1]
    return pl.pallas_call(
        _mix_out_body,
        grid=(t // tm,),
        in_specs=[res(a) for a in o_list + l_list] + [row(B_PAD_WIDTH), row(C_WIDTH), row(d), full(w_out_p),
                                                      full(ln_g), full(ln_b), full(spread)],
        out_specs=row(d),
        out_shape=jax.ShapeDtypeStruct((t, d), F32),
        scratch_shapes=[pltpu.VMEM((a.shape[3] // LANES, tm, LANES), F32) for a in dilated],
        compiler_params=_cparams("arbitrary"),
        name="mix_out_ln",
    )(*o_list, *l_list, ob, oc, x2d, w_out_p, ln_g, ln_b, spread)


def _router_body(h_ref, rw_ref, bias_ref, tri_ref, e_ref, rank_ref, gate_ref, cnt_ref, hp_ref, carry_ref):
    tn = h_ref.shape[0]
    per_group = N_EXPERTS // N_GROUPS
    neg_inf = -jnp.inf

    @pl.when(pl.program_id(0) == 0)
    def _():
        carry_ref[...] = jnp.zeros_like(carry_ref)

    hp_ref[...] = _pack_bf16_pairs(h_ref[...])

    h = h_ref[...]
    h_hi = h.astype(BF16)
    h_lo = (h - h_hi.astype(F32)).astype(BF16)
    nt = (((1,), (1,)), ((), ()))
    w_hi, w_lo = rw_ref[0], rw_ref[1]
    logits = (lax.dot_general(w_hi, h_hi, nt, preferred_element_type=F32)
              + lax.dot_general(w_hi, h_lo, nt, preferred_element_type=F32)
              + lax.dot_general(w_lo, h_hi, nt, preferred_element_type=F32))
    scores = jax.nn.sigmoid(logits)
    choice = scores + bias_ref[...]

    def first_max(vals, idx, sentinel):
        top = jnp.max(vals, axis=0, keepdims=True)
        return top, jnp.min(jnp.where(vals == top, idx, sentinel), axis=0, keepdims=True)

    li = lax.broadcasted_iota(I32, (per_group, tn), 0).astype(F32)
    group_rows = []
    for g in range(N_GROUPS):
        cg = choice[g * per_group:(g + 1) * per_group]
        m1, first = first_max(cg, li, float(per_group))
        m2 = jnp.max(jnp.where(li == first, neg_inf, cg), axis=0, keepdims=True)
        group_rows.append(m1 + m2)
    group_score = jnp.concatenate(group_rows, axis=0)

    gi = lax.broadcasted_iota(I32, (N_GROUPS, tn), 0).astype(F32)
    group_ok = jnp.zeros((N_GROUPS, tn), F32)
    cur = group_score
    for _ in range(TOPK_GROUPS):
        _, first = first_max(cur, gi, float(N_GROUPS))
        pick = gi == first
        group_ok = jnp.where(pick, 1.0, group_ok)
        cur = jnp.where(pick, neg_inf, cur)

    cur = jnp.concatenate(
        [jnp.where(group_ok[g:g + 1] > 0.0, choice[g * per_group:(g + 1) * per_group], MASK_VALUE)
         for g in range(N_GROUPS)], axis=0)
    ei = lax.broadcasted_iota(I32, (N_EXPERTS, tn), 0).astype(F32)
    chosen = jnp.zeros((N_EXPERTS, tn), F32)
    picks, gates = [], []
    for _ in range(TOP_K):
        _, idx = first_max(cur, ei, float(N_EXPERTS))
        pick = ei == idx
        picks.append(idx)
        gates.append(jnp.sum(jnp.where(pick, scores, 0.0), axis=0, keepdims=True))
        chosen = jnp.where(pick, 1.0, chosen)
        cur = jnp.where(pick, neg_inf, cur)

    gate = jnp.concatenate(gates, axis=0)
    gate_ref[...] = gate / jnp.sum(gate, axis=0, keepdims=True) * ROUTED_SCALE
    e_ref[...] = jnp.concatenate(picks, axis=0).astype(I32)

    before = jnp.dot(chosen.astype(BF16), tri_ref[...], preferred_element_type=F32) + carry_ref[...]
    ranks = [jnp.sum(jnp.where(ei == idx, before, 0.0), axis=0, keepdims=True) for idx in picks]
    rank_ref[...] = jnp.concatenate(ranks, axis=0).astype(I32)
    carry = carry_ref[...] + jnp.sum(chosen, axis=1, keepdims=True)
    carry_ref[...] = carry
    cnt_ref[...] = carry.astype(I32)


def _router(h2d, rw_t, bias_col):
    t, d = h2d.shape
    tn = ROUTER_TOKENS
    tri = jnp.asarray(np.triu(np.ones((tn, tn), np.float32), k=1), dtype=BF16)
    rw_hi = rw_t.astype(BF16)
    rw_split = jnp.stack([rw_hi, (rw_t - rw_hi.astype(F32)).astype(BF16)])
    tok = lambda: pl.BlockSpec((TOP_K, tn), lambda i: (0, i))
    return pl.pallas_call(
        _router_body,
        grid=(t // tn,),
        in_specs=[pl.BlockSpec((tn, d), lambda i: (i, 0)),
                  pl.BlockSpec((2, N_EXPERTS, d), lambda i: (0, 0, 0)),
                  pl.BlockSpec((N_EXPERTS, 1), lambda i: (0, 0)),
                  pl.BlockSpec((tn, tn), lambda i: (0, 0))],
        out_specs=[tok(), tok(), tok(), pl.BlockSpec((N_EXPERTS, 1), lambda i: (0, 0)),
                   pl.BlockSpec((tn, d // 2), lambda i: (i, 0))],
        out_shape=[jax.ShapeDtypeStruct((TOP_K, t), I32),
                   jax.ShapeDtypeStruct((TOP_K, t), I32),
                   jax.ShapeDtypeStruct((TOP_K, t), F32),
                   jax.ShapeDtypeStruct((N_EXPERTS, 1), I32),
                   jax.ShapeDtypeStruct((t, d // 2), U32)],
        scratch_shapes=[pltpu.VMEM((N_EXPERTS, 1), F32)],
        compiler_params=_cparams("arbitrary"),
        name="moe_router",
    )(h2d, rw_split, bias_col, tri)


def _pack_bf16_pairs(x):
    w = x.shape[1] // 2
    hi = lax.bitcast_convert_type(x[:, :w].astype(BF16).astype(F32), U32)
    lo = lax.bitcast_convert_type(x[:, w:].astype(BF16).astype(F32), U32)
    return hi | (lo >> 16)


def _unpack_bf16_pairs(p):
    hi = lax.bitcast_convert_type(p & jnp.uint32(0xFFFF0000), F32)
    lo = lax.bitcast_convert_type(p << 16, F32)
    return hi, lo


def _dest_body(e_ref, rank_ref, offs_ref, dest_ref):
    tn = e_ref.shape[1]
    ei = lax.broadcasted_iota(I32, (N_EXPERTS, tn), 0)
    offs = offs_ref[...]
    rows = [jnp.sum(jnp.where(ei == e_ref[k:k + 1, :], offs, 0.0), axis=0, keepdims=True) for k in range(TOP_K)]
    dest_ref[...] = jnp.concatenate(rows, axis=0).astype(I32) + rank_ref[...]


def _dest_rows(e_t, rank_t, offsets):
    t = e_t.shape[1]
    tn = DEST_TOKENS
    tok = pl.BlockSpec((TOP_K, tn), lambda i: (0, i))
    return pl.pallas_call(
        _dest_body,
        grid=(t // tn,),
        in_specs=[tok, tok, pl.BlockSpec((N_EXPERTS, 1), lambda i: (0, 0))],
        out_specs=tok,
        out_shape=jax.ShapeDtypeStruct((TOP_K, t), I32),
        compiler_params=_cparams("arbitrary"),
        name="moe_dest",
    )(e_t, rank_t, offsets.astype(F32).reshape(N_EXPERTS, 1))


def _expert_body(offs_ref, nblk_ref, cnt_ref, layer_ref, xs_hbm, wg_ref, wu_ref, wd_ref, ys_hbm,
                 xbuf, ybuf, wg_b, wu_b, wd_b, sem_in, sem_out):
    e = pl.program_id(0)
    slots, bm = xbuf.shape[:2]
    ahead = slots - 1
    n = nblk_ref[e]
    first = offs_ref[e] // bm
    total = (offs_ref[N_EXPERTS - 1] // bm) + nblk_ref[N_EXPERTS - 1]

    def rows(g):
        return pl.ds(pl.multiple_of(g * bm, bm), bm)

    def x_copy(g, slot):
        return pltpu.make_async_copy(xs_hbm.at[rows(g)], xbuf.at[slot], sem_in.at[slot])

    def y_copy(g, slot):
        return pltpu.make_async_copy(ybuf.at[slot], ys_hbm.at[rows(g)], sem_out.at[slot])

    @pl.when(e == 0)
    def _():
        for g0 in range(ahead):
            @pl.when(g0 < total)
            def _():
                x_copy(g0, g0).start()

    wg_b[...] = wg_ref[0, 0].astype(BF16)
    wu_b[...] = wu_ref[0, 0].astype(BF16)
    wd_b[...] = wd_ref[0, 0].astype(BF16)

    def block(j, carry):
        g = first + j
        slot = g % slots
        x_copy(g, slot).wait()

        @pl.when(g + ahead < total)
        def _():
            x_copy(g + ahead, (g + ahead) % slots).start()

        @pl.when(g >= slots)
        def _():
            y_copy(g - slots, slot).wait()

        live = lax.broadcasted_iota(I32, (bm, 1), 0) < (cnt_ref[e] - j * bm)
        hi, lo = _unpack_bf16_pairs(jnp.where(live, xbuf[slot], jnp.uint32(0)))
        xb = jnp.concatenate([hi.astype(BF16), lo.astype(BF16)], axis=1)
        gate = jnp.dot(xb, wg_b[...], preferred_element_type=F32)
        up = jnp.dot(xb, wu_b[...], preferred_element_type=F32)
        hidden = (gate * jax.nn.sigmoid(gate) * up).astype(BF16)
        ybuf[slot] = _pack_bf16_pairs(jnp.dot(hidden, wd_b[...], preferred_element_type=F32))
        y_copy(g, slot).start()
        return carry

    lax.fori_loop(0, n, block, 0)

    @pl.when(e == N_EXPERTS - 1)
    def _():
        for back in range(1, slots + 1):
            @pl.when(total >= back)
            def _():
                y_copy(total - back, (total - back) % slots).wait()


def _experts(xs, offsets, n_blk, counts, layer, w_gate, w_up, w_down):
    n_rows, dp = xs.shape
    bm = EXPERT_BLOCK
    d, ff = w_gate.shape[-2:]
    w_spec = lambda a, b: pl.BlockSpec((1, 1, a, b), lambda e, of, nb, ct, ly: (ly[0], e, 0, 0))
    grid_spec = pltpu.PrefetchScalarGridSpec(
        num_scalar_prefetch=4,
        grid=(N_EXPERTS,),
        in_specs=[pl.BlockSpec(memory_space=pl.ANY), w_spec(d, ff), w_spec(d, ff), w_spec(ff, d)],
        out_specs=pl.BlockSpec(memory_space=pl.ANY),
        scratch_shapes=[pltpu.VMEM((EXPERT_SLOTS, bm, dp), U32), pltpu.VMEM((EXPERT_SLOTS, bm, dp), U32),
                        pltpu.VMEM((d, ff), BF16), pltpu.VMEM((d, ff), BF16), pltpu.VMEM((ff, d), BF16),
                        pltpu.SemaphoreType.DMA((EXPERT_SLOTS,)), pltpu.SemaphoreType.DMA((EXPERT_SLOTS,))],
    )
    return pl.pallas_call(
        _expert_body,
        grid_spec=grid_spec,
        out_shape=jax.ShapeDtypeStruct((n_rows, dp), U32),
        compiler_params=_cparams("arbitrary"),
        name="moe_experts",
    )(offsets, n_blk, counts, layer, xs, w_gate, w_up, w_down)


SC_GATHER_ROWS = 64


def _sc_gather_rows(table, idx):
    info = plsc.get_sparse_core_info()
    nc, ns = info.num_cores, info.num_subcores
    workers = nc * ns
    n = idx.shape[0]
    w = table.shape[1]
    ch = SC_GATHER_ROWS
    per_worker = n // workers
    steps = per_worker // ch
    assert per_worker * workers == n and steps * ch == per_worker and steps % 2 == 0
    mesh = plsc.VectorSubcoreMesh(core_axis_name="c", subcore_axis_name="s")

    @functools.partial(
        pl.kernel, mesh=mesh,
        out_type=jax.ShapeDtypeStruct((n, w), table.dtype),
        scratch_types=[pltpu.VMEM((steps, ch), I32), pltpu.VMEM((2, ch, w), table.dtype),
                       pltpu.SemaphoreType.DMA((2,)), pltpu.SemaphoreType.DMA((2,))],
    )
    def gather_kernel(table_hbm, idx_hbm, out_hbm, idx_v, rows_v, gsem, wsem):
        wid = lax.axis_index("s") * nc + lax.axis_index("c")
        base = wid * per_worker
        pltpu.sync_copy(idx_hbm.at[wid], idx_v)

        def gather(i, slot):
            return pltpu.make_async_copy(table_hbm.at[idx_v.at[i]], rows_v.at[slot], gsem.at[slot])

        def write(i, slot):
            off = pl.multiple_of(base + i * ch, ch)
            return pltpu.make_async_copy(rows_v.at[slot], out_hbm.at[pl.ds(off, ch)], wsem.at[slot])

        gather(0, 0).start()

        @pl.loop(0, steps, step=2)
        def _(i):
            gather(i + 1, 1).start()
            gather(i, 0).wait()
            write(i, 0).start()
            write(i, 0).wait()

            @pl.when(i + 2 < steps)
            def _():
                gather(i + 2, 0).start()

            gather(i + 1, 1).wait()
            write(i + 1, 1).start()
            write(i + 1, 1).wait()

    return gather_kernel(table, idx.reshape(workers, steps, ch))


def _sc_scatter_rows(rows, idx, n_out):
    info = plsc.get_sparse_core_info()
    nc, ns = info.num_cores, info.num_subcores
    workers = nc * ns
    kk, t = idx.shape
    w = rows.shape[1]
    ch = SC_GATHER_ROWS
    per_worker = t // workers
    steps = per_worker // ch
    assert per_worker * workers == t and steps * ch == per_worker and steps % 2 == 0
    idx_w = idx.reshape(kk, workers, steps, ch).transpose(1, 0, 2, 3).reshape(workers, kk * steps, ch)
    mesh = plsc.VectorSubcoreMesh(core_axis_name="c", subcore_axis_name="s")

    @functools.partial(
        pl.kernel, mesh=mesh,
        out_type=jax.ShapeDtypeStruct((n_out, w), rows.dtype),
        scratch_types=[pltpu.VMEM((kk * steps, ch), I32), pltpu.VMEM((2, ch, w), rows.dtype),
                       pltpu.SemaphoreType.DMA((2,)), pltpu.SemaphoreType.DMA((2,))],
    )
    def scatter_kernel(rows_hbm, idx_hbm, out_hbm, idx_v, rows_v, rsem, ssem):
        wid = lax.axis_index("s") * nc + lax.axis_index("c")
        base = wid * per_worker
        pltpu.sync_copy(idx_hbm.at[wid], idx_v)

        def read(i, slot):
            off = pl.multiple_of(base + i * ch, ch)
            return pltpu.make_async_copy(rows_hbm.at[pl.ds(off, ch)], rows_v.at[slot], rsem.at[slot])

        def scatter(i, k, slot):
            return pltpu.make_async_copy(rows_v.at[slot], out_hbm.at[idx_v.at[k * steps + i]], ssem.at[slot])

        def scatter_all(i, slot):
            for k in range(kk):
                scatter(i, k, slot).start()
            for k in range(kk):
                scatter(i, k, slot).wait()

        read(0, 0).start()

        @pl.loop(0, steps, step=2)
        def _(i):
            read(i + 1, 1).start()
            read(i, 0).wait()
            scatter_all(i, 0)

            @pl.when(i + 2 < steps)
            def _():
                read(i + 2, 0).start()

            read(i + 1, 1).wait()
            scatter_all(i + 1, 1)

    return scatter_kernel(rows, idx_w)


def _combine_streamed_body(y_ref, gate_ref, h_ref, sg_ref, su_ref, sd_ref, g_ref, b_ref, *rest):
    out_ref = rest[-1]
    half = h_ref.shape[1] // 2
    h = h_ref[...]
    hb = h.astype(BF16)
    sg = jnp.dot(hb, sg_ref[...], preferred_element_type=F32)
    su = jnp.dot(hb, su_ref[...], preferred_element_type=F32)
    shared = jnp.dot((sg * jax.nn.sigmoid(sg) * su).astype(BF16), sd_ref[...], preferred_element_type=F32)
    z = DEEPNORM_ALPHA * h + shared
    z_hi, z_lo = z[:, :half], z[:, half:]
    for k in range(TOP_K):
        y_hi, y_lo = _unpack_bf16_pairs(y_ref[k])
        gate = gate_ref[:, k:k + 1]
        z_hi = z_hi + gate * y_hi
        z_lo = z_lo + gate * y_lo
    out_ref[...] = _layer_norm_rows(jnp.concatenate([z_hi, z_lo], axis=1), g_ref[...], b_ref[...])


def _combine_streamed(y_part, gate_tk, h2d, sh_gate, sh_up, sh_down, ln_g, ln_b, part, partial_out):
    t, d = h2d.shape
    tt = COMBINE_ROWS
    steps = y_part.shape[1] // tt
    first = part * steps
    full = lambda a: pl.BlockSpec(a.shape, lambda i: (0, 0))
    in_specs = [pl.BlockSpec((TOP_K, tt, d // 2), lambda i: (0, i, 0)),
                pl.BlockSpec((tt, TOP_K), lambda i: (first + i, 0)),
                pl.BlockSpec((tt, d), lambda i: (first + i, 0)),
                full(sh_gate), full(sh_up), full(sh_down), full(ln_g), full(ln_b)]
    args = [y_part, gate_tk, h2d, sh_gate, sh_up, sh_down, ln_g, ln_b]
    aliases = {}
    if partial_out is not None:
        in_specs.append(pl.BlockSpec(memory_space=pl.ANY))
        args.append(partial_out)
        aliases = {len(args) - 1: 0}
    return pl.pallas_call(
        _combine_streamed_body,
        grid=(steps,),
        in_specs=in_specs,
        out_specs=pl.BlockSpec((tt, d), lambda i: (first + i, 0)),
        out_shape=jax.ShapeDtypeStruct((t, d), F32),
        input_output_aliases=aliases,
        compiler_params=_cparams("arbitrary"),
        name="moe_combine_streamed_ln",
    )(*args)


def _mixer_sublayer(h2d, nb, s, w_in_p, w_out_p, a_biases, c_bias, lower_bound, norm_g_pad, sinks, w_sum, ln_g, ln_b):
    a_qkv, a_residue, b_all, c_qkv = _in_proj(h2d, w_in_p, nb, s)
    a_by_dilation = {1: a_qkv.reshape(nb, 1, s, A_COLS), **dict(zip(IN_PROJ_DILATIONS, a_residue))}
    o_list, l_list = [], []
    for (window, r), bias in zip(A_PATTERNS, a_biases):
        o, lse = _band_attn(a_by_dilation[r], bias, width=A_WIDTH, max_dist=window // r, want_lse=True)
        o_list.append(o)
        l_list.append(lse)
    oc = _band_attn(c_qkv.reshape(nb, 1, s, C_COLS), c_bias, width=C_WIDTH,
                    max_dist=C_WINDOW - 1, sinks=sinks, want_lse=False).reshape(nb * s, C_WIDTH)
    ob = _hgrn(b_all.reshape(nb, s, B_COLS), lower_bound, norm_g_pad, w_sum).reshape(nb * s, B_PAD_WIDTH)
    return _mix_out(o_list, l_list, ob, oc, h2d, w_out_p, ln_g, ln_b)


def _moe_sublayer(h2d, layer, router_w, router_bias, w_gate, w_up, w_down, sh_gate, sh_up, sh_down, ln_g, ln_b):
    t, d = h2d.shape
    bm = EXPERT_BLOCK
    e_t, rank_t, gate_t, counts, h_packed = _router(h2d, router_w.T.astype(F32),
                                                    router_bias.astype(F32).reshape(N_EXPERTS, 1))

    counts = counts.reshape(N_EXPERTS)
    padded = (counts + bm - 1) // bm * bm
    pad_end = jnp.cumsum(padded).astype(I32)
    offsets = pad_end - padded
    n_blocks = -(-(t * TOP_K + N_EXPERTS * (bm - 1)) // bm)

    dest_t = _dest_rows(e_t, rank_t, offsets)
    xs = _sc_scatter_rows(h_packed, dest_t, n_blocks * bm)
    ys = _experts(xs, offsets, (padded // bm).astype(I32), counts, jnp.full((1,), layer, I32), w_gate, w_up, w_down)
    tp = t // COMBINE_PARTS
    gate_tk = gate_t.T
    shared_w = (sh_gate.astype(BF16), sh_up.astype(BF16), sh_down.astype(BF16))
    out = None
    for part in range(COMBINE_PARTS):
        idx = dest_t[:, part * tp:(part + 1) * tp].reshape(TOP_K * tp)
        y_part = _sc_gather_rows(ys, idx).reshape(TOP_K, tp, d // 2)
        out = _combine_streamed(y_part, gate_tk, h2d, *shared_w, ln_g, ln_b, part, out)
    return out


def kernel(x, w_in, w_out, rel_bias_table, lower_bound_logits, hgrn_norm_g, attn_sinks, ln1_g, ln1_b, router_w, router_bias, expert_w_gate, expert_w_up, expert_w_down, shared_w_gate, shared_w_up, shared_w_down, ln2_g, ln2_b):
    nb, s, d = x.shape
    depth = w_in.shape[0]
    lb_probs = jax.nn.softmax(lower_bound_logits.astype(F32), axis=0)
    lower_bounds = jnp.cumsum(lb_probs, axis=0) - lb_probs[0]
    rel_table = rel_bias_table.astype(F32)
    a_biases = [_band_bias(rel_table, r, 0, A_HEADS) for _, r in A_PATTERNS]
    c_bias = _band_bias(rel_table, 1, A_HEADS, A_HEADS + C_HEADS)
    w_sum = jnp.asarray(_hgrn_sum_matrix(), dtype=BF16)
    row = lambda v: v.astype(F32).reshape(1, -1)

    h = x.astype(F32).reshape(nb * s, d)
    for l in range(depth):
        h = _mixer_sublayer(h, nb, s, _prep_w_in(w_in[l]), _prep_w_out(w_out[l].astype(F32)), a_biases, c_bias,
                            lower_bounds[l].reshape(1, B_KEY_WIDTH), _pad_heads_vec(hgrn_norm_g[l].astype(F32)),
                            attn_sinks[l].astype(F32), w_sum, row(ln1_g[l]), row(ln1_b[l]))
        h = _moe_sublayer(h, l, router_w[l], router_bias[l], expert_w_gate, expert_w_up, expert_w_down,
                          shared_w_gate[l], shared_w_up[l], shared_w_down[l], row(ln2_g[l]), row(ln2_b[l]))
    return h.reshape(nb, s, d).astype(x.dtype)
```

```python
import functools
import math

import numpy as np
import jax
import jax.numpy as jnp
from jax import lax
from jax.experimental import pallas as pl
from jax.experimental.pallas import tpu as pltpu
from jax.experimental.pallas import tpu_sc as plsc

F32 = jnp.float32
BF16 = jnp.bfloat16
I32 = jnp.int32
U32 = jnp.uint32

LANES = 128
SUBLANES = 8
VMEM_LIMIT = 56 * 1024 * 1024

D_MODEL = 1024
DEPTH = 2
HEAD_DIM = 64
BAND = 128
BAND_Q_TILE = 1024
MASK_VALUE = -1e30

A_HEADS = 6
A_PATTERNS = ((128, 1), (512, 4), (2048, 16))
IN_PROJ_DILATIONS = tuple(r for _, r in A_PATTERNS if r > 1)
IN_PROJ_ROWS = 512
MIX_OUT_ROWS = 512
B_HEADS = 4
B_KEY_DIM = 128
B_VAL_DIM = 96
B_VAL_PAD = 128
HG_CHUNK = 64
HG_STEP_ROWS = 512
C_HEADS = 4
C_KV_HEADS = 2
C_WINDOW = 128

A_WIDTH = A_HEADS * HEAD_DIM
B_KEY_WIDTH = B_HEADS * B_KEY_DIM
B_WIDTH = B_HEADS * B_VAL_DIM
B_PAD_WIDTH = B_HEADS * B_VAL_PAD
C_WIDTH = C_HEADS * HEAD_DIM
C_KV_WIDTH = C_KV_HEADS * HEAD_DIM
IN_SPLITS = (A_WIDTH, A_WIDTH, A_WIDTH, B_KEY_WIDTH, B_KEY_WIDTH, B_WIDTH, B_WIDTH, C_WIDTH, C_KV_WIDTH, C_KV_WIDTH)

A_COLS = 3 * A_WIDTH
B_COLS = 4 * B_KEY_WIDTH
C_COLS = 3 * C_WIDTH

REL_BUCKETS = 32
REL_MAX_DIST = 2048

N_EXPERTS = 256
TOP_K = 8
N_GROUPS = 8
TOPK_GROUPS = 4
EXPERT_FF = 256
SHARED_FF = 256
ROUTED_SCALE = 2.5
ROUTER_TOKENS = 512
DEST_TOKENS = 2048
EXPERT_BLOCK = 512
EXPERT_SLOTS = 4
COMBINE_PARTS = 2
COMBINE_ROWS = 512

DEEPNORM_ALPHA = (2 * DEPTH) ** 0.25
LN_EPS = 1e-5


def _cparams(*sem):
    return pltpu.CompilerParams(dimension_semantics=sem, vmem_limit_bytes=VMEM_LIMIT)


def _prep_w_in(w_in):
    d = w_in.shape[0]
    split_at = [int(i) for i in np.cumsum(IN_SPLITS)[:-1]]
    aq, ak, av, bq, bf, bi, bg, cq, ck, cv = jnp.split(w_in, split_at, axis=-1)
    pad_v = lambda w: jnp.pad(w.reshape(d, B_HEADS, B_VAL_DIM), ((0, 0), (0, 0), (0, B_VAL_PAD - B_VAL_DIM))).reshape(d, B_PAD_WIDTH)
    rep = lambda w: jnp.repeat(w.reshape(d, C_KV_HEADS, HEAD_DIM), C_HEADS // C_KV_HEADS, axis=1).reshape(d, C_WIDTH)
    scale = HEAD_DIM ** -0.5
    cols = [aq * scale, ak, av, bq, bf, pad_v(bi), pad_v(bg), cq * scale, rep(ck), rep(cv)]
    return jnp.concatenate(cols, axis=-1).astype(BF16)


def _prep_w_out(w_out):
    d = w_out.shape[1]
    wa = w_out[:A_WIDTH]
    wb = w_out[A_WIDTH:A_WIDTH + B_WIDTH].reshape(B_HEADS, B_VAL_DIM, d)
    wb = jnp.pad(wb, ((0, 0), (0, B_VAL_PAD - B_VAL_DIM), (0, 0))).reshape(B_PAD_WIDTH, d)
    wc = w_out[A_WIDTH + B_WIDTH:]
    return jnp.concatenate([wa, wb, wc], axis=0).astype(BF16)


def _pad_heads_vec(v):
    return jnp.pad(v.reshape(B_HEADS, B_VAL_DIM), ((0, 0), (0, B_VAL_PAD - B_VAL_DIM))).reshape(1, B_PAD_WIDTH)


def _rel_bucket(dist):
    max_exact = REL_BUCKETS // 2
    d = jnp.maximum(dist, 0)
    log_ratio = jnp.log(jnp.maximum(d, max_exact).astype(F32) / max_exact) / math.log(REL_MAX_DIST / max_exact)
    large = jnp.minimum(max_exact + (log_ratio * (REL_BUCKETS - max_exact)).astype(I32), REL_BUCKETS - 1)
    return jnp.where(d < max_exact, d, large)


def _band_bias(rel_table, r, head_lo, head_hi):
    dist = jnp.arange(BAND)[:, None] + BAND - jnp.arange(2 * BAND)[None, :]
    onehot = jax.nn.one_hot(_rel_bucket(dist * r), REL_BUCKETS, dtype=F32)
    return jnp.einsum("qkb,bh->hqk", onehot, rel_table[:, head_lo:head_hi], precision=lax.Precision.HIGHEST)


def _hgrn_sum_matrix():
    return np.tril(np.ones((HG_CHUNK, HG_CHUNK), np.float32))


HG_LEVELS = int(math.log2(HG_CHUNK))


def _in_proj_body(x_ref, w_ref, *rest):
    n_res = len(IN_PROJ_DILATIONS)
    a_ref, res_refs, (b_ref, c_ref, slabs) = rest[0], rest[1:1 + n_res], rest[1 + n_res:]
    tm = x_ref.shape[0]
    xb = x_ref[...].astype(BF16)
    a = jnp.dot(xb, w_ref[:, :A_COLS], preferred_element_type=F32)
    a_ref[...] = a.astype(BF16)
    for j in range(A_COLS // LANES):
        slabs[0, j] = a[:, j * LANES:(j + 1) * LANES]
    prev_r = 1
    for level, (r, ref) in enumerate(zip(IN_PROJ_DILATIONS, res_refs)):
        step, n, n_prev = r // prev_r, tm // r, tm // prev_r
        src, dst = slabs.at[level % 2], slabs.at[(level + 1) % 2]
        for p in range(r):
            p_prev, q = p % prev_r, p // prev_r
            for j in range(A_COLS // LANES):
                piece = src[j, pl.ds(p_prev * n_prev + q, n, stride=step), :]
                if level + 1 < len(IN_PROJ_DILATIONS):
                    dst[j, p * n:(p + 1) * n, :] = piece
                ref[0, p, :, j * LANES:(j + 1) * LANES] = piece.astype(BF16)
        prev_r = r
    for j in range(B_COLS // B_KEY_WIDTH):
        lo = A_COLS + j * B_KEY_WIDTH
        b_ref[:, j * B_KEY_WIDTH:(j + 1) * B_KEY_WIDTH] = jnp.dot(
            xb, w_ref[:, lo:lo + B_KEY_WIDTH], preferred_element_type=F32)
    c_ref[...] = jnp.dot(xb, w_ref[:, A_COLS + B_COLS:], preferred_element_type=F32).astype(BF16)


def _in_proj(x2d, w_p, nb, s):
    t, d = x2d.shape
    tm = IN_PROJ_ROWS
    n = w_p.shape[1]
    tiles = s // tm
    res_specs = [pl.BlockSpec((1, r, tm // r, A_COLS), lambda i: (i // tiles, 0, i % tiles, 0))
                 for r in IN_PROJ_DILATIONS]
    res_shapes = [jax.ShapeDtypeStruct((nb, r, s // r, A_COLS), BF16) for r in IN_PROJ_DILATIONS]
    outs = pl.pallas_call(
        _in_proj_body,
        grid=(t // tm,),
        in_specs=[pl.BlockSpec((tm, d), lambda i: (i, 0)),
                  pl.BlockSpec((d, n), lambda i: (0, 0))],
        out_specs=[pl.BlockSpec((tm, A_COLS), lambda i: (i, 0))] + res_specs + [
                   pl.BlockSpec((tm, B_COLS), lambda i: (i, 0)),
                   pl.BlockSpec((tm, C_COLS), lambda i: (i, 0))],
        out_shape=[jax.ShapeDtypeStruct((t, A_COLS), BF16)] + res_shapes + [
                   jax.ShapeDtypeStruct((t, B_COLS), F32),
                   jax.ShapeDtypeStruct((t, C_COLS), BF16)],
        scratch_shapes=[pltpu.VMEM((2, A_COLS // LANES, tm, LANES), F32)],
        compiler_params=_cparams("arbitrary"),
        name="in_proj",
    )(x2d, w_p)
    return outs[0], list(outs[1:-2]), outs[-2], outs[-1]


def _band_attn_body(*refs, width, max_dist, has_sink, want_lse):
    q_ref, kp_ref, kc_ref, vp_ref, vc_ref, bias_ref = refs[:6]
    rest = refs[6:]
    if has_sink:
        sink_ref, rest = rest[0], rest[1:]
    o_ref = rest[0]
    lse_ref = rest[1] if want_lse else None

    first_tile = pl.program_id(2) == 0
    row = lax.broadcasted_iota(I32, (BAND, 2 * BAND), 0)
    col = lax.broadcasted_iota(I32, (BAND, 2 * BAND), 1)
    dist = row + BAND - col
    in_band = (dist >= 0) & (dist <= max_dist)
    first_mask = in_band & ((col >= BAND) | jnp.logical_not(first_tile))
    lane = lax.broadcasted_iota(I32, (BAND, LANES), 1)
    low_half = lane < HEAD_DIM

    for qb in range(q_ref.shape[0] // BAND):
        rows = slice(qb * BAND, (qb + 1) * BAND)
        mask = first_mask if qb == 0 else in_band
        lse_tile = jnp.zeros((BAND, LANES), F32)
        for tile in range(width // LANES):
            sl = slice(tile * LANES, (tile + 1) * LANES)
            q2 = q_ref[rows, sl]
            if qb == 0:
                k2 = jnp.concatenate([kp_ref[:, sl], kc_ref[:BAND, sl]], axis=0)
                v2 = jnp.concatenate([vp_ref[:, sl], vc_ref[:BAND, sl]], axis=0)
            else:
                k2 = kc_ref[(qb - 1) * BAND:(qb + 1) * BAND, sl]
                v2 = vc_ref[(qb - 1) * BAND:(qb + 1) * BAND, sl]
            outs, lses = [], []
            for half in range(2):
                h = 2 * tile + half
                qm = jnp.where(low_half if half == 0 else jnp.logical_not(low_half), q2, jnp.zeros_like(q2))
                s = lax.dot_general(qm, k2, (((1,), (1,)), ((), ())), preferred_element_type=F32)
                s = s + bias_ref[h]
                s = jnp.where(mask, s, MASK_VALUE)
                m = jnp.max(s, axis=-1, keepdims=True)
                if has_sink:
                    sink = sink_ref[h]
                    m = jnp.maximum(m, sink)
                p = jnp.exp(s - m)
                den = jnp.sum(p, axis=-1, keepdims=True)
                if has_sink:
                    den = den + jnp.exp(sink - m)
                pv = jnp.dot(p.astype(BF16), v2, preferred_element_type=F32)
                outs.append(pv / den)
                if want_lse:
                    lse_tile = jnp.where(lane == h, m + jnp.log(den), lse_tile)
            o_ref[rows, sl] = jnp.where(low_half, outs[0], outs[1]).astype(o_ref.dtype)
        if want_lse:
            lse_ref[rows, :] = lse_tile


def _band_attn(src, bias, *, width, max_dist, sinks=None, want_lse):
    nb, r, length, _ = src.shape
    qt = min(BAND_Q_TILE, length)
    bands = qt // BAND
    heads = width // HEAD_DIM
    has_sink = sinks is not None

    def cur(off):
        return pl.BlockSpec((None, None, qt, width), lambda b, p, i: (b, p, i, off))

    def prev(off):
        return pl.BlockSpec((None, None, BAND, width), lambda b, p, i: (b, p, jnp.maximum(i * bands - 1, 0), off))

    in_specs = [cur(0), prev(1), cur(1), prev(2), cur(2),
                pl.BlockSpec((heads, BAND, 2 * BAND), lambda b, p, i: (0, 0, 0))]
    args = [src, src, src, src, src, bias]
    if has_sink:
        in_specs.append(pl.BlockSpec(memory_space=pltpu.SMEM))
        args.append(sinks)
    out_spec = pl.BlockSpec((None, None, qt, width), lambda b, p, i: (b, p, i, 0))
    out_sds = jax.ShapeDtypeStruct((nb, r, length, width), F32 if want_lse else BF16)
    lse_spec = pl.BlockSpec((None, None, qt, LANES), lambda b, p, i: (b, p, i, 0))
    lse_sds = jax.ShapeDtypeStruct((nb, r, length, LANES), F32)
    body = functools.partial(_band_attn_body, width=width, max_dist=max_dist, has_sink=has_sink, want_lse=want_lse)
    return pl.pallas_call(
        body,
        grid=(nb, r, length // qt),
        in_specs=in_specs,
        out_specs=[out_spec, lse_spec] if want_lse else out_spec,
        out_shape=[out_sds, lse_sds] if want_lse else out_sds,
        compiler_params=_cparams("arbitrary", "arbitrary", "arbitrary"),
        name="band_attn_r%d_w%d" % (r, width),
    )(*args)


def _hgrn_body(b_ref, lb_ref, ng_ref, w_ref, o_ref, state_ref):
    c = HG_CHUNK
    kd = B_KEY_DIM

    @pl.when(pl.program_id(1) == 0)
    def _():
        state_ref[...] = jnp.zeros_like(state_ref)

    trow = lax.broadcasted_iota(I32, (c, 1), 0)
    odd_row = (trow % 2) == 1
    low_sub = lax.broadcasted_iota(I32, (SUBLANES, 1), 0) < SUBLANES // 2
    ti = lax.broadcasted_iota(I32, (c, c), 0)
    si = lax.broadcasted_iota(I32, (c, c), 1)
    nt = (((1,), (1,)), ((), ()))
    tri = w_ref[...]

    in_level = [((ti // m) == (si // m)) & ((ti % m) >= (m // 2)) & ((si % m) < (m // 2))
                for m in (c >> lvl for lvl in range(HG_LEVELS))]

    states = [state_ref[h] for h in range(B_HEADS)]
    for chunk, h in [(ci, hi) for ci in range(b_ref.shape[1] // c) for hi in range(B_HEADS)]:
        rows = slice(chunk * c, (chunk + 1) * c)
        ks = slice(h * kd, (h + 1) * kd)
        q = b_ref[0, rows, ks]
        f = b_ref[0, rows, B_KEY_WIDTH + h * kd:B_KEY_WIDTH + (h + 1) * kd]
        inp = b_ref[0, rows, 2 * B_KEY_WIDTH + h * kd:2 * B_KEY_WIDTH + (h + 1) * kd]
        gate = b_ref[0, rows, 3 * B_KEY_WIDTH + h * kd:3 * B_KEY_WIDTH + (h + 1) * kd]
        lb = lb_ref[:, ks]

        a = jnp.exp(-jnp.abs(f))
        big = 1.0 / (1.0 + a)
        small = a * big
        pos = f >= 0.0
        forget = lb + (1.0 - lb) * jnp.where(pos, big, small)
        log_f = jnp.log(forget)
        key = (1.0 - lb) * jnp.where(pos, small, big)
        qs = q * jax.nn.sigmoid(q)

        g_hi = log_f.astype(BF16)
        g_lo = (log_f - g_hi.astype(F32)).astype(BF16)
        cum2 = jnp.dot(tri, jnp.concatenate([g_hi, g_lo], axis=1), preferred_element_type=F32)
        cum = cum2[:, :kd] + cum2[:, kd:]
        last = cum[c - 1:c]

        q_dec = (qs * jnp.exp(cum)).astype(BF16)
        k_dec = (key * jnp.exp(last - cum)).astype(BF16)
        inp_b = inp.astype(BF16)

        def level_decay(m):
            if m == 2:
                return jnp.where(odd_row, forget, 1.0)
            if m == SUBLANES // 2:
                row_bcast = lambda i: jnp.broadcast_to(cum[i:i + 1], (SUBLANES, kd))
                pieces = [jnp.where(low_sub, row_bcast(SUBLANES * j + m // 2 - 1), row_bcast(SUBLANES * j + m + m // 2 - 1))
                          for j in range(c // SUBLANES)]
            else:
                pieces = [jnp.broadcast_to(cum[b * m + m // 2 - 1:b * m + m // 2], (m, kd)) for b in range(c // m)]
            return jnp.exp(-jnp.abs(cum - jnp.concatenate(pieces, axis=0)))

        st = states[h]
        inter = lax.dot_general(q_dec, st.astype(BF16), nt, preferred_element_type=F32)

        scores = jnp.where(ti == si,
                           lax.dot_general(qs.astype(BF16), key.astype(BF16), nt, preferred_element_type=F32), 0.0)
        m = c
        for lvl in range(HG_LEVELS):
            el = level_decay(m)
            sl = lax.dot_general((qs * el).astype(BF16), (key * el).astype(BF16), nt, preferred_element_type=F32)
            scores = jnp.where(in_level[lvl], sl, scores)
            m //= 2
        intra = jnp.dot(scores.astype(BF16), inp_b, preferred_element_type=F32)

        new_st = st * jnp.exp(last) + lax.dot_general(
            inp_b, k_dec, (((0,), (0,)), ((), ())), preferred_element_type=F32)
        states[h] = new_st

        o = inter + intra
        ms = jnp.sum(o * o, axis=-1, keepdims=True) * (1.0 / B_VAL_DIM)
        o = o * lax.rsqrt(ms + 1e-6)
        o_ref[0, rows, ks] = (o * ng_ref[:, ks] * (gate * jax.nn.sigmoid(gate))).astype(o_ref.dtype)

    for h in range(B_HEADS):
        state_ref[h] = states[h]


def _hgrn(b_all, lower_bound, norm_g_pad, w_sum):
    nb, s, _ = b_all.shape
    c = HG_STEP_ROWS
    return pl.pallas_call(
        _hgrn_body,
        grid=(nb, s // c),
        in_specs=[pl.BlockSpec((1, c, B_COLS), lambda b, i: (b, i, 0)),
                  pl.BlockSpec((1, B_KEY_WIDTH), lambda b, i: (0, 0)),
                  pl.BlockSpec((1, B_PAD_WIDTH), lambda b, i: (0, 0)),
                  pl.BlockSpec(w_sum.shape, lambda b, i: (0, 0))],
        out_specs=pl.BlockSpec((1, c, B_PAD_WIDTH), lambda b, i: (b, i, 0)),
        out_shape=jax.ShapeDtypeStruct((nb, s, B_PAD_WIDTH), BF16),
        scratch_shapes=[pltpu.VMEM((B_HEADS, B_VAL_PAD, B_KEY_DIM), F32)],
        compiler_params=_cparams("arbitrary", "arbitrary"),
        name="hgrn2",
    )(b_all, lower_bound, norm_g_pad, w_sum)


def _layer_norm_rows(z, g, b):
    mu = jnp.mean(z, axis=-1, keepdims=True)
    zc = z - mu
    var = jnp.mean(zc * zc, axis=-1, keepdims=True)
    return zc * lax.rsqrt(var + LN_EPS) * g + b


def _mix_out_body(*refs):
    n_pat = len(A_PATTERNS)
    o_refs, l_refs = refs[:n_pat], refs[n_pat:2 * n_pat]
    ob, oc, x_ref, w_ref, g_ref, b_ref, spread_ref, out_ref = refs[2 * n_pat:2 * n_pat + 8]
    scratch = refs[2 * n_pat + 8:]
    tm = x_ref.shape[0]
    n_slabs = A_WIDTH // LANES

    def token_order(ref, r, scr):
        slabs = ref.shape[-1] // LANES
        if r == 1:
            return [ref[0, 0, :, j * LANES:(j + 1) * LANES] for j in range(slabs)]
        for p in range(r):
            for j in range(slabs):
                scr[j, pl.ds(p, tm // r, stride=r), :] = ref[0, p, :, j * LANES:(j + 1) * LANES]
        return [scr[j] for j in range(slabs)]

    scr_iter = iter(scratch)
    dil = [r for _, r in A_PATTERNS]
    o_slabs = [token_order(ref, r, None if r == 1 else next(scr_iter)) for ref, r in zip(o_refs, dil)]
    ls = [token_order(ref, r, None if r == 1 else next(scr_iter))[0] for ref, r in zip(l_refs, dil)]

    m = functools.reduce(jnp.maximum, ls)
    ws = [jnp.exp(l - m) for l in ls]
    inv = 1.0 / functools.reduce(lambda a, b: a + b, ws)
    spread = spread_ref[...]
    wide = []
    for w in ws:
        wn = w * inv
        hi = wn.astype(BF16)
        lo = (wn - hi.astype(F32)).astype(BF16)
        wide.append(jnp.dot(hi, spread, preferred_element_type=F32) + jnp.dot(lo, spread, preferred_element_type=F32))
    merged = [functools.reduce(lambda a, b: a + b,
                               [w[:, j * LANES:(j + 1) * LANES] * o[j] for w, o in zip(wide, o_slabs)])
              for j in range(n_slabs)]
    cat = jnp.concatenate([m.astype(BF16) for m in merged] + [ob[...], oc[...]], axis=1)
    y = jnp.dot(cat, w_ref[...], preferred_element_type=F32)
    z = DEEPNORM_ALPHA * x_ref[...] + y
    out_ref[...] = _layer_norm_rows(z, g_ref[...], b_ref[...])


def _mix_out(o_list, l_list, ob, oc, x2d, w_out_p, ln_g, ln_b):
    t, d = x2d.shape
    tm = MIX_OUT_ROWS
    tiles = o_list[0].shape[1] * o_list[0].shape[2] // tm
    row = lambda w: pl.BlockSpec((tm, w), lambda i: (i, 0))
    full = lambda a: pl.BlockSpec(a.shape, lambda i: (0, 0))
    res = lambda a: pl.BlockSpec((1, a.shape[1], tm // a.shape[1], a.shape[3]), lambda i: (i // tiles, 0, i % tiles, 0))
    spread = jnp.asarray(np.arange(LANES)[:, None] == np.arange(A_WIDTH)[None, :] // HEAD_DIM, dtype=BF16)
    dilated = [a for a in o_list + l_list if a.shape[1] > 1]
    return pl.pallas_call(
        _mix_out_body,
        grid=(t // tm,),
        in_specs=[res(a) for a in o_list + l_list] + [row(B_PAD_WIDTH), row(C_WIDTH), row(d), full(w_out_p),
                                                      full(ln_g), full(ln_b), full(spread)],
        out_specs=row(d),
        out_shape=jax.ShapeDtypeStruct((t, d), F32),
        scratch_shapes=[pltpu.VMEM((a.shape[3] // LANES, tm, LANES), F32) for a in dilated],
        compiler_params=_cparams("arbitrary"),
        name="mix_out_ln",
    )(*o_list, *l_list, ob, oc, x2d, w_out_p, ln_g, ln_b, spread)


def _router_body(h_ref, rw_ref, bias_ref, tri_ref, e_ref, rank_ref, gate_ref, cnt_ref, hp_ref, carry_ref):
    tn = h_ref.shape[0]
    per_group = N_EXPERTS // N_GROUPS
    neg_inf = -jnp.inf

    @pl.when(pl.program_id(0) == 0)
    def _():
        carry_ref[...] = jnp.zeros_like(carry_ref)

    hp_ref[...] = _pack_bf16_pairs(h_ref[...])

    h = h_ref[...]
    h_hi = h.astype(BF16)
    h_lo = (h - h_hi.astype(F32)).astype(BF16)
    nt = (((1,), (1,)), ((), ()))
    w_hi, w_lo = rw_ref[0], rw_ref[1]
    logits = (lax.dot_general(w_hi, h_hi, nt, preferred_element_type=F32)
              + lax.dot_general(w_hi, h_lo, nt, preferred_element_type=F32)
              + lax.dot_general(w_lo, h_hi, nt, preferred_element_type=F32))
    scores = jax.nn.sigmoid(logits)
    choice = scores + bias_ref[...]

    def first_max(vals, idx, sentinel):
        top = jnp.max(vals, axis=0, keepdims=True)
        return top, jnp.min(jnp.where(vals == top, idx, sentinel), axis=0, keepdims=True)

    li = lax.broadcasted_iota(I32, (per_group, tn), 0).astype(F32)
    group_rows = []
    for g in range(N_GROUPS):
        cg = choice[g * per_group:(g + 1) * per_group]
        m1, first = first_max(cg, li, float(per_group))
        m2 = jnp.max(jnp.where(li == first, neg_inf, cg), axis=0, keepdims=True)
        group_rows.append(m1 + m2)
    group_score = jnp.concatenate(group_rows, axis=0)

    gi = lax.broadcasted_iota(I32, (N_GROUPS, tn), 0).astype(F32)
    group_ok = jnp.zeros((N_GROUPS, tn), F32)
    cur = group_score
    for _ in range(TOPK_GROUPS):
        _, first = first_max(cur, gi, float(N_GROUPS))
        pick = gi == first
        group_ok = jnp.where(pick, 1.0, group_ok)
        cur = jnp.where(pick, neg_inf, cur)

    cur = jnp.concatenate(
        [jnp.where(group_ok[g:g + 1] > 0.0, choice[g * per_group:(g + 1) * per_group], MASK_VALUE)
         for g in range(N_GROUPS)], axis=0)
    ei = lax.broadcasted_iota(I32, (N_EXPERTS, tn), 0).astype(F32)
    chosen = jnp.zeros((N_EXPERTS, tn), F32)
    picks, gates = [], []
    for _ in range(TOP_K):
        _, idx = first_max(cur, ei, float(N_EXPERTS))
        pick = ei == idx
        picks.append(idx)
        gates.append(jnp.sum(jnp.where(pick, scores, 0.0), axis=0, keepdims=True))
        chosen = jnp.where(pick, 1.0, chosen)
        cur = jnp.where(pick, neg_inf, cur)

    gate = jnp.concatenate(gates, axis=0)
    gate_ref[...] = gate / jnp.sum(gate, axis=0, keepdims=True) * ROUTED_SCALE
    e_ref[...] = jnp.concatenate(picks, axis=0).astype(I32)

    before = jnp.dot(chosen.astype(BF16), tri_ref[...], preferred_element_type=F32) + carry_ref[...]
    ranks = [jnp.sum(jnp.where(ei == idx, before, 0.0), axis=0, keepdims=True) for idx in picks]
    rank_ref[...] = jnp.concatenate(ranks, axis=0).astype(I32)
    carry = carry_ref[...] + jnp.sum(chosen, axis=1, keepdims=True)
    carry_ref[...] = carry
    cnt_ref[...] = carry.astype(I32)


def _router(h2d, rw_t, bias_col):
    t, d = h2d.shape
    tn = ROUTER_TOKENS
    tri = jnp.asarray(np.triu(np.ones((tn, tn), np.float32), k=1), dtype=BF16)
    rw_hi = rw_t.astype(BF16)
    rw_split = jnp.stack([rw_hi, (rw_t - rw_hi.astype(F32)).astype(BF16)])
    tok = lambda: pl.BlockSpec((TOP_K, tn), lambda i: (0, i))
    return pl.pallas_call(
        _router_body,
        grid=(t // tn,),
        in_specs=[pl.BlockSpec((tn, d), lambda i: (i, 0)),
                  pl.BlockSpec((2, N_EXPERTS, d), lambda i: (0, 0, 0)),
                  pl.BlockSpec((N_EXPERTS, 1), lambda i: (0, 0)),
                  pl.BlockSpec((tn, tn), lambda i: (0, 0))],
        out_specs=[tok(), tok(), tok(), pl.BlockSpec((N_EXPERTS, 1), lambda i: (0, 0)),
                   pl.BlockSpec((tn, d // 2), lambda i: (i, 0))],
        out_shape=[jax.ShapeDtypeStruct((TOP_K, t), I32),
                   jax.ShapeDtypeStruct((TOP_K, t), I32),
                   jax.ShapeDtypeStruct((TOP_K, t), F32),
                   jax.ShapeDtypeStruct((N_EXPERTS, 1), I32),
                   jax.ShapeDtypeStruct((t, d // 2), U32)],
        scratch_shapes=[pltpu.VMEM((N_EXPERTS, 1), F32)],
        compiler_params=_cparams("arbitrary"),
        name="moe_router",
    )(h2d, rw_split, bias_col, tri)


def _pack_bf16_pairs(x):
    w = x.shape[1] // 2
    hi = lax.bitcast_convert_type(x[:, :w].astype(BF16).astype(F32), U32)
    lo = lax.bitcast_convert_type(x[:, w:].astype(BF16).astype(F32), U32)
    return hi | (lo >> 16)


def _unpack_bf16_pairs(p):
    hi = lax.bitcast_convert_type(p & jnp.uint32(0xFFFF0000), F32)
    lo = lax.bitcast_convert_type(p << 16, F32)
    return hi, lo


def _dest_body(e_ref, rank_ref, offs_ref, dest_ref):
    tn = e_ref.shape[1]
    ei = lax.broadcasted_iota(I32, (N_EXPERTS, tn), 0)
    offs = offs_ref[...]
    rows = [jnp.sum(jnp.where(ei == e_ref[k:k + 1, :], offs, 0.0), axis=0, keepdims=True) for k in range(TOP_K)]
    dest_ref[...] = jnp.concatenate(rows, axis=0).astype(I32) + rank_ref[...]


def _dest_rows(e_t, rank_t, offsets):
    t = e_t.shape[1]
    tn = DEST_TOKENS
    tok = pl.BlockSpec((TOP_K, tn), lambda i: (0, i))
    return pl.pallas_call(
        _dest_body,
        grid=(t // tn,),
        in_specs=[tok, tok, pl.BlockSpec((N_EXPERTS, 1), lambda i: (0, 0))],
        out_specs=tok,
        out_shape=jax.ShapeDtypeStruct((TOP_K, t), I32),
        compiler_params=_cparams("arbitrary"),
        name="moe_dest",
    )(e_t, rank_t, offsets.astype(F32).reshape(N_EXPERTS, 1))


def _expert_body(offs_ref, nblk_ref, cnt_ref, layer_ref, xs_hbm, wg_ref, wu_ref, wd_ref, ys_hbm,
                 xbuf, ybuf, wg_b, wu_b, wd_b, sem_in, sem_out):
    e = pl.program_id(0)
    slots, bm = xbuf.shape[:2]
    ahead = slots - 1
    n = nblk_ref[e]
    first = offs_ref[e] // bm
    total = (offs_ref[N_EXPERTS - 1] // bm) + nblk_ref[N_EXPERTS - 1]

    def rows(g):
        return pl.ds(pl.multiple_of(g * bm, bm), bm)

    def x_copy(g, slot):
        return pltpu.make_async_copy(xs_hbm.at[rows(g)], xbuf.at[slot], sem_in.at[slot])

    def y_copy(g, slot):
        return pltpu.make_async_copy(ybuf.at[slot], ys_hbm.at[rows(g)], sem_out.at[slot])

    @pl.when(e == 0)
    def _():
        for g0 in range(ahead):
            @pl.when(g0 < total)
            def _():
                x_copy(g0, g0).start()

    wg_b[...] = wg_ref[0, 0].astype(BF16)
    wu_b[...] = wu_ref[0, 0].astype(BF16)
    wd_b[...] = wd_ref[0, 0].astype(BF16)

    def block(j, carry):
        g = first + j
        slot = g % slots
        x_copy(g, slot).wait()

        @pl.when(g + ahead < total)
        def _():
            x_copy(g + ahead, (g + ahead) % slots).start()

        @pl.when(g >= slots)
        def _():
            y_copy(g - slots, slot).wait()

        live = lax.broadcasted_iota(I32, (bm, 1), 0) < (cnt_ref[e] - j * bm)
        hi, lo = _unpack_bf16_pairs(jnp.where(live, xbuf[slot], jnp.uint32(0)))
        xb = jnp.concatenate([hi.astype(BF16), lo.astype(BF16)], axis=1)
        gate = jnp.dot(xb, wg_b[...], preferred_element_type=F32)
        up = jnp.dot(xb, wu_b[...], preferred_element_type=F32)
        hidden = (gate * jax.nn.sigmoid(gate) * up).astype(BF16)
        ybuf[slot] = _pack_bf16_pairs(jnp.dot(hidden, wd_b[...], preferred_element_type=F32))
        y_copy(g, slot).start()
        return carry

    lax.fori_loop(0, n, block, 0)

    @pl.when(e == N_EXPERTS - 1)
    def _():
        for back in range(1, slots + 1):
            @pl.when(total >= back)
            def _():
                y_copy(total - back, (total - back) % slots).wait()


def _experts(xs, offsets, n_blk, counts, layer, w_gate, w_up, w_down):
    n_rows, dp = xs.shape
    bm = EXPERT_BLOCK
    d, ff = w_gate.shape[-2:]
    w_spec = lambda a, b: pl.BlockSpec((1, 1, a, b), lambda e, of, nb, ct, ly: (ly[0], e, 0, 0))
    grid_spec = pltpu.PrefetchScalarGridSpec(
        num_scalar_prefetch=4,
        grid=(N_EXPERTS,),
        in_specs=[pl.BlockSpec(memory_space=pl.ANY), w_spec(d, ff), w_spec(d, ff), w_spec(ff, d)],
        out_specs=pl.BlockSpec(memory_space=pl.ANY),
        scratch_shapes=[pltpu.VMEM((EXPERT_SLOTS, bm, dp), U32), pltpu.VMEM((EXPERT_SLOTS, bm, dp), U32),
                        pltpu.VMEM((d, ff), BF16), pltpu.VMEM((d, ff), BF16), pltpu.VMEM((ff, d), BF16),
                        pltpu.SemaphoreType.DMA((EXPERT_SLOTS,)), pltpu.SemaphoreType.DMA((EXPERT_SLOTS,))],
    )
    return pl.pallas_call(
        _expert_body,
        grid_spec=grid_spec,
        out_shape=jax.ShapeDtypeStruct((n_rows, dp), U32),
        compiler_params=_cparams("arbitrary"),
        name="moe_experts",
    )(offsets, n_blk, counts, layer, xs, w_gate, w_up, w_down)


SC_GATHER_ROWS = 64


def _sc_gather_rows(table, idx):
    info = plsc.get_sparse_core_info()
    nc, ns = info.num_cores, info.num_subcores
    workers = nc * ns
    n = idx.shape[0]
    w = table.shape[1]
    ch = SC_GATHER_ROWS
    per_worker = n // workers
    steps = per_worker // ch
    assert per_worker * workers == n and steps * ch == per_worker and steps % 2 == 0
    mesh = plsc.VectorSubcoreMesh(core_axis_name="c", subcore_axis_name="s")

    @functools.partial(
        pl.kernel, mesh=mesh,
        out_type=jax.ShapeDtypeStruct((n, w), table.dtype),
        scratch_types=[pltpu.VMEM((steps, ch), I32), pltpu.VMEM((2, ch, w), table.dtype),
                       pltpu.SemaphoreType.DMA((2,)), pltpu.SemaphoreType.DMA((2,))],
    )
    def gather_kernel(table_hbm, idx_hbm, out_hbm, idx_v, rows_v, gsem, wsem):
        wid = lax.axis_index("s") * nc + lax.axis_index("c")
        base = wid * per_worker
        pltpu.sync_copy(idx_hbm.at[wid], idx_v)

        def gather(i, slot):
            return pltpu.make_async_copy(table_hbm.at[idx_v.at[i]], rows_v.at[slot], gsem.at[slot])

        def write(i, slot):
            off = pl.multiple_of(base + i * ch, ch)
            return pltpu.make_async_copy(rows_v.at[slot], out_hbm.at[pl.ds(off, ch)], wsem.at[slot])

        gather(0, 0).start()

        @pl.loop(0, steps, step=2)
        def _(i):
            gather(i + 1, 1).start()
            gather(i, 0).wait()
            write(i, 0).start()
            write(i, 0).wait()

            @pl.when(i + 2 < steps)
            def _():
                gather(i + 2, 0).start()

            gather(i + 1, 1).wait()
            write(i + 1, 1).start()
            write(i + 1, 1).wait()

    return gather_kernel(table, idx.reshape(workers, steps, ch))


def _sc_scatter_rows(rows, idx, n_out):
    info = plsc.get_sparse_core_info()
    nc, ns = info.num_cores, info.num_subcores
    workers = nc * ns
    kk, t = idx.shape
    w = rows.shape[1]
    ch = SC_GATHER_ROWS
    per_worker = t // workers
    steps = per_worker // ch
    assert per_worker * workers == t and steps * ch == per_worker and steps % 2 == 0
    idx_w = idx.reshape(kk, workers, steps, ch).transpose(1, 0, 2, 3).reshape(workers, kk * steps, ch)
    mesh = plsc.VectorSubcoreMesh(core_axis_name="c", subcore_axis_name="s")

    @functools.partial(
        pl.kernel, mesh=mesh,
        out_type=jax.ShapeDtypeStruct((n_out, w), rows.dtype),
        scratch_types=[pltpu.VMEM((kk * steps, ch), I32), pltpu.VMEM((2, ch, w), rows.dtype),
                       pltpu.SemaphoreType.DMA((2,)), pltpu.SemaphoreType.DMA((2,))],
    )
    def scatter_kernel(rows_hbm, idx_hbm, out_hbm, idx_v, rows_v, rsem, ssem):
        wid = lax.axis_index("s") * nc + lax.axis_index("c")
        base = wid * per_worker
        pltpu.sync_copy(idx_hbm.at[wid], idx_v)

        def read(i, slot):
            off = pl.multiple_of(base + i * ch, ch)
            return pltpu.make_async_copy(rows_hbm.at[pl.ds(off, ch)], rows_v.at[slot], rsem.at[slot])

        def scatter(i, k, slot):
            return pltpu.make_async_copy(rows_v.at[slot], out_hbm.at[idx_v.at[k * steps + i]], ssem.at[slot])

        def scatter_all(i, slot):
            for k in range(kk):
                scatter(i, k, slot).start()
            for k in range(kk):
                scatter(i, k, slot).wait()

        read(0, 0).start()

        @pl.loop(0, steps, step=2)
        def _(i):
            read(i + 1, 1).start()
            read(i, 0).wait()
            scatter_all(i, 0)

            @pl.when(i + 2 < steps)
            def _():
                read(i + 2, 0).start()

            read(i + 1, 1).wait()
            scatter_all(i + 1, 1)

    return scatter_kernel(rows, idx_w)


def _combine_streamed_body(y_ref, gate_ref, h_ref, sg_ref, su_ref, sd_ref, g_ref, b_ref, *rest):
    out_ref = rest[-1]
    half = h_ref.shape[1] // 2
    h = h_ref[...]
    hb = h.astype(BF16)
    sg = jnp.dot(hb, sg_ref[...], preferred_element_type=F32)
    su = jnp.dot(hb, su_ref[...], preferred_element_type=F32)
    shared = jnp.dot((sg * jax.nn.sigmoid(sg) * su).astype(BF16), sd_ref[...], preferred_element_type=F32)
    z = DEEPNORM_ALPHA * h + shared
    z_hi, z_lo = z[:, :half], z[:, half:]
    for k in range(TOP_K):
        y_hi, y_lo = _unpack_bf16_pairs(y_ref[k])
        gate = gate_ref[:, k:k + 1]
        z_hi = z_hi + gate * y_hi
        z_lo = z_lo + gate * y_lo
    out_ref[...] = _layer_norm_rows(jnp.concatenate([z_hi, z_lo], axis=1), g_ref[...], b_ref[...])


def _combine_streamed(y_part, gate_tk, h2d, sh_gate, sh_up, sh_down, ln_g, ln_b, part, partial_out):
    t, d = h2d.shape
    tt = COMBINE_ROWS
    steps = y_part.shape[1] // tt
    first = part * steps
    full = lambda a: pl.BlockSpec(a.shape, lambda i: (0, 0))
    in_specs = [pl.BlockSpec((TOP_K, tt, d // 2), lambda i: (0, i, 0)),
                pl.BlockSpec((tt, TOP_K), lambda i: (first + i, 0)),
                pl.BlockSpec((tt, d), lambda i: (first + i, 0)),
                full(sh_gate), full(sh_up), full(sh_down), full(ln_g), full(ln_b)]
    args = [y_part, gate_tk, h2d, sh_gate, sh_up, sh_down, ln_g, ln_b]
    aliases = {}
    if partial_out is not None:
        in_specs.append(pl.BlockSpec(memory_space=pl.ANY))
        args.append(partial_out)
        aliases = {len(args) - 1: 0}
    return pl.pallas_call(
        _combine_streamed_body,
        grid=(steps,),
        in_specs=in_specs,
        out_specs=pl.BlockSpec((tt, d), lambda i: (first + i, 0)),
        out_shape=jax.ShapeDtypeStruct((t, d), F32),
        input_output_aliases=aliases,
        compiler_params=_cparams("arbitrary"),
        name="moe_combine_streamed_ln",
    )(*args)


def _mixer_sublayer(h2d, nb, s, w_in_p, w_out_p, a_biases, c_bias, lower_bound, norm_g_pad, sinks, w_sum, ln_g, ln_b):
    a_qkv, a_residue, b_all, c_qkv = _in_proj(h2d, w_in_p, nb, s)
    a_by_dilation = {1: a_qkv.reshape(nb, 1, s, A_COLS), **dict(zip(IN_PROJ_DILATIONS, a_residue))}
    o_list, l_list = [], []
    for (window, r), bias in zip(A_PATTERNS, a_biases):
        o, lse = _band_attn(a_by_dilation[r], bias, width=A_WIDTH, max_dist=window // r, want_lse=True)
        o_list.append(o)
        l_list.append(lse)
    oc = _band_attn(c_qkv.reshape(nb, 1, s, C_COLS), c_bias, width=C_WIDTH,
                    max_dist=C_WINDOW - 1, sinks=sinks, want_lse=False).reshape(nb * s, C_WIDTH)
    ob = _hgrn(b_all.reshape(nb, s, B_COLS), lower_bound, norm_g_pad, w_sum).reshape(nb * s, B_PAD_WIDTH)
    return _mix_out(o_list, l_list, ob, oc, h2d, w_out_p, ln_g, ln_b)


def _moe_sublayer(h2d, layer, router_w, router_bias, w_gate, w_up, w_down, sh_gate, sh_up, sh_down, ln_g, ln_b):
    t, d = h2d.shape
    bm = EXPERT_BLOCK
    e_t, rank_t, gate_t, counts, h_packed = _router(h2d, router_w.T.astype(F32),
                                                    router_bias.astype(F32).reshape(N_EXPERTS, 1))

    counts = counts.reshape(N_EXPERTS)
    padded = (counts + bm - 1) // bm * bm
    pad_end = jnp.cumsum(padded).astype(I32)
    offsets = pad_end - padded
    n_blocks = -(-(t * TOP_K + N_EXPERTS * (bm - 1)) // bm)

    dest_t = _dest_rows(e_t, rank_t, offsets)
    xs = _sc_scatter_rows(h_packed, dest_t, n_blocks * bm)
    ys = _experts(xs, offsets, (padded // bm).astype(I32), counts, jnp.full((1,), layer, I32), w_gate, w_up, w_down)
    tp = t // COMBINE_PARTS
    gate_tk = gate_t.T
    shared_w = (sh_gate.astype(BF16), sh_up.astype(BF16), sh_down.astype(BF16))
    out = None
    for part in range(COMBINE_PARTS):
        idx = dest_t[:, part * tp:(part + 1) * tp].reshape(TOP_K * tp)
        y_part = _sc_gather_rows(ys, idx).reshape(TOP_K, tp, d // 2)
        out = _combine_streamed(y_part, gate_tk, h2d, *shared_w, ln_g, ln_b, part, out)
    return out


def kernel(x, w_in, w_out, rel_bias_table, lower_bound_logits, hgrn_norm_g, attn_sinks, ln1_g, ln1_b, router_w, router_bias, expert_w_gate, expert_w_up, expert_w_down, shared_w_gate, shared_w_up, shared_w_down, ln2_g, ln2_b):
    nb, s, d = x.shape
    depth = w_in.shape[0]
    lb_probs = jax.nn.softmax(lower_bound_logits.astype(F32), axis=0)
    lower_bounds = jnp.cumsum(lb_probs, axis=0) - lb_probs[0]
    rel_table = rel_bias_table.astype(F32)
    a_biases = [_band_bias(rel_table, r, 0, A_HEADS) for _, r in A_PATTERNS]
    c_bias = _band_bias(rel_table, 1, A_HEADS, A_HEADS + C_HEADS)
    w_sum = jnp.asarray(_hgrn_sum_matrix(), dtype=BF16)
    row = lambda v: v.astype(F32).reshape(1, -1)

    h = x.astype(F32).reshape(nb * s, d)
    for l in range(depth):
        h = _mixer_sublayer(h, nb, s, _prep_w_in(w_in[l]), _prep_w_out(w_out[l].astype(F32)), a_biases, c_bias,
                            lower_bounds[l].reshape(1, B_KEY_WIDTH), _pad_heads_vec(hgrn_norm_g[l].astype(F32)),
                            attn_sinks[l].astype(F32), w_sum, row(ln1_g[l]), row(ln1_b[l]))
        h = _moe_sublayer(h, l, router_w[l], router_bias[l], expert_w_gate, expert_w_up, expert_w_down,
                          shared_w_gate[l], shared_w_up[l], shared_w_down[l], row(ln2_g[l]), row(ln2_b[l]))
    return h.reshape(nb, s, d).astype(x.dtype)
```

```python
import functools
import math

import numpy as np
import jax
import jax.numpy as jnp
from jax import lax
from jax.experimental import pallas as pl
from jax.experimental.pallas import tpu as pltpu
from jax.experimental.pallas import tpu_sc as plsc

F32 = jnp.float32
BF16 = jnp.bfloat16
I32 = jnp.int32
U32 = jnp.uint32

LANES = 128
SUBLANES = 8
VMEM_LIMIT = 56 * 1024 * 1024

D_MODEL = 1024
DEPTH = 2
HEAD_DIM = 64
BAND = 128
BAND_Q_TILE = 2048
MASK_VALUE = -1e30

A_HEADS = 6
A_PATTERNS = ((128, 1), (512, 4), (2048, 16))
IN_PROJ_DILATIONS = tuple(r for _, r in A_PATTERNS if r > 1)
IN_PROJ_ROWS = 512
MIX_OUT_ROWS = 512
B_HEADS = 4
B_KEY_DIM = 128
B_VAL_DIM = 96
B_VAL_PAD = 128
HG_CHUNK = 64
HG_STEP_ROWS = 512
C_HEADS = 4
C_KV_HEADS = 2
C_WINDOW = 128

A_WIDTH = A_HEADS * HEAD_DIM
B_KEY_WIDTH = B_HEADS * B_KEY_DIM
B_WIDTH = B_HEADS * B_VAL_DIM
B_PAD_WIDTH = B_HEADS * B_VAL_PAD
C_WIDTH = C_HEADS * HEAD_DIM
C_KV_WIDTH = C_KV_HEADS * HEAD_DIM
IN_SPLITS = (A_WIDTH, A_WIDTH, A_WIDTH, B_KEY_WIDTH, B_KEY_WIDTH, B_WIDTH, B_WIDTH, C_WIDTH, C_KV_WIDTH, C_KV_WIDTH)

A_COLS = 3 * A_WIDTH
B_COLS = 4 * B_KEY_WIDTH
C_COLS = 3 * C_WIDTH

REL_BUCKETS = 32
REL_MAX_DIST = 2048

N_EXPERTS = 256
TOP_K = 8
N_GROUPS = 8
TOPK_GROUPS = 4
EXPERT_FF = 256
SHARED_FF = 256
ROUTED_SCALE = 2.5
ROUTER_TOKENS = 512
DEST_TOKENS = 2048
EXPERT_BLOCK = 512
EXPERT_SLOTS = 4
COMBINE_PARTS = 2
COMBINE_ROWS = 512

DEEPNORM_ALPHA = (2 * DEPTH) ** 0.25
LN_EPS = 1e-5


def _cparams(*sem):
    return pltpu.CompilerParams(dimension_semantics=sem, vmem_limit_bytes=VMEM_LIMIT)


def _prep_w_in(w_in):
    d = w_in.shape[0]
    split_at = [int(i) for i in np.cumsum(IN_SPLITS)[:-1]]
    aq, ak, av, bq, bf, bi, bg, cq, ck, cv = jnp.split(w_in, split_at, axis=-1)
    pad_v = lambda w: jnp.pad(w.reshape(d, B_HEADS, B_VAL_DIM), ((0, 0), (0, 0), (0, B_VAL_PAD - B_VAL_DIM))).reshape(d, B_PAD_WIDTH)
    rep = lambda w: jnp.repeat(w.reshape(d, C_KV_HEADS, HEAD_DIM), C_HEADS // C_KV_HEADS, axis=1).reshape(d, C_WIDTH)
    scale = HEAD_DIM ** -0.5
    cols = [aq * scale, ak, av, bq, bf, pad_v(bi), pad_v(bg), cq * scale, rep(ck), rep(cv)]
    return jnp.concatenate(cols, axis=-1).astype(BF16)


def _prep_w_out(w_out):
    d = w_out.shape[1]
    wa = w_out[:A_WIDTH]
    wb = w_out[A_WIDTH:A_WIDTH + B_WIDTH].reshape(B_HEADS, B_VAL_DIM, d)
    wb = jnp.pad(wb, ((0, 0), (0, B_VAL_PAD - B_VAL_DIM), (0, 0))).reshape(B_PAD_WIDTH, d)
    wc = w_out[A_WIDTH + B_WIDTH:]
    return jnp.concatenate([wa, wb, wc], axis=0).astype(BF16)


def _pad_heads_vec(v):
    return jnp.pad(v.reshape(B_HEADS, B_VAL_DIM), ((0, 0), (0, B_VAL_PAD - B_VAL_DIM))).reshape(1, B_PAD_WIDTH)


def _rel_bucket(dist):
    max_exact = REL_BUCKETS // 2
    d = jnp.maximum(dist, 0)
    log_ratio = jnp.log(jnp.maximum(d, max_exact).astype(F32) / max_exact) / math.log(REL_MAX_DIST / max_exact)
    large = jnp.minimum(max_exact + (log_ratio * (REL_BUCKETS - max_exact)).astype(I32), REL_BUCKETS - 1)
    return jnp.where(d < max_exact, d, large)


def _band_bias(rel_table, r, head_lo, head_hi):
    dist = jnp.arange(BAND)[:, None] + BAND - jnp.arange(2 * BAND)[None, :]
    onehot = jax.nn.one_hot(_rel_bucket(dist * r), REL_BUCKETS, dtype=F32)
    return jnp.einsum("qkb,bh->hqk", onehot, rel_table[:, head_lo:head_hi], precision=lax.Precision.HIGHEST)


def _hgrn_sum_matrix():
    return np.tril(np.ones((HG_CHUNK, HG_CHUNK), np.float32))


HG_LEVELS = int(math.log2(HG_CHUNK))


def _in_proj_body(x_ref, w_ref, *rest):
    n_res = len(IN_PROJ_DILATIONS)
    a_ref, res_refs, (b_ref, c_ref, slabs) = rest[0], rest[1:1 + n_res], rest[1 + n_res:]
    tm = x_ref.shape[0]
    xb = x_ref[...].astype(BF16)
    a = jnp.dot(xb, w_ref[:, :A_COLS], preferred_element_type=F32)
    a_ref[...] = a.astype(BF16)
    for j in range(A_COLS // LANES):
        slabs[0, j] = a[:, j * LANES:(j + 1) * LANES]
    prev_r = 1
    for level, (r, ref) in enumerate(zip(IN_PROJ_DILATIONS, res_refs)):
        step, n, n_prev = r // prev_r, tm // r, tm // prev_r
        src, dst = slabs.at[level % 2], slabs.at[(level + 1) % 2]
        for p in range(r):
            p_prev, q = p % prev_r, p // prev_r
            for j in range(A_COLS // LANES):
                piece = src[j, pl.ds(p_prev * n_prev + q, n, stride=step), :]
                if level + 1 < len(IN_PROJ_DILATIONS):
                    dst[j, p * n:(p + 1) * n, :] = piece
                ref[0, p, :, j * LANES:(j + 1) * LANES] = piece.astype(BF16)
        prev_r = r
    for j in range(B_COLS // B_KEY_WIDTH):
        lo = A_COLS + j * B_KEY_WIDTH
        b_ref[:, j * B_KEY_WIDTH:(j + 1) * B_KEY_WIDTH] = jnp.dot(
            xb, w_ref[:, lo:lo + B_KEY_WIDTH], preferred_element_type=F32)
    c_ref[...] = jnp.dot(xb, w_ref[:, A_COLS + B_COLS:], preferred_element_type=F32).astype(BF16)


def _in_proj(x2d, w_p, nb, s):
    t, d = x2d.shape
    tm = IN_PROJ_ROWS
    n = w_p.shape[1]
    tiles = s // tm
    res_specs = [pl.BlockSpec((1, r, tm // r, A_COLS), lambda i: (i // tiles, 0, i % tiles, 0))
                 for r in IN_PROJ_DILATIONS]
    res_shapes = [jax.ShapeDtypeStruct((nb, r, s // r, A_COLS), BF16) for r in IN_PROJ_DILATIONS]
    outs = pl.pallas_call(
        _in_proj_body,
        grid=(t // tm,),
        in_specs=[pl.BlockSpec((tm, d), lambda i: (i, 0)),
                  pl.BlockSpec((d, n), lambda i: (0, 0))],
        out_specs=[pl.BlockSpec((tm, A_COLS), lambda i: (i, 0))] + res_specs + [
                   pl.BlockSpec((tm, B_COLS), lambda i: (i, 0)),
                   pl.BlockSpec((tm, C_COLS), lambda i: (i, 0))],
        out_shape=[jax.ShapeDtypeStruct((t, A_COLS), BF16)] + res_shapes + [
                   jax.ShapeDtypeStruct((t, B_COLS), F32),
                   jax.ShapeDtypeStruct((t, C_COLS), BF16)],
        scratch_shapes=[pltpu.VMEM((2, A_COLS // LANES, tm, LANES), F32)],
        compiler_params=_cparams("arbitrary"),
        name="in_proj",
    )(x2d, w_p)
    return outs[0], list(outs[1:-2]), outs[-2], outs[-1]


def _band_attn_body(*refs, width, max_dist, has_sink, want_lse):
    q_ref, kp_ref, kc_ref, vp_ref, vc_ref, bias_ref = refs[:6]
    rest = refs[6:]
    if has_sink:
        sink_ref, rest = rest[0], rest[1:]
    o_ref = rest[0]
    lse_ref = rest[1] if want_lse else None

    first_tile = pl.program_id(2) == 0
    row = lax.broadcasted_iota(I32, (BAND, 2 * BAND), 0)
    col = lax.broadcasted_iota(I32, (BAND, 2 * BAND), 1)
    dist = row + BAND - col
    in_band = (dist >= 0) & (dist <= max_dist)
    first_mask = in_band & ((col >= BAND) | jnp.logical_not(first_tile))
    lane = lax.broadcasted_iota(I32, (BAND, LANES), 1)
    low_half = lane < HEAD_DIM

    for qb in range(q_ref.shape[0] // BAND):
        rows = slice(qb * BAND, (qb + 1) * BAND)
        mask = first_mask if qb == 0 else in_band
        lse_tile = jnp.zeros((BAND, LANES), F32)
        for tile in range(width // LANES):
            sl = slice(tile * LANES, (tile + 1) * LANES)
            q2 = q_ref[rows, sl]
            if qb == 0:
                k2 = jnp.concatenate([kp_ref[:, sl], kc_ref[:BAND, sl]], axis=0)
                v2 = jnp.concatenate([vp_ref[:, sl], vc_ref[:BAND, sl]], axis=0)
            else:
                k2 = kc_ref[(qb - 1) * BAND:(qb + 1) * BAND, sl]
                v2 = vc_ref[(qb - 1) * BAND:(qb + 1) * BAND, sl]
            outs, lses = [], []
            for half in range(2):
                h = 2 * tile + half
                qm = jnp.where(low_half if half == 0 else jnp.logical_not(low_half), q2, jnp.zeros_like(q2))
                s = lax.dot_general(qm, k2, (((1,), (1,)), ((), ())), preferred_element_type=F32)
                s = s + bias_ref[h]
                s = jnp.where(mask, s, MASK_VALUE)
                m = jnp.max(s, axis=-1, keepdims=True)
                if has_sink:
                    sink = sink_ref[h]
                    m = jnp.maximum(m, sink)
                p = jnp.exp(s - m)
                den = jnp.sum(p, axis=-1, keepdims=True)
                if has_sink:
                    den = den + jnp.exp(sink - m)
                pv = jnp.dot(p.astype(BF16), v2, preferred_element_type=F32)
                outs.append(pv / den)
                if want_lse:
                    lse_tile = jnp.where(lane == h, m + jnp.log(den), lse_tile)
            o_ref[rows, sl] = jnp.where(low_half, outs[0], outs[1]).astype(o_ref.dtype)
        if want_lse:
            lse_ref[rows, :] = lse_tile


def _band_attn(src, bias, *, width, max_dist, sinks=None, want_lse):
    nb, r, length, _ = src.shape
    qt = min(BAND_Q_TILE, length)
    bands = qt // BAND
    heads = width // HEAD_DIM
    has_sink = sinks is not None

    def cur(off):
        return pl.BlockSpec((None, None, qt, width), lambda b, p, i: (b, p, i, off))

    def prev(off):
        return pl.BlockSpec((None, None, BAND, width), lambda b, p, i: (b, p, jnp.maximum(i * bands - 1, 0), off))

    in_specs = [cur(0), prev(1), cur(1), prev(2), cur(2),
                pl.BlockSpec((heads, BAND, 2 * BAND), lambda b, p, i: (0, 0, 0))]
    args = [src, src, src, src, src, bias]
    if has_sink:
        in_specs.append(pl.BlockSpec(memory_space=pltpu.SMEM))
        args.append(sinks)
    out_spec = pl.BlockSpec((None, None, qt, width), lambda b, p, i: (b, p, i, 0))
    out_sds = jax.ShapeDtypeStruct((nb, r, length, width), F32 if want_lse else BF16)
    lse_spec = pl.BlockSpec((None, None, qt, LANES), lambda b, p, i: (b, p, i, 0))
    lse_sds = jax.ShapeDtypeStruct((nb, r, length, LANES), F32)
    body = functools.partial(_band_attn_body, width=width, max_dist=max_dist, has_sink=has_sink, want_lse=want_lse)
    return pl.pallas_call(
        body,
        grid=(nb, r, length // qt),
        in_specs=in_specs,
        out_specs=[out_spec, lse_spec] if want_lse else out_spec,
        out_shape=[out_sds, lse_sds] if want_lse else out_sds,
        compiler_params=_cparams("arbitrary", "arbitrary", "arbitrary"),
        name="band_attn_r%d_w%d" % (r, width),
    )(*args)


def _hgrn_body(b_ref, lb_ref, ng_ref, w_ref, o_ref, state_ref):
    c = HG_CHUNK
    kd = B_KEY_DIM

    @pl.when(pl.program_id(1) == 0)
    def _():
        state_ref[...] = jnp.zeros_like(state_ref)

    trow = lax.broadcasted_iota(I32, (c, 1), 0)
    odd_row = (trow % 2) == 1
    low_sub = lax.broadcasted_iota(I32, (SUBLANES, 1), 0) < SUBLANES // 2
    ti = lax.broadcasted_iota(I32, (c, c), 0)
    si = lax.broadcasted_iota(I32, (c, c), 1)
    nt = (((1,), (1,)), ((), ()))
    tri = w_ref[...]

    in_level = [((ti // m) == (si // m)) & ((ti % m) >= (m // 2)) & ((si % m) < (m // 2))
                for m in (c >> lvl for lvl in range(HG_LEVELS))]

    states = [state_ref[h] for h in range(B_HEADS)]
    for chunk, h in [(ci, hi) for ci in range(b_ref.shape[1] // c) for hi in range(B_HEADS)]:
        rows = slice(chunk * c, (chunk + 1) * c)
        ks = slice(h * kd, (h + 1) * kd)
        q = b_ref[0, rows, ks]
        f = b_ref[0, rows, B_KEY_WIDTH + h * kd:B_KEY_WIDTH + (h + 1) * kd]
        inp = b_ref[0, rows, 2 * B_KEY_WIDTH + h * kd:2 * B_KEY_WIDTH + (h + 1) * kd]
        gate = b_ref[0, rows, 3 * B_KEY_WIDTH + h * kd:3 * B_KEY_WIDTH + (h + 1) * kd]
        lb = lb_ref[:, ks]

        a = jnp.exp(-jnp.abs(f))
        big = 1.0 / (1.0 + a)
        small = a * big
        pos = f >= 0.0
        forget = lb + (1.0 - lb) * jnp.where(pos, big, small)
        log_f = jnp.log(forget)
        key = (1.0 - lb) * jnp.where(pos, small, big)
        qs = q * jax.nn.sigmoid(q)

        g_hi = log_f.astype(BF16)
        g_lo = (log_f - g_hi.astype(F32)).astype(BF16)
        cum2 = jnp.dot(tri, jnp.concatenate([g_hi, g_lo], axis=1), preferred_element_type=F32)
        cum = cum2[:, :kd] + cum2[:, kd:]
        last = cum[c - 1:c]

        q_dec = (qs * jnp.exp(cum)).astype(BF16)
        k_dec = (key * jnp.exp(last - cum)).astype(BF16)
        inp_b = inp.astype(BF16)

        def level_decay(m):
            if m == 2:
                return jnp.where(odd_row, forget, 1.0)
            if m == SUBLANES // 2:
                row_bcast = lambda i: jnp.broadcast_to(cum[i:i + 1], (SUBLANES, kd))
                pieces = [jnp.where(low_sub, row_bcast(SUBLANES * j + m // 2 - 1), row_bcast(SUBLANES * j + m + m // 2 - 1))
                          for j in range(c // SUBLANES)]
            else:
                pieces = [jnp.broadcast_to(cum[b * m + m // 2 - 1:b * m + m // 2], (m, kd)) for b in range(c // m)]
            return jnp.exp(-jnp.abs(cum - jnp.concatenate(pieces, axis=0)))

        st = states[h]
        inter = lax.dot_general(q_dec, st.astype(BF16), nt, preferred_element_type=F32)

        scores = jnp.where(ti == si,
                           lax.dot_general(qs.astype(BF16), key.astype(BF16), nt, preferred_element_type=F32), 0.0)
        m = c
        for lvl in range(HG_LEVELS):
            el = level_decay(m)
            sl = lax.dot_general((qs * el).astype(BF16), (key * el).astype(BF16), nt, preferred_element_type=F32)
            scores = jnp.where(in_level[lvl], sl, scores)
            m //= 2
        intra = jnp.dot(scores.astype(BF16), inp_b, preferred_element_type=F32)

        new_st = st * jnp.exp(last) + lax.dot_general(
            inp_b, k_dec, (((0,), (0,)), ((), ())), preferred_element_type=F32)
        states[h] = new_st

        o = inter + intra
        ms = jnp.sum(o * o, axis=-1, keepdims=True) * (1.0 / B_VAL_DIM)
        o = o * lax.rsqrt(ms + 1e-6)
        o_ref[0, rows, ks] = (o * ng_ref[:, ks] * (gate * jax.nn.sigmoid(gate))).astype(o_ref.dtype)

    for h in range(B_HEADS):
        state_ref[h] = states[h]


def _hgrn(b_all, lower_bound, norm_g_pad, w_sum):
    nb, s, _ = b_all.shape
    c = HG_STEP_ROWS
    return pl.pallas_call(
        _hgrn_body,
        grid=(nb, s // c),
        in_specs=[pl.BlockSpec((1, c, B_COLS), lambda b, i: (b, i, 0)),
                  pl.BlockSpec((1, B_KEY_WIDTH), lambda b, i: (0, 0)),
                  pl.BlockSpec((1, B_PAD_WIDTH), lambda b, i: (0, 0)),
                  pl.BlockSpec(w_sum.shape, lambda b, i: (0, 0))],
        out_specs=pl.BlockSpec((1, c, B_PAD_WIDTH), lambda b, i: (b, i, 0)),
        out_shape=jax.ShapeDtypeStruct((nb, s, B_PAD_WIDTH), BF16),
        scratch_shapes=[pltpu.VMEM((B_HEADS, B_VAL_PAD, B_KEY_DIM), F32)],
        compiler_params=_cparams("arbitrary", "arbitrary"),
        name="hgrn2",
    )(b_all, lower_bound, norm_g_pad, w_sum)


def _layer_norm_rows(z, g, b):
    mu = jnp.mean(z, axis=-1, keepdims=True)
    zc = z - mu
    var = jnp.mean(zc * zc, axis=-1, keepdims=True)
    return zc * lax.rsqrt(var + LN_EPS) * g + b


def _mix_out_body(*refs):
    n_pat = len(A_PATTERNS)
    o_refs, l_refs = refs[:n_pat], refs[n_pat:2 * n_pat]
    ob, oc, x_ref, w_ref, g_ref, b_ref, spread_ref, out_ref = refs[2 * n_pat:2 * n_pat + 8]
    scratch = refs[2 * n_pat + 8:]
    tm = x_ref.shape[0]
    n_slabs = A_WIDTH // LANES

    def token_order(ref, r, scr):
        slabs = ref.shape[-1] // LANES
        if r == 1:
            return [ref[0, 0, :, j * LANES:(j + 1) * LANES] for j in range(slabs)]
        for p in range(r):
            for j in range(slabs):
                scr[j, pl.ds(p, tm // r, stride=r), :] = ref[0, p, :, j * LANES:(j + 1) * LANES]
        return [scr[j] for j in range(slabs)]

    scr_iter = iter(scratch)
    dil = [r for _, r in A_PATTERNS]
    o_slabs = [token_order(ref, r, None if r == 1 else next(scr_iter)) for ref, r in zip(o_refs, dil)]
    ls = [token_order(ref, r, None if r == 1 else next(scr_iter))[0] for ref, r in zip(l_refs, dil)]

    m = functools.reduce(jnp.maximum, ls)
    ws = [jnp.exp(l - m) for l in ls]
    inv = 1.0 / functools.reduce(lambda a, b: a + b, ws)
    spread = spread_ref[...]
    wide = []
    for w in ws:
        wn = w * inv
        hi = wn.astype(BF16)
        lo = (wn - hi.astype(F32)).astype(BF16)
        wide.append(jnp.dot(hi, spread, preferred_element_type=F32) + jnp.dot(lo, spread, preferred_element_type=F32))
    merged = [functools.reduce(lambda a, b: a + b,
                               [w[:, j * LANES:(j + 1) * LANES] * o[j] for w, o in zip(wide, o_slabs)])
              for j in range(n_slabs)]
    cat = jnp.concatenate([m.astype(BF16) for m in merged] + [ob[...], oc[...]], axis=1)
    y = jnp.dot(cat, w_ref[...], preferred_element_type=F32)
    z = DEEPNORM_ALPHA * x_ref[...] + y
    out_ref[...] = _layer_norm_rows(z, g_ref[...], b_ref[...])


def _mix_out(o_list, l_list, ob, oc, x2d, w_out_p, ln_g, ln_b):
    t, d = x2d.shape
    tm = MIX_OUT_ROWS
    tiles = o_list[0].shape[1] * o_list[0].shape[2] // tm
    row = lambda w: pl.BlockSpec((tm, w), lambda i: (i, 0))
    full = lambda a: pl.BlockSpec(a.shape, lambda i: (0, 0))
    res = lambda a: pl.BlockSpec((1, a.shape[1], tm // a.shape[1], a.shape[3]), lambda i: (i // tiles, 0, i % tiles, 0))
    spread = jnp.asarray(np.arange(LANES)[:, None] == np.arange(A_WIDTH)[None, :] // HEAD_DIM, dtype=BF16)
    dilated = [a for a in o_list + l_list if a.shape[1] > 1]
    return pl.pallas_call(
        _mix_out_body,
        grid=(t // tm,),
        in_specs=[res(a) for a in o_list + l_list] + [row(B_PAD_WIDTH), row(C_WIDTH), row(d), full(w_out_p),
                                                      full(ln_g), full(ln_b), full(spread)],
        out_specs=row(d),
        out_shape=jax.ShapeDtypeStruct((t, d), F32),
        scratch_shapes=[pltpu.VMEM((a.shape[3] // LANES, tm, LANES), F32) for a in dilated],
        compiler_params=_cparams("arbitrary"),
        name="mix_out_ln",
    )(*o_list, *l_list, ob, oc, x2d, w_out_p, ln_g, ln_b, spread)


def _router_body(h_ref, rw_ref, bias_ref, tri_ref, e_ref, rank_ref, gate_ref, cnt_ref, hp_ref, carry_ref):
    tn = h_ref.shape[0]
    per_group = N_EXPERTS // N_GROUPS
    neg_inf = -jnp.inf

    @pl.when(pl.program_id(0) == 0)
    def _():
        carry_ref[...] = jnp.zeros_like(carry_ref)

    hp_ref[...] = _pack_bf16_pairs(h_ref[...])

    h = h_ref[...]
    h_hi = h.astype(BF16)
    h_lo = (h - h_hi.astype(F32)).astype(BF16)
    nt = (((1,), (1,)), ((), ()))
    w_hi, w_lo = rw_ref[0], rw_ref[1]
    logits = (lax.dot_general(w_hi, h_hi, nt, preferred_element_type=F32)
              + lax.dot_general(w_hi, h_lo, nt, preferred_element_type=F32)
              + lax.dot_general(w_lo, h_hi, nt, preferred_element_type=F32))
    scores = jax.nn.sigmoid(logits)
    choice = scores + bias_ref[...]

    def first_max(vals, idx, sentinel):
        top = jnp.max(vals, axis=0, keepdims=True)
        return top, jnp.min(jnp.where(vals == top, idx, sentinel), axis=0, keepdims=True)

    li = lax.broadcasted_iota(I32, (per_group, tn), 0).astype(F32)
    group_rows = []
    for g in range(N_GROUPS):
        cg = choice[g * per_group:(g + 1) * per_group]
        m1, first = first_max(cg, li, float(per_group))
        m2 = jnp.max(jnp.where(li == first, neg_inf, cg), axis=0, keepdims=True)
        group_rows.append(m1 + m2)
    group_score = jnp.concatenate(group_rows, axis=0)

    gi = lax.broadcasted_iota(I32, (N_GROUPS, tn), 0).astype(F32)
    group_ok = jnp.zeros((N_GROUPS, tn), F32)
    cur = group_score
    for _ in range(TOPK_GROUPS):
        _, first = first_max(cur, gi, float(N_GROUPS))
        pick = gi == first
        group_ok = jnp.where(pick, 1.0, group_ok)
        cur = jnp.where(pick, neg_inf, cur)

    cur = jnp.concatenate(
        [jnp.where(group_ok[g:g + 1] > 0.0, choice[g * per_group:(g + 1) * per_group], MASK_VALUE)
         for g in range(N_GROUPS)], axis=0)
    ei = lax.broadcasted_iota(I32, (N_EXPERTS, tn), 0).astype(F32)
    chosen = jnp.zeros((N_EXPERTS, tn), F32)
    picks, gates = [], []
    for _ in range(TOP_K):
        _, idx = first_max(cur, ei, float(N_EXPERTS))
        pick = ei == idx
        picks.append(idx)
        gates.append(jnp.sum(jnp.where(pick, scores, 0.0), axis=0, keepdims=True))
        chosen = jnp.where(pick, 1.0, chosen)
        cur = jnp.where(pick, neg_inf, cur)

    gate = jnp.concatenate(gates, axis=0)
    gate_ref[...] = gate / jnp.sum(gate, axis=0, keepdims=True) * ROUTED_SCALE
    e_ref[...] = jnp.concatenate(picks, axis=0).astype(I32)

    before = jnp.dot(chosen.astype(BF16), tri_ref[...], preferred_element_type=F32) + carry_ref[...]
    ranks = [jnp.sum(jnp.where(ei == idx, before, 0.0), axis=0, keepdims=True) for idx in picks]
    rank_ref[...] = jnp.concatenate(ranks, axis=0).astype(I32)
    carry = carry_ref[...] + jnp.sum(chosen, axis=1, keepdims=True)
    carry_ref[...] = carry
    cnt_ref[...] = carry.astype(I32)


def _router(h2d, rw_t, bias_col):
    t, d = h2d.shape
    tn = ROUTER_TOKENS
    tri = jnp.asarray(np.triu(np.ones((tn, tn), np.float32), k=1), dtype=BF16)
    rw_hi = rw_t.astype(BF16)
    rw_split = jnp.stack([rw_hi, (rw_t - rw_hi.astype(F32)).astype(BF16)])
    tok = lambda: pl.BlockSpec((TOP_K, tn), lambda i: (0, i))
    return pl.pallas_call(
        _router_body,
        grid=(t // tn,),
        in_specs=[pl.BlockSpec((tn, d), lambda i: (i, 0)),
                  pl.BlockSpec((2, N_EXPERTS, d), lambda i: (0, 0, 0)),
                  pl.BlockSpec((N_EXPERTS, 1), lambda i: (0, 0)),
                  pl.BlockSpec((tn, tn), lambda i: (0, 0))],
        out_specs=[tok(), tok(), tok(), pl.BlockSpec((N_EXPERTS, 1), lambda i: (0, 0)),
                   pl.BlockSpec((tn, d // 2), lambda i: (i, 0))],
        out_shape=[jax.ShapeDtypeStruct((TOP_K, t), I32),
                   jax.ShapeDtypeStruct((TOP_K, t), I32),
                   jax.ShapeDtypeStruct((TOP_K, t), F32),
                   jax.ShapeDtypeStruct((N_EXPERTS, 1), I32),
                   jax.ShapeDtypeStruct((t, d // 2), U32)],
        scratch_shapes=[pltpu.VMEM((N_EXPERTS, 1), F32)],
        compiler_params=_cparams("arbitrary"),
        name="moe_router",
    )(h2d, rw_split, bias_col, tri)


def _pack_bf16_pairs(x):
    w = x.shape[1] // 2
    hi = lax.bitcast_convert_type(x[:, :w].astype(BF16).astype(F32), U32)
    lo = lax.bitcast_convert_type(x[:, w:].astype(BF16).astype(F32), U32)
    return hi | (lo >> 16)


def _unpack_bf16_pairs(p):
    hi = lax.bitcast_convert_type(p & jnp.uint32(0xFFFF0000), F32)
    lo = lax.bitcast_convert_type(p << 16, F32)
    return hi, lo


def _dest_body(e_ref, rank_ref, offs_ref, dest_ref):
    tn = e_ref.shape[1]
    ei = lax.broadcasted_iota(I32, (N_EXPERTS, tn), 0)
    offs = offs_ref[...]
    rows = [jnp.sum(jnp.where(ei == e_ref[k:k + 1, :], offs, 0.0), axis=0, keepdims=True) for k in range(TOP_K)]
    dest_ref[...] = jnp.concatenate(rows, axis=0).astype(I32) + rank_ref[...]


def _dest_rows(e_t, rank_t, offsets):
    t = e_t.shape[1]
    tn = DEST_TOKENS
    tok = pl.BlockSpec((TOP_K, tn), lambda i: (0, i))
    return pl.pallas_call(
        _dest_body,
        grid=(t // tn,),
        in_specs=[tok, tok, pl.BlockSpec((N_EXPERTS, 1), lambda i: (0, 0))],
        out_specs=tok,
        out_shape=jax.ShapeDtypeStruct((TOP_K, t), I32),
        compiler_params=_cparams("arbitrary"),
        name="moe_dest",
    )(e_t, rank_t, offsets.astype(F32).reshape(N_EXPERTS, 1))


def _expert_body(offs_ref, nblk_ref, cnt_ref, layer_ref, xs_hbm, wg_ref, wu_ref, wd_ref, ys_hbm,
                 xbuf, ybuf, wg_b, wu_b, wd_b, sem_in, sem_out):
    e = pl.program_id(0)
    slots, bm = xbuf.shape[:2]
    ahead = slots - 1
    n = nblk_ref[e]
    first = offs_ref[e] // bm
    total = (offs_ref[N_EXPERTS - 1] // bm) + nblk_ref[N_EXPERTS - 1]

    def rows(g):
        return pl.ds(pl.multiple_of(g * bm, bm), bm)

    def x_copy(g, slot):
        return pltpu.make_async_copy(xs_hbm.at[rows(g)], xbuf.at[slot], sem_in.at[slot])

    def y_copy(g, slot):
        return pltpu.make_async_copy(ybuf.at[slot], ys_hbm.at[rows(g)], sem_out.at[slot])

    @pl.when(e == 0)
    def _():
        for g0 in range(ahead):
            @pl.when(g0 < total)
            def _():
                x_copy(g0, g0).start()

    wg_b[...] = wg_ref[0, 0].astype(BF16)
    wu_b[...] = wu_ref[0, 0].astype(BF16)
    wd_b[...] = wd_ref[0, 0].astype(BF16)

    def block(j, carry):
        g = first + j
        slot = g % slots
        x_copy(g, slot).wait()

        @pl.when(g + ahead < total)
        def _():
            x_copy(g + ahead, (g + ahead) % slots).start()

        @pl.when(g >= slots)
        def _():
            y_copy(g - slots, slot).wait()

        live = lax.broadcasted_iota(I32, (bm, 1), 0) < (cnt_ref[e] - j * bm)
        hi, lo = _unpack_bf16_pairs(jnp.where(live, xbuf[slot], jnp.uint32(0)))
        xb = jnp.concatenate([hi.astype(BF16), lo.astype(BF16)], axis=1)
        gate = jnp.dot(xb, wg_b[...], preferred_element_type=F32)
        up = jnp.dot(xb, wu_b[...], preferred_element_type=F32)
        hidden = (gate * jax.nn.sigmoid(gate) * up).astype(BF16)
        ybuf[slot] = _pack_bf16_pairs(jnp.dot(hidden, wd_b[...], preferred_element_type=F32))
        y_copy(g, slot).start()
        return carry

    lax.fori_loop(0, n, block, 0)

    @pl.when(e == N_EXPERTS - 1)
    def _():
        for back in range(1, slots + 1):
            @pl.when(total >= back)
            def _():
                y_copy(total - back, (total - back) % slots).wait()


def _experts(xs, offsets, n_blk, counts, layer, w_gate, w_up, w_down):
    n_rows, dp = xs.shape
    bm = EXPERT_BLOCK
    d, ff = w_gate.shape[-2:]
    w_spec = lambda a, b: pl.BlockSpec((1, 1, a, b), lambda e, of, nb, ct, ly: (ly[0], e, 0, 0))
    grid_spec = pltpu.PrefetchScalarGridSpec(
        num_scalar_prefetch=4,
        grid=(N_EXPERTS,),
        in_specs=[pl.BlockSpec(memory_space=pl.ANY), w_spec(d, ff), w_spec(d, ff), w_spec(ff, d)],
        out_specs=pl.BlockSpec(memory_space=pl.ANY),
        scratch_shapes=[pltpu.VMEM((EXPERT_SLOTS, bm, dp), U32), pltpu.VMEM((EXPERT_SLOTS, bm, dp), U32),
                        pltpu.VMEM((d, ff), BF16), pltpu.VMEM((d, ff), BF16), pltpu.VMEM((ff, d), BF16),
                        pltpu.SemaphoreType.DMA((EXPERT_SLOTS,)), pltpu.SemaphoreType.DMA((EXPERT_SLOTS,))],
    )
    return pl.pallas_call(
        _expert_body,
        grid_spec=grid_spec,
        out_shape=jax.ShapeDtypeStruct((n_rows, dp), U32),
        compiler_params=_cparams("arbitrary"),
        name="moe_experts",
    )(offsets, n_blk, counts, layer, xs, w_gate, w_up, w_down)


SC_GATHER_ROWS = 64


def _sc_gather_rows(table, idx):
    info = plsc.get_sparse_core_info()
    nc, ns = info.num_cores, info.num_subcores
    workers = nc * ns
    n = idx.shape[0]
    w = table.shape[1]
    ch = SC_GATHER_ROWS
    per_worker = n // workers
    steps = per_worker // ch
    assert per_worker * workers == n and steps * ch == per_worker and steps % 2 == 0
    mesh = plsc.VectorSubcoreMesh(core_axis_name="c", subcore_axis_name="s")

    @functools.partial(
        pl.kernel, mesh=mesh,
        out_type=jax.ShapeDtypeStruct((n, w), table.dtype),
        scratch_types=[pltpu.VMEM((steps, ch), I32), pltpu.VMEM((2, ch, w), table.dtype),
                       pltpu.SemaphoreType.DMA((2,)), pltpu.SemaphoreType.DMA((2,))],
    )
    def gather_kernel(table_hbm, idx_hbm, out_hbm, idx_v, rows_v, gsem, wsem):
        wid = lax.axis_index("s") * nc + lax.axis_index("c")
        base = wid * per_worker
        pltpu.sync_copy(idx_hbm.at[wid], idx_v)

        def gather(i, slot):
            return pltpu.make_async_copy(table_hbm.at[idx_v.at[i]], rows_v.at[slot], gsem.at[slot])

        def write(i, slot):
            off = pl.multiple_of(base + i * ch, ch)
            return pltpu.make_async_copy(rows_v.at[slot], out_hbm.at[pl.ds(off, ch)], wsem.at[slot])

        gather(0, 0).start()

        @pl.loop(0, steps, step=2)
        def _(i):
            gather(i + 1, 1).start()
            gather(i, 0).wait()
            write(i, 0).start()
            write(i, 0).wait()

            @pl.when(i + 2 < steps)
            def _():
                gather(i + 2, 0).start()

            gather(i + 1, 1).wait()
            write(i + 1, 1).start()
            write(i + 1, 1).wait()

    return gather_kernel(table, idx.reshape(workers, steps, ch))


def _sc_scatter_rows(rows, idx, n_out):
    info = plsc.get_sparse_core_info()
    nc, ns = info.num_cores, info.num_subcores
    workers = nc * ns
    kk, t = idx.shape
    w = rows.shape[1]
    ch = SC_GATHER_ROWS
    per_worker = t // workers
    steps = per_worker // ch
    assert per_worker * workers == t and steps * ch == per_worker and steps % 2 == 0
    idx_w = idx.reshape(kk, workers, steps, ch).transpose(1, 0, 2, 3).reshape(workers, kk * steps, ch)
    mesh = plsc.VectorSubcoreMesh(core_axis_name="c", subcore_axis_name="s")

    @functools.partial(
        pl.kernel, mesh=mesh,
        out_type=jax.ShapeDtypeStruct((n_out, w), rows.dtype),
        scratch_types=[pltpu.VMEM((kk * steps, ch), I32), pltpu.VMEM((2, ch, w), rows.dtype),
                       pltpu.SemaphoreType.DMA((2,)), pltpu.SemaphoreType.DMA((2,))],
    )
    def scatter_kernel(rows_hbm, idx_hbm, out_hbm, idx_v, rows_v, rsem, ssem):
        wid = lax.axis_index("s") * nc + lax.axis_index("c")
        base = wid * per_worker
        pltpu.sync_copy(idx_hbm.at[wid], idx_v)

        def read(i, slot):
            off = pl.multiple_of(base + i * ch, ch)
            return pltpu.make_async_copy(rows_hbm.at[pl.ds(off, ch)], rows_v.at[slot], rsem.at[slot])

        def scatter(i, k, slot):
            return pltpu.make_async_copy(rows_v.at[slot], out_hbm.at[idx_v.at[k * steps + i]], ssem.at[slot])

        def scatter_all(i, slot):
            for k in range(kk):
                scatter(i, k, slot).start()
            for k in range(kk):
                scatter(i, k, slot).wait()

        read(0, 0).start()

        @pl.loop(0, steps, step=2)
        def _(i):
            read(i + 1, 1).start()
            read(i, 0).wait()
            scatter_all(i, 0)

            @pl.when(i + 2 < steps)
            def _():
                read(i + 2, 0).start()

            read(i + 1, 1).wait()
            scatter_all(i + 1, 1)

    return scatter_kernel(rows, idx_w)


def _combine_streamed_body(y_ref, gate_ref, h_ref, sg_ref, su_ref, sd_ref, g_ref, b_ref, *rest):
    out_ref = rest[-1]
    half = h_ref.shape[1] // 2
    h = h_ref[...]
    hb = h.astype(BF16)
    sg = jnp.dot(hb, sg_ref[...], preferred_element_type=F32)
    su = jnp.dot(hb, su_ref[...], preferred_element_type=F32)
    shared = jnp.dot((sg * jax.nn.sigmoid(sg) * su).astype(BF16), sd_ref[...], preferred_element_type=F32)
    z = DEEPNORM_ALPHA * h + shared
    z_hi, z_lo = z[:, :half], z[:, half:]
    for k in range(TOP_K):
        y_hi, y_lo = _unpack_bf16_pairs(y_ref[k])
        gate = gate_ref[:, k:k + 1]
        z_hi = z_hi + gate * y_hi
        z_lo = z_lo + gate * y_lo
    out_ref[...] = _layer_norm_rows(jnp.concatenate([z_hi, z_lo], axis=1), g_ref[...], b_ref[...])


def _combine_streamed(y_part, gate_tk, h2d, sh_gate, sh_up, sh_down, ln_g, ln_b, part, partial_out):
    t, d = h2d.shape
    tt = COMBINE_ROWS
    steps = y_part.shape[1] // tt
    first = part * steps
    full = lambda a: pl.BlockSpec(a.shape, lambda i: (0, 0))
    in_specs = [pl.BlockSpec((TOP_K, tt, d // 2), lambda i: (0, i, 0)),
                pl.BlockSpec((tt, TOP_K), lambda i: (first + i, 0)),
                pl.BlockSpec((tt, d), lambda i: (first + i, 0)),
                full(sh_gate), full(sh_up), full(sh_down), full(ln_g), full(ln_b)]
    args = [y_part, gate_tk, h2d, sh_gate, sh_up, sh_down, ln_g, ln_b]
    aliases = {}
    if partial_out is not None:
        in_specs.append(pl.BlockSpec(memory_space=pl.ANY))
        args.append(partial_out)
        aliases = {len(args) - 1: 0}
    return pl.pallas_call(
        _combine_streamed_body,
        grid=(steps,),
        in_specs=in_specs,
        out_specs=pl.BlockSpec((tt, d), lambda i: (first + i, 0)),
        out_shape=jax.ShapeDtypeStruct((t, d), F32),
        input_output_aliases=aliases,
        compiler_params=_cparams("arbitrary"),
        name="moe_combine_streamed_ln",
    )(*args)


def _mixer_sublayer(h2d, nb, s, w_in_p, w_out_p, a_biases, c_bias, lower_bound, norm_g_pad, sinks, w_sum, ln_g, ln_b):
    a_qkv, a_residue, b_all, c_qkv = _in_proj(h2d, w_in_p, nb, s)
    a_by_dilation = {1: a_qkv.reshape(nb, 1, s, A_COLS), **dict(zip(IN_PROJ_DILATIONS, a_residue))}
    o_list, l_list = [], []
    for (window, r), bias in zip(A_PATTERNS, a_biases):
        o, lse = _band_attn(a_by_dilation[r], bias, width=A_WIDTH, max_dist=window // r, want_lse=True)
        o_list.append(o)
        l_list.append(lse)
    oc = _band_attn(c_qkv.reshape(nb, 1, s, C_COLS), c_bias, width=C_WIDTH,
                    max_dist=C_WINDOW - 1, sinks=sinks, want_lse=False).reshape(nb * s, C_WIDTH)
    ob = _hgrn(b_all.reshape(nb, s, B_COLS), lower_bound, norm_g_pad, w_sum).reshape(nb * s, B_PAD_WIDTH)
    return _mix_out(o_list, l_list, ob, oc, h2d, w_out_p, ln_g, ln_b)


def _moe_sublayer(h2d, layer, router_w, router_bias, w_gate, w_up, w_down, sh_gate, sh_up, sh_down, ln_g, ln_b):
    t, d = h2d.shape
    bm = EXPERT_BLOCK
    e_t, rank_t, gate_t, counts, h_packed = _router(h2d, router_w.T.astype(F32),
                                                    router_bias.astype(F32).reshape(N_EXPERTS, 1))

    counts = counts.reshape(N_EXPERTS)
    padded = (counts + bm - 1) // bm * bm
    pad_end = jnp.cumsum(padded).astype(I32)
    offsets = pad_end - padded
    n_blocks = -(-(t * TOP_K + N_EXPERTS * (bm - 1)) // bm)

    dest_t = _dest_rows(e_t, rank_t, offsets)
    xs = _sc_scatter_rows(h_packed, dest_t, n_blocks * bm)
    ys = _experts(xs, offsets, (padded // bm).astype(I32), counts, jnp.full((1,), layer, I32), w_gate, w_up, w_down)
    tp = t // COMBINE_PARTS
    gate_tk = gate_t.T
    shared_w = (sh_gate.astype(BF16), sh_up.astype(BF16), sh_down.astype(BF16))
    out = None
    for part in range(COMBINE_PARTS):
        idx = dest_t[:, part * tp:(part + 1) * tp].reshape(TOP_K * tp)
        y_part = _sc_gather_rows(ys, idx).reshape(TOP_K, tp, d // 2)
        out = _combine_streamed(y_part, gate_tk, h2d, *shared_w, ln_g, ln_b, part, out)
    return out


def kernel(x, w_in, w_out, rel_bias_table, lower_bound_logits, hgrn_norm_g, attn_sinks, ln1_g, ln1_b, router_w, router_bias, expert_w_gate, expert_w_up, expert_w_down, shared_w_gate, shared_w_up, shared_w_down, ln2_g, ln2_b):
    nb, s, d = x.shape
    depth = w_in.shape[0]
    lb_probs = jax.nn.softmax(lower_bound_logits.astype(F32), axis=0)
    lower_bounds = jnp.cumsum(lb_probs, axis=0) - lb_probs[0]
    rel_table = rel_bias_table.astype(F32)
    a_biases = [_band_bias(rel_table, r, 0, A_HEADS) for _, r in A_PATTERNS]
    c_bias = _band_bias(rel_table, 1, A_HEADS, A_HEADS + C_HEADS)
    w_sum = jnp.asarray(_hgrn_sum_matrix(), dtype=BF16)
    row = lambda v: v.astype(F32).reshape(1, -1)

    h = x.astype(F32).reshape(nb * s, d)
    for l in range(depth):
        h = _mixer_sublayer(h, nb, s, _prep_w_in(w_in[l]), _prep_w_out(w_out[l].astype(F32)), a_biases, c_bias,
                            lower_bounds[l].reshape(1, B_KEY_WIDTH), _pad_heads_vec(hgrn_norm_g[l].astype(F32)),
                            attn_sinks[l].astype(F32), w_sum, row(ln1_g[l]), row(ln1_b[l]))
        h = _moe_sublayer(h, l, router_w[l], router_bias[l], expert_w_gate, expert_w_up, expert_w_down,
                          shared_w_gate[l], shared_w_up[l], shared_w_down[l], row(ln2_g[l]), row(ln2_b[l]))
    return h.reshape(nb, s, d).astype(x.dtype)
```
